```python
import math
import jax, jax.numpy as jnp
from jax import lax
import numpy as np

D_MODEL = 2048
BATCH = 8
SEQ = 8192
DEPTH = 2

CHUNK = 64
N_MIXERS = 2
EPS = 1e-6

A_HEADS = 16
A_HEAD_DIM = 128
A_INNER = A_HEADS * A_HEAD_DIM
CONV_K = 4

B_HEADS = 16
B_HEAD_DIM = 128
B_INNER = B_HEADS * B_HEAD_DIM
LEFT_CHUNKS = 8
BAND = (LEFT_CHUNKS + 1) * CHUNK
REL_CLIP = 256

kernel_name = "hybrid_gdn_chunkband_stream"


def rms_norm(x, w):
    xf = x.astype(jnp.float32)
    y = xf * lax.rsqrt(jnp.mean(xf * xf, axis=-1, keepdims=True) + EPS)
    return (y * w.astype(jnp.float32)).astype(x.dtype)


def l2_norm(x):
    xf = x.astype(jnp.float32)
    return xf * lax.rsqrt(jnp.sum(xf * xf, axis=-1, keepdims=True) + EPS)


def causal_depthwise_conv(x, w):
    c = x.shape[-1]
    return lax.conv_general_dilated(
        x, w[:, None, :].astype(x.dtype), window_strides=(1,),
        padding=[(CONV_K - 1, 0)], dimension_numbers=("NWC", "WIO", "NWC"),
        feature_group_count=c)


def gated_deltanet(h, w_in, conv_w, a_log, dt_bias, out_norm_w, w_out):
    bsz, t, _ = h.shape
    nc = t // CHUNK
    proj = h @ w_in.astype(h.dtype)
    qkv = proj[..., :3 * A_INNER]
    z = proj[..., 3 * A_INNER:4 * A_INNER]
    a_in = proj[..., 4 * A_INNER:4 * A_INNER + A_HEADS]
    b_in = proj[..., 4 * A_INNER + A_HEADS:]
    qkv = jax.nn.silu(causal_depthwise_conv(qkv, conv_w))
    q = qkv[..., :A_INNER].reshape(bsz, t, A_HEADS, A_HEAD_DIM)
    k = qkv[..., A_INNER:2 * A_INNER].reshape(bsz, t, A_HEADS, A_HEAD_DIM)
    v = qkv[..., 2 * A_INNER:].reshape(bsz, t, A_HEADS, A_HEAD_DIM).astype(jnp.float32)
    q = l2_norm(q) * (A_HEAD_DIM ** -0.5)
    k = l2_norm(k)
    beta = jax.nn.sigmoid(b_in.astype(jnp.float32))
    g = -jnp.exp(a_log.astype(jnp.float32)) * jax.nn.softplus(
        a_in.astype(jnp.float32) + dt_bias.astype(jnp.float32))

    def to_chunks(u):
        u = u.reshape((bsz, nc, CHUNK) + u.shape[2:])
        return jnp.moveaxis(u, 3, 1)

    q, k, v = to_chunks(q), to_chunks(k), to_chunks(v)
    beta, g = to_chunks(beta), to_chunks(g)
    gc = jnp.cumsum(g, axis=-1)
    tri_incl = jnp.tril(jnp.ones((CHUNK, CHUNK), dtype=bool))
    tri_strict = jnp.tril(jnp.ones((CHUNK, CHUNK), dtype=bool), k=-1)
    diff = gc[..., :, None] - gc[..., None, :]
    decay = jnp.exp(jnp.where(tri_incl, diff, -jnp.inf))

    k_beta = k * beta[..., None]
    v_beta = v * beta[..., None]
    lower = jnp.where(tri_strict, jnp.einsum('bhncd,bhnsd->bhncs', k_beta, k) * decay, 0.0)
    eye = jnp.eye(CHUNK, dtype=jnp.float32)
    rhs = jnp.concatenate([v_beta, k_beta * jnp.exp(gc)[..., None]], axis=-1)
    sol = lax.linalg.triangular_solve(eye + lower, rhs, left_side=True, lower=True,
                                      unit_diagonal=True)
    u = sol[..., :A_HEAD_DIM]
    w = sol[..., A_HEAD_DIM:]
    qk = jnp.einsum('bhncd,bhnsd->bhncs', q, k) * decay
    q_dec = q * jnp.exp(gc)[..., None]
    k_dec = k * jnp.exp(gc[..., -1:] - gc)[..., None]
    g_last = jnp.exp(gc[..., -1])

    def step(state, inp):
        qk_n, q_dec_n, k_dec_n, u_n, w_n, gl_n = inp
        v_new = u_n - jnp.einsum('bhcd,bhdv->bhcv', w_n, state)
        o_n = (jnp.einsum('bhcd,bhdv->bhcv', q_dec_n, state)
               + jnp.einsum('bhcs,bhsv->bhcv', qk_n, v_new))
        state = state * gl_n[..., None, None] + jnp.einsum('bhcd,bhcv->bhdv', k_dec_n, v_new)
        return state, o_n

    xs = tuple(jnp.moveaxis(a, 2, 0) for a in (qk, q_dec, k_dec, u, w, g_last))
    s0 = jnp.zeros((bsz, A_HEADS, A_HEAD_DIM, A_HEAD_DIM), jnp.float32)
    _, o = lax.scan(step, s0, xs)
    o = jnp.transpose(o, (1, 0, 3, 2, 4)).reshape(bsz, t, A_HEADS, A_HEAD_DIM)
    zg = jax.nn.silu(z.astype(jnp.float32)).reshape(bsz, t, A_HEADS, A_HEAD_DIM)
    o = rms_norm(o, out_norm_w) * zg
    return o.reshape(bsz, t, A_INNER).astype(h.dtype) @ w_out.astype(h.dtype)


def chunk_band_attention(h, w_in, q_norm_w, k_norm_w, rel_bias, w_out):
    bsz, t, _ = h.shape
    nc = t // CHUNK
    pad = LEFT_CHUNKS * CHUNK
    proj = h @ w_in.astype(h.dtype)
    q = rms_norm(proj[..., :B_INNER].reshape(bsz, t, B_HEADS, B_HEAD_DIM), q_norm_w)
    k = rms_norm(proj[..., B_INNER:2 * B_INNER].reshape(bsz, t, B_HEADS, B_HEAD_DIM), k_norm_w)
    v = proj[..., 2 * B_INNER:3 * B_INNER].reshape(bsz, t, B_HEADS, B_HEAD_DIM)
    z = proj[..., 3 * B_INNER:]
    k_pad = jnp.pad(k, ((0, 0), (pad, 0), (0, 0), (0, 0)))
    v_pad = jnp.pad(v, ((0, 0), (pad, 0), (0, 0), (0, 0)))
    q_chunks = jnp.moveaxis(q.reshape(bsz, nc, CHUNK, B_HEADS, B_HEAD_DIM), 1, 0)

    r = jnp.arange(CHUNK)
    m = jnp.arange(BAND)
    rel = (pad + r[:, None]) - m[None, :]
    idx = jnp.clip(rel, -REL_CLIP, REL_CLIP) + REL_CLIP
    bias = rel_bias.astype(jnp.float32)[:, idx]
    scale = B_HEAD_DIM ** -0.5

    def one_chunk(args):
        n, q_n = args
        start = n * CHUNK
        k_n = lax.dynamic_slice_in_dim(k_pad, start, BAND, axis=1)
        v_n = lax.dynamic_slice_in_dim(v_pad, start, BAND, axis=1)
        s = jnp.einsum('bchd,bmhd->bhcm', q_n, k_n).astype(jnp.float32) * scale + bias
        valid = (start - pad + m) >= 0
        s = jnp.where(valid[None, None, None, :], s, -jnp.inf)
        p = jax.nn.softmax(s, axis=-1).astype(v_n.dtype)
        return jnp.einsum('bhcm,bmhd->bchd', p, v_n)

    o = lax.map(one_chunk, (jnp.arange(nc, dtype=jnp.int32), q_chunks))
    o = jnp.moveaxis(o, 0, 1).reshape(bsz, t, B_INNER)
    o = (o.astype(jnp.float32) * jax.nn.silu(z.astype(jnp.float32))).astype(h.dtype)
    return o @ w_out.astype(h.dtype)


def _fwd_setup_inputs(seed: int = 0) -> dict:
    key = jax.random.key(seed)
    ks = jax.random.split(key, 16)
    n_a = (DEPTH + 1) // N_MIXERS
    n_b = DEPTH // N_MIXERS
    f32 = jnp.float32
    x = jax.random.normal(ks[0], (BATCH, SEQ, D_MODEL), f32)
    norm_w = 1.0 + 0.02 * jax.random.normal(ks[1], (DEPTH, D_MODEL), f32)
    a_w_in = jax.random.normal(ks[2], (n_a, D_MODEL, 4 * A_INNER + 2 * A_HEADS), f32) * D_MODEL ** -0.5
    a_conv_w = jax.random.normal(ks[3], (n_a, CONV_K, 3 * A_INNER), f32) * CONV_K ** -0.5
    a_a_log = jnp.log(jax.random.uniform(ks[4], (n_a, A_HEADS), f32, 1.0, 16.0))
    dt = jnp.exp(jax.random.uniform(ks[5], (n_a, A_HEADS), f32, math.log(1e-3), math.log(1e-1)))
    a_dt_bias = dt + jnp.log(-jnp.expm1(-dt))
    a_out_norm_w = 1.0 + 0.02 * jax.random.normal(ks[6], (n_a, A_HEAD_DIM), f32)
    a_w_out = jax.random.normal(ks[7], (n_a, A_INNER, D_MODEL), f32) * A_INNER ** -0.5
    b_w_in = jax.random.normal(ks[8], (n_b, D_MODEL, 4 * B_INNER), f32) * D_MODEL ** -0.5
    b_q_norm_w = 1.0 + 0.02 * jax.random.normal(ks[9], (n_b, B_HEAD_DIM), f32)
    b_k_norm_w = 1.0 + 0.02 * jax.random.normal(ks[10], (n_b, B_HEAD_DIM), f32)
    b_rel_bias = 0.5 * jax.random.normal(ks[11], (n_b, B_HEADS, 2 * REL_CLIP + 1), f32)
    b_w_out = jax.random.normal(ks[12], (n_b, B_INNER, D_MODEL), f32) * B_INNER ** -0.5
    return {"x": x, "norm_w": norm_w, "a_w_in": a_w_in, "a_conv_w": a_conv_w,
            "a_a_log": a_a_log, "a_dt_bias": a_dt_bias, "a_out_norm_w": a_out_norm_w,
            "a_w_out": a_w_out, "b_w_in": b_w_in, "b_q_norm_w": b_q_norm_w,
            "b_k_norm_w": b_k_norm_w, "b_rel_bias": b_rel_bias, "b_w_out": b_w_out}


def _fwd_reference(x, norm_w, a_w_in, a_conv_w, a_a_log, a_dt_bias, a_out_norm_w, a_w_out,
              b_w_in, b_q_norm_w, b_k_norm_w, b_rel_bias, b_w_out):
    h = x
    for i in range(DEPTH):
        j = i // N_MIXERS
        hn = rms_norm(h, norm_w[i])
        if i % N_MIXERS == 0:
            y = gated_deltanet(hn, a_w_in[j], a_conv_w[j], a_a_log[j], a_dt_bias[j],
                               a_out_norm_w[j], a_w_out[j])
        else:
            y = chunk_band_attention(hn, b_w_in[j], b_q_norm_w[j], b_k_norm_w[j],
                                     b_rel_bias[j], b_w_out[j])
        h = h + y
    return h


import jax as _jax
import jax.numpy as _jnp

TWIN_FORMAT = 'train_step'
FWD_PARAMS = ['x', 'norm_w', 'a_w_in', 'a_conv_w', 'a_a_log', 'a_dt_bias', 'a_out_norm_w', 'a_w_out', 'b_w_in', 'b_q_norm_w', 'b_k_norm_w', 'b_rel_bias', 'b_w_out']
TWIN_WEIGHTS = ['norm_w', 'a_w_in', 'a_conv_w', 'a_a_log', 'a_dt_bias', 'a_out_norm_w', 'a_w_out', 'b_w_in', 'b_q_norm_w', 'b_k_norm_w', 'b_rel_bias', 'b_w_out']
TWIN_DIFF_INPUT = 'x'
TWIN_INPUTS = ['x', 'norm_w', 'a_w_in', 'a_conv_w', 'a_a_log', 'a_dt_bias', 'a_out_norm_w', 'a_w_out', 'b_w_in', 'b_q_norm_w', 'b_k_norm_w', 'b_rel_bias', 'b_w_out', 'loss_target', 'm_norm_w', 'm_a_w_in', 'm_a_conv_w', 'm_a_a_log', 'm_a_dt_bias', 'm_a_out_norm_w', 'm_a_w_out', 'm_b_w_in', 'm_b_q_norm_w', 'm_b_k_norm_w', 'm_b_rel_bias', 'm_b_w_out', 'v_norm_w', 'v_a_w_in', 'v_a_conv_w', 'v_a_a_log', 'v_a_dt_bias', 'v_a_out_norm_w', 'v_a_w_out', 'v_b_w_in', 'v_b_q_norm_w', 'v_b_k_norm_w', 'v_b_rel_bias', 'v_b_w_out']
TWIN_OUTPUTS = ['loss', 'grad_x', 'grad_norm_w', 'grad_a_w_in', 'grad_a_conv_w', 'grad_a_a_log', 'grad_a_dt_bias', 'grad_a_out_norm_w', 'grad_a_w_out', 'grad_b_w_in', 'grad_b_q_norm_w', 'grad_b_k_norm_w', 'grad_b_rel_bias', 'grad_b_w_out', 'delta_norm_w', 'delta_a_w_in', 'delta_a_conv_w', 'delta_a_a_log', 'delta_a_dt_bias', 'delta_a_out_norm_w', 'delta_a_w_out', 'delta_b_w_in', 'delta_b_q_norm_w', 'delta_b_k_norm_w', 'delta_b_rel_bias', 'delta_b_w_out', 'new_m_norm_w', 'new_m_a_w_in', 'new_m_a_conv_w', 'new_m_a_a_log', 'new_m_a_dt_bias', 'new_m_a_out_norm_w', 'new_m_a_w_out', 'new_m_b_w_in', 'new_m_b_q_norm_w', 'new_m_b_k_norm_w', 'new_m_b_rel_bias', 'new_m_b_w_out', 'new_v_norm_w', 'new_v_a_w_in', 'new_v_a_conv_w', 'new_v_a_a_log', 'new_v_a_dt_bias', 'new_v_a_out_norm_w', 'new_v_a_w_out', 'new_v_b_w_in', 'new_v_b_q_norm_w', 'new_v_b_k_norm_w', 'new_v_b_rel_bias', 'new_v_b_w_out']
TWIN_LEAF_KINDS = {'loss': 'loss', 'grad_x': 'grad_x', 'grad_norm_w': 'grad_w', 'grad_a_w_in': 'grad_w', 'grad_a_conv_w': 'grad_w', 'grad_a_a_log': 'grad_w', 'grad_a_dt_bias': 'grad_w', 'grad_a_out_norm_w': 'grad_w', 'grad_a_w_out': 'grad_w', 'grad_b_w_in': 'grad_w', 'grad_b_q_norm_w': 'grad_w', 'grad_b_k_norm_w': 'grad_w', 'grad_b_rel_bias': 'grad_w', 'grad_b_w_out': 'grad_w', 'delta_norm_w': 'delta_w', 'delta_a_w_in': 'delta_w', 'delta_a_conv_w': 'delta_w', 'delta_a_a_log': 'delta_w', 'delta_a_dt_bias': 'delta_w', 'delta_a_out_norm_w': 'delta_w', 'delta_a_w_out': 'delta_w', 'delta_b_w_in': 'delta_w', 'delta_b_q_norm_w': 'delta_w', 'delta_b_k_norm_w': 'delta_w', 'delta_b_rel_bias': 'delta_w', 'delta_b_w_out': 'delta_w', 'new_m_norm_w': 'new_m', 'new_m_a_w_in': 'new_m', 'new_m_a_conv_w': 'new_m', 'new_m_a_a_log': 'new_m', 'new_m_a_dt_bias': 'new_m', 'new_m_a_out_norm_w': 'new_m', 'new_m_a_w_out': 'new_m', 'new_m_b_w_in': 'new_m', 'new_m_b_q_norm_w': 'new_m', 'new_m_b_k_norm_w': 'new_m', 'new_m_b_rel_bias': 'new_m', 'new_m_b_w_out': 'new_m', 'new_v_norm_w': 'new_v', 'new_v_a_w_in': 'new_v', 'new_v_a_conv_w': 'new_v', 'new_v_a_a_log': 'new_v', 'new_v_a_dt_bias': 'new_v', 'new_v_a_out_norm_w': 'new_v', 'new_v_a_w_out': 'new_v', 'new_v_b_w_in': 'new_v', 'new_v_b_q_norm_w': 'new_v', 'new_v_b_k_norm_w': 'new_v', 'new_v_b_rel_bias': 'new_v', 'new_v_b_w_out': 'new_v'}


def _forward(args):
    return _fwd_reference(*[args[k] for k in FWD_PARAMS])


def _output_shape():
    def fwd():
        inp = _fwd_setup_inputs(0)
        return _fwd_reference(*[inp[k] for k in FWD_PARAMS])
    out = _jax.eval_shape(fwd)
    return out.shape, out.dtype

N_MICROBATCH = 1
ADAM_LR = 0.001
ADAM_B1 = 0.9
ADAM_B2 = 0.999
ADAM_EPS = 1e-08
ADAM_WD = 0.01
ADAM_STEP = 10
PER_EXAMPLE_BATCH_AXIS = {'x': 0, 'loss_target': 0}
SHARED_INPUTS = []
_WEIGHT_DTYPES = {'norm_w': _jnp.float32, 'a_w_in': _jnp.float32, 'a_conv_w': _jnp.float32, 'a_a_log': _jnp.float32, 'a_dt_bias': _jnp.float32, 'a_out_norm_w': _jnp.float32, 'a_w_out': _jnp.float32, 'b_w_in': _jnp.float32, 'b_q_norm_w': _jnp.float32, 'b_k_norm_w': _jnp.float32, 'b_rel_bias': _jnp.float32, 'b_w_out': _jnp.float32}
MOMENT_SCALE = {'norm_w': 9.559374e+00, 'a_w_in': 2.187802e-01, 'a_conv_w': 3.634645e-01, 'a_a_log': 4.362012e+01, 'a_dt_bias': 4.160706e+01, 'a_out_norm_w': 1.782110e+02, 'a_w_out': 5.700665e-01, 'b_w_in': 4.869748e-02, 'b_q_norm_w': 6.246244e-01, 'b_k_norm_w': 6.257527e-01, 'b_rel_bias': 1.650130e-02, 'b_w_out': 6.198495e-02}


def _to_microbatches(a, axis):
    t = _jnp.moveaxis(a, axis, 0)
    t = t.reshape((N_MICROBATCH, t.shape[0] // N_MICROBATCH) + t.shape[1:])
    return _jnp.moveaxis(t, 1, axis + 1)


def setup_inputs(seed: int = 0) -> dict:
    inp = _fwd_setup_inputs(seed)
    key = _jax.random.fold_in(_jax.random.key(seed), 7919)
    shape, _ = _output_shape()
    out = dict(inp)
    out["loss_target"] = _jax.random.normal(_jax.random.fold_in(key, 0), shape, _jnp.float32)
    for i, name in enumerate(TWIN_WEIGHTS):
        w = inp[name].astype(_jnp.float32)
        if MOMENT_SCALE is None:
            s = _jnp.sqrt(_jnp.mean(_jnp.square(w)) + 1e-30)
        else:
            s = MOMENT_SCALE[name]
        km, kv = _jax.random.split(_jax.random.fold_in(key, i + 1))
        out[name] = w
        out["m_" + name] = s * _jax.random.normal(km, w.shape, _jnp.float32)
        out["v_" + name] = (s * s) * _jax.random.uniform(kv, w.shape, _jnp.float32, 0.5, 1.5)
    if N_MICROBATCH > 1:
        for name, axis in PER_EXAMPLE_BATCH_AXIS.items():
            out[name] = _to_microbatches(out[name], axis)
    return {'x': out['x'], 'norm_w': out['norm_w'], 'a_w_in': out['a_w_in'], 'a_conv_w': out['a_conv_w'], 'a_a_log': out['a_a_log'], 'a_dt_bias': out['a_dt_bias'], 'a_out_norm_w': out['a_out_norm_w'], 'a_w_out': out['a_w_out'], 'b_w_in': out['b_w_in'], 'b_q_norm_w': out['b_q_norm_w'], 'b_k_norm_w': out['b_k_norm_w'], 'b_rel_bias': out['b_rel_bias'], 'b_w_out': out['b_w_out'], 'loss_target': out['loss_target'], 'm_norm_w': out['m_norm_w'], 'm_a_w_in': out['m_a_w_in'], 'm_a_conv_w': out['m_a_conv_w'], 'm_a_a_log': out['m_a_a_log'], 'm_a_dt_bias': out['m_a_dt_bias'], 'm_a_out_norm_w': out['m_a_out_norm_w'], 'm_a_w_out': out['m_a_w_out'], 'm_b_w_in': out['m_b_w_in'], 'm_b_q_norm_w': out['m_b_q_norm_w'], 'm_b_k_norm_w': out['m_b_k_norm_w'], 'm_b_rel_bias': out['m_b_rel_bias'], 'm_b_w_out': out['m_b_w_out'], 'v_norm_w': out['v_norm_w'], 'v_a_w_in': out['v_a_w_in'], 'v_a_conv_w': out['v_a_conv_w'], 'v_a_a_log': out['v_a_a_log'], 'v_a_dt_bias': out['v_a_dt_bias'], 'v_a_out_norm_w': out['v_a_out_norm_w'], 'v_a_w_out': out['v_a_w_out'], 'v_b_w_in': out['v_b_w_in'], 'v_b_q_norm_w': out['v_b_q_norm_w'], 'v_b_k_norm_w': out['v_b_k_norm_w'], 'v_b_rel_bias': out['v_b_rel_bias'], 'v_b_w_out': out['v_b_w_out']}


def _loss(weights, diff, rest, loss_target):
    with _jax.named_scope("forward"):
        args = {**rest, TWIN_DIFF_INPUT: diff, **{k: w.astype(_WEIGHT_DTYPES[k]) for k, w in weights.items()}}
        y = _forward(args)
    with _jax.named_scope("loss_head"):
        err = _jnp.square(y.astype(_jnp.float32) - loss_target)
        return 0.5 * _jnp.sum(_jnp.mean(err, axis=-1)) if err.ndim else 0.5 * err


def _adamw(w, g, m, v):
    m = ADAM_B1 * m + (1.0 - ADAM_B1) * g
    v = ADAM_B2 * v + (1.0 - ADAM_B2) * _jnp.square(g)
    m_hat = m / (1.0 - ADAM_B1 ** ADAM_STEP)
    v_hat = v / (1.0 - ADAM_B2 ** ADAM_STEP)
    delta = -ADAM_LR * (m_hat / (_jnp.sqrt(v_hat) + ADAM_EPS) + ADAM_WD * w)
    return delta, m, v


def reference(x, norm_w, a_w_in, a_conv_w, a_a_log, a_dt_bias, a_out_norm_w, a_w_out, b_w_in, b_q_norm_w, b_k_norm_w, b_rel_bias, b_w_out, loss_target, m_norm_w, m_a_w_in, m_a_conv_w, m_a_a_log, m_a_dt_bias, m_a_out_norm_w, m_a_w_out, m_b_w_in, m_b_q_norm_w, m_b_k_norm_w, m_b_rel_bias, m_b_w_out, v_norm_w, v_a_w_in, v_a_conv_w, v_a_a_log, v_a_dt_bias, v_a_out_norm_w, v_a_w_out, v_b_w_in, v_b_q_norm_w, v_b_k_norm_w, v_b_rel_bias, v_b_w_out):
    given = dict(x=x, norm_w=norm_w, a_w_in=a_w_in, a_conv_w=a_conv_w, a_a_log=a_a_log, a_dt_bias=a_dt_bias, a_out_norm_w=a_out_norm_w, a_w_out=a_w_out, b_w_in=b_w_in, b_q_norm_w=b_q_norm_w, b_k_norm_w=b_k_norm_w, b_rel_bias=b_rel_bias, b_w_out=b_w_out, loss_target=loss_target, m_norm_w=m_norm_w, m_a_w_in=m_a_w_in, m_a_conv_w=m_a_conv_w, m_a_a_log=m_a_a_log, m_a_dt_bias=m_a_dt_bias, m_a_out_norm_w=m_a_out_norm_w, m_a_w_out=m_a_w_out, m_b_w_in=m_b_w_in, m_b_q_norm_w=m_b_q_norm_w, m_b_k_norm_w=m_b_k_norm_w, m_b_rel_bias=m_b_rel_bias, m_b_w_out=m_b_w_out, v_norm_w=v_norm_w, v_a_w_in=v_a_w_in, v_a_conv_w=v_a_conv_w, v_a_a_log=v_a_a_log, v_a_dt_bias=v_a_dt_bias, v_a_out_norm_w=v_a_out_norm_w, v_a_w_out=v_a_w_out, v_b_w_in=v_b_w_in, v_b_q_norm_w=v_b_q_norm_w, v_b_k_norm_w=v_b_k_norm_w, v_b_rel_bias=v_b_rel_bias, v_b_w_out=v_b_w_out)
    weights = {n: given[n] for n in TWIN_WEIGHTS}
    shared = {n: given[n] for n in SHARED_INPUTS}
    per_example = {n: given[n] for n in ['x']}
    grad_fn = _jax.value_and_grad(_loss, argnums=(0, 1))

    def one_microbatch(ex, loss_target):
        ex = dict(ex)
        diff = ex.pop(TWIN_DIFF_INPUT)
        return grad_fn(weights, diff, {**shared, **ex}, loss_target)

    if N_MICROBATCH == 1:
        loss, (grad_w, grad_x) = one_microbatch(per_example, given["loss_target"])
    else:
        def body(carry, xs):
            loss_sum, grad_sum = carry
            l_k, (gw_k, gx_k) = one_microbatch(xs[0], xs[1])
            with _jax.named_scope("update"):
                return (loss_sum + l_k, _jax.tree.map(_jnp.add, grad_sum, gw_k)), gx_k

        init = (_jnp.zeros((), _jnp.float32), _jax.tree.map(_jnp.zeros_like, weights))
        (loss, grad_w), grad_x = _jax.lax.scan(body, init, (per_example, given["loss_target"]))
    with _jax.named_scope("update"):
        delta_w, new_m, new_v = {}, {}, {}
        for n in TWIN_WEIGHTS:
            delta_w[n], new_m[n], new_v[n] = _adamw(weights[n], grad_w[n], given["m_" + n], given["v_" + n])
    return (loss, grad_x, *[grad_w[n] for n in TWIN_WEIGHTS], *[delta_w[n] for n in TWIN_WEIGHTS],
            *[new_m[n] for n in TWIN_WEIGHTS], *[new_v[n] for n in TWIN_WEIGHTS])
```

```python
import functools
import math

import jax
import jax.numpy as jnp
from jax import lax
from jax.experimental import pallas as pl
from jax.experimental.pallas import tpu as pltpu

F32 = jnp.float32
BF16 = jnp.bfloat16
MESH_IDS = pl.DeviceIdType.MESH
N_DEV = 8
CHUNK = 64
HEAD_DIM = 128
EPS = 1e-6
CONV_K = 4
LEFT_CHUNKS = 8
REL_CLIP = 256
Q_TILE = LEFT_CHUNKS * CHUNK
ADAM_LR = 0.001
ADAM_B1 = 0.9
ADAM_B2 = 0.999
ADAM_EPS = 1e-08
ADAM_WD = 0.01
ADAM_STEP = 10
NEG_BIG = -1e30
VMEM_LIMIT_BYTES = 56 * 1024 * 1024
HIGHEST = lax.Precision.HIGHEST
ANY = pl.BlockSpec(memory_space=pl.ANY)


def _cparams(*sem):
    return pltpu.CompilerParams(dimension_semantics=tuple(sem), vmem_limit_bytes=VMEM_LIMIT_BYTES)


def _dot(a, b, dims, precision=None):
    return lax.dot_general(a, b, (dims, ((), ())), preferred_element_type=F32, precision=precision)


def _nn(a, b, precision=None):
    return _dot(a, b, ((1,), (0,)), precision)


def _nt(a, b, precision=None):
    return _dot(a, b, ((1,), (1,)), precision)


def _tn(a, b, precision=None):
    return _dot(a, b, ((0,), (0,)), precision)


def _bf(x):
    return x.astype(BF16)


def _sigmoid(x):
    return 1.0 / (1.0 + jnp.exp(-x))


def _silu(x):
    return x * _sigmoid(x)


def _dsilu(x):
    s = _sigmoid(x)
    return s * (1.0 + x * (1.0 - s))


def _my_pos():
    return lax.axis_index("x"), lax.axis_index("y"), lax.axis_index("c")


def _peers(x, y, c):
    def flip(v, f):
        return 1 - v if f else v

    return [(flip(x, kx), flip(y, ky), flip(c, kc)) for kx in (0, 1) for ky in (0, 1) for kc in (0, 1)][1:]


def _lin(p):
    return 4 * p[0] + 2 * p[1] + p[2]


def _all_gather(shards, name):
    n = len(shards)

    def body(*refs):
        ins, outs = refs[:n], refs[n:2 * n]
        send_sems, recv_sems, local_sems = refs[2 * n:]
        x, y, c = _my_pos()
        me = _lin((x, y, c))
        local = [pltpu.make_async_copy(ins[t], outs[t].at[me], local_sems.at[t]) for t in range(n)]
        for cp in local:
            cp.start()
        copies = []
        for k, peer in enumerate(_peers(x, y, c)):
            for t in range(n):
                copies.append(pltpu.make_async_remote_copy(
                    src_ref=ins[t], dst_ref=outs[t].at[me],
                    send_sem=send_sems.at[k * n + t], recv_sem=recv_sems.at[k * n + t],
                    device_id=peer, device_id_type=MESH_IDS))
        for cp in copies:
            cp.start()
        for k, peer in enumerate(_peers(x, y, c)):
            for t in range(n):
                pltpu.make_async_remote_copy(
                    src_ref=ins[t], dst_ref=outs[t].at[_lin(peer)],
                    send_sem=send_sems.at[k * n + t], recv_sem=recv_sems.at[k * n + t],
                    device_id=peer, device_id_type=MESH_IDS).wait_recv()
        for cp in copies:
            cp.wait_send()
        for cp in local:
            cp.wait()

    return pl.pallas_call(
        body, name=name,
        out_shape=[jax.ShapeDtypeStruct((N_DEV,) + s.shape, s.dtype) for s in shards],
        in_specs=[ANY] * n, out_specs=[ANY] * n,
        scratch_shapes=[pltpu.SemaphoreType.DMA((7 * n,)), pltpu.SemaphoreType.DMA((7 * n,)),
                        pltpu.SemaphoreType.DMA((n,))],
    )(*shards)


def _slab_exchange(slabs, name):
    n = len(slabs)

    def body(*refs):
        ins, outs = refs[:n], refs[n:2 * n]
        send_sems, recv_sems, local_sems = refs[2 * n:]
        x, y, c = _my_pos()
        me = _lin((x, y, c))
        local = [pltpu.make_async_copy(ins[t].at[me], outs[t].at[me], local_sems.at[t]) for t in range(n)]
        for cp in local:
            cp.start()
        copies = []
        for k, peer in enumerate(_peers(x, y, c)):
            for t in range(n):
                copies.append(pltpu.make_async_remote_copy(
                    src_ref=ins[t].at[_lin(peer)], dst_ref=outs[t].at[me],
                    send_sem=send_sems.at[k * n + t], recv_sem=recv_sems.at[k * n + t],
                    device_id=peer, device_id_type=MESH_IDS))
        for cp in copies:
            cp.start()
        for k, peer in enumerate(_peers(x, y, c)):
            for t in range(n):
                pltpu.make_async_remote_copy(
                    src_ref=ins[t].at[me], dst_ref=outs[t].at[_lin(peer)],
                    send_sem=send_sems.at[k * n + t], recv_sem=recv_sems.at[k * n + t],
                    device_id=peer, device_id_type=MESH_IDS).wait_recv()
        for cp in copies:
            cp.wait_send()
        for cp in local:
            cp.wait()

    return pl.pallas_call(
        body, name=name,
        out_shape=[jax.ShapeDtypeStruct(s.shape, s.dtype) for s in slabs],
        in_specs=[ANY] * n, out_specs=[ANY] * n,
        scratch_shapes=[pltpu.SemaphoreType.DMA((7 * n,)), pltpu.SemaphoreType.DMA((7 * n,)),
                        pltpu.SemaphoreType.DMA((n,))],
    )(*slabs)


def _mm(a, b, mode, m, n, k, *, out_dtype, name, tm=1024, tn=1024, tk=2048,
        a_m0=0, a_k0=0, b_n0=0, b_k0=0, res=None):
    tm, tn, tk = min(tm, m), min(tn, n), min(tk, k)
    nm, nn, nk = m // tm, n // tn, k // tk
    assert nm * tm == m and nn * tn == n and nk * tk == k
    am, ak, bn, bk = a_m0 // tm, a_k0 // tk, b_n0 // tn, b_k0 // tk
    assert am * tm == a_m0 and ak * tk == a_k0 and bn * tn == b_n0 and bk * tk == b_k0
    if mode == "tn":
        a_spec = pl.BlockSpec((tk, tm), lambda i, j, q: (q + ak, i + am))
        a_dims = (0,)
    else:
        a_spec = pl.BlockSpec((tm, tk), lambda i, j, q: (i + am, q + ak))
        a_dims = (1,)
    if mode == "nt":
        b_spec = pl.BlockSpec((tn, tk), lambda i, j, q: (j + bn, q + bk))
        b_dims = (1,)
    else:
        b_spec = pl.BlockSpec((tk, tn), lambda i, j, q: (q + bk, j + bn))
        b_dims = (0,)
    o_spec = pl.BlockSpec((tm, tn), lambda i, j, q: (i, j))
    has_res = res is not None

    def body(*refs):
        a_ref, b_ref = refs[0], refs[1]
        res_ref = refs[2] if has_res else None
        o_ref = refs[2 + has_res]
        p = _dot(a_ref[...], b_ref[...], (a_dims, b_dims))

        def finish(total):
            if has_res:
                total = total + res_ref[...].astype(F32)
            o_ref[...] = total.astype(out_dtype)

        if nk == 1:
            finish(p)
        else:
            acc_ref = refs[3 + has_res]
            q = pl.program_id(2)

            @pl.when(q == 0)
            def _():
                acc_ref[...] = p

            @pl.when(q > 0)
            def _():
                acc_ref[...] += p

            @pl.when(q == nk - 1)
            def _():
                finish(acc_ref[...])

    return pl.pallas_call(
        body, name=name, grid=(nm, nn, nk),
        in_specs=[a_spec, b_spec] + ([o_spec] if has_res else []),
        out_specs=o_spec, out_shape=jax.ShapeDtypeStruct((m, n), out_dtype),
        scratch_shapes=[pltpu.VMEM((tm, tn), F32)] if nk > 1 else [],
        compiler_params=_cparams("parallel", "parallel", "arbitrary"),
    )(*([a, b] + ([res] if has_res else [])))


def _rms_fwd(x, w, name, tr=512):
    t, d = x.shape
    tr = min(tr, t)

    def body(x_ref, w_ref, o_ref):
        xv = x_ref[...]
        r = lax.rsqrt(jnp.mean(xv * xv, axis=-1, keepdims=True) + EPS)
        o_ref[...] = (xv * r * w_ref[...]).astype(BF16)

    return pl.pallas_call(
        body, name=name, grid=(t // tr,),
        in_specs=[pl.BlockSpec((tr, d), lambda i: (i, 0)), pl.BlockSpec((1, d), lambda i: (0, 0))],
        out_specs=pl.BlockSpec((tr, d), lambda i: (i, 0)),
        out_shape=jax.ShapeDtypeStruct((t, d), BF16),
        compiler_params=_cparams("parallel"),
    )(x, w)


def _rms_bwd(x, w, dy, dres, name, tr=256):
    t, d = x.shape
    tr = min(tr, t)

    def body(x_ref, w_ref, dy_ref, dres_ref, dx_ref, dxb_ref, dw_ref):
        xv = x_ref[...]
        dyv = dy_ref[...].astype(F32)
        r = lax.rsqrt(jnp.mean(xv * xv, axis=-1, keepdims=True) + EPS)
        gy = dyv * w_ref[...]
        proj = jnp.sum(gy * xv, axis=-1, keepdims=True) * (1.0 / d)
        dx = dres_ref[...] + r * gy - xv * (r * r * r) * proj
        dx_ref[...] = dx
        dxb_ref[...] = dx.astype(BF16)
        part = jnp.sum(dyv * xv * r, axis=0, keepdims=True)

        @pl.when(pl.program_id(0) == 0)
        def _():
            dw_ref[...] = part

        @pl.when(pl.program_id(0) > 0)
        def _():
            dw_ref[...] += part

    row = pl.BlockSpec((tr, d), lambda i: (i, 0))
    vec = pl.BlockSpec((1, d), lambda i: (0, 0))
    return pl.pallas_call(
        body, name=name, grid=(t // tr,),
        in_specs=[row, vec, row, row], out_specs=[row, row, vec],
        out_shape=[jax.ShapeDtypeStruct((t, d), F32), jax.ShapeDtypeStruct((t, d), BF16),
                   jax.ShapeDtypeStruct((1, d), F32)],
        compiler_params=_cparams("arbitrary"),
    )(x, w, dy, dres)


def _adamw(parts, w, m, v, name, tr=128):
    r, c = w.shape
    tr = tr if r % tr == 0 else r
    c1 = 1.0 - ADAM_B1 ** ADAM_STEP
    c2 = 1.0 - ADAM_B2 ** ADAM_STEP

    def body(p_ref, w_ref, m_ref, v_ref, g_ref, d_ref, nm_ref, nv_ref):
        g = p_ref[0]
        for s in range(1, N_DEV):
            g = g + p_ref[s]
        nm = ADAM_B1 * m_ref[...] + (1.0 - ADAM_B1) * g
        nv = ADAM_B2 * v_ref[...] + (1.0 - ADAM_B2) * (g * g)
        m_hat = nm / c1
        v_hat = nv / c2
        g_ref[...] = g
        d_ref[...] = -ADAM_LR * (m_hat / (jnp.sqrt(v_hat) + ADAM_EPS) + ADAM_WD * w_ref[...])
        nm_ref[...] = nm
        nv_ref[...] = nv

    blk = pl.BlockSpec((tr, c), lambda i: (i, 0))
    return pl.pallas_call(
        body, name=name, grid=(r // tr,),
        in_specs=[pl.BlockSpec((N_DEV, tr, c), lambda i: (0, i, 0)), blk, blk, blk],
        out_specs=[blk] * 4, out_shape=[jax.ShapeDtypeStruct((r, c), F32)] * 4,
        compiler_params=_cparams("parallel"),
    )(parts, w, m, v)


def _heads_of(x, nh):
    return [x[:, h * HEAD_DIM:(h + 1) * HEAD_DIM] for h in range(nh)]


def _headnorm_fwd(proj, w, col0, inner, name, tr=512, hb=4):
    t = proj.shape[0]
    tr = min(tr, t)
    wc = hb * HEAD_DIM
    c0 = col0 // wc

    def body(x_ref, w_ref, o_ref):
        outs = []
        for xh in _heads_of(x_ref[...], hb):
            r = lax.rsqrt(jnp.mean(xh * xh, axis=-1, keepdims=True) + EPS)
            outs.append((xh * r * w_ref[...]).astype(BF16))
        o_ref[...] = jnp.concatenate(outs, axis=1)

    return pl.pallas_call(
        body, name=name, grid=(t // tr, inner // wc),
        in_specs=[pl.BlockSpec((tr, wc), lambda i, j: (i, j + c0)), pl.BlockSpec((1, HEAD_DIM), lambda i, j: (0, 0))],
        out_specs=pl.BlockSpec((tr, wc), lambda i, j: (i, j)),
        out_shape=jax.ShapeDtypeStruct((t, inner), BF16),
        compiler_params=_cparams("parallel", "parallel"),
    )(proj, w)


def _headnorm_bwd(dy, proj, w, col0, inner, name, tr=512, hb=4):
    t = proj.shape[0]
    tr = min(tr, t)
    wc = hb * HEAD_DIM
    c0 = col0 // wc

    def body(dy_ref, x_ref, w_ref, dx_ref, dw_ref):
        outs = []
        part = jnp.zeros((1, HEAD_DIM), F32)
        for dyh, xh in zip(_heads_of(dy_ref[...], hb), _heads_of(x_ref[...], hb)):
            r = lax.rsqrt(jnp.mean(xh * xh, axis=-1, keepdims=True) + EPS)
            gy = dyh * w_ref[...]
            pr = jnp.sum(gy * xh, axis=-1, keepdims=True) * (1.0 / HEAD_DIM)
            outs.append((r * gy - xh * (r * r * r) * pr).astype(BF16))
            part = part + jnp.sum(dyh * xh * r, axis=0, keepdims=True)
        dx_ref[...] = jnp.concatenate(outs, axis=1)
        first = (pl.program_id(0) == 0) & (pl.program_id(1) == 0)

        @pl.when(first)
        def _():
            dw_ref[...] = part

        @pl.when(jnp.logical_not(first))
        def _():
            dw_ref[...] += part

    blk = pl.BlockSpec((tr, wc), lambda i, j: (i, j))
    return pl.pallas_call(
        body, name=name, grid=(t // tr, inner // wc),
        in_specs=[blk, pl.BlockSpec((tr, wc), lambda i, j: (i, j + c0)), pl.BlockSpec((1, HEAD_DIM), lambda i, j: (0, 0))],
        out_specs=[blk, pl.BlockSpec((1, HEAD_DIM), lambda i, j: (0, 0))],
        out_shape=[jax.ShapeDtypeStruct((t, inner), BF16), jax.ShapeDtypeStruct((1, HEAD_DIM), F32)],
        compiler_params=_cparams("arbitrary", "arbitrary"),
    )(dy, proj, w)


def _gate_fwd(o, proj, zcol0, inner, name, norm_w=None, tr=512, hb=4):
    t = o.shape[0]
    tr = min(tr, t)
    wc = hb * HEAD_DIM
    c0 = zcol0 // wc
    has_w = norm_w is not None

    def body(*refs):
        o_ref, z_ref = refs[0], refs[1]
        out_ref = refs[2 + has_w]
        outs = []
        for oh, zh in zip(_heads_of(o_ref[...], hb), _heads_of(z_ref[...], hb)):
            if has_w:
                r = lax.rsqrt(jnp.mean(oh * oh, axis=-1, keepdims=True) + EPS)
                oh = oh * r * refs[2][...]
            outs.append((oh * _silu(zh)).astype(BF16))
        out_ref[...] = jnp.concatenate(outs, axis=1)

    blk = pl.BlockSpec((tr, wc), lambda i, j: (i, j))
    vec = pl.BlockSpec((1, HEAD_DIM), lambda i, j: (0, 0))
    return pl.pallas_call(
        body, name=name, grid=(t // tr, inner // wc),
        in_specs=[blk, pl.BlockSpec((tr, wc), lambda i, j: (i, j + c0))] + ([vec] if has_w else []),
        out_specs=blk, out_shape=jax.ShapeDtypeStruct((t, inner), BF16),
        compiler_params=_cparams("parallel", "parallel"),
    )(*([o, proj] + ([norm_w] if has_w else [])))


def _gate_bwd(dg, o, proj, zcol0, inner, name, do_dtype, norm_w=None, tr=512, hb=4):
    t = o.shape[0]
    tr = min(tr, t)
    wc = hb * HEAD_DIM
    c0 = zcol0 // wc
    has_w = norm_w is not None

    def body(*refs):
        dg_ref, o_ref, z_ref = refs[0], refs[1], refs[2]
        do_ref, dz_ref = refs[3 + has_w], refs[4 + has_w]
        dos, dzs = [], []
        part = jnp.zeros((1, HEAD_DIM), F32)
        for dgh, oh, zh in zip(_heads_of(dg_ref[...], hb), _heads_of(o_ref[...], hb), _heads_of(z_ref[...], hb)):
            dy = dgh * _silu(zh)
            if has_w:
                w = refs[3][...]
                r = lax.rsqrt(jnp.mean(oh * oh, axis=-1, keepdims=True) + EPS)
                on = oh * r
                dzs.append((dgh * on * w * _dsilu(zh)).astype(BF16))
                gy = dy * w
                pr = jnp.sum(gy * oh, axis=-1, keepdims=True) * (1.0 / HEAD_DIM)
                dos.append((r * gy - oh * (r * r * r) * pr).astype(do_dtype))
                part = part + jnp.sum(dy * on, axis=0, keepdims=True)
            else:
                dzs.append((dgh * oh * _dsilu(zh)).astype(BF16))
                dos.append(dy.astype(do_dtype))
        do_ref[...] = jnp.concatenate(dos, axis=1)
        dz_ref[...] = jnp.concatenate(dzs, axis=1)
        if has_w:
            dw_ref = refs[6]
            first = (pl.program_id(0) == 0) & (pl.program_id(1) == 0)

            @pl.when(first)
            def _():
                dw_ref[...] = part

            @pl.when(jnp.logical_not(first))
            def _():
                dw_ref[...] += part

    blk = pl.BlockSpec((tr, wc), lambda i, j: (i, j))
    vec = pl.BlockSpec((1, HEAD_DIM), lambda i, j: (0, 0))
    return pl.pallas_call(
        body, name=name, grid=(t // tr, inner // wc),
        in_specs=[blk, blk, pl.BlockSpec((tr, wc), lambda i, j: (i, j + c0))] + ([vec] if has_w else []),
        out_specs=[blk, blk] + ([vec] if has_w else []),
        out_shape=[jax.ShapeDtypeStruct((t, inner), do_dtype), jax.ShapeDtypeStruct((t, inner), BF16)]
        + ([jax.ShapeDtypeStruct((1, HEAD_DIM), F32)] if has_w else []),
        compiler_params=_cparams("arbitrary", "arbitrary"),
    )(*([dg, o, proj] + ([norm_w] if has_w else [])))


def _loss_grad(h, target, name, tr=512):
    t, d = h.shape
    tr = min(tr, t)

    def body(h_ref, t_ref, g_ref, gb_ref, l_ref):
        e = h_ref[...] - t_ref[...]
        g = e * (1.0 / d)
        g_ref[...] = g
        gb_ref[...] = g.astype(BF16)
        part = jnp.zeros((1, HEAD_DIM), F32) + 0.5 * jnp.sum(jnp.sum(e * e, axis=-1, keepdims=True) * (1.0 / d))

        @pl.when(pl.program_id(0) == 0)
        def _():
            l_ref[...] = part

        @pl.when(pl.program_id(0) > 0)
        def _():
            l_ref[...] += part

    row = pl.BlockSpec((tr, d), lambda i: (i, 0))
    return pl.pallas_call(
        body, name=name, grid=(t // tr,),
        in_specs=[row, row], out_specs=[row, row, pl.BlockSpec((1, HEAD_DIM), lambda i: (0, 0))],
        out_shape=[jax.ShapeDtypeStruct((t, d), F32), jax.ShapeDtypeStruct((t, d), BF16),
                   jax.ShapeDtypeStruct((1, HEAD_DIM), F32)],
        compiler_params=_cparams("arbitrary"),
    )(h, target)


N_REL = 2 * REL_CLIP + 1
REL_PAD = 640
WIN = 2 * Q_TILE


def _diag_onehot():
    i = lax.broadcasted_iota(jnp.int32, (REL_PAD, WIN), 0)
    j = lax.broadcasted_iota(jnp.int32, (REL_PAD, WIN), 1)
    rel = jnp.where(j < Q_TILE + CHUNK, Q_TILE - j, Q_TILE + WIN - j)
    used = (j < Q_TILE + CHUNK) | (j > WIN - CHUNK)
    idx = jnp.clip(rel, -REL_CLIP, REL_CLIP) + REL_CLIP
    return jnp.where(used & (i == idx), 1.0, 0.0).astype(F32)


def _band_mask():
    r = lax.broadcasted_iota(jnp.int32, (Q_TILE, WIN), 0) // CHUNK
    kc = lax.broadcasted_iota(jnp.int32, (Q_TILE, WIN), 1) // CHUNK - LEFT_CHUNKS
    return (kc <= r) & (kc >= r - LEFT_CHUNKS)


def _bias_tiles(rel_bias_pad, name):
    nh = rel_bias_pad.shape[0]

    def body(rb_ref, o_ref):
        dvec = _nn(rb_ref[...], _diag_onehot(), HIGHEST)[0:1, :]
        tile = pltpu.roll(jnp.broadcast_to(dvec, (Q_TILE, WIN)), 0, 1, stride=1, stride_axis=0)
        o_ref[...] = jnp.where(_band_mask(), tile, NEG_BIG)

    return pl.pallas_call(
        body, name=name, grid=(nh,),
        in_specs=[pl.BlockSpec((None, 8, REL_PAD), lambda h: (h, 0, 0))],
        out_specs=pl.BlockSpec((None, Q_TILE, WIN), lambda h: (h, 0, 0)),
        out_shape=jax.ShapeDtypeStruct((nh, Q_TILE, WIN), F32),
        compiler_params=_cparams("parallel"),
    )(rel_bias_pad)


def _bias_grad(dtile, name):
    nh = dtile.shape[0]

    def body(d_ref, o_ref):
        ri = lax.broadcasted_iota(jnp.int32, (Q_TILE, Q_TILE), 0)
        ci = lax.broadcasted_iota(jnp.int32, (Q_TILE, Q_TILE), 1)
        flip = jnp.where(ri + ci == Q_TILE - 1, 1.0, 0.0).astype(F32)
        rev = _nn(flip, d_ref[...], HIGHEST)
        rolled = pltpu.roll(rev, WIN - (Q_TILE - 1), 1, stride=1, stride_axis=0)
        diag = jnp.broadcast_to(jnp.sum(rolled, axis=0, keepdims=True), (8, WIN))
        o_ref[...] = _nt(diag, _diag_onehot(), HIGHEST)

    return pl.pallas_call(
        body, name=name, grid=(nh,),
        in_specs=[pl.BlockSpec((None, Q_TILE, WIN), lambda h: (h, 0, 0))],
        out_specs=pl.BlockSpec((None, 8, REL_PAD), lambda h: (h, 0, 0)),
        out_shape=jax.ShapeDtypeStruct((nh, 8, REL_PAD), F32),
        compiler_params=_cparams("parallel"),
    )(dtile)


def _attn_probs(q, k0, k1, bias, first_tile):
    s = jnp.concatenate([_nt(q, k0), _nt(q, k1)], axis=1) * (HEAD_DIM ** -0.5) + bias
    col = lax.broadcasted_iota(jnp.int32, (Q_TILE, WIN), 1)
    s = jnp.where(first_tile & (col < Q_TILE), NEG_BIG, s)
    p = jnp.exp(s - jnp.max(s, axis=-1, keepdims=True))
    return p * (1.0 / jnp.sum(p, axis=-1, keepdims=True))


def _attn_fwd(q, k, v, bias, name):
    t, inner = q.shape
    nh, nt = inner // HEAD_DIM, t // Q_TILE

    def body(q_ref, k0_ref, k1_ref, v0_ref, v1_ref, b_ref, o_ref):
        p = _bf(_attn_probs(q_ref[...], k0_ref[...], k1_ref[...], b_ref[...], pl.program_id(1) == 0))
        o_ref[...] = _nn(p[:, :Q_TILE], v0_ref[...]) + _nn(p[:, Q_TILE:], v1_ref[...])

    cur = pl.BlockSpec((Q_TILE, HEAD_DIM), lambda h, i: (i, h))
    prev = pl.BlockSpec((Q_TILE, HEAD_DIM), lambda h, i: (jnp.maximum(i - 1, 0), h))
    return pl.pallas_call(
        body, name=name, grid=(nh, nt),
        in_specs=[cur, prev, cur, prev, cur, pl.BlockSpec((None, Q_TILE, WIN), lambda h, i: (h, 0, 0))],
        out_specs=cur, out_shape=jax.ShapeDtypeStruct((t, inner), F32),
        compiler_params=_cparams("parallel", "parallel"),
    )(q, k, k, v, v, bias)


def _attn_bwd(q, k, v, do, bias, name):
    t, inner = q.shape
    nh, nt = inner // HEAD_DIM, t // Q_TILE
    scale = HEAD_DIM ** -0.5

    def body(q_ref, k0_ref, k1_ref, v0_ref, v1_ref, do_ref, b_ref, dq_ref, dk_ref, dv_ref, db_ref, ck_ref, cv_ref):
        i = pl.program_id(1)

        @pl.when(i == 0)
        def _():
            ck_ref[...] = jnp.zeros(blk, F32)
            cv_ref[...] = jnp.zeros(blk, F32)

        @pl.when(i < nt)
        def _():
            qv, dov = q_ref[...], do_ref[...]
            p = _attn_probs(qv, k0_ref[...], k1_ref[...], b_ref[...], i == 0)
            dp = jnp.concatenate([_nt(dov, v0_ref[...]), _nt(dov, v1_ref[...])], axis=1)
            ds = p * (dp - jnp.sum(p * dp, axis=-1, keepdims=True))

            @pl.when(i == 0)
            def _():
                db_ref[...] = ds

            @pl.when(i > 0)
            def _():
                db_ref[...] += ds

            pb, dsb = _bf(p), _bf(ds)
            dq_ref[...] = (_nn(dsb[:, :Q_TILE], k0_ref[...]) + _nn(dsb[:, Q_TILE:], k1_ref[...])) * scale
            dk_ref[...] = ck_ref[...] + _tn(dsb[:, :Q_TILE], qv) * scale
            dv_ref[...] = (cv_ref[...] + _tn(pb[:, :Q_TILE], dov)).astype(BF16)
            ck_ref[...] = _tn(dsb[:, Q_TILE:], qv) * scale
            cv_ref[...] = _tn(pb[:, Q_TILE:], dov)

        @pl.when(i == nt)
        def _():
            dk_ref[...] = ck_ref[...]
            dv_ref[...] = cv_ref[...].astype(BF16)

    blk = (Q_TILE, HEAD_DIM)
    cur = pl.BlockSpec(blk, lambda h, i: (jnp.minimum(i, nt - 1), h))
    prev = pl.BlockSpec(blk, lambda h, i: (jnp.clip(i - 1, 0, nt - 1), h))
    lag = pl.BlockSpec(blk, lambda h, i: (jnp.maximum(i - 1, 0), h))
    tile = pl.BlockSpec((None, Q_TILE, WIN), lambda h, i: (h, 0, 0))
    return pl.pallas_call(
        body, name=name, grid=(nh, nt + 1),
        in_specs=[cur, prev, cur, prev, cur, cur, tile],
        out_specs=[cur, lag, lag, tile],
        out_shape=[jax.ShapeDtypeStruct((t, inner), F32)] * 2 + [jax.ShapeDtypeStruct((t, inner), BF16),
                                                                 jax.ShapeDtypeStruct((nh, Q_TILE, WIN), F32)],
        scratch_shapes=[pltpu.VMEM(blk, F32), pltpu.VMEM(blk, F32)],
        compiler_params=_cparams("arbitrary", "arbitrary"),
    )(q, k, k, v, v, do, bias)


def _pad_rel_bias(rel_bias):
    nh = rel_bias.shape[0]
    return jnp.broadcast_to(jnp.pad(rel_bias, ((0, 0), (0, REL_PAD - N_REL)))[:, None, :], (nh, 8, REL_PAD))


def _layer_b_fwd(h1, nw, w_in, qw, kw, bias, w_out):
    t, d = h1.shape
    inner = w_out.shape[0]
    hn = _rms_fwd(h1, nw, "b_rms")
    proj = _mm(hn, w_in, "nn", t, 4 * inner, d, out_dtype=F32, name="b_proj")
    qn = _headnorm_fwd(proj, qw, 0, inner, "b_qnorm")
    kn = _headnorm_fwd(proj, kw, inner, inner, "b_knorm")
    vb = proj[:, 2 * inner:3 * inner].astype(BF16)
    o = _attn_fwd(qn, kn, vb, bias, "b_attn")
    g = _gate_fwd(o, proj, 3 * inner, inner, "b_gate")
    h2 = _mm(g, w_out, "nn", t, d, inner, out_dtype=F32, name="b_out", res=h1)
    return h2, (hn, proj, qn, kn, vb, o, g)


def _layer_b_bwd(dh2, dh2b, h1, nw, w_in, qw, kw, bias, w_out, saved):
    hn, proj, qn, kn, vb, o, g = saved
    t, d = h1.shape
    inner = w_out.shape[0]
    dg = _mm(dh2b, w_out, "nt", t, inner, d, out_dtype=F32, name="b_dgate")
    dw_out = _mm(g, dh2b, "tn", inner, d, t, out_dtype=F32, name="b_dwout")
    do, dz = _gate_bwd(dg, o, proj, 3 * inner, inner, "b_gate_bwd", BF16)
    dq, dk, dv, dtile = _attn_bwd(qn, kn, vb, do, bias, "b_attn_bwd")
    dqr, dqw = _headnorm_bwd(dq, proj, qw, 0, inner, "b_qnorm_bwd")
    dkr, dkw = _headnorm_bwd(dk, proj, kw, inner, inner, "b_knorm_bwd")
    dproj = jnp.concatenate([dqr, dkr, dv, dz], axis=1)
    dhn = _mm(dproj, w_in, "nt", t, d, 4 * inner, out_dtype=F32, name="b_dhn")
    dw_in = _mm(hn, dproj, "tn", d, 4 * inner, t, out_dtype=F32, name="b_dwin")
    dh1, dh1b, dnw = _rms_bwd(h1, nw, dhn, dh2, "b_rms_bwd")
    drb = _bias_grad(dtile, "b_bias_grad")[:, 0, :N_REL]
    return dh1, dh1b, dnw, dw_in, dqw, dkw, drb, dw_out


LANES = 128


def _softplus(x):
    return jnp.maximum(x, 0.0) + jnp.log1p(jnp.exp(-jnp.abs(x)))


def _gates_fwd(ab, alog_row, dt_row, nh, name, tr=1024):
    t = ab.shape[0]
    tr = min(tr, t)

    def body(x_ref, al_ref, dt_ref, o_ref):
        x = x_ref[...]
        lane = lax.broadcasted_iota(jnp.int32, x.shape, 1)
        g = -jnp.exp(al_ref[...]) * _softplus(x + dt_ref[...])
        o_ref[...] = jnp.where(lane < nh, g, jnp.where(lane < 2 * nh, _sigmoid(x), 0.0))

    row = pl.BlockSpec((tr, LANES), lambda i: (i, 0))
    vec = pl.BlockSpec((1, LANES), lambda i: (0, 0))
    return pl.pallas_call(
        body, name=name, grid=(t // tr,), in_specs=[row, vec, vec], out_specs=row,
        out_shape=jax.ShapeDtypeStruct((t, LANES), F32), compiler_params=_cparams("parallel"),
    )(ab, alog_row, dt_row)


def _gates_bwd(ab, alog_row, dt_row, dgates, nh, name, tr=1024):
    t = ab.shape[0]
    tr = min(tr, t)
    npart = dgates.shape[0]

    def body(x_ref, al_ref, dt_ref, dg_ref, dx_ref, s_ref):
        x = x_ref[...]
        lane = lax.broadcasted_iota(jnp.int32, x.shape, 1)
        dgt = dg_ref[0]
        for p in range(1, npart):
            dgt = dgt + dg_ref[p]
        ea = jnp.exp(al_ref[...])
        xa = x + dt_ref[...]
        da = jnp.where(lane < nh, dgt * (-ea) * _sigmoid(xa), 0.0)
        beta = _sigmoid(x)
        db = jnp.where((lane >= nh) & (lane < 2 * nh), dgt * beta * (1.0 - beta), 0.0)
        dx_ref[...] = (da + db).astype(BF16)
        dal = jnp.sum(jnp.where(lane < nh, dgt * (-ea) * _softplus(xa), 0.0), axis=0, keepdims=True)
        ddt = jnp.sum(da, axis=0, keepdims=True)
        r8 = lax.broadcasted_iota(jnp.int32, (8, LANES), 0)
        part = jnp.where(r8 == 0, dal, jnp.where(r8 == 1, ddt, 0.0))

        @pl.when(pl.program_id(0) == 0)
        def _():
            s_ref[...] = part

        @pl.when(pl.program_id(0) > 0)
        def _():
            s_ref[...] += part

    row = pl.BlockSpec((tr, LANES), lambda i: (i, 0))
    vec = pl.BlockSpec((1, LANES), lambda i: (0, 0))
    return pl.pallas_call(
        body, name=name, grid=(t // tr,),
        in_specs=[row, vec, vec, pl.BlockSpec((npart, tr, LANES), lambda i: (0, i, 0))],
        out_specs=[row, pl.BlockSpec((8, LANES), lambda i: (0, 0))],
        out_shape=[jax.ShapeDtypeStruct((t, LANES), BF16), jax.ShapeDtypeStruct((8, LANES), F32)],
        compiler_params=_cparams("arbitrary"),
    )(ab, alog_row, dt_row, dgates)


HALO = 8


def _conv_taps(ext, w, rows):
    acc = ext[HALO:HALO + rows] * w[CONV_K - 1:CONV_K]
    for s in range(1, CONV_K):
        acc = acc + pltpu.roll(ext, s, 0)[HALO:HALO + rows] * w[CONV_K - 1 - s:CONV_K - s]
    return acc


def _conv_fwd(proj, conv_w, col0, inner, mode, name, tt=512, hb=4):
    t = proj.shape[0]
    tt = min(tt, t)
    wc = hb * HEAD_DIM
    c0 = col0 // wc
    hpb = tt // HALO

    def body(x_ref, halo_ref, w_ref, o_ref):
        halo = jnp.where(pl.program_id(1) == 0, 0.0, halo_ref[...])
        s = _silu(_conv_taps(jnp.concatenate([halo, x_ref[...]], axis=0), w_ref[...], tt))
        if mode == "v":
            o_ref[...] = s
        else:
            mul = HEAD_DIM ** -0.5 if mode == "q" else 1.0
            o_ref[...] = jnp.concatenate(
                [sh * (lax.rsqrt(jnp.sum(sh * sh, axis=-1, keepdims=True) + EPS) * mul) for sh in _heads_of(s, hb)], axis=1)

    return pl.pallas_call(
        body, name=name, grid=(inner // wc, t // tt),
        in_specs=[pl.BlockSpec((tt, wc), lambda j, i: (i, j + c0)),
                  pl.BlockSpec((HALO, wc), lambda j, i: (jnp.maximum(i * hpb - 1, 0), j + c0)),
                  pl.BlockSpec((CONV_K, wc), lambda j, i: (0, j + c0))],
        out_specs=pl.BlockSpec((tt, wc), lambda j, i: (i, j)),
        out_shape=jax.ShapeDtypeStruct((t, inner), F32),
        compiler_params=_cparams("parallel", "parallel"),
    )(proj, proj, conv_w)


def _conv_bwd(dy, proj, conv_w, col0, inner, mode, name, tt=512, hb=4):
    t = proj.shape[0]
    tt = min(tt, t)
    nt = t // tt
    wc = hb * HEAD_DIM
    c0 = col0 // wc
    hpb = tt // HALO
    rows = tt + HALO

    def body(dy_ref, dyn_ref, x_ref, xp_ref, xn_ref, w_ref, dx_ref, dw_ref):
        i = pl.program_id(1)
        w = w_ref[...]
        xprev = jnp.where(i == 0, 0.0, xp_ref[...])
        ext = jnp.concatenate([xprev, x_ref[...], xn_ref[...]], axis=0)
        c = _conv_taps(ext, w, rows)
        dyv = jnp.concatenate([dy_ref[...], jnp.where(i == nt - 1, 0.0, dyn_ref[...])], axis=0)
        sg = _sigmoid(c)
        s = c * sg
        if mode == "v":
            ds = dyv
        else:
            mul = HEAD_DIM ** -0.5 if mode == "q" else 1.0
            parts = []
            for dyh, sh in zip(_heads_of(dyv, hb), _heads_of(s, hb)):
                r = lax.rsqrt(jnp.sum(sh * sh, axis=-1, keepdims=True) + EPS)
                parts.append(mul * (r * dyh - sh * (r * r * r) * jnp.sum(dyh * sh, axis=-1, keepdims=True)))
            ds = jnp.concatenate(parts, axis=1)
        dc = ds * (sg * (1.0 + c * (1.0 - sg)))
        dx = dc[:tt] * w[CONV_K - 1:CONV_K]
        for sft in range(1, CONV_K):
            dx = dx + pltpu.roll(dc, rows - sft, 0)[:tt] * w[CONV_K - 1 - sft:CONV_K - sft]
        dx_ref[...] = dx.astype(BF16)
        r8 = lax.broadcasted_iota(jnp.int32, (8, wc), 0)
        part = jnp.zeros((8, wc), F32)
        for sft in range(CONV_K):
            xs = ext[HALO:HALO + tt] if sft == 0 else pltpu.roll(ext, sft, 0)[HALO:HALO + tt]
            part = part + jnp.where(r8 == CONV_K - 1 - sft, jnp.sum(dc[:tt] * xs, axis=0, keepdims=True), 0.0)

        @pl.when(i == 0)
        def _():
            dw_ref[...] = part

        @pl.when(i > 0)
        def _():
            dw_ref[...] += part

    cur = lambda off: pl.BlockSpec((tt, wc), lambda j, i: (i, j + off))
    nxt = lambda off: pl.BlockSpec((HALO, wc), lambda j, i: (jnp.minimum((i + 1) * hpb, t // HALO - 1), j + off))
    return pl.pallas_call(
        body, name=name, grid=(inner // wc, nt),
        in_specs=[cur(0), nxt(0), cur(c0),
                  pl.BlockSpec((HALO, wc), lambda j, i: (jnp.maximum(i * hpb - 1, 0), j + c0)), nxt(c0),
                  pl.BlockSpec((CONV_K, wc), lambda j, i: (0, j + c0))],
        out_specs=[pl.BlockSpec((tt, wc), lambda j, i: (i, j)), pl.BlockSpec((8, wc), lambda j, i: (0, j))],
        out_shape=[jax.ShapeDtypeStruct((t, inner), BF16), jax.ShapeDtypeStruct((8, inner), F32)],
        compiler_params=_cparams("parallel", "arbitrary"),
    )(dy, dy, proj, proj, proj, conv_w)


GDN_HB = 4
GDN_NB = 2


def _iota2(n, m):
    return lax.broadcasted_iota(jnp.int32, (n, m), 0), lax.broadcasted_iota(jnp.int32, (n, m), 1)


def _head_select(first_head, hb, lane0):
    r, lane = _iota2(8, LANES)
    return jnp.where((r < hb) & (lane == lane0 + first_head + r), 1.0, 0.0).astype(F32)


def _chunk_gates(gt, selg, selb):
    i, j = _iota2(CHUNK, CHUNK)
    gc_all = _nn(jnp.where(j <= i, 1.0, 0.0).astype(F32), gt, HIGHEST)
    return _nt(gc_all, selg, HIGHEST), _nt(selg, gc_all, HIGHEST), _nt(gt, selb, HIGHEST)


def _decay_terms(gcol, grow):
    i, j = _iota2(CHUNK, CHUNK)
    glast = gcol[CHUNK - 1:CHUNK]
    decay = jnp.exp(jnp.where(j <= i, gcol - grow, NEG_BIG))
    return jnp.exp(gcol), jnp.exp(glast - gcol), jnp.exp(glast), decay


def _unit_lower_inverse(a):
    i, j = _iota2(CHUNK, CHUNK)
    same16 = (i // 16) == (j // 16)
    same32 = (i // 32) == (j // 32)
    m = jnp.where(same16, -a, 0.0)
    x = jnp.where(i == j, 1.0, 0.0) + m
    for _ in range(3):
        m = _nn(m, m, HIGHEST)
        x = x + _nn(x, m, HIGHEST)
    for off in (jnp.where(same32 & jnp.logical_not(same16), a, 0.0), jnp.where(same32, 0.0, a)):
        x = x - _nn(_nn(x, off, HIGHEST), x, HIGHEST)
    return x


def _gdn_specs(nh, inner, t):
    hb, nb = min(GDN_HB, nh), GDN_NB
    rows = nb * CHUNK
    wide = pl.BlockSpec((rows, hb * HEAD_DIM), lambda g, n: (n, g))
    sq = pl.BlockSpec((hb, rows, CHUNK), lambda g, n: (g, n, 0))
    gts = pl.BlockSpec((rows, LANES), lambda g, n: (n, 0))
    glb = pl.BlockSpec((nb * 8, hb * HEAD_DIM), lambda g, n: (n, g))
    return hb, nb, rows, wide, sq, gts, glb


def _gdn_intra_fwd(q, k, v, gates, nh, name):
    t, inner = q.shape
    hb, nb, rows, wide, sq, gts, glb = _gdn_specs(nh, inner, t)

    def body(q_ref, k_ref, v_ref, g_ref, qe_ref, kel_ref, wb_ref, w_ref, u_ref, qk_ref, tm_ref, gl_ref):
        first = pl.program_id(0) * hb
        selg, selb = _head_select(first, hb, 0), _head_select(first, hb, nh)
        i, j = _iota2(CHUNK, CHUNK)
        for c in range(nb):
            rs = slice(c * CHUNK, (c + 1) * CHUNK)
            gcols, grows, bcols = _chunk_gates(g_ref[rs, :], selg, selb)
            for h in range(hb):
                cs = slice(h * HEAD_DIM, (h + 1) * HEAD_DIM)
                qv, kv, vv = q_ref[rs, cs], k_ref[rs, cs], v_ref[rs, cs]
                bcol = bcols[:, h:h + 1]
                e, el, gl, decay = _decay_terms(gcols[:, h:h + 1], grows[h:h + 1, :])
                kb = kv * bcol
                a = jnp.where(j < i, _nt(_bf(kb), _bf(kv)) * decay, 0.0)
                tm = _unit_lower_inverse(a)
                w = _nn(tm, kb * e, HIGHEST)
                qe_ref[rs, cs] = _bf(qv * e)
                kel_ref[rs, cs] = _bf(kv * el)
                wb_ref[rs, cs] = _bf(w)
                w_ref[rs, cs] = w
                u_ref[rs, cs] = _nn(tm, vv * bcol, HIGHEST)
                qk_ref[h, rs, :] = _bf(_nt(_bf(qv), _bf(kv)) * decay)
                tm_ref[h, rs, :] = tm
                gl_ref[c * 8:(c + 1) * 8, cs] = jnp.broadcast_to(gl, (8, HEAD_DIM))

    big = lambda dt: jax.ShapeDtypeStruct((t, inner), dt)
    return pl.pallas_call(
        body, name=name, grid=(nh // hb, t // rows),
        in_specs=[wide, wide, wide, gts],
        out_specs=[wide] * 5 + [sq, sq, glb],
        out_shape=[big(BF16), big(BF16), big(BF16), big(F32), big(F32),
                   jax.ShapeDtypeStruct((nh, t, CHUNK), BF16), jax.ShapeDtypeStruct((nh, t, CHUNK), F32),
                   jax.ShapeDtypeStruct((t // CHUNK * 8, inner), F32)],
        compiler_params=_cparams("parallel", "parallel"),
    )(q, k, v, gates)


def _gdn_scan_fwd(qe, kel, wb, u, qk, glb, nh, name):
    t, inner = u.shape
    hb, nb, rows, wide, sq, _, glb_spec = _gdn_specs(nh, inner, t)

    def body(qe_ref, kel_ref, wb_ref, u_ref, qk_ref, gl_ref, o_ref, vn_ref, sall_ref, s_ref):
        @pl.when(pl.program_id(1) == 0)
        def _():
            s_ref[...] = jnp.zeros(s_ref.shape, F32)

        for c in range(nb):
            rs = slice(c * CHUNK, (c + 1) * CHUNK)
            for h in range(hb):
                cs = slice(h * HEAD_DIM, (h + 1) * HEAD_DIM)
                s = s_ref[h]
                sall_ref[c, h] = s
                sb = _bf(s)
                vn = u_ref[rs, cs] - _nn(wb_ref[rs, cs], sb)
                vn_ref[rs, cs] = vn
                vnb = _bf(vn)
                o_ref[rs, cs] = _nn(qe_ref[rs, cs], sb) + _nn(qk_ref[h, rs, :], vnb)
                s_ref[h] = s * gl_ref[c * 8:c * 8 + 1, cs] + _tn(kel_ref[rs, cs], vnb)

    return pl.pallas_call(
        body, name=name, grid=(nh // hb, t // rows),
        in_specs=[wide, wide, wide, wide, sq, glb_spec],
        out_specs=[wide, wide, pl.BlockSpec((nb, hb, HEAD_DIM, HEAD_DIM), lambda g, n: (n, g, 0, 0))],
        out_shape=[jax.ShapeDtypeStruct((t, inner), F32), jax.ShapeDtypeStruct((t, inner), F32),
                   jax.ShapeDtypeStruct((t // CHUNK, nh, HEAD_DIM, HEAD_DIM), F32)],
        scratch_shapes=[pltpu.VMEM((hb, HEAD_DIM, HEAD_DIM), F32)],
        compiler_params=_cparams("parallel", "arbitrary"),
    )(qe, kel, wb, u, qk, glb)


def _gdn_scan_bwd(do, qe, kel, wb, vn, qk, glb, sall, nh, name):
    t, inner = do.shape
    hb, nb, rows, _, _, _, _ = _gdn_specs(nh, inner, t)
    last = t // rows - 1
    wide = pl.BlockSpec((rows, hb * HEAD_DIM), lambda g, n: (last - n, g))
    sq = pl.BlockSpec((hb, rows, CHUNK), lambda g, n: (g, last - n, 0))
    glb_spec = pl.BlockSpec((nb * 8, hb * HEAD_DIM), lambda g, n: (last - n, g))

    def body(do_ref, qe_ref, kel_ref, wb_ref, vn_ref, qk_ref, gl_ref, sall_ref,
             dvn_ref, dw_ref, dqe_ref, dkel_ref, dqk_ref, dgl_ref, ds_ref):
        @pl.when(pl.program_id(1) == 0)
        def _():
            ds_ref[...] = jnp.zeros(ds_ref.shape, F32)

        for c in reversed(range(nb)):
            rs = slice(c * CHUNK, (c + 1) * CHUNK)
            for h in range(hb):
                cs = slice(h * HEAD_DIM, (h + 1) * HEAD_DIM)
                ds, s = ds_ref[h], sall_ref[c, h]
                dsb, sb = _bf(ds), _bf(s)
                dob, vnb = _bf(do_ref[rs, cs]), _bf(vn_ref[rs, cs])
                kel, qe, wb = kel_ref[rs, cs], qe_ref[rs, cs], wb_ref[rs, cs]
                dvn = _tn(qk_ref[h, rs, :], dob) + _nn(kel, dsb)
                dvnb = _bf(dvn)
                dvn_ref[rs, cs] = dvn
                dw_ref[rs, cs] = -_nt(dvnb, sb)
                dqe_ref[rs, cs] = _nt(dob, sb)
                dkel_ref[rs, cs] = _nt(vnb, dsb)
                dqk_ref[h, rs, :] = _nt(dob, vnb)
                dgl_ref[c * 8:(c + 1) * 8, cs] = jnp.zeros((8, HEAD_DIM), F32) + jnp.sum(ds * s)
                ds_ref[h] = ds * gl_ref[c * 8:c * 8 + 1, cs] + _tn(qe, dob) - _tn(wb, dvnb)

    big = jax.ShapeDtypeStruct((t, inner), F32)
    return pl.pallas_call(
        body, name=name, grid=(nh // hb, t // rows),
        in_specs=[wide, wide, wide, wide, wide, sq, glb_spec,
                  pl.BlockSpec((nb, hb, HEAD_DIM, HEAD_DIM), lambda g, n: (last - n, g, 0, 0))],
        out_specs=[wide] * 4 + [sq, glb_spec],
        out_shape=[big] * 4 + [jax.ShapeDtypeStruct((nh, t, CHUNK), F32),
                               jax.ShapeDtypeStruct((t // CHUNK * 8, inner), F32)],
        scratch_shapes=[pltpu.VMEM((hb, HEAD_DIM, HEAD_DIM), F32)],
        compiler_params=_cparams("parallel", "arbitrary"),
    )(do, qe, kel, wb, vn, qk, glb, sall)


def _gdn_intra_bwd(q, k, v, gates, tm, w, u, dvn, dw, dqe, dkel, dqk, dglb, nh, name):
    t, inner = q.shape
    hb, nb, rows, wide, sq, gts, glb = _gdn_specs(nh, inner, t)

    def body(q_ref, k_ref, v_ref, g_ref, tm_ref, w_ref, u_ref, dvn_ref, dw_ref, dqe_ref, dkel_ref, dqk_ref,
             dgl_ref, dq_ref, dk_ref, dv_ref, dg_ref):
        first = pl.program_id(0) * hb
        selg, selb = _head_select(first, hb, 0), _head_select(first, hb, nh)
        i, j = _iota2(CHUNK, CHUNK)
        lane8 = lax.broadcasted_iota(jnp.int32, (CHUNK, 8), 1)
        row8 = lax.broadcasted_iota(jnp.int32, (CHUNK, 8), 0)
        ones = jnp.ones((CHUNK, LANES), F32)
        lower = jnp.where(j <= i, 1.0, 0.0).astype(F32)
        rsum = lambda x: jnp.sum(x, axis=-1, keepdims=True)
        for c in range(nb):
            rs = slice(c * CHUNK, (c + 1) * CHUNK)
            gcols, grows, bcols = _chunk_gates(g_ref[rs, :], selg, selb)
            dgc_cols = jnp.zeros((CHUNK, 8), F32)
            dbeta_cols = jnp.zeros((CHUNK, 8), F32)
            for h in range(hb):
                cs = slice(h * HEAD_DIM, (h + 1) * HEAD_DIM)
                qv, kv, vv = q_ref[rs, cs], k_ref[rs, cs], v_ref[rs, cs]
                bcol = bcols[:, h:h + 1]
                e, el, gl, decay = _decay_terms(gcols[:, h:h + 1], grows[h:h + 1, :])
                kb = kv * bcol
                tmv, wv, uv = tm_ref[h, rs, :], w_ref[rs, cs], u_ref[rs, cs]
                dqe, dkel = dqe_ref[rs, cs], dkel_ref[rs, cs]
                qb, kbf, kbb = _bf(qv), _bf(kv), _bf(kb)
                dqk = jnp.where(j <= i, dqk_ref[h, rs, :], 0.0)
                dqkr = _bf(dqk * decay)
                dq = dqe * e + _nn(dqkr, kbf)
                dk = dkel * el + _tn(dqkr, qb)
                de = rsum(dqe * qv)
                del_ = rsum(dkel * kv)
                mq = dqk * _nt(qb, kbf) * decay
                dvb = _tn(tmv, dvn_ref[rs, cs], HIGHEST)
                dkbe = _tn(tmv, dw_ref[rs, cs], HIGHEST)
                da = -jnp.where(j < i, _nt(dvb, uv, HIGHEST) + _nt(dkbe, wv, HIGHEST), 0.0)
                dkk = _bf(da * decay)
                ma = da * _nt(kbb, kbf) * decay
                dkb = dkbe * e + _nn(dkk, kbf)
                de = de + rsum(dkbe * kb)
                dk = dk + _tn(dkk, kbb) + dkb * bcol
                dq_ref[rs, cs] = dq
                dk_ref[rs, cs] = dk
                dv_ref[rs, cs] = dvb * bcol
                dbeta = rsum(dkb * kv) + rsum(dvb * vv)
                m = mq + ma
                dgc = rsum(m) - _tn(m, ones, HIGHEST)[:, 0:1] + de * e - del_ * el
                tail = jnp.sum(del_ * el) + dgl_ref[c * 8:c * 8 + 1, h * HEAD_DIM:h * HEAD_DIM + 1] * gl
                dgc_cols = jnp.where(lane8 == h, dgc + jnp.where(row8 == CHUNK - 1, tail, 0.0), dgc_cols)
                dbeta_cols = jnp.where(lane8 == h, dbeta, dbeta_cols)
            dg_cols = _tn(lower, dgc_cols, HIGHEST)
            dg_ref[rs, :] = _nn(dg_cols, selg, HIGHEST) + _nn(dbeta_cols, selb, HIGHEST)

    big = jax.ShapeDtypeStruct((t, inner), F32)
    return pl.pallas_call(
        body, name=name, grid=(nh // hb, t // rows),
        in_specs=[wide, wide, wide, gts, sq, wide, wide, wide, wide, wide, wide, sq, glb],
        out_specs=[wide, wide, wide, pl.BlockSpec((None, rows, LANES), lambda g, n: (g, n, 0))],
        out_shape=[big, big, big, jax.ShapeDtypeStruct((nh // hb, t, LANES), F32)],
        compiler_params=_cparams("parallel", "parallel"),
    )(q, k, v, gates, tm, w, u, dvn, dw, dqe, dkel, dqk, dglb)


def _layer_a_fwd(x, nw, w_main, w_ab, conv_w, alog_row, dt_row, onw, w_out, nh):
    t, d = x.shape
    inner = w_out.shape[0]
    hn = _rms_fwd(x, nw, "a_rms")
    proj = _mm(hn, w_main, "nn", t, 4 * inner, d, out_dtype=F32, name="a_proj")
    ab = _mm(hn, w_ab, "nn", t, LANES, d, out_dtype=F32, name="a_proj_ab")
    gates = _gates_fwd(ab, alog_row, dt_row, nh, "a_gates")
    q = _conv_fwd(proj, conv_w, 0, inner, "q", "a_conv_q")
    k = _conv_fwd(proj, conv_w, inner, inner, "k", "a_conv_k")
    v = _conv_fwd(proj, conv_w, 2 * inner, inner, "v", "a_conv_v")
    qe, kel, wb, w, u, qk, tm, glb = _gdn_intra_fwd(q, k, v, gates, nh, "a_intra")
    o, vn, sall = _gdn_scan_fwd(qe, kel, wb, u, qk, glb, nh, "a_scan")
    g = _gate_fwd(o, proj, 3 * inner, inner, "a_gate", norm_w=onw)
    h1 = _mm(g, w_out, "nn", t, d, inner, out_dtype=F32, name="a_out", res=x)
    return h1, (hn, proj, ab, gates, q, k, v, qe, kel, wb, w, u, qk, tm, glb, o, vn, sall, g)


def _layer_a_bwd(dh1, dh1b, x, nw, w_main, w_ab, conv_w, alog_row, dt_row, onw, w_out, nh, saved):
    hn, proj, ab, gates, q, k, v, qe, kel, wb, w, u, qk, tm, glb, o, vn, sall, g = saved
    t, d = x.shape
    inner = w_out.shape[0]
    dg = _mm(dh1b, w_out, "nt", t, inner, d, out_dtype=F32, name="a_dgate")
    dw_out = _mm(g, dh1b, "tn", inner, d, t, out_dtype=F32, name="a_dwout")
    do, dz, donw = _gate_bwd(dg, o, proj, 3 * inner, inner, "a_gate_bwd", F32, norm_w=onw)
    dvn, dw, dqe, dkel, dqk, dglb = _gdn_scan_bwd(do, qe, kel, wb, vn, qk, glb, sall, nh, "a_scan_bwd")
    dq, dk, dv, dgates = _gdn_intra_bwd(q, k, v, gates, tm, w, u, dvn, dw, dqe, dkel, dqk, dglb, nh, "a_intra_bwd")
    dxq, dcq = _conv_bwd(dq, proj, conv_w, 0, inner, "q", "a_conv_q_bwd")
    dxk, dck = _conv_bwd(dk, proj, conv_w, inner, inner, "k", "a_conv_k_bwd")
    dxv, dcv = _conv_bwd(dv, proj, conv_w, 2 * inner, inner, "v", "a_conv_v_bwd")
    dab, dsmall = _gates_bwd(ab, alog_row, dt_row, dgates, nh, "a_gates_bwd")
    dproj = jnp.concatenate([dxq, dxk, dxv, dz], axis=1)
    dhn = _mm(dab, w_ab, "nt", t, d, LANES, out_dtype=F32, name="a_dhn_ab")
    dhn = _mm(dproj, w_main, "nt", t, d, 4 * inner, out_dtype=F32, name="a_dhn", res=dhn)
    dw_main = _mm(hn, dproj, "tn", d, 4 * inner, t, out_dtype=F32, name="a_dwin")
    dw_ab = _mm(hn, dab, "tn", d, LANES, t, out_dtype=F32, name="a_dwin_ab")
    dx, _, dnw = _rms_bwd(x, nw, dhn, dh1, "a_rms_bwd")
    dconv = jnp.concatenate([dcq[:CONV_K], dck[:CONV_K], dcv[:CONV_K]], axis=1)
    return dx, dnw, dw_main, dw_ab, dconv, dsmall, donw, dw_out


def _rows_of(a, rows):
    flat = a.reshape(-1)
    return jnp.pad(flat, (0, rows * LANES - flat.shape[0])).reshape(rows, LANES)


def _to_slabs(g, axis):
    shape = g.shape[:axis] + (N_DEV, g.shape[axis] // N_DEV) + g.shape[axis + 1:]
    return jnp.moveaxis(g.reshape(shape), axis, 0)


def _from_slabs(s, axis):
    m = jnp.moveaxis(s, 0, axis)
    return m.reshape(m.shape[:axis] + (m.shape[axis] * m.shape[axis + 1],) + m.shape[axis + 2:])


def kernel(x, norm_w, a_w_in, a_conv_w, a_a_log, a_dt_bias, a_out_norm_w, a_w_out, b_w_in, b_q_norm_w, b_k_norm_w, b_rel_bias, b_w_out, loss_target, m_norm_w, m_a_w_in, m_a_conv_w, m_a_a_log, m_a_dt_bias, m_a_out_norm_w, m_a_w_out, m_b_w_in, m_b_q_norm_w, m_b_k_norm_w, m_b_rel_bias, m_b_w_out, v_norm_w, v_a_w_in, v_a_conv_w, v_a_a_log, v_a_dt_bias, v_a_out_norm_w, v_a_w_out, v_b_w_in, v_b_q_norm_w, v_b_k_norm_w, v_b_rel_bias, v_b_w_out):
    xs, target = x[0], loss_target[0]
    nh = a_a_log.shape[-1]
    inner = N_DEV * a_w_out.shape[1]

    ga_in, ga_out, gb_in, gb_out, g_conv = _all_gather(
        [a_w_in[0].astype(BF16), a_w_out[0].astype(BF16), b_w_in[0].astype(BF16), b_w_out[0].astype(BF16),
         a_conv_w[0]], "gather_weights")
    wa_in = _from_slabs(ga_in, 1)
    wa_main = wa_in[:, :4 * inner]
    wa_ab = jnp.pad(wa_in[:, 4 * inner:], ((0, 0), (0, LANES - 2 * nh)))
    wa_out = _from_slabs(ga_out, 0)
    wb_in = _from_slabs(gb_in, 1)
    wb_out = _from_slabs(gb_out, 0)
    conv_w = _from_slabs(g_conv, 1)
    nw0, nw1 = norm_w[0:1], norm_w[1:2]
    alog_row = jnp.pad(a_a_log, ((0, 0), (0, LANES - nh)))
    dt_row = jnp.pad(a_dt_bias, ((0, 0), (0, LANES - nh)))

    h1, saved_a = _layer_a_fwd(xs, nw0, wa_main, wa_ab, conv_w, alog_row, dt_row, a_out_norm_w, wa_out, nh)
    bias = _bias_tiles(_pad_rel_bias(b_rel_bias[0]), "b_bias_tiles")
    h2, saved_b = _layer_b_fwd(h1, nw1, wb_in, b_q_norm_w, b_k_norm_w, bias, wb_out)
    dh2, dh2b, loss_row = _loss_grad(h2, target, "loss")

    dh1, dh1b, dnw1, dwb_in, dqw, dkw, drb, dwb_out = _layer_b_bwd(
        dh2, dh2b, h1, nw1, wb_in, b_q_norm_w, b_k_norm_w, bias, wb_out, saved_b)
    dx, dnw0, dwa_main, dwa_ab, dconv, dsmall, donw, dwa_out = _layer_a_bwd(
        dh1, dh1b, xs, nw0, wa_main, wa_ab, conv_w, alog_row, dt_row, a_out_norm_w, wa_out, nh, saved_a)

    dwa_in = jnp.concatenate([dwa_main, dwa_ab[:, :2 * nh]], axis=1)
    parts = _slab_exchange(
        [_to_slabs(dwa_in, 1), _to_slabs(dwa_out, 0), _to_slabs(dwb_in, 1), _to_slabs(dwb_out, 0),
         _to_slabs(dconv, 1)], "exchange_grads")
    big = {}
    for name, p, w, m, v in (("a_w_in", parts[0], a_w_in, m_a_w_in, v_a_w_in),
                             ("a_w_out", parts[1], a_w_out, m_a_w_out, v_a_w_out),
                             ("b_w_in", parts[2], b_w_in, m_b_w_in, v_b_w_in),
                             ("b_w_out", parts[3], b_w_out, m_b_w_out, v_b_w_out),
                             ("a_conv_w", parts[4], a_conv_w, m_a_conv_w, v_a_conv_w)):
        big[name] = [o[None] for o in _adamw(p, w[0], m[0], v[0], "adamw_" + name)]

    small = (("norm_w", norm_w, m_norm_w, v_norm_w, jnp.concatenate([dnw0, dnw1], axis=0)),
             ("a_a_log", a_a_log, m_a_a_log, v_a_a_log, dsmall[0:1, :nh]),
             ("a_dt_bias", a_dt_bias, m_a_dt_bias, v_a_dt_bias, dsmall[1:2, :nh]),
             ("a_out_norm_w", a_out_norm_w, m_a_out_norm_w, v_a_out_norm_w, donw),
             ("b_q_norm_w", b_q_norm_w, m_b_q_norm_w, v_b_q_norm_w, dqw),
             ("b_k_norm_w", b_k_norm_w, m_b_k_norm_w, v_b_k_norm_w, dkw),
             ("b_rel_bias", b_rel_bias, m_b_rel_bias, v_b_rel_bias, drb))
    rows = [8 * (-(-w.size // (8 * LANES))) for _, w, _, _, _ in small]
    pack = lambda arrs: jnp.concatenate([_rows_of(a, r) for a, r in zip(arrs, rows)] + [jnp.zeros((8, LANES), F32)], axis=0)
    g_pack = jnp.concatenate([_rows_of(g, r) for (_, _, _, _, g), r in zip(small, rows)]
                             + [jnp.broadcast_to(loss_row, (8, LANES))], axis=0)
    (g_all,) = _all_gather([g_pack], "gather_small_grads")
    outs_small = _adamw(g_all, pack([s[1] for s in small]), pack([s[2] for s in small]),
                        pack([s[3] for s in small]), "adamw_small")
    start = 0
    for (name, w, _, _, _), r in zip(small, rows):
        big[name] = [o[start:start + r].reshape(-1)[:w.size].reshape(w.shape) for o in outs_small]
        start += r
    loss = outs_small[0][start, 0]

    order = ("norm_w", "a_w_in", "a_conv_w", "a_a_log", "a_dt_bias", "a_out_norm_w", "a_w_out", "b_w_in",
             "b_q_norm_w", "b_k_norm_w", "b_rel_bias", "b_w_out")
    return (loss, dx[None]) + tuple(big[n][i] for i in range(4) for n in order)
```

```python
import functools
import math

import jax
import jax.numpy as jnp
from jax import lax
from jax.experimental import pallas as pl
from jax.experimental.pallas import tpu as pltpu

F32 = jnp.float32
BF16 = jnp.bfloat16
MESH_IDS = pl.DeviceIdType.MESH
N_DEV = 8
CHUNK = 64
HEAD_DIM = 128
EPS = 1e-6
CONV_K = 4
LEFT_CHUNKS = 8
REL_CLIP = 256
Q_TILE = LEFT_CHUNKS * CHUNK
ADAM_LR = 0.001
ADAM_B1 = 0.9
ADAM_B2 = 0.999
ADAM_EPS = 1e-08
ADAM_WD = 0.01
ADAM_STEP = 10
NEG_BIG = -1e30
VMEM_LIMIT_BYTES = 56 * 1024 * 1024
HIGHEST = lax.Precision.HIGHEST
ANY = pl.BlockSpec(memory_space=pl.ANY)


def _cparams(*sem):
    return pltpu.CompilerParams(dimension_semantics=tuple(sem), vmem_limit_bytes=VMEM_LIMIT_BYTES)


def _dot(a, b, dims, precision=None):
    return lax.dot_general(a, b, (dims, ((), ())), preferred_element_type=F32, precision=precision)


def _nn(a, b, precision=None):
    return _dot(a, b, ((1,), (0,)), precision)


def _nt(a, b, precision=None):
    return _dot(a, b, ((1,), (1,)), precision)


def _tn(a, b, precision=None):
    return _dot(a, b, ((0,), (0,)), precision)


def _bf(x):
    return x.astype(BF16)


def _split(x, pieces=2):
    out = []
    for _ in range(pieces - 1):
        hi = x.astype(BF16)
        out.append(hi)
        x = x - hi.astype(F32)
    return out + [x.astype(BF16)]


def _dot3(a, b, dims):
    (ah, al), (bh, bl) = _split(a), _split(b)
    return _dot(ah, bh, dims) + (_dot(ah, bl, dims) + _dot(al, bh, dims))


def _dot_exact(a, b, dims, split_b):
    if split_b:
        a = a.astype(BF16)
        parts = [_dot(a, p, dims) for p in _split(b, 3)]
    else:
        b = b.astype(BF16)
        parts = [_dot(p, b, dims) for p in _split(a, 3)]
    return parts[0] + (parts[1] + parts[2])


NN, NT, TN = ((1,), (0,)), ((1,), (1,)), ((0,), (0,))


def _sigmoid(x):
    return 1.0 / (1.0 + jnp.exp(-x))


def _silu(x):
    return x * _sigmoid(x)


def _dsilu(x):
    s = _sigmoid(x)
    return s * (1.0 + x * (1.0 - s))


def _my_pos():
    return lax.axis_index("x"), lax.axis_index("y"), lax.axis_index("c")


def _peers(x, y, c):
    def flip(v, f):
        return 1 - v if f else v

    return [(flip(x, kx), flip(y, ky), flip(c, kc)) for kx in (0, 1) for ky in (0, 1) for kc in (0, 1)][1:]


def _lin(p):
    return 4 * p[0] + 2 * p[1] + p[2]


class _Comm:
    def __init__(self, kind, arrays):
        self.kind, self.arrays, self.n = kind, list(arrays), len(arrays)

    def out_shape(self):
        lead = (N_DEV,) if self.kind == "gather" else ()
        return [jax.ShapeDtypeStruct(lead + a.shape, a.dtype) for a in self.arrays]

    def scratch(self):
        return [pltpu.SemaphoreType.DMA((7 * self.n,)), pltpu.SemaphoreType.DMA((7 * self.n,)),
                pltpu.SemaphoreType.DMA((self.n,))]

    def _copies(self, ins, outs, sems, arrivals):
        send_sems, recv_sems, local_sems = sems
        x, y, c = _my_pos()
        me = _lin((x, y, c))
        gather = self.kind == "gather"
        mine = [ins[t] if gather else ins[t].at[me] for t in range(self.n)]
        remote = []
        for k, peer in enumerate(_peers(x, y, c)):
            for t in range(self.n):
                if arrivals:
                    src, dst = mine[t], outs[t].at[_lin(peer)]
                else:
                    src, dst = (ins[t] if gather else ins[t].at[_lin(peer)]), outs[t].at[me]
                remote.append(pltpu.make_async_remote_copy(
                    src_ref=src, dst_ref=dst, send_sem=send_sems.at[k * self.n + t],
                    recv_sem=recv_sems.at[k * self.n + t], device_id=peer, device_id_type=MESH_IDS))
        if arrivals:
            return remote
        return [pltpu.make_async_copy(mine[t], outs[t].at[me], local_sems.at[t]) for t in range(self.n)], remote

    def start(self, ins, outs, sems):
        local, sends = self._copies(ins, outs, sems, False)
        for cp in local + sends:
            cp.start()

    def finish(self, ins, outs, sems):
        for cp in self._copies(ins, outs, sems, True):
            cp.wait_recv()
        local, sends = self._copies(ins, outs, sems, False)
        for cp in sends:
            cp.wait_send()
        for cp in local:
            cp.wait()


def _comm_call(comm, name):
    n = comm.n

    def body(*refs):
        ins, outs, sems = refs[:n], refs[n:2 * n], refs[2 * n:]
        comm.start(ins, outs, sems)
        comm.finish(ins, outs, sems)

    return pl.pallas_call(
        body, name=name, out_shape=comm.out_shape(), in_specs=[ANY] * n, out_specs=[ANY] * n,
        scratch_shapes=comm.scratch(),
    )(*comm.arrays)


def _grid_call(body, *, name, grid, in_specs, out_specs, out_shape, args, scratch_shapes=(), semantics=None, comm=None):
    if comm is None:
        return pl.pallas_call(
            body, name=name, grid=grid, in_specs=in_specs, out_specs=out_specs, out_shape=out_shape,
            scratch_shapes=list(scratch_shapes), compiler_params=_cparams(*semantics),
        )(*args)
    n_in, n_out, n_sc, n = len(in_specs), len(out_specs), len(scratch_shapes), comm.n

    def full(*refs):
        ins, refs = refs[:n_in], refs[n_in:]
        cins, refs = refs[:n], refs[n:]
        outs, refs = refs[:n_out], refs[n_out:]
        couts, refs = refs[:n], refs[n:]
        scratch, sems = refs[:n_sc], refs[n_sc:]
        ids = [pl.program_id(a) for a in range(len(grid))]
        first = functools.reduce(jnp.logical_and, [i == 0 for i in ids])
        last = functools.reduce(jnp.logical_and, [i == g - 1 for i, g in zip(ids, grid)])

        @pl.when(first)
        def _():
            comm.start(cins, couts, sems)

        body(*ins, *outs, *scratch)

        @pl.when(last)
        def _():
            comm.finish(cins, couts, sems)

    return pl.pallas_call(
        full, name=name, grid=grid, in_specs=list(in_specs) + [ANY] * n, out_specs=list(out_specs) + [ANY] * n,
        out_shape=list(out_shape) + comm.out_shape(), scratch_shapes=list(scratch_shapes) + comm.scratch(),
        compiler_params=_cparams(*(["arbitrary"] * len(grid))),
    )(*(list(args) + comm.arrays))


def _mm(a, b, mode, m, n, k, *, out_dtype, name, tm=1024, tn=1024, tk=2048,
        a_m0=0, a_k0=0, b_n0=0, b_k0=0, res=None):
    tm, tn, tk = min(tm, m), min(tn, n), min(tk, k)
    nm, nn, nk = m // tm, n // tn, k // tk
    assert nm * tm == m and nn * tn == n and nk * tk == k
    am, ak, bn, bk = a_m0 // tm, a_k0 // tk, b_n0 // tn, b_k0 // tk
    assert am * tm == a_m0 and ak * tk == a_k0 and bn * tn == b_n0 and bk * tk == b_k0
    if mode == "tn":
        a_spec = pl.BlockSpec((tk, tm), lambda i, j, q: (q + ak, i + am))
        a_dims = (0,)
    else:
        a_spec = pl.BlockSpec((tm, tk), lambda i, j, q: (i + am, q + ak))
        a_dims = (1,)
    if mode == "nt":
        b_spec = pl.BlockSpec((tn, tk), lambda i, j, q: (j + bn, q + bk))
        b_dims = (1,)
    else:
        b_spec = pl.BlockSpec((tk, tn), lambda i, j, q: (q + bk, j + bn))
        b_dims = (0,)
    o_spec = pl.BlockSpec((tm, tn), lambda i, j, q: (i, j))
    has_res = res is not None

    def body(*refs):
        a_ref, b_ref = refs[0], refs[1]
        res_ref = refs[2] if has_res else None
        o_ref = refs[2 + has_res]
        p = _dot(a_ref[...], b_ref[...], (a_dims, b_dims))

        def finish(total):
            if has_res:
                total = total + res_ref[...].astype(F32)
            o_ref[...] = total.astype(out_dtype)

        if nk == 1:
            finish(p)
        else:
            acc_ref = refs[3 + has_res]
            q = pl.program_id(2)

            @pl.when(q == 0)
            def _():
                acc_ref[...] = p

            @pl.when(q > 0)
            def _():
                acc_ref[...] += p

            @pl.when(q == nk - 1)
            def _():
                finish(acc_ref[...])

    return pl.pallas_call(
        body, name=name, grid=(nm, nn, nk),
        in_specs=[a_spec, b_spec] + ([o_spec] if has_res else []),
        out_specs=o_spec, out_shape=jax.ShapeDtypeStruct((m, n), out_dtype),
        scratch_shapes=[pltpu.VMEM((tm, tn), F32)] if nk > 1 else [],
        compiler_params=_cparams("parallel", "parallel", "arbitrary"),
    )(*([a, b] + ([res] if has_res else [])))


def _rms_fwd(x, w, name, tr=512):
    t, d = x.shape
    tr = min(tr, t)

    def body(x_ref, w_ref, o_ref):
        xv = x_ref[...]
        r = lax.rsqrt(jnp.mean(xv * xv, axis=-1, keepdims=True) + EPS)
        o_ref[...] = (xv * r * w_ref[...]).astype(BF16)

    return pl.pallas_call(
        body, name=name, grid=(t // tr,),
        in_specs=[pl.BlockSpec((tr, d), lambda i: (i, 0)), pl.BlockSpec((1, d), lambda i: (0, 0))],
        out_specs=pl.BlockSpec((tr, d), lambda i: (i, 0)),
        out_shape=jax.ShapeDtypeStruct((t, d), BF16),
        compiler_params=_cparams("parallel"),
    )(x, w)


def _rms_bwd(x, w, dy, dres, name, tr=256):
    t, d = x.shape
    tr = min(tr, t)

    def body(x_ref, w_ref, dy_ref, dres_ref, dx_ref, dxb_ref, dw_ref):
        xv = x_ref[...]
        dyv = dy_ref[...].astype(F32)
        r = lax.rsqrt(jnp.mean(xv * xv, axis=-1, keepdims=True) + EPS)
        gy = dyv * w_ref[...]
        proj = jnp.sum(gy * xv, axis=-1, keepdims=True) * (1.0 / d)
        dx = dres_ref[...] + r * gy - xv * (r * r * r) * proj
        dx_ref[...] = dx
        dxb_ref[...] = dx.astype(BF16)
        part = jnp.sum(dyv * xv * r, axis=0, keepdims=True)

        @pl.when(pl.program_id(0) == 0)
        def _():
            dw_ref[...] = part

        @pl.when(pl.program_id(0) > 0)
        def _():
            dw_ref[...] += part

    row = pl.BlockSpec((tr, d), lambda i: (i, 0))
    vec = pl.BlockSpec((1, d), lambda i: (0, 0))
    return pl.pallas_call(
        body, name=name, grid=(t // tr,),
        in_specs=[row, vec, row, row], out_specs=[row, row, vec],
        out_shape=[jax.ShapeDtypeStruct((t, d), F32), jax.ShapeDtypeStruct((t, d), BF16),
                   jax.ShapeDtypeStruct((1, d), F32)],
        compiler_params=_cparams("arbitrary"),
    )(x, w, dy, dres)


def _adamw(parts, w, m, v, name, tr=128):
    r, c = w.shape
    tr = tr if r % tr == 0 else r
    c1 = 1.0 - ADAM_B1 ** ADAM_STEP
    c2 = 1.0 - ADAM_B2 ** ADAM_STEP

    def body(p_ref, w_ref, m_ref, v_ref, g_ref, d_ref, nm_ref, nv_ref):
        g = p_ref[0].astype(F32)
        for s in range(1, N_DEV):
            g = g + p_ref[s].astype(F32)
        nm = ADAM_B1 * m_ref[...] + (1.0 - ADAM_B1) * g
        nv = ADAM_B2 * v_ref[...] + (1.0 - ADAM_B2) * (g * g)
        m_hat = nm / c1
        v_hat = nv / c2
        g_ref[...] = g
        d_ref[...] = -ADAM_LR * (m_hat / (jnp.sqrt(v_hat) + ADAM_EPS) + ADAM_WD * w_ref[...])
        nm_ref[...] = nm
        nv_ref[...] = nv

    blk = pl.BlockSpec((tr, c), lambda i: (i, 0))
    return pl.pallas_call(
        body, name=name, grid=(r // tr,),
        in_specs=[pl.BlockSpec((N_DEV, tr, c), lambda i: (0, i, 0)), blk, blk, blk],
        out_specs=[blk] * 4, out_shape=[jax.ShapeDtypeStruct((r, c), F32)] * 4,
        compiler_params=_cparams("parallel"),
    )(parts, w, m, v)


def _heads_of(x, nh):
    return [x[:, h * HEAD_DIM:(h + 1) * HEAD_DIM] for h in range(nh)]


def _headnorm_fwd(proj, w, col0, inner, name, tr=512, hb=4):
    t = proj.shape[0]
    tr = min(tr, t)
    wc = hb * HEAD_DIM
    c0 = col0 // wc

    def body(x_ref, w_ref, o_ref):
        outs = []
        for xh in _heads_of(x_ref[...], hb):
            r = lax.rsqrt(jnp.mean(xh * xh, axis=-1, keepdims=True) + EPS)
            outs.append((xh * r * w_ref[...]).astype(BF16))
        o_ref[...] = jnp.concatenate(outs, axis=1)

    return pl.pallas_call(
        body, name=name, grid=(t // tr, inner // wc),
        in_specs=[pl.BlockSpec((tr, wc), lambda i, j: (i, j + c0)), pl.BlockSpec((1, HEAD_DIM), lambda i, j: (0, 0))],
        out_specs=pl.BlockSpec((tr, wc), lambda i, j: (i, j)),
        out_shape=jax.ShapeDtypeStruct((t, inner), BF16),
        compiler_params=_cparams("parallel", "parallel"),
    )(proj, w)


def _headnorm_bwd(dy, proj, w, col0, inner, name, tr=512, hb=4):
    t = proj.shape[0]
    tr = min(tr, t)
    wc = hb * HEAD_DIM
    c0 = col0 // wc

    def body(dy_ref, x_ref, w_ref, dx_ref, dw_ref):
        outs = []
        part = jnp.zeros((1, HEAD_DIM), F32)
        for dyh, xh in zip(_heads_of(dy_ref[...], hb), _heads_of(x_ref[...], hb)):
            r = lax.rsqrt(jnp.mean(xh * xh, axis=-1, keepdims=True) + EPS)
            gy = dyh * w_ref[...]
            pr = jnp.sum(gy * xh, axis=-1, keepdims=True) * (1.0 / HEAD_DIM)
            outs.append((r * gy - xh * (r * r * r) * pr).astype(BF16))
            part = part + jnp.sum(dyh * xh * r, axis=0, keepdims=True)
        dx_ref[...] = jnp.concatenate(outs, axis=1)
        first = (pl.program_id(0) == 0) & (pl.program_id(1) == 0)

        @pl.when(first)
        def _():
            dw_ref[...] = part

        @pl.when(jnp.logical_not(first))
        def _():
            dw_ref[...] += part

    blk = pl.BlockSpec((tr, wc), lambda i, j: (i, j))
    return pl.pallas_call(
        body, name=name, grid=(t // tr, inner // wc),
        in_specs=[blk, pl.BlockSpec((tr, wc), lambda i, j: (i, j + c0)), pl.BlockSpec((1, HEAD_DIM), lambda i, j: (0, 0))],
        out_specs=[blk, pl.BlockSpec((1, HEAD_DIM), lambda i, j: (0, 0))],
        out_shape=[jax.ShapeDtypeStruct((t, inner), BF16), jax.ShapeDtypeStruct((1, HEAD_DIM), F32)],
        compiler_params=_cparams("arbitrary", "arbitrary"),
    )(dy, proj, w)


def _gate_fwd(o, proj, zcol0, inner, name, norm_w=None, tr=512, hb=4):
    t = o.shape[0]
    tr = min(tr, t)
    wc = hb * HEAD_DIM
    c0 = zcol0 // wc
    has_w = norm_w is not None

    def body(*refs):
        o_ref, z_ref = refs[0], refs[1]
        out_ref = refs[2 + has_w]
        outs = []
        for oh, zh in zip(_heads_of(o_ref[...], hb), _heads_of(z_ref[...], hb)):
            if has_w:
                r = lax.rsqrt(jnp.mean(oh * oh, axis=-1, keepdims=True) + EPS)
                oh = oh * r * refs[2][...]
            outs.append((oh * _silu(zh)).astype(BF16))
        out_ref[...] = jnp.concatenate(outs, axis=1)

    blk = pl.BlockSpec((tr, wc), lambda i, j: (i, j))
    vec = pl.BlockSpec((1, HEAD_DIM), lambda i, j: (0, 0))
    return pl.pallas_call(
        body, name=name, grid=(t // tr, inner // wc),
        in_specs=[blk, pl.BlockSpec((tr, wc), lambda i, j: (i, j + c0))] + ([vec] if has_w else []),
        out_specs=blk, out_shape=jax.ShapeDtypeStruct((t, inner), BF16),
        compiler_params=_cparams("parallel", "parallel"),
    )(*([o, proj] + ([norm_w] if has_w else [])))


def _gate_bwd(dg, o, proj, zcol0, inner, name, do_dtype, norm_w=None, tr=512, hb=4):
    t = o.shape[0]
    tr = min(tr, t)
    wc = hb * HEAD_DIM
    c0 = zcol0 // wc
    has_w = norm_w is not None

    def body(*refs):
        dg_ref, o_ref, z_ref = refs[0], refs[1], refs[2]
        do_ref, dz_ref = refs[3 + has_w], refs[4 + has_w]
        dos, dzs = [], []
        part = jnp.zeros((1, HEAD_DIM), F32)
        for dgh, oh, zh in zip(_heads_of(dg_ref[...], hb), _heads_of(o_ref[...], hb), _heads_of(z_ref[...], hb)):
            dy = dgh * _silu(zh)
            if has_w:
                w = refs[3][...]
                r = lax.rsqrt(jnp.mean(oh * oh, axis=-1, keepdims=True) + EPS)
                on = oh * r
                dzs.append((dgh * on * w * _dsilu(zh)).astype(BF16))
                gy = dy * w
                pr = jnp.sum(gy * oh, axis=-1, keepdims=True) * (1.0 / HEAD_DIM)
                dos.append((r * gy - oh * (r * r * r) * pr).astype(do_dtype))
                part = part + jnp.sum(dy * on, axis=0, keepdims=True)
            else:
                dzs.append((dgh * oh * _dsilu(zh)).astype(BF16))
                dos.append(dy.astype(do_dtype))
        do_ref[...] = jnp.concatenate(dos, axis=1)
        dz_ref[...] = jnp.concatenate(dzs, axis=1)
        if has_w:
            dw_ref = refs[6]
            first = (pl.program_id(0) == 0) & (pl.program_id(1) == 0)

            @pl.when(first)
            def _():
                dw_ref[...] = part

            @pl.when(jnp.logical_not(first))
            def _():
                dw_ref[...] += part

    blk = pl.BlockSpec((tr, wc), lambda i, j: (i, j))
    vec = pl.BlockSpec((1, HEAD_DIM), lambda i, j: (0, 0))
    return pl.pallas_call(
        body, name=name, grid=(t // tr, inner // wc),
        in_specs=[blk, blk, pl.BlockSpec((tr, wc), lambda i, j: (i, j + c0))] + ([vec] if has_w else []),
        out_specs=[blk, blk] + ([vec] if has_w else []),
        out_shape=[jax.ShapeDtypeStruct((t, inner), do_dtype), jax.ShapeDtypeStruct((t, inner), BF16)]
        + ([jax.ShapeDtypeStruct((1, HEAD_DIM), F32)] if has_w else []),
        compiler_params=_cparams("arbitrary", "arbitrary"),
    )(*([dg, o, proj] + ([norm_w] if has_w else [])))


def _loss_grad(h, target, name, tr=512):
    t, d = h.shape
    tr = min(tr, t)

    def body(h_ref, t_ref, g_ref, gb_ref, l_ref):
        e = h_ref[...] - t_ref[...]
        g = e * (1.0 / d)
        g_ref[...] = g
        gb_ref[...] = g.astype(BF16)
        part = jnp.zeros((1, HEAD_DIM), F32) + 0.5 * jnp.sum(jnp.sum(e * e, axis=-1, keepdims=True) * (1.0 / d))

        @pl.when(pl.program_id(0) == 0)
        def _():
            l_ref[...] = part

        @pl.when(pl.program_id(0) > 0)
        def _():
            l_ref[...] += part

    row = pl.BlockSpec((tr, d), lambda i: (i, 0))
    return pl.pallas_call(
        body, name=name, grid=(t // tr,),
        in_specs=[row, row], out_specs=[row, row, pl.BlockSpec((1, HEAD_DIM), lambda i: (0, 0))],
        out_shape=[jax.ShapeDtypeStruct((t, d), F32), jax.ShapeDtypeStruct((t, d), BF16),
                   jax.ShapeDtypeStruct((1, HEAD_DIM), F32)],
        compiler_params=_cparams("arbitrary"),
    )(h, target)


N_REL = 2 * REL_CLIP + 1
REL_PAD = 640
WIN = 2 * Q_TILE


def _diag_onehot():
    i = lax.broadcasted_iota(jnp.int32, (REL_PAD, WIN), 0)
    j = lax.broadcasted_iota(jnp.int32, (REL_PAD, WIN), 1)
    rel = jnp.where(j < Q_TILE + CHUNK, Q_TILE - j, Q_TILE + WIN - j)
    used = (j < Q_TILE + CHUNK) | (j > WIN - CHUNK)
    idx = jnp.clip(rel, -REL_CLIP, REL_CLIP) + REL_CLIP
    return jnp.where(used & (i == idx), 1.0, 0.0).astype(F32)


def _band_mask():
    r = lax.broadcasted_iota(jnp.int32, (Q_TILE, WIN), 0) // CHUNK
    kc = lax.broadcasted_iota(jnp.int32, (Q_TILE, WIN), 1) // CHUNK - LEFT_CHUNKS
    return (kc <= r) & (kc >= r - LEFT_CHUNKS)


def _bias_tiles(rel_bias_pad, name):
    nh = rel_bias_pad.shape[0]

    def body(rb_ref, o_ref):
        dvec = _nn(rb_ref[...], _diag_onehot(), HIGHEST)[0:1, :]
        tile = pltpu.roll(jnp.broadcast_to(dvec, (Q_TILE, WIN)), 0, 1, stride=1, stride_axis=0)
        o_ref[...] = jnp.where(_band_mask(), tile, NEG_BIG)

    return pl.pallas_call(
        body, name=name, grid=(nh,),
        in_specs=[pl.BlockSpec((None, 8, REL_PAD), lambda h: (h, 0, 0))],
        out_specs=pl.BlockSpec((None, Q_TILE, WIN), lambda h: (h, 0, 0)),
        out_shape=jax.ShapeDtypeStruct((nh, Q_TILE, WIN), F32),
        compiler_params=_cparams("parallel"),
    )(rel_bias_pad)


def _bias_grad(dtile, name):
    nh = dtile.shape[0]

    def body(d_ref, o_ref):
        ri = lax.broadcasted_iota(jnp.int32, (Q_TILE, Q_TILE), 0)
        ci = lax.broadcasted_iota(jnp.int32, (Q_TILE, Q_TILE), 1)
        flip = jnp.where(ri + ci == Q_TILE - 1, 1.0, 0.0).astype(F32)
        rev = _nn(flip, d_ref[...], HIGHEST)
        rolled = pltpu.roll(rev, WIN - (Q_TILE - 1), 1, stride=1, stride_axis=0)
        diag = jnp.broadcast_to(jnp.sum(rolled, axis=0, keepdims=True), (8, WIN))
        o_ref[...] = _nt(diag, _diag_onehot(), HIGHEST)

    return pl.pallas_call(
        body, name=name, grid=(nh,),
        in_specs=[pl.BlockSpec((None, Q_TILE, WIN), lambda h: (h, 0, 0))],
        out_specs=pl.BlockSpec((None, 8, REL_PAD), lambda h: (h, 0, 0)),
        out_shape=jax.ShapeDtypeStruct((nh, 8, REL_PAD), F32),
        compiler_params=_cparams("parallel"),
    )(dtile)


def _attn_probs(q, k0, k1, bias, first_tile):
    s = jnp.concatenate([_nt(q, k0), _nt(q, k1)], axis=1) * (HEAD_DIM ** -0.5) + bias
    col = lax.broadcasted_iota(jnp.int32, (Q_TILE, WIN), 1)
    s = jnp.where(first_tile & (col < Q_TILE), NEG_BIG, s)
    p = jnp.exp(s - jnp.max(s, axis=-1, keepdims=True))
    return p * (1.0 / jnp.sum(p, axis=-1, keepdims=True))


def _attn_fwd(q, k, v, bias, name):
    t, inner = q.shape
    nh, nt = inner // HEAD_DIM, t // Q_TILE

    def body(q_ref, k0_ref, k1_ref, v0_ref, v1_ref, b_ref, o_ref):
        p = _bf(_attn_probs(q_ref[...], k0_ref[...], k1_ref[...], b_ref[...], pl.program_id(1) == 0))
        o_ref[...] = _nn(p[:, :Q_TILE], v0_ref[...]) + _nn(p[:, Q_TILE:], v1_ref[...])

    cur = pl.BlockSpec((Q_TILE, HEAD_DIM), lambda h, i: (i, h))
    prev = pl.BlockSpec((Q_TILE, HEAD_DIM), lambda h, i: (jnp.maximum(i - 1, 0), h))
    return pl.pallas_call(
        body, name=name, grid=(nh, nt),
        in_specs=[cur, prev, cur, prev, cur, pl.BlockSpec((None, Q_TILE, WIN), lambda h, i: (h, 0, 0))],
        out_specs=cur, out_shape=jax.ShapeDtypeStruct((t, inner), F32),
        compiler_params=_cparams("parallel", "parallel"),
    )(q, k, k, v, v, bias)


def _attn_bwd(q, k, v, do, bias, name):
    t, inner = q.shape
    nh, nt = inner // HEAD_DIM, t // Q_TILE
    scale = HEAD_DIM ** -0.5

    def body(q_ref, k0_ref, k1_ref, v0_ref, v1_ref, do_ref, b_ref, dq_ref, dk_ref, dv_ref, db_ref, ck_ref, cv_ref):
        i = pl.program_id(1)

        @pl.when(i == 0)
        def _():
            ck_ref[...] = jnp.zeros(blk, F32)
            cv_ref[...] = jnp.zeros(blk, F32)

        @pl.when(i < nt)
        def _():
            qv, dov = q_ref[...], do_ref[...]
            p = _attn_probs(qv, k0_ref[...], k1_ref[...], b_ref[...], i == 0)
            dp = jnp.concatenate([_nt(dov, v0_ref[...]), _nt(dov, v1_ref[...])], axis=1)
            ds = p * (dp - jnp.sum(p * dp, axis=-1, keepdims=True))

            @pl.when(i == 0)
            def _():
                db_ref[...] = ds

            @pl.when(i > 0)
            def _():
                db_ref[...] += ds

            pb, dsb = _bf(p), _bf(ds)
            dq_ref[...] = (_nn(dsb[:, :Q_TILE], k0_ref[...]) + _nn(dsb[:, Q_TILE:], k1_ref[...])) * scale
            dk_ref[...] = ck_ref[...] + _tn(dsb[:, :Q_TILE], qv) * scale
            dv_ref[...] = (cv_ref[...] + _tn(pb[:, :Q_TILE], dov)).astype(BF16)
            ck_ref[...] = _tn(dsb[:, Q_TILE:], qv) * scale
            cv_ref[...] = _tn(pb[:, Q_TILE:], dov)

        @pl.when(i == nt)
        def _():
            dk_ref[...] = ck_ref[...]
            dv_ref[...] = cv_ref[...].astype(BF16)

    blk = (Q_TILE, HEAD_DIM)
    cur = pl.BlockSpec(blk, lambda h, i: (jnp.minimum(i, nt - 1), h))
    prev = pl.BlockSpec(blk, lambda h, i: (jnp.clip(i - 1, 0, nt - 1), h))
    lag = pl.BlockSpec(blk, lambda h, i: (jnp.maximum(i - 1, 0), h))
    tile = pl.BlockSpec((None, Q_TILE, WIN), lambda h, i: (h, 0, 0))
    return pl.pallas_call(
        body, name=name, grid=(nh, nt + 1),
        in_specs=[cur, prev, cur, prev, cur, cur, tile],
        out_specs=[cur, lag, lag, tile],
        out_shape=[jax.ShapeDtypeStruct((t, inner), F32)] * 2 + [jax.ShapeDtypeStruct((t, inner), BF16),
                                                                 jax.ShapeDtypeStruct((nh, Q_TILE, WIN), F32)],
        scratch_shapes=[pltpu.VMEM(blk, F32), pltpu.VMEM(blk, F32)],
        compiler_params=_cparams("arbitrary", "arbitrary"),
    )(q, k, k, v, v, do, bias)


def _pad_rel_bias(rel_bias):
    nh = rel_bias.shape[0]
    return jnp.broadcast_to(jnp.pad(rel_bias, ((0, 0), (0, REL_PAD - N_REL)))[:, None, :], (nh, 8, REL_PAD))


def _layer_b_fwd(h1, nw, w_in, qw, kw, bias, w_out):
    t, d = h1.shape
    inner = w_out.shape[0]
    hn = _rms_fwd(h1, nw, "b_rms")
    proj = _mm(hn, w_in, "nn", t, 4 * inner, d, out_dtype=F32, name="b_proj")
    qn = _headnorm_fwd(proj, qw, 0, inner, "b_qnorm")
    kn = _headnorm_fwd(proj, kw, inner, inner, "b_knorm")
    vb = proj[:, 2 * inner:3 * inner].astype(BF16)
    o = _attn_fwd(qn, kn, vb, bias, "b_attn")
    g = _gate_fwd(o, proj, 3 * inner, inner, "b_gate")
    h2 = _mm(g, w_out, "nn", t, d, inner, out_dtype=F32, name="b_out", res=h1)
    return h2, (hn, proj, qn, kn, vb, o, g)


def _layer_b_bwd(dh2, dh2b, h1, nw, w_in, qw, kw, bias, w_out, saved):
    hn, proj, qn, kn, vb, o, g = saved
    t, d = h1.shape
    inner = w_out.shape[0]
    dg = _mm(dh2b, w_out, "nt", t, inner, d, out_dtype=F32, name="b_dgate")
    dw_out = _mm(g, dh2b, "tn", inner, d, t, out_dtype=F32, name="b_dwout")
    do, dz = _gate_bwd(dg, o, proj, 3 * inner, inner, "b_gate_bwd", BF16)
    dq, dk, dv, dtile = _attn_bwd(qn, kn, vb, do, bias, "b_attn_bwd")
    dqr, dqw = _headnorm_bwd(dq, proj, qw, 0, inner, "b_qnorm_bwd")
    dkr, dkw = _headnorm_bwd(dk, proj, kw, inner, inner, "b_knorm_bwd")
    dproj = jnp.concatenate([dqr, dkr, dv, dz], axis=1)
    dhn = _mm(dproj, w_in, "nt", t, d, 4 * inner, out_dtype=F32, name="b_dhn")
    dw_in = _mm(hn, dproj, "tn", d, 4 * inner, t, out_dtype=F32, name="b_dwin")
    dh1, dh1b, dnw = _rms_bwd(h1, nw, dhn, dh2, "b_rms_bwd")
    drb = _bias_grad(dtile, "b_bias_grad")[:, 0, :N_REL]
    return dh1, dh1b, dnw, dw_in, dqw, dkw, drb, dw_out


LANES = 128


def _softplus(x):
    return jnp.maximum(x, 0.0) + jnp.log1p(jnp.exp(-jnp.abs(x)))


def _gates_fwd(ab, alog_row, dt_row, nh, name, tr=1024):
    t = ab.shape[0]
    tr = min(tr, t)

    def body(x_ref, al_ref, dt_ref, o_ref):
        x = x_ref[...]
        lane = lax.broadcasted_iota(jnp.int32, x.shape, 1)
        g = -jnp.exp(al_ref[...]) * _softplus(x + dt_ref[...])
        o_ref[...] = jnp.where(lane < nh, g, jnp.where(lane < 2 * nh, _sigmoid(x), 0.0))

    row = pl.BlockSpec((tr, LANES), lambda i: (i, 0))
    vec = pl.BlockSpec((1, LANES), lambda i: (0, 0))
    return pl.pallas_call(
        body, name=name, grid=(t // tr,), in_specs=[row, vec, vec], out_specs=row,
        out_shape=jax.ShapeDtypeStruct((t, LANES), F32), compiler_params=_cparams("parallel"),
    )(ab, alog_row, dt_row)


def _gates_bwd(ab, alog_row, dt_row, dgates, nh, name, tr=1024):
    t = ab.shape[0]
    tr = min(tr, t)
    npart = dgates.shape[0]

    def body(x_ref, al_ref, dt_ref, dg_ref, dx_ref, s_ref):
        x = x_ref[...]
        lane = lax.broadcasted_iota(jnp.int32, x.shape, 1)
        dgt = dg_ref[0]
        for p in range(1, npart):
            dgt = dgt + dg_ref[p]
        ea = jnp.exp(al_ref[...])
        xa = x + dt_ref[...]
        da = jnp.where(lane < nh, dgt * (-ea) * _sigmoid(xa), 0.0)
        beta = _sigmoid(x)
        db = jnp.where((lane >= nh) & (lane < 2 * nh), dgt * beta * (1.0 - beta), 0.0)
        dx_ref[...] = (da + db).astype(BF16)
        dal = jnp.sum(jnp.where(lane < nh, dgt * (-ea) * _softplus(xa), 0.0), axis=0, keepdims=True)
        ddt = jnp.sum(da, axis=0, keepdims=True)
        r8 = lax.broadcasted_iota(jnp.int32, (8, LANES), 0)
        part = jnp.where(r8 == 0, dal, jnp.where(r8 == 1, ddt, 0.0))

        @pl.when(pl.program_id(0) == 0)
        def _():
            s_ref[...] = part

        @pl.when(pl.program_id(0) > 0)
        def _():
            s_ref[...] += part

    row = pl.BlockSpec((tr, LANES), lambda i: (i, 0))
    vec = pl.BlockSpec((1, LANES), lambda i: (0, 0))
    return pl.pallas_call(
        body, name=name, grid=(t // tr,),
        in_specs=[row, vec, vec, pl.BlockSpec((npart, tr, LANES), lambda i: (0, i, 0))],
        out_specs=[row, pl.BlockSpec((8, LANES), lambda i: (0, 0))],
        out_shape=[jax.ShapeDtypeStruct((t, LANES), BF16), jax.ShapeDtypeStruct((8, LANES), F32)],
        compiler_params=_cparams("arbitrary"),
    )(ab, alog_row, dt_row, dgates)


HALO = 8


def _conv_taps(ext, w, rows):
    acc = ext[HALO:HALO + rows] * w[CONV_K - 1:CONV_K]
    for s in range(1, CONV_K):
        acc = acc + pltpu.roll(ext, s, 0)[HALO:HALO + rows] * w[CONV_K - 1 - s:CONV_K - s]
    return acc


def _conv_fwd(proj, conv_w, col0, inner, mode, name, tt=512, hb=4):
    t = proj.shape[0]
    tt = min(tt, t)
    wc = hb * HEAD_DIM
    c0 = col0 // wc
    hpb = tt // HALO

    def body(x_ref, halo_ref, w_ref, o_ref):
        halo = jnp.where(pl.program_id(1) == 0, 0.0, halo_ref[...])
        s = _silu(_conv_taps(jnp.concatenate([halo, x_ref[...]], axis=0), w_ref[...], tt))
        if mode == "v":
            o_ref[...] = s
        else:
            mul = HEAD_DIM ** -0.5 if mode == "q" else 1.0
            o_ref[...] = jnp.concatenate(
                [sh * (lax.rsqrt(jnp.sum(sh * sh, axis=-1, keepdims=True) + EPS) * mul) for sh in _heads_of(s, hb)], axis=1)

    return pl.pallas_call(
        body, name=name, grid=(inner // wc, t // tt),
        in_specs=[pl.BlockSpec((tt, wc), lambda j, i: (i, j + c0)),
                  pl.BlockSpec((HALO, wc), lambda j, i: (jnp.maximum(i * hpb - 1, 0), j + c0)),
                  pl.BlockSpec((CONV_K, wc), lambda j, i: (0, j + c0))],
        out_specs=pl.BlockSpec((tt, wc), lambda j, i: (i, j)),
        out_shape=jax.ShapeDtypeStruct((t, inner), F32),
        compiler_params=_cparams("parallel", "parallel"),
    )(proj, proj, conv_w)


def _conv_bwd(dy, proj, conv_w, col0, inner, mode, name, tt=512, hb=4):
    t = proj.shape[0]
    tt = min(tt, t)
    nt = t // tt
    wc = hb * HEAD_DIM
    c0 = col0 // wc
    hpb = tt // HALO
    rows = tt + HALO

    def body(dy_ref, dyn_ref, x_ref, xp_ref, xn_ref, w_ref, dx_ref, dw_ref):
        i = pl.program_id(1)
        w = w_ref[...]
        xprev = jnp.where(i == 0, 0.0, xp_ref[...])
        ext = jnp.concatenate([xprev, x_ref[...], xn_ref[...]], axis=0)
        c = _conv_taps(ext, w, rows)
        dyv = jnp.concatenate([dy_ref[...], jnp.where(i == nt - 1, 0.0, dyn_ref[...])], axis=0)
        sg = _sigmoid(c)
        s = c * sg
        if mode == "v":
            ds = dyv
        else:
            mul = HEAD_DIM ** -0.5 if mode == "q" else 1.0
            parts = []
            for dyh, sh in zip(_heads_of(dyv, hb), _heads_of(s, hb)):
                r = lax.rsqrt(jnp.sum(sh * sh, axis=-1, keepdims=True) + EPS)
                parts.append(mul * (r * dyh - sh * (r * r * r) * jnp.sum(dyh * sh, axis=-1, keepdims=True)))
            ds = jnp.concatenate(parts, axis=1)
        dc = ds * (sg * (1.0 + c * (1.0 - sg)))
        dx = dc[:tt] * w[CONV_K - 1:CONV_K]
        for sft in range(1, CONV_K):
            dx = dx + pltpu.roll(dc, rows - sft, 0)[:tt] * w[CONV_K - 1 - sft:CONV_K - sft]
        dx_ref[...] = dx.astype(BF16)
        r8 = lax.broadcasted_iota(jnp.int32, (8, wc), 0)
        part = jnp.zeros((8, wc), F32)
        for sft in range(CONV_K):
            xs = ext[HALO:HALO + tt] if sft == 0 else pltpu.roll(ext, sft, 0)[HALO:HALO + tt]
            part = part + jnp.where(r8 == CONV_K - 1 - sft, jnp.sum(dc[:tt] * xs, axis=0, keepdims=True), 0.0)

        @pl.when(i == 0)
        def _():
            dw_ref[...] = part

        @pl.when(i > 0)
        def _():
            dw_ref[...] += part

    cur = lambda off: pl.BlockSpec((tt, wc), lambda j, i: (i, j + off))
    nxt = lambda off: pl.BlockSpec((HALO, wc), lambda j, i: (jnp.minimum((i + 1) * hpb, t // HALO - 1), j + off))
    return pl.pallas_call(
        body, name=name, grid=(inner // wc, nt),
        in_specs=[cur(0), nxt(0), cur(c0),
                  pl.BlockSpec((HALO, wc), lambda j, i: (jnp.maximum(i * hpb - 1, 0), j + c0)), nxt(c0),
                  pl.BlockSpec((CONV_K, wc), lambda j, i: (0, j + c0))],
        out_specs=[pl.BlockSpec((tt, wc), lambda j, i: (i, j)), pl.BlockSpec((8, wc), lambda j, i: (0, j))],
        out_shape=[jax.ShapeDtypeStruct((t, inner), BF16), jax.ShapeDtypeStruct((8, inner), F32)],
        compiler_params=_cparams("parallel", "arbitrary"),
    )(dy, dy, proj, proj, proj, conv_w)


GDN_HB = 4
GDN_NB = 2


def _iota2(n, m):
    return lax.broadcasted_iota(jnp.int32, (n, m), 0), lax.broadcasted_iota(jnp.int32, (n, m), 1)


def _head_select(first_head, hb, lane0):
    r, lane = _iota2(8, LANES)
    return jnp.where((r < hb) & (lane == lane0 + first_head + r), 1.0, 0.0).astype(F32)


def _chunk_gates(gt, selg, selb):
    i, j = _iota2(CHUNK, CHUNK)
    gc_all = _dot_exact(jnp.where(j <= i, 1.0, 0.0), gt, NN, True)
    return (_dot_exact(gc_all, selg, NT, False), _dot_exact(selg, gc_all, NT, True),
            _dot_exact(gt, selb, NT, False))


def _decay_terms(gcol, grow):
    i, j = _iota2(CHUNK, CHUNK)
    glast = gcol[CHUNK - 1:CHUNK]
    decay = jnp.exp(jnp.where(j <= i, gcol - grow, NEG_BIG))
    return jnp.exp(gcol), jnp.exp(glast - gcol), jnp.exp(glast), decay


def _unit_lower_inverse(a):
    i, j = _iota2(CHUNK, CHUNK)
    same16 = (i // 16) == (j // 16)
    same32 = (i // 32) == (j // 32)
    m = jnp.where(same16, -a, 0.0)
    x = jnp.where(i == j, 1.0, 0.0) + m
    for _ in range(3):
        m = _dot3(m, m, NN)
        x = x + _dot3(x, m, NN)
    for off in (jnp.where(same32 & jnp.logical_not(same16), a, 0.0), jnp.where(same32, 0.0, a)):
        x = x - _dot3(_dot3(x, off, NN), x, NN)
    return x


def _gdn_specs(nh, inner, t):
    hb, nb = min(GDN_HB, nh), GDN_NB
    rows = nb * CHUNK
    wide = pl.BlockSpec((rows, hb * HEAD_DIM), lambda g, n: (n, g))
    sq = pl.BlockSpec((hb, rows, CHUNK), lambda g, n: (g, n, 0))
    gts = pl.BlockSpec((rows, LANES), lambda g, n: (n, 0))
    glb = pl.BlockSpec((nb * 8, hb * HEAD_DIM), lambda g, n: (n, g))
    return hb, nb, rows, wide, sq, gts, glb


def _gdn_intra_fwd(q, k, v, gates, nh, name, comm=None):
    t, inner = q.shape
    hb, nb, rows, wide, sq, gts, glb = _gdn_specs(nh, inner, t)

    def body(q_ref, k_ref, v_ref, g_ref, qe_ref, kel_ref, wb_ref, w_ref, u_ref, qk_ref, tm_ref, gl_ref):
        first = pl.program_id(0) * hb
        selg, selb = _head_select(first, hb, 0), _head_select(first, hb, nh)
        i, j = _iota2(CHUNK, CHUNK)
        for c in range(nb):
            rs = slice(c * CHUNK, (c + 1) * CHUNK)
            gcols, grows, bcols = _chunk_gates(g_ref[rs, :], selg, selb)
            for h in range(hb):
                cs = slice(h * HEAD_DIM, (h + 1) * HEAD_DIM)
                qv, kv, vv = q_ref[rs, cs], k_ref[rs, cs], v_ref[rs, cs]
                bcol = bcols[:, h:h + 1]
                e, el, gl, decay = _decay_terms(gcols[:, h:h + 1], grows[h:h + 1, :])
                kb = kv * bcol
                a = jnp.where(j < i, _nt(_bf(kb), _bf(kv)) * decay, 0.0)
                tm = _unit_lower_inverse(a)
                uw = _dot3(tm, jnp.concatenate([vv * bcol, kb * e], axis=1), NN)
                w = uw[:, HEAD_DIM:]
                qe_ref[rs, cs] = _bf(qv * e)
                kel_ref[rs, cs] = _bf(kv * el)
                wb_ref[rs, cs] = _bf(w)
                w_ref[rs, cs] = w
                u_ref[rs, cs] = uw[:, :HEAD_DIM]
                qk_ref[h, rs, :] = _bf(_nt(_bf(qv), _bf(kv)) * decay)
                tm_ref[h, rs, :] = tm
                gl_ref[c * 8:(c + 1) * 8, cs] = jnp.broadcast_to(gl, (8, HEAD_DIM))

    big = lambda dt: jax.ShapeDtypeStruct((t, inner), dt)
    return _grid_call(
        body, name=name, grid=(nh // hb, t // rows),
        in_specs=[wide, wide, wide, gts],
        out_specs=[wide] * 5 + [sq, sq, glb],
        out_shape=[big(BF16), big(BF16), big(BF16), big(F32), big(F32),
                   jax.ShapeDtypeStruct((nh, t, CHUNK), BF16), jax.ShapeDtypeStruct((nh, t, CHUNK), F32),
                   jax.ShapeDtypeStruct((t // CHUNK * 8, inner), F32)],
        args=(q, k, v, gates), semantics=("parallel", "parallel"), comm=comm)


def _gdn_scan_fwd(qe, kel, wb, u, qk, glb, nh, name):
    t, inner = u.shape
    hb, nb, rows, wide, sq, _, glb_spec = _gdn_specs(nh, inner, t)

    def body(qe_ref, kel_ref, wb_ref, u_ref, qk_ref, gl_ref, o_ref, vn_ref, sall_ref, s_ref):
        @pl.when(pl.program_id(1) == 0)
        def _():
            s_ref[...] = jnp.zeros(s_ref.shape, F32)

        for c in range(nb):
            rs = slice(c * CHUNK, (c + 1) * CHUNK)
            for h in range(hb):
                cs = slice(h * HEAD_DIM, (h + 1) * HEAD_DIM)
                s = s_ref[h]
                sall_ref[c, h] = s
                sb = _bf(s)
                vn = u_ref[rs, cs] - _nn(wb_ref[rs, cs], sb)
                vn_ref[rs, cs] = vn
                vnb = _bf(vn)
                o_ref[rs, cs] = _nn(qe_ref[rs, cs], sb) + _nn(qk_ref[h, rs, :], vnb)
                s_ref[h] = s * gl_ref[c * 8:c * 8 + 1, cs] + _tn(kel_ref[rs, cs], vnb)

    return pl.pallas_call(
        body, name=name, grid=(nh // hb, t // rows),
        in_specs=[wide, wide, wide, wide, sq, glb_spec],
        out_specs=[wide, wide, pl.BlockSpec((nb, hb, HEAD_DIM, HEAD_DIM), lambda g, n: (n, g, 0, 0))],
        out_shape=[jax.ShapeDtypeStruct((t, inner), F32), jax.ShapeDtypeStruct((t, inner), F32),
                   jax.ShapeDtypeStruct((t // CHUNK, nh, HEAD_DIM, HEAD_DIM), F32)],
        scratch_shapes=[pltpu.VMEM((hb, HEAD_DIM, HEAD_DIM), F32)],
        compiler_params=_cparams("parallel", "arbitrary"),
    )(qe, kel, wb, u, qk, glb)


def _gdn_scan_bwd(do, qe, kel, wb, vn, qk, glb, sall, nh, name):
    t, inner = do.shape
    hb, nb, rows, _, _, _, _ = _gdn_specs(nh, inner, t)
    last = t // rows - 1
    wide = pl.BlockSpec((rows, hb * HEAD_DIM), lambda g, n: (last - n, g))
    sq = pl.BlockSpec((hb, rows, CHUNK), lambda g, n: (g, last - n, 0))
    glb_spec = pl.BlockSpec((nb * 8, hb * HEAD_DIM), lambda g, n: (last - n, g))

    def body(do_ref, qe_ref, kel_ref, wb_ref, vn_ref, qk_ref, gl_ref, sall_ref,
             dvn_ref, dw_ref, dqe_ref, dkel_ref, dqk_ref, dgl_ref, ds_ref):
        @pl.when(pl.program_id(1) == 0)
        def _():
            ds_ref[...] = jnp.zeros(ds_ref.shape, F32)

        for c in reversed(range(nb)):
            rs = slice(c * CHUNK, (c + 1) * CHUNK)
            for h in range(hb):
                cs = slice(h * HEAD_DIM, (h + 1) * HEAD_DIM)
                ds, s = ds_ref[h], sall_ref[c, h]
                dsb, sb = _bf(ds), _bf(s)
                dob, vnb = _bf(do_ref[rs, cs]), _bf(vn_ref[rs, cs])
                kel, qe, wb = kel_ref[rs, cs], qe_ref[rs, cs], wb_ref[rs, cs]
                dvn = _tn(qk_ref[h, rs, :], dob) + _nn(kel, dsb)
                dvnb = _bf(dvn)
                dvn_ref[rs, cs] = dvn
                dw_ref[rs, cs] = -_nt(dvnb, sb)
                dqe_ref[rs, cs] = _nt(dob, sb)
                dkel_ref[rs, cs] = _nt(vnb, dsb)
                dqk_ref[h, rs, :] = _nt(dob, vnb)
                dgl_ref[c * 8:(c + 1) * 8, cs] = jnp.zeros((8, HEAD_DIM), F32) + jnp.sum(ds * s)
                ds_ref[h] = ds * gl_ref[c * 8:c * 8 + 1, cs] + _tn(qe, dob) - _tn(wb, dvnb)

    big = jax.ShapeDtypeStruct((t, inner), F32)
    return pl.pallas_call(
        body, name=name, grid=(nh // hb, t // rows),
        in_specs=[wide, wide, wide, wide, wide, sq, glb_spec,
                  pl.BlockSpec((nb, hb, HEAD_DIM, HEAD_DIM), lambda g, n: (last - n, g, 0, 0))],
        out_specs=[wide] * 4 + [sq, glb_spec],
        out_shape=[big] * 4 + [jax.ShapeDtypeStruct((nh, t, CHUNK), F32),
                               jax.ShapeDtypeStruct((t // CHUNK * 8, inner), F32)],
        scratch_shapes=[pltpu.VMEM((hb, HEAD_DIM, HEAD_DIM), F32)],
        compiler_params=_cparams("parallel", "arbitrary"),
    )(do, qe, kel, wb, vn, qk, glb, sall)


def _gdn_intra_bwd(q, k, v, gates, tm, w, u, dvn, dw, dqe, dkel, dqk, dglb, nh, name, comm=None):
    t, inner = q.shape
    hb, nb, rows, wide, sq, gts, glb = _gdn_specs(nh, inner, t)

    def body(q_ref, k_ref, v_ref, g_ref, tm_ref, w_ref, u_ref, dvn_ref, dw_ref, dqe_ref, dkel_ref, dqk_ref,
             dgl_ref, dq_ref, dk_ref, dv_ref, dg_ref):
        first = pl.program_id(0) * hb
        selg, selb = _head_select(first, hb, 0), _head_select(first, hb, nh)
        i, j = _iota2(CHUNK, CHUNK)
        lane8 = lax.broadcasted_iota(jnp.int32, (CHUNK, 8), 1)
        row8 = lax.broadcasted_iota(jnp.int32, (CHUNK, 8), 0)
        ones = jnp.ones((CHUNK, LANES), F32)
        lower = jnp.where(j <= i, 1.0, 0.0).astype(F32)
        rsum = lambda x: jnp.sum(x, axis=-1, keepdims=True)
        for c in range(nb):
            rs = slice(c * CHUNK, (c + 1) * CHUNK)
            gcols, grows, bcols = _chunk_gates(g_ref[rs, :], selg, selb)
            dgc_cols = jnp.zeros((CHUNK, 8), F32)
            dbeta_cols = jnp.zeros((CHUNK, 8), F32)
            for h in range(hb):
                cs = slice(h * HEAD_DIM, (h + 1) * HEAD_DIM)
                qv, kv, vv = q_ref[rs, cs], k_ref[rs, cs], v_ref[rs, cs]
                bcol = bcols[:, h:h + 1]
                e, el, gl, decay = _decay_terms(gcols[:, h:h + 1], grows[h:h + 1, :])
                kb = kv * bcol
                tmv, wv, uv = tm_ref[h, rs, :], w_ref[rs, cs], u_ref[rs, cs]
                dqe, dkel = dqe_ref[rs, cs], dkel_ref[rs, cs]
                qb, kbf, kbb = _bf(qv), _bf(kv), _bf(kb)
                dqk = jnp.where(j <= i, dqk_ref[h, rs, :], 0.0)
                dqkr = _bf(dqk * decay)
                dq = dqe * e + _nn(dqkr, kbf)
                dk = dkel * el + _tn(dqkr, qb)
                de = rsum(dqe * qv)
                del_ = rsum(dkel * kv)
                mq = dqk * _nt(qb, kbf) * decay
                dsol = _dot3(tmv, jnp.concatenate([dvn_ref[rs, cs], dw_ref[rs, cs]], axis=1), TN)
                dvb, dkbe = dsol[:, :HEAD_DIM], dsol[:, HEAD_DIM:]
                da = -jnp.where(j < i, _dot3(dsol, jnp.concatenate([uv, wv], axis=1), NT), 0.0)
                dkk = _bf(da * decay)
                ma = da * _nt(kbb, kbf) * decay
                dkb = dkbe * e + _nn(dkk, kbf)
                de = de + rsum(dkbe * kb)
                dk = dk + _tn(dkk, kbb) + dkb * bcol
                dq_ref[rs, cs] = dq
                dk_ref[rs, cs] = dk
                dv_ref[rs, cs] = dvb * bcol
                dbeta = rsum(dkb * kv) + rsum(dvb * vv)
                m = mq + ma
                dgc = rsum(m) - _dot_exact(m, ones, TN, False)[:, 0:1] + de * e - del_ * el
                tail = jnp.sum(del_ * el) + dgl_ref[c * 8:c * 8 + 1, h * HEAD_DIM:h * HEAD_DIM + 1] * gl
                dgc_cols = jnp.where(lane8 == h, dgc + jnp.where(row8 == CHUNK - 1, tail, 0.0), dgc_cols)
                dbeta_cols = jnp.where(lane8 == h, dbeta, dbeta_cols)
            dg_cols = _dot_exact(lower, dgc_cols, TN, True)
            dg_ref[rs, :] = _dot_exact(dg_cols, selg, NN, False) + _dot_exact(dbeta_cols, selb, NN, False)

    big = jax.ShapeDtypeStruct((t, inner), F32)
    return _grid_call(
        body, name=name, grid=(nh // hb, t // rows),
        in_specs=[wide, wide, wide, gts, sq, wide, wide, wide, wide, wide, wide, sq, glb],
        out_specs=[wide, wide, wide, pl.BlockSpec((None, rows, LANES), lambda g, n: (g, n, 0))],
        out_shape=[big, big, big, jax.ShapeDtypeStruct((nh // hb, t, LANES), F32)],
        args=(q, k, v, gates, tm, w, u, dvn, dw, dqe, dkel, dqk, dglb), semantics=("parallel", "parallel"),
        comm=comm)


def _layer_a_fwd(x, nw, w_main, w_ab, conv_w, alog_row, dt_row, onw, w_out, nh, comm=None):
    t, d = x.shape
    inner = w_out.shape[0]
    hn = _rms_fwd(x, nw, "a_rms")
    proj = _mm(hn, w_main, "nn", t, 4 * inner, d, out_dtype=F32, name="a_proj")
    ab = _mm(hn, w_ab, "nn", t, LANES, d, out_dtype=F32, name="a_proj_ab")
    gates = _gates_fwd(ab, alog_row, dt_row, nh, "a_gates")
    q = _conv_fwd(proj, conv_w, 0, inner, "q", "a_conv_q")
    k = _conv_fwd(proj, conv_w, inner, inner, "k", "a_conv_k")
    v = _conv_fwd(proj, conv_w, 2 * inner, inner, "v", "a_conv_v")
    qe, kel, wb, w, u, qk, tm, glb, *carried = _gdn_intra_fwd(q, k, v, gates, nh, "a_intra", comm)
    o, vn, sall = _gdn_scan_fwd(qe, kel, wb, u, qk, glb, nh, "a_scan")
    g = _gate_fwd(o, proj, 3 * inner, inner, "a_gate", norm_w=onw)
    h1 = _mm(g, w_out, "nn", t, d, inner, out_dtype=F32, name="a_out", res=x)
    return h1, (hn, proj, ab, gates, q, k, v, qe, kel, wb, w, u, qk, tm, glb, o, vn, sall, g), carried


def _layer_a_bwd(dh1, dh1b, x, nw, w_main, w_ab, conv_w, alog_row, dt_row, onw, w_out, nh, saved, comm=None):
    hn, proj, ab, gates, q, k, v, qe, kel, wb, w, u, qk, tm, glb, o, vn, sall, g = saved
    t, d = x.shape
    inner = w_out.shape[0]
    dg = _mm(dh1b, w_out, "nt", t, inner, d, out_dtype=F32, name="a_dgate")
    dw_out = _mm(g, dh1b, "tn", inner, d, t, out_dtype=F32, name="a_dwout")
    do, dz, donw = _gate_bwd(dg, o, proj, 3 * inner, inner, "a_gate_bwd", F32, norm_w=onw)
    dvn, dw, dqe, dkel, dqk, dglb = _gdn_scan_bwd(do, qe, kel, wb, vn, qk, glb, sall, nh, "a_scan_bwd")
    dq, dk, dv, dgates, *carried = _gdn_intra_bwd(q, k, v, gates, tm, w, u, dvn, dw, dqe, dkel, dqk, dglb, nh,
                                                  "a_intra_bwd", comm)
    dxq, dcq = _conv_bwd(dq, proj, conv_w, 0, inner, "q", "a_conv_q_bwd")
    dxk, dck = _conv_bwd(dk, proj, conv_w, inner, inner, "k", "a_conv_k_bwd")
    dxv, dcv = _conv_bwd(dv, proj, conv_w, 2 * inner, inner, "v", "a_conv_v_bwd")
    dab, dsmall = _gates_bwd(ab, alog_row, dt_row, dgates, nh, "a_gates_bwd")
    dproj = jnp.concatenate([dxq, dxk, dxv, dz], axis=1)
    dhn = _mm(dab, w_ab, "nt", t, d, LANES, out_dtype=F32, name="a_dhn_ab")
    dhn = _mm(dproj, w_main, "nt", t, d, 4 * inner, out_dtype=F32, name="a_dhn", res=dhn)
    dw_main = _mm(hn, dproj, "tn", d, 4 * inner, t, out_dtype=F32, name="a_dwin")
    dw_ab = _mm(hn, dab, "tn", d, LANES, t, out_dtype=F32, name="a_dwin_ab")
    dx, _, dnw = _rms_bwd(x, nw, dhn, dh1, "a_rms_bwd")
    dconv = jnp.concatenate([dcq[:CONV_K], dck[:CONV_K], dcv[:CONV_K]], axis=1)
    return dx, dnw, dw_main, dw_ab, dconv, dsmall, donw, dw_out, carried


def _rows_of(a, rows):
    flat = a.reshape(-1)
    return jnp.pad(flat, (0, rows * LANES - flat.shape[0])).reshape(rows, LANES)


def _to_slabs(g, axis):
    shape = g.shape[:axis] + (N_DEV, g.shape[axis] // N_DEV) + g.shape[axis + 1:]
    return jnp.moveaxis(g.reshape(shape), axis, 0)


def _from_slabs(s, axis):
    m = jnp.moveaxis(s, 0, axis)
    return m.reshape(m.shape[:axis] + (m.shape[axis] * m.shape[axis + 1],) + m.shape[axis + 2:])


def kernel(x, norm_w, a_w_in, a_conv_w, a_a_log, a_dt_bias, a_out_norm_w, a_w_out, b_w_in, b_q_norm_w, b_k_norm_w, b_rel_bias, b_w_out, loss_target, m_norm_w, m_a_w_in, m_a_conv_w, m_a_a_log, m_a_dt_bias, m_a_out_norm_w, m_a_w_out, m_b_w_in, m_b_q_norm_w, m_b_k_norm_w, m_b_rel_bias, m_b_w_out, v_norm_w, v_a_w_in, v_a_conv_w, v_a_a_log, v_a_dt_bias, v_a_out_norm_w, v_a_w_out, v_b_w_in, v_b_q_norm_w, v_b_k_norm_w, v_b_rel_bias, v_b_w_out):
    xs, target = x[0], loss_target[0]
    nh = a_a_log.shape[-1]
    inner = N_DEV * a_w_out.shape[1]

    ga_in, ga_out, g_conv = _comm_call(
        _Comm("gather", [a_w_in[0].astype(BF16), a_w_out[0].astype(BF16), a_conv_w[0]]), "gather_a_weights")
    wa_in = _from_slabs(ga_in, 1)
    wa_main = wa_in[:, :4 * inner]
    wa_ab = jnp.pad(wa_in[:, 4 * inner:], ((0, 0), (0, LANES - 2 * nh)))
    wa_out = _from_slabs(ga_out, 0)
    conv_w = _from_slabs(g_conv, 1)
    nw0, nw1 = norm_w[0:1], norm_w[1:2]
    alog_row = jnp.pad(a_a_log, ((0, 0), (0, LANES - nh)))
    dt_row = jnp.pad(a_dt_bias, ((0, 0), (0, LANES - nh)))

    h1, saved_a, (gb_in, gb_out) = _layer_a_fwd(
        xs, nw0, wa_main, wa_ab, conv_w, alog_row, dt_row, a_out_norm_w, wa_out, nh,
        _Comm("gather", [b_w_in[0].astype(BF16), b_w_out[0].astype(BF16)]))
    wb_in = _from_slabs(gb_in, 1)
    wb_out = _from_slabs(gb_out, 0)
    bias = _bias_tiles(_pad_rel_bias(b_rel_bias[0]), "b_bias_tiles")
    h2, saved_b = _layer_b_fwd(h1, nw1, wb_in, b_q_norm_w, b_k_norm_w, bias, wb_out)
    dh2, dh2b, loss_row = _loss_grad(h2, target, "loss")

    dh1, dh1b, dnw1, dwb_in, dqw, dkw, drb, dwb_out = _layer_b_bwd(
        dh2, dh2b, h1, nw1, wb_in, b_q_norm_w, b_k_norm_w, bias, wb_out, saved_b)
    dx, dnw0, dwa_main, dwa_ab, dconv, dsmall, donw, dwa_out, (pb_in, pb_out) = _layer_a_bwd(
        dh1, dh1b, xs, nw0, wa_main, wa_ab, conv_w, alog_row, dt_row, a_out_norm_w, wa_out, nh, saved_a,
        _Comm("exchange", [_to_slabs(dwb_in, 1).astype(BF16), _to_slabs(dwb_out, 0).astype(BF16)]))
    dwa_in = jnp.concatenate([dwa_main, dwa_ab[:, :2 * nh]], axis=1)
    pa_in, pa_out, p_conv = _comm_call(
        _Comm("exchange", [_to_slabs(dwa_in, 1).astype(BF16), _to_slabs(dwa_out, 0).astype(BF16),
                           _to_slabs(dconv, 1)]), "exchange_a_grads")
    big = {}
    for name, p, w, m, v in (("a_w_in", pa_in, a_w_in, m_a_w_in, v_a_w_in),
                             ("a_w_out", pa_out, a_w_out, m_a_w_out, v_a_w_out),
                             ("b_w_in", pb_in, b_w_in, m_b_w_in, v_b_w_in),
                             ("b_w_out", pb_out, b_w_out, m_b_w_out, v_b_w_out),
                             ("a_conv_w", p_conv, a_conv_w, m_a_conv_w, v_a_conv_w)):
        big[name] = [o[None] for o in _adamw(p, w[0], m[0], v[0], "adamw_" + name)]

    small = (("norm_w", norm_w, m_norm_w, v_norm_w, jnp.concatenate([dnw0, dnw1], axis=0)),
             ("a_a_log", a_a_log, m_a_a_log, v_a_a_log, dsmall[0:1, :nh]),
             ("a_dt_bias", a_dt_bias, m_a_dt_bias, v_a_dt_bias, dsmall[1:2, :nh]),
             ("a_out_norm_w", a_out_norm_w, m_a_out_norm_w, v_a_out_norm_w, donw),
             ("b_q_norm_w", b_q_norm_w, m_b_q_norm_w, v_b_q_norm_w, dqw),
             ("b_k_norm_w", b_k_norm_w, m_b_k_norm_w, v_b_k_norm_w, dkw),
             ("b_rel_bias", b_rel_bias, m_b_rel_bias, v_b_rel_bias, drb))
    rows = [8 * (-(-w.size // (8 * LANES))) for _, w, _, _, _ in small]
    pack = lambda arrs: jnp.concatenate([_rows_of(a, r) for a, r in zip(arrs, rows)] + [jnp.zeros((8, LANES), F32)], axis=0)
    g_pack = jnp.concatenate([_rows_of(g, r) for (_, _, _, _, g), r in zip(small, rows)]
                             + [jnp.broadcast_to(loss_row, (8, LANES))], axis=0)
    (g_all,) = _comm_call(_Comm("gather", [g_pack]), "gather_small_grads")
    outs_small = _adamw(g_all, pack([s[1] for s in small]), pack([s[2] for s in small]),
                        pack([s[3] for s in small]), "adamw_small")
    start = 0
    for (name, w, _, _, _), r in zip(small, rows):
        big[name] = [o[start:start + r].reshape(-1)[:w.size].reshape(w.shape) for o in outs_small]
        start += r
    loss = outs_small[0][start, 0]

    order = ("norm_w", "a_w_in", "a_conv_w", "a_a_log", "a_dt_bias", "a_out_norm_w", "a_w_out", "b_w_in",
             "b_q_norm_w", "b_k_norm_w", "b_rel_bias", "b_w_out")
    return (loss, dx[None]) + tuple(big[n][i] for i in range(4) for n in order)
```

```python
import functools
import math

import jax
import jax.numpy as jnp
from jax import lax
from jax.experimental import pallas as pl
from jax.experimental.pallas import tpu as pltpu

F32 = jnp.float32
BF16 = jnp.bfloat16
MESH_IDS = pl.DeviceIdType.MESH
N_DEV = 8
CHUNK = 64
HEAD_DIM = 128
EPS = 1e-6
CONV_K = 4
LEFT_CHUNKS = 8
REL_CLIP = 256
Q_TILE = LEFT_CHUNKS * CHUNK
ADAM_LR = 0.001
ADAM_B1 = 0.9
ADAM_B2 = 0.999
ADAM_EPS = 1e-08
ADAM_WD = 0.01
ADAM_STEP = 10
NEG_BIG = -1e30
VMEM_LIMIT_BYTES = 56 * 1024 * 1024
HIGHEST = lax.Precision.HIGHEST
ANY = pl.BlockSpec(memory_space=pl.ANY)


def _cparams(*sem):
    return pltpu.CompilerParams(dimension_semantics=tuple(sem), vmem_limit_bytes=VMEM_LIMIT_BYTES)


NN, NT, TN = (((1,), (0,)), ((), ())), (((1,), (1,)), ((), ())), (((0,), (0,)), ((), ()))
BNN, BNT, BTN = (((2,), (1,)), ((0,), (0,))), (((2,), (2,)), ((0,), (0,))), (((1,), (1,)), ((0,), (0,)))


def _dot(a, b, dims, precision=None):
    return lax.dot_general(a, b, dims, preferred_element_type=F32, precision=precision)


def _nn(a, b, precision=None):
    return _dot(a, b, NN, precision)


def _nt(a, b, precision=None):
    return _dot(a, b, NT, precision)


def _tn(a, b, precision=None):
    return _dot(a, b, TN, precision)


def _bf(x):
    return x.astype(BF16)


def _split(x, pieces=2):
    out = []
    for _ in range(pieces - 1):
        hi = x.astype(BF16)
        out.append(hi)
        x = x - hi.astype(F32)
    return out + [x.astype(BF16)]


def _dot3(a, b, dims):
    (ah, al), (bh, bl) = _split(a), _split(b)
    return _dot(ah, bh, dims) + (_dot(ah, bl, dims) + _dot(al, bh, dims))


def _dot_exact(a, b, dims, split_b):
    if split_b:
        a = a.astype(BF16)
        parts = [_dot(a, p, dims) for p in _split(b, 3)]
    else:
        b = b.astype(BF16)
        parts = [_dot(p, b, dims) for p in _split(a, 3)]
    return parts[0] + (parts[1] + parts[2])


def _sigmoid(x):
    return 1.0 / (1.0 + jnp.exp(-x))


def _silu(x):
    return x * _sigmoid(x)


def _dsilu(x):
    s = _sigmoid(x)
    return s * (1.0 + x * (1.0 - s))


def _my_pos():
    return lax.axis_index("x"), lax.axis_index("y"), lax.axis_index("c")


def _peers(x, y, c):
    def flip(v, f):
        return 1 - v if f else v

    return [(flip(x, kx), flip(y, ky), flip(c, kc)) for kx in (0, 1) for ky in (0, 1) for kc in (0, 1)][1:]


def _lin(p):
    return 4 * p[0] + 2 * p[1] + p[2]


class _Comm:
    def __init__(self, kind, arrays):
        self.kind, self.arrays, self.n = kind, list(arrays), len(arrays)

    def out_shape(self):
        lead = (N_DEV,) if self.kind == "gather" else ()
        return [jax.ShapeDtypeStruct(lead + a.shape, a.dtype) for a in self.arrays]

    def scratch(self):
        return [pltpu.SemaphoreType.DMA((7 * self.n,)), pltpu.SemaphoreType.DMA((7 * self.n,)),
                pltpu.SemaphoreType.DMA((self.n,))]

    def _copies(self, ins, outs, sems, arrivals):
        send_sems, recv_sems, local_sems = sems
        x, y, c = _my_pos()
        me = _lin((x, y, c))
        gather = self.kind == "gather"
        mine = [ins[t] if gather else ins[t].at[me] for t in range(self.n)]
        remote = []
        for k, peer in enumerate(_peers(x, y, c)):
            for t in range(self.n):
                if arrivals:
                    src, dst = mine[t], outs[t].at[_lin(peer)]
                else:
                    src, dst = (ins[t] if gather else ins[t].at[_lin(peer)]), outs[t].at[me]
                remote.append(pltpu.make_async_remote_copy(
                    src_ref=src, dst_ref=dst, send_sem=send_sems.at[k * self.n + t],
                    recv_sem=recv_sems.at[k * self.n + t], device_id=peer, device_id_type=MESH_IDS))
        if arrivals:
            return remote
        return [pltpu.make_async_copy(mine[t], outs[t].at[me], local_sems.at[t]) for t in range(self.n)], remote

    def start(self, ins, outs, sems):
        local, sends = self._copies(ins, outs, sems, False)
        for cp in local + sends:
            cp.start()

    def finish(self, ins, outs, sems):
        for cp in self._copies(ins, outs, sems, True):
            cp.wait_recv()
        local, sends = self._copies(ins, outs, sems, False)
        for cp in sends:
            cp.wait_send()
        for cp in local:
            cp.wait()


def _comm_call(comm, name):
    n = comm.n

    def body(*refs):
        ins, outs, sems = refs[:n], refs[n:2 * n], refs[2 * n:]
        comm.start(ins, outs, sems)
        comm.finish(ins, outs, sems)

    return pl.pallas_call(
        body, name=name, out_shape=comm.out_shape(), in_specs=[ANY] * n, out_specs=[ANY] * n,
        scratch_shapes=comm.scratch(),
    )(*comm.arrays)


def _grid_call(body, *, name, grid, in_specs, out_specs, out_shape, args, scratch_shapes=(), semantics=None, comm=None):
    if comm is None:
        return pl.pallas_call(
            body, name=name, grid=grid, in_specs=in_specs, out_specs=out_specs, out_shape=out_shape,
            scratch_shapes=list(scratch_shapes), compiler_params=_cparams(*semantics),
        )(*args)
    n_in, n_out, n_sc, n = len(in_specs), len(out_specs), len(scratch_shapes), comm.n

    def full(*refs):
        ins, refs = refs[:n_in], refs[n_in:]
        cins, refs = refs[:n], refs[n:]
        outs, refs = refs[:n_out], refs[n_out:]
        couts, refs = refs[:n], refs[n:]
        scratch, sems = refs[:n_sc], refs[n_sc:]
        ids = [pl.program_id(a) for a in range(len(grid))]
        first = functools.reduce(jnp.logical_and, [i == 0 for i in ids])
        last = functools.reduce(jnp.logical_and, [i == g - 1 for i, g in zip(ids, grid)])

        @pl.when(first)
        def _():
            comm.start(cins, couts, sems)

        body(*ins, *outs, *scratch)

        @pl.when(last)
        def _():
            comm.finish(cins, couts, sems)

    return pl.pallas_call(
        full, name=name, grid=grid, in_specs=list(in_specs) + [ANY] * n, out_specs=list(out_specs) + [ANY] * n,
        out_shape=list(out_shape) + comm.out_shape(), scratch_shapes=list(scratch_shapes) + comm.scratch(),
        compiler_params=_cparams(*(["arbitrary"] * len(grid))),
    )(*(list(args) + comm.arrays))


def _mm(a, b, mode, m, n, k, *, out_dtype, name, tm=1024, tn=1024, tk=2048,
        a_m0=0, a_k0=0, b_n0=0, b_k0=0, res=None, comm=None):
    tm, tn, tk = min(tm, m), min(tn, n), min(tk, k)
    nm, nn, nk = m // tm, n // tn, k // tk
    assert nm * tm == m and nn * tn == n and nk * tk == k
    am, ak, bn, bk = a_m0 // tm, a_k0 // tk, b_n0 // tn, b_k0 // tk
    assert am * tm == a_m0 and ak * tk == a_k0 and bn * tn == b_n0 and bk * tk == b_k0
    if mode == "tn":
        a_spec = pl.BlockSpec((tk, tm), lambda i, j, q: (q + ak, i + am))
        a_dims = (0,)
    else:
        a_spec = pl.BlockSpec((tm, tk), lambda i, j, q: (i + am, q + ak))
        a_dims = (1,)
    if mode == "nt":
        b_spec = pl.BlockSpec((tn, tk), lambda i, j, q: (j + bn, q + bk))
        b_dims = (1,)
    else:
        b_spec = pl.BlockSpec((tk, tn), lambda i, j, q: (q + bk, j + bn))
        b_dims = (0,)
    o_spec = pl.BlockSpec((tm, tn), lambda i, j, q: (i, j))
    has_res = res is not None

    def body(*refs):
        a_ref, b_ref = refs[0], refs[1]
        res_ref = refs[2] if has_res else None
        o_ref = refs[2 + has_res]
        p = _dot(a_ref[...], b_ref[...], ((a_dims, b_dims), ((), ())))

        def finish(total):
            if has_res:
                total = total + res_ref[...].astype(F32)
            o_ref[...] = total.astype(out_dtype)

        if nk == 1:
            finish(p)
        else:
            acc_ref = refs[3 + has_res]
            q = pl.program_id(2)

            @pl.when(q == 0)
            def _():
                acc_ref[...] = p

            @pl.when(q > 0)
            def _():
                acc_ref[...] += p

            @pl.when(q == nk - 1)
            def _():
                finish(acc_ref[...])

    out, *carried = _grid_call(
        body, name=name, grid=(nm, nn, nk),
        in_specs=[a_spec, b_spec] + ([o_spec] if has_res else []),
        out_specs=[o_spec], out_shape=[jax.ShapeDtypeStruct((m, n), out_dtype)],
        scratch_shapes=[pltpu.VMEM((tm, tn), F32)] if nk > 1 else [],
        args=[a, b] + ([res] if has_res else []), semantics=("parallel", "parallel", "arbitrary"), comm=comm)
    return out if comm is None else (out, carried)


def _rms_fwd(x, w, name, tr=512):
    t, d = x.shape
    tr = min(tr, t)

    def body(x_ref, w_ref, o_ref):
        xv = x_ref[...]
        r = lax.rsqrt(jnp.mean(xv * xv, axis=-1, keepdims=True) + EPS)
        o_ref[...] = (xv * r * w_ref[...]).astype(BF16)

    return pl.pallas_call(
        body, name=name, grid=(t // tr,),
        in_specs=[pl.BlockSpec((tr, d), lambda i: (i, 0)), pl.BlockSpec((1, d), lambda i: (0, 0))],
        out_specs=pl.BlockSpec((tr, d), lambda i: (i, 0)),
        out_shape=jax.ShapeDtypeStruct((t, d), BF16),
        compiler_params=_cparams("parallel"),
    )(x, w)


def _rms_bwd(x, w, dy, dres, name, tr=256):
    t, d = x.shape
    tr = min(tr, t)

    def body(x_ref, w_ref, dy_ref, dres_ref, dx_ref, dxb_ref, dw_ref):
        xv = x_ref[...]
        dyv = dy_ref[...].astype(F32)
        r = lax.rsqrt(jnp.mean(xv * xv, axis=-1, keepdims=True) + EPS)
        gy = dyv * w_ref[...]
        proj = jnp.sum(gy * xv, axis=-1, keepdims=True) * (1.0 / d)
        dx = dres_ref[...] + r * gy - xv * (r * r * r) * proj
        dx_ref[...] = dx
        dxb_ref[...] = dx.astype(BF16)
        part = jnp.sum(dyv * xv * r, axis=0, keepdims=True)

        @pl.when(pl.program_id(0) == 0)
        def _():
            dw_ref[...] = part

        @pl.when(pl.program_id(0) > 0)
        def _():
            dw_ref[...] += part

    row = pl.BlockSpec((tr, d), lambda i: (i, 0))
    vec = pl.BlockSpec((1, d), lambda i: (0, 0))
    return pl.pallas_call(
        body, name=name, grid=(t // tr,),
        in_specs=[row, vec, row, row], out_specs=[row, row, vec],
        out_shape=[jax.ShapeDtypeStruct((t, d), F32), jax.ShapeDtypeStruct((t, d), BF16),
                   jax.ShapeDtypeStruct((1, d), F32)],
        compiler_params=_cparams("arbitrary"),
    )(x, w, dy, dres)


def _adamw(parts, w, m, v, name, tr=128):
    r, c = w.shape
    tr = tr if r % tr == 0 else r
    c1 = 1.0 - ADAM_B1 ** ADAM_STEP
    c2 = 1.0 - ADAM_B2 ** ADAM_STEP

    def body(p_ref, w_ref, m_ref, v_ref, g_ref, d_ref, nm_ref, nv_ref):
        g = p_ref[0].astype(F32)
        for s in range(1, N_DEV):
            g = g + p_ref[s].astype(F32)
        nm = ADAM_B1 * m_ref[...] + (1.0 - ADAM_B1) * g
        nv = ADAM_B2 * v_ref[...] + (1.0 - ADAM_B2) * (g * g)
        m_hat = nm / c1
        v_hat = nv / c2
        g_ref[...] = g
        d_ref[...] = -ADAM_LR * (m_hat / (jnp.sqrt(v_hat) + ADAM_EPS) + ADAM_WD * w_ref[...])
        nm_ref[...] = nm
        nv_ref[...] = nv

    blk = pl.BlockSpec((tr, c), lambda i: (i, 0))
    return pl.pallas_call(
        body, name=name, grid=(r // tr,),
        in_specs=[pl.BlockSpec((N_DEV, tr, c), lambda i: (0, i, 0)), blk, blk, blk],
        out_specs=[blk] * 4, out_shape=[jax.ShapeDtypeStruct((r, c), F32)] * 4,
        compiler_params=_cparams("parallel"),
    )(parts, w, m, v)


def _heads_of(x, nh):
    return [x[:, h * HEAD_DIM:(h + 1) * HEAD_DIM] for h in range(nh)]


def _headnorm_fwd(proj, w, col0, inner, name, tr=512, hb=4):
    t = proj.shape[0]
    tr = min(tr, t)
    wc = hb * HEAD_DIM
    c0 = col0 // wc

    def body(x_ref, w_ref, o_ref):
        outs = []
        for xh in _heads_of(x_ref[...], hb):
            r = lax.rsqrt(jnp.mean(xh * xh, axis=-1, keepdims=True) + EPS)
            outs.append((xh * r * w_ref[...]).astype(BF16))
        o_ref[...] = jnp.concatenate(outs, axis=1)

    return pl.pallas_call(
        body, name=name, grid=(t // tr, inner // wc),
        in_specs=[pl.BlockSpec((tr, wc), lambda i, j: (i, j + c0)), pl.BlockSpec((1, HEAD_DIM), lambda i, j: (0, 0))],
        out_specs=pl.BlockSpec((tr, wc), lambda i, j: (i, j)),
        out_shape=jax.ShapeDtypeStruct((t, inner), BF16),
        compiler_params=_cparams("parallel", "parallel"),
    )(proj, w)


def _headnorm_bwd(dy, proj, w, col0, inner, name, tr=512, hb=4):
    t = proj.shape[0]
    tr = min(tr, t)
    wc = hb * HEAD_DIM
    c0 = col0 // wc

    def body(dy_ref, x_ref, w_ref, dx_ref, dw_ref):
        outs = []
        part = jnp.zeros((1, HEAD_DIM), F32)
        for dyh, xh in zip(_heads_of(dy_ref[...], hb), _heads_of(x_ref[...], hb)):
            r = lax.rsqrt(jnp.mean(xh * xh, axis=-1, keepdims=True) + EPS)
            gy = dyh * w_ref[...]
            pr = jnp.sum(gy * xh, axis=-1, keepdims=True) * (1.0 / HEAD_DIM)
            outs.append((r * gy - xh * (r * r * r) * pr).astype(BF16))
            part = part + jnp.sum(dyh * xh * r, axis=0, keepdims=True)
        dx_ref[...] = jnp.concatenate(outs, axis=1)
        first = (pl.program_id(0) == 0) & (pl.program_id(1) == 0)

        @pl.when(first)
        def _():
            dw_ref[...] = part

        @pl.when(jnp.logical_not(first))
        def _():
            dw_ref[...] += part

    blk = pl.BlockSpec((tr, wc), lambda i, j: (i, j))
    return pl.pallas_call(
        body, name=name, grid=(t // tr, inner // wc),
        in_specs=[blk, pl.BlockSpec((tr, wc), lambda i, j: (i, j + c0)), pl.BlockSpec((1, HEAD_DIM), lambda i, j: (0, 0))],
        out_specs=[blk, pl.BlockSpec((1, HEAD_DIM), lambda i, j: (0, 0))],
        out_shape=[jax.ShapeDtypeStruct((t, inner), BF16), jax.ShapeDtypeStruct((1, HEAD_DIM), F32)],
        compiler_params=_cparams("arbitrary", "arbitrary"),
    )(dy, proj, w)


def _gate_fwd(o, proj, zcol0, inner, name, norm_w=None, tr=512, hb=4):
    t = o.shape[0]
    tr = min(tr, t)
    wc = hb * HEAD_DIM
    c0 = zcol0 // wc
    has_w = norm_w is not None

    def body(*refs):
        o_ref, z_ref = refs[0], refs[1]
        out_ref = refs[2 + has_w]
        outs = []
        for oh, zh in zip(_heads_of(o_ref[...], hb), _heads_of(z_ref[...], hb)):
            if has_w:
                r = lax.rsqrt(jnp.mean(oh * oh, axis=-1, keepdims=True) + EPS)
                oh = oh * r * refs[2][...]
            outs.append((oh * _silu(zh)).astype(BF16))
        out_ref[...] = jnp.concatenate(outs, axis=1)

    blk = pl.BlockSpec((tr, wc), lambda i, j: (i, j))
    vec = pl.BlockSpec((1, HEAD_DIM), lambda i, j: (0, 0))
    return pl.pallas_call(
        body, name=name, grid=(t // tr, inner // wc),
        in_specs=[blk, pl.BlockSpec((tr, wc), lambda i, j: (i, j + c0))] + ([vec] if has_w else []),
        out_specs=blk, out_shape=jax.ShapeDtypeStruct((t, inner), BF16),
        compiler_params=_cparams("parallel", "parallel"),
    )(*([o, proj] + ([norm_w] if has_w else [])))


def _gate_bwd(dg, o, proj, zcol0, inner, name, do_dtype, norm_w=None, tr=512, hb=4):
    t = o.shape[0]
    tr = min(tr, t)
    wc = hb * HEAD_DIM
    c0 = zcol0 // wc
    has_w = norm_w is not None

    def body(*refs):
        dg_ref, o_ref, z_ref = refs[0], refs[1], refs[2]
        do_ref, dz_ref = refs[3 + has_w], refs[4 + has_w]
        dos, dzs = [], []
        part = jnp.zeros((1, HEAD_DIM), F32)
        for dgh, oh, zh in zip(_heads_of(dg_ref[...], hb), _heads_of(o_ref[...], hb), _heads_of(z_ref[...], hb)):
            dy = dgh * _silu(zh)
            if has_w:
                w = refs[3][...]
                r = lax.rsqrt(jnp.mean(oh * oh, axis=-1, keepdims=True) + EPS)
                on = oh * r
                dzs.append((dgh * on * w * _dsilu(zh)).astype(BF16))
                gy = dy * w
                pr = jnp.sum(gy * oh, axis=-1, keepdims=True) * (1.0 / HEAD_DIM)
                dos.append((r * gy - oh * (r * r * r) * pr).astype(do_dtype))
                part = part + jnp.sum(dy * on, axis=0, keepdims=True)
            else:
                dzs.append((dgh * oh * _dsilu(zh)).astype(BF16))
                dos.append(dy.astype(do_dtype))
        do_ref[...] = jnp.concatenate(dos, axis=1)
        dz_ref[...] = jnp.concatenate(dzs, axis=1)
        if has_w:
            dw_ref = refs[6]
            first = (pl.program_id(0) == 0) & (pl.program_id(1) == 0)

            @pl.when(first)
            def _():
                dw_ref[...] = part

            @pl.when(jnp.logical_not(first))
            def _():
                dw_ref[...] += part

    blk = pl.BlockSpec((tr, wc), lambda i, j: (i, j))
    vec = pl.BlockSpec((1, HEAD_DIM), lambda i, j: (0, 0))
    return pl.pallas_call(
        body, name=name, grid=(t // tr, inner // wc),
        in_specs=[blk, blk, pl.BlockSpec((tr, wc), lambda i, j: (i, j + c0))] + ([vec] if has_w else []),
        out_specs=[blk, blk] + ([vec] if has_w else []),
        out_shape=[jax.ShapeDtypeStruct((t, inner), do_dtype), jax.ShapeDtypeStruct((t, inner), BF16)]
        + ([jax.ShapeDtypeStruct((1, HEAD_DIM), F32)] if has_w else []),
        compiler_params=_cparams("arbitrary", "arbitrary"),
    )(*([dg, o, proj] + ([norm_w] if has_w else [])))


def _loss_grad(h, target, name, tr=512):
    t, d = h.shape
    tr = min(tr, t)

    def body(h_ref, t_ref, g_ref, gb_ref, l_ref):
        e = h_ref[...] - t_ref[...]
        g = e * (1.0 / d)
        g_ref[...] = g
        gb_ref[...] = g.astype(BF16)
        part = jnp.zeros((1, HEAD_DIM), F32) + 0.5 * jnp.sum(jnp.sum(e * e, axis=-1, keepdims=True) * (1.0 / d))

        @pl.when(pl.program_id(0) == 0)
        def _():
            l_ref[...] = part

        @pl.when(pl.program_id(0) > 0)
        def _():
            l_ref[...] += part

    row = pl.BlockSpec((tr, d), lambda i: (i, 0))
    return pl.pallas_call(
        body, name=name, grid=(t // tr,),
        in_specs=[row, row], out_specs=[row, row, pl.BlockSpec((1, HEAD_DIM), lambda i: (0, 0))],
        out_shape=[jax.ShapeDtypeStruct((t, d), F32), jax.ShapeDtypeStruct((t, d), BF16),
                   jax.ShapeDtypeStruct((1, HEAD_DIM), F32)],
        compiler_params=_cparams("arbitrary"),
    )(h, target)


N_REL = 2 * REL_CLIP + 1
REL_PAD = 640
WIN = 2 * Q_TILE


def _diag_onehot():
    i = lax.broadcasted_iota(jnp.int32, (REL_PAD, WIN), 0)
    j = lax.broadcasted_iota(jnp.int32, (REL_PAD, WIN), 1)
    rel = jnp.where(j < Q_TILE + CHUNK, Q_TILE - j, Q_TILE + WIN - j)
    used = (j < Q_TILE + CHUNK) | (j > WIN - CHUNK)
    idx = jnp.clip(rel, -REL_CLIP, REL_CLIP) + REL_CLIP
    return jnp.where(used & (i == idx), 1.0, 0.0).astype(F32)


def _band_mask():
    r = lax.broadcasted_iota(jnp.int32, (Q_TILE, WIN), 0) // CHUNK
    kc = lax.broadcasted_iota(jnp.int32, (Q_TILE, WIN), 1) // CHUNK - LEFT_CHUNKS
    return (kc <= r) & (kc >= r - LEFT_CHUNKS)


def _bias_tiles(rel_bias_pad, name):
    nh = rel_bias_pad.shape[0]

    def body(rb_ref, o_ref):
        dvec = _nn(rb_ref[...], _diag_onehot(), HIGHEST)[0:1, :]
        tile = pltpu.roll(jnp.broadcast_to(dvec, (Q_TILE, WIN)), 0, 1, stride=1, stride_axis=0)
        o_ref[...] = jnp.where(_band_mask(), tile, NEG_BIG)

    return pl.pallas_call(
        body, name=name, grid=(nh,),
        in_specs=[pl.BlockSpec((None, 8, REL_PAD), lambda h: (h, 0, 0))],
        out_specs=pl.BlockSpec((None, Q_TILE, WIN), lambda h: (h, 0, 0)),
        out_shape=jax.ShapeDtypeStruct((nh, Q_TILE, WIN), F32),
        compiler_params=_cparams("parallel"),
    )(rel_bias_pad)


def _bias_grad(dtile, name):
    nh = dtile.shape[0]

    def body(d_ref, o_ref):
        ri = lax.broadcasted_iota(jnp.int32, (Q_TILE, Q_TILE), 0)
        ci = lax.broadcasted_iota(jnp.int32, (Q_TILE, Q_TILE), 1)
        flip = jnp.where(ri + ci == Q_TILE - 1, 1.0, 0.0).astype(F32)
        rev = _nn(flip, d_ref[...], HIGHEST)
        rolled = pltpu.roll(rev, WIN - (Q_TILE - 1), 1, stride=1, stride_axis=0)
        diag = jnp.broadcast_to(jnp.sum(rolled, axis=0, keepdims=True), (8, WIN))
        o_ref[...] = _nt(diag, _diag_onehot(), HIGHEST)

    return pl.pallas_call(
        body, name=name, grid=(nh,),
        in_specs=[pl.BlockSpec((None, Q_TILE, WIN), lambda h: (h, 0, 0))],
        out_specs=pl.BlockSpec((None, 8, REL_PAD), lambda h: (h, 0, 0)),
        out_shape=jax.ShapeDtypeStruct((nh, 8, REL_PAD), F32),
        compiler_params=_cparams("parallel"),
    )(dtile)


def _attn_probs(q, k0, k1, bias, first_tile):
    s = jnp.concatenate([_nt(q, k0), _nt(q, k1)], axis=1) * (HEAD_DIM ** -0.5) + bias
    col = lax.broadcasted_iota(jnp.int32, (Q_TILE, WIN), 1)
    s = jnp.where(first_tile & (col < Q_TILE), NEG_BIG, s)
    p = jnp.exp(s - jnp.max(s, axis=-1, keepdims=True))
    return p * (1.0 / jnp.sum(p, axis=-1, keepdims=True))


def _attn_fwd(q, k, v, bias, name):
    t, inner = q.shape
    nh, nt = inner // HEAD_DIM, t // Q_TILE

    def body(q_ref, k0_ref, k1_ref, v0_ref, v1_ref, b_ref, o_ref):
        p = _bf(_attn_probs(q_ref[...], k0_ref[...], k1_ref[...], b_ref[...], pl.program_id(1) == 0))
        o_ref[...] = _nn(p[:, :Q_TILE], v0_ref[...]) + _nn(p[:, Q_TILE:], v1_ref[...])

    cur = pl.BlockSpec((Q_TILE, HEAD_DIM), lambda h, i: (i, h))
    prev = pl.BlockSpec((Q_TILE, HEAD_DIM), lambda h, i: (jnp.maximum(i - 1, 0), h))
    return pl.pallas_call(
        body, name=name, grid=(nh, nt),
        in_specs=[cur, prev, cur, prev, cur, pl.BlockSpec((None, Q_TILE, WIN), lambda h, i: (h, 0, 0))],
        out_specs=cur, out_shape=jax.ShapeDtypeStruct((t, inner), F32),
        compiler_params=_cparams("parallel", "parallel"),
    )(q, k, k, v, v, bias)


def _attn_bwd(q, k, v, do, bias, name):
    t, inner = q.shape
    nh, nt = inner // HEAD_DIM, t // Q_TILE
    scale = HEAD_DIM ** -0.5

    def body(q_ref, k0_ref, k1_ref, v0_ref, v1_ref, do_ref, b_ref, dq_ref, dk_ref, dv_ref, db_ref, ck_ref, cv_ref):
        i = pl.program_id(1)

        @pl.when(i == 0)
        def _():
            ck_ref[...] = jnp.zeros(blk, F32)
            cv_ref[...] = jnp.zeros(blk, F32)

        @pl.when(i < nt)
        def _():
            qv, dov = q_ref[...], do_ref[...]
            p = _attn_probs(qv, k0_ref[...], k1_ref[...], b_ref[...], i == 0)
            dp = jnp.concatenate([_nt(dov, v0_ref[...]), _nt(dov, v1_ref[...])], axis=1)
            ds = p * (dp - jnp.sum(p * dp, axis=-1, keepdims=True))

            @pl.when(i == 0)
            def _():
                db_ref[...] = ds

            @pl.when(i > 0)
            def _():
                db_ref[...] += ds

            pb, dsb = _bf(p), _bf(ds)
            dq_ref[...] = (_nn(dsb[:, :Q_TILE], k0_ref[...]) + _nn(dsb[:, Q_TILE:], k1_ref[...])) * scale
            dk_ref[...] = ck_ref[...] + _tn(dsb[:, :Q_TILE], qv) * scale
            dv_ref[...] = (cv_ref[...] + _tn(pb[:, :Q_TILE], dov)).astype(BF16)
            ck_ref[...] = _tn(dsb[:, Q_TILE:], qv) * scale
            cv_ref[...] = _tn(pb[:, Q_TILE:], dov)

        @pl.when(i == nt)
        def _():
            dk_ref[...] = ck_ref[...]
            dv_ref[...] = cv_ref[...].astype(BF16)

    blk = (Q_TILE, HEAD_DIM)
    cur = pl.BlockSpec(blk, lambda h, i: (jnp.minimum(i, nt - 1), h))
    prev = pl.BlockSpec(blk, lambda h, i: (jnp.clip(i - 1, 0, nt - 1), h))
    lag = pl.BlockSpec(blk, lambda h, i: (jnp.maximum(i - 1, 0), h))
    tile = pl.BlockSpec((None, Q_TILE, WIN), lambda h, i: (h, 0, 0))
    return pl.pallas_call(
        body, name=name, grid=(nh, nt + 1),
        in_specs=[cur, prev, cur, prev, cur, cur, tile],
        out_specs=[cur, lag, lag, tile],
        out_shape=[jax.ShapeDtypeStruct((t, inner), F32)] * 2 + [jax.ShapeDtypeStruct((t, inner), BF16),
                                                                 jax.ShapeDtypeStruct((nh, Q_TILE, WIN), F32)],
        scratch_shapes=[pltpu.VMEM(blk, F32), pltpu.VMEM(blk, F32)],
        compiler_params=_cparams("arbitrary", "arbitrary"),
    )(q, k, k, v, v, do, bias)


def _pad_rel_bias(rel_bias):
    nh = rel_bias.shape[0]
    return jnp.broadcast_to(jnp.pad(rel_bias, ((0, 0), (0, REL_PAD - N_REL)))[:, None, :], (nh, 8, REL_PAD))


def _layer_b_fwd(h1, nw, w_in, qw, kw, bias, w_out):
    t, d = h1.shape
    inner = w_out.shape[0]
    hn = _rms_fwd(h1, nw, "b_rms")
    proj = _mm(hn, w_in, "nn", t, 4 * inner, d, out_dtype=F32, name="b_proj")
    qn = _headnorm_fwd(proj, qw, 0, inner, "b_qnorm")
    kn = _headnorm_fwd(proj, kw, inner, inner, "b_knorm")
    vb = proj[:, 2 * inner:3 * inner].astype(BF16)
    o = _attn_fwd(qn, kn, vb, bias, "b_attn")
    g = _gate_fwd(o, proj, 3 * inner, inner, "b_gate")
    h2 = _mm(g, w_out, "nn", t, d, inner, out_dtype=F32, name="b_out", res=h1)
    return h2, (hn, proj, qn, kn, vb, o, g)


def _layer_b_bwd(dh2, dh2b, h1, nw, w_in, qw, kw, bias, w_out, saved):
    hn, proj, qn, kn, vb, o, g = saved
    t, d = h1.shape
    inner = w_out.shape[0]
    dg = _mm(dh2b, w_out, "nt", t, inner, d, out_dtype=F32, name="b_dgate")
    dw_out = _mm(g, dh2b, "tn", inner, d, t, out_dtype=F32, name="b_dwout")
    do, dz = _gate_bwd(dg, o, proj, 3 * inner, inner, "b_gate_bwd", BF16)
    dq, dk, dv, dtile = _attn_bwd(qn, kn, vb, do, bias, "b_attn_bwd")
    dqr, dqw = _headnorm_bwd(dq, proj, qw, 0, inner, "b_qnorm_bwd")
    dkr, dkw = _headnorm_bwd(dk, proj, kw, inner, inner, "b_knorm_bwd")
    dproj = jnp.concatenate([dqr, dkr, dv, dz], axis=1)
    dhn = _mm(dproj, w_in, "nt", t, d, 4 * inner, out_dtype=F32, name="b_dhn")
    dw_in = _mm(hn, dproj, "tn", d, 4 * inner, t, out_dtype=F32, name="b_dwin")
    dh1, dh1b, dnw = _rms_bwd(h1, nw, dhn, dh2, "b_rms_bwd")
    drb = _bias_grad(dtile, "b_bias_grad")[:, 0, :N_REL]
    return dh1, dh1b, dnw, dw_in, dqw, dkw, drb, dw_out


LANES = 128


def _softplus(x):
    return jnp.maximum(x, 0.0) + jnp.log1p(jnp.exp(-jnp.abs(x)))


def _gates_fwd(ab, alog_row, dt_row, nh, name, tr=1024):
    t = ab.shape[0]
    tr = min(tr, t)

    def body(x_ref, al_ref, dt_ref, o_ref):
        x = x_ref[...]
        lane = lax.broadcasted_iota(jnp.int32, x.shape, 1)
        g = -jnp.exp(al_ref[...]) * _softplus(x + dt_ref[...])
        o_ref[...] = jnp.where(lane < nh, g, jnp.where(lane < 2 * nh, _sigmoid(x), 0.0))

    row = pl.BlockSpec((tr, LANES), lambda i: (i, 0))
    vec = pl.BlockSpec((1, LANES), lambda i: (0, 0))
    return pl.pallas_call(
        body, name=name, grid=(t // tr,), in_specs=[row, vec, vec], out_specs=row,
        out_shape=jax.ShapeDtypeStruct((t, LANES), F32), compiler_params=_cparams("parallel"),
    )(ab, alog_row, dt_row)


def _gates_bwd(ab, alog_row, dt_row, dgates, nh, name, tr=1024):
    t = ab.shape[0]
    tr = min(tr, t)
    npart = dgates.shape[0]

    def body(x_ref, al_ref, dt_ref, dg_ref, dx_ref, s_ref):
        x = x_ref[...]
        lane = lax.broadcasted_iota(jnp.int32, x.shape, 1)
        dgt = dg_ref[0]
        for p in range(1, npart):
            dgt = dgt + dg_ref[p]
        ea = jnp.exp(al_ref[...])
        xa = x + dt_ref[...]
        da = jnp.where(lane < nh, dgt * (-ea) * _sigmoid(xa), 0.0)
        beta = _sigmoid(x)
        db = jnp.where((lane >= nh) & (lane < 2 * nh), dgt * beta * (1.0 - beta), 0.0)
        dx_ref[...] = (da + db).astype(BF16)
        dal = jnp.sum(jnp.where(lane < nh, dgt * (-ea) * _softplus(xa), 0.0), axis=0, keepdims=True)
        ddt = jnp.sum(da, axis=0, keepdims=True)
        r8 = lax.broadcasted_iota(jnp.int32, (8, LANES), 0)
        part = jnp.where(r8 == 0, dal, jnp.where(r8 == 1, ddt, 0.0))

        @pl.when(pl.program_id(0) == 0)
        def _():
            s_ref[...] = part

        @pl.when(pl.program_id(0) > 0)
        def _():
            s_ref[...] += part

    row = pl.BlockSpec((tr, LANES), lambda i: (i, 0))
    vec = pl.BlockSpec((1, LANES), lambda i: (0, 0))
    return pl.pallas_call(
        body, name=name, grid=(t // tr,),
        in_specs=[row, vec, vec, pl.BlockSpec((npart, tr, LANES), lambda i: (0, i, 0))],
        out_specs=[row, pl.BlockSpec((8, LANES), lambda i: (0, 0))],
        out_shape=[jax.ShapeDtypeStruct((t, LANES), BF16), jax.ShapeDtypeStruct((8, LANES), F32)],
        compiler_params=_cparams("arbitrary"),
    )(ab, alog_row, dt_row, dgates)


HALO = 8


def _conv_taps(ext, w, rows):
    acc = ext[HALO:HALO + rows] * w[CONV_K - 1:CONV_K]
    for s in range(1, CONV_K):
        acc = acc + pltpu.roll(ext, s, 0)[HALO:HALO + rows] * w[CONV_K - 1 - s:CONV_K - s]
    return acc


def _conv_fwd(proj, conv_w, col0, inner, mode, name, tt=512, hb=4):
    t = proj.shape[0]
    tt = min(tt, t)
    wc = hb * HEAD_DIM
    c0 = col0 // wc
    hpb = tt // HALO

    def body(x_ref, halo_ref, w_ref, o_ref):
        halo = jnp.where(pl.program_id(1) == 0, 0.0, halo_ref[...])
        s = _silu(_conv_taps(jnp.concatenate([halo, x_ref[...]], axis=0), w_ref[...], tt))
        if mode == "v":
            o_ref[...] = s
        else:
            mul = HEAD_DIM ** -0.5 if mode == "q" else 1.0
            o_ref[...] = jnp.concatenate(
                [sh * (lax.rsqrt(jnp.sum(sh * sh, axis=-1, keepdims=True) + EPS) * mul) for sh in _heads_of(s, hb)], axis=1)

    return pl.pallas_call(
        body, name=name, grid=(inner // wc, t // tt),
        in_specs=[pl.BlockSpec((tt, wc), lambda j, i: (i, j + c0)),
                  pl.BlockSpec((HALO, wc), lambda j, i: (jnp.maximum(i * hpb - 1, 0), j + c0)),
                  pl.BlockSpec((CONV_K, wc), lambda j, i: (0, j + c0))],
        out_specs=pl.BlockSpec((tt, wc), lambda j, i: (i, j)),
        out_shape=jax.ShapeDtypeStruct((t, inner), F32),
        compiler_params=_cparams("parallel", "parallel"),
    )(proj, proj, conv_w)


def _conv_bwd(dy, proj, conv_w, col0, inner, mode, name, tt=512, hb=4):
    t = proj.shape[0]
    tt = min(tt, t)
    nt = t // tt
    wc = hb * HEAD_DIM
    c0 = col0 // wc
    hpb = tt // HALO
    rows = tt + HALO

    def body(dy_ref, dyn_ref, x_ref, xp_ref, xn_ref, w_ref, dx_ref, dw_ref):
        i = pl.program_id(1)
        w = w_ref[...]
        xprev = jnp.where(i == 0, 0.0, xp_ref[...])
        ext = jnp.concatenate([xprev, x_ref[...], xn_ref[...]], axis=0)
        c = _conv_taps(ext, w, rows)
        dyv = jnp.concatenate([dy_ref[...], jnp.where(i == nt - 1, 0.0, dyn_ref[...])], axis=0)
        sg = _sigmoid(c)
        s = c * sg
        if mode == "v":
            ds = dyv
        else:
            mul = HEAD_DIM ** -0.5 if mode == "q" else 1.0
            parts = []
            for dyh, sh in zip(_heads_of(dyv, hb), _heads_of(s, hb)):
                r = lax.rsqrt(jnp.sum(sh * sh, axis=-1, keepdims=True) + EPS)
                parts.append(mul * (r * dyh - sh * (r * r * r) * jnp.sum(dyh * sh, axis=-1, keepdims=True)))
            ds = jnp.concatenate(parts, axis=1)
        dc = ds * (sg * (1.0 + c * (1.0 - sg)))
        dx = dc[:tt] * w[CONV_K - 1:CONV_K]
        for sft in range(1, CONV_K):
            dx = dx + pltpu.roll(dc, rows - sft, 0)[:tt] * w[CONV_K - 1 - sft:CONV_K - sft]
        dx_ref[...] = dx.astype(BF16)
        r8 = lax.broadcasted_iota(jnp.int32, (8, wc), 0)
        part = jnp.zeros((8, wc), F32)
        for sft in range(CONV_K):
            xs = ext[HALO:HALO + tt] if sft == 0 else pltpu.roll(ext, sft, 0)[HALO:HALO + tt]
            part = part + jnp.where(r8 == CONV_K - 1 - sft, jnp.sum(dc[:tt] * xs, axis=0, keepdims=True), 0.0)

        @pl.when(i == 0)
        def _():
            dw_ref[...] = part

        @pl.when(i > 0)
        def _():
            dw_ref[...] += part

    cur = lambda off: pl.BlockSpec((tt, wc), lambda j, i: (i, j + off))
    nxt = lambda off: pl.BlockSpec((HALO, wc), lambda j, i: (jnp.minimum((i + 1) * hpb, t // HALO - 1), j + off))
    return pl.pallas_call(
        body, name=name, grid=(inner // wc, nt),
        in_specs=[cur(0), nxt(0), cur(c0),
                  pl.BlockSpec((HALO, wc), lambda j, i: (jnp.maximum(i * hpb - 1, 0), j + c0)), nxt(c0),
                  pl.BlockSpec((CONV_K, wc), lambda j, i: (0, j + c0))],
        out_specs=[pl.BlockSpec((tt, wc), lambda j, i: (i, j)), pl.BlockSpec((8, wc), lambda j, i: (0, j))],
        out_shape=[jax.ShapeDtypeStruct((t, inner), BF16), jax.ShapeDtypeStruct((8, inner), F32)],
        compiler_params=_cparams("parallel", "arbitrary"),
    )(dy, dy, proj, proj, proj, conv_w)


GDN_HB = 4
GDN_NB = 2
SCAN_HB = 8
SCAN_NB = 2


def _iota2(n, m):
    return lax.broadcasted_iota(jnp.int32, (n, m), 0), lax.broadcasted_iota(jnp.int32, (n, m), 1)


def _head_select(first_head, hb, lane0):
    r, lane = _iota2(8, LANES)
    return jnp.where((r < hb) & (lane == lane0 + first_head + r), 1.0, 0.0).astype(F32)


def _chunk_gates(gt, selg, selb):
    i, j = _iota2(CHUNK, CHUNK)
    gc_all = _dot_exact(jnp.where(j <= i, 1.0, 0.0), gt, NN, True)
    return (_dot_exact(gc_all, selg, NT, False), _dot_exact(selg, gc_all, NT, True),
            _dot_exact(gt, selb, NT, False))


def _decay_terms(gcol, grow):
    i, j = _iota2(CHUNK, CHUNK)
    glast = gcol[:, CHUNK - 1:CHUNK, :]
    decay = jnp.exp(jnp.where(j <= i, gcol - grow, NEG_BIG))
    return jnp.exp(gcol), jnp.exp(glast - gcol), jnp.exp(glast), decay


def _unit_lower_inverse(a):
    i, j = _iota2(CHUNK, CHUNK)
    same16 = (i // 16) == (j // 16)
    same32 = (i // 32) == (j // 32)
    m = jnp.where(same16, -a, 0.0)
    x = jnp.where(i == j, 1.0, 0.0) + m
    for _ in range(3):
        m = _dot3(m, m, BNN)
        x = x + _dot3(x, m, BNN)
    for off in (jnp.where(same32 & jnp.logical_not(same16), a, 0.0), jnp.where(same32, 0.0, a)):
        x = x - _dot3(_dot3(x, off, BNN), x, BNN)
    return x


def _unit_inputs(refs, g_ref, selg, selb, hb, nb):
    units = [(c, h) for c in range(nb) for h in range(hb)]
    rs = lambda c: slice(c * CHUNK, (c + 1) * CHUNK)
    cs = lambda h: slice(h * HEAD_DIM, (h + 1) * HEAD_DIM)
    gates = [_chunk_gates(g_ref[rs(c), :], selg, selb) for c in range(nb)]
    stacked = [jnp.stack([r[rs(c), cs(h)] for c, h in units]) for r in refs]
    gcol = jnp.stack([gates[c][0][:, h:h + 1] for c, h in units])
    grow = jnp.stack([gates[c][1][h:h + 1, :] for c, h in units])
    bcol = jnp.stack([gates[c][2][:, h:h + 1] for c, h in units])
    return units, rs, cs, stacked, gcol, grow, bcol


def _gdn_specs(nh, inner, t, heads=GDN_HB, chunks=GDN_NB):
    hb, nb = min(heads, nh), chunks
    rows = nb * CHUNK
    wide = pl.BlockSpec((rows, hb * HEAD_DIM), lambda g, n: (n, g))
    sq = pl.BlockSpec((hb, rows, CHUNK), lambda g, n: (g, n, 0))
    gts = pl.BlockSpec((rows, LANES), lambda g, n: (n, 0))
    glb = pl.BlockSpec((nb * 8, hb * HEAD_DIM), lambda g, n: (n, g))
    return hb, nb, rows, wide, sq, gts, glb


def _gdn_intra_fwd(q, k, v, gates, nh, name, comm=None):
    t, inner = q.shape
    hb, nb, rows, wide, sq, gts, glb = _gdn_specs(nh, inner, t)

    def body(q_ref, k_ref, v_ref, g_ref, qe_ref, kel_ref, wb_ref, w_ref, u_ref, qk_ref, tm_ref, gl_ref):
        first = pl.program_id(0) * hb
        selg, selb = _head_select(first, hb, 0), _head_select(first, hb, nh)
        i, j = _iota2(CHUNK, CHUNK)
        units, rs, cs, (qv, kv, vv), gcol, grow, bcol = _unit_inputs(
            (q_ref, k_ref, v_ref), g_ref, selg, selb, hb, nb)
        e, el, gl, decay = _decay_terms(gcol, grow)
        kb = kv * bcol
        qbf, kbf = _bf(qv), _bf(kv)
        a = jnp.where(j < i, _dot(_bf(kb), kbf, BNT) * decay, 0.0)
        tm = _unit_lower_inverse(a)
        uw = _dot3(tm, jnp.concatenate([vv * bcol, kb * e], axis=2), BNN)
        qk = _bf(_dot(qbf, kbf, BNT) * decay)
        qe, kel = _bf(qv * e), _bf(kv * el)
        for n, (c, h) in enumerate(units):
            w = uw[n, :, HEAD_DIM:]
            qe_ref[rs(c), cs(h)] = qe[n]
            kel_ref[rs(c), cs(h)] = kel[n]
            wb_ref[rs(c), cs(h)] = _bf(w)
            w_ref[rs(c), cs(h)] = w
            u_ref[rs(c), cs(h)] = uw[n, :, :HEAD_DIM]
            qk_ref[h, rs(c), :] = qk[n]
            tm_ref[h, rs(c), :] = tm[n]
            gl_ref[c * 8:(c + 1) * 8, cs(h)] = jnp.broadcast_to(gl[n], (8, HEAD_DIM))

    big = lambda dt: jax.ShapeDtypeStruct((t, inner), dt)
    return _grid_call(
        body, name=name, grid=(nh // hb, t // rows),
        in_specs=[wide, wide, wide, gts],
        out_specs=[wide] * 5 + [sq, sq, glb],
        out_shape=[big(BF16), big(BF16), big(BF16), big(F32), big(F32),
                   jax.ShapeDtypeStruct((nh, t, CHUNK), BF16), jax.ShapeDtypeStruct((nh, t, CHUNK), F32),
                   jax.ShapeDtypeStruct((t // CHUNK * 8, inner), F32)],
        args=(q, k, v, gates), semantics=("parallel", "parallel"), comm=comm)


def _gdn_scan_fwd(qe, kel, wb, u, qk, glb, nh, name):
    t, inner = u.shape
    hb, nb, rows, wide, sq, _, glb_spec = _gdn_specs(nh, inner, t, SCAN_HB, SCAN_NB)

    def body(qe_ref, kel_ref, wb_ref, u_ref, qk_ref, gl_ref, o_ref, vn_ref, sall_ref, s_ref):
        @pl.when(pl.program_id(1) == 0)
        def _():
            s_ref[...] = jnp.zeros(s_ref.shape, F32)

        cs = lambda h: slice(h * HEAD_DIM, (h + 1) * HEAD_DIM)
        for c in range(nb):
            rs = slice(c * CHUNK, (c + 1) * CHUNK)
            heads = lambda ref: jnp.stack([ref[rs, cs(h)] for h in range(hb)])
            s = s_ref[...]
            sall_ref[c] = s
            sb = _bf(s)
            vn = heads(u_ref) - _dot(heads(wb_ref), sb, BNN)
            vnb = _bf(vn)
            o = _dot(heads(qe_ref), sb, BNN) + _dot(qk_ref[:, rs, :], vnb, BNN)
            gl = jnp.stack([gl_ref[c * 8:c * 8 + 1, cs(h)] for h in range(hb)])
            s_ref[...] = s * gl + _dot(heads(kel_ref), vnb, BTN)
            for h in range(hb):
                vn_ref[rs, cs(h)] = vn[h]
                o_ref[rs, cs(h)] = o[h]

    return pl.pallas_call(
        body, name=name, grid=(nh // hb, t // rows),
        in_specs=[wide, wide, wide, wide, sq, glb_spec],
        out_specs=[wide, wide, pl.BlockSpec((nb, hb, HEAD_DIM, HEAD_DIM), lambda g, n: (n, g, 0, 0))],
        out_shape=[jax.ShapeDtypeStruct((t, inner), F32), jax.ShapeDtypeStruct((t, inner), F32),
                   jax.ShapeDtypeStruct((t // CHUNK, nh, HEAD_DIM, HEAD_DIM), F32)],
        scratch_shapes=[pltpu.VMEM((hb, HEAD_DIM, HEAD_DIM), F32)],
        compiler_params=_cparams("parallel", "arbitrary"),
    )(qe, kel, wb, u, qk, glb)


def _gdn_scan_bwd(do, qe, kel, wb, vn, qk, glb, sall, nh, name):
    t, inner = do.shape
    hb, nb, rows, _, _, _, _ = _gdn_specs(nh, inner, t, SCAN_HB, SCAN_NB)
    last = t // rows - 1
    wide = pl.BlockSpec((rows, hb * HEAD_DIM), lambda g, n: (last - n, g))
    sq = pl.BlockSpec((hb, rows, CHUNK), lambda g, n: (g, last - n, 0))
    glb_spec = pl.BlockSpec((nb * 8, hb * HEAD_DIM), lambda g, n: (last - n, g))

    def body(do_ref, qe_ref, kel_ref, wb_ref, vn_ref, qk_ref, gl_ref, sall_ref,
             dvn_ref, dw_ref, dqe_ref, dkel_ref, dqk_ref, dgl_ref, ds_ref):
        @pl.when(pl.program_id(1) == 0)
        def _():
            ds_ref[...] = jnp.zeros(ds_ref.shape, F32)

        cs = lambda h: slice(h * HEAD_DIM, (h + 1) * HEAD_DIM)
        for c in reversed(range(nb)):
            rs = slice(c * CHUNK, (c + 1) * CHUNK)
            heads = lambda ref: jnp.stack([ref[rs, cs(h)] for h in range(hb)])
            ds, s = ds_ref[...], sall_ref[c]
            dsb, sb = _bf(ds), _bf(s)
            dob, vnb = _bf(heads(do_ref)), _bf(heads(vn_ref))
            dvn = _dot(qk_ref[:, rs, :], dob, BTN) + _dot(heads(kel_ref), dsb, BNN)
            dvnb = _bf(dvn)
            dw = -_dot(dvnb, sb, BNT)
            dqe = _dot(dob, sb, BNT)
            dkel = _dot(vnb, dsb, BNT)
            dqk_ref[:, rs, :] = _dot(dob, vnb, BNT)
            dgl = jnp.sum(jnp.sum(ds * s, axis=2, keepdims=True), axis=1, keepdims=True)
            gl = jnp.stack([gl_ref[c * 8:c * 8 + 1, cs(h)] for h in range(hb)])
            ds_ref[...] = ds * gl + _dot(heads(qe_ref), dob, BTN) - _dot(heads(wb_ref), dvnb, BTN)
            for h in range(hb):
                dvn_ref[rs, cs(h)] = dvn[h]
                dw_ref[rs, cs(h)] = dw[h]
                dqe_ref[rs, cs(h)] = dqe[h]
                dkel_ref[rs, cs(h)] = dkel[h]
                dgl_ref[c * 8:(c + 1) * 8, cs(h)] = jnp.broadcast_to(dgl[h], (8, HEAD_DIM))

    big = jax.ShapeDtypeStruct((t, inner), F32)
    return pl.pallas_call(
        body, name=name, grid=(nh // hb, t // rows),
        in_specs=[wide, wide, wide, wide, wide, sq, glb_spec,
                  pl.BlockSpec((nb, hb, HEAD_DIM, HEAD_DIM), lambda g, n: (last - n, g, 0, 0))],
        out_specs=[wide] * 4 + [sq, glb_spec],
        out_shape=[big] * 4 + [jax.ShapeDtypeStruct((nh, t, CHUNK), F32),
                               jax.ShapeDtypeStruct((t // CHUNK * 8, inner), F32)],
        scratch_shapes=[pltpu.VMEM((hb, HEAD_DIM, HEAD_DIM), F32)],
        compiler_params=_cparams("parallel", "arbitrary"),
    )(do, qe, kel, wb, vn, qk, glb, sall)


def _gdn_intra_bwd(q, k, v, gates, tm, w, u, dvn, dw, dqe, dkel, dqk, dglb, nh, name, comm=None):
    t, inner = q.shape
    hb, nb, rows, wide, sq, gts, glb = _gdn_specs(nh, inner, t)

    def body(q_ref, k_ref, v_ref, g_ref, tm_ref, w_ref, u_ref, dvn_ref, dw_ref, dqe_ref, dkel_ref, dqk_ref,
             dgl_ref, dq_ref, dk_ref, dv_ref, dg_ref):
        first = pl.program_id(0) * hb
        selg, selb = _head_select(first, hb, 0), _head_select(first, hb, nh)
        i, j = _iota2(CHUNK, CHUNK)
        lane8 = lax.broadcasted_iota(jnp.int32, (CHUNK, 8), 1)
        row = lax.broadcasted_iota(jnp.int32, (CHUNK, 1), 0)
        lower = jnp.where(j <= i, 1.0, 0.0).astype(F32)
        rsum = lambda x: jnp.sum(x, axis=-1, keepdims=True)
        units, rs, cs, (qv, kv, vv, wv, uv, dvn, dw, dqe, dkel), gcol, grow, bcol = _unit_inputs(
            (q_ref, k_ref, v_ref, w_ref, u_ref, dvn_ref, dw_ref, dqe_ref, dkel_ref), g_ref, selg, selb, hb, nb)
        nu = len(units)
        tmv = jnp.stack([tm_ref[h, rs(c), :] for c, h in units])
        dqk = jnp.where(j <= i, jnp.stack([dqk_ref[h, rs(c), :] for c, h in units]), 0.0)
        dgl = jnp.stack([dgl_ref[c * 8:c * 8 + 1, h * HEAD_DIM:h * HEAD_DIM + 1] for c, h in units])
        e, el, gl, decay = _decay_terms(gcol, grow)
        kb = kv * bcol
        qb, kbf, kbb = _bf(qv), _bf(kv), _bf(kb)
        dqkr = _bf(dqk * decay)
        dq = dqe * e + _dot(dqkr, kbf, BNN)
        dk = dkel * el + _dot(dqkr, qb, BTN)
        de = rsum(dqe * qv)
        del_ = rsum(dkel * kv)
        mq = dqk * _dot(qb, kbf, BNT) * decay
        dsol = _dot3(tmv, jnp.concatenate([dvn, dw], axis=2), BTN)
        dvb, dkbe = dsol[:, :, :HEAD_DIM], dsol[:, :, HEAD_DIM:]
        da = -jnp.where(j < i, _dot3(dsol, jnp.concatenate([uv, wv], axis=2), BNT), 0.0)
        dkk = _bf(da * decay)
        ma = da * _dot(kbb, kbf, BNT) * decay
        dkb = dkbe * e + _dot(dkk, kbf, BNN)
        de = de + rsum(dkbe * kb)
        dk = dk + _dot(dkk, kbb, BTN) + dkb * bcol
        dv = dvb * bcol
        dbeta = rsum(dkb * kv) + rsum(dvb * vv)
        m = mq + ma
        ones = jnp.ones((nu, CHUNK, LANES), F32)
        dgc = rsum(m) - _dot_exact(m, ones, BTN, False)[:, :, 0:1] + de * e - del_ * el
        tail = jnp.sum(del_ * el, axis=1, keepdims=True) + dgl * gl
        dgc = dgc + jnp.where(row == CHUNK - 1, tail, 0.0)
        for n, (c, h) in enumerate(units):
            dq_ref[rs(c), cs(h)] = dq[n]
            dk_ref[rs(c), cs(h)] = dk[n]
            dv_ref[rs(c), cs(h)] = dv[n]
        for c in range(nb):
            dgc_cols = jnp.zeros((CHUNK, 8), F32)
            dbeta_cols = jnp.zeros((CHUNK, 8), F32)
            for h in range(hb):
                dgc_cols = jnp.where(lane8 == h, dgc[c * hb + h], dgc_cols)
                dbeta_cols = jnp.where(lane8 == h, dbeta[c * hb + h], dbeta_cols)
            dg_cols = _dot_exact(lower, dgc_cols, TN, True)
            dg_ref[rs(c), :] = _dot_exact(dg_cols, selg, NN, False) + _dot_exact(dbeta_cols, selb, NN, False)

    big = jax.ShapeDtypeStruct((t, inner), F32)
    return _grid_call(
        body, name=name, grid=(nh // hb, t // rows),
        in_specs=[wide, wide, wide, gts, sq, wide, wide, wide, wide, wide, wide, sq, glb],
        out_specs=[wide, wide, wide, pl.BlockSpec((None, rows, LANES), lambda g, n: (g, n, 0))],
        out_shape=[big, big, big, jax.ShapeDtypeStruct((nh // hb, t, LANES), F32)],
        args=(q, k, v, gates, tm, w, u, dvn, dw, dqe, dkel, dqk, dglb), semantics=("parallel", "parallel"),
        comm=comm)


def _layer_a_fwd(x, nw, w_main, w_ab, conv_w, alog_row, dt_row, onw, w_out, nh, comm=None):
    t, d = x.shape
    inner = w_out.shape[0]
    hn = _rms_fwd(x, nw, "a_rms")
    proj = _mm(hn, w_main, "nn", t, 4 * inner, d, out_dtype=F32, name="a_proj")
    ab = _mm(hn, w_ab, "nn", t, LANES, d, out_dtype=F32, name="a_proj_ab")
    gates = _gates_fwd(ab, alog_row, dt_row, nh, "a_gates")
    q = _conv_fwd(proj, conv_w, 0, inner, "q", "a_conv_q")
    k = _conv_fwd(proj, conv_w, inner, inner, "k", "a_conv_k")
    v = _conv_fwd(proj, conv_w, 2 * inner, inner, "v", "a_conv_v")
    qe, kel, wb, w, u, qk, tm, glb, *carried = _gdn_intra_fwd(q, k, v, gates, nh, "a_intra", comm)
    o, vn, sall = _gdn_scan_fwd(qe, kel, wb, u, qk, glb, nh, "a_scan")
    g = _gate_fwd(o, proj, 3 * inner, inner, "a_gate", norm_w=onw)
    h1 = _mm(g, w_out, "nn", t, d, inner, out_dtype=F32, name="a_out", res=x)
    return h1, (hn, proj, ab, gates, q, k, v, qe, kel, wb, w, u, qk, tm, glb, o, vn, sall, g), carried


def _layer_a_bwd(dh1, dh1b, x, nw, w_main, w_ab, conv_w, alog_row, dt_row, onw, w_out, nh, saved, comm, own_comm):
    hn, proj, ab, gates, q, k, v, qe, kel, wb, w, u, qk, tm, glb, o, vn, sall, g = saved
    t, d = x.shape
    inner = w_out.shape[0]
    dg = _mm(dh1b, w_out, "nt", t, inner, d, out_dtype=F32, name="a_dgate")
    dw_out = _mm(g, dh1b, "tn", inner, d, t, out_dtype=F32, name="a_dwout")
    do, dz, donw = _gate_bwd(dg, o, proj, 3 * inner, inner, "a_gate_bwd", F32, norm_w=onw)
    dvn, dw, dqe, dkel, dqk, dglb = _gdn_scan_bwd(do, qe, kel, wb, vn, qk, glb, sall, nh, "a_scan_bwd")
    dq, dk, dv, dgates, *carried = _gdn_intra_bwd(q, k, v, gates, tm, w, u, dvn, dw, dqe, dkel, dqk, dglb, nh,
                                                  "a_intra_bwd", comm)
    dxq, dcq = _conv_bwd(dq, proj, conv_w, 0, inner, "q", "a_conv_q_bwd")
    dxk, dck = _conv_bwd(dk, proj, conv_w, inner, inner, "k", "a_conv_k_bwd")
    dxv, dcv = _conv_bwd(dv, proj, conv_w, 2 * inner, inner, "v", "a_conv_v_bwd")
    dab, dsmall = _gates_bwd(ab, alog_row, dt_row, dgates, nh, "a_gates_bwd")
    dproj = jnp.concatenate([dxq, dxk, dxv, dz], axis=1)
    dw_main = _mm(hn, dproj, "tn", d, 4 * inner, t, out_dtype=F32, name="a_dwin")
    dw_ab = _mm(hn, dab, "tn", d, LANES, t, out_dtype=F32, name="a_dwin_ab")
    dconv = jnp.concatenate([dcq[:CONV_K], dck[:CONV_K], dcv[:CONV_K]], axis=1)
    dhn = _mm(dab, w_ab, "nt", t, d, LANES, out_dtype=F32, name="a_dhn_ab")
    own = own_comm(dw_main, dw_ab, dw_out, dconv)
    dhn = _mm(dproj, w_main, "nt", t, d, 4 * inner, out_dtype=F32, name="a_dhn", res=dhn, comm=own)
    dhn, carried_own = dhn if own is not None else (dhn, [])
    dx, _, dnw = _rms_bwd(x, nw, dhn, dh1, "a_rms_bwd")
    return dx, dnw, dsmall, donw, carried, carried_own


def _rows_of(a, rows):
    flat = a.reshape(-1)
    return jnp.pad(flat, (0, rows * LANES - flat.shape[0])).reshape(rows, LANES)


def _to_slabs(g, axis):
    shape = g.shape[:axis] + (N_DEV, g.shape[axis] // N_DEV) + g.shape[axis + 1:]
    return jnp.moveaxis(g.reshape(shape), axis, 0)


def _from_slabs(s, axis):
    m = jnp.moveaxis(s, 0, axis)
    return m.reshape(m.shape[:axis] + (m.shape[axis] * m.shape[axis + 1],) + m.shape[axis + 2:])


def kernel(x, norm_w, a_w_in, a_conv_w, a_a_log, a_dt_bias, a_out_norm_w, a_w_out, b_w_in, b_q_norm_w, b_k_norm_w, b_rel_bias, b_w_out, loss_target, m_norm_w, m_a_w_in, m_a_conv_w, m_a_a_log, m_a_dt_bias, m_a_out_norm_w, m_a_w_out, m_b_w_in, m_b_q_norm_w, m_b_k_norm_w, m_b_rel_bias, m_b_w_out, v_norm_w, v_a_w_in, v_a_conv_w, v_a_a_log, v_a_dt_bias, v_a_out_norm_w, v_a_w_out, v_b_w_in, v_b_q_norm_w, v_b_k_norm_w, v_b_rel_bias, v_b_w_out):
    xs, target = x[0], loss_target[0]
    nh = a_a_log.shape[-1]
    inner = N_DEV * a_w_out.shape[1]

    ga_in, ga_out, g_conv = _comm_call(
        _Comm("gather", [a_w_in[0].astype(BF16), a_w_out[0].astype(BF16), a_conv_w[0]]), "gather_a_weights")
    wa_in = _from_slabs(ga_in, 1)
    wa_main = wa_in[:, :4 * inner]
    wa_ab = jnp.pad(wa_in[:, 4 * inner:], ((0, 0), (0, LANES - 2 * nh)))
    wa_out = _from_slabs(ga_out, 0)
    conv_w = _from_slabs(g_conv, 1)
    nw0, nw1 = norm_w[0:1], norm_w[1:2]
    alog_row = jnp.pad(a_a_log, ((0, 0), (0, LANES - nh)))
    dt_row = jnp.pad(a_dt_bias, ((0, 0), (0, LANES - nh)))

    h1, saved_a, (gb_in, gb_out) = _layer_a_fwd(
        xs, nw0, wa_main, wa_ab, conv_w, alog_row, dt_row, a_out_norm_w, wa_out, nh,
        _Comm("gather", [b_w_in[0].astype(BF16), b_w_out[0].astype(BF16)]))
    wb_in = _from_slabs(gb_in, 1)
    wb_out = _from_slabs(gb_out, 0)
    bias = _bias_tiles(_pad_rel_bias(b_rel_bias[0]), "b_bias_tiles")
    h2, saved_b = _layer_b_fwd(h1, nw1, wb_in, b_q_norm_w, b_k_norm_w, bias, wb_out)
    dh2, dh2b, loss_row = _loss_grad(h2, target, "loss")

    dh1, dh1b, dnw1, dwb_in, dqw, dkw, drb, dwb_out = _layer_b_bwd(
        dh2, dh2b, h1, nw1, wb_in, b_q_norm_w, b_k_norm_w, bias, wb_out, saved_b)
    def exchange_a(dwa_main, dwa_ab, dwa_out, dconv):
        dwa_in = jnp.concatenate([dwa_main, dwa_ab[:, :2 * nh]], axis=1)
        return _Comm("exchange", [_to_slabs(dwa_in, 1).astype(BF16), _to_slabs(dwa_out, 0).astype(BF16),
                                  _to_slabs(dconv, 1)])

    dx, dnw0, dsmall, donw, (pb_in, pb_out), (pa_in, pa_out, p_conv) = _layer_a_bwd(
        dh1, dh1b, xs, nw0, wa_main, wa_ab, conv_w, alog_row, dt_row, a_out_norm_w, wa_out, nh, saved_a,
        _Comm("exchange", [_to_slabs(dwb_in, 1).astype(BF16), _to_slabs(dwb_out, 0).astype(BF16)]), exchange_a)
    big = {}
    for name, p, w, m, v in (("a_w_in", pa_in, a_w_in, m_a_w_in, v_a_w_in),
                             ("a_w_out", pa_out, a_w_out, m_a_w_out, v_a_w_out),
                             ("b_w_in", pb_in, b_w_in, m_b_w_in, v_b_w_in),
                             ("b_w_out", pb_out, b_w_out, m_b_w_out, v_b_w_out),
                             ("a_conv_w", p_conv, a_conv_w, m_a_conv_w, v_a_conv_w)):
        big[name] = [o[None] for o in _adamw(p, w[0], m[0], v[0], "adamw_" + name)]

    small = (("norm_w", norm_w, m_norm_w, v_norm_w, jnp.concatenate([dnw0, dnw1], axis=0)),
             ("a_a_log", a_a_log, m_a_a_log, v_a_a_log, dsmall[0:1, :nh]),
             ("a_dt_bias", a_dt_bias, m_a_dt_bias, v_a_dt_bias, dsmall[1:2, :nh]),
             ("a_out_norm_w", a_out_norm_w, m_a_out_norm_w, v_a_out_norm_w, donw),
             ("b_q_norm_w", b_q_norm_w, m_b_q_norm_w, v_b_q_norm_w, dqw),
             ("b_k_norm_w", b_k_norm_w, m_b_k_norm_w, v_b_k_norm_w, dkw),
             ("b_rel_bias", b_rel_bias, m_b_rel_bias, v_b_rel_bias, drb))
    rows = [8 * (-(-w.size // (8 * LANES))) for _, w, _, _, _ in small]
    pack = lambda arrs: jnp.concatenate([_rows_of(a, r) for a, r in zip(arrs, rows)] + [jnp.zeros((8, LANES), F32)], axis=0)
    g_pack = jnp.concatenate([_rows_of(g, r) for (_, _, _, _, g), r in zip(small, rows)]
                             + [jnp.broadcast_to(loss_row, (8, LANES))], axis=0)
    (g_all,) = _comm_call(_Comm("gather", [g_pack]), "gather_small_grads")
    outs_small = _adamw(g_all, pack([s[1] for s in small]), pack([s[2] for s in small]),
                        pack([s[3] for s in small]), "adamw_small")
    start = 0
    for (name, w, _, _, _), r in zip(small, rows):
        big[name] = [o[start:start + r].reshape(-1)[:w.size].reshape(w.shape) for o in outs_small]
        start += r
    loss = outs_small[0][start, 0]

    order = ("norm_w", "a_w_in", "a_conv_w", "a_a_log", "a_dt_bias", "a_out_norm_w", "a_w_out", "b_w_in",
             "b_q_norm_w", "b_k_norm_w", "b_rel_bias", "b_w_out")
    return (loss, dx[None]) + tuple(big[n][i] for i in range(4) for n in order)
```

```python
import functools
import math

import jax
import jax.numpy as jnp
from jax import lax
from jax.experimental import pallas as pl
from jax.experimental.pallas import tpu as pltpu

F32 = jnp.float32
BF16 = jnp.bfloat16
MESH_IDS = pl.DeviceIdType.MESH
N_DEV = 8
CHUNK = 64
HEAD_DIM = 128
EPS = 1e-6
CONV_K = 4
LEFT_CHUNKS = 8
REL_CLIP = 256
Q_TILE = LEFT_CHUNKS * CHUNK
ADAM_LR = 0.001
ADAM_B1 = 0.9
ADAM_B2 = 0.999
ADAM_EPS = 1e-08
ADAM_WD = 0.01
ADAM_STEP = 10
NEG_BIG = -1e30
VMEM_LIMIT_BYTES = 56 * 1024 * 1024
HIGHEST = lax.Precision.HIGHEST
ANY = pl.BlockSpec(memory_space=pl.ANY)


def _cparams(*sem):
    return pltpu.CompilerParams(dimension_semantics=tuple(sem), vmem_limit_bytes=VMEM_LIMIT_BYTES)


NN, NT, TN = (((1,), (0,)), ((), ())), (((1,), (1,)), ((), ())), (((0,), (0,)), ((), ()))
BNN, BNT, BTN = (((2,), (1,)), ((0,), (0,))), (((2,), (2,)), ((0,), (0,))), (((1,), (1,)), ((0,), (0,)))


def _dot(a, b, dims, precision=None):
    return lax.dot_general(a, b, dims, preferred_element_type=F32, precision=precision)


def _nn(a, b, precision=None):
    return _dot(a, b, NN, precision)


def _nt(a, b, precision=None):
    return _dot(a, b, NT, precision)


def _tn(a, b, precision=None):
    return _dot(a, b, TN, precision)


def _bf(x):
    return x.astype(BF16)


def _split(x, pieces=2):
    out = []
    for _ in range(pieces - 1):
        hi = x.astype(BF16)
        out.append(hi)
        x = x - hi.astype(F32)
    return out + [x.astype(BF16)]


def _dot3(a, b, dims):
    (ah, al), (bh, bl) = _split(a), _split(b)
    return _dot(ah, bh, dims) + (_dot(ah, bl, dims) + _dot(al, bh, dims))


def _dot_exact(a, b, dims, split_b):
    if split_b:
        a = a.astype(BF16)
        parts = [_dot(a, p, dims) for p in _split(b, 3)]
    else:
        b = b.astype(BF16)
        parts = [_dot(p, b, dims) for p in _split(a, 3)]
    return parts[0] + (parts[1] + parts[2])


def _sigmoid(x):
    return 1.0 / (1.0 + jnp.exp(-x))


def _silu(x):
    return x * _sigmoid(x)


def _dsilu(x):
    s = _sigmoid(x)
    return s * (1.0 + x * (1.0 - s))


def _my_pos():
    return lax.axis_index("x"), lax.axis_index("y"), lax.axis_index("c")


def _peers(x, y, c):
    def flip(v, f):
        return 1 - v if f else v

    return [(flip(x, kx), flip(y, ky), flip(c, kc)) for kx in (0, 1) for ky in (0, 1) for kc in (0, 1)][1:]


def _lin(p):
    return 4 * p[0] + 2 * p[1] + p[2]


class _Comm:
    def __init__(self, kind, arrays):
        self.kind, self.arrays, self.n = kind, list(arrays), len(arrays)

    def out_shape(self):
        lead = (N_DEV,) if self.kind == "gather" else ()
        return [jax.ShapeDtypeStruct(lead + a.shape, a.dtype) for a in self.arrays]

    def scratch(self):
        return [pltpu.SemaphoreType.DMA((7 * self.n,)), pltpu.SemaphoreType.DMA((7 * self.n,)),
                pltpu.SemaphoreType.DMA((self.n,))]

    def _copies(self, ins, outs, sems, arrivals):
        send_sems, recv_sems, local_sems = sems
        x, y, c = _my_pos()
        me = _lin((x, y, c))
        gather = self.kind == "gather"
        mine = [ins[t] if gather else ins[t].at[me] for t in range(self.n)]
        remote = []
        for k, peer in enumerate(_peers(x, y, c)):
            for t in range(self.n):
                if arrivals:
                    src, dst = mine[t], outs[t].at[_lin(peer)]
                else:
                    src, dst = (ins[t] if gather else ins[t].at[_lin(peer)]), outs[t].at[me]
                remote.append(pltpu.make_async_remote_copy(
                    src_ref=src, dst_ref=dst, send_sem=send_sems.at[k * self.n + t],
                    recv_sem=recv_sems.at[k * self.n + t], device_id=peer, device_id_type=MESH_IDS))
        if arrivals:
            return remote
        return [pltpu.make_async_copy(mine[t], outs[t].at[me], local_sems.at[t]) for t in range(self.n)], remote

    def start(self, ins, outs, sems):
        local, sends = self._copies(ins, outs, sems, False)
        for cp in local + sends:
            cp.start()

    def finish(self, ins, outs, sems):
        for cp in self._copies(ins, outs, sems, True):
            cp.wait_recv()
        local, sends = self._copies(ins, outs, sems, False)
        for cp in sends:
            cp.wait_send()
        for cp in local:
            cp.wait()


def _xor(a, b):
    return a + b - 2 * a * b


class _RoutedGather(_Comm):
    def __init__(self, arrays):
        super().__init__("gather", arrays)

    def _plan(self, outs, sems):
        send_sems, recv_sems, _ = sems
        x, y, c = _my_pos()
        sib, xn, yn, dg = (x, y, 1 - c), (1 - x, y, c), (x, 1 - y, c), (1 - x, 1 - y, c)
        via = (_xor(x, 1 - c), _xor(y, c), c)
        onto = (_xor(x, c), _xor(y, 1 - c), c)
        routes = [(None, sib, sib), (None, xn, xn), (None, yn, yn), (via, onto, dg),
                  (xn, sib, (1 - x, y, 1 - c)), (yn, sib, (x, 1 - y, 1 - c)), (dg, sib, (1 - x, 1 - y, 1 - c))]

        def copy(k, t, src, slot, target):
            return pltpu.make_async_remote_copy(
                src_ref=src, dst_ref=outs[t].at[slot], send_sem=send_sems.at[k * self.n + t],
                recv_sem=recv_sems.at[k * self.n + t], device_id=target, device_id_type=MESH_IDS)

        return (x, y, c), routes, copy

    def start(self, ins, outs, sems):
        me, routes, copy = self._plan(outs, sems)
        for t in range(self.n):
            pltpu.make_async_copy(ins[t], outs[t].at[_lin(me)], sems[2].at[t]).start()
            for k in range(3):
                copy(k, t, ins[t], _lin(me), routes[k][1]).start()

    def finish(self, ins, outs, sems):
        me, routes, copy = self._plan(outs, sems)

        def arrived(k):
            for t in range(self.n):
                copy(k, t, ins[t], _lin(routes[k][2]), me).wait_recv()

        def pass_on(k):
            for t in range(self.n):
                copy(k, t, outs[t].at[_lin(routes[k][0])], _lin(routes[k][0]), routes[k][1]).start()

        arrived(1)
        arrived(2)
        for k in (3, 4, 5):
            pass_on(k)
        arrived(3)
        pass_on(6)
        for k in (0, 4, 5, 6):
            arrived(k)
        for t in range(self.n):
            for k in range(7):
                src = ins[t] if k < 3 else outs[t].at[_lin(routes[k][0])]
                copy(k, t, src, _lin(me), routes[k][1]).wait_send()
            pltpu.make_async_copy(ins[t], outs[t].at[_lin(me)], sems[2].at[t]).wait()


def _comm_call(comm, name):
    n = comm.n

    def body(*refs):
        ins, outs, sems = refs[:n], refs[n:2 * n], refs[2 * n:]
        comm.start(ins, outs, sems)
        comm.finish(ins, outs, sems)

    return pl.pallas_call(
        body, name=name, out_shape=comm.out_shape(), in_specs=[ANY] * n, out_specs=[ANY] * n,
        scratch_shapes=comm.scratch(),
    )(*comm.arrays)


def _grid_call(body, *, name, grid, in_specs, out_specs, out_shape, args, scratch_shapes=(), semantics=None, comm=None):
    if comm is None:
        return pl.pallas_call(
            body, name=name, grid=grid, in_specs=in_specs, out_specs=out_specs, out_shape=out_shape,
            scratch_shapes=list(scratch_shapes), compiler_params=_cparams(*semantics),
        )(*args)
    n_in, n_out, n_sc, n = len(in_specs), len(out_specs), len(scratch_shapes), comm.n

    def full(*refs):
        ins, refs = refs[:n_in], refs[n_in:]
        cins, refs = refs[:n], refs[n:]
        outs, refs = refs[:n_out], refs[n_out:]
        couts, refs = refs[:n], refs[n:]
        scratch, sems = refs[:n_sc], refs[n_sc:]
        ids = [pl.program_id(a) for a in range(len(grid))]
        first = functools.reduce(jnp.logical_and, [i == 0 for i in ids])
        last = functools.reduce(jnp.logical_and, [i == g - 1 for i, g in zip(ids, grid)])

        @pl.when(first)
        def _():
            comm.start(cins, couts, sems)

        body(*ins, *outs, *scratch)

        @pl.when(last)
        def _():
            comm.finish(cins, couts, sems)

    return pl.pallas_call(
        full, name=name, grid=grid, in_specs=list(in_specs) + [ANY] * n, out_specs=list(out_specs) + [ANY] * n,
        out_shape=list(out_shape) + comm.out_shape(), scratch_shapes=list(scratch_shapes) + comm.scratch(),
        compiler_params=_cparams(*(["arbitrary"] * len(grid))),
    )(*(list(args) + comm.arrays))


def _mm(a, b, mode, m, n, k, *, out_dtype, name, tm=1024, tn=1024, tk=2048,
        a_m0=0, a_k0=0, b_n0=0, b_k0=0, res=None, comm=None):
    tm, tn, tk = min(tm, m), min(tn, n), min(tk, k)
    nm, nn, nk = m // tm, n // tn, k // tk
    assert nm * tm == m and nn * tn == n and nk * tk == k
    am, ak, bn, bk = a_m0 // tm, a_k0 // tk, b_n0 // tn, b_k0 // tk
    assert am * tm == a_m0 and ak * tk == a_k0 and bn * tn == b_n0 and bk * tk == b_k0
    if mode == "tn":
        a_spec = pl.BlockSpec((tk, tm), lambda i, j, q: (q + ak, i + am))
        a_dims = (0,)
    else:
        a_spec = pl.BlockSpec((tm, tk), lambda i, j, q: (i + am, q + ak))
        a_dims = (1,)
    if mode == "nt":
        b_spec = pl.BlockSpec((tn, tk), lambda i, j, q: (j + bn, q + bk))
        b_dims = (1,)
    else:
        b_spec = pl.BlockSpec((tk, tn), lambda i, j, q: (q + bk, j + bn))
        b_dims = (0,)
    o_spec = pl.BlockSpec((tm, tn), lambda i, j, q: (i, j))
    has_res = res is not None

    def body(*refs):
        a_ref, b_ref = refs[0], refs[1]
        res_ref = refs[2] if has_res else None
        o_ref = refs[2 + has_res]
        p = _dot(a_ref[...], b_ref[...], ((a_dims, b_dims), ((), ())))

        def finish(total):
            if has_res:
                total = total + res_ref[...].astype(F32)
            o_ref[...] = total.astype(out_dtype)

        if nk == 1:
            finish(p)
        else:
            acc_ref = refs[3 + has_res]
            q = pl.program_id(2)

            @pl.when(q == 0)
            def _():
                acc_ref[...] = p

            @pl.when(q > 0)
            def _():
                acc_ref[...] += p

            @pl.when(q == nk - 1)
            def _():
                finish(acc_ref[...])

    out, *carried = _grid_call(
        body, name=name, grid=(nm, nn, nk),
        in_specs=[a_spec, b_spec] + ([o_spec] if has_res else []),
        out_specs=[o_spec], out_shape=[jax.ShapeDtypeStruct((m, n), out_dtype)],
        scratch_shapes=[pltpu.VMEM((tm, tn), F32)] if nk > 1 else [],
        args=[a, b] + ([res] if has_res else []), semantics=("parallel", "parallel", "arbitrary"), comm=comm)
    return out if comm is None else (out, carried)


def _rms_fwd(x, w, name, tr=512, comm=None):
    t, d = x.shape
    tr = min(tr, t)

    def body(x_ref, w_ref, o_ref):
        xv = x_ref[...]
        r = lax.rsqrt(jnp.mean(xv * xv, axis=-1, keepdims=True) + EPS)
        o_ref[...] = (xv * r * w_ref[...]).astype(BF16)

    out, *carried = _grid_call(
        body, name=name, grid=(t // tr,),
        in_specs=[pl.BlockSpec((tr, d), lambda i: (i, 0)), pl.BlockSpec((1, d), lambda i: (0, 0))],
        out_specs=[pl.BlockSpec((tr, d), lambda i: (i, 0))],
        out_shape=[jax.ShapeDtypeStruct((t, d), BF16)], args=(x, w), semantics=("parallel",), comm=comm)
    return out if comm is None else (out, carried)


def _rms_bwd(x, w, dy, dres, name, tr=256):
    t, d = x.shape
    tr = min(tr, t)

    def body(x_ref, w_ref, dy_ref, dres_ref, dx_ref, dxb_ref, dw_ref):
        xv = x_ref[...]
        dyv = dy_ref[...].astype(F32)
        r = lax.rsqrt(jnp.mean(xv * xv, axis=-1, keepdims=True) + EPS)
        gy = dyv * w_ref[...]
        proj = jnp.sum(gy * xv, axis=-1, keepdims=True) * (1.0 / d)
        dx = dres_ref[...] + r * gy - xv * (r * r * r) * proj
        dx_ref[...] = dx
        dxb_ref[...] = dx.astype(BF16)
        part = jnp.sum(dyv * xv * r, axis=0, keepdims=True)

        @pl.when(pl.program_id(0) == 0)
        def _():
            dw_ref[...] = part

        @pl.when(pl.program_id(0) > 0)
        def _():
            dw_ref[...] += part

    row = pl.BlockSpec((tr, d), lambda i: (i, 0))
    vec = pl.BlockSpec((1, d), lambda i: (0, 0))
    return pl.pallas_call(
        body, name=name, grid=(t // tr,),
        in_specs=[row, vec, row, row], out_specs=[row, row, vec],
        out_shape=[jax.ShapeDtypeStruct((t, d), F32), jax.ShapeDtypeStruct((t, d), BF16),
                   jax.ShapeDtypeStruct((1, d), F32)],
        compiler_params=_cparams("arbitrary"),
    )(x, w, dy, dres)


def _adamw(parts, w, m, v, name, tr=128):
    r, c = w.shape
    tr = tr if r % tr == 0 else r
    c1 = 1.0 - ADAM_B1 ** ADAM_STEP
    c2 = 1.0 - ADAM_B2 ** ADAM_STEP

    def body(p_ref, w_ref, m_ref, v_ref, g_ref, d_ref, nm_ref, nv_ref):
        g = p_ref[0].astype(F32)
        for s in range(1, N_DEV):
            g = g + p_ref[s].astype(F32)
        nm = ADAM_B1 * m_ref[...] + (1.0 - ADAM_B1) * g
        nv = ADAM_B2 * v_ref[...] + (1.0 - ADAM_B2) * (g * g)
        m_hat = nm / c1
        v_hat = nv / c2
        g_ref[...] = g
        d_ref[...] = -ADAM_LR * (m_hat / (jnp.sqrt(v_hat) + ADAM_EPS) + ADAM_WD * w_ref[...])
        nm_ref[...] = nm
        nv_ref[...] = nv

    blk = pl.BlockSpec((tr, c), lambda i: (i, 0))
    return pl.pallas_call(
        body, name=name, grid=(r // tr,),
        in_specs=[pl.BlockSpec((N_DEV, tr, c), lambda i: (0, i, 0)), blk, blk, blk],
        out_specs=[blk] * 4, out_shape=[jax.ShapeDtypeStruct((r, c), F32)] * 4,
        compiler_params=_cparams("parallel"),
    )(parts, w, m, v)


def _heads_of(x, nh):
    return [x[:, h * HEAD_DIM:(h + 1) * HEAD_DIM] for h in range(nh)]


def _headnorm_fwd(proj, w, col0, inner, name, tr=512, hb=4):
    t = proj.shape[0]
    tr = min(tr, t)
    wc = hb * HEAD_DIM
    c0 = col0 // wc

    def body(x_ref, w_ref, o_ref):
        outs = []
        for xh in _heads_of(x_ref[...], hb):
            r = lax.rsqrt(jnp.mean(xh * xh, axis=-1, keepdims=True) + EPS)
            outs.append((xh * r * w_ref[...]).astype(BF16))
        o_ref[...] = jnp.concatenate(outs, axis=1)

    return pl.pallas_call(
        body, name=name, grid=(t // tr, inner // wc),
        in_specs=[pl.BlockSpec((tr, wc), lambda i, j: (i, j + c0)), pl.BlockSpec((1, HEAD_DIM), lambda i, j: (0, 0))],
        out_specs=pl.BlockSpec((tr, wc), lambda i, j: (i, j)),
        out_shape=jax.ShapeDtypeStruct((t, inner), BF16),
        compiler_params=_cparams("parallel", "parallel"),
    )(proj, w)


def _headnorm_bwd(dy, proj, w, col0, inner, name, tr=512, hb=4):
    t = proj.shape[0]
    tr = min(tr, t)
    wc = hb * HEAD_DIM
    c0 = col0 // wc

    def body(dy_ref, x_ref, w_ref, dx_ref, dw_ref):
        outs = []
        part = jnp.zeros((1, HEAD_DIM), F32)
        for dyh, xh in zip(_heads_of(dy_ref[...], hb), _heads_of(x_ref[...], hb)):
            r = lax.rsqrt(jnp.mean(xh * xh, axis=-1, keepdims=True) + EPS)
            gy = dyh * w_ref[...]
            pr = jnp.sum(gy * xh, axis=-1, keepdims=True) * (1.0 / HEAD_DIM)
            outs.append((r * gy - xh * (r * r * r) * pr).astype(BF16))
            part = part + jnp.sum(dyh * xh * r, axis=0, keepdims=True)
        dx_ref[...] = jnp.concatenate(outs, axis=1)
        first = (pl.program_id(0) == 0) & (pl.program_id(1) == 0)

        @pl.when(first)
        def _():
            dw_ref[...] = part

        @pl.when(jnp.logical_not(first))
        def _():
            dw_ref[...] += part

    blk = pl.BlockSpec((tr, wc), lambda i, j: (i, j))
    return pl.pallas_call(
        body, name=name, grid=(t // tr, inner // wc),
        in_specs=[blk, pl.BlockSpec((tr, wc), lambda i, j: (i, j + c0)), pl.BlockSpec((1, HEAD_DIM), lambda i, j: (0, 0))],
        out_specs=[blk, pl.BlockSpec((1, HEAD_DIM), lambda i, j: (0, 0))],
        out_shape=[jax.ShapeDtypeStruct((t, inner), BF16), jax.ShapeDtypeStruct((1, HEAD_DIM), F32)],
        compiler_params=_cparams("arbitrary", "arbitrary"),
    )(dy, proj, w)


def _gate_fwd(o, proj, zcol0, inner, name, norm_w=None, tr=512, hb=4):
    t = o.shape[0]
    tr = min(tr, t)
    wc = hb * HEAD_DIM
    c0 = zcol0 // wc
    has_w = norm_w is not None

    def body(*refs):
        o_ref, z_ref = refs[0], refs[1]
        out_ref = refs[2 + has_w]
        outs = []
        for oh, zh in zip(_heads_of(o_ref[...], hb), _heads_of(z_ref[...], hb)):
            if has_w:
                r = lax.rsqrt(jnp.mean(oh * oh, axis=-1, keepdims=True) + EPS)
                oh = oh * r * refs[2][...]
            outs.append((oh * _silu(zh)).astype(BF16))
        out_ref[...] = jnp.concatenate(outs, axis=1)

    blk = pl.BlockSpec((tr, wc), lambda i, j: (i, j))
    vec = pl.BlockSpec((1, HEAD_DIM), lambda i, j: (0, 0))
    return pl.pallas_call(
        body, name=name, grid=(t // tr, inner // wc),
        in_specs=[blk, pl.BlockSpec((tr, wc), lambda i, j: (i, j + c0))] + ([vec] if has_w else []),
        out_specs=blk, out_shape=jax.ShapeDtypeStruct((t, inner), BF16),
        compiler_params=_cparams("parallel", "parallel"),
    )(*([o, proj] + ([norm_w] if has_w else [])))


def _gate_bwd(dg, o, proj, zcol0, inner, name, do_dtype, norm_w=None, tr=512, hb=4):
    t = o.shape[0]
    tr = min(tr, t)
    wc = hb * HEAD_DIM
    c0 = zcol0 // wc
    has_w = norm_w is not None

    def body(*refs):
        dg_ref, o_ref, z_ref = refs[0], refs[1], refs[2]
        do_ref, dz_ref = refs[3 + has_w], refs[4 + has_w]
        dos, dzs = [], []
        part = jnp.zeros((1, HEAD_DIM), F32)
        for dgh, oh, zh in zip(_heads_of(dg_ref[...], hb), _heads_of(o_ref[...], hb), _heads_of(z_ref[...], hb)):
            dy = dgh * _silu(zh)
            if has_w:
                w = refs[3][...]
                r = lax.rsqrt(jnp.mean(oh * oh, axis=-1, keepdims=True) + EPS)
                on = oh * r
                dzs.append((dgh * on * w * _dsilu(zh)).astype(BF16))
                gy = dy * w
                pr = jnp.sum(gy * oh, axis=-1, keepdims=True) * (1.0 / HEAD_DIM)
                dos.append((r * gy - oh * (r * r * r) * pr).astype(do_dtype))
                part = part + jnp.sum(dy * on, axis=0, keepdims=True)
            else:
                dzs.append((dgh * oh * _dsilu(zh)).astype(BF16))
                dos.append(dy.astype(do_dtype))
        do_ref[...] = jnp.concatenate(dos, axis=1)
        dz_ref[...] = jnp.concatenate(dzs, axis=1)
        if has_w:
            dw_ref = refs[6]
            first = (pl.program_id(0) == 0) & (pl.program_id(1) == 0)

            @pl.when(first)
            def _():
                dw_ref[...] = part

            @pl.when(jnp.logical_not(first))
            def _():
                dw_ref[...] += part

    blk = pl.BlockSpec((tr, wc), lambda i, j: (i, j))
    vec = pl.BlockSpec((1, HEAD_DIM), lambda i, j: (0, 0))
    return pl.pallas_call(
        body, name=name, grid=(t // tr, inner // wc),
        in_specs=[blk, blk, pl.BlockSpec((tr, wc), lambda i, j: (i, j + c0))] + ([vec] if has_w else []),
        out_specs=[blk, blk] + ([vec] if has_w else []),
        out_shape=[jax.ShapeDtypeStruct((t, inner), do_dtype), jax.ShapeDtypeStruct((t, inner), BF16)]
        + ([jax.ShapeDtypeStruct((1, HEAD_DIM), F32)] if has_w else []),
        compiler_params=_cparams("arbitrary", "arbitrary"),
    )(*([dg, o, proj] + ([norm_w] if has_w else [])))


def _loss_grad(h, target, name, tr=512):
    t, d = h.shape
    tr = min(tr, t)

    def body(h_ref, t_ref, g_ref, gb_ref, l_ref):
        e = h_ref[...] - t_ref[...]
        g = e * (1.0 / d)
        g_ref[...] = g
        gb_ref[...] = g.astype(BF16)
        part = jnp.zeros((1, HEAD_DIM), F32) + 0.5 * jnp.sum(jnp.sum(e * e, axis=-1, keepdims=True) * (1.0 / d))

        @pl.when(pl.program_id(0) == 0)
        def _():
            l_ref[...] = part

        @pl.when(pl.program_id(0) > 0)
        def _():
            l_ref[...] += part

    row = pl.BlockSpec((tr, d), lambda i: (i, 0))
    return pl.pallas_call(
        body, name=name, grid=(t // tr,),
        in_specs=[row, row], out_specs=[row, row, pl.BlockSpec((1, HEAD_DIM), lambda i: (0, 0))],
        out_shape=[jax.ShapeDtypeStruct((t, d), F32), jax.ShapeDtypeStruct((t, d), BF16),
                   jax.ShapeDtypeStruct((1, HEAD_DIM), F32)],
        compiler_params=_cparams("arbitrary"),
    )(h, target)


N_REL = 2 * REL_CLIP + 1
REL_PAD = 640
WIN = 2 * Q_TILE


def _diag_onehot():
    i = lax.broadcasted_iota(jnp.int32, (REL_PAD, WIN), 0)
    j = lax.broadcasted_iota(jnp.int32, (REL_PAD, WIN), 1)
    rel = jnp.where(j < Q_TILE + CHUNK, Q_TILE - j, Q_TILE + WIN - j)
    used = (j < Q_TILE + CHUNK) | (j > WIN - CHUNK)
    idx = jnp.clip(rel, -REL_CLIP, REL_CLIP) + REL_CLIP
    return jnp.where(used & (i == idx), 1.0, 0.0).astype(F32)


def _band_mask():
    r = lax.broadcasted_iota(jnp.int32, (Q_TILE, WIN), 0) // CHUNK
    kc = lax.broadcasted_iota(jnp.int32, (Q_TILE, WIN), 1) // CHUNK - LEFT_CHUNKS
    return (kc <= r) & (kc >= r - LEFT_CHUNKS)


def _bias_tiles(rel_bias_pad, name):
    nh = rel_bias_pad.shape[0]

    def body(rb_ref, o_ref):
        dvec = _nn(rb_ref[...], _diag_onehot(), HIGHEST)[0:1, :]
        tile = pltpu.roll(jnp.broadcast_to(dvec, (Q_TILE, WIN)), 0, 1, stride=1, stride_axis=0)
        o_ref[...] = jnp.where(_band_mask(), tile, NEG_BIG)

    return pl.pallas_call(
        body, name=name, grid=(nh,),
        in_specs=[pl.BlockSpec((None, 8, REL_PAD), lambda h: (h, 0, 0))],
        out_specs=pl.BlockSpec((None, Q_TILE, WIN), lambda h: (h, 0, 0)),
        out_shape=jax.ShapeDtypeStruct((nh, Q_TILE, WIN), F32),
        compiler_params=_cparams("parallel"),
    )(rel_bias_pad)


def _bias_grad(dtile, name):
    nh = dtile.shape[0]

    def body(d_ref, o_ref):
        ri = lax.broadcasted_iota(jnp.int32, (Q_TILE, Q_TILE), 0)
        ci = lax.broadcasted_iota(jnp.int32, (Q_TILE, Q_TILE), 1)
        flip = jnp.where(ri + ci == Q_TILE - 1, 1.0, 0.0).astype(F32)
        rev = _nn(flip, d_ref[...], HIGHEST)
        rolled = pltpu.roll(rev, WIN - (Q_TILE - 1), 1, stride=1, stride_axis=0)
        diag = jnp.broadcast_to(jnp.sum(rolled, axis=0, keepdims=True), (8, WIN))
        o_ref[...] = _nt(diag, _diag_onehot(), HIGHEST)

    return pl.pallas_call(
        body, name=name, grid=(nh,),
        in_specs=[pl.BlockSpec((None, Q_TILE, WIN), lambda h: (h, 0, 0))],
        out_specs=pl.BlockSpec((None, 8, REL_PAD), lambda h: (h, 0, 0)),
        out_shape=jax.ShapeDtypeStruct((nh, 8, REL_PAD), F32),
        compiler_params=_cparams("parallel"),
    )(dtile)


GROUP = 2 * CHUNK
BAND = Q_TILE + GROUP


def _band_rows(r0_ref, r1_ref, g):
    return jnp.concatenate([r0_ref[GROUP * g:, :], r1_ref[:GROUP * (g + 1), :]], axis=0)


N_GROUPS = Q_TILE // GROUP


def _groups(ref):
    return jnp.stack([ref[GROUP * g:GROUP * (g + 1), :] for g in range(N_GROUPS)])


def _bands(r0_ref, r1_ref):
    return jnp.stack([_band_rows(r0_ref, r1_ref, g) for g in range(N_GROUPS)])


def _group_probs(q, kw, b_ref, first_tile):
    bias = jnp.stack([b_ref[GROUP * g:GROUP * (g + 1), GROUP * g:GROUP * g + BAND] for g in range(N_GROUPS)])
    s = _dot(q, kw, BNT) * (HEAD_DIM ** -0.5) + bias
    col = (lax.broadcasted_iota(jnp.int32, (N_GROUPS, GROUP, BAND), 2)
           + GROUP * lax.broadcasted_iota(jnp.int32, (N_GROUPS, GROUP, BAND), 0))
    s = jnp.where(first_tile & (col < Q_TILE), NEG_BIG, s)
    p = jnp.exp(s - jnp.max(s, axis=-1, keepdims=True))
    return p * (1.0 / jnp.sum(p, axis=-1, keepdims=True))


def _attn_fwd(q, k, v, bias, name):
    t, inner = q.shape
    nh, nt = inner // HEAD_DIM, t // Q_TILE

    def body(q_ref, k0_ref, k1_ref, v0_ref, v1_ref, b_ref, o_ref):
        p = _group_probs(_groups(q_ref), _bands(k0_ref, k1_ref), b_ref, pl.program_id(1) == 0)
        o = _dot(_bf(p), _bands(v0_ref, v1_ref), BNN)
        for g in range(N_GROUPS):
            o_ref[GROUP * g:GROUP * (g + 1), :] = o[g]

    cur = pl.BlockSpec((Q_TILE, HEAD_DIM), lambda h, i: (i, h))
    prev = pl.BlockSpec((Q_TILE, HEAD_DIM), lambda h, i: (jnp.maximum(i - 1, 0), h))
    return pl.pallas_call(
        body, name=name, grid=(nh, nt),
        in_specs=[cur, prev, cur, prev, cur, pl.BlockSpec((None, Q_TILE, WIN), lambda h, i: (h, 0, 0))],
        out_specs=cur, out_shape=jax.ShapeDtypeStruct((t, inner), F32),
        compiler_params=_cparams("parallel", "parallel"),
    )(q, k, k, v, v, bias)


def _attn_bwd(q, k, v, do, bias, name):
    t, inner = q.shape
    nh, nt = inner // HEAD_DIM, t // Q_TILE
    scale = HEAD_DIM ** -0.5

    def body(q_ref, k0_ref, k1_ref, v0_ref, v1_ref, do_ref, b_ref, dq_ref, dk_ref, dv_ref, db_ref,
             ck_ref, cv_ref, wk_ref, wv_ref):
        i = pl.program_id(1)

        @pl.when(i == 0)
        def _():
            ck_ref[...] = jnp.zeros(blk, F32)
            cv_ref[...] = jnp.zeros(blk, F32)
            db_ref[...] = jnp.zeros((Q_TILE, WIN), F32)

        @pl.when(i < nt)
        def _():
            wk_ref[...] = jnp.zeros((WIN, HEAD_DIM), F32)
            wv_ref[...] = jnp.zeros((WIN, HEAD_DIM), F32)
            qv, dov = _groups(q_ref), _groups(do_ref)
            kw, vw = _bands(k0_ref, k1_ref), _bands(v0_ref, v1_ref)
            p = _group_probs(qv, kw, b_ref, i == 0)
            dp = _dot(dov, vw, BNT)
            ds = p * (dp - jnp.sum(p * dp, axis=-1, keepdims=True))
            pb, dsb = _bf(p), _bf(ds)
            dq = _dot(dsb, kw, BNN) * scale
            dkw = _dot(dsb, qv, BTN) * scale
            dvw = _dot(pb, dov, BTN)
            for g in range(N_GROUPS):
                rows, cols = slice(GROUP * g, GROUP * (g + 1)), slice(GROUP * g, GROUP * g + BAND)
                db_ref[rows, cols] += ds[g]
                dq_ref[rows, :] = dq[g]
                wk_ref[cols, :] += dkw[g]
                wv_ref[cols, :] += dvw[g]
            dk_ref[...] = ck_ref[...] + wk_ref[:Q_TILE, :]
            dv_ref[...] = (cv_ref[...] + wv_ref[:Q_TILE, :]).astype(BF16)
            ck_ref[...] = wk_ref[Q_TILE:, :]
            cv_ref[...] = wv_ref[Q_TILE:, :]

        @pl.when(i == nt)
        def _():
            dk_ref[...] = ck_ref[...]
            dv_ref[...] = cv_ref[...].astype(BF16)

    blk = (Q_TILE, HEAD_DIM)
    cur = pl.BlockSpec(blk, lambda h, i: (jnp.minimum(i, nt - 1), h))
    prev = pl.BlockSpec(blk, lambda h, i: (jnp.clip(i - 1, 0, nt - 1), h))
    lag = pl.BlockSpec(blk, lambda h, i: (jnp.maximum(i - 1, 0), h))
    tile = pl.BlockSpec((None, Q_TILE, WIN), lambda h, i: (h, 0, 0))
    return pl.pallas_call(
        body, name=name, grid=(nh, nt + 1),
        in_specs=[cur, prev, cur, prev, cur, cur, tile],
        out_specs=[cur, lag, lag, tile],
        out_shape=[jax.ShapeDtypeStruct((t, inner), F32)] * 2 + [jax.ShapeDtypeStruct((t, inner), BF16),
                                                                 jax.ShapeDtypeStruct((nh, Q_TILE, WIN), F32)],
        scratch_shapes=[pltpu.VMEM(blk, F32), pltpu.VMEM(blk, F32),
                        pltpu.VMEM((WIN, HEAD_DIM), F32), pltpu.VMEM((WIN, HEAD_DIM), F32)],
        compiler_params=_cparams("arbitrary", "arbitrary"),
    )(q, k, k, v, v, do, bias)


def _pad_rel_bias(rel_bias):
    nh = rel_bias.shape[0]
    return jnp.broadcast_to(jnp.pad(rel_bias, ((0, 0), (0, REL_PAD - N_REL)))[:, None, :], (nh, 8, REL_PAD))


def _layer_b_fwd(h1, nw, w_in, qw, kw, bias, w_out):
    t, d = h1.shape
    inner = w_out.shape[0]
    hn = _rms_fwd(h1, nw, "b_rms")
    proj = _mm(hn, w_in, "nn", t, 4 * inner, d, out_dtype=F32, name="b_proj")
    qn = _headnorm_fwd(proj, qw, 0, inner, "b_qnorm")
    kn = _headnorm_fwd(proj, kw, inner, inner, "b_knorm")
    vb = proj[:, 2 * inner:3 * inner].astype(BF16)
    o = _attn_fwd(qn, kn, vb, bias, "b_attn")
    g = _gate_fwd(o, proj, 3 * inner, inner, "b_gate")
    h2 = _mm(g, w_out, "nn", t, d, inner, out_dtype=F32, name="b_out", res=h1)
    return h2, (hn, proj, qn, kn, vb, o, g)


def _layer_b_bwd(dh2, dh2b, h1, nw, w_in, qw, kw, bias, w_out, saved):
    hn, proj, qn, kn, vb, o, g = saved
    t, d = h1.shape
    inner = w_out.shape[0]
    dg = _mm(dh2b, w_out, "nt", t, inner, d, out_dtype=F32, name="b_dgate")
    dw_out = _mm(g, dh2b, "tn", inner, d, t, out_dtype=F32, name="b_dwout")
    do, dz = _gate_bwd(dg, o, proj, 3 * inner, inner, "b_gate_bwd", BF16)
    dq, dk, dv, dtile = _attn_bwd(qn, kn, vb, do, bias, "b_attn_bwd")
    dqr, dqw = _headnorm_bwd(dq, proj, qw, 0, inner, "b_qnorm_bwd")
    dkr, dkw = _headnorm_bwd(dk, proj, kw, inner, inner, "b_knorm_bwd")
    dproj = jnp.concatenate([dqr, dkr, dv, dz], axis=1)
    dhn = _mm(dproj, w_in, "nt", t, d, 4 * inner, out_dtype=F32, name="b_dhn")
    dw_in = _mm(hn, dproj, "tn", d, 4 * inner, t, out_dtype=F32, name="b_dwin")
    dh1, dh1b, dnw = _rms_bwd(h1, nw, dhn, dh2, "b_rms_bwd")
    drb = _bias_grad(dtile, "b_bias_grad")[:, 0, :N_REL]
    return dh1, dh1b, dnw, dw_in, dqw, dkw, drb, dw_out


LANES = 128


def _softplus(x):
    return jnp.maximum(x, 0.0) + jnp.log1p(jnp.exp(-jnp.abs(x)))


def _gates_fwd(ab, alog_row, dt_row, nh, name, tr=1024):
    t = ab.shape[0]
    tr = min(tr, t)

    def body(x_ref, al_ref, dt_ref, o_ref):
        x = x_ref[...]
        lane = lax.broadcasted_iota(jnp.int32, x.shape, 1)
        g = -jnp.exp(al_ref[...]) * _softplus(x + dt_ref[...])
        o_ref[...] = jnp.where(lane < nh, g, jnp.where(lane < 2 * nh, _sigmoid(x), 0.0))

    row = pl.BlockSpec((tr, LANES), lambda i: (i, 0))
    vec = pl.BlockSpec((1, LANES), lambda i: (0, 0))
    return pl.pallas_call(
        body, name=name, grid=(t // tr,), in_specs=[row, vec, vec], out_specs=row,
        out_shape=jax.ShapeDtypeStruct((t, LANES), F32), compiler_params=_cparams("parallel"),
    )(ab, alog_row, dt_row)


def _gates_bwd(ab, alog_row, dt_row, dgates, nh, name, tr=1024):
    t = ab.shape[0]
    tr = min(tr, t)
    npart = dgates.shape[0]

    def body(x_ref, al_ref, dt_ref, dg_ref, dx_ref, s_ref):
        x = x_ref[...]
        lane = lax.broadcasted_iota(jnp.int32, x.shape, 1)
        dgt = dg_ref[0]
        for p in range(1, npart):
            dgt = dgt + dg_ref[p]
        ea = jnp.exp(al_ref[...])
        xa = x + dt_ref[...]
        da = jnp.where(lane < nh, dgt * (-ea) * _sigmoid(xa), 0.0)
        beta = _sigmoid(x)
        db = jnp.where((lane >= nh) & (lane < 2 * nh), dgt * beta * (1.0 - beta), 0.0)
        dx_ref[...] = (da + db).astype(BF16)
        dal = jnp.sum(jnp.where(lane < nh, dgt * (-ea) * _softplus(xa), 0.0), axis=0, keepdims=True)
        ddt = jnp.sum(da, axis=0, keepdims=True)
        r8 = lax.broadcasted_iota(jnp.int32, (8, LANES), 0)
        part = jnp.where(r8 == 0, dal, jnp.where(r8 == 1, ddt, 0.0))

        @pl.when(pl.program_id(0) == 0)
        def _():
            s_ref[...] = part

        @pl.when(pl.program_id(0) > 0)
        def _():
            s_ref[...] += part

    row = pl.BlockSpec((tr, LANES), lambda i: (i, 0))
    vec = pl.BlockSpec((1, LANES), lambda i: (0, 0))
    return pl.pallas_call(
        body, name=name, grid=(t // tr,),
        in_specs=[row, vec, vec, pl.BlockSpec((npart, tr, LANES), lambda i: (0, i, 0))],
        out_specs=[row, pl.BlockSpec((8, LANES), lambda i: (0, 0))],
        out_shape=[jax.ShapeDtypeStruct((t, LANES), BF16), jax.ShapeDtypeStruct((8, LANES), F32)],
        compiler_params=_cparams("arbitrary"),
    )(ab, alog_row, dt_row, dgates)


HALO = 8


def _conv_taps(ext, w, rows):
    acc = ext[HALO:HALO + rows] * w[CONV_K - 1:CONV_K]
    for s in range(1, CONV_K):
        acc = acc + pltpu.roll(ext, s, 0)[HALO:HALO + rows] * w[CONV_K - 1 - s:CONV_K - s]
    return acc


def _conv_fwd(proj, conv_w, col0, inner, mode, name, tt=512, hb=4):
    t = proj.shape[0]
    tt = min(tt, t)
    wc = hb * HEAD_DIM
    c0 = col0 // wc
    hpb = tt // HALO

    def body(x_ref, halo_ref, w_ref, o_ref):
        halo = jnp.where(pl.program_id(1) == 0, 0.0, halo_ref[...])
        s = _silu(_conv_taps(jnp.concatenate([halo, x_ref[...]], axis=0), w_ref[...], tt))
        if mode == "v":
            o_ref[...] = s
        else:
            mul = HEAD_DIM ** -0.5 if mode == "q" else 1.0
            o_ref[...] = jnp.concatenate(
                [sh * (lax.rsqrt(jnp.sum(sh * sh, axis=-1, keepdims=True) + EPS) * mul) for sh in _heads_of(s, hb)], axis=1)

    return pl.pallas_call(
        body, name=name, grid=(inner // wc, t // tt),
        in_specs=[pl.BlockSpec((tt, wc), lambda j, i: (i, j + c0)),
                  pl.BlockSpec((HALO, wc), lambda j, i: (jnp.maximum(i * hpb - 1, 0), j + c0)),
                  pl.BlockSpec((CONV_K, wc), lambda j, i: (0, j + c0))],
        out_specs=pl.BlockSpec((tt, wc), lambda j, i: (i, j)),
        out_shape=jax.ShapeDtypeStruct((t, inner), F32),
        compiler_params=_cparams("parallel", "parallel"),
    )(proj, proj, conv_w)


def _conv_bwd(dy, proj, conv_w, col0, inner, mode, name, tt=512, hb=4):
    t = proj.shape[0]
    tt = min(tt, t)
    nt = t // tt
    wc = hb * HEAD_DIM
    c0 = col0 // wc
    hpb = tt // HALO
    rows = tt + HALO

    def body(dy_ref, dyn_ref, x_ref, xp_ref, xn_ref, w_ref, dx_ref, dw_ref):
        i = pl.program_id(1)
        w = w_ref[...]
        xprev = jnp.where(i == 0, 0.0, xp_ref[...])
        ext = jnp.concatenate([xprev, x_ref[...], xn_ref[...]], axis=0)
        c = _conv_taps(ext, w, rows)
        dyv = jnp.concatenate([dy_ref[...], jnp.where(i == nt - 1, 0.0, dyn_ref[...])], axis=0)
        sg = _sigmoid(c)
        s = c * sg
        if mode == "v":
            ds = dyv
        else:
            mul = HEAD_DIM ** -0.5 if mode == "q" else 1.0
            parts = []
            for dyh, sh in zip(_heads_of(dyv, hb), _heads_of(s, hb)):
                r = lax.rsqrt(jnp.sum(sh * sh, axis=-1, keepdims=True) + EPS)
                parts.append(mul * (r * dyh - sh * (r * r * r) * jnp.sum(dyh * sh, axis=-1, keepdims=True)))
            ds = jnp.concatenate(parts, axis=1)
        dc = ds * (sg * (1.0 + c * (1.0 - sg)))
        dx = dc[:tt] * w[CONV_K - 1:CONV_K]
        for sft in range(1, CONV_K):
            dx = dx + pltpu.roll(dc, rows - sft, 0)[:tt] * w[CONV_K - 1 - sft:CONV_K - sft]
        dx_ref[...] = dx.astype(BF16)
        r8 = lax.broadcasted_iota(jnp.int32, (8, wc), 0)
        part = jnp.zeros((8, wc), F32)
        for sft in range(CONV_K):
            xs = ext[HALO:HALO + tt] if sft == 0 else pltpu.roll(ext, sft, 0)[HALO:HALO + tt]
            part = part + jnp.where(r8 == CONV_K - 1 - sft, jnp.sum(dc[:tt] * xs, axis=0, keepdims=True), 0.0)

        @pl.when(i == 0)
        def _():
            dw_ref[...] = part

        @pl.when(i > 0)
        def _():
            dw_ref[...] += part

    cur = lambda off: pl.BlockSpec((tt, wc), lambda j, i: (i, j + off))
    nxt = lambda off: pl.BlockSpec((HALO, wc), lambda j, i: (jnp.minimum((i + 1) * hpb, t // HALO - 1), j + off))
    return pl.pallas_call(
        body, name=name, grid=(inner // wc, nt),
        in_specs=[cur(0), nxt(0), cur(c0),
                  pl.BlockSpec((HALO, wc), lambda j, i: (jnp.maximum(i * hpb - 1, 0), j + c0)), nxt(c0),
                  pl.BlockSpec((CONV_K, wc), lambda j, i: (0, j + c0))],
        out_specs=[pl.BlockSpec((tt, wc), lambda j, i: (i, j)), pl.BlockSpec((8, wc), lambda j, i: (0, j))],
        out_shape=[jax.ShapeDtypeStruct((t, inner), BF16), jax.ShapeDtypeStruct((8, inner), F32)],
        compiler_params=_cparams("parallel", "arbitrary"),
    )(dy, dy, proj, proj, proj, conv_w)


GDN_HB = 4
GDN_NB = 8
SCAN_HB = 16
SCAN_NB = 2


def _iota2(n, m):
    return lax.broadcasted_iota(jnp.int32, (n, m), 0), lax.broadcasted_iota(jnp.int32, (n, m), 1)


def _head_select(first_head, hb, lane0):
    r, lane = _iota2(8, LANES)
    return jnp.where((r < hb) & (lane == lane0 + first_head + r), 1.0, 0.0).astype(F32)


def _chunk_gates(gt, selg, selb):
    i, j = _iota2(CHUNK, CHUNK)
    gc_all = _dot_exact(jnp.where(j <= i, 1.0, 0.0), gt, NN, True)
    return (_dot_exact(gc_all, selg, NT, False), _dot_exact(selg, gc_all, NT, True),
            _dot_exact(gt, selb, NT, False))


def _decay_terms(gcol, grow):
    i, j = _iota2(CHUNK, CHUNK)
    glast = gcol[:, CHUNK - 1:CHUNK, :]
    decay = jnp.exp(jnp.where(j <= i, gcol - grow, NEG_BIG))
    return jnp.exp(gcol), jnp.exp(glast - gcol), jnp.exp(glast), decay


def _unit_lower_inverse(a):
    i, j = _iota2(CHUNK, CHUNK)
    same16 = (i // 16) == (j // 16)
    same32 = (i // 32) == (j // 32)
    m = jnp.where(same16, -a, 0.0)
    x = jnp.where(i == j, 1.0, 0.0) + m
    for _ in range(3):
        m = _dot3(m, m, BNN)
        x = x + _dot3(x, m, BNN)
    for off in (jnp.where(same32 & jnp.logical_not(same16), a, 0.0), jnp.where(same32, 0.0, a)):
        x = x - _dot3(_dot3(x, off, BNN), x, BNN)
    return x


def _unit_inputs(refs, g_ref, selg, selb, hb, nb):
    units = [(c, h) for c in range(nb) for h in range(hb)]
    rs = lambda c: slice(c * CHUNK, (c + 1) * CHUNK)
    cs = lambda h: slice(h * HEAD_DIM, (h + 1) * HEAD_DIM)
    gates = [_chunk_gates(g_ref[rs(c), :], selg, selb) for c in range(nb)]
    stacked = [jnp.stack([r[rs(c), cs(h)] for c, h in units]) for r in refs]
    gcol = jnp.stack([gates[c][0][:, h:h + 1] for c, h in units])
    grow = jnp.stack([gates[c][1][h:h + 1, :] for c, h in units])
    bcol = jnp.stack([gates[c][2][:, h:h + 1] for c, h in units])
    return units, rs, cs, stacked, gcol, grow, bcol


def _gdn_specs(nh, inner, t, heads=GDN_HB, chunks=GDN_NB):
    hb, nb = min(heads, nh), chunks
    rows = nb * CHUNK
    wide = pl.BlockSpec((rows, hb * HEAD_DIM), lambda g, n: (n, g))
    sq = pl.BlockSpec((hb, rows, CHUNK), lambda g, n: (g, n, 0))
    gts = pl.BlockSpec((rows, LANES), lambda g, n: (n, 0))
    glb = pl.BlockSpec((nb * 8, hb * HEAD_DIM), lambda g, n: (n, g))
    return hb, nb, rows, wide, sq, gts, glb


def _gdn_intra_fwd(q, k, v, gates, nh, name, comm=None):
    t, inner = q.shape
    hb, nb, rows, wide, sq, gts, glb = _gdn_specs(nh, inner, t)

    def body(q_ref, k_ref, v_ref, g_ref, qe_ref, kel_ref, wb_ref, w_ref, u_ref, qk_ref, tm_ref, gl_ref):
        first = pl.program_id(0) * hb
        selg, selb = _head_select(first, hb, 0), _head_select(first, hb, nh)
        i, j = _iota2(CHUNK, CHUNK)
        units, rs, cs, (qv, kv, vv), gcol, grow, bcol = _unit_inputs(
            (q_ref, k_ref, v_ref), g_ref, selg, selb, hb, nb)
        e, el, gl, decay = _decay_terms(gcol, grow)
        kb = kv * bcol
        qbf, kbf = _bf(qv), _bf(kv)
        a = jnp.where(j < i, _dot(_bf(kb), kbf, BNT) * decay, 0.0)
        tm = _unit_lower_inverse(a)
        uw = _dot3(tm, jnp.concatenate([vv * bcol, kb * e], axis=2), BNN)
        qk = _bf(_dot(qbf, kbf, BNT) * decay)
        qe, kel = _bf(qv * e), _bf(kv * el)
        for n, (c, h) in enumerate(units):
            w = uw[n, :, HEAD_DIM:]
            qe_ref[rs(c), cs(h)] = qe[n]
            kel_ref[rs(c), cs(h)] = kel[n]
            wb_ref[rs(c), cs(h)] = _bf(w)
            w_ref[rs(c), cs(h)] = w
            u_ref[rs(c), cs(h)] = uw[n, :, :HEAD_DIM]
            qk_ref[h, rs(c), :] = qk[n]
            tm_ref[h, rs(c), :] = tm[n]
            gl_ref[c * 8:(c + 1) * 8, cs(h)] = jnp.broadcast_to(gl[n], (8, HEAD_DIM))

    big = lambda dt: jax.ShapeDtypeStruct((t, inner), dt)
    return _grid_call(
        body, name=name, grid=(nh // hb, t // rows),
        in_specs=[wide, wide, wide, gts],
        out_specs=[wide] * 5 + [sq, sq, glb],
        out_shape=[big(BF16), big(BF16), big(BF16), big(F32), big(F32),
                   jax.ShapeDtypeStruct((nh, t, CHUNK), BF16), jax.ShapeDtypeStruct((nh, t, CHUNK), F32),
                   jax.ShapeDtypeStruct((t // CHUNK * 8, inner), F32)],
        args=(q, k, v, gates), semantics=("parallel", "parallel"), comm=comm)


def _gdn_scan_fwd(qe, kel, wb, u, qk, glb, nh, name):
    t, inner = u.shape
    hb, nb, rows, wide, sq, _, glb_spec = _gdn_specs(nh, inner, t, SCAN_HB, SCAN_NB)

    def body(qe_ref, kel_ref, wb_ref, u_ref, qk_ref, gl_ref, o_ref, vn_ref, sall_ref, s_ref):
        @pl.when(pl.program_id(1) == 0)
        def _():
            s_ref[...] = jnp.zeros(s_ref.shape, F32)

        cs = lambda h: slice(h * HEAD_DIM, (h + 1) * HEAD_DIM)
        for c in range(nb):
            rs = slice(c * CHUNK, (c + 1) * CHUNK)
            heads = lambda ref: jnp.stack([ref[rs, cs(h)] for h in range(hb)])
            s = s_ref[...]
            sall_ref[c] = s
            sb = _bf(s)
            vn = heads(u_ref) - _dot(heads(wb_ref), sb, BNN)
            vnb = _bf(vn)
            o = _dot(heads(qe_ref), sb, BNN) + _dot(qk_ref[:, rs, :], vnb, BNN)
            gl = jnp.stack([gl_ref[c * 8:c * 8 + 1, cs(h)] for h in range(hb)])
            s_ref[...] = s * gl + _dot(heads(kel_ref), vnb, BTN)
            for h in range(hb):
                vn_ref[rs, cs(h)] = vn[h]
                o_ref[rs, cs(h)] = o[h]

    return pl.pallas_call(
        body, name=name, grid=(nh // hb, t // rows),
        in_specs=[wide, wide, wide, wide, sq, glb_spec],
        out_specs=[wide, wide, pl.BlockSpec((nb, hb, HEAD_DIM, HEAD_DIM), lambda g, n: (n, g, 0, 0))],
        out_shape=[jax.ShapeDtypeStruct((t, inner), F32), jax.ShapeDtypeStruct((t, inner), F32),
                   jax.ShapeDtypeStruct((t // CHUNK, nh, HEAD_DIM, HEAD_DIM), F32)],
        scratch_shapes=[pltpu.VMEM((hb, HEAD_DIM, HEAD_DIM), F32)],
        compiler_params=_cparams("parallel", "arbitrary"),
    )(qe, kel, wb, u, qk, glb)


def _gdn_scan_bwd(do, qe, kel, wb, vn, qk, glb, sall, nh, name):
    t, inner = do.shape
    hb, nb, rows, _, _, _, _ = _gdn_specs(nh, inner, t, SCAN_HB, SCAN_NB)
    last = t // rows - 1
    wide = pl.BlockSpec((rows, hb * HEAD_DIM), lambda g, n: (last - n, g))
    sq = pl.BlockSpec((hb, rows, CHUNK), lambda g, n: (g, last - n, 0))
    glb_spec = pl.BlockSpec((nb * 8, hb * HEAD_DIM), lambda g, n: (last - n, g))

    def body(do_ref, qe_ref, kel_ref, wb_ref, vn_ref, qk_ref, gl_ref, sall_ref,
             dvn_ref, dw_ref, dqe_ref, dkel_ref, dqk_ref, dgl_ref, ds_ref):
        @pl.when(pl.program_id(1) == 0)
        def _():
            ds_ref[...] = jnp.zeros(ds_ref.shape, F32)

        cs = lambda h: slice(h * HEAD_DIM, (h + 1) * HEAD_DIM)
        for c in reversed(range(nb)):
            rs = slice(c * CHUNK, (c + 1) * CHUNK)
            heads = lambda ref: jnp.stack([ref[rs, cs(h)] for h in range(hb)])
            ds, s = ds_ref[...], sall_ref[c]
            dsb, sb = _bf(ds), _bf(s)
            dob, vnb = _bf(heads(do_ref)), _bf(heads(vn_ref))
            dvn = _dot(qk_ref[:, rs, :], dob, BTN) + _dot(heads(kel_ref), dsb, BNN)
            dvnb = _bf(dvn)
            dw = -_dot(dvnb, sb, BNT)
            dqe = _dot(dob, sb, BNT)
            dkel = _dot(vnb, dsb, BNT)
            dqk_ref[:, rs, :] = _dot(dob, vnb, BNT)
            dgl = jnp.sum(jnp.sum(ds * s, axis=2, keepdims=True), axis=1, keepdims=True)
            gl = jnp.stack([gl_ref[c * 8:c * 8 + 1, cs(h)] for h in range(hb)])
            ds_ref[...] = ds * gl + _dot(heads(qe_ref), dob, BTN) - _dot(heads(wb_ref), dvnb, BTN)
            for h in range(hb):
                dvn_ref[rs, cs(h)] = dvn[h]
                dw_ref[rs, cs(h)] = dw[h]
                dqe_ref[rs, cs(h)] = dqe[h]
                dkel_ref[rs, cs(h)] = dkel[h]
                dgl_ref[c * 8:(c + 1) * 8, cs(h)] = jnp.broadcast_to(dgl[h], (8, HEAD_DIM))

    big = jax.ShapeDtypeStruct((t, inner), F32)
    return pl.pallas_call(
        body, name=name, grid=(nh // hb, t // rows),
        in_specs=[wide, wide, wide, wide, wide, sq, glb_spec,
                  pl.BlockSpec((nb, hb, HEAD_DIM, HEAD_DIM), lambda g, n: (last - n, g, 0, 0))],
        out_specs=[wide] * 4 + [sq, glb_spec],
        out_shape=[big] * 4 + [jax.ShapeDtypeStruct((nh, t, CHUNK), F32),
                               jax.ShapeDtypeStruct((t // CHUNK * 8, inner), F32)],
        scratch_shapes=[pltpu.VMEM((hb, HEAD_DIM, HEAD_DIM), F32)],
        compiler_params=_cparams("parallel", "arbitrary"),
    )(do, qe, kel, wb, vn, qk, glb, sall)


def _gdn_intra_bwd(q, k, v, gates, tm, w, u, dvn, dw, dqe, dkel, dqk, dglb, nh, name, comm=None):
    t, inner = q.shape
    hb, nb, rows, wide, sq, gts, glb = _gdn_specs(nh, inner, t)

    def body(q_ref, k_ref, v_ref, g_ref, tm_ref, w_ref, u_ref, dvn_ref, dw_ref, dqe_ref, dkel_ref, dqk_ref,
             dgl_ref, dq_ref, dk_ref, dv_ref, dg_ref):
        first = pl.program_id(0) * hb
        selg, selb = _head_select(first, hb, 0), _head_select(first, hb, nh)
        i, j = _iota2(CHUNK, CHUNK)
        lane8 = lax.broadcasted_iota(jnp.int32, (CHUNK, 8), 1)
        row = lax.broadcasted_iota(jnp.int32, (CHUNK, 1), 0)
        lower = jnp.where(j <= i, 1.0, 0.0).astype(F32)
        rsum = lambda x: jnp.sum(x, axis=-1, keepdims=True)
        units, rs, cs, (qv, kv, vv, wv, uv, dvn, dw, dqe, dkel), gcol, grow, bcol = _unit_inputs(
            (q_ref, k_ref, v_ref, w_ref, u_ref, dvn_ref, dw_ref, dqe_ref, dkel_ref), g_ref, selg, selb, hb, nb)
        nu = len(units)
        tmv = jnp.stack([tm_ref[h, rs(c), :] for c, h in units])
        dqk = jnp.where(j <= i, jnp.stack([dqk_ref[h, rs(c), :] for c, h in units]), 0.0)
        dgl = jnp.stack([dgl_ref[c * 8:c * 8 + 1, h * HEAD_DIM:h * HEAD_DIM + 1] for c, h in units])
        e, el, gl, decay = _decay_terms(gcol, grow)
        kb = kv * bcol
        qb, kbf, kbb = _bf(qv), _bf(kv), _bf(kb)
        dqkr = _bf(dqk * decay)
        dq = dqe * e + _dot(dqkr, kbf, BNN)
        dk = dkel * el + _dot(dqkr, qb, BTN)
        de = rsum(dqe * qv)
        del_ = rsum(dkel * kv)
        mq = dqk * _dot(qb, kbf, BNT) * decay
        dsol = _dot3(tmv, jnp.concatenate([dvn, dw], axis=2), BTN)
        dvb, dkbe = dsol[:, :, :HEAD_DIM], dsol[:, :, HEAD_DIM:]
        da = -jnp.where(j < i, _dot3(dsol, jnp.concatenate([uv, wv], axis=2), BNT), 0.0)
        dkk = _bf(da * decay)
        ma = da * _dot(kbb, kbf, BNT) * decay
        dkb = dkbe * e + _dot(dkk, kbf, BNN)
        de = de + rsum(dkbe * kb)
        dk = dk + _dot(dkk, kbb, BTN) + dkb * bcol
        dv = dvb * bcol
        dbeta = rsum(dkb * kv) + rsum(dvb * vv)
        m = mq + ma
        ones = jnp.ones((nu, CHUNK, LANES), F32)
        dgc = rsum(m) - _dot_exact(m, ones, BTN, False)[:, :, 0:1] + de * e - del_ * el
        tail = jnp.sum(del_ * el, axis=1, keepdims=True) + dgl * gl
        dgc = dgc + jnp.where(row == CHUNK - 1, tail, 0.0)
        for n, (c, h) in enumerate(units):
            dq_ref[rs(c), cs(h)] = dq[n]
            dk_ref[rs(c), cs(h)] = dk[n]
            dv_ref[rs(c), cs(h)] = dv[n]
        for c in range(nb):
            dgc_cols = jnp.zeros((CHUNK, 8), F32)
            dbeta_cols = jnp.zeros((CHUNK, 8), F32)
            for h in range(hb):
                dgc_cols = jnp.where(lane8 == h, dgc[c * hb + h], dgc_cols)
                dbeta_cols = jnp.where(lane8 == h, dbeta[c * hb + h], dbeta_cols)
            dg_cols = _dot_exact(lower, dgc_cols, TN, True)
            dg_ref[rs(c), :] = _dot_exact(dg_cols, selg, NN, False) + _dot_exact(dbeta_cols, selb, NN, False)

    big = jax.ShapeDtypeStruct((t, inner), F32)
    return _grid_call(
        body, name=name, grid=(nh // hb, t // rows),
        in_specs=[wide, wide, wide, gts, sq, wide, wide, wide, wide, wide, wide, sq, glb],
        out_specs=[wide, wide, wide, pl.BlockSpec((None, rows, LANES), lambda g, n: (g, n, 0))],
        out_shape=[big, big, big, jax.ShapeDtypeStruct((nh // hb, t, LANES), F32)],
        args=(q, k, v, gates, tm, w, u, dvn, dw, dqe, dkel, dqk, dglb), semantics=("parallel", "parallel"),
        comm=comm)


def _layer_a_fwd(x, hn, w_main, w_ab, conv_w, alog_row, dt_row, onw, w_out, nh, comm=None):
    t, d = x.shape
    inner = w_out.shape[0]
    proj = _mm(hn, w_main, "nn", t, 4 * inner, d, out_dtype=F32, name="a_proj")
    ab = _mm(hn, w_ab, "nn", t, LANES, d, out_dtype=F32, name="a_proj_ab")
    gates = _gates_fwd(ab, alog_row, dt_row, nh, "a_gates")
    q = _conv_fwd(proj, conv_w, 0, inner, "q", "a_conv_q")
    k = _conv_fwd(proj, conv_w, inner, inner, "k", "a_conv_k")
    v = _conv_fwd(proj, conv_w, 2 * inner, inner, "v", "a_conv_v")
    qe, kel, wb, w, u, qk, tm, glb, *carried = _gdn_intra_fwd(q, k, v, gates, nh, "a_intra", comm)
    o, vn, sall = _gdn_scan_fwd(qe, kel, wb, u, qk, glb, nh, "a_scan")
    g = _gate_fwd(o, proj, 3 * inner, inner, "a_gate", norm_w=onw)
    h1 = _mm(g, w_out, "nn", t, d, inner, out_dtype=F32, name="a_out", res=x)
    return h1, (hn, proj, ab, gates, q, k, v, qe, kel, wb, w, u, qk, tm, glb, o, vn, sall, g), carried


def _layer_a_bwd(dh1, dh1b, x, nw, w_main, w_ab, conv_w, alog_row, dt_row, onw, w_out, nh, saved, comm, own_comm):
    hn, proj, ab, gates, q, k, v, qe, kel, wb, w, u, qk, tm, glb, o, vn, sall, g = saved
    t, d = x.shape
    inner = w_out.shape[0]
    dg = _mm(dh1b, w_out, "nt", t, inner, d, out_dtype=F32, name="a_dgate")
    dw_out = _mm(g, dh1b, "tn", inner, d, t, out_dtype=F32, name="a_dwout")
    do, dz, donw = _gate_bwd(dg, o, proj, 3 * inner, inner, "a_gate_bwd", F32, norm_w=onw)
    dvn, dw, dqe, dkel, dqk, dglb = _gdn_scan_bwd(do, qe, kel, wb, vn, qk, glb, sall, nh, "a_scan_bwd")
    dq, dk, dv, dgates, *carried = _gdn_intra_bwd(q, k, v, gates, tm, w, u, dvn, dw, dqe, dkel, dqk, dglb, nh,
                                                  "a_intra_bwd", comm)
    dxq, dcq = _conv_bwd(dq, proj, conv_w, 0, inner, "q", "a_conv_q_bwd")
    dxk, dck = _conv_bwd(dk, proj, conv_w, inner, inner, "k", "a_conv_k_bwd")
    dxv, dcv = _conv_bwd(dv, proj, conv_w, 2 * inner, inner, "v", "a_conv_v_bwd")
    dab, dsmall = _gates_bwd(ab, alog_row, dt_row, dgates, nh, "a_gates_bwd")
    dproj = jnp.concatenate([dxq, dxk, dxv, dz], axis=1)
    dw_main = _mm(hn, dproj, "tn", d, 4 * inner, t, out_dtype=F32, name="a_dwin")
    dw_ab = _mm(hn, dab, "tn", d, LANES, t, out_dtype=F32, name="a_dwin_ab")
    dconv = jnp.concatenate([dcq[:CONV_K], dck[:CONV_K], dcv[:CONV_K]], axis=1)
    dhn = _mm(dab, w_ab, "nt", t, d, LANES, out_dtype=F32, name="a_dhn_ab")
    own = own_comm(dw_main, dw_ab, dw_out, dconv)
    dhn = _mm(dproj, w_main, "nt", t, d, 4 * inner, out_dtype=F32, name="a_dhn", res=dhn, comm=own)
    dhn, carried_own = dhn if own is not None else (dhn, [])
    dx, _, dnw = _rms_bwd(x, nw, dhn, dh1, "a_rms_bwd")
    return dx, dnw, dsmall, donw, carried, carried_own


def _rows_of(a, rows):
    flat = a.reshape(-1)
    return jnp.pad(flat, (0, rows * LANES - flat.shape[0])).reshape(rows, LANES)


def _to_slabs(g, axis):
    shape = g.shape[:axis] + (N_DEV, g.shape[axis] // N_DEV) + g.shape[axis + 1:]
    return jnp.moveaxis(g.reshape(shape), axis, 0)


def _from_slabs(s, axis):
    m = jnp.moveaxis(s, 0, axis)
    return m.reshape(m.shape[:axis] + (m.shape[axis] * m.shape[axis + 1],) + m.shape[axis + 2:])


def kernel(x, norm_w, a_w_in, a_conv_w, a_a_log, a_dt_bias, a_out_norm_w, a_w_out, b_w_in, b_q_norm_w, b_k_norm_w, b_rel_bias, b_w_out, loss_target, m_norm_w, m_a_w_in, m_a_conv_w, m_a_a_log, m_a_dt_bias, m_a_out_norm_w, m_a_w_out, m_b_w_in, m_b_q_norm_w, m_b_k_norm_w, m_b_rel_bias, m_b_w_out, v_norm_w, v_a_w_in, v_a_conv_w, v_a_a_log, v_a_dt_bias, v_a_out_norm_w, v_a_w_out, v_b_w_in, v_b_q_norm_w, v_b_k_norm_w, v_b_rel_bias, v_b_w_out):
    xs, target = x[0], loss_target[0]
    nh = a_a_log.shape[-1]
    inner = N_DEV * a_w_out.shape[1]

    nw0, nw1 = norm_w[0:1], norm_w[1:2]
    hn0, (ga_in, ga_out, g_conv) = _rms_fwd(
        xs, nw0, "a_rms", comm=_RoutedGather([a_w_in[0].astype(BF16), a_w_out[0].astype(BF16), a_conv_w[0]]))
    wa_in = _from_slabs(ga_in, 1)
    wa_main = wa_in[:, :4 * inner]
    wa_ab = jnp.pad(wa_in[:, 4 * inner:], ((0, 0), (0, LANES - 2 * nh)))
    wa_out = _from_slabs(ga_out, 0)
    conv_w = _from_slabs(g_conv, 1)
    alog_row = jnp.pad(a_a_log, ((0, 0), (0, LANES - nh)))
    dt_row = jnp.pad(a_dt_bias, ((0, 0), (0, LANES - nh)))

    h1, saved_a, (gb_in, gb_out) = _layer_a_fwd(
        xs, hn0, wa_main, wa_ab, conv_w, alog_row, dt_row, a_out_norm_w, wa_out, nh,
        _Comm("gather", [b_w_in[0].astype(BF16), b_w_out[0].astype(BF16)]))
    wb_in = _from_slabs(gb_in, 1)
    wb_out = _from_slabs(gb_out, 0)
    bias = _bias_tiles(_pad_rel_bias(b_rel_bias[0]), "b_bias_tiles")
    h2, saved_b = _layer_b_fwd(h1, nw1, wb_in, b_q_norm_w, b_k_norm_w, bias, wb_out)
    dh2, dh2b, loss_row = _loss_grad(h2, target, "loss")

    dh1, dh1b, dnw1, dwb_in, dqw, dkw, drb, dwb_out = _layer_b_bwd(
        dh2, dh2b, h1, nw1, wb_in, b_q_norm_w, b_k_norm_w, bias, wb_out, saved_b)
    def exchange_a(dwa_main, dwa_ab, dwa_out, dconv):
        dwa_in = jnp.concatenate([dwa_main, dwa_ab[:, :2 * nh]], axis=1)
        return _Comm("exchange", [_to_slabs(dwa_in, 1).astype(BF16), _to_slabs(dwa_out, 0).astype(BF16),
                                  _to_slabs(dconv, 1)])

    dx, dnw0, dsmall, donw, (pb_in, pb_out), (pa_in, pa_out, p_conv) = _layer_a_bwd(
        dh1, dh1b, xs, nw0, wa_main, wa_ab, conv_w, alog_row, dt_row, a_out_norm_w, wa_out, nh, saved_a,
        _Comm("exchange", [_to_slabs(dwb_in, 1).astype(BF16), _to_slabs(dwb_out, 0).astype(BF16)]), exchange_a)
    big = {}
    for name, p, w, m, v in (("a_w_in", pa_in, a_w_in, m_a_w_in, v_a_w_in),
                             ("a_w_out", pa_out, a_w_out, m_a_w_out, v_a_w_out),
                             ("b_w_in", pb_in, b_w_in, m_b_w_in, v_b_w_in),
                             ("b_w_out", pb_out, b_w_out, m_b_w_out, v_b_w_out),
                             ("a_conv_w", p_conv, a_conv_w, m_a_conv_w, v_a_conv_w)):
        big[name] = [o[None] for o in _adamw(p, w[0], m[0], v[0], "adamw_" + name)]

    small = (("norm_w", norm_w, m_norm_w, v_norm_w, jnp.concatenate([dnw0, dnw1], axis=0)),
             ("a_a_log", a_a_log, m_a_a_log, v_a_a_log, dsmall[0:1, :nh]),
             ("a_dt_bias", a_dt_bias, m_a_dt_bias, v_a_dt_bias, dsmall[1:2, :nh]),
             ("a_out_norm_w", a_out_norm_w, m_a_out_norm_w, v_a_out_norm_w, donw),
             ("b_q_norm_w", b_q_norm_w, m_b_q_norm_w, v_b_q_norm_w, dqw),
             ("b_k_norm_w", b_k_norm_w, m_b_k_norm_w, v_b_k_norm_w, dkw),
             ("b_rel_bias", b_rel_bias, m_b_rel_bias, v_b_rel_bias, drb))
    rows = [8 * (-(-w.size // (8 * LANES))) for _, w, _, _, _ in small]
    pack = lambda arrs: jnp.concatenate([_rows_of(a, r) for a, r in zip(arrs, rows)] + [jnp.zeros((8, LANES), F32)], axis=0)
    g_pack = jnp.concatenate([_rows_of(g, r) for (_, _, _, _, g), r in zip(small, rows)]
                             + [jnp.broadcast_to(loss_row, (8, LANES))], axis=0)
    (g_all,) = _comm_call(_Comm("gather", [g_pack]), "gather_small_grads")
    outs_small = _adamw(g_all, pack([s[1] for s in small]), pack([s[2] for s in small]),
                        pack([s[3] for s in small]), "adamw_small")
    start = 0
    for (name, w, _, _, _), r in zip(small, rows):
        big[name] = [o[start:start + r].reshape(-1)[:w.size].reshape(w.shape) for o in outs_small]
        start += r
    loss = outs_small[0][start, 0]

    order = ("norm_w", "a_w_in", "a_conv_w", "a_a_log", "a_dt_bias", "a_out_norm_w", "a_w_out", "b_w_in",
             "b_q_norm_w", "b_k_norm_w", "b_rel_bias", "b_w_out")
    return (loss, dx[None]) + tuple(big[n][i] for i in range(4) for n in order)
```

```python
import functools
import math

import jax
import jax.numpy as jnp
from jax import lax
from jax.experimental import pallas as pl
from jax.experimental.pallas import tpu as pltpu

F32 = jnp.float32
BF16 = jnp.bfloat16
MESH_IDS = pl.DeviceIdType.MESH
N_DEV = 8
CHUNK = 64
HEAD_DIM = 128
EPS = 1e-6
CONV_K = 4
LEFT_CHUNKS = 8
REL_CLIP = 256
Q_TILE = LEFT_CHUNKS * CHUNK
ADAM_LR = 0.001
ADAM_B1 = 0.9
ADAM_B2 = 0.999
ADAM_EPS = 1e-08
ADAM_WD = 0.01
ADAM_STEP = 10
NEG_BIG = -1e30
VMEM_LIMIT_BYTES = 56 * 1024 * 1024
HIGHEST = lax.Precision.HIGHEST
ANY = pl.BlockSpec(memory_space=pl.ANY)


def _cparams(*sem):
    return pltpu.CompilerParams(dimension_semantics=tuple(sem), vmem_limit_bytes=VMEM_LIMIT_BYTES)


NN, NT, TN = (((1,), (0,)), ((), ())), (((1,), (1,)), ((), ())), (((0,), (0,)), ((), ()))
BNN, BNT, BTN = (((2,), (1,)), ((0,), (0,))), (((2,), (2,)), ((0,), (0,))), (((1,), (1,)), ((0,), (0,)))


def _dot(a, b, dims, precision=None):
    return lax.dot_general(a, b, dims, preferred_element_type=F32, precision=precision)


def _nn(a, b, precision=None):
    return _dot(a, b, NN, precision)


def _nt(a, b, precision=None):
    return _dot(a, b, NT, precision)


def _tn(a, b, precision=None):
    return _dot(a, b, TN, precision)


def _bf(x):
    return x.astype(BF16)


def _split(x, pieces=2):
    out = []
    for _ in range(pieces - 1):
        hi = x.astype(BF16)
        out.append(hi)
        x = x - hi.astype(F32)
    return out + [x.astype(BF16)]


def _dot3(a, b, dims):
    (ah, al), (bh, bl) = _split(a), _split(b)
    return _dot(ah, bh, dims) + (_dot(ah, bl, dims) + _dot(al, bh, dims))


def _dot_exact(a, b, dims, split_b):
    if split_b:
        a = a.astype(BF16)
        parts = [_dot(a, p, dims) for p in _split(b, 3)]
    else:
        b = b.astype(BF16)
        parts = [_dot(p, b, dims) for p in _split(a, 3)]
    return parts[0] + (parts[1] + parts[2])


def _sigmoid(x):
    return 1.0 / (1.0 + jnp.exp(-x))


def _silu(x):
    return x * _sigmoid(x)


def _dsilu(x):
    s = _sigmoid(x)
    return s * (1.0 + x * (1.0 - s))


def _my_pos():
    return lax.axis_index("x"), lax.axis_index("y"), lax.axis_index("c")


def _peers(x, y, c):
    def flip(v, f):
        return 1 - v if f else v

    return [(flip(x, kx), flip(y, ky), flip(c, kc)) for kx in (0, 1) for ky in (0, 1) for kc in (0, 1)][1:]


def _lin(p):
    return 4 * p[0] + 2 * p[1] + p[2]


class _Comm:
    def __init__(self, kind, arrays):
        self.kind, self.arrays, self.n = kind, list(arrays), len(arrays)

    def out_shape(self):
        lead = (N_DEV,) if self.kind == "gather" else ()
        return [jax.ShapeDtypeStruct(lead + a.shape, a.dtype) for a in self.arrays]

    def scratch(self):
        return [pltpu.SemaphoreType.DMA((7 * self.n,)), pltpu.SemaphoreType.DMA((7 * self.n,)),
                pltpu.SemaphoreType.DMA((self.n,))]

    def _copies(self, ins, outs, sems, arrivals):
        send_sems, recv_sems, local_sems = sems
        x, y, c = _my_pos()
        me = _lin((x, y, c))
        gather = self.kind == "gather"
        mine = [ins[t] if gather else ins[t].at[me] for t in range(self.n)]
        remote = []
        for k, peer in enumerate(_peers(x, y, c)):
            for t in range(self.n):
                if arrivals:
                    src, dst = mine[t], outs[t].at[_lin(peer)]
                else:
                    src, dst = (ins[t] if gather else ins[t].at[_lin(peer)]), outs[t].at[me]
                remote.append(pltpu.make_async_remote_copy(
                    src_ref=src, dst_ref=dst, send_sem=send_sems.at[k * self.n + t],
                    recv_sem=recv_sems.at[k * self.n + t], device_id=peer, device_id_type=MESH_IDS))
        if arrivals:
            return remote
        return [pltpu.make_async_copy(mine[t], outs[t].at[me], local_sems.at[t]) for t in range(self.n)], remote

    def start(self, ins, outs, sems):
        local, sends = self._copies(ins, outs, sems, False)
        for cp in local + sends:
            cp.start()

    def finish(self, ins, outs, sems):
        for cp in self._copies(ins, outs, sems, True):
            cp.wait_recv()
        local, sends = self._copies(ins, outs, sems, False)
        for cp in sends:
            cp.wait_send()
        for cp in local:
            cp.wait()


def _xor(a, b):
    return a + b - 2 * a * b


class _RoutedGather(_Comm):
    def __init__(self, arrays):
        super().__init__("gather", arrays)

    def _plan(self, outs, sems):
        send_sems, recv_sems, _ = sems
        x, y, c = _my_pos()
        sib, xn, yn, dg = (x, y, 1 - c), (1 - x, y, c), (x, 1 - y, c), (1 - x, 1 - y, c)
        via = (_xor(x, 1 - c), _xor(y, c), c)
        onto = (_xor(x, c), _xor(y, 1 - c), c)
        routes = [(None, sib, sib), (None, xn, xn), (None, yn, yn), (via, onto, dg),
                  (xn, sib, (1 - x, y, 1 - c)), (yn, sib, (x, 1 - y, 1 - c)), (dg, sib, (1 - x, 1 - y, 1 - c))]

        def copy(k, t, src, slot, target):
            return pltpu.make_async_remote_copy(
                src_ref=src, dst_ref=outs[t].at[slot], send_sem=send_sems.at[k * self.n + t],
                recv_sem=recv_sems.at[k * self.n + t], device_id=target, device_id_type=MESH_IDS)

        return (x, y, c), routes, copy

    def start(self, ins, outs, sems):
        me, routes, copy = self._plan(outs, sems)
        for t in range(self.n):
            pltpu.make_async_copy(ins[t], outs[t].at[_lin(me)], sems[2].at[t]).start()
            for k in range(3):
                copy(k, t, ins[t], _lin(me), routes[k][1]).start()

    def finish(self, ins, outs, sems):
        me, routes, copy = self._plan(outs, sems)

        def arrived(k):
            for t in range(self.n):
                copy(k, t, ins[t], _lin(routes[k][2]), me).wait_recv()

        def pass_on(k):
            for t in range(self.n):
                copy(k, t, outs[t].at[_lin(routes[k][0])], _lin(routes[k][0]), routes[k][1]).start()

        arrived(1)
        arrived(2)
        for k in (3, 4, 5):
            pass_on(k)
        arrived(3)
        pass_on(6)
        for k in (0, 4, 5, 6):
            arrived(k)
        for t in range(self.n):
            for k in range(7):
                src = ins[t] if k < 3 else outs[t].at[_lin(routes[k][0])]
                copy(k, t, src, _lin(me), routes[k][1]).wait_send()
            pltpu.make_async_copy(ins[t], outs[t].at[_lin(me)], sems[2].at[t]).wait()


def _comm_call(comm, name):
    n = comm.n

    def body(*refs):
        ins, outs, sems = refs[:n], refs[n:2 * n], refs[2 * n:]
        comm.start(ins, outs, sems)
        comm.finish(ins, outs, sems)

    return pl.pallas_call(
        body, name=name, out_shape=comm.out_shape(), in_specs=[ANY] * n, out_specs=[ANY] * n,
        scratch_shapes=comm.scratch(),
    )(*comm.arrays)


def _grid_call(body, *, name, grid, in_specs, out_specs, out_shape, args, scratch_shapes=(), semantics=None, comm=None):
    if comm is None:
        return pl.pallas_call(
            body, name=name, grid=grid, in_specs=in_specs, out_specs=out_specs, out_shape=out_shape,
            scratch_shapes=list(scratch_shapes), compiler_params=_cparams(*semantics),
        )(*args)
    n_in, n_out, n_sc, n = len(in_specs), len(out_specs), len(scratch_shapes), comm.n

    def full(*refs):
        ins, refs = refs[:n_in], refs[n_in:]
        cins, refs = refs[:n], refs[n:]
        outs, refs = refs[:n_out], refs[n_out:]
        couts, refs = refs[:n], refs[n:]
        scratch, sems = refs[:n_sc], refs[n_sc:]
        ids = [pl.program_id(a) for a in range(len(grid))]
        first = functools.reduce(jnp.logical_and, [i == 0 for i in ids])
        last = functools.reduce(jnp.logical_and, [i == g - 1 for i, g in zip(ids, grid)])

        @pl.when(first)
        def _():
            comm.start(cins, couts, sems)

        body(*ins, *outs, *scratch)

        @pl.when(last)
        def _():
            comm.finish(cins, couts, sems)

    return pl.pallas_call(
        full, name=name, grid=grid, in_specs=list(in_specs) + [ANY] * n, out_specs=list(out_specs) + [ANY] * n,
        out_shape=list(out_shape) + comm.out_shape(), scratch_shapes=list(scratch_shapes) + comm.scratch(),
        compiler_params=_cparams(*(["arbitrary"] * len(grid))),
    )(*(list(args) + comm.arrays))


def _mm(a, b, mode, m, n, k, *, out_dtype, name, tm=1024, tn=1024, tk=2048,
        a_m0=0, a_k0=0, b_n0=0, b_k0=0, res=None, comm=None):
    tm, tn, tk = min(tm, m), min(tn, n), min(tk, k)
    nm, nn, nk = m // tm, n // tn, k // tk
    assert nm * tm == m and nn * tn == n and nk * tk == k
    am, ak, bn, bk = a_m0 // tm, a_k0 // tk, b_n0 // tn, b_k0 // tk
    assert am * tm == a_m0 and ak * tk == a_k0 and bn * tn == b_n0 and bk * tk == b_k0
    if mode == "tn":
        a_spec = pl.BlockSpec((tk, tm), lambda i, j, q: (q + ak, i + am))
        a_dims = (0,)
    else:
        a_spec = pl.BlockSpec((tm, tk), lambda i, j, q: (i + am, q + ak))
        a_dims = (1,)
    if mode == "nt":
        b_spec = pl.BlockSpec((tn, tk), lambda i, j, q: (j + bn, q + bk))
        b_dims = (1,)
    else:
        b_spec = pl.BlockSpec((tk, tn), lambda i, j, q: (q + bk, j + bn))
        b_dims = (0,)
    o_spec = pl.BlockSpec((tm, tn), lambda i, j, q: (i, j))
    has_res = res is not None

    def body(*refs):
        a_ref, b_ref = refs[0], refs[1]
        res_ref = refs[2] if has_res else None
        o_ref = refs[2 + has_res]
        p = _dot(a_ref[...], b_ref[...], ((a_dims, b_dims), ((), ())))

        def finish(total):
            if has_res:
                total = total + res_ref[...].astype(F32)
            o_ref[...] = total.astype(out_dtype)

        if nk == 1:
            finish(p)
        else:
            acc_ref = refs[3 + has_res]
            q = pl.program_id(2)

            @pl.when(q == 0)
            def _():
                acc_ref[...] = p

            @pl.when(q > 0)
            def _():
                acc_ref[...] += p

            @pl.when(q == nk - 1)
            def _():
                finish(acc_ref[...])

    out, *carried = _grid_call(
        body, name=name, grid=(nm, nn, nk),
        in_specs=[a_spec, b_spec] + ([o_spec] if has_res else []),
        out_specs=[o_spec], out_shape=[jax.ShapeDtypeStruct((m, n), out_dtype)],
        scratch_shapes=[pltpu.VMEM((tm, tn), F32)] if nk > 1 else [],
        args=[a, b] + ([res] if has_res else []), semantics=("parallel", "parallel", "arbitrary"), comm=comm)
    return out if comm is None else (out, carried)


def _rms_fwd(x, w, name, tr=512, comm=None):
    t, d = x.shape
    tr = min(tr, t)

    def body(x_ref, w_ref, o_ref):
        xv = x_ref[...]
        r = lax.rsqrt(jnp.mean(xv * xv, axis=-1, keepdims=True) + EPS)
        o_ref[...] = (xv * r * w_ref[...]).astype(BF16)

    out, *carried = _grid_call(
        body, name=name, grid=(t // tr,),
        in_specs=[pl.BlockSpec((tr, d), lambda i: (i, 0)), pl.BlockSpec((1, d), lambda i: (0, 0))],
        out_specs=[pl.BlockSpec((tr, d), lambda i: (i, 0))],
        out_shape=[jax.ShapeDtypeStruct((t, d), BF16)], args=(x, w), semantics=("parallel",), comm=comm)
    return out if comm is None else (out, carried)


def _rms_bwd(x, w, dy, dres, name, tr=256):
    t, d = x.shape
    tr = min(tr, t)

    def body(x_ref, w_ref, dy_ref, dres_ref, dx_ref, dxb_ref, dw_ref):
        xv = x_ref[...]
        dyv = dy_ref[...].astype(F32)
        r = lax.rsqrt(jnp.mean(xv * xv, axis=-1, keepdims=True) + EPS)
        gy = dyv * w_ref[...]
        proj = jnp.sum(gy * xv, axis=-1, keepdims=True) * (1.0 / d)
        dx = dres_ref[...] + r * gy - xv * (r * r * r) * proj
        dx_ref[...] = dx
        dxb_ref[...] = dx.astype(BF16)
        part = jnp.sum(dyv * xv * r, axis=0, keepdims=True)

        @pl.when(pl.program_id(0) == 0)
        def _():
            dw_ref[...] = part

        @pl.when(pl.program_id(0) > 0)
        def _():
            dw_ref[...] += part

    row = pl.BlockSpec((tr, d), lambda i: (i, 0))
    vec = pl.BlockSpec((1, d), lambda i: (0, 0))
    return pl.pallas_call(
        body, name=name, grid=(t // tr,),
        in_specs=[row, vec, row, row], out_specs=[row, row, vec],
        out_shape=[jax.ShapeDtypeStruct((t, d), F32), jax.ShapeDtypeStruct((t, d), BF16),
                   jax.ShapeDtypeStruct((1, d), F32)],
        compiler_params=_cparams("arbitrary"),
    )(x, w, dy, dres)


def _adamw(parts, w, m, v, name, tr=128):
    r, c = w.shape
    tr = tr if r % tr == 0 else r
    c1 = 1.0 - ADAM_B1 ** ADAM_STEP
    c2 = 1.0 - ADAM_B2 ** ADAM_STEP

    def body(p_ref, w_ref, m_ref, v_ref, g_ref, d_ref, nm_ref, nv_ref):
        g = p_ref[0].astype(F32)
        for s in range(1, N_DEV):
            g = g + p_ref[s].astype(F32)
        nm = ADAM_B1 * m_ref[...] + (1.0 - ADAM_B1) * g
        nv = ADAM_B2 * v_ref[...] + (1.0 - ADAM_B2) * (g * g)
        m_hat = nm / c1
        v_hat = nv / c2
        g_ref[...] = g
        d_ref[...] = -ADAM_LR * (m_hat / (jnp.sqrt(v_hat) + ADAM_EPS) + ADAM_WD * w_ref[...])
        nm_ref[...] = nm
        nv_ref[...] = nv

    blk = pl.BlockSpec((tr, c), lambda i: (i, 0))
    return pl.pallas_call(
        body, name=name, grid=(r // tr,),
        in_specs=[pl.BlockSpec((N_DEV, tr, c), lambda i: (0, i, 0)), blk, blk, blk],
        out_specs=[blk] * 4, out_shape=[jax.ShapeDtypeStruct((r, c), F32)] * 4,
        compiler_params=_cparams("parallel"),
    )(parts, w, m, v)


ROW_TILE, ROW_HEADS = 256, 16
CONV_ROWS, CONV_HEADS = 512, 8


def _window(dest, inner, n_in, out_index):
    if dest is None:
        return inner, 0, [], [], {}
    buf, col0, total = dest
    if buf is None:
        return total, col0, [], [], {}
    return total, col0, [buf], [ANY], {n_in: out_index}


def _skip_ref(body, at, count):
    return body if count == 0 else (lambda *refs: body(*refs[:at], *refs[at + count:]))


def _heads_of(x, nh):
    return [x[:, h * HEAD_DIM:(h + 1) * HEAD_DIM] for h in range(nh)]


def _headnorm_fwd(proj, w, col0, inner, name, tr=ROW_TILE, hb=ROW_HEADS):
    t = proj.shape[0]
    tr = min(tr, t)
    hb = min(hb, inner // HEAD_DIM)
    wc = hb * HEAD_DIM
    c0 = col0 // wc

    def body(x_ref, w_ref, o_ref):
        outs = []
        for xh in _heads_of(x_ref[...], hb):
            r = lax.rsqrt(jnp.mean(xh * xh, axis=-1, keepdims=True) + EPS)
            outs.append((xh * r * w_ref[...]).astype(BF16))
        o_ref[...] = jnp.concatenate(outs, axis=1)

    return pl.pallas_call(
        body, name=name, grid=(t // tr, inner // wc),
        in_specs=[pl.BlockSpec((tr, wc), lambda i, j: (i, j + c0)), pl.BlockSpec((1, HEAD_DIM), lambda i, j: (0, 0))],
        out_specs=pl.BlockSpec((tr, wc), lambda i, j: (i, j)),
        out_shape=jax.ShapeDtypeStruct((t, inner), BF16),
        compiler_params=_cparams("parallel", "parallel"),
    )(proj, w)


def _headnorm_bwd(dy, proj, w, col0, inner, name, tr=ROW_TILE, hb=ROW_HEADS, dest=None):
    t = proj.shape[0]
    tr = min(tr, t)
    hb = min(hb, inner // HEAD_DIM)
    wc = hb * HEAD_DIM
    c0 = col0 // wc
    width, out0, more, more_specs, aliases = _window(dest, inner, 3, 0)

    def body(dy_ref, x_ref, w_ref, dx_ref, dw_ref):
        outs = []
        part = jnp.zeros((1, HEAD_DIM), F32)
        for dyh, xh in zip(_heads_of(dy_ref[...], hb), _heads_of(x_ref[...], hb)):
            r = lax.rsqrt(jnp.mean(xh * xh, axis=-1, keepdims=True) + EPS)
            gy = dyh * w_ref[...]
            pr = jnp.sum(gy * xh, axis=-1, keepdims=True) * (1.0 / HEAD_DIM)
            outs.append((r * gy - xh * (r * r * r) * pr).astype(BF16))
            part = part + jnp.sum(dyh * xh * r, axis=0, keepdims=True)
        dx_ref[...] = jnp.concatenate(outs, axis=1)
        first = (pl.program_id(0) == 0) & (pl.program_id(1) == 0)

        @pl.when(first)
        def _():
            dw_ref[...] = part

        @pl.when(jnp.logical_not(first))
        def _():
            dw_ref[...] += part

    blk = pl.BlockSpec((tr, wc), lambda i, j: (i, j))
    return pl.pallas_call(
        _skip_ref(body, 3, len(more)), name=name, grid=(t // tr, inner // wc),
        in_specs=[blk, pl.BlockSpec((tr, wc), lambda i, j: (i, j + c0)),
                  pl.BlockSpec((1, HEAD_DIM), lambda i, j: (0, 0))] + more_specs,
        out_specs=[pl.BlockSpec((tr, wc), lambda i, j: (i, j + out0 // wc)),
                   pl.BlockSpec((1, HEAD_DIM), lambda i, j: (0, 0))],
        out_shape=[jax.ShapeDtypeStruct((t, width), BF16), jax.ShapeDtypeStruct((1, HEAD_DIM), F32)],
        input_output_aliases=aliases, compiler_params=_cparams("arbitrary", "arbitrary"),
    )(dy, proj, w, *more)


def _gate_fwd(o, proj, zcol0, inner, name, norm_w=None, tr=ROW_TILE, hb=ROW_HEADS):
    t = o.shape[0]
    tr = min(tr, t)
    hb = min(hb, inner // HEAD_DIM)
    wc = hb * HEAD_DIM
    c0 = zcol0 // wc
    has_w = norm_w is not None

    def body(*refs):
        o_ref, z_ref = refs[0], refs[1]
        out_ref = refs[2 + has_w]
        outs = []
        for oh, zh in zip(_heads_of(o_ref[...], hb), _heads_of(z_ref[...], hb)):
            if has_w:
                r = lax.rsqrt(jnp.mean(oh * oh, axis=-1, keepdims=True) + EPS)
                oh = oh * r * refs[2][...]
            outs.append((oh * _silu(zh)).astype(BF16))
        out_ref[...] = jnp.concatenate(outs, axis=1)

    blk = pl.BlockSpec((tr, wc), lambda i, j: (i, j))
    vec = pl.BlockSpec((1, HEAD_DIM), lambda i, j: (0, 0))
    return pl.pallas_call(
        body, name=name, grid=(t // tr, inner // wc),
        in_specs=[blk, pl.BlockSpec((tr, wc), lambda i, j: (i, j + c0))] + ([vec] if has_w else []),
        out_specs=blk, out_shape=jax.ShapeDtypeStruct((t, inner), BF16),
        compiler_params=_cparams("parallel", "parallel"),
    )(*([o, proj] + ([norm_w] if has_w else [])))


def _gate_bwd(dg, o, proj, zcol0, inner, name, do_dtype, norm_w=None, tr=ROW_TILE, hb=ROW_HEADS, dest=None):
    t = o.shape[0]
    tr = min(tr, t)
    hb = min(hb, inner // HEAD_DIM)
    wc = hb * HEAD_DIM
    c0 = zcol0 // wc
    has_w = norm_w is not None
    width, out0, more, more_specs, aliases = _window(dest, inner, 3 + has_w, 1)

    def body(*refs):
        dg_ref, o_ref, z_ref = refs[0], refs[1], refs[2]
        do_ref, dz_ref = refs[3 + has_w], refs[4 + has_w]
        dos, dzs = [], []
        part = jnp.zeros((1, HEAD_DIM), F32)
        for dgh, oh, zh in zip(_heads_of(dg_ref[...], hb), _heads_of(o_ref[...], hb), _heads_of(z_ref[...], hb)):
            dy = dgh * _silu(zh)
            if has_w:
                w = refs[3][...]
                r = lax.rsqrt(jnp.mean(oh * oh, axis=-1, keepdims=True) + EPS)
                on = oh * r
                dzs.append((dgh * on * w * _dsilu(zh)).astype(BF16))
                gy = dy * w
                pr = jnp.sum(gy * oh, axis=-1, keepdims=True) * (1.0 / HEAD_DIM)
                dos.append((r * gy - oh * (r * r * r) * pr).astype(do_dtype))
                part = part + jnp.sum(dy * on, axis=0, keepdims=True)
            else:
                dzs.append((dgh * oh * _dsilu(zh)).astype(BF16))
                dos.append(dy.astype(do_dtype))
        do_ref[...] = jnp.concatenate(dos, axis=1)
        dz_ref[...] = jnp.concatenate(dzs, axis=1)
        if has_w:
            dw_ref = refs[6]
            first = (pl.program_id(0) == 0) & (pl.program_id(1) == 0)

            @pl.when(first)
            def _():
                dw_ref[...] = part

            @pl.when(jnp.logical_not(first))
            def _():
                dw_ref[...] += part

    blk = pl.BlockSpec((tr, wc), lambda i, j: (i, j))
    vec = pl.BlockSpec((1, HEAD_DIM), lambda i, j: (0, 0))
    return pl.pallas_call(
        _skip_ref(body, 3 + has_w, len(more)), name=name, grid=(t // tr, inner // wc),
        in_specs=[blk, blk, pl.BlockSpec((tr, wc), lambda i, j: (i, j + c0))] + ([vec] if has_w else []) + more_specs,
        out_specs=[blk, pl.BlockSpec((tr, wc), lambda i, j: (i, j + out0 // wc))] + ([vec] if has_w else []),
        out_shape=[jax.ShapeDtypeStruct((t, inner), do_dtype), jax.ShapeDtypeStruct((t, width), BF16)]
        + ([jax.ShapeDtypeStruct((1, HEAD_DIM), F32)] if has_w else []),
        input_output_aliases=aliases, compiler_params=_cparams("arbitrary", "arbitrary"),
    )(*([dg, o, proj] + ([norm_w] if has_w else []) + more))


def _loss_grad(h, target, name, tr=512):
    t, d = h.shape
    tr = min(tr, t)

    def body(h_ref, t_ref, g_ref, gb_ref, l_ref):
        e = h_ref[...] - t_ref[...]
        g = e * (1.0 / d)
        g_ref[...] = g
        gb_ref[...] = g.astype(BF16)
        part = jnp.zeros((1, HEAD_DIM), F32) + 0.5 * jnp.sum(jnp.sum(e * e, axis=-1, keepdims=True) * (1.0 / d))

        @pl.when(pl.program_id(0) == 0)
        def _():
            l_ref[...] = part

        @pl.when(pl.program_id(0) > 0)
        def _():
            l_ref[...] += part

    row = pl.BlockSpec((tr, d), lambda i: (i, 0))
    return pl.pallas_call(
        body, name=name, grid=(t // tr,),
        in_specs=[row, row], out_specs=[row, row, pl.BlockSpec((1, HEAD_DIM), lambda i: (0, 0))],
        out_shape=[jax.ShapeDtypeStruct((t, d), F32), jax.ShapeDtypeStruct((t, d), BF16),
                   jax.ShapeDtypeStruct((1, HEAD_DIM), F32)],
        compiler_params=_cparams("arbitrary"),
    )(h, target)


N_REL = 2 * REL_CLIP + 1
REL_PAD = 640
WIN = 2 * Q_TILE


def _diag_onehot():
    i = lax.broadcasted_iota(jnp.int32, (REL_PAD, WIN), 0)
    j = lax.broadcasted_iota(jnp.int32, (REL_PAD, WIN), 1)
    rel = jnp.where(j < Q_TILE + CHUNK, Q_TILE - j, Q_TILE + WIN - j)
    used = (j < Q_TILE + CHUNK) | (j > WIN - CHUNK)
    idx = jnp.clip(rel, -REL_CLIP, REL_CLIP) + REL_CLIP
    return jnp.where(used & (i == idx), 1.0, 0.0).astype(F32)


def _band_mask():
    r = lax.broadcasted_iota(jnp.int32, (Q_TILE, WIN), 0) // CHUNK
    kc = lax.broadcasted_iota(jnp.int32, (Q_TILE, WIN), 1) // CHUNK - LEFT_CHUNKS
    return (kc <= r) & (kc >= r - LEFT_CHUNKS)


def _bias_tiles(rel_bias_pad, name):
    nh = rel_bias_pad.shape[0]

    def body(rb_ref, o_ref):
        dvec = _nn(rb_ref[...], _diag_onehot(), HIGHEST)[0:1, :]
        tile = pltpu.roll(jnp.broadcast_to(dvec, (Q_TILE, WIN)), 0, 1, stride=1, stride_axis=0)
        o_ref[...] = jnp.where(_band_mask(), tile, NEG_BIG)

    return pl.pallas_call(
        body, name=name, grid=(nh,),
        in_specs=[pl.BlockSpec((None, 8, REL_PAD), lambda h: (h, 0, 0))],
        out_specs=pl.BlockSpec((None, Q_TILE, WIN), lambda h: (h, 0, 0)),
        out_shape=jax.ShapeDtypeStruct((nh, Q_TILE, WIN), F32),
        compiler_params=_cparams("parallel"),
    )(rel_bias_pad)


def _bias_grad(dtile, name):
    nh = dtile.shape[0]

    def body(d_ref, o_ref):
        ri = lax.broadcasted_iota(jnp.int32, (Q_TILE, Q_TILE), 0)
        ci = lax.broadcasted_iota(jnp.int32, (Q_TILE, Q_TILE), 1)
        flip = jnp.where(ri + ci == Q_TILE - 1, 1.0, 0.0).astype(F32)
        rev = _dot_exact(flip, d_ref[...], NN, True)
        rolled = pltpu.roll(rev, WIN - (Q_TILE - 1), 1, stride=1, stride_axis=0)
        diag = jnp.broadcast_to(jnp.sum(rolled, axis=0, keepdims=True), (8, WIN))
        o_ref[...] = _nt(diag, _diag_onehot(), HIGHEST)

    return pl.pallas_call(
        body, name=name, grid=(nh,),
        in_specs=[pl.BlockSpec((None, Q_TILE, WIN), lambda h: (h, 0, 0))],
        out_specs=pl.BlockSpec((None, 8, REL_PAD), lambda h: (h, 0, 0)),
        out_shape=jax.ShapeDtypeStruct((nh, 8, REL_PAD), F32),
        compiler_params=_cparams("parallel"),
    )(dtile)


GROUP = 2 * CHUNK
BAND = Q_TILE + GROUP


def _band_rows(r0_ref, r1_ref, g):
    return jnp.concatenate([r0_ref[GROUP * g:, :], r1_ref[:GROUP * (g + 1), :]], axis=0)


N_GROUPS = Q_TILE // GROUP


def _groups(ref):
    return jnp.stack([ref[GROUP * g:GROUP * (g + 1), :] for g in range(N_GROUPS)])


def _bands(r0_ref, r1_ref):
    return jnp.stack([_band_rows(r0_ref, r1_ref, g) for g in range(N_GROUPS)])


def _group_probs(q, kw, b_ref, first_tile):
    bias = jnp.stack([b_ref[GROUP * g:GROUP * (g + 1), GROUP * g:GROUP * g + BAND] for g in range(N_GROUPS)])
    s = _dot(q, kw, BNT) * (HEAD_DIM ** -0.5) + bias
    col = (lax.broadcasted_iota(jnp.int32, (N_GROUPS, GROUP, BAND), 2)
           + GROUP * lax.broadcasted_iota(jnp.int32, (N_GROUPS, GROUP, BAND), 0))
    s = jnp.where(first_tile & (col < Q_TILE), NEG_BIG, s)
    p = jnp.exp(s - jnp.max(s, axis=-1, keepdims=True))
    return p * (1.0 / jnp.sum(p, axis=-1, keepdims=True))


def _attn_fwd(q, k, v, bias, name):
    t, inner = q.shape
    nh, nt = inner // HEAD_DIM, t // Q_TILE

    def body(q_ref, k0_ref, k1_ref, v0_ref, v1_ref, b_ref, o_ref):
        p = _group_probs(_groups(q_ref), _bands(k0_ref, k1_ref), b_ref, pl.program_id(1) == 0)
        o = _dot(_bf(p), _bands(v0_ref, v1_ref), BNN)
        for g in range(N_GROUPS):
            o_ref[GROUP * g:GROUP * (g + 1), :] = o[g]

    cur = pl.BlockSpec((Q_TILE, HEAD_DIM), lambda h, i: (i, h))
    prev = pl.BlockSpec((Q_TILE, HEAD_DIM), lambda h, i: (jnp.maximum(i - 1, 0), h))
    return pl.pallas_call(
        body, name=name, grid=(nh, nt),
        in_specs=[cur, prev, cur, prev, cur, pl.BlockSpec((None, Q_TILE, WIN), lambda h, i: (h, 0, 0))],
        out_specs=cur, out_shape=jax.ShapeDtypeStruct((t, inner), F32),
        compiler_params=_cparams("parallel", "parallel"),
    )(q, k, k, v, v, bias)


def _attn_bwd(q, k, v, do, bias, name, dest=None):
    t, inner = q.shape
    nh, nt = inner // HEAD_DIM, t // Q_TILE
    scale = HEAD_DIM ** -0.5

    def body(q_ref, k0_ref, k1_ref, v0_ref, v1_ref, do_ref, b_ref, dq_ref, dk_ref, dv_ref, db_ref,
             ck_ref, cv_ref, wk_ref, wv_ref):
        i = pl.program_id(1)

        @pl.when(i == 0)
        def _():
            ck_ref[...] = jnp.zeros(blk, F32)
            cv_ref[...] = jnp.zeros(blk, F32)
            db_ref[...] = jnp.zeros((Q_TILE, WIN), F32)

        @pl.when(i < nt)
        def _():
            wk_ref[...] = jnp.zeros((WIN, HEAD_DIM), F32)
            wv_ref[...] = jnp.zeros((WIN, HEAD_DIM), F32)
            qv, dov = _groups(q_ref), _groups(do_ref)
            kw, vw = _bands(k0_ref, k1_ref), _bands(v0_ref, v1_ref)
            p = _group_probs(qv, kw, b_ref, i == 0)
            dp = _dot(dov, vw, BNT)
            ds = p * (dp - jnp.sum(p * dp, axis=-1, keepdims=True))
            pb, dsb = _bf(p), _bf(ds)
            dq = _dot(dsb, kw, BNN) * scale
            dkw = _dot(dsb, qv, BTN) * scale
            dvw = _dot(pb, dov, BTN)
            for g in range(N_GROUPS):
                rows, cols = slice(GROUP * g, GROUP * (g + 1)), slice(GROUP * g, GROUP * g + BAND)
                db_ref[rows, cols] += ds[g]
                dq_ref[rows, :] = dq[g]
                wk_ref[cols, :] += dkw[g]
                wv_ref[cols, :] += dvw[g]
            dk_ref[...] = ck_ref[...] + wk_ref[:Q_TILE, :]
            dv_ref[...] = (cv_ref[...] + wv_ref[:Q_TILE, :]).astype(BF16)
            ck_ref[...] = wk_ref[Q_TILE:, :]
            cv_ref[...] = wv_ref[Q_TILE:, :]

        @pl.when(i == nt)
        def _():
            dk_ref[...] = ck_ref[...]
            dv_ref[...] = cv_ref[...].astype(BF16)

    blk = (Q_TILE, HEAD_DIM)
    cur = pl.BlockSpec(blk, lambda h, i: (jnp.minimum(i, nt - 1), h))
    prev = pl.BlockSpec(blk, lambda h, i: (jnp.clip(i - 1, 0, nt - 1), h))
    lag = pl.BlockSpec(blk, lambda h, i: (jnp.maximum(i - 1, 0), h))
    tile = pl.BlockSpec((None, Q_TILE, WIN), lambda h, i: (h, 0, 0))
    width, out0, more, more_specs, aliases = _window(dest, inner, 7, 2)
    return pl.pallas_call(
        _skip_ref(body, 7, len(more)), name=name, grid=(nh, nt + 1),
        in_specs=[cur, prev, cur, prev, cur, cur, tile] + more_specs,
        out_specs=[cur, lag, pl.BlockSpec(blk, lambda h, i: (jnp.maximum(i - 1, 0), h + out0 // HEAD_DIM)), tile],
        out_shape=[jax.ShapeDtypeStruct((t, inner), F32)] * 2 + [jax.ShapeDtypeStruct((t, width), BF16),
                                                                 jax.ShapeDtypeStruct((nh, Q_TILE, WIN), F32)],
        scratch_shapes=[pltpu.VMEM(blk, F32), pltpu.VMEM(blk, F32),
                        pltpu.VMEM((WIN, HEAD_DIM), F32), pltpu.VMEM((WIN, HEAD_DIM), F32)],
        input_output_aliases=aliases, compiler_params=_cparams("arbitrary", "arbitrary"),
    )(q, k, k, v, v, do, bias, *more)


def _pad_rel_bias(rel_bias):
    nh = rel_bias.shape[0]
    return jnp.broadcast_to(jnp.pad(rel_bias, ((0, 0), (0, REL_PAD - N_REL)))[:, None, :], (nh, 8, REL_PAD))


def _layer_b_fwd(h1, nw, w_in, qw, kw, bias, w_out):
    t, d = h1.shape
    inner = w_out.shape[0]
    hn = _rms_fwd(h1, nw, "b_rms")
    proj = _mm(hn, w_in, "nn", t, 4 * inner, d, out_dtype=F32, name="b_proj")
    qn = _headnorm_fwd(proj, qw, 0, inner, "b_qnorm")
    kn = _headnorm_fwd(proj, kw, inner, inner, "b_knorm")
    vb = proj[:, 2 * inner:3 * inner].astype(BF16)
    o = _attn_fwd(qn, kn, vb, bias, "b_attn")
    g = _gate_fwd(o, proj, 3 * inner, inner, "b_gate")
    h2 = _mm(g, w_out, "nn", t, d, inner, out_dtype=F32, name="b_out", res=h1)
    return h2, (hn, proj, qn, kn, vb, o, g)


def _layer_b_bwd(dh2, dh2b, h1, nw, w_in, qw, kw, bias, w_out, saved):
    hn, proj, qn, kn, vb, o, g = saved
    t, d = h1.shape
    inner = w_out.shape[0]
    dg = _mm(dh2b, w_out, "nt", t, inner, d, out_dtype=F32, name="b_dgate")
    dw_out = _mm(g, dh2b, "tn", inner, d, t, out_dtype=F32, name="b_dwout")
    do, dproj = _gate_bwd(dg, o, proj, 3 * inner, inner, "b_gate_bwd", BF16, dest=(None, 3 * inner, 4 * inner))
    dq, dk, dproj, dtile = _attn_bwd(qn, kn, vb, do, bias, "b_attn_bwd", dest=(dproj, 2 * inner, 4 * inner))
    dproj, dqw = _headnorm_bwd(dq, proj, qw, 0, inner, "b_qnorm_bwd", dest=(dproj, 0, 4 * inner))
    dproj, dkw = _headnorm_bwd(dk, proj, kw, inner, inner, "b_knorm_bwd", dest=(dproj, inner, 4 * inner))
    dhn = _mm(dproj, w_in, "nt", t, d, 4 * inner, out_dtype=F32, name="b_dhn")
    dw_in = _mm(hn, dproj, "tn", d, 4 * inner, t, out_dtype=F32, name="b_dwin")
    dh1, dh1b, dnw = _rms_bwd(h1, nw, dhn, dh2, "b_rms_bwd")
    drb = _bias_grad(dtile, "b_bias_grad")[:, 0, :N_REL]
    return dh1, dh1b, dnw, dw_in, dqw, dkw, drb, dw_out


LANES = 128


def _softplus(x):
    return jnp.maximum(x, 0.0) + jnp.log1p(jnp.exp(-jnp.abs(x)))


def _gates_fwd(ab, alog_row, dt_row, nh, name, tr=1024):
    t = ab.shape[0]
    tr = min(tr, t)

    def body(x_ref, al_ref, dt_ref, o_ref):
        x = x_ref[...]
        lane = lax.broadcasted_iota(jnp.int32, x.shape, 1)
        g = -jnp.exp(al_ref[...]) * _softplus(x + dt_ref[...])
        o_ref[...] = jnp.where(lane < nh, g, jnp.where(lane < 2 * nh, _sigmoid(x), 0.0))

    row = pl.BlockSpec((tr, LANES), lambda i: (i, 0))
    vec = pl.BlockSpec((1, LANES), lambda i: (0, 0))
    return pl.pallas_call(
        body, name=name, grid=(t // tr,), in_specs=[row, vec, vec], out_specs=row,
        out_shape=jax.ShapeDtypeStruct((t, LANES), F32), compiler_params=_cparams("parallel"),
    )(ab, alog_row, dt_row)


def _gates_bwd(ab, alog_row, dt_row, dgates, nh, name, tr=1024):
    t = ab.shape[0]
    tr = min(tr, t)
    npart = dgates.shape[0]

    def body(x_ref, al_ref, dt_ref, dg_ref, dx_ref, s_ref):
        x = x_ref[...]
        lane = lax.broadcasted_iota(jnp.int32, x.shape, 1)
        dgt = dg_ref[0]
        for p in range(1, npart):
            dgt = dgt + dg_ref[p]
        ea = jnp.exp(al_ref[...])
        xa = x + dt_ref[...]
        da = jnp.where(lane < nh, dgt * (-ea) * _sigmoid(xa), 0.0)
        beta = _sigmoid(x)
        db = jnp.where((lane >= nh) & (lane < 2 * nh), dgt * beta * (1.0 - beta), 0.0)
        dx_ref[...] = (da + db).astype(BF16)
        dal = jnp.sum(jnp.where(lane < nh, dgt * (-ea) * _softplus(xa), 0.0), axis=0, keepdims=True)
        ddt = jnp.sum(da, axis=0, keepdims=True)
        r8 = lax.broadcasted_iota(jnp.int32, (8, LANES), 0)
        part = jnp.where(r8 == 0, dal, jnp.where(r8 == 1, ddt, 0.0))

        @pl.when(pl.program_id(0) == 0)
        def _():
            s_ref[...] = part

        @pl.when(pl.program_id(0) > 0)
        def _():
            s_ref[...] += part

    row = pl.BlockSpec((tr, LANES), lambda i: (i, 0))
    vec = pl.BlockSpec((1, LANES), lambda i: (0, 0))
    return pl.pallas_call(
        body, name=name, grid=(t // tr,),
        in_specs=[row, vec, vec, pl.BlockSpec((npart, tr, LANES), lambda i: (0, i, 0))],
        out_specs=[row, pl.BlockSpec((8, LANES), lambda i: (0, 0))],
        out_shape=[jax.ShapeDtypeStruct((t, LANES), BF16), jax.ShapeDtypeStruct((8, LANES), F32)],
        compiler_params=_cparams("arbitrary"),
    )(ab, alog_row, dt_row, dgates)


HALO = 8


def _conv_taps(ext, w, rows):
    acc = ext[HALO:HALO + rows] * w[CONV_K - 1:CONV_K]
    for s in range(1, CONV_K):
        acc = acc + pltpu.roll(ext, s, 0)[HALO:HALO + rows] * w[CONV_K - 1 - s:CONV_K - s]
    return acc


def _conv_fwd(proj, conv_w, col0, inner, mode, name, tt=CONV_ROWS, hb=CONV_HEADS):
    t = proj.shape[0]
    tt = min(tt, t)
    hb = min(hb, inner // HEAD_DIM)
    wc = hb * HEAD_DIM
    c0 = col0 // wc
    hpb = tt // HALO

    def body(x_ref, halo_ref, w_ref, o_ref):
        halo = jnp.where(pl.program_id(1) == 0, 0.0, halo_ref[...])
        s = _silu(_conv_taps(jnp.concatenate([halo, x_ref[...]], axis=0), w_ref[...], tt))
        if mode == "v":
            o_ref[...] = s
        else:
            mul = HEAD_DIM ** -0.5 if mode == "q" else 1.0
            o_ref[...] = jnp.concatenate(
                [sh * (lax.rsqrt(jnp.sum(sh * sh, axis=-1, keepdims=True) + EPS) * mul) for sh in _heads_of(s, hb)], axis=1)

    return pl.pallas_call(
        body, name=name, grid=(inner // wc, t // tt),
        in_specs=[pl.BlockSpec((tt, wc), lambda j, i: (i, j + c0)),
                  pl.BlockSpec((HALO, wc), lambda j, i: (jnp.maximum(i * hpb - 1, 0), j + c0)),
                  pl.BlockSpec((CONV_K, wc), lambda j, i: (0, j + c0))],
        out_specs=pl.BlockSpec((tt, wc), lambda j, i: (i, j)),
        out_shape=jax.ShapeDtypeStruct((t, inner), F32),
        compiler_params=_cparams("parallel", "parallel"),
    )(proj, proj, conv_w)


def _conv_bwd(dy, proj, conv_w, col0, inner, mode, name, tt=CONV_ROWS, hb=CONV_HEADS, dest=None):
    t = proj.shape[0]
    tt = min(tt, t)
    nt = t // tt
    hb = min(hb, inner // HEAD_DIM)
    wc = hb * HEAD_DIM
    c0 = col0 // wc
    hpb = tt // HALO
    rows = tt + HALO

    def body(dy_ref, dyn_ref, x_ref, xp_ref, xn_ref, w_ref, dx_ref, dw_ref):
        i = pl.program_id(1)
        w = w_ref[...]
        xprev = jnp.where(i == 0, 0.0, xp_ref[...])
        ext = jnp.concatenate([xprev, x_ref[...], xn_ref[...]], axis=0)
        c = _conv_taps(ext, w, rows)
        dyv = jnp.concatenate([dy_ref[...], jnp.where(i == nt - 1, 0.0, dyn_ref[...])], axis=0)
        sg = _sigmoid(c)
        s = c * sg
        if mode == "v":
            ds = dyv
        else:
            mul = HEAD_DIM ** -0.5 if mode == "q" else 1.0
            parts = []
            for dyh, sh in zip(_heads_of(dyv, hb), _heads_of(s, hb)):
                r = lax.rsqrt(jnp.sum(sh * sh, axis=-1, keepdims=True) + EPS)
                parts.append(mul * (r * dyh - sh * (r * r * r) * jnp.sum(dyh * sh, axis=-1, keepdims=True)))
            ds = jnp.concatenate(parts, axis=1)
        dc = ds * (sg * (1.0 + c * (1.0 - sg)))
        dx = dc[:tt] * w[CONV_K - 1:CONV_K]
        for sft in range(1, CONV_K):
            dx = dx + pltpu.roll(dc, rows - sft, 0)[:tt] * w[CONV_K - 1 - sft:CONV_K - sft]
        dx_ref[...] = dx.astype(BF16)
        r8 = lax.broadcasted_iota(jnp.int32, (8, wc), 0)
        part = jnp.zeros((8, wc), F32)
        for sft in range(CONV_K):
            xs = ext[HALO:HALO + tt] if sft == 0 else pltpu.roll(ext, sft, 0)[HALO:HALO + tt]
            part = part + jnp.where(r8 == CONV_K - 1 - sft, jnp.sum(dc[:tt] * xs, axis=0, keepdims=True), 0.0)

        @pl.when(i == 0)
        def _():
            dw_ref[...] = part

        @pl.when(i > 0)
        def _():
            dw_ref[...] += part

    cur = lambda off: pl.BlockSpec((tt, wc), lambda j, i: (i, j + off))
    nxt = lambda off: pl.BlockSpec((HALO, wc), lambda j, i: (jnp.minimum((i + 1) * hpb, t // HALO - 1), j + off))
    width, out0, more, more_specs, aliases = _window(dest, inner, 6, 0)
    return pl.pallas_call(
        _skip_ref(body, 6, len(more)), name=name, grid=(inner // wc, nt),
        in_specs=[cur(0), nxt(0), cur(c0),
                  pl.BlockSpec((HALO, wc), lambda j, i: (jnp.maximum(i * hpb - 1, 0), j + c0)), nxt(c0),
                  pl.BlockSpec((CONV_K, wc), lambda j, i: (0, j + c0))] + more_specs,
        out_specs=[pl.BlockSpec((tt, wc), lambda j, i: (i, j + out0 // wc)),
                   pl.BlockSpec((8, wc), lambda j, i: (0, j))],
        out_shape=[jax.ShapeDtypeStruct((t, width), BF16), jax.ShapeDtypeStruct((8, inner), F32)],
        input_output_aliases=aliases, compiler_params=_cparams("parallel", "arbitrary"),
    )(dy, dy, proj, proj, proj, conv_w, *more)


GDN_HB = 4
GDN_NB = 8
SCAN_HB = 16
SCAN_NB = 2


def _iota2(n, m):
    return lax.broadcasted_iota(jnp.int32, (n, m), 0), lax.broadcasted_iota(jnp.int32, (n, m), 1)


def _head_select(first_head, hb, lane0):
    r, lane = _iota2(8, LANES)
    return jnp.where((r < hb) & (lane == lane0 + first_head + r), 1.0, 0.0).astype(F32)


def _chunk_gates(gt, selg, selb):
    i, j = _iota2(CHUNK, CHUNK)
    gc_all = _dot_exact(jnp.where(j <= i, 1.0, 0.0), gt, NN, True)
    return (_dot_exact(gc_all, selg, NT, False), _dot_exact(selg, gc_all, NT, True),
            _dot_exact(gt, selb, NT, False))


def _decay_terms(gcol, grow):
    i, j = _iota2(CHUNK, CHUNK)
    glast = gcol[:, CHUNK - 1:CHUNK, :]
    decay = jnp.exp(jnp.where(j <= i, gcol - grow, NEG_BIG))
    return jnp.exp(gcol), jnp.exp(glast - gcol), jnp.exp(glast), decay


def _unit_lower_inverse(a):
    i, j = _iota2(CHUNK, CHUNK)
    same16 = (i // 16) == (j // 16)
    same32 = (i // 32) == (j // 32)
    m = jnp.where(same16, -a, 0.0)
    x = jnp.where(i == j, 1.0, 0.0) + m
    for _ in range(3):
        m = _dot3(m, m, BNN)
        x = x + _dot3(x, m, BNN)
    for off in (jnp.where(same32 & jnp.logical_not(same16), a, 0.0), jnp.where(same32, 0.0, a)):
        x = x - _dot3(_dot3(x, off, BNN), x, BNN)
    return x


def _unit_inputs(refs, g_ref, selg, selb, hb, nb):
    units = [(c, h) for c in range(nb) for h in range(hb)]
    rs = lambda c: slice(c * CHUNK, (c + 1) * CHUNK)
    cs = lambda h: slice(h * HEAD_DIM, (h + 1) * HEAD_DIM)
    gates = [_chunk_gates(g_ref[rs(c), :], selg, selb) for c in range(nb)]
    stacked = [jnp.stack([r[rs(c), cs(h)] for c, h in units]) for r in refs]
    gcol = jnp.stack([gates[c][0][:, h:h + 1] for c, h in units])
    grow = jnp.stack([gates[c][1][h:h + 1, :] for c, h in units])
    bcol = jnp.stack([gates[c][2][:, h:h + 1] for c, h in units])
    return units, rs, cs, stacked, gcol, grow, bcol


def _gdn_specs(nh, inner, t, heads=GDN_HB, chunks=GDN_NB):
    hb, nb = min(heads, nh), chunks
    rows = nb * CHUNK
    wide = pl.BlockSpec((rows, hb * HEAD_DIM), lambda g, n: (n, g))
    sq = pl.BlockSpec((hb, rows, CHUNK), lambda g, n: (g, n, 0))
    gts = pl.BlockSpec((rows, LANES), lambda g, n: (n, 0))
    glb = pl.BlockSpec((nb * 8, hb * HEAD_DIM), lambda g, n: (n, g))
    return hb, nb, rows, wide, sq, gts, glb


def _gdn_intra_fwd(q, k, v, gates, nh, name, comm=None):
    t, inner = q.shape
    hb, nb, rows, wide, sq, gts, glb = _gdn_specs(nh, inner, t)

    def body(q_ref, k_ref, v_ref, g_ref, qe_ref, kel_ref, wb_ref, w_ref, u_ref, qk_ref, tm_ref, gl_ref):
        first = pl.program_id(0) * hb
        selg, selb = _head_select(first, hb, 0), _head_select(first, hb, nh)
        i, j = _iota2(CHUNK, CHUNK)
        units, rs, cs, (qv, kv, vv), gcol, grow, bcol = _unit_inputs(
            (q_ref, k_ref, v_ref), g_ref, selg, selb, hb, nb)
        e, el, gl, decay = _decay_terms(gcol, grow)
        kb = kv * bcol
        qbf, kbf = _bf(qv), _bf(kv)
        a = jnp.where(j < i, _dot(_bf(kb), kbf, BNT) * decay, 0.0)
        tm = _unit_lower_inverse(a)
        uw = _dot3(tm, jnp.concatenate([vv * bcol, kb * e], axis=2), BNN)
        qk = _bf(_dot(qbf, kbf, BNT) * decay)
        qe, kel = _bf(qv * e), _bf(kv * el)
        for n, (c, h) in enumerate(units):
            w = uw[n, :, HEAD_DIM:]
            qe_ref[rs(c), cs(h)] = qe[n]
            kel_ref[rs(c), cs(h)] = kel[n]
            wb_ref[rs(c), cs(h)] = _bf(w)
            w_ref[rs(c), cs(h)] = w
            u_ref[rs(c), cs(h)] = uw[n, :, :HEAD_DIM]
            qk_ref[h, rs(c), :] = qk[n]
            tm_ref[h, rs(c), :] = tm[n]
            gl_ref[c * 8:(c + 1) * 8, cs(h)] = jnp.broadcast_to(gl[n], (8, HEAD_DIM))

    big = lambda dt: jax.ShapeDtypeStruct((t, inner), dt)
    return _grid_call(
        body, name=name, grid=(nh // hb, t // rows),
        in_specs=[wide, wide, wide, gts],
        out_specs=[wide] * 5 + [sq, sq, glb],
        out_shape=[big(BF16), big(BF16), big(BF16), big(F32), big(F32),
                   jax.ShapeDtypeStruct((nh, t, CHUNK), BF16), jax.ShapeDtypeStruct((nh, t, CHUNK), F32),
                   jax.ShapeDtypeStruct((t // CHUNK * 8, inner), F32)],
        args=(q, k, v, gates), semantics=("parallel", "parallel"), comm=comm)


def _gdn_scan_fwd(qe, kel, wb, u, qk, glb, nh, name):
    t, inner = u.shape
    hb, nb, rows, wide, sq, _, glb_spec = _gdn_specs(nh, inner, t, SCAN_HB, SCAN_NB)

    def body(qe_ref, kel_ref, wb_ref, u_ref, qk_ref, gl_ref, o_ref, vn_ref, sall_ref, s_ref):
        @pl.when(pl.program_id(1) == 0)
        def _():
            s_ref[...] = jnp.zeros(s_ref.shape, F32)

        cs = lambda h: slice(h * HEAD_DIM, (h + 1) * HEAD_DIM)
        for c in range(nb):
            rs = slice(c * CHUNK, (c + 1) * CHUNK)
            heads = lambda ref: jnp.stack([ref[rs, cs(h)] for h in range(hb)])
            s = s_ref[...]
            sall_ref[c] = s
            sb = _bf(s)
            vn = heads(u_ref) - _dot(heads(wb_ref), sb, BNN)
            vnb = _bf(vn)
            o = _dot(heads(qe_ref), sb, BNN) + _dot(qk_ref[:, rs, :], vnb, BNN)
            gl = jnp.stack([gl_ref[c * 8:c * 8 + 1, cs(h)] for h in range(hb)])
            s_ref[...] = s * gl + _dot(heads(kel_ref), vnb, BTN)
            for h in range(hb):
                vn_ref[rs, cs(h)] = vn[h]
                o_ref[rs, cs(h)] = o[h]

    return pl.pallas_call(
        body, name=name, grid=(nh // hb, t // rows),
        in_specs=[wide, wide, wide, wide, sq, glb_spec],
        out_specs=[wide, wide, pl.BlockSpec((nb, hb, HEAD_DIM, HEAD_DIM), lambda g, n: (n, g, 0, 0))],
        out_shape=[jax.ShapeDtypeStruct((t, inner), F32), jax.ShapeDtypeStruct((t, inner), F32),
                   jax.ShapeDtypeStruct((t // CHUNK, nh, HEAD_DIM, HEAD_DIM), F32)],
        scratch_shapes=[pltpu.VMEM((hb, HEAD_DIM, HEAD_DIM), F32)],
        compiler_params=_cparams("parallel", "arbitrary"),
    )(qe, kel, wb, u, qk, glb)


def _gdn_scan_bwd(do, qe, kel, wb, vn, qk, glb, sall, nh, name):
    t, inner = do.shape
    hb, nb, rows, _, _, _, _ = _gdn_specs(nh, inner, t, SCAN_HB, SCAN_NB)
    last = t // rows - 1
    wide = pl.BlockSpec((rows, hb * HEAD_DIM), lambda g, n: (last - n, g))
    sq = pl.BlockSpec((hb, rows, CHUNK), lambda g, n: (g, last - n, 0))
    glb_spec = pl.BlockSpec((nb * 8, hb * HEAD_DIM), lambda g, n: (last - n, g))

    def body(do_ref, qe_ref, kel_ref, wb_ref, vn_ref, qk_ref, gl_ref, sall_ref,
             dvn_ref, dw_ref, dqe_ref, dkel_ref, dqk_ref, dgl_ref, ds_ref):
        @pl.when(pl.program_id(1) == 0)
        def _():
            ds_ref[...] = jnp.zeros(ds_ref.shape, F32)

        cs = lambda h: slice(h * HEAD_DIM, (h + 1) * HEAD_DIM)
        for c in reversed(range(nb)):
            rs = slice(c * CHUNK, (c + 1) * CHUNK)
            heads = lambda ref: jnp.stack([ref[rs, cs(h)] for h in range(hb)])
            ds, s = ds_ref[...], sall_ref[c]
            dsb, sb = _bf(ds), _bf(s)
            dob, vnb = _bf(heads(do_ref)), _bf(heads(vn_ref))
            dvn = _dot(qk_ref[:, rs, :], dob, BTN) + _dot(heads(kel_ref), dsb, BNN)
            dvnb = _bf(dvn)
            dw = -_dot(dvnb, sb, BNT)
            dqe = _dot(dob, sb, BNT)
            dkel = _dot(vnb, dsb, BNT)
            dqk_ref[:, rs, :] = _dot(dob, vnb, BNT)
            dgl = jnp.sum(jnp.sum(ds * s, axis=2, keepdims=True), axis=1, keepdims=True)
            gl = jnp.stack([gl_ref[c * 8:c * 8 + 1, cs(h)] for h in range(hb)])
            ds_ref[...] = ds * gl + _dot(heads(qe_ref), dob, BTN) - _dot(heads(wb_ref), dvnb, BTN)
            for h in range(hb):
                dvn_ref[rs, cs(h)] = dvn[h]
                dw_ref[rs, cs(h)] = dw[h]
                dqe_ref[rs, cs(h)] = dqe[h]
                dkel_ref[rs, cs(h)] = dkel[h]
                dgl_ref[c * 8:(c + 1) * 8, cs(h)] = jnp.broadcast_to(dgl[h], (8, HEAD_DIM))

    big = jax.ShapeDtypeStruct((t, inner), F32)
    return pl.pallas_call(
        body, name=name, grid=(nh // hb, t // rows),
        in_specs=[wide, wide, wide, wide, wide, sq, glb_spec,
                  pl.BlockSpec((nb, hb, HEAD_DIM, HEAD_DIM), lambda g, n: (last - n, g, 0, 0))],
        out_specs=[wide] * 4 + [sq, glb_spec],
        out_shape=[big] * 4 + [jax.ShapeDtypeStruct((nh, t, CHUNK), F32),
                               jax.ShapeDtypeStruct((t // CHUNK * 8, inner), F32)],
        scratch_shapes=[pltpu.VMEM((hb, HEAD_DIM, HEAD_DIM), F32)],
        compiler_params=_cparams("parallel", "arbitrary"),
    )(do, qe, kel, wb, vn, qk, glb, sall)


def _gdn_intra_bwd(q, k, v, gates, tm, w, u, dvn, dw, dqe, dkel, dqk, dglb, nh, name, comm=None):
    t, inner = q.shape
    hb, nb, rows, wide, sq, gts, glb = _gdn_specs(nh, inner, t)

    def body(q_ref, k_ref, v_ref, g_ref, tm_ref, w_ref, u_ref, dvn_ref, dw_ref, dqe_ref, dkel_ref, dqk_ref,
             dgl_ref, dq_ref, dk_ref, dv_ref, dg_ref):
        first = pl.program_id(0) * hb
        selg, selb = _head_select(first, hb, 0), _head_select(first, hb, nh)
        i, j = _iota2(CHUNK, CHUNK)
        lane8 = lax.broadcasted_iota(jnp.int32, (CHUNK, 8), 1)
        row = lax.broadcasted_iota(jnp.int32, (CHUNK, 1), 0)
        lower = jnp.where(j <= i, 1.0, 0.0).astype(F32)
        rsum = lambda x: jnp.sum(x, axis=-1, keepdims=True)
        units, rs, cs, (qv, kv, vv, wv, uv, dvn, dw, dqe, dkel), gcol, grow, bcol = _unit_inputs(
            (q_ref, k_ref, v_ref, w_ref, u_ref, dvn_ref, dw_ref, dqe_ref, dkel_ref), g_ref, selg, selb, hb, nb)
        nu = len(units)
        tmv = jnp.stack([tm_ref[h, rs(c), :] for c, h in units])
        dqk = jnp.where(j <= i, jnp.stack([dqk_ref[h, rs(c), :] for c, h in units]), 0.0)
        dgl = jnp.stack([dgl_ref[c * 8:c * 8 + 1, h * HEAD_DIM:h * HEAD_DIM + 1] for c, h in units])
        e, el, gl, decay = _decay_terms(gcol, grow)
        kb = kv * bcol
        qb, kbf, kbb = _bf(qv), _bf(kv), _bf(kb)
        dqkr = _bf(dqk * decay)
        dq = dqe * e + _dot(dqkr, kbf, BNN)
        dk = dkel * el + _dot(dqkr, qb, BTN)
        de = rsum(dqe * qv)
        del_ = rsum(dkel * kv)
        mq = dqk * _dot(qb, kbf, BNT) * decay
        dsol = _dot3(tmv, jnp.concatenate([dvn, dw], axis=2), BTN)
        dvb, dkbe = dsol[:, :, :HEAD_DIM], dsol[:, :, HEAD_DIM:]
        da = -jnp.where(j < i, _dot3(dsol, jnp.concatenate([uv, wv], axis=2), BNT), 0.0)
        dkk = _bf(da * decay)
        ma = da * _dot(kbb, kbf, BNT) * decay
        dkb = dkbe * e + _dot(dkk, kbf, BNN)
        de = de + rsum(dkbe * kb)
        dk = dk + _dot(dkk, kbb, BTN) + dkb * bcol
        dv = dvb * bcol
        dbeta = rsum(dkb * kv) + rsum(dvb * vv)
        m = mq + ma
        ones = jnp.ones((nu, CHUNK, LANES), F32)
        dgc = rsum(m) - _dot_exact(m, ones, BTN, False)[:, :, 0:1] + de * e - del_ * el
        tail = jnp.sum(del_ * el, axis=1, keepdims=True) + dgl * gl
        dgc = dgc + jnp.where(row == CHUNK - 1, tail, 0.0)
        for n, (c, h) in enumerate(units):
            dq_ref[rs(c), cs(h)] = dq[n]
            dk_ref[rs(c), cs(h)] = dk[n]
            dv_ref[rs(c), cs(h)] = dv[n]
        for c in range(nb):
            dgc_cols = jnp.zeros((CHUNK, 8), F32)
            dbeta_cols = jnp.zeros((CHUNK, 8), F32)
            for h in range(hb):
                dgc_cols = jnp.where(lane8 == h, dgc[c * hb + h], dgc_cols)
                dbeta_cols = jnp.where(lane8 == h, dbeta[c * hb + h], dbeta_cols)
            dg_cols = _dot_exact(lower, dgc_cols, TN, True)
            dg_ref[rs(c), :] = _dot_exact(dg_cols, selg, NN, False) + _dot_exact(dbeta_cols, selb, NN, False)

    big = jax.ShapeDtypeStruct((t, inner), F32)
    return _grid_call(
        body, name=name, grid=(nh // hb, t // rows),
        in_specs=[wide, wide, wide, gts, sq, wide, wide, wide, wide, wide, wide, sq, glb],
        out_specs=[wide, wide, wide, pl.BlockSpec((None, rows, LANES), lambda g, n: (g, n, 0))],
        out_shape=[big, big, big, jax.ShapeDtypeStruct((nh // hb, t, LANES), F32)],
        args=(q, k, v, gates, tm, w, u, dvn, dw, dqe, dkel, dqk, dglb), semantics=("parallel", "parallel"),
        comm=comm)


def _layer_a_fwd(x, hn, w_main, w_ab, conv_w, alog_row, dt_row, onw, w_out, nh, comm=None):
    t, d = x.shape
    inner = w_out.shape[0]
    proj = _mm(hn, w_main, "nn", t, 4 * inner, d, out_dtype=F32, name="a_proj")
    ab = _mm(hn, w_ab, "nn", t, LANES, d, out_dtype=F32, name="a_proj_ab")
    gates = _gates_fwd(ab, alog_row, dt_row, nh, "a_gates")
    q = _conv_fwd(proj, conv_w, 0, inner, "q", "a_conv_q")
    k = _conv_fwd(proj, conv_w, inner, inner, "k", "a_conv_k")
    v = _conv_fwd(proj, conv_w, 2 * inner, inner, "v", "a_conv_v")
    qe, kel, wb, w, u, qk, tm, glb, *carried = _gdn_intra_fwd(q, k, v, gates, nh, "a_intra", comm)
    o, vn, sall = _gdn_scan_fwd(qe, kel, wb, u, qk, glb, nh, "a_scan")
    g = _gate_fwd(o, proj, 3 * inner, inner, "a_gate", norm_w=onw)
    h1 = _mm(g, w_out, "nn", t, d, inner, out_dtype=F32, name="a_out", res=x)
    return h1, (hn, proj, ab, gates, q, k, v, qe, kel, wb, w, u, qk, tm, glb, o, vn, sall, g), carried


def _layer_a_bwd(dh1, dh1b, x, nw, w_main, w_ab, conv_w, alog_row, dt_row, onw, w_out, nh, saved, comm, own_comm):
    hn, proj, ab, gates, q, k, v, qe, kel, wb, w, u, qk, tm, glb, o, vn, sall, g = saved
    t, d = x.shape
    inner = w_out.shape[0]
    dg = _mm(dh1b, w_out, "nt", t, inner, d, out_dtype=F32, name="a_dgate")
    dw_out = _mm(g, dh1b, "tn", inner, d, t, out_dtype=F32, name="a_dwout")
    do, dproj, donw = _gate_bwd(dg, o, proj, 3 * inner, inner, "a_gate_bwd", F32, norm_w=onw,
                                dest=(None, 3 * inner, 4 * inner))
    dvn, dw, dqe, dkel, dqk, dglb = _gdn_scan_bwd(do, qe, kel, wb, vn, qk, glb, sall, nh, "a_scan_bwd")
    dq, dk, dv, dgates, *carried = _gdn_intra_bwd(q, k, v, gates, tm, w, u, dvn, dw, dqe, dkel, dqk, dglb, nh,
                                                  "a_intra_bwd", comm)
    dproj, dcq = _conv_bwd(dq, proj, conv_w, 0, inner, "q", "a_conv_q_bwd", dest=(dproj, 0, 4 * inner))
    dproj, dck = _conv_bwd(dk, proj, conv_w, inner, inner, "k", "a_conv_k_bwd", dest=(dproj, inner, 4 * inner))
    dproj, dcv = _conv_bwd(dv, proj, conv_w, 2 * inner, inner, "v", "a_conv_v_bwd",
                           dest=(dproj, 2 * inner, 4 * inner))
    dab, dsmall = _gates_bwd(ab, alog_row, dt_row, dgates, nh, "a_gates_bwd")
    dw_main = _mm(hn, dproj, "tn", d, 4 * inner, t, out_dtype=F32, name="a_dwin")
    dw_ab = _mm(hn, dab, "tn", d, LANES, t, out_dtype=F32, name="a_dwin_ab")
    dconv = jnp.concatenate([dcq[:CONV_K], dck[:CONV_K], dcv[:CONV_K]], axis=1)
    dhn = _mm(dab, w_ab, "nt", t, d, LANES, out_dtype=F32, name="a_dhn_ab")
    own = own_comm(dw_main, dw_ab, dw_out, dconv)
    dhn = _mm(dproj, w_main, "nt", t, d, 4 * inner, out_dtype=F32, name="a_dhn", res=dhn, comm=own)
    dhn, carried_own = dhn if own is not None else (dhn, [])
    dx, _, dnw = _rms_bwd(x, nw, dhn, dh1, "a_rms_bwd")
    return dx, dnw, dsmall, donw, carried, carried_own


def _rows_of(a, rows):
    flat = a.reshape(-1)
    return jnp.pad(flat, (0, rows * LANES - flat.shape[0])).reshape(rows, LANES)


def _to_slabs(g, axis):
    shape = g.shape[:axis] + (N_DEV, g.shape[axis] // N_DEV) + g.shape[axis + 1:]
    return jnp.moveaxis(g.reshape(shape), axis, 0)


def _from_slabs(s, axis):
    m = jnp.moveaxis(s, 0, axis)
    return m.reshape(m.shape[:axis] + (m.shape[axis] * m.shape[axis + 1],) + m.shape[axis + 2:])


def kernel(x, norm_w, a_w_in, a_conv_w, a_a_log, a_dt_bias, a_out_norm_w, a_w_out, b_w_in, b_q_norm_w, b_k_norm_w, b_rel_bias, b_w_out, loss_target, m_norm_w, m_a_w_in, m_a_conv_w, m_a_a_log, m_a_dt_bias, m_a_out_norm_w, m_a_w_out, m_b_w_in, m_b_q_norm_w, m_b_k_norm_w, m_b_rel_bias, m_b_w_out, v_norm_w, v_a_w_in, v_a_conv_w, v_a_a_log, v_a_dt_bias, v_a_out_norm_w, v_a_w_out, v_b_w_in, v_b_q_norm_w, v_b_k_norm_w, v_b_rel_bias, v_b_w_out):
    xs, target = x[0], loss_target[0]
    nh = a_a_log.shape[-1]
    inner = N_DEV * a_w_out.shape[1]

    nw0, nw1 = norm_w[0:1], norm_w[1:2]
    hn0, (ga_in, ga_out, g_conv) = _rms_fwd(
        xs, nw0, "a_rms", comm=_RoutedGather([a_w_in[0].astype(BF16), a_w_out[0].astype(BF16), a_conv_w[0]]))
    wa_in = _from_slabs(ga_in, 1)
    wa_main = wa_in[:, :4 * inner]
    wa_ab = jnp.pad(wa_in[:, 4 * inner:], ((0, 0), (0, LANES - 2 * nh)))
    wa_out = _from_slabs(ga_out, 0)
    conv_w = _from_slabs(g_conv, 1)
    alog_row = jnp.pad(a_a_log, ((0, 0), (0, LANES - nh)))
    dt_row = jnp.pad(a_dt_bias, ((0, 0), (0, LANES - nh)))

    h1, saved_a, (gb_in, gb_out) = _layer_a_fwd(
        xs, hn0, wa_main, wa_ab, conv_w, alog_row, dt_row, a_out_norm_w, wa_out, nh,
        _Comm("gather", [b_w_in[0].astype(BF16), b_w_out[0].astype(BF16)]))
    wb_in = _from_slabs(gb_in, 1)
    wb_out = _from_slabs(gb_out, 0)
    bias = _bias_tiles(_pad_rel_bias(b_rel_bias[0]), "b_bias_tiles")
    h2, saved_b = _layer_b_fwd(h1, nw1, wb_in, b_q_norm_w, b_k_norm_w, bias, wb_out)
    dh2, dh2b, loss_row = _loss_grad(h2, target, "loss")

    dh1, dh1b, dnw1, dwb_in, dqw, dkw, drb, dwb_out = _layer_b_bwd(
        dh2, dh2b, h1, nw1, wb_in, b_q_norm_w, b_k_norm_w, bias, wb_out, saved_b)
    def exchange_a(dwa_main, dwa_ab, dwa_out, dconv):
        dwa_in = jnp.concatenate([dwa_main, dwa_ab[:, :2 * nh]], axis=1)
        return _Comm("exchange", [_to_slabs(dwa_in, 1).astype(BF16), _to_slabs(dwa_out, 0).astype(BF16),
                                  _to_slabs(dconv, 1)])

    dx, dnw0, dsmall, donw, (pb_in, pb_out), (pa_in, pa_out, p_conv) = _layer_a_bwd(
        dh1, dh1b, xs, nw0, wa_main, wa_ab, conv_w, alog_row, dt_row, a_out_norm_w, wa_out, nh, saved_a,
        _Comm("exchange", [_to_slabs(dwb_in, 1).astype(BF16), _to_slabs(dwb_out, 0).astype(BF16)]), exchange_a)
    big = {}
    for name, p, w, m, v in (("a_w_in", pa_in, a_w_in, m_a_w_in, v_a_w_in),
                             ("a_w_out", pa_out, a_w_out, m_a_w_out, v_a_w_out),
                             ("b_w_in", pb_in, b_w_in, m_b_w_in, v_b_w_in),
                             ("b_w_out", pb_out, b_w_out, m_b_w_out, v_b_w_out),
                             ("a_conv_w", p_conv, a_conv_w, m_a_conv_w, v_a_conv_w)):
        big[name] = [o[None] for o in _adamw(p, w[0], m[0], v[0], "adamw_" + name)]

    small = (("norm_w", norm_w, m_norm_w, v_norm_w, jnp.concatenate([dnw0, dnw1], axis=0)),
             ("a_a_log", a_a_log, m_a_a_log, v_a_a_log, dsmall[0:1, :nh]),
             ("a_dt_bias", a_dt_bias, m_a_dt_bias, v_a_dt_bias, dsmall[1:2, :nh]),
             ("a_out_norm_w", a_out_norm_w, m_a_out_norm_w, v_a_out_norm_w, donw),
             ("b_q_norm_w", b_q_norm_w, m_b_q_norm_w, v_b_q_norm_w, dqw),
             ("b_k_norm_w", b_k_norm_w, m_b_k_norm_w, v_b_k_norm_w, dkw),
             ("b_rel_bias", b_rel_bias, m_b_rel_bias, v_b_rel_bias, drb))
    rows = [8 * (-(-w.size // (8 * LANES))) for _, w, _, _, _ in small]
    pack = lambda arrs: jnp.concatenate([_rows_of(a, r) for a, r in zip(arrs, rows)] + [jnp.zeros((8, LANES), F32)], axis=0)
    g_pack = jnp.concatenate([_rows_of(g, r) for (_, _, _, _, g), r in zip(small, rows)]
                             + [jnp.broadcast_to(loss_row, (8, LANES))], axis=0)
    (g_all,) = _comm_call(_Comm("gather", [g_pack]), "gather_small_grads")
    outs_small = _adamw(g_all, pack([s[1] for s in small]), pack([s[2] for s in small]),
                        pack([s[3] for s in small]), "adamw_small")
    start = 0
    for (name, w, _, _, _), r in zip(small, rows):
        big[name] = [o[start:start + r].reshape(-1)[:w.size].reshape(w.shape) for o in outs_small]
        start += r
    loss = outs_small[0][start, 0]

    order = ("norm_w", "a_w_in", "a_conv_w", "a_a_log", "a_dt_bias", "a_out_norm_w", "a_w_out", "b_w_in",
             "b_q_norm_w", "b_k_norm_w", "b_rel_bias", "b_w_out")
    return (loss, dx[None]) + tuple(big[n][i] for i in range(4) for n in order)
```

```python
import functools
import math

import jax
import jax.numpy as jnp
from jax import lax
from jax.experimental import pallas as pl
from jax.experimental.pallas import tpu as pltpu

F32 = jnp.float32
BF16 = jnp.bfloat16
MESH_IDS = pl.DeviceIdType.MESH
N_DEV = 8
CHUNK = 64
HEAD_DIM = 128
EPS = 1e-6
CONV_K = 4
LEFT_CHUNKS = 8
REL_CLIP = 256
Q_TILE = LEFT_CHUNKS * CHUNK
ADAM_LR = 0.001
ADAM_B1 = 0.9
ADAM_B2 = 0.999
ADAM_EPS = 1e-08
ADAM_WD = 0.01
ADAM_STEP = 10
NEG_BIG = -1e30
VMEM_LIMIT_BYTES = 56 * 1024 * 1024
HIGHEST = lax.Precision.HIGHEST
ANY = pl.BlockSpec(memory_space=pl.ANY)


def _cparams(*sem):
    return pltpu.CompilerParams(dimension_semantics=tuple(sem), vmem_limit_bytes=VMEM_LIMIT_BYTES)


NN, NT, TN = (((1,), (0,)), ((), ())), (((1,), (1,)), ((), ())), (((0,), (0,)), ((), ()))
BNN, BNT, BTN = (((2,), (1,)), ((0,), (0,))), (((2,), (2,)), ((0,), (0,))), (((1,), (1,)), ((0,), (0,)))


def _dot(a, b, dims, precision=None):
    return lax.dot_general(a, b, dims, preferred_element_type=F32, precision=precision)


def _nn(a, b, precision=None):
    return _dot(a, b, NN, precision)


def _nt(a, b, precision=None):
    return _dot(a, b, NT, precision)


def _tn(a, b, precision=None):
    return _dot(a, b, TN, precision)


def _bf(x):
    return x.astype(BF16)


def _split(x, pieces=2):
    out = []
    for _ in range(pieces - 1):
        hi = x.astype(BF16)
        out.append(hi)
        x = x - hi.astype(F32)
    return out + [x.astype(BF16)]


def _dot3(a, b, dims):
    (ah, al), (bh, bl) = _split(a), _split(b)
    return _dot(ah, bh, dims) + (_dot(ah, bl, dims) + _dot(al, bh, dims))


def _dot_exact(a, b, dims, split_b):
    if split_b:
        a = a.astype(BF16)
        parts = [_dot(a, p, dims) for p in _split(b, 3)]
    else:
        b = b.astype(BF16)
        parts = [_dot(p, b, dims) for p in _split(a, 3)]
    return parts[0] + (parts[1] + parts[2])


def _sigmoid(x):
    return 0.5 * jnp.tanh(0.5 * x) + 0.5


def _silu(x):
    return x * _sigmoid(x)


def _dsilu(x):
    s = _sigmoid(x)
    return s * (1.0 + x * (1.0 - s))


def _my_pos():
    return lax.axis_index("x"), lax.axis_index("y"), lax.axis_index("c")


def _peers(x, y, c):
    def flip(v, f):
        return 1 - v if f else v

    return [(flip(x, kx), flip(y, ky), flip(c, kc)) for kx in (0, 1) for ky in (0, 1) for kc in (0, 1)][1:]


def _lin(p):
    return 4 * p[0] + 2 * p[1] + p[2]


class _Comm:
    def __init__(self, kind, arrays):
        self.kind, self.arrays, self.n = kind, list(arrays), len(arrays)

    def out_shape(self):
        lead = (N_DEV,) if self.kind == "gather" else ()
        return [jax.ShapeDtypeStruct(lead + a.shape, a.dtype) for a in self.arrays]

    def scratch(self):
        return [pltpu.SemaphoreType.DMA((7 * self.n,)), pltpu.SemaphoreType.DMA((7 * self.n,)),
                pltpu.SemaphoreType.DMA((self.n,))]

    def _copies(self, ins, outs, sems, arrivals):
        send_sems, recv_sems, local_sems = sems
        x, y, c = _my_pos()
        me = _lin((x, y, c))
        gather = self.kind == "gather"
        mine = [ins[t] if gather else ins[t].at[me] for t in range(self.n)]
        remote = []
        for k, peer in enumerate(_peers(x, y, c)):
            for t in range(self.n):
                if arrivals:
                    src, dst = mine[t], outs[t].at[_lin(peer)]
                else:
                    src, dst = (ins[t] if gather else ins[t].at[_lin(peer)]), outs[t].at[me]
                remote.append(pltpu.make_async_remote_copy(
                    src_ref=src, dst_ref=dst, send_sem=send_sems.at[k * self.n + t],
                    recv_sem=recv_sems.at[k * self.n + t], device_id=peer, device_id_type=MESH_IDS))
        if arrivals:
            return remote
        return [pltpu.make_async_copy(mine[t], outs[t].at[me], local_sems.at[t]) for t in range(self.n)], remote

    def start(self, ins, outs, sems):
        local, sends = self._copies(ins, outs, sems, False)
        for cp in local + sends:
            cp.start()

    def finish(self, ins, outs, sems):
        for cp in self._copies(ins, outs, sems, True):
            cp.wait_recv()
        local, sends = self._copies(ins, outs, sems, False)
        for cp in sends:
            cp.wait_send()
        for cp in local:
            cp.wait()


def _xor(a, b):
    return a + b - 2 * a * b


class _RoutedGather(_Comm):
    def __init__(self, arrays):
        super().__init__("gather", arrays)

    def _plan(self, outs, sems):
        send_sems, recv_sems, _ = sems
        x, y, c = _my_pos()
        sib, xn, yn, dg = (x, y, 1 - c), (1 - x, y, c), (x, 1 - y, c), (1 - x, 1 - y, c)
        via = (_xor(x, 1 - c), _xor(y, c), c)
        onto = (_xor(x, c), _xor(y, 1 - c), c)
        routes = [(None, sib, sib), (None, xn, xn), (None, yn, yn), (via, onto, dg),
                  (xn, sib, (1 - x, y, 1 - c)), (yn, sib, (x, 1 - y, 1 - c)), (dg, sib, (1 - x, 1 - y, 1 - c))]

        def copy(k, t, src, slot, target):
            return pltpu.make_async_remote_copy(
                src_ref=src, dst_ref=outs[t].at[slot], send_sem=send_sems.at[k * self.n + t],
                recv_sem=recv_sems.at[k * self.n + t], device_id=target, device_id_type=MESH_IDS)

        return (x, y, c), routes, copy

    def start(self, ins, outs, sems):
        me, routes, copy = self._plan(outs, sems)
        for t in range(self.n):
            pltpu.make_async_copy(ins[t], outs[t].at[_lin(me)], sems[2].at[t]).start()
            for k in range(3):
                copy(k, t, ins[t], _lin(me), routes[k][1]).start()

    def finish(self, ins, outs, sems):
        me, routes, copy = self._plan(outs, sems)

        def arrived(k):
            for t in range(self.n):
                copy(k, t, ins[t], _lin(routes[k][2]), me).wait_recv()

        def pass_on(k):
            for t in range(self.n):
                copy(k, t, outs[t].at[_lin(routes[k][0])], _lin(routes[k][0]), routes[k][1]).start()

        arrived(1)
        arrived(2)
        for k in (3, 4, 5):
            pass_on(k)
        arrived(3)
        pass_on(6)
        for k in (0, 4, 5, 6):
            arrived(k)
        for t in range(self.n):
            for k in range(7):
                src = ins[t] if k < 3 else outs[t].at[_lin(routes[k][0])]
                copy(k, t, src, _lin(me), routes[k][1]).wait_send()
            pltpu.make_async_copy(ins[t], outs[t].at[_lin(me)], sems[2].at[t]).wait()


def _comm_call(comm, name):
    n = comm.n

    def body(*refs):
        ins, outs, sems = refs[:n], refs[n:2 * n], refs[2 * n:]
        comm.start(ins, outs, sems)
        comm.finish(ins, outs, sems)

    return pl.pallas_call(
        body, name=name, out_shape=comm.out_shape(), in_specs=[ANY] * n, out_specs=[ANY] * n,
        scratch_shapes=comm.scratch(),
    )(*comm.arrays)


def _grid_call(body, *, name, grid, in_specs, out_specs, out_shape, args, scratch_shapes=(), semantics=None, comm=None):
    if comm is None:
        return pl.pallas_call(
            body, name=name, grid=grid, in_specs=in_specs, out_specs=out_specs, out_shape=out_shape,
            scratch_shapes=list(scratch_shapes), compiler_params=_cparams(*semantics),
        )(*args)
    n_in, n_out, n_sc, n = len(in_specs), len(out_specs), len(scratch_shapes), comm.n

    def full(*refs):
        ins, refs = refs[:n_in], refs[n_in:]
        cins, refs = refs[:n], refs[n:]
        outs, refs = refs[:n_out], refs[n_out:]
        couts, refs = refs[:n], refs[n:]
        scratch, sems = refs[:n_sc], refs[n_sc:]
        ids = [pl.program_id(a) for a in range(len(grid))]
        first = functools.reduce(jnp.logical_and, [i == 0 for i in ids])
        last = functools.reduce(jnp.logical_and, [i == g - 1 for i, g in zip(ids, grid)])

        @pl.when(first)
        def _():
            comm.start(cins, couts, sems)

        body(*ins, *outs, *scratch)

        @pl.when(last)
        def _():
            comm.finish(cins, couts, sems)

    return pl.pallas_call(
        full, name=name, grid=grid, in_specs=list(in_specs) + [ANY] * n, out_specs=list(out_specs) + [ANY] * n,
        out_shape=list(out_shape) + comm.out_shape(), scratch_shapes=list(scratch_shapes) + comm.scratch(),
        compiler_params=_cparams(*(["arbitrary"] * len(grid))),
    )(*(list(args) + comm.arrays))


def _mm(a, b, mode, m, n, k, *, out_dtype, name, tm=1024, tn=1024, tk=2048,
        a_m0=0, a_k0=0, b_n0=0, b_k0=0, res=None, comm=None):
    tm, tn, tk = min(tm, m), min(tn, n), min(tk, k)
    nm, nn, nk = m // tm, n // tn, k // tk
    assert nm * tm == m and nn * tn == n and nk * tk == k
    am, ak, bn, bk = a_m0 // tm, a_k0 // tk, b_n0 // tn, b_k0 // tk
    assert am * tm == a_m0 and ak * tk == a_k0 and bn * tn == b_n0 and bk * tk == b_k0
    if mode == "tn":
        a_spec = pl.BlockSpec((tk, tm), lambda i, j, q: (q + ak, i + am))
        a_dims = (0,)
    else:
        a_spec = pl.BlockSpec((tm, tk), lambda i, j, q: (i + am, q + ak))
        a_dims = (1,)
    if mode == "nt":
        b_spec = pl.BlockSpec((tn, tk), lambda i, j, q: (j + bn, q + bk))
        b_dims = (1,)
    else:
        b_spec = pl.BlockSpec((tk, tn), lambda i, j, q: (q + bk, j + bn))
        b_dims = (0,)
    o_spec = pl.BlockSpec((tm, tn), lambda i, j, q: (i, j))
    has_res = res is not None

    def body(*refs):
        a_ref, b_ref = refs[0], refs[1]
        res_ref = refs[2] if has_res else None
        o_ref = refs[2 + has_res]
        p = _dot(a_ref[...], b_ref[...], ((a_dims, b_dims), ((), ())))

        def finish(total):
            if has_res:
                total = total + res_ref[...].astype(F32)
            o_ref[...] = total.astype(out_dtype)

        if nk == 1:
            finish(p)
        else:
            acc_ref = refs[3 + has_res]
            q = pl.program_id(2)

            @pl.when(q == 0)
            def _():
                acc_ref[...] = p

            @pl.when(q > 0)
            def _():
                acc_ref[...] += p

            @pl.when(q == nk - 1)
            def _():
                finish(acc_ref[...])

    out, *carried = _grid_call(
        body, name=name, grid=(nm, nn, nk),
        in_specs=[a_spec, b_spec] + ([o_spec] if has_res else []),
        out_specs=[o_spec], out_shape=[jax.ShapeDtypeStruct((m, n), out_dtype)],
        scratch_shapes=[pltpu.VMEM((tm, tn), F32)] if nk > 1 else [],
        args=[a, b] + ([res] if has_res else []), semantics=("parallel", "parallel", "arbitrary"), comm=comm)
    return out if comm is None else (out, carried)


def _rms_fwd(x, w, name, tr=512, comm=None):
    t, d = x.shape
    tr = min(tr, t)

    def body(x_ref, w_ref, o_ref):
        xv = x_ref[...]
        r = lax.rsqrt(jnp.mean(xv * xv, axis=-1, keepdims=True) + EPS)
        o_ref[...] = (xv * r * w_ref[...]).astype(BF16)

    out, *carried = _grid_call(
        body, name=name, grid=(t // tr,),
        in_specs=[pl.BlockSpec((tr, d), lambda i: (i, 0)), pl.BlockSpec((1, d), lambda i: (0, 0))],
        out_specs=[pl.BlockSpec((tr, d), lambda i: (i, 0))],
        out_shape=[jax.ShapeDtypeStruct((t, d), BF16)], args=(x, w), semantics=("parallel",), comm=comm)
    return out if comm is None else (out, carried)


def _rms_bwd(x, w, dy, dres, name, tr=256):
    t, d = x.shape
    tr = min(tr, t)

    def body(x_ref, w_ref, dy_ref, dres_ref, dx_ref, dxb_ref, dw_ref):
        xv = x_ref[...]
        dyv = dy_ref[...].astype(F32)
        r = lax.rsqrt(jnp.mean(xv * xv, axis=-1, keepdims=True) + EPS)
        gy = dyv * w_ref[...]
        proj = jnp.sum(gy * xv, axis=-1, keepdims=True) * (1.0 / d)
        dx = dres_ref[...] + r * gy - xv * (r * r * r) * proj
        dx_ref[...] = dx
        dxb_ref[...] = dx.astype(BF16)
        part = jnp.sum(dyv * xv * r, axis=0, keepdims=True)

        @pl.when(pl.program_id(0) == 0)
        def _():
            dw_ref[...] = part

        @pl.when(pl.program_id(0) > 0)
        def _():
            dw_ref[...] += part

    row = pl.BlockSpec((tr, d), lambda i: (i, 0))
    vec = pl.BlockSpec((1, d), lambda i: (0, 0))
    return pl.pallas_call(
        body, name=name, grid=(t // tr,),
        in_specs=[row, vec, row, row], out_specs=[row, row, vec],
        out_shape=[jax.ShapeDtypeStruct((t, d), F32), jax.ShapeDtypeStruct((t, d), BF16),
                   jax.ShapeDtypeStruct((1, d), F32)],
        compiler_params=_cparams("arbitrary"),
    )(x, w, dy, dres)


def _adamw(parts, w, m, v, name, tr=128):
    r, c = w.shape
    tr = tr if r % tr == 0 else r
    c1 = 1.0 - ADAM_B1 ** ADAM_STEP
    c2 = 1.0 - ADAM_B2 ** ADAM_STEP

    def body(p_ref, w_ref, m_ref, v_ref, g_ref, d_ref, nm_ref, nv_ref):
        g = p_ref[0].astype(F32)
        for s in range(1, N_DEV):
            g = g + p_ref[s].astype(F32)
        nm = ADAM_B1 * m_ref[...] + (1.0 - ADAM_B1) * g
        nv = ADAM_B2 * v_ref[...] + (1.0 - ADAM_B2) * (g * g)
        m_hat = nm / c1
        v_hat = nv / c2
        g_ref[...] = g
        d_ref[...] = -ADAM_LR * (m_hat / (jnp.sqrt(v_hat) + ADAM_EPS) + ADAM_WD * w_ref[...])
        nm_ref[...] = nm
        nv_ref[...] = nv

    blk = pl.BlockSpec((tr, c), lambda i: (i, 0))
    return pl.pallas_call(
        body, name=name, grid=(r // tr,),
        in_specs=[pl.BlockSpec((N_DEV, tr, c), lambda i: (0, i, 0)), blk, blk, blk],
        out_specs=[blk] * 4, out_shape=[jax.ShapeDtypeStruct((r, c), F32)] * 4,
        compiler_params=_cparams("parallel"),
    )(parts, w, m, v)


ROW_TILE, ROW_HEADS = 256, 16
CONV_ROWS, CONV_HEADS = 512, 8


def _window(dest, inner, n_in, out_index):
    if dest is None:
        return inner, 0, [], [], {}
    buf, col0, total = dest
    if buf is None:
        return total, col0, [], [], {}
    return total, col0, [buf], [ANY], {n_in: out_index}


def _skip_ref(body, at, count):
    return body if count == 0 else (lambda *refs: body(*refs[:at], *refs[at + count:]))


def _heads_of(x, nh):
    return [x[:, h * HEAD_DIM:(h + 1) * HEAD_DIM] for h in range(nh)]


def _headnorm_fwd(proj, w, col0, inner, name, tr=ROW_TILE, hb=ROW_HEADS):
    t = proj.shape[0]
    tr = min(tr, t)
    hb = min(hb, inner // HEAD_DIM)
    wc = hb * HEAD_DIM
    c0 = col0 // wc

    def body(x_ref, w_ref, o_ref):
        outs = []
        for xh in _heads_of(x_ref[...], hb):
            r = lax.rsqrt(jnp.mean(xh * xh, axis=-1, keepdims=True) + EPS)
            outs.append((xh * r * w_ref[...]).astype(BF16))
        o_ref[...] = jnp.concatenate(outs, axis=1)

    return pl.pallas_call(
        body, name=name, grid=(t // tr, inner // wc),
        in_specs=[pl.BlockSpec((tr, wc), lambda i, j: (i, j + c0)), pl.BlockSpec((1, HEAD_DIM), lambda i, j: (0, 0))],
        out_specs=pl.BlockSpec((tr, wc), lambda i, j: (i, j)),
        out_shape=jax.ShapeDtypeStruct((t, inner), BF16),
        compiler_params=_cparams("parallel", "parallel"),
    )(proj, w)


def _headnorm_bwd(dy, proj, w, col0, inner, name, tr=ROW_TILE, hb=ROW_HEADS, dest=None):
    t = proj.shape[0]
    tr = min(tr, t)
    hb = min(hb, inner // HEAD_DIM)
    wc = hb * HEAD_DIM
    c0 = col0 // wc
    width, out0, more, more_specs, aliases = _window(dest, inner, 3, 0)

    def body(dy_ref, x_ref, w_ref, dx_ref, dw_ref):
        outs = []
        part = jnp.zeros((1, HEAD_DIM), F32)
        for dyh, xh in zip(_heads_of(dy_ref[...], hb), _heads_of(x_ref[...], hb)):
            r = lax.rsqrt(jnp.mean(xh * xh, axis=-1, keepdims=True) + EPS)
            gy = dyh * w_ref[...]
            pr = jnp.sum(gy * xh, axis=-1, keepdims=True) * (1.0 / HEAD_DIM)
            outs.append((r * gy - xh * (r * r * r) * pr).astype(BF16))
            part = part + jnp.sum(dyh * xh * r, axis=0, keepdims=True)
        dx_ref[...] = jnp.concatenate(outs, axis=1)
        first = (pl.program_id(0) == 0) & (pl.program_id(1) == 0)

        @pl.when(first)
        def _():
            dw_ref[...] = part

        @pl.when(jnp.logical_not(first))
        def _():
            dw_ref[...] += part

    blk = pl.BlockSpec((tr, wc), lambda i, j: (i, j))
    return pl.pallas_call(
        _skip_ref(body, 3, len(more)), name=name, grid=(t // tr, inner // wc),
        in_specs=[blk, pl.BlockSpec((tr, wc), lambda i, j: (i, j + c0)),
                  pl.BlockSpec((1, HEAD_DIM), lambda i, j: (0, 0))] + more_specs,
        out_specs=[pl.BlockSpec((tr, wc), lambda i, j: (i, j + out0 // wc)),
                   pl.BlockSpec((1, HEAD_DIM), lambda i, j: (0, 0))],
        out_shape=[jax.ShapeDtypeStruct((t, width), BF16), jax.ShapeDtypeStruct((1, HEAD_DIM), F32)],
        input_output_aliases=aliases, compiler_params=_cparams("arbitrary", "arbitrary"),
    )(dy, proj, w, *more)


def _gate_fwd(o, proj, zcol0, inner, name, norm_w=None, tr=ROW_TILE, hb=ROW_HEADS):
    t = o.shape[0]
    tr = min(tr, t)
    hb = min(hb, inner // HEAD_DIM)
    wc = hb * HEAD_DIM
    c0 = zcol0 // wc
    has_w = norm_w is not None

    def body(*refs):
        o_ref, z_ref = refs[0], refs[1]
        out_ref = refs[2 + has_w]
        outs = []
        for oh, zh in zip(_heads_of(o_ref[...], hb), _heads_of(z_ref[...], hb)):
            if has_w:
                r = lax.rsqrt(jnp.mean(oh * oh, axis=-1, keepdims=True) + EPS)
                oh = oh * r * refs[2][...]
            outs.append((oh * _silu(zh)).astype(BF16))
        out_ref[...] = jnp.concatenate(outs, axis=1)

    blk = pl.BlockSpec((tr, wc), lambda i, j: (i, j))
    vec = pl.BlockSpec((1, HEAD_DIM), lambda i, j: (0, 0))
    return pl.pallas_call(
        body, name=name, grid=(t // tr, inner // wc),
        in_specs=[blk, pl.BlockSpec((tr, wc), lambda i, j: (i, j + c0))] + ([vec] if has_w else []),
        out_specs=blk, out_shape=jax.ShapeDtypeStruct((t, inner), BF16),
        compiler_params=_cparams("parallel", "parallel"),
    )(*([o, proj] + ([norm_w] if has_w else [])))


def _gate_bwd(dg, o, proj, zcol0, inner, name, do_dtype, norm_w=None, tr=ROW_TILE, hb=ROW_HEADS, dest=None):
    t = o.shape[0]
    tr = min(tr, t)
    hb = min(hb, inner // HEAD_DIM)
    wc = hb * HEAD_DIM
    c0 = zcol0 // wc
    has_w = norm_w is not None
    width, out0, more, more_specs, aliases = _window(dest, inner, 3 + has_w, 1)

    def body(*refs):
        dg_ref, o_ref, z_ref = refs[0], refs[1], refs[2]
        do_ref, dz_ref = refs[3 + has_w], refs[4 + has_w]
        dos, dzs = [], []
        part = jnp.zeros((1, HEAD_DIM), F32)
        for dgh, oh, zh in zip(_heads_of(dg_ref[...], hb), _heads_of(o_ref[...], hb), _heads_of(z_ref[...], hb)):
            dy = dgh * _silu(zh)
            if has_w:
                w = refs[3][...]
                r = lax.rsqrt(jnp.mean(oh * oh, axis=-1, keepdims=True) + EPS)
                on = oh * r
                dzs.append((dgh * on * w * _dsilu(zh)).astype(BF16))
                gy = dy * w
                pr = jnp.sum(gy * oh, axis=-1, keepdims=True) * (1.0 / HEAD_DIM)
                dos.append((r * gy - oh * (r * r * r) * pr).astype(do_dtype))
                part = part + jnp.sum(dy * on, axis=0, keepdims=True)
            else:
                dzs.append((dgh * oh * _dsilu(zh)).astype(BF16))
                dos.append(dy.astype(do_dtype))
        do_ref[...] = jnp.concatenate(dos, axis=1)
        dz_ref[...] = jnp.concatenate(dzs, axis=1)
        if has_w:
            dw_ref = refs[6]
            first = (pl.program_id(0) == 0) & (pl.program_id(1) == 0)

            @pl.when(first)
            def _():
                dw_ref[...] = part

            @pl.when(jnp.logical_not(first))
            def _():
                dw_ref[...] += part

    blk = pl.BlockSpec((tr, wc), lambda i, j: (i, j))
    vec = pl.BlockSpec((1, HEAD_DIM), lambda i, j: (0, 0))
    return pl.pallas_call(
        _skip_ref(body, 3 + has_w, len(more)), name=name, grid=(t // tr, inner // wc),
        in_specs=[blk, blk, pl.BlockSpec((tr, wc), lambda i, j: (i, j + c0))] + ([vec] if has_w else []) + more_specs,
        out_specs=[blk, pl.BlockSpec((tr, wc), lambda i, j: (i, j + out0 // wc))] + ([vec] if has_w else []),
        out_shape=[jax.ShapeDtypeStruct((t, inner), do_dtype), jax.ShapeDtypeStruct((t, width), BF16)]
        + ([jax.ShapeDtypeStruct((1, HEAD_DIM), F32)] if has_w else []),
        input_output_aliases=aliases, compiler_params=_cparams("arbitrary", "arbitrary"),
    )(*([dg, o, proj] + ([norm_w] if has_w else []) + more))


def _loss_grad(h, target, name, tr=512):
    t, d = h.shape
    tr = min(tr, t)

    def body(h_ref, t_ref, g_ref, gb_ref, l_ref):
        e = h_ref[...] - t_ref[...]
        g = e * (1.0 / d)
        g_ref[...] = g
        gb_ref[...] = g.astype(BF16)
        part = jnp.zeros((1, HEAD_DIM), F32) + 0.5 * jnp.sum(jnp.sum(e * e, axis=-1, keepdims=True) * (1.0 / d))

        @pl.when(pl.program_id(0) == 0)
        def _():
            l_ref[...] = part

        @pl.when(pl.program_id(0) > 0)
        def _():
            l_ref[...] += part

    row = pl.BlockSpec((tr, d), lambda i: (i, 0))
    return pl.pallas_call(
        body, name=name, grid=(t // tr,),
        in_specs=[row, row], out_specs=[row, row, pl.BlockSpec((1, HEAD_DIM), lambda i: (0, 0))],
        out_shape=[jax.ShapeDtypeStruct((t, d), F32), jax.ShapeDtypeStruct((t, d), BF16),
                   jax.ShapeDtypeStruct((1, HEAD_DIM), F32)],
        compiler_params=_cparams("arbitrary"),
    )(h, target)


N_REL = 2 * REL_CLIP + 1
REL_PAD = 640
WIN = 2 * Q_TILE


def _diag_onehot():
    i = lax.broadcasted_iota(jnp.int32, (REL_PAD, WIN), 0)
    j = lax.broadcasted_iota(jnp.int32, (REL_PAD, WIN), 1)
    rel = jnp.where(j < Q_TILE + CHUNK, Q_TILE - j, Q_TILE + WIN - j)
    used = (j < Q_TILE + CHUNK) | (j > WIN - CHUNK)
    idx = jnp.clip(rel, -REL_CLIP, REL_CLIP) + REL_CLIP
    return jnp.where(used & (i == idx), 1.0, 0.0).astype(F32)


def _band_mask():
    r = lax.broadcasted_iota(jnp.int32, (Q_TILE, WIN), 0) // CHUNK
    kc = lax.broadcasted_iota(jnp.int32, (Q_TILE, WIN), 1) // CHUNK - LEFT_CHUNKS
    return (kc <= r) & (kc >= r - LEFT_CHUNKS)


def _bias_tiles(rel_bias_pad, name):
    nh = rel_bias_pad.shape[0]

    def body(rb_ref, o_ref):
        dvec = _nn(rb_ref[...], _diag_onehot(), HIGHEST)[0:1, :]
        tile = pltpu.roll(jnp.broadcast_to(dvec, (Q_TILE, WIN)), 0, 1, stride=1, stride_axis=0)
        o_ref[...] = jnp.where(_band_mask(), tile, NEG_BIG)

    return pl.pallas_call(
        body, name=name, grid=(nh,),
        in_specs=[pl.BlockSpec((None, 8, REL_PAD), lambda h: (h, 0, 0))],
        out_specs=pl.BlockSpec((None, Q_TILE, WIN), lambda h: (h, 0, 0)),
        out_shape=jax.ShapeDtypeStruct((nh, Q_TILE, WIN), F32),
        compiler_params=_cparams("parallel"),
    )(rel_bias_pad)


def _bias_grad(dtile, name):
    nh = dtile.shape[0]

    def body(d_ref, o_ref):
        ri = lax.broadcasted_iota(jnp.int32, (Q_TILE, Q_TILE), 0)
        ci = lax.broadcasted_iota(jnp.int32, (Q_TILE, Q_TILE), 1)
        flip = jnp.where(ri + ci == Q_TILE - 1, 1.0, 0.0).astype(F32)
        rev = _dot_exact(flip, d_ref[...], NN, True)
        rolled = pltpu.roll(rev, WIN - (Q_TILE - 1), 1, stride=1, stride_axis=0)
        diag = jnp.broadcast_to(jnp.sum(rolled, axis=0, keepdims=True), (8, WIN))
        o_ref[...] = _nt(diag, _diag_onehot(), HIGHEST)

    return pl.pallas_call(
        body, name=name, grid=(nh,),
        in_specs=[pl.BlockSpec((None, Q_TILE, WIN), lambda h: (h, 0, 0))],
        out_specs=pl.BlockSpec((None, 8, REL_PAD), lambda h: (h, 0, 0)),
        out_shape=jax.ShapeDtypeStruct((nh, 8, REL_PAD), F32),
        compiler_params=_cparams("parallel"),
    )(dtile)


GROUP = 2 * CHUNK
BAND = Q_TILE + GROUP


def _band_rows(r0_ref, r1_ref, g):
    return jnp.concatenate([r0_ref[GROUP * g:, :], r1_ref[:GROUP * (g + 1), :]], axis=0)


N_GROUPS = Q_TILE // GROUP


def _groups(ref):
    return jnp.stack([ref[GROUP * g:GROUP * (g + 1), :] for g in range(N_GROUPS)])


def _bands(r0_ref, r1_ref):
    return jnp.stack([_band_rows(r0_ref, r1_ref, g) for g in range(N_GROUPS)])


def _group_probs(q, kw, b_ref, first_tile):
    bias = jnp.stack([b_ref[GROUP * g:GROUP * (g + 1), GROUP * g:GROUP * g + BAND] for g in range(N_GROUPS)])
    s = _dot(q, kw, BNT) * (HEAD_DIM ** -0.5) + bias
    col = (lax.broadcasted_iota(jnp.int32, (N_GROUPS, GROUP, BAND), 2)
           + GROUP * lax.broadcasted_iota(jnp.int32, (N_GROUPS, GROUP, BAND), 0))
    s = jnp.where(first_tile & (col < Q_TILE), NEG_BIG, s)
    p = jnp.exp(s - jnp.max(s, axis=-1, keepdims=True))
    return p * (1.0 / jnp.sum(p, axis=-1, keepdims=True))


def _attn_fwd(q, k, v, bias, name):
    t, inner = q.shape
    nh, nt = inner // HEAD_DIM, t // Q_TILE

    def body(q_ref, k0_ref, k1_ref, v0_ref, v1_ref, b_ref, o_ref):
        p = _group_probs(_groups(q_ref), _bands(k0_ref, k1_ref), b_ref, pl.program_id(1) == 0)
        o = _dot(_bf(p), _bands(v0_ref, v1_ref), BNN)
        for g in range(N_GROUPS):
            o_ref[GROUP * g:GROUP * (g + 1), :] = o[g]

    cur = pl.BlockSpec((Q_TILE, HEAD_DIM), lambda h, i: (i, h))
    prev = pl.BlockSpec((Q_TILE, HEAD_DIM), lambda h, i: (jnp.maximum(i - 1, 0), h))
    return pl.pallas_call(
        body, name=name, grid=(nh, nt),
        in_specs=[cur, prev, cur, prev, cur, pl.BlockSpec((None, Q_TILE, WIN), lambda h, i: (h, 0, 0))],
        out_specs=cur, out_shape=jax.ShapeDtypeStruct((t, inner), F32),
        compiler_params=_cparams("parallel", "parallel"),
    )(q, k, k, v, v, bias)


def _attn_bwd(q, k, v, do, bias, name, dest=None):
    t, inner = q.shape
    nh, nt = inner // HEAD_DIM, t // Q_TILE
    scale = HEAD_DIM ** -0.5

    def body(q_ref, k0_ref, k1_ref, v0_ref, v1_ref, do_ref, b_ref, dq_ref, dk_ref, dv_ref, db_ref,
             ck_ref, cv_ref, wk_ref, wv_ref):
        i = pl.program_id(1)

        @pl.when(i == 0)
        def _():
            ck_ref[...] = jnp.zeros(blk, F32)
            cv_ref[...] = jnp.zeros(blk, F32)
            db_ref[...] = jnp.zeros((Q_TILE, WIN), F32)

        @pl.when(i < nt)
        def _():
            wk_ref[...] = jnp.zeros((WIN, HEAD_DIM), F32)
            wv_ref[...] = jnp.zeros((WIN, HEAD_DIM), F32)
            qv, dov = _groups(q_ref), _groups(do_ref)
            kw, vw = _bands(k0_ref, k1_ref), _bands(v0_ref, v1_ref)
            p = _group_probs(qv, kw, b_ref, i == 0)
            dp = _dot(dov, vw, BNT)
            ds = p * (dp - jnp.sum(p * dp, axis=-1, keepdims=True))
            pb, dsb = _bf(p), _bf(ds)
            dq = _dot(dsb, kw, BNN) * scale
            dkw = _dot(dsb, qv, BTN) * scale
            dvw = _dot(pb, dov, BTN)
            for g in range(N_GROUPS):
                rows, cols = slice(GROUP * g, GROUP * (g + 1)), slice(GROUP * g, GROUP * g + BAND)
                db_ref[rows, cols] += ds[g]
                dq_ref[rows, :] = dq[g]
                wk_ref[cols, :] += dkw[g]
                wv_ref[cols, :] += dvw[g]
            dk_ref[...] = ck_ref[...] + wk_ref[:Q_TILE, :]
            dv_ref[...] = (cv_ref[...] + wv_ref[:Q_TILE, :]).astype(BF16)
            ck_ref[...] = wk_ref[Q_TILE:, :]
            cv_ref[...] = wv_ref[Q_TILE:, :]

        @pl.when(i == nt)
        def _():
            dk_ref[...] = ck_ref[...]
            dv_ref[...] = cv_ref[...].astype(BF16)

    blk = (Q_TILE, HEAD_DIM)
    cur = pl.BlockSpec(blk, lambda h, i: (jnp.minimum(i, nt - 1), h))
    prev = pl.BlockSpec(blk, lambda h, i: (jnp.clip(i - 1, 0, nt - 1), h))
    lag = pl.BlockSpec(blk, lambda h, i: (jnp.maximum(i - 1, 0), h))
    tile = pl.BlockSpec((None, Q_TILE, WIN), lambda h, i: (h, 0, 0))
    width, out0, more, more_specs, aliases = _window(dest, inner, 7, 2)
    return pl.pallas_call(
        _skip_ref(body, 7, len(more)), name=name, grid=(nh, nt + 1),
        in_specs=[cur, prev, cur, prev, cur, cur, tile] + more_specs,
        out_specs=[cur, lag, pl.BlockSpec(blk, lambda h, i: (jnp.maximum(i - 1, 0), h + out0 // HEAD_DIM)), tile],
        out_shape=[jax.ShapeDtypeStruct((t, inner), F32)] * 2 + [jax.ShapeDtypeStruct((t, width), BF16),
                                                                 jax.ShapeDtypeStruct((nh, Q_TILE, WIN), F32)],
        scratch_shapes=[pltpu.VMEM(blk, F32), pltpu.VMEM(blk, F32),
                        pltpu.VMEM((WIN, HEAD_DIM), F32), pltpu.VMEM((WIN, HEAD_DIM), F32)],
        input_output_aliases=aliases, compiler_params=_cparams("arbitrary", "arbitrary"),
    )(q, k, k, v, v, do, bias, *more)


def _pad_rel_bias(rel_bias):
    nh = rel_bias.shape[0]
    return jnp.broadcast_to(jnp.pad(rel_bias, ((0, 0), (0, REL_PAD - N_REL)))[:, None, :], (nh, 8, REL_PAD))


def _layer_b_fwd(h1, nw, w_in_t, qw, kw, bias, w_out):
    t, d = h1.shape
    inner = w_out.shape[0]
    hn = _rms_fwd(h1, nw, "b_rms")
    proj = _mm(hn, w_in_t, "nt", t, 4 * inner, d, out_dtype=F32, name="b_proj")
    qn = _headnorm_fwd(proj, qw, 0, inner, "b_qnorm")
    kn = _headnorm_fwd(proj, kw, inner, inner, "b_knorm")
    vb = proj[:, 2 * inner:3 * inner].astype(BF16)
    o = _attn_fwd(qn, kn, vb, bias, "b_attn")
    g = _gate_fwd(o, proj, 3 * inner, inner, "b_gate")
    h2 = _mm(g, w_out, "nn", t, d, inner, out_dtype=F32, name="b_out", res=h1)
    return h2, (hn, proj, qn, kn, vb, o, g)


def _layer_b_bwd(dh2, dh2b, h1, nw, w_in_t, qw, kw, bias, w_out, saved):
    hn, proj, qn, kn, vb, o, g = saved
    t, d = h1.shape
    inner = w_out.shape[0]
    dg = _mm(dh2b, w_out, "nt", t, inner, d, out_dtype=F32, name="b_dgate")
    dw_out = _mm(g, dh2b, "tn", inner, d, t, out_dtype=F32, name="b_dwout")
    do, dproj = _gate_bwd(dg, o, proj, 3 * inner, inner, "b_gate_bwd", BF16, dest=(None, 3 * inner, 4 * inner))
    dq, dk, dproj, dtile = _attn_bwd(qn, kn, vb, do, bias, "b_attn_bwd", dest=(dproj, 2 * inner, 4 * inner))
    dproj, dqw = _headnorm_bwd(dq, proj, qw, 0, inner, "b_qnorm_bwd", dest=(dproj, 0, 4 * inner))
    dproj, dkw = _headnorm_bwd(dk, proj, kw, inner, inner, "b_knorm_bwd", dest=(dproj, inner, 4 * inner))
    dhn = _mm(dproj, w_in_t, "nn", t, d, 4 * inner, out_dtype=F32, name="b_dhn")
    dw_in_t = _mm(dproj, hn, "tn", 4 * inner, d, t, out_dtype=F32, name="b_dwin")
    dh1, dh1b, dnw = _rms_bwd(h1, nw, dhn, dh2, "b_rms_bwd")
    drb = _bias_grad(dtile, "b_bias_grad")[:, 0, :N_REL]
    return dh1, dh1b, dnw, dw_in_t, dqw, dkw, drb, dw_out


LANES = 128


def _softplus(x):
    return jnp.maximum(x, 0.0) + jnp.log1p(jnp.exp(-jnp.abs(x)))


def _gates_fwd(ab, alog_row, dt_row, nh, name, tr=1024):
    t = ab.shape[0]
    tr = min(tr, t)

    def body(x_ref, al_ref, dt_ref, o_ref):
        x = x_ref[...]
        lane = lax.broadcasted_iota(jnp.int32, x.shape, 1)
        g = -jnp.exp(al_ref[...]) * _softplus(x + dt_ref[...])
        o_ref[...] = jnp.where(lane < nh, g, jnp.where(lane < 2 * nh, _sigmoid(x), 0.0))

    row = pl.BlockSpec((tr, LANES), lambda i: (i, 0))
    vec = pl.BlockSpec((1, LANES), lambda i: (0, 0))
    return pl.pallas_call(
        body, name=name, grid=(t // tr,), in_specs=[row, vec, vec], out_specs=row,
        out_shape=jax.ShapeDtypeStruct((t, LANES), F32), compiler_params=_cparams("parallel"),
    )(ab, alog_row, dt_row)


def _gates_bwd(ab, alog_row, dt_row, dgates, nh, name, tr=1024):
    t = ab.shape[0]
    tr = min(tr, t)
    npart = dgates.shape[0]

    def body(x_ref, al_ref, dt_ref, dg_ref, dx_ref, s_ref):
        x = x_ref[...]
        lane = lax.broadcasted_iota(jnp.int32, x.shape, 1)
        dgt = dg_ref[0]
        for p in range(1, npart):
            dgt = dgt + dg_ref[p]
        ea = jnp.exp(al_ref[...])
        xa = x + dt_ref[...]
        da = jnp.where(lane < nh, dgt * (-ea) * _sigmoid(xa), 0.0)
        beta = _sigmoid(x)
        db = jnp.where((lane >= nh) & (lane < 2 * nh), dgt * beta * (1.0 - beta), 0.0)
        dx_ref[...] = (da + db).astype(BF16)
        dal = jnp.sum(jnp.where(lane < nh, dgt * (-ea) * _softplus(xa), 0.0), axis=0, keepdims=True)
        ddt = jnp.sum(da, axis=0, keepdims=True)
        r8 = lax.broadcasted_iota(jnp.int32, (8, LANES), 0)
        part = jnp.where(r8 == 0, dal, jnp.where(r8 == 1, ddt, 0.0))

        @pl.when(pl.program_id(0) == 0)
        def _():
            s_ref[...] = part

        @pl.when(pl.program_id(0) > 0)
        def _():
            s_ref[...] += part

    row = pl.BlockSpec((tr, LANES), lambda i: (i, 0))
    vec = pl.BlockSpec((1, LANES), lambda i: (0, 0))
    return pl.pallas_call(
        body, name=name, grid=(t // tr,),
        in_specs=[row, vec, vec, pl.BlockSpec((npart, tr, LANES), lambda i: (0, i, 0))],
        out_specs=[row, pl.BlockSpec((8, LANES), lambda i: (0, 0))],
        out_shape=[jax.ShapeDtypeStruct((t, LANES), BF16), jax.ShapeDtypeStruct((8, LANES), F32)],
        compiler_params=_cparams("arbitrary"),
    )(ab, alog_row, dt_row, dgates)


HALO = 8


def _conv_taps(ext, w, rows):
    acc = ext[HALO:HALO + rows] * w[CONV_K - 1:CONV_K]
    for s in range(1, CONV_K):
        acc = acc + pltpu.roll(ext, s, 0)[HALO:HALO + rows] * w[CONV_K - 1 - s:CONV_K - s]
    return acc


def _conv_fwd(proj, conv_w, col0, inner, mode, name, tt=CONV_ROWS, hb=CONV_HEADS):
    t = proj.shape[0]
    tt = min(tt, t)
    hb = min(hb, inner // HEAD_DIM)
    wc = hb * HEAD_DIM
    c0 = col0 // wc
    hpb = tt // HALO

    def body(x_ref, halo_ref, w_ref, o_ref):
        halo = jnp.where(pl.program_id(1) == 0, 0.0, halo_ref[...])
        s = _silu(_conv_taps(jnp.concatenate([halo, x_ref[...]], axis=0), w_ref[...], tt))
        if mode == "v":
            o_ref[...] = s
        else:
            mul = HEAD_DIM ** -0.5 if mode == "q" else 1.0
            o_ref[...] = jnp.concatenate(
                [sh * (lax.rsqrt(jnp.sum(sh * sh, axis=-1, keepdims=True) + EPS) * mul) for sh in _heads_of(s, hb)], axis=1)

    return pl.pallas_call(
        body, name=name, grid=(inner // wc, t // tt),
        in_specs=[pl.BlockSpec((tt, wc), lambda j, i: (i, j + c0)),
                  pl.BlockSpec((HALO, wc), lambda j, i: (jnp.maximum(i * hpb - 1, 0), j + c0)),
                  pl.BlockSpec((CONV_K, wc), lambda j, i: (0, j + c0))],
        out_specs=pl.BlockSpec((tt, wc), lambda j, i: (i, j)),
        out_shape=jax.ShapeDtypeStruct((t, inner), F32),
        compiler_params=_cparams("parallel", "parallel"),
    )(proj, proj, conv_w)


def _conv_bwd(dy, proj, conv_w, col0, inner, mode, name, tt=CONV_ROWS, hb=CONV_HEADS, dest=None):
    t = proj.shape[0]
    tt = min(tt, t)
    nt = t // tt
    hb = min(hb, inner // HEAD_DIM)
    wc = hb * HEAD_DIM
    c0 = col0 // wc
    hpb = tt // HALO
    rows = tt + HALO

    def body(dy_ref, dyn_ref, x_ref, xp_ref, xn_ref, w_ref, dx_ref, dw_ref):
        i = pl.program_id(1)
        w = w_ref[...]
        xprev = jnp.where(i == 0, 0.0, xp_ref[...])
        ext = jnp.concatenate([xprev, x_ref[...], xn_ref[...]], axis=0)
        c = _conv_taps(ext, w, rows)
        dyv = jnp.concatenate([dy_ref[...], jnp.where(i == nt - 1, 0.0, dyn_ref[...])], axis=0)
        sg = _sigmoid(c)
        s = c * sg
        if mode == "v":
            ds = dyv
        else:
            mul = HEAD_DIM ** -0.5 if mode == "q" else 1.0
            parts = []
            for dyh, sh in zip(_heads_of(dyv, hb), _heads_of(s, hb)):
                r = lax.rsqrt(jnp.sum(sh * sh, axis=-1, keepdims=True) + EPS)
                parts.append(mul * (r * dyh - sh * (r * r * r) * jnp.sum(dyh * sh, axis=-1, keepdims=True)))
            ds = jnp.concatenate(parts, axis=1)
        dc = ds * (sg * (1.0 + c * (1.0 - sg)))
        dx = dc[:tt] * w[CONV_K - 1:CONV_K]
        for sft in range(1, CONV_K):
            dx = dx + pltpu.roll(dc, rows - sft, 0)[:tt] * w[CONV_K - 1 - sft:CONV_K - sft]
        dx_ref[...] = dx.astype(BF16)
        r8 = lax.broadcasted_iota(jnp.int32, (8, wc), 0)
        part = jnp.zeros((8, wc), F32)
        for sft in range(CONV_K):
            xs = ext[HALO:HALO + tt] if sft == 0 else pltpu.roll(ext, sft, 0)[HALO:HALO + tt]
            part = part + jnp.where(r8 == CONV_K - 1 - sft, jnp.sum(dc[:tt] * xs, axis=0, keepdims=True), 0.0)

        @pl.when(i == 0)
        def _():
            dw_ref[...] = part

        @pl.when(i > 0)
        def _():
            dw_ref[...] += part

    cur = lambda off: pl.BlockSpec((tt, wc), lambda j, i: (i, j + off))
    nxt = lambda off: pl.BlockSpec((HALO, wc), lambda j, i: (jnp.minimum((i + 1) * hpb, t // HALO - 1), j + off))
    width, out0, more, more_specs, aliases = _window(dest, inner, 6, 0)
    return pl.pallas_call(
        _skip_ref(body, 6, len(more)), name=name, grid=(inner // wc, nt),
        in_specs=[cur(0), nxt(0), cur(c0),
                  pl.BlockSpec((HALO, wc), lambda j, i: (jnp.maximum(i * hpb - 1, 0), j + c0)), nxt(c0),
                  pl.BlockSpec((CONV_K, wc), lambda j, i: (0, j + c0))] + more_specs,
        out_specs=[pl.BlockSpec((tt, wc), lambda j, i: (i, j + out0 // wc)),
                   pl.BlockSpec((8, wc), lambda j, i: (0, j))],
        out_shape=[jax.ShapeDtypeStruct((t, width), BF16), jax.ShapeDtypeStruct((8, inner), F32)],
        input_output_aliases=aliases, compiler_params=_cparams("parallel", "arbitrary"),
    )(dy, dy, proj, proj, proj, conv_w, *more)


GDN_HB = 4
GDN_NB = 8
SCAN_HB = 16
SCAN_NB = 2


def _iota2(n, m):
    return lax.broadcasted_iota(jnp.int32, (n, m), 0), lax.broadcasted_iota(jnp.int32, (n, m), 1)


def _head_select(first_head, hb, lane0):
    r, lane = _iota2(8, LANES)
    return jnp.where((r < hb) & (lane == lane0 + first_head + r), 1.0, 0.0).astype(F32)


def _chunk_gates(gt, selg, selb):
    i, j = _iota2(CHUNK, CHUNK)
    gc_all = _dot_exact(jnp.where(j <= i, 1.0, 0.0), gt, NN, True)
    return (_dot_exact(gc_all, selg, NT, False), _dot_exact(selg, gc_all, NT, True),
            _dot_exact(gt, selb, NT, False))


def _decay_terms(gcol, grow):
    i, j = _iota2(CHUNK, CHUNK)
    glast = gcol[:, CHUNK - 1:CHUNK, :]
    decay = jnp.exp(jnp.where(j <= i, gcol - grow, NEG_BIG))
    return jnp.exp(gcol), jnp.exp(glast - gcol), jnp.exp(glast), decay


def _unit_lower_inverse(a):
    i, j = _iota2(CHUNK, CHUNK)
    same16 = (i // 16) == (j // 16)
    same32 = (i // 32) == (j // 32)
    m = jnp.where(same16, -a, 0.0)
    x = jnp.where(i == j, 1.0, 0.0) + m
    for _ in range(3):
        m = _dot3(m, m, BNN)
        x = x + _dot3(x, m, BNN)
    for off in (jnp.where(same32 & jnp.logical_not(same16), a, 0.0), jnp.where(same32, 0.0, a)):
        x = x - _dot3(_dot3(x, off, BNN), x, BNN)
    return x


def _unit_inputs(refs, g_ref, selg, selb, hb, nb):
    units = [(c, h) for c in range(nb) for h in range(hb)]
    rs = lambda c: slice(c * CHUNK, (c + 1) * CHUNK)
    cs = lambda h: slice(h * HEAD_DIM, (h + 1) * HEAD_DIM)
    gates = [_chunk_gates(g_ref[rs(c), :], selg, selb) for c in range(nb)]
    stacked = [jnp.stack([r[rs(c), cs(h)] for c, h in units]) for r in refs]
    gcol = jnp.stack([gates[c][0][:, h:h + 1] for c, h in units])
    grow = jnp.stack([gates[c][1][h:h + 1, :] for c, h in units])
    bcol = jnp.stack([gates[c][2][:, h:h + 1] for c, h in units])
    return units, rs, cs, stacked, gcol, grow, bcol


def _gdn_specs(nh, inner, t, heads=GDN_HB, chunks=GDN_NB):
    hb, nb = min(heads, nh), chunks
    rows = nb * CHUNK
    wide = pl.BlockSpec((rows, hb * HEAD_DIM), lambda g, n: (n, g))
    sq = pl.BlockSpec((hb, rows, CHUNK), lambda g, n: (g, n, 0))
    gts = pl.BlockSpec((rows, LANES), lambda g, n: (n, 0))
    glb = pl.BlockSpec((nb * 8, hb * HEAD_DIM), lambda g, n: (n, g))
    return hb, nb, rows, wide, sq, gts, glb


def _gdn_intra_fwd(q, k, v, gates, nh, name, comm=None):
    t, inner = q.shape
    hb, nb, rows, wide, sq, gts, glb = _gdn_specs(nh, inner, t)

    def body(q_ref, k_ref, v_ref, g_ref, qe_ref, kel_ref, wb_ref, w_ref, u_ref, qk_ref, tm_ref, gl_ref):
        first = pl.program_id(0) * hb
        selg, selb = _head_select(first, hb, 0), _head_select(first, hb, nh)
        i, j = _iota2(CHUNK, CHUNK)
        units, rs, cs, (qv, kv, vv), gcol, grow, bcol = _unit_inputs(
            (q_ref, k_ref, v_ref), g_ref, selg, selb, hb, nb)
        e, el, gl, decay = _decay_terms(gcol, grow)
        kb = kv * bcol
        qbf, kbf = _bf(qv), _bf(kv)
        a = jnp.where(j < i, _dot(_bf(kb), kbf, BNT) * decay, 0.0)
        tm = _unit_lower_inverse(a)
        uw = _dot3(tm, jnp.concatenate([vv * bcol, kb * e], axis=2), BNN)
        qk = _bf(_dot(qbf, kbf, BNT) * decay)
        qe, kel = _bf(qv * e), _bf(kv * el)
        for n, (c, h) in enumerate(units):
            w = uw[n, :, HEAD_DIM:]
            qe_ref[rs(c), cs(h)] = qe[n]
            kel_ref[rs(c), cs(h)] = kel[n]
            wb_ref[rs(c), cs(h)] = _bf(w)
            w_ref[rs(c), cs(h)] = w
            u_ref[rs(c), cs(h)] = uw[n, :, :HEAD_DIM]
            qk_ref[h, rs(c), :] = qk[n]
            tm_ref[h, rs(c), :] = tm[n]
            gl_ref[c * 8:(c + 1) * 8, cs(h)] = jnp.broadcast_to(gl[n], (8, HEAD_DIM))

    big = lambda dt: jax.ShapeDtypeStruct((t, inner), dt)
    return _grid_call(
        body, name=name, grid=(nh // hb, t // rows),
        in_specs=[wide, wide, wide, gts],
        out_specs=[wide] * 5 + [sq, sq, glb],
        out_shape=[big(BF16), big(BF16), big(BF16), big(F32), big(F32),
                   jax.ShapeDtypeStruct((nh, t, CHUNK), BF16), jax.ShapeDtypeStruct((nh, t, CHUNK), F32),
                   jax.ShapeDtypeStruct((t // CHUNK * 8, inner), F32)],
        args=(q, k, v, gates), semantics=("parallel", "parallel"), comm=comm)


def _gdn_scan_fwd(qe, kel, wb, u, qk, glb, nh, name):
    t, inner = u.shape
    hb, nb, rows, wide, sq, _, glb_spec = _gdn_specs(nh, inner, t, SCAN_HB, SCAN_NB)

    def body(qe_ref, kel_ref, wb_ref, u_ref, qk_ref, gl_ref, o_ref, vn_ref, sall_ref, s_ref):
        @pl.when(pl.program_id(1) == 0)
        def _():
            s_ref[...] = jnp.zeros(s_ref.shape, F32)

        cs = lambda h: slice(h * HEAD_DIM, (h + 1) * HEAD_DIM)
        for c in range(nb):
            rs = slice(c * CHUNK, (c + 1) * CHUNK)
            heads = lambda ref: jnp.stack([ref[rs, cs(h)] for h in range(hb)])
            s = s_ref[...]
            sall_ref[c] = s
            sb = _bf(s)
            vn = heads(u_ref) - _dot(heads(wb_ref), sb, BNN)
            vnb = _bf(vn)
            o = _dot(heads(qe_ref), sb, BNN) + _dot(qk_ref[:, rs, :], vnb, BNN)
            gl = jnp.stack([gl_ref[c * 8:c * 8 + 1, cs(h)] for h in range(hb)])
            s_ref[...] = s * gl + _dot(heads(kel_ref), vnb, BTN)
            for h in range(hb):
                vn_ref[rs, cs(h)] = vn[h]
                o_ref[rs, cs(h)] = o[h]

    return pl.pallas_call(
        body, name=name, grid=(nh // hb, t // rows),
        in_specs=[wide, wide, wide, wide, sq, glb_spec],
        out_specs=[wide, wide, pl.BlockSpec((nb, hb, HEAD_DIM, HEAD_DIM), lambda g, n: (n, g, 0, 0))],
        out_shape=[jax.ShapeDtypeStruct((t, inner), F32), jax.ShapeDtypeStruct((t, inner), F32),
                   jax.ShapeDtypeStruct((t // CHUNK, nh, HEAD_DIM, HEAD_DIM), F32)],
        scratch_shapes=[pltpu.VMEM((hb, HEAD_DIM, HEAD_DIM), F32)],
        compiler_params=_cparams("parallel", "arbitrary"),
    )(qe, kel, wb, u, qk, glb)


def _gdn_scan_bwd(do, qe, kel, wb, vn, qk, glb, sall, nh, name):
    t, inner = do.shape
    hb, nb, rows, _, _, _, _ = _gdn_specs(nh, inner, t, SCAN_HB, SCAN_NB)
    last = t // rows - 1
    wide = pl.BlockSpec((rows, hb * HEAD_DIM), lambda g, n: (last - n, g))
    sq = pl.BlockSpec((hb, rows, CHUNK), lambda g, n: (g, last - n, 0))
    glb_spec = pl.BlockSpec((nb * 8, hb * HEAD_DIM), lambda g, n: (last - n, g))

    def body(do_ref, qe_ref, kel_ref, wb_ref, vn_ref, qk_ref, gl_ref, sall_ref,
             dvn_ref, dw_ref, dqe_ref, dkel_ref, dqk_ref, dgl_ref, ds_ref):
        @pl.when(pl.program_id(1) == 0)
        def _():
            ds_ref[...] = jnp.zeros(ds_ref.shape, F32)

        cs = lambda h: slice(h * HEAD_DIM, (h + 1) * HEAD_DIM)
        for c in reversed(range(nb)):
            rs = slice(c * CHUNK, (c + 1) * CHUNK)
            heads = lambda ref: jnp.stack([ref[rs, cs(h)] for h in range(hb)])
            ds, s = ds_ref[...], sall_ref[c]
            dsb, sb = _bf(ds), _bf(s)
            dob, vnb = _bf(heads(do_ref)), _bf(heads(vn_ref))
            dvn = _dot(qk_ref[:, rs, :], dob, BTN) + _dot(heads(kel_ref), dsb, BNN)
            dvnb = _bf(dvn)
            dw = -_dot(dvnb, sb, BNT)
            dqe = _dot(dob, sb, BNT)
            dkel = _dot(vnb, dsb, BNT)
            dqk_ref[:, rs, :] = _dot(dob, vnb, BNT)
            dgl = jnp.sum(jnp.sum(ds * s, axis=2, keepdims=True), axis=1, keepdims=True)
            gl = jnp.stack([gl_ref[c * 8:c * 8 + 1, cs(h)] for h in range(hb)])
            ds_ref[...] = ds * gl + _dot(heads(qe_ref), dob, BTN) - _dot(heads(wb_ref), dvnb, BTN)
            for h in range(hb):
                dvn_ref[rs, cs(h)] = dvn[h]
                dw_ref[rs, cs(h)] = dw[h]
                dqe_ref[rs, cs(h)] = dqe[h]
                dkel_ref[rs, cs(h)] = dkel[h]
                dgl_ref[c * 8:(c + 1) * 8, cs(h)] = jnp.broadcast_to(dgl[h], (8, HEAD_DIM))

    big = jax.ShapeDtypeStruct((t, inner), F32)
    return pl.pallas_call(
        body, name=name, grid=(nh // hb, t // rows),
        in_specs=[wide, wide, wide, wide, wide, sq, glb_spec,
                  pl.BlockSpec((nb, hb, HEAD_DIM, HEAD_DIM), lambda g, n: (last - n, g, 0, 0))],
        out_specs=[wide] * 4 + [sq, glb_spec],
        out_shape=[big] * 4 + [jax.ShapeDtypeStruct((nh, t, CHUNK), F32),
                               jax.ShapeDtypeStruct((t // CHUNK * 8, inner), F32)],
        scratch_shapes=[pltpu.VMEM((hb, HEAD_DIM, HEAD_DIM), F32)],
        compiler_params=_cparams("parallel", "arbitrary"),
    )(do, qe, kel, wb, vn, qk, glb, sall)


def _gdn_intra_bwd(q, k, v, gates, tm, w, u, dvn, dw, dqe, dkel, dqk, dglb, nh, name, comm=None):
    t, inner = q.shape
    hb, nb, rows, wide, sq, gts, glb = _gdn_specs(nh, inner, t)

    def body(q_ref, k_ref, v_ref, g_ref, tm_ref, w_ref, u_ref, dvn_ref, dw_ref, dqe_ref, dkel_ref, dqk_ref,
             dgl_ref, dq_ref, dk_ref, dv_ref, dg_ref):
        first = pl.program_id(0) * hb
        selg, selb = _head_select(first, hb, 0), _head_select(first, hb, nh)
        i, j = _iota2(CHUNK, CHUNK)
        lane8 = lax.broadcasted_iota(jnp.int32, (CHUNK, 8), 1)
        row = lax.broadcasted_iota(jnp.int32, (CHUNK, 1), 0)
        lower = jnp.where(j <= i, 1.0, 0.0).astype(F32)
        rsum = lambda x: jnp.sum(x, axis=-1, keepdims=True)
        units, rs, cs, (qv, kv, vv, wv, uv, dvn, dw, dqe, dkel), gcol, grow, bcol = _unit_inputs(
            (q_ref, k_ref, v_ref, w_ref, u_ref, dvn_ref, dw_ref, dqe_ref, dkel_ref), g_ref, selg, selb, hb, nb)
        nu = len(units)
        tmv = jnp.stack([tm_ref[h, rs(c), :] for c, h in units])
        dqk = jnp.where(j <= i, jnp.stack([dqk_ref[h, rs(c), :] for c, h in units]), 0.0)
        dgl = jnp.stack([dgl_ref[c * 8:c * 8 + 1, h * HEAD_DIM:h * HEAD_DIM + 1] for c, h in units])
        e, el, gl, decay = _decay_terms(gcol, grow)
        kb = kv * bcol
        qb, kbf, kbb = _bf(qv), _bf(kv), _bf(kb)
        dqkr = _bf(dqk * decay)
        dq = dqe * e + _dot(dqkr, kbf, BNN)
        dk = dkel * el + _dot(dqkr, qb, BTN)
        de = rsum(dqe * qv)
        del_ = rsum(dkel * kv)
        mq = dqk * _dot(qb, kbf, BNT) * decay
        dsol = _dot3(tmv, jnp.concatenate([dvn, dw], axis=2), BTN)
        dvb, dkbe = dsol[:, :, :HEAD_DIM], dsol[:, :, HEAD_DIM:]
        da = -jnp.where(j < i, _dot3(dsol, jnp.concatenate([uv, wv], axis=2), BNT), 0.0)
        dkk = _bf(da * decay)
        ma = da * _dot(kbb, kbf, BNT) * decay
        dkb = dkbe * e + _dot(dkk, kbf, BNN)
        de = de + rsum(dkbe * kb)
        dk = dk + _dot(dkk, kbb, BTN) + dkb * bcol
        dv = dvb * bcol
        dbeta = rsum(dkb * kv) + rsum(dvb * vv)
        m = mq + ma
        ones = jnp.ones((nu, CHUNK, LANES), F32)
        dgc = rsum(m) - _dot_exact(m, ones, BTN, False)[:, :, 0:1] + de * e - del_ * el
        tail = jnp.sum(del_ * el, axis=1, keepdims=True) + dgl * gl
        dgc = dgc + jnp.where(row == CHUNK - 1, tail, 0.0)
        for n, (c, h) in enumerate(units):
            dq_ref[rs(c), cs(h)] = dq[n]
            dk_ref[rs(c), cs(h)] = dk[n]
            dv_ref[rs(c), cs(h)] = dv[n]
        for c in range(nb):
            dgc_cols = jnp.zeros((CHUNK, 8), F32)
            dbeta_cols = jnp.zeros((CHUNK, 8), F32)
            for h in range(hb):
                dgc_cols = jnp.where(lane8 == h, dgc[c * hb + h], dgc_cols)
                dbeta_cols = jnp.where(lane8 == h, dbeta[c * hb + h], dbeta_cols)
            dg_cols = _dot_exact(lower, dgc_cols, TN, True)
            dg_ref[rs(c), :] = _dot_exact(dg_cols, selg, NN, False) + _dot_exact(dbeta_cols, selb, NN, False)

    big = jax.ShapeDtypeStruct((t, inner), F32)
    return _grid_call(
        body, name=name, grid=(nh // hb, t // rows),
        in_specs=[wide, wide, wide, gts, sq, wide, wide, wide, wide, wide, wide, sq, glb],
        out_specs=[wide, wide, wide, pl.BlockSpec((None, rows, LANES), lambda g, n: (g, n, 0))],
        out_shape=[big, big, big, jax.ShapeDtypeStruct((nh // hb, t, LANES), F32)],
        args=(q, k, v, gates, tm, w, u, dvn, dw, dqe, dkel, dqk, dglb), semantics=("parallel", "parallel"),
        comm=comm)


def _layer_a_fwd(x, hn, w_in_t, w_ab_t, conv_w, alog_row, dt_row, onw, nh, comm, w_out_of):
    t, d = x.shape
    inner = nh * HEAD_DIM
    proj = _mm(hn, w_in_t, "nt", t, 4 * inner, d, out_dtype=F32, name="a_proj")
    ab = _mm(hn, w_ab_t, "nt", t, LANES, d, out_dtype=F32, name="a_proj_ab")
    gates = _gates_fwd(ab, alog_row, dt_row, nh, "a_gates")
    q = _conv_fwd(proj, conv_w, 0, inner, "q", "a_conv_q")
    k = _conv_fwd(proj, conv_w, inner, inner, "k", "a_conv_k")
    v = _conv_fwd(proj, conv_w, 2 * inner, inner, "v", "a_conv_v")
    qe, kel, wb, w, u, qk, tm, glb, *carried = _gdn_intra_fwd(q, k, v, gates, nh, "a_intra", comm)
    o, vn, sall = _gdn_scan_fwd(qe, kel, wb, u, qk, glb, nh, "a_scan")
    g = _gate_fwd(o, proj, 3 * inner, inner, "a_gate", norm_w=onw)
    w_out = w_out_of(carried)
    h1 = _mm(g, w_out, "nn", t, d, inner, out_dtype=F32, name="a_out", res=x)
    return h1, (hn, proj, ab, gates, q, k, v, qe, kel, wb, w, u, qk, tm, glb, o, vn, sall, g), w_out, carried


def _layer_a_bwd(dh1, dh1b, x, nw, w_in_t, w_ab_t, conv_w, alog_row, dt_row, onw, w_out, nh, saved, comm_of,
                 own_comm):
    hn, proj, ab, gates, q, k, v, qe, kel, wb, w, u, qk, tm, glb, o, vn, sall, g = saved
    t, d = x.shape
    inner = w_out.shape[0]
    dg = _mm(dh1b, w_out, "nt", t, inner, d, out_dtype=F32, name="a_dgate")
    dw_out = _mm(g, dh1b, "tn", inner, d, t, out_dtype=F32, name="a_dwout")
    comm = comm_of(dw_out)
    do, dproj, donw = _gate_bwd(dg, o, proj, 3 * inner, inner, "a_gate_bwd", F32, norm_w=onw,
                                dest=(None, 3 * inner, 4 * inner))
    dvn, dw, dqe, dkel, dqk, dglb = _gdn_scan_bwd(do, qe, kel, wb, vn, qk, glb, sall, nh, "a_scan_bwd")
    dq, dk, dv, dgates, *carried = _gdn_intra_bwd(q, k, v, gates, tm, w, u, dvn, dw, dqe, dkel, dqk, dglb, nh,
                                                  "a_intra_bwd", comm)
    dproj, dcq = _conv_bwd(dq, proj, conv_w, 0, inner, "q", "a_conv_q_bwd", dest=(dproj, 0, 4 * inner))
    dproj, dck = _conv_bwd(dk, proj, conv_w, inner, inner, "k", "a_conv_k_bwd", dest=(dproj, inner, 4 * inner))
    dproj, dcv = _conv_bwd(dv, proj, conv_w, 2 * inner, inner, "v", "a_conv_v_bwd",
                           dest=(dproj, 2 * inner, 4 * inner))
    dab, dsmall = _gates_bwd(ab, alog_row, dt_row, dgates, nh, "a_gates_bwd")
    dw_in_t = _mm(dproj, hn, "tn", 4 * inner, d, t, out_dtype=F32, name="a_dwin")
    dw_ab_t = _mm(dab, hn, "tn", LANES, d, t, out_dtype=F32, name="a_dwin_ab")
    dconv = jnp.concatenate([dcq[:CONV_K], dck[:CONV_K], dcv[:CONV_K]], axis=1)
    dhn = _mm(dab, w_ab_t, "nn", t, d, LANES, out_dtype=F32, name="a_dhn_ab")
    own = own_comm(dw_in_t, dw_ab_t, dconv)
    dhn = _mm(dproj, w_in_t, "nn", t, d, 4 * inner, out_dtype=F32, name="a_dhn", res=dhn, comm=own)
    dhn, carried_own = dhn if own is not None else (dhn, [])
    dx, _, dnw = _rms_bwd(x, nw, dhn, dh1, "a_rms_bwd")
    return dx, dnw, dsmall, donw, carried, carried_own


def _rows_of(a, rows):
    flat = a.reshape(-1)
    return jnp.pad(flat, (0, rows * LANES - flat.shape[0])).reshape(rows, LANES)


def _to_slabs(g, axis):
    shape = g.shape[:axis] + (N_DEV, g.shape[axis] // N_DEV) + g.shape[axis + 1:]
    return jnp.moveaxis(g.reshape(shape), axis, 0)


def _from_slabs(s, axis):
    m = jnp.moveaxis(s, 0, axis)
    return m.reshape(m.shape[:axis] + (m.shape[axis] * m.shape[axis + 1],) + m.shape[axis + 2:])


def kernel(x, norm_w, a_w_in, a_conv_w, a_a_log, a_dt_bias, a_out_norm_w, a_w_out, b_w_in, b_q_norm_w, b_k_norm_w, b_rel_bias, b_w_out, loss_target, m_norm_w, m_a_w_in, m_a_conv_w, m_a_a_log, m_a_dt_bias, m_a_out_norm_w, m_a_w_out, m_b_w_in, m_b_q_norm_w, m_b_k_norm_w, m_b_rel_bias, m_b_w_out, v_norm_w, v_a_w_in, v_a_conv_w, v_a_a_log, v_a_dt_bias, v_a_out_norm_w, v_a_w_out, v_b_w_in, v_b_q_norm_w, v_b_k_norm_w, v_b_rel_bias, v_b_w_out):
    xs, target = x[0], loss_target[0]
    nh = a_a_log.shape[-1]
    inner = N_DEV * a_w_out.shape[1]

    d = xs.shape[1]
    nw0, nw1 = norm_w[0:1], norm_w[1:2]
    hn0, (ga_in, g_conv) = _rms_fwd(
        xs, nw0, "a_rms", comm=_RoutedGather([a_w_in[0].T.astype(BF16), a_conv_w[0]]))
    wa_in_t = ga_in.reshape(-1, d)
    wa_ab_t = jnp.pad(wa_in_t[4 * inner:], ((0, LANES - 2 * nh), (0, 0)))
    conv_w = _from_slabs(g_conv, 1)
    alog_row = jnp.pad(a_a_log, ((0, 0), (0, LANES - nh)))
    dt_row = jnp.pad(a_dt_bias, ((0, 0), (0, LANES - nh)))

    h1, saved_a, wa_out, (gb_in, gb_out, _) = _layer_a_fwd(
        xs, hn0, wa_in_t, wa_ab_t, conv_w, alog_row, dt_row, a_out_norm_w, nh,
        _Comm("gather", [b_w_in[0].T.astype(BF16), b_w_out[0].astype(BF16), a_w_out[0].astype(BF16)]),
        lambda gathered: _from_slabs(gathered[2], 0))
    wb_in_t = gb_in.reshape(-1, d)
    wb_out = _from_slabs(gb_out, 0)
    bias = _bias_tiles(_pad_rel_bias(b_rel_bias[0]), "b_bias_tiles")
    h2, saved_b = _layer_b_fwd(h1, nw1, wb_in_t, b_q_norm_w, b_k_norm_w, bias, wb_out)
    dh2, dh2b, loss_row = _loss_grad(h2, target, "loss")

    dh1, dh1b, dnw1, dwb_in_t, dqw, dkw, drb, dwb_out = _layer_b_bwd(
        dh2, dh2b, h1, nw1, wb_in_t, b_q_norm_w, b_k_norm_w, bias, wb_out, saved_b)

    def exchange_early(dwa_out):
        return _Comm("exchange", [dwb_in_t.reshape(N_DEV, -1, d).astype(BF16), _to_slabs(dwb_out, 0).astype(BF16),
                                  _to_slabs(dwa_out, 0).astype(BF16)])

    def exchange_last(dwa_in_t, dwa_ab_t, dconv):
        full = jnp.concatenate([dwa_in_t, dwa_ab_t[:2 * nh]], axis=0)
        return _Comm("exchange", [full.reshape(N_DEV, -1, d).astype(BF16), _to_slabs(dconv, 1)])

    dx, dnw0, dsmall, donw, (pb_in, pb_out, pa_out), (pa_in, p_conv) = _layer_a_bwd(
        dh1, dh1b, xs, nw0, wa_in_t, wa_ab_t, conv_w, alog_row, dt_row, a_out_norm_w, wa_out, nh, saved_a,
        exchange_early, exchange_last)
    pa_in, pb_in = jnp.swapaxes(pa_in, 1, 2), jnp.swapaxes(pb_in, 1, 2)
    big = {}
    for name, p, w, m, v in (("a_w_in", pa_in, a_w_in, m_a_w_in, v_a_w_in),
                             ("a_w_out", pa_out, a_w_out, m_a_w_out, v_a_w_out),
                             ("b_w_in", pb_in, b_w_in, m_b_w_in, v_b_w_in),
                             ("b_w_out", pb_out, b_w_out, m_b_w_out, v_b_w_out),
                             ("a_conv_w", p_conv, a_conv_w, m_a_conv_w, v_a_conv_w)):
        big[name] = [o[None] for o in _adamw(p, w[0], m[0], v[0], "adamw_" + name)]

    small = (("norm_w", norm_w, m_norm_w, v_norm_w, jnp.concatenate([dnw0, dnw1], axis=0)),
             ("a_a_log", a_a_log, m_a_a_log, v_a_a_log, dsmall[0:1, :nh]),
             ("a_dt_bias", a_dt_bias, m_a_dt_bias, v_a_dt_bias, dsmall[1:2, :nh]),
             ("a_out_norm_w", a_out_norm_w, m_a_out_norm_w, v_a_out_norm_w, donw),
             ("b_q_norm_w", b_q_norm_w, m_b_q_norm_w, v_b_q_norm_w, dqw),
             ("b_k_norm_w", b_k_norm_w, m_b_k_norm_w, v_b_k_norm_w, dkw),
             ("b_rel_bias", b_rel_bias, m_b_rel_bias, v_b_rel_bias, drb))
    rows = [8 * (-(-w.size // (8 * LANES))) for _, w, _, _, _ in small]
    pack = lambda arrs: jnp.concatenate([_rows_of(a, r) for a, r in zip(arrs, rows)] + [jnp.zeros((8, LANES), F32)], axis=0)
    g_pack = jnp.concatenate([_rows_of(g, r) for (_, _, _, _, g), r in zip(small, rows)]
                             + [jnp.broadcast_to(loss_row, (8, LANES))], axis=0)
    (g_all,) = _comm_call(_Comm("gather", [g_pack]), "gather_small_grads")
    outs_small = _adamw(g_all, pack([s[1] for s in small]), pack([s[2] for s in small]),
                        pack([s[3] for s in small]), "adamw_small")
    start = 0
    for (name, w, _, _, _), r in zip(small, rows):
        big[name] = [o[start:start + r].reshape(-1)[:w.size].reshape(w.shape) for o in outs_small]
        start += r
    loss = outs_small[0][start, 0]

    order = ("norm_w", "a_w_in", "a_conv_w", "a_a_log", "a_dt_bias", "a_out_norm_w", "a_w_out", "b_w_in",
             "b_q_norm_w", "b_k_norm_w", "b_rel_bias", "b_w_out")
    return (loss, dx[None]) + tuple(big[n][i] for i in range(4) for n in order)
```

```python
import functools
import math

import jax
import jax.numpy as jnp
from jax import lax
from jax.experimental import pallas as pl
from jax.experimental.pallas import tpu as pltpu

F32 = jnp.float32
BF16 = jnp.bfloat16
MESH_IDS = pl.DeviceIdType.MESH
N_DEV = 8
CHUNK = 64
HEAD_DIM = 128
EPS = 1e-6
CONV_K = 4
LEFT_CHUNKS = 8
REL_CLIP = 256
Q_TILE = LEFT_CHUNKS * CHUNK
ADAM_LR = 0.001
ADAM_B1 = 0.9
ADAM_B2 = 0.999
ADAM_EPS = 1e-08
ADAM_WD = 0.01
ADAM_STEP = 10
NEG_BIG = -1e30
VMEM_LIMIT_BYTES = 56 * 1024 * 1024
HIGHEST = lax.Precision.HIGHEST
ANY = pl.BlockSpec(memory_space=pl.ANY)


def _cparams(*sem):
    return pltpu.CompilerParams(dimension_semantics=tuple(sem), vmem_limit_bytes=VMEM_LIMIT_BYTES)


NN, NT, TN = (((1,), (0,)), ((), ())), (((1,), (1,)), ((), ())), (((0,), (0,)), ((), ()))
BNN, BNT, BTN = (((2,), (1,)), ((0,), (0,))), (((2,), (2,)), ((0,), (0,))), (((1,), (1,)), ((0,), (0,)))


def _dot(a, b, dims, precision=None):
    return lax.dot_general(a, b, dims, preferred_element_type=F32, precision=precision)


def _nn(a, b, precision=None):
    return _dot(a, b, NN, precision)


def _nt(a, b, precision=None):
    return _dot(a, b, NT, precision)


def _tn(a, b, precision=None):
    return _dot(a, b, TN, precision)


def _bf(x):
    return x.astype(BF16)


def _split(x, pieces=2):
    out = []
    for _ in range(pieces - 1):
        hi = x.astype(BF16)
        out.append(hi)
        x = x - hi.astype(F32)
    return out + [x.astype(BF16)]


def _dot3(a, b, dims):
    (ah, al), (bh, bl) = _split(a), _split(b)
    return _dot(ah, bh, dims) + (_dot(ah, bl, dims) + _dot(al, bh, dims))


def _dot_exact(a, b, dims, split_b):
    if split_b:
        a = a.astype(BF16)
        parts = [_dot(a, p, dims) for p in _split(b, 3)]
    else:
        b = b.astype(BF16)
        parts = [_dot(p, b, dims) for p in _split(a, 3)]
    return parts[0] + (parts[1] + parts[2])


def _sigmoid(x):
    return 0.5 * jnp.tanh(0.5 * x) + 0.5


def _silu(x):
    return x * _sigmoid(x)


def _dsilu(x):
    s = _sigmoid(x)
    return s * (1.0 + x * (1.0 - s))


def _my_pos():
    return lax.axis_index("x"), lax.axis_index("y"), lax.axis_index("c")


def _peers(x, y, c):
    def flip(v, f):
        return 1 - v if f else v

    return [(flip(x, kx), flip(y, ky), flip(c, kc)) for kx in (0, 1) for ky in (0, 1) for kc in (0, 1)][1:]


def _lin(p):
    return 4 * p[0] + 2 * p[1] + p[2]


class _Comm:
    def __init__(self, kind, arrays):
        self.kind, self.arrays, self.n = kind, list(arrays), len(arrays)

    def out_shape(self):
        lead = (N_DEV,) if self.kind == "gather" else ()
        return [jax.ShapeDtypeStruct(lead + a.shape, a.dtype) for a in self.arrays]

    def scratch(self):
        return [pltpu.SemaphoreType.DMA((7 * self.n,)), pltpu.SemaphoreType.DMA((7 * self.n,)),
                pltpu.SemaphoreType.DMA((self.n,))]

    def _copies(self, ins, outs, sems, arrivals):
        send_sems, recv_sems, local_sems = sems
        x, y, c = _my_pos()
        me = _lin((x, y, c))
        gather = self.kind == "gather"
        mine = [ins[t] if gather else ins[t].at[me] for t in range(self.n)]
        remote = []
        for k, peer in enumerate(_peers(x, y, c)):
            for t in range(self.n):
                if arrivals:
                    src, dst = mine[t], outs[t].at[_lin(peer)]
                else:
                    src, dst = (ins[t] if gather else ins[t].at[_lin(peer)]), outs[t].at[me]
                remote.append(pltpu.make_async_remote_copy(
                    src_ref=src, dst_ref=dst, send_sem=send_sems.at[k * self.n + t],
                    recv_sem=recv_sems.at[k * self.n + t], device_id=peer, device_id_type=MESH_IDS))
        if arrivals:
            return remote
        return [pltpu.make_async_copy(mine[t], outs[t].at[me], local_sems.at[t]) for t in range(self.n)], remote

    def start(self, ins, outs, sems):
        local, sends = self._copies(ins, outs, sems, False)
        for cp in local + sends:
            cp.start()

    def finish(self, ins, outs, sems):
        for cp in self._copies(ins, outs, sems, True):
            cp.wait_recv()
        local, sends = self._copies(ins, outs, sems, False)
        for cp in sends:
            cp.wait_send()
        for cp in local:
            cp.wait()


def _xor(a, b):
    return a + b - 2 * a * b


class _RoutedGather(_Comm):
    def __init__(self, arrays):
        super().__init__("gather", arrays)

    def _plan(self, outs, sems):
        send_sems, recv_sems, _ = sems
        x, y, c = _my_pos()
        sib, xn, yn, dg = (x, y, 1 - c), (1 - x, y, c), (x, 1 - y, c), (1 - x, 1 - y, c)
        via = (_xor(x, 1 - c), _xor(y, c), c)
        onto = (_xor(x, c), _xor(y, 1 - c), c)
        routes = [(None, sib, sib), (None, xn, xn), (None, yn, yn), (via, onto, dg),
                  (xn, sib, (1 - x, y, 1 - c)), (yn, sib, (x, 1 - y, 1 - c)), (dg, sib, (1 - x, 1 - y, 1 - c))]

        def copy(k, t, src, slot, target):
            return pltpu.make_async_remote_copy(
                src_ref=src, dst_ref=outs[t].at[slot], send_sem=send_sems.at[k * self.n + t],
                recv_sem=recv_sems.at[k * self.n + t], device_id=target, device_id_type=MESH_IDS)

        return (x, y, c), routes, copy

    def start(self, ins, outs, sems):
        me, routes, copy = self._plan(outs, sems)
        for t in range(self.n):
            pltpu.make_async_copy(ins[t], outs[t].at[_lin(me)], sems[2].at[t]).start()
            for k in range(3):
                copy(k, t, ins[t], _lin(me), routes[k][1]).start()

    def finish(self, ins, outs, sems):
        me, routes, copy = self._plan(outs, sems)

        def arrived(k):
            for t in range(self.n):
                copy(k, t, ins[t], _lin(routes[k][2]), me).wait_recv()

        def pass_on(k):
            for t in range(self.n):
                copy(k, t, outs[t].at[_lin(routes[k][0])], _lin(routes[k][0]), routes[k][1]).start()

        arrived(1)
        arrived(2)
        for k in (3, 4, 5):
            pass_on(k)
        arrived(3)
        pass_on(6)
        for k in (0, 4, 5, 6):
            arrived(k)
        for t in range(self.n):
            for k in range(7):
                src = ins[t] if k < 3 else outs[t].at[_lin(routes[k][0])]
                copy(k, t, src, _lin(me), routes[k][1]).wait_send()
            pltpu.make_async_copy(ins[t], outs[t].at[_lin(me)], sems[2].at[t]).wait()


def _comm_call(comm, name):
    n = comm.n

    def body(*refs):
        ins, outs, sems = refs[:n], refs[n:2 * n], refs[2 * n:]
        comm.start(ins, outs, sems)
        comm.finish(ins, outs, sems)

    return pl.pallas_call(
        body, name=name, out_shape=comm.out_shape(), in_specs=[ANY] * n, out_specs=[ANY] * n,
        scratch_shapes=comm.scratch(),
    )(*comm.arrays)


def _grid_call(body, *, name, grid, in_specs, out_specs, out_shape, args, scratch_shapes=(), semantics=None, comm=None):
    if comm is None:
        return pl.pallas_call(
            body, name=name, grid=grid, in_specs=in_specs, out_specs=out_specs, out_shape=out_shape,
            scratch_shapes=list(scratch_shapes), compiler_params=_cparams(*semantics),
        )(*args)
    n_in, n_out, n_sc, n = len(in_specs), len(out_specs), len(scratch_shapes), comm.n

    def full(*refs):
        ins, refs = refs[:n_in], refs[n_in:]
        cins, refs = refs[:n], refs[n:]
        outs, refs = refs[:n_out], refs[n_out:]
        couts, refs = refs[:n], refs[n:]
        scratch, sems = refs[:n_sc], refs[n_sc:]
        ids = [pl.program_id(a) for a in range(len(grid))]
        first = functools.reduce(jnp.logical_and, [i == 0 for i in ids])
        last = functools.reduce(jnp.logical_and, [i == g - 1 for i, g in zip(ids, grid)])

        @pl.when(first)
        def _():
            comm.start(cins, couts, sems)

        body(*ins, *outs, *scratch)

        @pl.when(last)
        def _():
            comm.finish(cins, couts, sems)

    return pl.pallas_call(
        full, name=name, grid=grid, in_specs=list(in_specs) + [ANY] * n, out_specs=list(out_specs) + [ANY] * n,
        out_shape=list(out_shape) + comm.out_shape(), scratch_shapes=list(scratch_shapes) + comm.scratch(),
        compiler_params=_cparams(*(["arbitrary"] * len(grid))),
    )(*(list(args) + comm.arrays))


def _mm(a, b, mode, m, n, k, *, out_dtype, name, tm=1024, tn=1024, tk=2048,
        a_m0=0, a_k0=0, b_n0=0, b_k0=0, res=None, comm=None, loss_target=None):
    tm, tn, tk = min(tm, m), min(tn, n), min(tk, k)
    nm, nn, nk = m // tm, n // tn, k // tk
    assert nm * tm == m and nn * tn == n and nk * tk == k
    am, ak, bn, bk = a_m0 // tm, a_k0 // tk, b_n0 // tn, b_k0 // tk
    assert am * tm == a_m0 and ak * tk == a_k0 and bn * tn == b_n0 and bk * tk == b_k0
    if mode == "tn":
        a_spec = pl.BlockSpec((tk, tm), lambda i, j, q: (q + ak, i + am))
        a_dims = (0,)
    else:
        a_spec = pl.BlockSpec((tm, tk), lambda i, j, q: (i + am, q + ak))
        a_dims = (1,)
    if mode == "nt":
        b_spec = pl.BlockSpec((tn, tk), lambda i, j, q: (j + bn, q + bk))
        b_dims = (1,)
    else:
        b_spec = pl.BlockSpec((tk, tn), lambda i, j, q: (q + bk, j + bn))
        b_dims = (0,)
    o_spec = pl.BlockSpec((tm, tn), lambda i, j, q: (i, j))
    has_res = res is not None
    has_loss = loss_target is not None
    n_in = 2 + has_res + has_loss
    n_out = 3 if has_loss else 1

    def body(*refs):
        a_ref, b_ref = refs[0], refs[1]
        res_ref = refs[2] if has_res else None
        o_ref = refs[n_in]
        p = _dot(a_ref[...], b_ref[...], ((a_dims, b_dims), ((), ())))

        def finish(total):
            if has_res:
                total = total + res_ref[...].astype(F32)
            if not has_loss:
                o_ref[...] = total.astype(out_dtype)
                return
            err = total - refs[n_in - 1][...]
            grad = err * (1.0 / n)
            o_ref[...] = grad
            refs[n_in + 1][...] = grad.astype(BF16)
            l_ref = refs[n_in + 2]
            part = jnp.zeros((1, LANES), F32) + 0.5 * jnp.sum(err * err) * (1.0 / n)
            first = (pl.program_id(0) == 0) & (pl.program_id(1) == 0)

            @pl.when(first)
            def _():
                l_ref[...] = part

            @pl.when(jnp.logical_not(first))
            def _():
                l_ref[...] += part

        if nk == 1:
            finish(p)
        else:
            acc_ref = refs[n_in + n_out]
            q = pl.program_id(2)

            @pl.when(q == 0)
            def _():
                acc_ref[...] = p

            @pl.when(q > 0)
            def _():
                acc_ref[...] += p

            @pl.when(q == nk - 1)
            def _():
                finish(acc_ref[...])

    extra_in = ([res] if has_res else []) + ([loss_target] if has_loss else [])
    if has_loss:
        return _grid_call(
            body, name=name, grid=(nm, nn, nk), in_specs=[a_spec, b_spec] + [o_spec] * len(extra_in),
            out_specs=[o_spec, o_spec, pl.BlockSpec((1, LANES), lambda i, j, q: (0, 0))],
            out_shape=[jax.ShapeDtypeStruct((m, n), F32), jax.ShapeDtypeStruct((m, n), BF16),
                       jax.ShapeDtypeStruct((1, LANES), F32)],
            scratch_shapes=[pltpu.VMEM((tm, tn), F32)] if nk > 1 else [],
            args=[a, b] + extra_in, semantics=("arbitrary", "arbitrary", "arbitrary"))
    out, *carried = _grid_call(
        body, name=name, grid=(nm, nn, nk),
        in_specs=[a_spec, b_spec] + [o_spec] * len(extra_in),
        out_specs=[o_spec], out_shape=[jax.ShapeDtypeStruct((m, n), out_dtype)],
        scratch_shapes=[pltpu.VMEM((tm, tn), F32)] if nk > 1 else [],
        args=[a, b] + extra_in, semantics=("parallel", "parallel", "arbitrary"), comm=comm)
    return out if comm is None else (out, carried)


def _rms_fwd(x, w, name, tr=512, comm=None):
    t, d = x.shape
    tr = min(tr, t)

    def body(x_ref, w_ref, o_ref):
        xv = x_ref[...]
        r = lax.rsqrt(jnp.mean(xv * xv, axis=-1, keepdims=True) + EPS)
        o_ref[...] = (xv * r * w_ref[...]).astype(BF16)

    out, *carried = _grid_call(
        body, name=name, grid=(t // tr,),
        in_specs=[pl.BlockSpec((tr, d), lambda i: (i, 0)), pl.BlockSpec((1, d), lambda i: (0, 0))],
        out_specs=[pl.BlockSpec((tr, d), lambda i: (i, 0))],
        out_shape=[jax.ShapeDtypeStruct((t, d), BF16)], args=(x, w), semantics=("parallel",), comm=comm)
    return out if comm is None else (out, carried)


def _rms_bwd(x, w, dy, dres, name, tr=256):
    t, d = x.shape
    tr = min(tr, t)

    def body(x_ref, w_ref, dy_ref, dres_ref, dx_ref, dxb_ref, dw_ref):
        xv = x_ref[...]
        dyv = dy_ref[...].astype(F32)
        r = lax.rsqrt(jnp.mean(xv * xv, axis=-1, keepdims=True) + EPS)
        gy = dyv * w_ref[...]
        proj = jnp.sum(gy * xv, axis=-1, keepdims=True) * (1.0 / d)
        dx = dres_ref[...] + r * gy - xv * (r * r * r) * proj
        dx_ref[...] = dx
        dxb_ref[...] = dx.astype(BF16)
        part = jnp.sum(dyv * xv * r, axis=0, keepdims=True)

        @pl.when(pl.program_id(0) == 0)
        def _():
            dw_ref[...] = part

        @pl.when(pl.program_id(0) > 0)
        def _():
            dw_ref[...] += part

    row = pl.BlockSpec((tr, d), lambda i: (i, 0))
    vec = pl.BlockSpec((1, d), lambda i: (0, 0))
    return pl.pallas_call(
        body, name=name, grid=(t // tr,),
        in_specs=[row, vec, row, row], out_specs=[row, row, vec],
        out_shape=[jax.ShapeDtypeStruct((t, d), F32), jax.ShapeDtypeStruct((t, d), BF16),
                   jax.ShapeDtypeStruct((1, d), F32)],
        compiler_params=_cparams("arbitrary"),
    )(x, w, dy, dres)


def _adamw(parts, w, m, v, name, tr=128, transposed=False):
    r, c = w.shape
    tr = tr if r % tr == 0 else r
    c1 = 1.0 - ADAM_B1 ** ADAM_STEP
    c2 = 1.0 - ADAM_B2 ** ADAM_STEP

    def body(p_ref, w_ref, m_ref, v_ref, g_ref, d_ref, nm_ref, nv_ref):
        g = p_ref[0].astype(F32)
        for s in range(1, N_DEV):
            g = g + p_ref[s].astype(F32)
        if transposed:
            i, j = lax.broadcasted_iota(jnp.int32, (tr, tr), 0), lax.broadcasted_iota(jnp.int32, (tr, tr), 1)
            g = _dot_exact(jnp.where(i == j, 1.0, 0.0), g, NT, True)
        nm = ADAM_B1 * m_ref[...] + (1.0 - ADAM_B1) * g
        nv = ADAM_B2 * v_ref[...] + (1.0 - ADAM_B2) * (g * g)
        m_hat = nm / c1
        v_hat = nv / c2
        g_ref[...] = g
        d_ref[...] = -ADAM_LR * (m_hat / (jnp.sqrt(v_hat) + ADAM_EPS) + ADAM_WD * w_ref[...])
        nm_ref[...] = nm
        nv_ref[...] = nv

    blk = pl.BlockSpec((tr, c), lambda i: (i, 0))
    p_spec = (pl.BlockSpec((N_DEV, c, tr), lambda i: (0, 0, i)) if transposed
              else pl.BlockSpec((N_DEV, tr, c), lambda i: (0, i, 0)))
    return pl.pallas_call(
        body, name=name, grid=(r // tr,),
        in_specs=[p_spec, blk, blk, blk],
        out_specs=[blk] * 4, out_shape=[jax.ShapeDtypeStruct((r, c), F32)] * 4,
        compiler_params=_cparams("parallel"),
    )(parts, w, m, v)


ROW_TILE, ROW_HEADS = 256, 16
CONV_ROWS, CONV_HEADS = 512, 8


def _window(dest, inner, n_in, out_index):
    if dest is None:
        return inner, 0, [], [], {}
    buf, col0, total = dest
    if buf is None:
        return total, col0, [], [], {}
    return total, col0, [buf], [ANY], {n_in: out_index}


def _skip_ref(body, at, count):
    return body if count == 0 else (lambda *refs: body(*refs[:at], *refs[at + count:]))


def _heads_of(x, nh):
    return [x[:, h * HEAD_DIM:(h + 1) * HEAD_DIM] for h in range(nh)]


def _headnorm_fwd(proj, w, col0, inner, name, tr=ROW_TILE, hb=ROW_HEADS):
    t = proj.shape[0]
    tr = min(tr, t)
    hb = min(hb, inner // HEAD_DIM)
    wc = hb * HEAD_DIM
    c0 = col0 // wc

    def body(x_ref, w_ref, o_ref):
        outs = []
        for xh in _heads_of(x_ref[...], hb):
            r = lax.rsqrt(jnp.mean(xh * xh, axis=-1, keepdims=True) + EPS)
            outs.append((xh * r * w_ref[...]).astype(BF16))
        o_ref[...] = jnp.concatenate(outs, axis=1)

    return pl.pallas_call(
        body, name=name, grid=(t // tr, inner // wc),
        in_specs=[pl.BlockSpec((tr, wc), lambda i, j: (i, j + c0)), pl.BlockSpec((1, HEAD_DIM), lambda i, j: (0, 0))],
        out_specs=pl.BlockSpec((tr, wc), lambda i, j: (i, j)),
        out_shape=jax.ShapeDtypeStruct((t, inner), BF16),
        compiler_params=_cparams("parallel", "parallel"),
    )(proj, w)


def _headnorm_bwd(dy, proj, w, col0, inner, name, tr=ROW_TILE, hb=ROW_HEADS, dest=None):
    t = proj.shape[0]
    tr = min(tr, t)
    hb = min(hb, inner // HEAD_DIM)
    wc = hb * HEAD_DIM
    c0 = col0 // wc
    width, out0, more, more_specs, aliases = _window(dest, inner, 3, 0)

    def body(dy_ref, x_ref, w_ref, dx_ref, dw_ref):
        outs = []
        part = jnp.zeros((1, HEAD_DIM), F32)
        for dyh, xh in zip(_heads_of(dy_ref[...], hb), _heads_of(x_ref[...], hb)):
            r = lax.rsqrt(jnp.mean(xh * xh, axis=-1, keepdims=True) + EPS)
            gy = dyh * w_ref[...]
            pr = jnp.sum(gy * xh, axis=-1, keepdims=True) * (1.0 / HEAD_DIM)
            outs.append((r * gy - xh * (r * r * r) * pr).astype(BF16))
            part = part + jnp.sum(dyh * xh * r, axis=0, keepdims=True)
        dx_ref[...] = jnp.concatenate(outs, axis=1)
        first = (pl.program_id(0) == 0) & (pl.program_id(1) == 0)

        @pl.when(first)
        def _():
            dw_ref[...] = part

        @pl.when(jnp.logical_not(first))
        def _():
            dw_ref[...] += part

    blk = pl.BlockSpec((tr, wc), lambda i, j: (i, j))
    return pl.pallas_call(
        _skip_ref(body, 3, len(more)), name=name, grid=(t // tr, inner // wc),
        in_specs=[blk, pl.BlockSpec((tr, wc), lambda i, j: (i, j + c0)),
                  pl.BlockSpec((1, HEAD_DIM), lambda i, j: (0, 0))] + more_specs,
        out_specs=[pl.BlockSpec((tr, wc), lambda i, j: (i, j + out0 // wc)),
                   pl.BlockSpec((1, HEAD_DIM), lambda i, j: (0, 0))],
        out_shape=[jax.ShapeDtypeStruct((t, width), BF16), jax.ShapeDtypeStruct((1, HEAD_DIM), F32)],
        input_output_aliases=aliases, compiler_params=_cparams("arbitrary", "arbitrary"),
    )(dy, proj, w, *more)


def _gate_fwd(o, proj, zcol0, inner, name, norm_w=None, tr=ROW_TILE, hb=ROW_HEADS):
    t = o.shape[0]
    tr = min(tr, t)
    hb = min(hb, inner // HEAD_DIM)
    wc = hb * HEAD_DIM
    c0 = zcol0 // wc
    has_w = norm_w is not None

    def body(*refs):
        o_ref, z_ref = refs[0], refs[1]
        out_ref = refs[2 + has_w]
        outs = []
        for oh, zh in zip(_heads_of(o_ref[...], hb), _heads_of(z_ref[...], hb)):
            if has_w:
                r = lax.rsqrt(jnp.mean(oh * oh, axis=-1, keepdims=True) + EPS)
                oh = oh * r * refs[2][...]
            outs.append((oh * _silu(zh)).astype(BF16))
        out_ref[...] = jnp.concatenate(outs, axis=1)

    blk = pl.BlockSpec((tr, wc), lambda i, j: (i, j))
    vec = pl.BlockSpec((1, HEAD_DIM), lambda i, j: (0, 0))
    return pl.pallas_call(
        body, name=name, grid=(t // tr, inner // wc),
        in_specs=[blk, pl.BlockSpec((tr, wc), lambda i, j: (i, j + c0))] + ([vec] if has_w else []),
        out_specs=blk, out_shape=jax.ShapeDtypeStruct((t, inner), BF16),
        compiler_params=_cparams("parallel", "parallel"),
    )(*([o, proj] + ([norm_w] if has_w else [])))


def _gate_bwd(dg, o, proj, zcol0, inner, name, do_dtype, norm_w=None, tr=ROW_TILE, hb=ROW_HEADS, dest=None):
    t = o.shape[0]
    tr = min(tr, t)
    hb = min(hb, inner // HEAD_DIM)
    wc = hb * HEAD_DIM
    c0 = zcol0 // wc
    has_w = norm_w is not None
    width, out0, more, more_specs, aliases = _window(dest, inner, 3 + has_w, 1)

    def body(*refs):
        dg_ref, o_ref, z_ref = refs[0], refs[1], refs[2]
        do_ref, dz_ref = refs[3 + has_w], refs[4 + has_w]
        dos, dzs = [], []
        part = jnp.zeros((1, HEAD_DIM), F32)
        for dgh, oh, zh in zip(_heads_of(dg_ref[...], hb), _heads_of(o_ref[...], hb), _heads_of(z_ref[...], hb)):
            dy = dgh * _silu(zh)
            if has_w:
                w = refs[3][...]
                r = lax.rsqrt(jnp.mean(oh * oh, axis=-1, keepdims=True) + EPS)
                on = oh * r
                dzs.append((dgh * on * w * _dsilu(zh)).astype(BF16))
                gy = dy * w
                pr = jnp.sum(gy * oh, axis=-1, keepdims=True) * (1.0 / HEAD_DIM)
                dos.append((r * gy - oh * (r * r * r) * pr).astype(do_dtype))
                part = part + jnp.sum(dy * on, axis=0, keepdims=True)
            else:
                dzs.append((dgh * oh * _dsilu(zh)).astype(BF16))
                dos.append(dy.astype(do_dtype))
        do_ref[...] = jnp.concatenate(dos, axis=1)
        dz_ref[...] = jnp.concatenate(dzs, axis=1)
        if has_w:
            dw_ref = refs[6]
            first = (pl.program_id(0) == 0) & (pl.program_id(1) == 0)

            @pl.when(first)
            def _():
                dw_ref[...] = part

            @pl.when(jnp.logical_not(first))
            def _():
                dw_ref[...] += part

    blk = pl.BlockSpec((tr, wc), lambda i, j: (i, j))
    vec = pl.BlockSpec((1, HEAD_DIM), lambda i, j: (0, 0))
    return pl.pallas_call(
        _skip_ref(body, 3 + has_w, len(more)), name=name, grid=(t // tr, inner // wc),
        in_specs=[blk, blk, pl.BlockSpec((tr, wc), lambda i, j: (i, j + c0))] + ([vec] if has_w else []) + more_specs,
        out_specs=[blk, pl.BlockSpec((tr, wc), lambda i, j: (i, j + out0 // wc))] + ([vec] if has_w else []),
        out_shape=[jax.ShapeDtypeStruct((t, inner), do_dtype), jax.ShapeDtypeStruct((t, width), BF16)]
        + ([jax.ShapeDtypeStruct((1, HEAD_DIM), F32)] if has_w else []),
        input_output_aliases=aliases, compiler_params=_cparams("arbitrary", "arbitrary"),
    )(*([dg, o, proj] + ([norm_w] if has_w else []) + more))


N_REL = 2 * REL_CLIP + 1
REL_PAD = 640
WIN = 2 * Q_TILE


def _diag_onehot():
    i = lax.broadcasted_iota(jnp.int32, (REL_PAD, WIN), 0)
    j = lax.broadcasted_iota(jnp.int32, (REL_PAD, WIN), 1)
    rel = jnp.where(j < Q_TILE + CHUNK, Q_TILE - j, Q_TILE + WIN - j)
    used = (j < Q_TILE + CHUNK) | (j > WIN - CHUNK)
    idx = jnp.clip(rel, -REL_CLIP, REL_CLIP) + REL_CLIP
    return jnp.where(used & (i == idx), 1.0, 0.0).astype(F32)


def _band_mask():
    r = lax.broadcasted_iota(jnp.int32, (Q_TILE, WIN), 0) // CHUNK
    kc = lax.broadcasted_iota(jnp.int32, (Q_TILE, WIN), 1) // CHUNK - LEFT_CHUNKS
    return (kc <= r) & (kc >= r - LEFT_CHUNKS)


def _bias_tiles(rel_bias_pad, name):
    nh = rel_bias_pad.shape[0]

    def body(rb_ref, o_ref):
        dvec = _nn(rb_ref[...], _diag_onehot(), HIGHEST)[0:1, :]
        tile = pltpu.roll(jnp.broadcast_to(dvec, (Q_TILE, WIN)), 0, 1, stride=1, stride_axis=0)
        o_ref[...] = jnp.where(_band_mask(), tile, NEG_BIG)

    return pl.pallas_call(
        body, name=name, grid=(nh,),
        in_specs=[pl.BlockSpec((None, 8, REL_PAD), lambda h: (h, 0, 0))],
        out_specs=pl.BlockSpec((None, Q_TILE, WIN), lambda h: (h, 0, 0)),
        out_shape=jax.ShapeDtypeStruct((nh, Q_TILE, WIN), F32),
        compiler_params=_cparams("parallel"),
    )(rel_bias_pad)


def _bias_grad(dtile, name):
    nh = dtile.shape[0]

    def body(d_ref, o_ref):
        ri = lax.broadcasted_iota(jnp.int32, (Q_TILE, Q_TILE), 0)
        ci = lax.broadcasted_iota(jnp.int32, (Q_TILE, Q_TILE), 1)
        flip = jnp.where(ri + ci == Q_TILE - 1, 1.0, 0.0).astype(F32)
        rev = _dot_exact(flip, d_ref[...], NN, True)
        rolled = pltpu.roll(rev, WIN - (Q_TILE - 1), 1, stride=1, stride_axis=0)
        diag = jnp.broadcast_to(jnp.sum(rolled, axis=0, keepdims=True), (8, WIN))
        o_ref[...] = _nt(diag, _diag_onehot(), HIGHEST)

    return pl.pallas_call(
        body, name=name, grid=(nh,),
        in_specs=[pl.BlockSpec((None, Q_TILE, WIN), lambda h: (h, 0, 0))],
        out_specs=pl.BlockSpec((None, 8, REL_PAD), lambda h: (h, 0, 0)),
        out_shape=jax.ShapeDtypeStruct((nh, 8, REL_PAD), F32),
        compiler_params=_cparams("parallel"),
    )(dtile)


GROUP = 2 * CHUNK
BAND = Q_TILE + GROUP


def _band_rows(r0_ref, r1_ref, g):
    return jnp.concatenate([r0_ref[GROUP * g:, :], r1_ref[:GROUP * (g + 1), :]], axis=0)


N_GROUPS = Q_TILE // GROUP


def _groups(ref):
    return jnp.stack([ref[GROUP * g:GROUP * (g + 1), :] for g in range(N_GROUPS)])


def _bands(r0_ref, r1_ref):
    return jnp.stack([_band_rows(r0_ref, r1_ref, g) for g in range(N_GROUPS)])


def _group_probs(q, kw, b_ref, first_tile):
    bias = jnp.stack([b_ref[GROUP * g:GROUP * (g + 1), GROUP * g:GROUP * g + BAND] for g in range(N_GROUPS)])
    s = _dot(q, kw, BNT) * (HEAD_DIM ** -0.5) + bias
    col = (lax.broadcasted_iota(jnp.int32, (N_GROUPS, GROUP, BAND), 2)
           + GROUP * lax.broadcasted_iota(jnp.int32, (N_GROUPS, GROUP, BAND), 0))
    s = jnp.where(first_tile & (col < Q_TILE), NEG_BIG, s)
    p = jnp.exp(s - jnp.max(s, axis=-1, keepdims=True))
    return p * (1.0 / jnp.sum(p, axis=-1, keepdims=True))


def _attn_fwd(q, k, v, v_col0, bias, name):
    t, inner = q.shape
    vh = v_col0 // HEAD_DIM
    nh, nt = inner // HEAD_DIM, t // Q_TILE

    def body(q_ref, k0_ref, k1_ref, v0_ref, v1_ref, b_ref, o_ref):
        p = _group_probs(_groups(q_ref), _bands(k0_ref, k1_ref), b_ref, pl.program_id(1) == 0)
        o = _dot(_bf(p), _bf(_bands(v0_ref, v1_ref)), BNN)
        for g in range(N_GROUPS):
            o_ref[GROUP * g:GROUP * (g + 1), :] = o[g]

    cur = pl.BlockSpec((Q_TILE, HEAD_DIM), lambda h, i: (i, h))
    prev = pl.BlockSpec((Q_TILE, HEAD_DIM), lambda h, i: (jnp.maximum(i - 1, 0), h))
    v_cur = pl.BlockSpec((Q_TILE, HEAD_DIM), lambda h, i: (i, h + vh))
    v_prev = pl.BlockSpec((Q_TILE, HEAD_DIM), lambda h, i: (jnp.maximum(i - 1, 0), h + vh))
    return pl.pallas_call(
        body, name=name, grid=(nh, nt),
        in_specs=[cur, prev, cur, v_prev, v_cur, pl.BlockSpec((None, Q_TILE, WIN), lambda h, i: (h, 0, 0))],
        out_specs=cur, out_shape=jax.ShapeDtypeStruct((t, inner), F32),
        compiler_params=_cparams("parallel", "parallel"),
    )(q, k, k, v, v, bias)


def _attn_bwd(q, k, v, v_col0, do, bias, name, dest=None):
    t, inner = q.shape
    nh, nt = inner // HEAD_DIM, t // Q_TILE
    scale = HEAD_DIM ** -0.5

    def body(q_ref, k0_ref, k1_ref, v0_ref, v1_ref, do_ref, b_ref, dq_ref, dk_ref, dv_ref, db_ref,
             ck_ref, cv_ref, wk_ref, wv_ref):
        i = pl.program_id(1)

        @pl.when(i == 0)
        def _():
            ck_ref[...] = jnp.zeros(blk, F32)
            cv_ref[...] = jnp.zeros(blk, F32)
            db_ref[...] = jnp.zeros((Q_TILE, WIN), F32)

        @pl.when(i < nt)
        def _():
            wk_ref[...] = jnp.zeros((WIN, HEAD_DIM), F32)
            wv_ref[...] = jnp.zeros((WIN, HEAD_DIM), F32)
            qv, dov = _groups(q_ref), _groups(do_ref)
            kw, vw = _bands(k0_ref, k1_ref), _bf(_bands(v0_ref, v1_ref))
            p = _group_probs(qv, kw, b_ref, i == 0)
            dp = _dot(dov, vw, BNT)
            ds = p * (dp - jnp.sum(p * dp, axis=-1, keepdims=True))
            pb, dsb = _bf(p), _bf(ds)
            dq = _dot(dsb, kw, BNN) * scale
            dkw = _dot(dsb, qv, BTN) * scale
            dvw = _dot(pb, dov, BTN)
            for g in range(N_GROUPS):
                rows, cols = slice(GROUP * g, GROUP * (g + 1)), slice(GROUP * g, GROUP * g + BAND)
                db_ref[rows, cols] += ds[g]
                dq_ref[rows, :] = dq[g]
                wk_ref[cols, :] += dkw[g]
                wv_ref[cols, :] += dvw[g]
            dk_ref[...] = ck_ref[...] + wk_ref[:Q_TILE, :]
            dv_ref[...] = (cv_ref[...] + wv_ref[:Q_TILE, :]).astype(BF16)
            ck_ref[...] = wk_ref[Q_TILE:, :]
            cv_ref[...] = wv_ref[Q_TILE:, :]

        @pl.when(i == nt)
        def _():
            dk_ref[...] = ck_ref[...]
            dv_ref[...] = cv_ref[...].astype(BF16)

    blk = (Q_TILE, HEAD_DIM)
    cur = pl.BlockSpec(blk, lambda h, i: (jnp.minimum(i, nt - 1), h))
    prev = pl.BlockSpec(blk, lambda h, i: (jnp.clip(i - 1, 0, nt - 1), h))
    lag = pl.BlockSpec(blk, lambda h, i: (jnp.maximum(i - 1, 0), h))
    vh = v_col0 // HEAD_DIM
    v_cur = pl.BlockSpec(blk, lambda h, i: (jnp.minimum(i, nt - 1), h + vh))
    v_prev = pl.BlockSpec(blk, lambda h, i: (jnp.clip(i - 1, 0, nt - 1), h + vh))
    tile = pl.BlockSpec((None, Q_TILE, WIN), lambda h, i: (h, 0, 0))
    width, out0, more, more_specs, aliases = _window(dest, inner, 7, 2)
    return pl.pallas_call(
        _skip_ref(body, 7, len(more)), name=name, grid=(nh, nt + 1),
        in_specs=[cur, prev, cur, v_prev, v_cur, cur, tile] + more_specs,
        out_specs=[cur, lag, pl.BlockSpec(blk, lambda h, i: (jnp.maximum(i - 1, 0), h + out0 // HEAD_DIM)), tile],
        out_shape=[jax.ShapeDtypeStruct((t, inner), F32)] * 2 + [jax.ShapeDtypeStruct((t, width), BF16),
                                                                 jax.ShapeDtypeStruct((nh, Q_TILE, WIN), F32)],
        scratch_shapes=[pltpu.VMEM(blk, F32), pltpu.VMEM(blk, F32),
                        pltpu.VMEM((WIN, HEAD_DIM), F32), pltpu.VMEM((WIN, HEAD_DIM), F32)],
        input_output_aliases=aliases, compiler_params=_cparams("arbitrary", "arbitrary"),
    )(q, k, k, v, v, do, bias, *more)


def _pad_rel_bias(rel_bias):
    nh = rel_bias.shape[0]
    return jnp.broadcast_to(jnp.pad(rel_bias, ((0, 0), (0, REL_PAD - N_REL)))[:, None, :], (nh, 8, REL_PAD))


def _layer_b_fwd(h1, nw, w_in_t, qw, kw, bias, w_out, target):
    t, d = h1.shape
    inner = w_out.shape[0]
    hn = _rms_fwd(h1, nw, "b_rms")
    proj = _mm(hn, w_in_t, "nt", t, 4 * inner, d, out_dtype=F32, name="b_proj")
    qn = _headnorm_fwd(proj, qw, 0, inner, "b_qnorm")
    kn = _headnorm_fwd(proj, kw, inner, inner, "b_knorm")
    o = _attn_fwd(qn, kn, proj, 2 * inner, bias, "b_attn")
    g = _gate_fwd(o, proj, 3 * inner, inner, "b_gate")
    loss_parts = _mm(g, w_out, "nn", t, d, inner, out_dtype=F32, name="b_out", res=h1, loss_target=target)
    return loss_parts, (hn, proj, qn, kn, o, g)


def _layer_b_bwd(dh2, dh2b, h1, nw, w_in_t, qw, kw, bias, w_out, saved):
    hn, proj, qn, kn, o, g = saved
    t, d = h1.shape
    inner = w_out.shape[0]
    dg = _mm(dh2b, w_out, "nt", t, inner, d, out_dtype=F32, name="b_dgate")
    dw_out = _mm(g, dh2b, "tn", inner, d, t, out_dtype=F32, name="b_dwout")
    do, dproj = _gate_bwd(dg, o, proj, 3 * inner, inner, "b_gate_bwd", BF16, dest=(None, 3 * inner, 4 * inner))
    dq, dk, dproj, dtile = _attn_bwd(qn, kn, proj, 2 * inner, do, bias, "b_attn_bwd",
                                     dest=(dproj, 2 * inner, 4 * inner))
    dproj, dqw = _headnorm_bwd(dq, proj, qw, 0, inner, "b_qnorm_bwd", dest=(dproj, 0, 4 * inner))
    dproj, dkw = _headnorm_bwd(dk, proj, kw, inner, inner, "b_knorm_bwd", dest=(dproj, inner, 4 * inner))
    dhn = _mm(dproj, w_in_t, "nn", t, d, 4 * inner, out_dtype=F32, name="b_dhn")
    dw_in_t = _mm(dproj, hn, "tn", 4 * inner, d, t, out_dtype=F32, name="b_dwin")
    dh1, dh1b, dnw = _rms_bwd(h1, nw, dhn, dh2, "b_rms_bwd")
    drb = _bias_grad(dtile, "b_bias_grad")[:, 0, :N_REL]
    return dh1, dh1b, dnw, dw_in_t, dqw, dkw, drb, dw_out


LANES = 128


def _softplus(x):
    return jnp.maximum(x, 0.0) + jnp.log1p(jnp.exp(-jnp.abs(x)))


def _gates_fwd(ab, alog_row, dt_row, nh, name, tr=1024):
    t = ab.shape[0]
    tr = min(tr, t)

    def body(x_ref, al_ref, dt_ref, o_ref):
        x = x_ref[...]
        lane = lax.broadcasted_iota(jnp.int32, x.shape, 1)
        g = -jnp.exp(al_ref[...]) * _softplus(x + dt_ref[...])
        o_ref[...] = jnp.where(lane < nh, g, jnp.where(lane < 2 * nh, _sigmoid(x), 0.0))

    row = pl.BlockSpec((tr, LANES), lambda i: (i, 0))
    vec = pl.BlockSpec((1, LANES), lambda i: (0, 0))
    return pl.pallas_call(
        body, name=name, grid=(t // tr,), in_specs=[row, vec, vec], out_specs=row,
        out_shape=jax.ShapeDtypeStruct((t, LANES), F32), compiler_params=_cparams("parallel"),
    )(ab, alog_row, dt_row)


def _gates_bwd(ab, alog_row, dt_row, dgates, nh, name, tr=1024):
    t = ab.shape[0]
    tr = min(tr, t)
    npart = dgates.shape[0]

    def body(x_ref, al_ref, dt_ref, dg_ref, dx_ref, s_ref):
        x = x_ref[...]
        lane = lax.broadcasted_iota(jnp.int32, x.shape, 1)
        dgt = dg_ref[0]
        for p in range(1, npart):
            dgt = dgt + dg_ref[p]
        ea = jnp.exp(al_ref[...])
        xa = x + dt_ref[...]
        da = jnp.where(lane < nh, dgt * (-ea) * _sigmoid(xa), 0.0)
        beta = _sigmoid(x)
        db = jnp.where((lane >= nh) & (lane < 2 * nh), dgt * beta * (1.0 - beta), 0.0)
        dx_ref[...] = (da + db).astype(BF16)
        dal = jnp.sum(jnp.where(lane < nh, dgt * (-ea) * _softplus(xa), 0.0), axis=0, keepdims=True)
        ddt = jnp.sum(da, axis=0, keepdims=True)
        r8 = lax.broadcasted_iota(jnp.int32, (8, LANES), 0)
        part = jnp.where(r8 == 0, dal, jnp.where(r8 == 1, ddt, 0.0))

        @pl.when(pl.program_id(0) == 0)
        def _():
            s_ref[...] = part

        @pl.when(pl.program_id(0) > 0)
        def _():
            s_ref[...] += part

    row = pl.BlockSpec((tr, LANES), lambda i: (i, 0))
    vec = pl.BlockSpec((1, LANES), lambda i: (0, 0))
    return pl.pallas_call(
        body, name=name, grid=(t // tr,),
        in_specs=[row, vec, vec, pl.BlockSpec((npart, tr, LANES), lambda i: (0, i, 0))],
        out_specs=[row, pl.BlockSpec((8, LANES), lambda i: (0, 0))],
        out_shape=[jax.ShapeDtypeStruct((t, LANES), BF16), jax.ShapeDtypeStruct((8, LANES), F32)],
        compiler_params=_cparams("arbitrary"),
    )(ab, alog_row, dt_row, dgates)


HALO = 8


def _conv_taps(ext, w, rows):
    acc = ext[HALO:HALO + rows] * w[CONV_K - 1:CONV_K]
    for s in range(1, CONV_K):
        acc = acc + pltpu.roll(ext, s, 0)[HALO:HALO + rows] * w[CONV_K - 1 - s:CONV_K - s]
    return acc


def _conv_fwd(proj, conv_w, col0, inner, mode, name, tt=CONV_ROWS, hb=CONV_HEADS):
    t = proj.shape[0]
    tt = min(tt, t)
    hb = min(hb, inner // HEAD_DIM)
    wc = hb * HEAD_DIM
    c0 = col0 // wc
    hpb = tt // HALO

    def body(x_ref, halo_ref, w_ref, o_ref):
        halo = jnp.where(pl.program_id(1) == 0, 0.0, halo_ref[...])
        s = _silu(_conv_taps(jnp.concatenate([halo, x_ref[...]], axis=0), w_ref[...], tt))
        if mode == "v":
            o_ref[...] = s
        else:
            mul = HEAD_DIM ** -0.5 if mode == "q" else 1.0
            o_ref[...] = jnp.concatenate(
                [sh * (lax.rsqrt(jnp.sum(sh * sh, axis=-1, keepdims=True) + EPS) * mul) for sh in _heads_of(s, hb)], axis=1)

    return pl.pallas_call(
        body, name=name, grid=(inner // wc, t // tt),
        in_specs=[pl.BlockSpec((tt, wc), lambda j, i: (i, j + c0)),
                  pl.BlockSpec((HALO, wc), lambda j, i: (jnp.maximum(i * hpb - 1, 0), j + c0)),
                  pl.BlockSpec((CONV_K, wc), lambda j, i: (0, j + c0))],
        out_specs=pl.BlockSpec((tt, wc), lambda j, i: (i, j)),
        out_shape=jax.ShapeDtypeStruct((t, inner), F32),
        compiler_params=_cparams("parallel", "parallel"),
    )(proj, proj, conv_w)


def _conv_bwd(dy, proj, conv_w, col0, inner, mode, name, tt=CONV_ROWS, hb=CONV_HEADS, dest=None):
    t = proj.shape[0]
    tt = min(tt, t)
    nt = t // tt
    hb = min(hb, inner // HEAD_DIM)
    wc = hb * HEAD_DIM
    c0 = col0 // wc
    hpb = tt // HALO
    rows = tt + HALO

    def body(dy_ref, dyn_ref, x_ref, xp_ref, xn_ref, w_ref, dx_ref, dw_ref):
        i = pl.program_id(1)
        w = w_ref[...]
        xprev = jnp.where(i == 0, 0.0, xp_ref[...])
        ext = jnp.concatenate([xprev, x_ref[...], xn_ref[...]], axis=0)
        c = _conv_taps(ext, w, rows)
        dyv = jnp.concatenate([dy_ref[...], jnp.where(i == nt - 1, 0.0, dyn_ref[...])], axis=0)
        sg = _sigmoid(c)
        s = c * sg
        if mode == "v":
            ds = dyv
        else:
            mul = HEAD_DIM ** -0.5 if mode == "q" else 1.0
            parts = []
            for dyh, sh in zip(_heads_of(dyv, hb), _heads_of(s, hb)):
                r = lax.rsqrt(jnp.sum(sh * sh, axis=-1, keepdims=True) + EPS)
                parts.append(mul * (r * dyh - sh * (r * r * r) * jnp.sum(dyh * sh, axis=-1, keepdims=True)))
            ds = jnp.concatenate(parts, axis=1)
        dc = ds * (sg * (1.0 + c * (1.0 - sg)))
        dx = dc[:tt] * w[CONV_K - 1:CONV_K]
        for sft in range(1, CONV_K):
            dx = dx + pltpu.roll(dc, rows - sft, 0)[:tt] * w[CONV_K - 1 - sft:CONV_K - sft]
        dx_ref[...] = dx.astype(BF16)
        r8 = lax.broadcasted_iota(jnp.int32, (8, wc), 0)
        part = jnp.zeros((8, wc), F32)
        for sft in range(CONV_K):
            xs = ext[HALO:HALO + tt] if sft == 0 else pltpu.roll(ext, sft, 0)[HALO:HALO + tt]
            part = part + jnp.where(r8 == CONV_K - 1 - sft, jnp.sum(dc[:tt] * xs, axis=0, keepdims=True), 0.0)

        @pl.when(i == 0)
        def _():
            dw_ref[...] = part

        @pl.when(i > 0)
        def _():
            dw_ref[...] += part

    cur = lambda off: pl.BlockSpec((tt, wc), lambda j, i: (i, j + off))
    nxt = lambda off: pl.BlockSpec((HALO, wc), lambda j, i: (jnp.minimum((i + 1) * hpb, t // HALO - 1), j + off))
    width, out0, more, more_specs, aliases = _window(dest, inner, 6, 0)
    return pl.pallas_call(
        _skip_ref(body, 6, len(more)), name=name, grid=(inner // wc, nt),
        in_specs=[cur(0), nxt(0), cur(c0),
                  pl.BlockSpec((HALO, wc), lambda j, i: (jnp.maximum(i * hpb - 1, 0), j + c0)), nxt(c0),
                  pl.BlockSpec((CONV_K, wc), lambda j, i: (0, j + c0))] + more_specs,
        out_specs=[pl.BlockSpec((tt, wc), lambda j, i: (i, j + out0 // wc)),
                   pl.BlockSpec((8, wc), lambda j, i: (0, j))],
        out_shape=[jax.ShapeDtypeStruct((t, width), BF16), jax.ShapeDtypeStruct((8, inner), F32)],
        input_output_aliases=aliases, compiler_params=_cparams("parallel", "arbitrary"),
    )(dy, dy, proj, proj, proj, conv_w, *more)


GDN_HB = 4
GDN_NB = 8
SCAN_HB = 16
SCAN_NB = 2


def _iota2(n, m):
    return lax.broadcasted_iota(jnp.int32, (n, m), 0), lax.broadcasted_iota(jnp.int32, (n, m), 1)


def _head_select(first_head, hb, lane0):
    r, lane = _iota2(8, LANES)
    return jnp.where((r < hb) & (lane == lane0 + first_head + r), 1.0, 0.0).astype(F32)


def _chunk_gates(gt, selg, selb):
    i, j = _iota2(CHUNK, CHUNK)
    gc_all = _dot_exact(jnp.where(j <= i, 1.0, 0.0), gt, NN, True)
    return (_dot_exact(gc_all, selg, NT, False), _dot_exact(selg, gc_all, NT, True),
            _dot_exact(gt, selb, NT, False))


def _decay_terms(gcol, grow):
    i, j = _iota2(CHUNK, CHUNK)
    glast = gcol[:, CHUNK - 1:CHUNK, :]
    decay = jnp.exp(jnp.where(j <= i, gcol - grow, NEG_BIG))
    return jnp.exp(gcol), jnp.exp(glast - gcol), jnp.exp(glast), decay


def _unit_lower_inverse(a):
    i, j = _iota2(CHUNK, CHUNK)
    same16 = (i // 16) == (j // 16)
    same32 = (i // 32) == (j // 32)
    m = jnp.where(same16, -a, 0.0)
    x = jnp.where(i == j, 1.0, 0.0) + m
    for _ in range(3):
        m = _dot3(m, m, BNN)
        x = x + _dot3(x, m, BNN)
    for off in (jnp.where(same32 & jnp.logical_not(same16), a, 0.0), jnp.where(same32, 0.0, a)):
        x = x - _dot3(_dot3(x, off, BNN), x, BNN)
    return x


def _unit_inputs(refs, g_ref, selg, selb, hb, nb):
    units = [(c, h) for c in range(nb) for h in range(hb)]
    rs = lambda c: slice(c * CHUNK, (c + 1) * CHUNK)
    cs = lambda h: slice(h * HEAD_DIM, (h + 1) * HEAD_DIM)
    gates = [_chunk_gates(g_ref[rs(c), :], selg, selb) for c in range(nb)]
    stacked = [jnp.stack([r[rs(c), cs(h)] for c, h in units]) for r in refs]
    gcol = jnp.stack([gates[c][0][:, h:h + 1] for c, h in units])
    grow = jnp.stack([gates[c][1][h:h + 1, :] for c, h in units])
    bcol = jnp.stack([gates[c][2][:, h:h + 1] for c, h in units])
    return units, rs, cs, stacked, gcol, grow, bcol


def _gdn_specs(nh, inner, t, heads=GDN_HB, chunks=GDN_NB):
    hb, nb = min(heads, nh), chunks
    rows = nb * CHUNK
    wide = pl.BlockSpec((rows, hb * HEAD_DIM), lambda g, n: (n, g))
    sq = pl.BlockSpec((hb, rows, CHUNK), lambda g, n: (g, n, 0))
    gts = pl.BlockSpec((rows, LANES), lambda g, n: (n, 0))
    glb = pl.BlockSpec((nb * 8, hb * HEAD_DIM), lambda g, n: (n, g))
    return hb, nb, rows, wide, sq, gts, glb


def _gdn_intra_fwd(q, k, v, gates, nh, name, comm=None):
    t, inner = q.shape
    hb, nb, rows, wide, sq, gts, glb = _gdn_specs(nh, inner, t)

    def body(q_ref, k_ref, v_ref, g_ref, qe_ref, kel_ref, wb_ref, w_ref, u_ref, qk_ref, tm_ref, gl_ref):
        first = pl.program_id(0) * hb
        selg, selb = _head_select(first, hb, 0), _head_select(first, hb, nh)
        i, j = _iota2(CHUNK, CHUNK)
        units, rs, cs, (qv, kv, vv), gcol, grow, bcol = _unit_inputs(
            (q_ref, k_ref, v_ref), g_ref, selg, selb, hb, nb)
        e, el, gl, decay = _decay_terms(gcol, grow)
        kb = kv * bcol
        qbf, kbf = _bf(qv), _bf(kv)
        a = jnp.where(j < i, _dot(_bf(kb), kbf, BNT) * decay, 0.0)
        tm = _unit_lower_inverse(a)
        uw = _dot3(tm, jnp.concatenate([vv * bcol, kb * e], axis=2), BNN)
        qk = _bf(_dot(qbf, kbf, BNT) * decay)
        qe, kel = _bf(qv * e), _bf(kv * el)
        for n, (c, h) in enumerate(units):
            w = uw[n, :, HEAD_DIM:]
            qe_ref[rs(c), cs(h)] = qe[n]
            kel_ref[rs(c), cs(h)] = kel[n]
            wb_ref[rs(c), cs(h)] = _bf(w)
            w_ref[rs(c), cs(h)] = w
            u_ref[rs(c), cs(h)] = uw[n, :, :HEAD_DIM]
            qk_ref[h, rs(c), :] = qk[n]
            tm_ref[h, rs(c), :] = tm[n]
            gl_ref[c * 8:(c + 1) * 8, cs(h)] = jnp.broadcast_to(gl[n], (8, HEAD_DIM))

    big = lambda dt: jax.ShapeDtypeStruct((t, inner), dt)
    return _grid_call(
        body, name=name, grid=(nh // hb, t // rows),
        in_specs=[wide, wide, wide, gts],
        out_specs=[wide] * 5 + [sq, sq, glb],
        out_shape=[big(BF16), big(BF16), big(BF16), big(F32), big(F32),
                   jax.ShapeDtypeStruct((nh, t, CHUNK), BF16), jax.ShapeDtypeStruct((nh, t, CHUNK), F32),
                   jax.ShapeDtypeStruct((t // CHUNK * 8, inner), F32)],
        args=(q, k, v, gates), semantics=("parallel", "parallel"), comm=comm)


def _gdn_scan_fwd(qe, kel, wb, u, qk, glb, nh, name):
    t, inner = u.shape
    hb, nb, rows, wide, sq, _, glb_spec = _gdn_specs(nh, inner, t, SCAN_HB, SCAN_NB)

    def body(qe_ref, kel_ref, wb_ref, u_ref, qk_ref, gl_ref, o_ref, vn_ref, sall_ref, s_ref):
        @pl.when(pl.program_id(1) == 0)
        def _():
            s_ref[...] = jnp.zeros(s_ref.shape, F32)

        cs = lambda h: slice(h * HEAD_DIM, (h + 1) * HEAD_DIM)
        for c in range(nb):
            rs = slice(c * CHUNK, (c + 1) * CHUNK)
            heads = lambda ref: jnp.stack([ref[rs, cs(h)] for h in range(hb)])
            s = s_ref[...]
            sall_ref[c] = s
            sb = _bf(s)
            vn = heads(u_ref) - _dot(heads(wb_ref), sb, BNN)
            vnb = _bf(vn)
            o = _dot(heads(qe_ref), sb, BNN) + _dot(qk_ref[:, rs, :], vnb, BNN)
            gl = jnp.stack([gl_ref[c * 8:c * 8 + 1, cs(h)] for h in range(hb)])
            s_ref[...] = s * gl + _dot(heads(kel_ref), vnb, BTN)
            for h in range(hb):
                vn_ref[rs, cs(h)] = vn[h]
                o_ref[rs, cs(h)] = o[h]

    return pl.pallas_call(
        body, name=name, grid=(nh // hb, t // rows),
        in_specs=[wide, wide, wide, wide, sq, glb_spec],
        out_specs=[wide, wide, pl.BlockSpec((nb, hb, HEAD_DIM, HEAD_DIM), lambda g, n: (n, g, 0, 0))],
        out_shape=[jax.ShapeDtypeStruct((t, inner), F32), jax.ShapeDtypeStruct((t, inner), F32),
                   jax.ShapeDtypeStruct((t // CHUNK, nh, HEAD_DIM, HEAD_DIM), F32)],
        scratch_shapes=[pltpu.VMEM((hb, HEAD_DIM, HEAD_DIM), F32)],
        compiler_params=_cparams("parallel", "arbitrary"),
    )(qe, kel, wb, u, qk, glb)


def _gdn_scan_bwd(do, qe, kel, wb, vn, qk, glb, sall, nh, name):
    t, inner = do.shape
    hb, nb, rows, _, _, _, _ = _gdn_specs(nh, inner, t, SCAN_HB, SCAN_NB)
    last = t // rows - 1
    wide = pl.BlockSpec((rows, hb * HEAD_DIM), lambda g, n: (last - n, g))
    sq = pl.BlockSpec((hb, rows, CHUNK), lambda g, n: (g, last - n, 0))
    glb_spec = pl.BlockSpec((nb * 8, hb * HEAD_DIM), lambda g, n: (last - n, g))

    def body(do_ref, qe_ref, kel_ref, wb_ref, vn_ref, qk_ref, gl_ref, sall_ref,
             dvn_ref, dw_ref, dqe_ref, dkel_ref, dqk_ref, dgl_ref, ds_ref):
        @pl.when(pl.program_id(1) == 0)
        def _():
            ds_ref[...] = jnp.zeros(ds_ref.shape, F32)

        cs = lambda h: slice(h * HEAD_DIM, (h + 1) * HEAD_DIM)
        for c in reversed(range(nb)):
            rs = slice(c * CHUNK, (c + 1) * CHUNK)
            heads = lambda ref: jnp.stack([ref[rs, cs(h)] for h in range(hb)])
            ds, s = ds_ref[...], sall_ref[c]
            dsb, sb = _bf(ds), _bf(s)
            dob, vnb = _bf(heads(do_ref)), _bf(heads(vn_ref))
            dvn = _dot(qk_ref[:, rs, :], dob, BTN) + _dot(heads(kel_ref), dsb, BNN)
            dvnb = _bf(dvn)
            dw = -_dot(dvnb, sb, BNT)
            dqe = _dot(dob, sb, BNT)
            dkel = _dot(vnb, dsb, BNT)
            dqk_ref[:, rs, :] = _dot(dob, vnb, BNT)
            dgl = jnp.sum(jnp.sum(ds * s, axis=2, keepdims=True), axis=1, keepdims=True)
            gl = jnp.stack([gl_ref[c * 8:c * 8 + 1, cs(h)] for h in range(hb)])
            ds_ref[...] = ds * gl + _dot(heads(qe_ref), dob, BTN) - _dot(heads(wb_ref), dvnb, BTN)
            for h in range(hb):
                dvn_ref[rs, cs(h)] = dvn[h]
                dw_ref[rs, cs(h)] = dw[h]
                dqe_ref[rs, cs(h)] = dqe[h]
                dkel_ref[rs, cs(h)] = dkel[h]
                dgl_ref[c * 8:(c + 1) * 8, cs(h)] = jnp.broadcast_to(dgl[h], (8, HEAD_DIM))

    big = jax.ShapeDtypeStruct((t, inner), F32)
    return pl.pallas_call(
        body, name=name, grid=(nh // hb, t // rows),
        in_specs=[wide, wide, wide, wide, wide, sq, glb_spec,
                  pl.BlockSpec((nb, hb, HEAD_DIM, HEAD_DIM), lambda g, n: (last - n, g, 0, 0))],
        out_specs=[wide] * 4 + [sq, glb_spec],
        out_shape=[big] * 4 + [jax.ShapeDtypeStruct((nh, t, CHUNK), F32),
                               jax.ShapeDtypeStruct((t // CHUNK * 8, inner), F32)],
        scratch_shapes=[pltpu.VMEM((hb, HEAD_DIM, HEAD_DIM), F32)],
        compiler_params=_cparams("parallel", "arbitrary"),
    )(do, qe, kel, wb, vn, qk, glb, sall)


def _gdn_intra_bwd(q, k, v, gates, tm, w, u, dvn, dw, dqe, dkel, dqk, dglb, nh, name, comm=None):
    t, inner = q.shape
    hb, nb, rows, wide, sq, gts, glb = _gdn_specs(nh, inner, t)

    def body(q_ref, k_ref, v_ref, g_ref, tm_ref, w_ref, u_ref, dvn_ref, dw_ref, dqe_ref, dkel_ref, dqk_ref,
             dgl_ref, dq_ref, dk_ref, dv_ref, dg_ref):
        first = pl.program_id(0) * hb
        selg, selb = _head_select(first, hb, 0), _head_select(first, hb, nh)
        i, j = _iota2(CHUNK, CHUNK)
        lane8 = lax.broadcasted_iota(jnp.int32, (CHUNK, 8), 1)
        row = lax.broadcasted_iota(jnp.int32, (CHUNK, 1), 0)
        lower = jnp.where(j <= i, 1.0, 0.0).astype(F32)
        rsum = lambda x: jnp.sum(x, axis=-1, keepdims=True)
        units, rs, cs, (qv, kv, vv, wv, uv, dvn, dw, dqe, dkel), gcol, grow, bcol = _unit_inputs(
            (q_ref, k_ref, v_ref, w_ref, u_ref, dvn_ref, dw_ref, dqe_ref, dkel_ref), g_ref, selg, selb, hb, nb)
        nu = len(units)
        tmv = jnp.stack([tm_ref[h, rs(c), :] for c, h in units])
        dqk = jnp.where(j <= i, jnp.stack([dqk_ref[h, rs(c), :] for c, h in units]), 0.0)
        dgl = jnp.stack([dgl_ref[c * 8:c * 8 + 1, h * HEAD_DIM:h * HEAD_DIM + 1] for c, h in units])
        e, el, gl, decay = _decay_terms(gcol, grow)
        kb = kv * bcol
        qb, kbf, kbb = _bf(qv), _bf(kv), _bf(kb)
        dqkr = _bf(dqk * decay)
        dq = dqe * e + _dot(dqkr, kbf, BNN)
        dk = dkel * el + _dot(dqkr, qb, BTN)
        de = rsum(dqe * qv)
        del_ = rsum(dkel * kv)
        mq = dqk * _dot(qb, kbf, BNT) * decay
        dsol = _dot3(tmv, jnp.concatenate([dvn, dw], axis=2), BTN)
        dvb, dkbe = dsol[:, :, :HEAD_DIM], dsol[:, :, HEAD_DIM:]
        da = -jnp.where(j < i, _dot3(dsol, jnp.concatenate([uv, wv], axis=2), BNT), 0.0)
        dkk = _bf(da * decay)
        ma = da * _dot(kbb, kbf, BNT) * decay
        dkb = dkbe * e + _dot(dkk, kbf, BNN)
        de = de + rsum(dkbe * kb)
        dk = dk + _dot(dkk, kbb, BTN) + dkb * bcol
        dv = dvb * bcol
        dbeta = rsum(dkb * kv) + rsum(dvb * vv)
        m = mq + ma
        ones = jnp.ones((nu, CHUNK, LANES), F32)
        dgc = rsum(m) - _dot_exact(m, ones, BTN, False)[:, :, 0:1] + de * e - del_ * el
        tail = jnp.sum(del_ * el, axis=1, keepdims=True) + dgl * gl
        dgc = dgc + jnp.where(row == CHUNK - 1, tail, 0.0)
        for n, (c, h) in enumerate(units):
            dq_ref[rs(c), cs(h)] = dq[n]
            dk_ref[rs(c), cs(h)] = dk[n]
            dv_ref[rs(c), cs(h)] = dv[n]
        for c in range(nb):
            dgc_cols = jnp.zeros((CHUNK, 8), F32)
            dbeta_cols = jnp.zeros((CHUNK, 8), F32)
            for h in range(hb):
                dgc_cols = jnp.where(lane8 == h, dgc[c * hb + h], dgc_cols)
                dbeta_cols = jnp.where(lane8 == h, dbeta[c * hb + h], dbeta_cols)
            dg_cols = _dot_exact(lower, dgc_cols, TN, True)
            dg_ref[rs(c), :] = _dot_exact(dg_cols, selg, NN, False) + _dot_exact(dbeta_cols, selb, NN, False)

    big = jax.ShapeDtypeStruct((t, inner), F32)
    return _grid_call(
        body, name=name, grid=(nh // hb, t // rows),
        in_specs=[wide, wide, wide, gts, sq, wide, wide, wide, wide, wide, wide, sq, glb],
        out_specs=[wide, wide, wide, pl.BlockSpec((None, rows, LANES), lambda g, n: (g, n, 0))],
        out_shape=[big, big, big, jax.ShapeDtypeStruct((nh // hb, t, LANES), F32)],
        args=(q, k, v, gates, tm, w, u, dvn, dw, dqe, dkel, dqk, dglb), semantics=("parallel", "parallel"),
        comm=comm)


def _layer_a_fwd(x, hn, w_in_t, w_ab_t, conv_w, alog_row, dt_row, onw, nh, comm, w_out_of):
    t, d = x.shape
    inner = nh * HEAD_DIM
    proj = _mm(hn, w_in_t, "nt", t, 4 * inner, d, out_dtype=F32, name="a_proj")
    ab = _mm(hn, w_ab_t, "nt", t, LANES, d, out_dtype=F32, name="a_proj_ab")
    gates = _gates_fwd(ab, alog_row, dt_row, nh, "a_gates")
    q = _conv_fwd(proj, conv_w, 0, inner, "q", "a_conv_q")
    k = _conv_fwd(proj, conv_w, inner, inner, "k", "a_conv_k")
    v = _conv_fwd(proj, conv_w, 2 * inner, inner, "v", "a_conv_v")
    qe, kel, wb, w, u, qk, tm, glb, *carried = _gdn_intra_fwd(q, k, v, gates, nh, "a_intra", comm)
    o, vn, sall = _gdn_scan_fwd(qe, kel, wb, u, qk, glb, nh, "a_scan")
    g = _gate_fwd(o, proj, 3 * inner, inner, "a_gate", norm_w=onw)
    w_out = w_out_of(carried)
    h1 = _mm(g, w_out, "nn", t, d, inner, out_dtype=F32, name="a_out", res=x)
    return h1, (hn, proj, ab, gates, q, k, v, qe, kel, wb, w, u, qk, tm, glb, o, vn, sall, g), w_out, carried


def _layer_a_bwd(dh1, dh1b, x, nw, w_in_t, w_ab_t, conv_w, alog_row, dt_row, onw, w_out, nh, saved, comm_of,
                 own_comm):
    hn, proj, ab, gates, q, k, v, qe, kel, wb, w, u, qk, tm, glb, o, vn, sall, g = saved
    t, d = x.shape
    inner = w_out.shape[0]
    dg = _mm(dh1b, w_out, "nt", t, inner, d, out_dtype=F32, name="a_dgate")
    dw_out = _mm(g, dh1b, "tn", inner, d, t, out_dtype=F32, name="a_dwout")
    comm = comm_of(dw_out)
    do, dproj, donw = _gate_bwd(dg, o, proj, 3 * inner, inner, "a_gate_bwd", F32, norm_w=onw,
                                dest=(None, 3 * inner, 4 * inner))
    dvn, dw, dqe, dkel, dqk, dglb = _gdn_scan_bwd(do, qe, kel, wb, vn, qk, glb, sall, nh, "a_scan_bwd")
    dq, dk, dv, dgates, *carried = _gdn_intra_bwd(q, k, v, gates, tm, w, u, dvn, dw, dqe, dkel, dqk, dglb, nh,
                                                  "a_intra_bwd", comm)
    dproj, dcq = _conv_bwd(dq, proj, conv_w, 0, inner, "q", "a_conv_q_bwd", dest=(dproj, 0, 4 * inner))
    dproj, dck = _conv_bwd(dk, proj, conv_w, inner, inner, "k", "a_conv_k_bwd", dest=(dproj, inner, 4 * inner))
    dproj, dcv = _conv_bwd(dv, proj, conv_w, 2 * inner, inner, "v", "a_conv_v_bwd",
                           dest=(dproj, 2 * inner, 4 * inner))
    dab, dsmall = _gates_bwd(ab, alog_row, dt_row, dgates, nh, "a_gates_bwd")
    dw_in_t = _mm(dproj, hn, "tn", 4 * inner, d, t, out_dtype=F32, name="a_dwin")
    dw_ab_t = _mm(dab, hn, "tn", LANES, d, t, out_dtype=F32, name="a_dwin_ab")
    dconv = jnp.concatenate([dcq[:CONV_K], dck[:CONV_K], dcv[:CONV_K]], axis=1)
    dhn = _mm(dab, w_ab_t, "nn", t, d, LANES, out_dtype=F32, name="a_dhn_ab")
    own = own_comm(dw_in_t, dw_ab_t, dconv)
    dhn = _mm(dproj, w_in_t, "nn", t, d, 4 * inner, out_dtype=F32, name="a_dhn", res=dhn, comm=own)
    dhn, carried_own = dhn if own is not None else (dhn, [])
    dx, _, dnw = _rms_bwd(x, nw, dhn, dh1, "a_rms_bwd")
    return dx, dnw, dsmall, donw, carried, carried_own


def _rows_of(a, rows):
    flat = a.reshape(-1)
    return jnp.pad(flat, (0, rows * LANES - flat.shape[0])).reshape(rows, LANES)


def _to_slabs(g, axis):
    shape = g.shape[:axis] + (N_DEV, g.shape[axis] // N_DEV) + g.shape[axis + 1:]
    return jnp.moveaxis(g.reshape(shape), axis, 0)


def _from_slabs(s, axis):
    m = jnp.moveaxis(s, 0, axis)
    return m.reshape(m.shape[:axis] + (m.shape[axis] * m.shape[axis + 1],) + m.shape[axis + 2:])


def kernel(x, norm_w, a_w_in, a_conv_w, a_a_log, a_dt_bias, a_out_norm_w, a_w_out, b_w_in, b_q_norm_w, b_k_norm_w, b_rel_bias, b_w_out, loss_target, m_norm_w, m_a_w_in, m_a_conv_w, m_a_a_log, m_a_dt_bias, m_a_out_norm_w, m_a_w_out, m_b_w_in, m_b_q_norm_w, m_b_k_norm_w, m_b_rel_bias, m_b_w_out, v_norm_w, v_a_w_in, v_a_conv_w, v_a_a_log, v_a_dt_bias, v_a_out_norm_w, v_a_w_out, v_b_w_in, v_b_q_norm_w, v_b_k_norm_w, v_b_rel_bias, v_b_w_out):
    xs, target = x[0], loss_target[0]
    nh = a_a_log.shape[-1]
    inner = N_DEV * a_w_out.shape[1]

    d = xs.shape[1]
    nw0, nw1 = norm_w[0:1], norm_w[1:2]
    hn0, (ga_in, g_conv) = _rms_fwd(
        xs, nw0, "a_rms", comm=_RoutedGather([a_w_in[0].T.astype(BF16), a_conv_w[0]]))
    wa_in_t = ga_in.reshape(-1, d)
    wa_ab_t = jnp.pad(wa_in_t[4 * inner:], ((0, LANES - 2 * nh), (0, 0)))
    conv_w = _from_slabs(g_conv, 1)
    alog_row = jnp.pad(a_a_log, ((0, 0), (0, LANES - nh)))
    dt_row = jnp.pad(a_dt_bias, ((0, 0), (0, LANES - nh)))

    h1, saved_a, wa_out, (gb_in, gb_out, _) = _layer_a_fwd(
        xs, hn0, wa_in_t, wa_ab_t, conv_w, alog_row, dt_row, a_out_norm_w, nh,
        _Comm("gather", [b_w_in[0].T.astype(BF16), b_w_out[0].astype(BF16), a_w_out[0].astype(BF16)]),
        lambda gathered: _from_slabs(gathered[2], 0))
    wb_in_t = gb_in.reshape(-1, d)
    wb_out = _from_slabs(gb_out, 0)
    bias = _bias_tiles(_pad_rel_bias(b_rel_bias[0]), "b_bias_tiles")
    (dh2, dh2b, loss_row), saved_b = _layer_b_fwd(h1, nw1, wb_in_t, b_q_norm_w, b_k_norm_w, bias, wb_out, target)

    dh1, dh1b, dnw1, dwb_in_t, dqw, dkw, drb, dwb_out = _layer_b_bwd(
        dh2, dh2b, h1, nw1, wb_in_t, b_q_norm_w, b_k_norm_w, bias, wb_out, saved_b)

    def exchange_early(dwa_out):
        return _Comm("exchange", [dwb_in_t.reshape(N_DEV, -1, d).astype(BF16), _to_slabs(dwb_out, 0).astype(BF16),
                                  _to_slabs(dwa_out, 0).astype(BF16)])

    def exchange_last(dwa_in_t, dwa_ab_t, dconv):
        full = jnp.concatenate([dwa_in_t, dwa_ab_t[:2 * nh]], axis=0)
        return _Comm("exchange", [full.reshape(N_DEV, -1, d).astype(BF16), _to_slabs(dconv, 1)])

    dx, dnw0, dsmall, donw, (pb_in, pb_out, pa_out), (pa_in, p_conv) = _layer_a_bwd(
        dh1, dh1b, xs, nw0, wa_in_t, wa_ab_t, conv_w, alog_row, dt_row, a_out_norm_w, wa_out, nh, saved_a,
        exchange_early, exchange_last)
    big = {}
    for name, p, w, m, v in (("a_w_in", pa_in, a_w_in, m_a_w_in, v_a_w_in),
                             ("a_w_out", pa_out, a_w_out, m_a_w_out, v_a_w_out),
                             ("b_w_in", pb_in, b_w_in, m_b_w_in, v_b_w_in),
                             ("b_w_out", pb_out, b_w_out, m_b_w_out, v_b_w_out),
                             ("a_conv_w", p_conv, a_conv_w, m_a_conv_w, v_a_conv_w)):
        big[name] = [o[None] for o in _adamw(p, w[0], m[0], v[0], "adamw_" + name,
                                             transposed=name in ("a_w_in", "b_w_in"))]

    small = (("norm_w", norm_w, m_norm_w, v_norm_w, jnp.concatenate([dnw0, dnw1], axis=0)),
             ("a_a_log", a_a_log, m_a_a_log, v_a_a_log, dsmall[0:1, :nh]),
             ("a_dt_bias", a_dt_bias, m_a_dt_bias, v_a_dt_bias, dsmall[1:2, :nh]),
             ("a_out_norm_w", a_out_norm_w, m_a_out_norm_w, v_a_out_norm_w, donw),
             ("b_q_norm_w", b_q_norm_w, m_b_q_norm_w, v_b_q_norm_w, dqw),
             ("b_k_norm_w", b_k_norm_w, m_b_k_norm_w, v_b_k_norm_w, dkw),
             ("b_rel_bias", b_rel_bias, m_b_rel_bias, v_b_rel_bias, drb))
    rows = [8 * (-(-w.size // (8 * LANES))) for _, w, _, _, _ in small]
    pack = lambda arrs: jnp.concatenate([_rows_of(a, r) for a, r in zip(arrs, rows)] + [jnp.zeros((8, LANES), F32)], axis=0)
    g_pack = jnp.concatenate([_rows_of(g, r) for (_, _, _, _, g), r in zip(small, rows)]
                             + [jnp.broadcast_to(loss_row, (8, LANES))], axis=0)
    (g_all,) = _comm_call(_Comm("gather", [g_pack]), "gather_small_grads")
    outs_small = _adamw(g_all, pack([s[1] for s in small]), pack([s[2] for s in small]),
                        pack([s[3] for s in small]), "adamw_small")
    start = 0
    for (name, w, _, _, _), r in zip(small, rows):
        big[name] = [o[start:start + r].reshape(-1)[:w.size].reshape(w.shape) for o in outs_small]
        start += r
    loss = outs_small[0][start, 0]

    order = ("norm_w", "a_w_in", "a_conv_w", "a_a_log", "a_dt_bias", "a_out_norm_w", "a_w_out", "b_w_in",
             "b_q_norm_w", "b_k_norm_w", "b_rel_bias", "b_w_out")
    return (loss, dx[None]) + tuple(big[n][i] for i in range(4) for n in order)
```

```python
import functools
import math

import jax
import jax.numpy as jnp
from jax import lax
from jax.experimental import pallas as pl
from jax.experimental.pallas import tpu as pltpu

F32 = jnp.float32
BF16 = jnp.bfloat16
MESH_IDS = pl.DeviceIdType.MESH
N_DEV = 8
CHUNK = 64
HEAD_DIM = 128
EPS = 1e-6
CONV_K = 4
LEFT_CHUNKS = 8
REL_CLIP = 256
Q_TILE = LEFT_CHUNKS * CHUNK
ADAM_LR = 0.001
ADAM_B1 = 0.9
ADAM_B2 = 0.999
ADAM_EPS = 1e-08
ADAM_WD = 0.01
ADAM_STEP = 10
NEG_BIG = -1e30
VMEM_LIMIT_BYTES = 56 * 1024 * 1024
HIGHEST = lax.Precision.HIGHEST
ANY = pl.BlockSpec(memory_space=pl.ANY)


def _cparams(*sem):
    return pltpu.CompilerParams(dimension_semantics=tuple(sem), vmem_limit_bytes=VMEM_LIMIT_BYTES)


NN, NT, TN = (((1,), (0,)), ((), ())), (((1,), (1,)), ((), ())), (((0,), (0,)), ((), ()))
BNN, BNT, BTN = (((2,), (1,)), ((0,), (0,))), (((2,), (2,)), ((0,), (0,))), (((1,), (1,)), ((0,), (0,)))


def _dot(a, b, dims, precision=None):
    return lax.dot_general(a, b, dims, preferred_element_type=F32, precision=precision)


def _nn(a, b, precision=None):
    return _dot(a, b, NN, precision)


def _nt(a, b, precision=None):
    return _dot(a, b, NT, precision)


def _tn(a, b, precision=None):
    return _dot(a, b, TN, precision)


def _bf(x):
    return x.astype(BF16)


def _split(x, pieces=2):
    out = []
    for _ in range(pieces - 1):
        hi = x.astype(BF16)
        out.append(hi)
        x = x - hi.astype(F32)
    return out + [x.astype(BF16)]


def _dot3(a, b, dims):
    (ah, al), (bh, bl) = _split(a), _split(b)
    return _dot(ah, bh, dims) + (_dot(ah, bl, dims) + _dot(al, bh, dims))


def _dot_exact(a, b, dims, split_b):
    if split_b:
        a = a.astype(BF16)
        parts = [_dot(a, p, dims) for p in _split(b, 3)]
    else:
        b = b.astype(BF16)
        parts = [_dot(p, b, dims) for p in _split(a, 3)]
    return parts[0] + (parts[1] + parts[2])


def _sigmoid(x):
    return 0.5 * jnp.tanh(0.5 * x) + 0.5


def _silu(x):
    return x * _sigmoid(x)


def _dsilu(x):
    s = _sigmoid(x)
    return s * (1.0 + x * (1.0 - s))


def _my_pos():
    return lax.axis_index("x"), lax.axis_index("y"), lax.axis_index("c")


def _peers(x, y, c):
    def flip(v, f):
        return 1 - v if f else v

    return [(flip(x, kx), flip(y, ky), flip(c, kc)) for kx in (0, 1) for ky in (0, 1) for kc in (0, 1)][1:]


def _lin(p):
    return 4 * p[0] + 2 * p[1] + p[2]


class _Comm:
    def __init__(self, kind, arrays):
        self.kind, self.arrays, self.n = kind, list(arrays), len(arrays)

    def out_shape(self):
        lead = (N_DEV,) if self.kind == "gather" else ()
        return [jax.ShapeDtypeStruct(lead + a.shape, a.dtype) for a in self.arrays]

    def scratch(self):
        return [pltpu.SemaphoreType.DMA((7 * self.n,)), pltpu.SemaphoreType.DMA((7 * self.n,)),
                pltpu.SemaphoreType.DMA((self.n,))]

    def _copies(self, ins, outs, sems, arrivals):
        send_sems, recv_sems, local_sems = sems
        x, y, c = _my_pos()
        me = _lin((x, y, c))
        gather = self.kind == "gather"
        mine = [ins[t] if gather else ins[t].at[me] for t in range(self.n)]
        remote = []
        for k, peer in enumerate(_peers(x, y, c)):
            for t in range(self.n):
                if arrivals:
                    src, dst = mine[t], outs[t].at[_lin(peer)]
                else:
                    src, dst = (ins[t] if gather else ins[t].at[_lin(peer)]), outs[t].at[me]
                remote.append(pltpu.make_async_remote_copy(
                    src_ref=src, dst_ref=dst, send_sem=send_sems.at[k * self.n + t],
                    recv_sem=recv_sems.at[k * self.n + t], device_id=peer, device_id_type=MESH_IDS))
        if arrivals:
            return remote
        return [pltpu.make_async_copy(mine[t], outs[t].at[me], local_sems.at[t]) for t in range(self.n)], remote

    def start(self, ins, outs, sems):
        local, sends = self._copies(ins, outs, sems, False)
        for cp in local + sends:
            cp.start()

    def finish(self, ins, outs, sems):
        for cp in self._copies(ins, outs, sems, True):
            cp.wait_recv()
        local, sends = self._copies(ins, outs, sems, False)
        for cp in sends:
            cp.wait_send()
        for cp in local:
            cp.wait()


def _xor(a, b):
    return a + b - 2 * a * b


class _RoutedGather(_Comm):
    def __init__(self, arrays):
        super().__init__("gather", arrays)

    def _plan(self, outs, sems):
        send_sems, recv_sems, _ = sems
        x, y, c = _my_pos()
        sib, xn, yn, dg = (x, y, 1 - c), (1 - x, y, c), (x, 1 - y, c), (1 - x, 1 - y, c)
        via = (_xor(x, 1 - c), _xor(y, c), c)
        onto = (_xor(x, c), _xor(y, 1 - c), c)
        routes = [(None, sib, sib), (None, xn, xn), (None, yn, yn), (via, onto, dg),
                  (xn, sib, (1 - x, y, 1 - c)), (yn, sib, (x, 1 - y, 1 - c)), (dg, sib, (1 - x, 1 - y, 1 - c))]

        def copy(k, t, src, slot, target):
            return pltpu.make_async_remote_copy(
                src_ref=src, dst_ref=outs[t].at[slot], send_sem=send_sems.at[k * self.n + t],
                recv_sem=recv_sems.at[k * self.n + t], device_id=target, device_id_type=MESH_IDS)

        return (x, y, c), routes, copy

    def start(self, ins, outs, sems):
        me, routes, copy = self._plan(outs, sems)
        for t in range(self.n):
            pltpu.make_async_copy(ins[t], outs[t].at[_lin(me)], sems[2].at[t]).start()
            for k in range(3):
                copy(k, t, ins[t], _lin(me), routes[k][1]).start()

    def finish(self, ins, outs, sems):
        me, routes, copy = self._plan(outs, sems)

        def arrived(k):
            for t in range(self.n):
                copy(k, t, ins[t], _lin(routes[k][2]), me).wait_recv()

        def pass_on(k):
            for t in range(self.n):
                copy(k, t, outs[t].at[_lin(routes[k][0])], _lin(routes[k][0]), routes[k][1]).start()

        arrived(1)
        arrived(2)
        for k in (3, 4, 5):
            pass_on(k)
        arrived(3)
        pass_on(6)
        for k in (0, 4, 5, 6):
            arrived(k)
        for t in range(self.n):
            for k in range(7):
                src = ins[t] if k < 3 else outs[t].at[_lin(routes[k][0])]
                copy(k, t, src, _lin(me), routes[k][1]).wait_send()
            pltpu.make_async_copy(ins[t], outs[t].at[_lin(me)], sems[2].at[t]).wait()


def _comm_call(comm, name):
    n = comm.n

    def body(*refs):
        ins, outs, sems = refs[:n], refs[n:2 * n], refs[2 * n:]
        comm.start(ins, outs, sems)
        comm.finish(ins, outs, sems)

    return pl.pallas_call(
        body, name=name, out_shape=comm.out_shape(), in_specs=[ANY] * n, out_specs=[ANY] * n,
        scratch_shapes=comm.scratch(),
    )(*comm.arrays)


def _grid_call(body, *, name, grid, in_specs, out_specs, out_shape, args, scratch_shapes=(), semantics=None, comm=None):
    if comm is None:
        return pl.pallas_call(
            body, name=name, grid=grid, in_specs=in_specs, out_specs=out_specs, out_shape=out_shape,
            scratch_shapes=list(scratch_shapes), compiler_params=_cparams(*semantics),
        )(*args)
    n_in, n_out, n_sc, n = len(in_specs), len(out_specs), len(scratch_shapes), comm.n

    def full(*refs):
        ins, refs = refs[:n_in], refs[n_in:]
        cins, refs = refs[:n], refs[n:]
        outs, refs = refs[:n_out], refs[n_out:]
        couts, refs = refs[:n], refs[n:]
        scratch, sems = refs[:n_sc], refs[n_sc:]
        ids = [pl.program_id(a) for a in range(len(grid))]
        first = functools.reduce(jnp.logical_and, [i == 0 for i in ids])
        last = functools.reduce(jnp.logical_and, [i == g - 1 for i, g in zip(ids, grid)])

        @pl.when(first)
        def _():
            comm.start(cins, couts, sems)

        body(*ins, *outs, *scratch)

        @pl.when(last)
        def _():
            comm.finish(cins, couts, sems)

    return pl.pallas_call(
        full, name=name, grid=grid, in_specs=list(in_specs) + [ANY] * n, out_specs=list(out_specs) + [ANY] * n,
        out_shape=list(out_shape) + comm.out_shape(), scratch_shapes=list(scratch_shapes) + comm.scratch(),
        compiler_params=_cparams(*(["arbitrary"] * len(grid))),
    )(*(list(args) + comm.arrays))


def _mm(a, b, mode, m, n, k, *, out_dtype, name, tm=1024, tn=1024, tk=2048,
        a_m0=0, a_k0=0, b_n0=0, b_k0=0, res=None, comm=None, loss_target=None):
    tm, tn, tk = min(tm, m), min(tn, n), min(tk, k)
    nm, nn, nk = m // tm, n // tn, k // tk
    assert nm * tm == m and nn * tn == n and nk * tk == k
    am, ak, bn, bk = a_m0 // tm, a_k0 // tk, b_n0 // tn, b_k0 // tk
    assert am * tm == a_m0 and ak * tk == a_k0 and bn * tn == b_n0 and bk * tk == b_k0
    if mode == "tn":
        a_spec = pl.BlockSpec((tk, tm), lambda i, j, q: (q + ak, i + am))
        a_dims = (0,)
    else:
        a_spec = pl.BlockSpec((tm, tk), lambda i, j, q: (i + am, q + ak))
        a_dims = (1,)
    if mode == "nt":
        b_spec = pl.BlockSpec((tn, tk), lambda i, j, q: (j + bn, q + bk))
        b_dims = (1,)
    else:
        b_spec = pl.BlockSpec((tk, tn), lambda i, j, q: (q + bk, j + bn))
        b_dims = (0,)
    o_spec = pl.BlockSpec((tm, tn), lambda i, j, q: (i, j))
    has_res = res is not None
    has_loss = loss_target is not None
    n_in = 2 + has_res + has_loss
    n_out = 3 if has_loss else 1

    def body(*refs):
        a_ref, b_ref = refs[0], refs[1]
        res_ref = refs[2] if has_res else None
        o_ref = refs[n_in]
        p = _dot(a_ref[...], b_ref[...], ((a_dims, b_dims), ((), ())))

        def finish(total):
            if has_res:
                total = total + res_ref[...].astype(F32)
            if not has_loss:
                o_ref[...] = total.astype(out_dtype)
                return
            err = total - refs[n_in - 1][...]
            grad = err * (1.0 / n)
            o_ref[...] = grad
            refs[n_in + 1][...] = grad.astype(BF16)
            l_ref = refs[n_in + 2]
            part = jnp.zeros((1, LANES), F32) + 0.5 * jnp.sum(err * err) * (1.0 / n)
            first = (pl.program_id(0) == 0) & (pl.program_id(1) == 0)

            @pl.when(first)
            def _():
                l_ref[...] = part

            @pl.when(jnp.logical_not(first))
            def _():
                l_ref[...] += part

        if nk == 1:
            finish(p)
        else:
            acc_ref = refs[n_in + n_out]
            q = pl.program_id(2)

            @pl.when(q == 0)
            def _():
                acc_ref[...] = p

            @pl.when(q > 0)
            def _():
                acc_ref[...] += p

            @pl.when(q == nk - 1)
            def _():
                finish(acc_ref[...])

    extra_in = ([res] if has_res else []) + ([loss_target] if has_loss else [])
    if has_loss:
        return _grid_call(
            body, name=name, grid=(nm, nn, nk), in_specs=[a_spec, b_spec] + [o_spec] * len(extra_in),
            out_specs=[o_spec, o_spec, pl.BlockSpec((1, LANES), lambda i, j, q: (0, 0))],
            out_shape=[jax.ShapeDtypeStruct((m, n), F32), jax.ShapeDtypeStruct((m, n), BF16),
                       jax.ShapeDtypeStruct((1, LANES), F32)],
            scratch_shapes=[pltpu.VMEM((tm, tn), F32)] if nk > 1 else [],
            args=[a, b] + extra_in, semantics=("arbitrary", "arbitrary", "arbitrary"))
    out, *carried = _grid_call(
        body, name=name, grid=(nm, nn, nk),
        in_specs=[a_spec, b_spec] + [o_spec] * len(extra_in),
        out_specs=[o_spec], out_shape=[jax.ShapeDtypeStruct((m, n), out_dtype)],
        scratch_shapes=[pltpu.VMEM((tm, tn), F32)] if nk > 1 else [],
        args=[a, b] + extra_in, semantics=("parallel", "parallel", "arbitrary"), comm=comm)
    return out if comm is None else (out, carried)


def _rms_fwd(x, w, name, tr=512, comm=None):
    t, d = x.shape
    tr = min(tr, t)

    def body(x_ref, w_ref, o_ref):
        xv = x_ref[...]
        r = lax.rsqrt(jnp.mean(xv * xv, axis=-1, keepdims=True) + EPS)
        o_ref[...] = (xv * r * w_ref[...]).astype(BF16)

    out, *carried = _grid_call(
        body, name=name, grid=(t // tr,),
        in_specs=[pl.BlockSpec((tr, d), lambda i: (i, 0)), pl.BlockSpec((1, d), lambda i: (0, 0))],
        out_specs=[pl.BlockSpec((tr, d), lambda i: (i, 0))],
        out_shape=[jax.ShapeDtypeStruct((t, d), BF16)], args=(x, w), semantics=("parallel",), comm=comm)
    return out if comm is None else (out, carried)


def _rms_bwd(x, w, dy, dres, name, tr=256):
    t, d = x.shape
    tr = min(tr, t)

    def body(x_ref, w_ref, dy_ref, dres_ref, dx_ref, dxb_ref, dw_ref):
        xv = x_ref[...]
        dyv = dy_ref[...].astype(F32)
        r = lax.rsqrt(jnp.mean(xv * xv, axis=-1, keepdims=True) + EPS)
        gy = dyv * w_ref[...]
        proj = jnp.sum(gy * xv, axis=-1, keepdims=True) * (1.0 / d)
        dx = dres_ref[...] + r * gy - xv * (r * r * r) * proj
        dx_ref[...] = dx
        dxb_ref[...] = dx.astype(BF16)
        part = jnp.sum(dyv * xv * r, axis=0, keepdims=True)

        @pl.when(pl.program_id(0) == 0)
        def _():
            dw_ref[...] = part

        @pl.when(pl.program_id(0) > 0)
        def _():
            dw_ref[...] += part

    row = pl.BlockSpec((tr, d), lambda i: (i, 0))
    vec = pl.BlockSpec((1, d), lambda i: (0, 0))
    return pl.pallas_call(
        body, name=name, grid=(t // tr,),
        in_specs=[row, vec, row, row], out_specs=[row, row, vec],
        out_shape=[jax.ShapeDtypeStruct((t, d), F32), jax.ShapeDtypeStruct((t, d), BF16),
                   jax.ShapeDtypeStruct((1, d), F32)],
        compiler_params=_cparams("arbitrary"),
    )(x, w, dy, dres)


def _adamw(parts, w, m, v, name, tr=128, transposed=False):
    _, r, c = w.shape
    tr = tr if r % tr == 0 else r
    c1 = 1.0 - ADAM_B1 ** ADAM_STEP
    c2 = 1.0 - ADAM_B2 ** ADAM_STEP

    def body(p_ref, w_ref, m_ref, v_ref, g_ref, d_ref, nm_ref, nv_ref):
        g = p_ref[0].astype(F32)
        for s in range(1, N_DEV):
            g = g + p_ref[s].astype(F32)
        if transposed:
            i, j = lax.broadcasted_iota(jnp.int32, (tr, tr), 0), lax.broadcasted_iota(jnp.int32, (tr, tr), 1)
            g = _dot_exact(jnp.where(i == j, 1.0, 0.0), g, NT, True)
        nm = ADAM_B1 * m_ref[...] + (1.0 - ADAM_B1) * g
        nv = ADAM_B2 * v_ref[...] + (1.0 - ADAM_B2) * (g * g)
        m_hat = nm / c1
        v_hat = nv / c2
        g_ref[...] = g
        d_ref[...] = -ADAM_LR * (m_hat / (jnp.sqrt(v_hat) + ADAM_EPS) + ADAM_WD * w_ref[...])
        nm_ref[...] = nm
        nv_ref[...] = nv

    blk = pl.BlockSpec((None, tr, c), lambda i: (0, i, 0))
    p_spec = (pl.BlockSpec((N_DEV, c, tr), lambda i: (0, 0, i)) if transposed
              else pl.BlockSpec((N_DEV, tr, c), lambda i: (0, i, 0)))
    return pl.pallas_call(
        body, name=name, grid=(r // tr,),
        in_specs=[p_spec, blk, blk, blk],
        out_specs=[blk] * 4, out_shape=[jax.ShapeDtypeStruct((1, r, c), F32)] * 4,
        compiler_params=_cparams("parallel"),
    )(parts, w, m, v)


ROW_TILE, ROW_HEADS = 256, 16
CONV_ROWS, CONV_HEADS = 512, 8


def _window(dest, inner, n_in, out_index):
    if dest is None:
        return inner, 0, [], [], {}
    buf, col0, total = dest
    if buf is None:
        return total, col0, [], [], {}
    return total, col0, [buf], [ANY], {n_in: out_index}


def _skip_ref(body, at, count):
    return body if count == 0 else (lambda *refs: body(*refs[:at], *refs[at + count:]))


def _heads_of(x, nh):
    return [x[:, h * HEAD_DIM:(h + 1) * HEAD_DIM] for h in range(nh)]


def _headnorm_fwd(proj, w, col0, inner, name, tr=ROW_TILE, hb=ROW_HEADS):
    t = proj.shape[0]
    tr = min(tr, t)
    hb = min(hb, inner // HEAD_DIM)
    wc = hb * HEAD_DIM
    c0 = col0 // wc

    def body(x_ref, w_ref, o_ref):
        outs = []
        for xh in _heads_of(x_ref[...], hb):
            r = lax.rsqrt(jnp.mean(xh * xh, axis=-1, keepdims=True) + EPS)
            outs.append((xh * r * w_ref[...]).astype(BF16))
        o_ref[...] = jnp.concatenate(outs, axis=1)

    return pl.pallas_call(
        body, name=name, grid=(t // tr, inner // wc),
        in_specs=[pl.BlockSpec((tr, wc), lambda i, j: (i, j + c0)), pl.BlockSpec((1, HEAD_DIM), lambda i, j: (0, 0))],
        out_specs=pl.BlockSpec((tr, wc), lambda i, j: (i, j)),
        out_shape=jax.ShapeDtypeStruct((t, inner), BF16),
        compiler_params=_cparams("parallel", "parallel"),
    )(proj, w)


def _headnorm_bwd(dy, proj, w, col0, inner, name, tr=ROW_TILE, hb=ROW_HEADS, dest=None):
    t = proj.shape[0]
    tr = min(tr, t)
    hb = min(hb, inner // HEAD_DIM)
    wc = hb * HEAD_DIM
    c0 = col0 // wc
    width, out0, more, more_specs, aliases = _window(dest, inner, 3, 0)

    def body(dy_ref, x_ref, w_ref, dx_ref, dw_ref):
        outs = []
        part = jnp.zeros((1, HEAD_DIM), F32)
        for dyh, xh in zip(_heads_of(dy_ref[...], hb), _heads_of(x_ref[...], hb)):
            r = lax.rsqrt(jnp.mean(xh * xh, axis=-1, keepdims=True) + EPS)
            gy = dyh * w_ref[...]
            pr = jnp.sum(gy * xh, axis=-1, keepdims=True) * (1.0 / HEAD_DIM)
            outs.append((r * gy - xh * (r * r * r) * pr).astype(BF16))
            part = part + jnp.sum(dyh * xh * r, axis=0, keepdims=True)
        dx_ref[...] = jnp.concatenate(outs, axis=1)
        first = (pl.program_id(0) == 0) & (pl.program_id(1) == 0)

        @pl.when(first)
        def _():
            dw_ref[...] = part

        @pl.when(jnp.logical_not(first))
        def _():
            dw_ref[...] += part

    blk = pl.BlockSpec((tr, wc), lambda i, j: (i, j))
    return pl.pallas_call(
        _skip_ref(body, 3, len(more)), name=name, grid=(t // tr, inner // wc),
        in_specs=[blk, pl.BlockSpec((tr, wc), lambda i, j: (i, j + c0)),
                  pl.BlockSpec((1, HEAD_DIM), lambda i, j: (0, 0))] + more_specs,
        out_specs=[pl.BlockSpec((tr, wc), lambda i, j: (i, j + out0 // wc)),
                   pl.BlockSpec((1, HEAD_DIM), lambda i, j: (0, 0))],
        out_shape=[jax.ShapeDtypeStruct((t, width), BF16), jax.ShapeDtypeStruct((1, HEAD_DIM), F32)],
        input_output_aliases=aliases, compiler_params=_cparams("arbitrary", "arbitrary"),
    )(dy, proj, w, *more)


def _gate_fwd(o, proj, zcol0, inner, name, norm_w=None, tr=ROW_TILE, hb=ROW_HEADS):
    t = o.shape[0]
    tr = min(tr, t)
    hb = min(hb, inner // HEAD_DIM)
    wc = hb * HEAD_DIM
    c0 = zcol0 // wc
    has_w = norm_w is not None

    def body(*refs):
        o_ref, z_ref = refs[0], refs[1]
        out_ref = refs[2 + has_w]
        outs = []
        for oh, zh in zip(_heads_of(o_ref[...], hb), _heads_of(z_ref[...], hb)):
            if has_w:
                r = lax.rsqrt(jnp.mean(oh * oh, axis=-1, keepdims=True) + EPS)
                oh = oh * r * refs[2][...]
            outs.append((oh * _silu(zh)).astype(BF16))
        out_ref[...] = jnp.concatenate(outs, axis=1)

    blk = pl.BlockSpec((tr, wc), lambda i, j: (i, j))
    vec = pl.BlockSpec((1, HEAD_DIM), lambda i, j: (0, 0))
    return pl.pallas_call(
        body, name=name, grid=(t // tr, inner // wc),
        in_specs=[blk, pl.BlockSpec((tr, wc), lambda i, j: (i, j + c0))] + ([vec] if has_w else []),
        out_specs=blk, out_shape=jax.ShapeDtypeStruct((t, inner), BF16),
        compiler_params=_cparams("parallel", "parallel"),
    )(*([o, proj] + ([norm_w] if has_w else [])))


def _gate_bwd(dg, o, proj, zcol0, inner, name, do_dtype, norm_w=None, tr=ROW_TILE, hb=ROW_HEADS, dest=None):
    t = o.shape[0]
    tr = min(tr, t)
    hb = min(hb, inner // HEAD_DIM)
    wc = hb * HEAD_DIM
    c0 = zcol0 // wc
    has_w = norm_w is not None
    width, out0, more, more_specs, aliases = _window(dest, inner, 3 + has_w, 1)

    def body(*refs):
        dg_ref, o_ref, z_ref = refs[0], refs[1], refs[2]
        do_ref, dz_ref = refs[3 + has_w], refs[4 + has_w]
        dos, dzs = [], []
        part = jnp.zeros((1, HEAD_DIM), F32)
        for dgh, oh, zh in zip(_heads_of(dg_ref[...], hb), _heads_of(o_ref[...], hb), _heads_of(z_ref[...], hb)):
            dy = dgh * _silu(zh)
            if has_w:
                w = refs[3][...]
                r = lax.rsqrt(jnp.mean(oh * oh, axis=-1, keepdims=True) + EPS)
                on = oh * r
                dzs.append((dgh * on * w * _dsilu(zh)).astype(BF16))
                gy = dy * w
                pr = jnp.sum(gy * oh, axis=-1, keepdims=True) * (1.0 / HEAD_DIM)
                dos.append((r * gy - oh * (r * r * r) * pr).astype(do_dtype))
                part = part + jnp.sum(dy * on, axis=0, keepdims=True)
            else:
                dzs.append((dgh * oh * _dsilu(zh)).astype(BF16))
                dos.append(dy.astype(do_dtype))
        do_ref[...] = jnp.concatenate(dos, axis=1)
        dz_ref[...] = jnp.concatenate(dzs, axis=1)
        if has_w:
            dw_ref = refs[6]
            first = (pl.program_id(0) == 0) & (pl.program_id(1) == 0)

            @pl.when(first)
            def _():
                dw_ref[...] = part

            @pl.when(jnp.logical_not(first))
            def _():
                dw_ref[...] += part

    blk = pl.BlockSpec((tr, wc), lambda i, j: (i, j))
    vec = pl.BlockSpec((1, HEAD_DIM), lambda i, j: (0, 0))
    return pl.pallas_call(
        _skip_ref(body, 3 + has_w, len(more)), name=name, grid=(t // tr, inner // wc),
        in_specs=[blk, blk, pl.BlockSpec((tr, wc), lambda i, j: (i, j + c0))] + ([vec] if has_w else []) + more_specs,
        out_specs=[blk, pl.BlockSpec((tr, wc), lambda i, j: (i, j + out0 // wc))] + ([vec] if has_w else []),
        out_shape=[jax.ShapeDtypeStruct((t, inner), do_dtype), jax.ShapeDtypeStruct((t, width), BF16)]
        + ([jax.ShapeDtypeStruct((1, HEAD_DIM), F32)] if has_w else []),
        input_output_aliases=aliases, compiler_params=_cparams("arbitrary", "arbitrary"),
    )(*([dg, o, proj] + ([norm_w] if has_w else []) + more))


N_REL = 2 * REL_CLIP + 1
REL_PAD = 640
WIN = 2 * Q_TILE


def _diag_onehot():
    i = lax.broadcasted_iota(jnp.int32, (REL_PAD, WIN), 0)
    j = lax.broadcasted_iota(jnp.int32, (REL_PAD, WIN), 1)
    rel = jnp.where(j < Q_TILE + CHUNK, Q_TILE - j, Q_TILE + WIN - j)
    used = (j < Q_TILE + CHUNK) | (j > WIN - CHUNK)
    idx = jnp.clip(rel, -REL_CLIP, REL_CLIP) + REL_CLIP
    return jnp.where(used & (i == idx), 1.0, 0.0).astype(F32)


def _band_mask():
    r = lax.broadcasted_iota(jnp.int32, (Q_TILE, WIN), 0) // CHUNK
    kc = lax.broadcasted_iota(jnp.int32, (Q_TILE, WIN), 1) // CHUNK - LEFT_CHUNKS
    return (kc <= r) & (kc >= r - LEFT_CHUNKS)


def _bias_tiles(rel_bias_pad, name):
    nh = rel_bias_pad.shape[0]

    def body(rb_ref, o_ref):
        dvec = _nn(rb_ref[...], _diag_onehot(), HIGHEST)[0:1, :]
        tile = pltpu.roll(jnp.broadcast_to(dvec, (Q_TILE, WIN)), 0, 1, stride=1, stride_axis=0)
        o_ref[...] = jnp.where(_band_mask(), tile, NEG_BIG)

    return pl.pallas_call(
        body, name=name, grid=(nh,),
        in_specs=[pl.BlockSpec((None, 8, REL_PAD), lambda h: (h, 0, 0))],
        out_specs=pl.BlockSpec((None, Q_TILE, WIN), lambda h: (h, 0, 0)),
        out_shape=jax.ShapeDtypeStruct((nh, Q_TILE, WIN), F32),
        compiler_params=_cparams("parallel"),
    )(rel_bias_pad)


def _bias_grad(dtile, name):
    nh = dtile.shape[0]

    def body(d_ref, o_ref):
        ri = lax.broadcasted_iota(jnp.int32, (Q_TILE, Q_TILE), 0)
        ci = lax.broadcasted_iota(jnp.int32, (Q_TILE, Q_TILE), 1)
        flip = jnp.where(ri + ci == Q_TILE - 1, 1.0, 0.0).astype(F32)
        rev = _dot_exact(flip, d_ref[...], NN, True)
        rolled = pltpu.roll(rev, WIN - (Q_TILE - 1), 1, stride=1, stride_axis=0)
        diag = jnp.broadcast_to(jnp.sum(rolled, axis=0, keepdims=True), (8, WIN))
        o_ref[...] = _nt(diag, _diag_onehot(), HIGHEST)

    return pl.pallas_call(
        body, name=name, grid=(nh,),
        in_specs=[pl.BlockSpec((None, Q_TILE, WIN), lambda h: (h, 0, 0))],
        out_specs=pl.BlockSpec((None, 8, REL_PAD), lambda h: (h, 0, 0)),
        out_shape=jax.ShapeDtypeStruct((nh, 8, REL_PAD), F32),
        compiler_params=_cparams("parallel"),
    )(dtile)


GROUP = 2 * CHUNK
BAND = Q_TILE + GROUP


def _band_rows(r0_ref, r1_ref, g):
    return jnp.concatenate([r0_ref[GROUP * g:, :], r1_ref[:GROUP * (g + 1), :]], axis=0)


N_GROUPS = Q_TILE // GROUP


def _groups(ref):
    return jnp.stack([ref[GROUP * g:GROUP * (g + 1), :] for g in range(N_GROUPS)])


def _bands(r0_ref, r1_ref):
    return jnp.stack([_band_rows(r0_ref, r1_ref, g) for g in range(N_GROUPS)])


def _group_probs(q, kw, b_ref, first_tile):
    bias = jnp.stack([b_ref[GROUP * g:GROUP * (g + 1), GROUP * g:GROUP * g + BAND] for g in range(N_GROUPS)])
    s = _dot(q, kw, BNT) * (HEAD_DIM ** -0.5) + bias
    col = (lax.broadcasted_iota(jnp.int32, (N_GROUPS, GROUP, BAND), 2)
           + GROUP * lax.broadcasted_iota(jnp.int32, (N_GROUPS, GROUP, BAND), 0))
    s = jnp.where(first_tile & (col < Q_TILE), NEG_BIG, s)
    p = jnp.exp(s - jnp.max(s, axis=-1, keepdims=True))
    return p * (1.0 / jnp.sum(p, axis=-1, keepdims=True))


def _attn_fwd(q, k, v, v_col0, bias, name):
    t, inner = q.shape
    vh = v_col0 // HEAD_DIM
    nh, nt = inner // HEAD_DIM, t // Q_TILE

    def body(q_ref, k0_ref, k1_ref, v0_ref, v1_ref, b_ref, o_ref):
        p = _group_probs(_groups(q_ref), _bands(k0_ref, k1_ref), b_ref, pl.program_id(1) == 0)
        o = _dot(_bf(p), _bf(_bands(v0_ref, v1_ref)), BNN)
        for g in range(N_GROUPS):
            o_ref[GROUP * g:GROUP * (g + 1), :] = o[g]

    cur = pl.BlockSpec((Q_TILE, HEAD_DIM), lambda h, i: (i, h))
    prev = pl.BlockSpec((Q_TILE, HEAD_DIM), lambda h, i: (jnp.maximum(i - 1, 0), h))
    v_cur = pl.BlockSpec((Q_TILE, HEAD_DIM), lambda h, i: (i, h + vh))
    v_prev = pl.BlockSpec((Q_TILE, HEAD_DIM), lambda h, i: (jnp.maximum(i - 1, 0), h + vh))
    return pl.pallas_call(
        body, name=name, grid=(nh, nt),
        in_specs=[cur, prev, cur, v_prev, v_cur, pl.BlockSpec((None, Q_TILE, WIN), lambda h, i: (h, 0, 0))],
        out_specs=cur, out_shape=jax.ShapeDtypeStruct((t, inner), F32),
        compiler_params=_cparams("parallel", "parallel"),
    )(q, k, k, v, v, bias)


def _attn_bwd(q, k, v, v_col0, do, bias, name, dest=None):
    t, inner = q.shape
    nh, nt = inner // HEAD_DIM, t // Q_TILE
    scale = HEAD_DIM ** -0.5

    def body(q_ref, k0_ref, k1_ref, v0_ref, v1_ref, do_ref, b_ref, dq_ref, dk_ref, dv_ref, db_ref,
             ck_ref, cv_ref, wk_ref, wv_ref):
        i = pl.program_id(1)

        @pl.when(i == 0)
        def _():
            ck_ref[...] = jnp.zeros(blk, F32)
            cv_ref[...] = jnp.zeros(blk, F32)
            db_ref[...] = jnp.zeros((Q_TILE, WIN), F32)

        @pl.when(i < nt)
        def _():
            wk_ref[...] = jnp.zeros((WIN, HEAD_DIM), F32)
            wv_ref[...] = jnp.zeros((WIN, HEAD_DIM), F32)
            qv, dov = _groups(q_ref), _groups(do_ref)
            kw, vw = _bands(k0_ref, k1_ref), _bf(_bands(v0_ref, v1_ref))
            p = _group_probs(qv, kw, b_ref, i == 0)
            dp = _dot(dov, vw, BNT)
            ds = p * (dp - jnp.sum(p * dp, axis=-1, keepdims=True))
            pb, dsb = _bf(p), _bf(ds)
            dq = _dot(dsb, kw, BNN) * scale
            dkw = _dot(dsb, qv, BTN) * scale
            dvw = _dot(pb, dov, BTN)
            for g in range(N_GROUPS):
                rows, cols = slice(GROUP * g, GROUP * (g + 1)), slice(GROUP * g, GROUP * g + BAND)
                db_ref[rows, cols] += ds[g]
                dq_ref[rows, :] = dq[g]
                wk_ref[cols, :] += dkw[g]
                wv_ref[cols, :] += dvw[g]
            dk_ref[...] = ck_ref[...] + wk_ref[:Q_TILE, :]
            dv_ref[...] = (cv_ref[...] + wv_ref[:Q_TILE, :]).astype(BF16)
            ck_ref[...] = wk_ref[Q_TILE:, :]
            cv_ref[...] = wv_ref[Q_TILE:, :]

        @pl.when(i == nt)
        def _():
            dk_ref[...] = ck_ref[...]
            dv_ref[...] = cv_ref[...].astype(BF16)

    blk = (Q_TILE, HEAD_DIM)
    cur = pl.BlockSpec(blk, lambda h, i: (jnp.minimum(i, nt - 1), h))
    prev = pl.BlockSpec(blk, lambda h, i: (jnp.clip(i - 1, 0, nt - 1), h))
    lag = pl.BlockSpec(blk, lambda h, i: (jnp.maximum(i - 1, 0), h))
    vh = v_col0 // HEAD_DIM
    v_cur = pl.BlockSpec(blk, lambda h, i: (jnp.minimum(i, nt - 1), h + vh))
    v_prev = pl.BlockSpec(blk, lambda h, i: (jnp.clip(i - 1, 0, nt - 1), h + vh))
    tile = pl.BlockSpec((None, Q_TILE, WIN), lambda h, i: (h, 0, 0))
    width, out0, more, more_specs, aliases = _window(dest, inner, 7, 2)
    return pl.pallas_call(
        _skip_ref(body, 7, len(more)), name=name, grid=(nh, nt + 1),
        in_specs=[cur, prev, cur, v_prev, v_cur, cur, tile] + more_specs,
        out_specs=[cur, lag, pl.BlockSpec(blk, lambda h, i: (jnp.maximum(i - 1, 0), h + out0 // HEAD_DIM)), tile],
        out_shape=[jax.ShapeDtypeStruct((t, inner), F32)] * 2 + [jax.ShapeDtypeStruct((t, width), BF16),
                                                                 jax.ShapeDtypeStruct((nh, Q_TILE, WIN), F32)],
        scratch_shapes=[pltpu.VMEM(blk, F32), pltpu.VMEM(blk, F32),
                        pltpu.VMEM((WIN, HEAD_DIM), F32), pltpu.VMEM((WIN, HEAD_DIM), F32)],
        input_output_aliases=aliases, compiler_params=_cparams("arbitrary", "arbitrary"),
    )(q, k, k, v, v, do, bias, *more)


def _pad_rel_bias(rel_bias):
    nh = rel_bias.shape[0]
    return jnp.broadcast_to(jnp.pad(rel_bias, ((0, 0), (0, REL_PAD - N_REL)))[:, None, :], (nh, 8, REL_PAD))


def _layer_b_fwd(h1, nw, w_in_t, qw, kw, bias, w_out, target):
    t, d = h1.shape
    inner = w_out.shape[0]
    hn = _rms_fwd(h1, nw, "b_rms")
    proj = _mm(hn, w_in_t, "nt", t, 4 * inner, d, out_dtype=F32, name="b_proj")
    qn = _headnorm_fwd(proj, qw, 0, inner, "b_qnorm")
    kn = _headnorm_fwd(proj, kw, inner, inner, "b_knorm")
    o = _attn_fwd(qn, kn, proj, 2 * inner, bias, "b_attn")
    g = _gate_fwd(o, proj, 3 * inner, inner, "b_gate")
    loss_parts = _mm(g, w_out, "nn", t, d, inner, out_dtype=F32, name="b_out", res=h1, loss_target=target)
    return loss_parts, (hn, proj, qn, kn, o, g)


def _layer_b_bwd(dh2, dh2b, h1, nw, w_in_t, qw, kw, bias, w_out, saved):
    hn, proj, qn, kn, o, g = saved
    t, d = h1.shape
    inner = w_out.shape[0]
    dg = _mm(dh2b, w_out, "nt", t, inner, d, out_dtype=F32, name="b_dgate")
    dw_out = _mm(g, dh2b, "tn", inner, d, t, out_dtype=BF16, name="b_dwout")
    do, dproj = _gate_bwd(dg, o, proj, 3 * inner, inner, "b_gate_bwd", BF16, dest=(None, 3 * inner, 4 * inner))
    dq, dk, dproj, dtile = _attn_bwd(qn, kn, proj, 2 * inner, do, bias, "b_attn_bwd",
                                     dest=(dproj, 2 * inner, 4 * inner))
    dproj, dqw = _headnorm_bwd(dq, proj, qw, 0, inner, "b_qnorm_bwd", dest=(dproj, 0, 4 * inner))
    dproj, dkw = _headnorm_bwd(dk, proj, kw, inner, inner, "b_knorm_bwd", dest=(dproj, inner, 4 * inner))
    dhn = _mm(dproj, w_in_t, "nn", t, d, 4 * inner, out_dtype=F32, name="b_dhn")
    dw_in_t = _mm(dproj, hn, "tn", 4 * inner, d, t, out_dtype=BF16, name="b_dwin")
    dh1, dh1b, dnw = _rms_bwd(h1, nw, dhn, dh2, "b_rms_bwd")
    drb = _bias_grad(dtile, "b_bias_grad")[:, 0, :N_REL]
    return dh1, dh1b, dnw, dw_in_t, dqw, dkw, drb, dw_out


LANES = 128


def _softplus(x):
    return jnp.maximum(x, 0.0) + jnp.log1p(jnp.exp(-jnp.abs(x)))


def _gates_fwd(ab, alog_row, dt_row, nh, name, tr=1024):
    t = ab.shape[0]
    tr = min(tr, t)

    def body(x_ref, al_ref, dt_ref, o_ref):
        x = x_ref[...]
        lane = lax.broadcasted_iota(jnp.int32, x.shape, 1)
        g = -jnp.exp(al_ref[...]) * _softplus(x + dt_ref[...])
        o_ref[...] = jnp.where(lane < nh, g, jnp.where(lane < 2 * nh, _sigmoid(x), 0.0))

    row = pl.BlockSpec((tr, LANES), lambda i: (i, 0))
    vec = pl.BlockSpec((1, LANES), lambda i: (0, 0))
    return pl.pallas_call(
        body, name=name, grid=(t // tr,), in_specs=[row, vec, vec], out_specs=row,
        out_shape=jax.ShapeDtypeStruct((t, LANES), F32), compiler_params=_cparams("parallel"),
    )(ab, alog_row, dt_row)


def _gates_bwd(ab, alog_row, dt_row, dgates, nh, name, tr=1024):
    t = ab.shape[0]
    tr = min(tr, t)
    npart = dgates.shape[0]

    def body(x_ref, al_ref, dt_ref, dg_ref, dx_ref, s_ref):
        x = x_ref[...]
        lane = lax.broadcasted_iota(jnp.int32, x.shape, 1)
        dgt = dg_ref[0]
        for p in range(1, npart):
            dgt = dgt + dg_ref[p]
        ea = jnp.exp(al_ref[...])
        xa = x + dt_ref[...]
        da = jnp.where(lane < nh, dgt * (-ea) * _sigmoid(xa), 0.0)
        beta = _sigmoid(x)
        db = jnp.where((lane >= nh) & (lane < 2 * nh), dgt * beta * (1.0 - beta), 0.0)
        dx_ref[...] = (da + db).astype(BF16)
        dal = jnp.sum(jnp.where(lane < nh, dgt * (-ea) * _softplus(xa), 0.0), axis=0, keepdims=True)
        ddt = jnp.sum(da, axis=0, keepdims=True)
        r8 = lax.broadcasted_iota(jnp.int32, (8, LANES), 0)
        part = jnp.where(r8 == 0, dal, jnp.where(r8 == 1, ddt, 0.0))

        @pl.when(pl.program_id(0) == 0)
        def _():
            s_ref[...] = part

        @pl.when(pl.program_id(0) > 0)
        def _():
            s_ref[...] += part

    row = pl.BlockSpec((tr, LANES), lambda i: (i, 0))
    vec = pl.BlockSpec((1, LANES), lambda i: (0, 0))
    return pl.pallas_call(
        body, name=name, grid=(t // tr,),
        in_specs=[row, vec, vec, pl.BlockSpec((npart, tr, LANES), lambda i: (0, i, 0))],
        out_specs=[row, pl.BlockSpec((8, LANES), lambda i: (0, 0))],
        out_shape=[jax.ShapeDtypeStruct((t, LANES), BF16), jax.ShapeDtypeStruct((8, LANES), F32)],
        compiler_params=_cparams("arbitrary"),
    )(ab, alog_row, dt_row, dgates)


HALO = 8


def _conv_taps(ext, w, rows):
    acc = ext[HALO:HALO + rows] * w[CONV_K - 1:CONV_K]
    for s in range(1, CONV_K):
        acc = acc + pltpu.roll(ext, s, 0)[HALO:HALO + rows] * w[CONV_K - 1 - s:CONV_K - s]
    return acc


def _conv_fwd(proj, conv_w, col0, inner, mode, name, tt=CONV_ROWS, hb=CONV_HEADS):
    t = proj.shape[0]
    tt = min(tt, t)
    hb = min(hb, inner // HEAD_DIM)
    wc = hb * HEAD_DIM
    c0 = col0 // wc
    hpb = tt // HALO

    def body(x_ref, halo_ref, w_ref, o_ref):
        halo = jnp.where(pl.program_id(1) == 0, 0.0, halo_ref[...])
        s = _silu(_conv_taps(jnp.concatenate([halo, x_ref[...]], axis=0), w_ref[...], tt))
        if mode == "v":
            o_ref[...] = s
        else:
            mul = HEAD_DIM ** -0.5 if mode == "q" else 1.0
            o_ref[...] = jnp.concatenate(
                [sh * (lax.rsqrt(jnp.sum(sh * sh, axis=-1, keepdims=True) + EPS) * mul) for sh in _heads_of(s, hb)], axis=1)

    return pl.pallas_call(
        body, name=name, grid=(inner // wc, t // tt),
        in_specs=[pl.BlockSpec((tt, wc), lambda j, i: (i, j + c0)),
                  pl.BlockSpec((HALO, wc), lambda j, i: (jnp.maximum(i * hpb - 1, 0), j + c0)),
                  pl.BlockSpec((CONV_K, wc), lambda j, i: (0, j + c0))],
        out_specs=pl.BlockSpec((tt, wc), lambda j, i: (i, j)),
        out_shape=jax.ShapeDtypeStruct((t, inner), F32),
        compiler_params=_cparams("parallel", "parallel"),
    )(proj, proj, conv_w)


def _conv_bwd(dy, proj, conv_w, col0, inner, mode, name, tt=CONV_ROWS, hb=CONV_HEADS, dest=None):
    t = proj.shape[0]
    tt = min(tt, t)
    nt = t // tt
    hb = min(hb, inner // HEAD_DIM)
    wc = hb * HEAD_DIM
    c0 = col0 // wc
    hpb = tt // HALO
    rows = tt + HALO

    def body(dy_ref, dyn_ref, x_ref, xp_ref, xn_ref, w_ref, dx_ref, dw_ref):
        i = pl.program_id(1)
        w = w_ref[...]
        xprev = jnp.where(i == 0, 0.0, xp_ref[...])
        ext = jnp.concatenate([xprev, x_ref[...], xn_ref[...]], axis=0)
        c = _conv_taps(ext, w, rows)
        dyv = jnp.concatenate([dy_ref[...], jnp.where(i == nt - 1, 0.0, dyn_ref[...])], axis=0)
        sg = _sigmoid(c)
        s = c * sg
        if mode == "v":
            ds = dyv
        else:
            mul = HEAD_DIM ** -0.5 if mode == "q" else 1.0
            parts = []
            for dyh, sh in zip(_heads_of(dyv, hb), _heads_of(s, hb)):
                r = lax.rsqrt(jnp.sum(sh * sh, axis=-1, keepdims=True) + EPS)
                parts.append(mul * (r * dyh - sh * (r * r * r) * jnp.sum(dyh * sh, axis=-1, keepdims=True)))
            ds = jnp.concatenate(parts, axis=1)
        dc = ds * (sg * (1.0 + c * (1.0 - sg)))
        dx = dc[:tt] * w[CONV_K - 1:CONV_K]
        for sft in range(1, CONV_K):
            dx = dx + pltpu.roll(dc, rows - sft, 0)[:tt] * w[CONV_K - 1 - sft:CONV_K - sft]
        dx_ref[...] = dx.astype(BF16)
        r8 = lax.broadcasted_iota(jnp.int32, (8, wc), 0)
        part = jnp.zeros((8, wc), F32)
        for sft in range(CONV_K):
            xs = ext[HALO:HALO + tt] if sft == 0 else pltpu.roll(ext, sft, 0)[HALO:HALO + tt]
            part = part + jnp.where(r8 == CONV_K - 1 - sft, jnp.sum(dc[:tt] * xs, axis=0, keepdims=True), 0.0)

        @pl.when(i == 0)
        def _():
            dw_ref[...] = part

        @pl.when(i > 0)
        def _():
            dw_ref[...] += part

    cur = lambda off: pl.BlockSpec((tt, wc), lambda j, i: (i, j + off))
    nxt = lambda off: pl.BlockSpec((HALO, wc), lambda j, i: (jnp.minimum((i + 1) * hpb, t // HALO - 1), j + off))
    width, out0, more, more_specs, aliases = _window(dest, inner, 6, 0)
    return pl.pallas_call(
        _skip_ref(body, 6, len(more)), name=name, grid=(inner // wc, nt),
        in_specs=[cur(0), nxt(0), cur(c0),
                  pl.BlockSpec((HALO, wc), lambda j, i: (jnp.maximum(i * hpb - 1, 0), j + c0)), nxt(c0),
                  pl.BlockSpec((CONV_K, wc), lambda j, i: (0, j + c0))] + more_specs,
        out_specs=[pl.BlockSpec((tt, wc), lambda j, i: (i, j + out0 // wc)),
                   pl.BlockSpec((8, wc), lambda j, i: (0, j))],
        out_shape=[jax.ShapeDtypeStruct((t, width), BF16), jax.ShapeDtypeStruct((8, inner), F32)],
        input_output_aliases=aliases, compiler_params=_cparams("parallel", "arbitrary"),
    )(dy, dy, proj, proj, proj, conv_w, *more)


GDN_HB = 4
GDN_NB = 8
SCAN_HB = 16
SCAN_NB = 2


def _iota2(n, m):
    return lax.broadcasted_iota(jnp.int32, (n, m), 0), lax.broadcasted_iota(jnp.int32, (n, m), 1)


def _head_select(first_head, hb, lane0):
    r, lane = _iota2(8, LANES)
    return jnp.where((r < hb) & (lane == lane0 + first_head + r), 1.0, 0.0).astype(F32)


def _chunk_gates(gt, selg, selb):
    i, j = _iota2(CHUNK, CHUNK)
    gc_all = _dot_exact(jnp.where(j <= i, 1.0, 0.0), gt, NN, True)
    return (_dot_exact(gc_all, selg, NT, False), _dot_exact(selg, gc_all, NT, True),
            _dot_exact(gt, selb, NT, False))


def _decay_terms(gcol, grow):
    i, j = _iota2(CHUNK, CHUNK)
    glast = gcol[:, CHUNK - 1:CHUNK, :]
    decay = jnp.exp(jnp.where(j <= i, gcol - grow, NEG_BIG))
    return jnp.exp(gcol), jnp.exp(glast - gcol), jnp.exp(glast), decay


def _unit_lower_inverse(a):
    i, j = _iota2(CHUNK, CHUNK)
    same16 = (i // 16) == (j // 16)
    same32 = (i // 32) == (j // 32)
    m = jnp.where(same16, -a, 0.0)
    x = jnp.where(i == j, 1.0, 0.0) + m
    for _ in range(3):
        m = _dot3(m, m, BNN)
        x = x + _dot3(x, m, BNN)
    for off in (jnp.where(same32 & jnp.logical_not(same16), a, 0.0), jnp.where(same32, 0.0, a)):
        x = x - _dot3(_dot3(x, off, BNN), x, BNN)
    return x


def _unit_inputs(refs, g_ref, selg, selb, hb, nb):
    units = [(c, h) for c in range(nb) for h in range(hb)]
    rs = lambda c: slice(c * CHUNK, (c + 1) * CHUNK)
    cs = lambda h: slice(h * HEAD_DIM, (h + 1) * HEAD_DIM)
    gates = [_chunk_gates(g_ref[rs(c), :], selg, selb) for c in range(nb)]
    stacked = [jnp.stack([r[rs(c), cs(h)] for c, h in units]) for r in refs]
    gcol = jnp.stack([gates[c][0][:, h:h + 1] for c, h in units])
    grow = jnp.stack([gates[c][1][h:h + 1, :] for c, h in units])
    bcol = jnp.stack([gates[c][2][:, h:h + 1] for c, h in units])
    return units, rs, cs, stacked, gcol, grow, bcol


def _gdn_specs(nh, inner, t, heads=GDN_HB, chunks=GDN_NB):
    hb, nb = min(heads, nh), chunks
    rows = nb * CHUNK
    wide = pl.BlockSpec((rows, hb * HEAD_DIM), lambda g, n: (n, g))
    sq = pl.BlockSpec((hb, rows, CHUNK), lambda g, n: (g, n, 0))
    gts = pl.BlockSpec((rows, LANES), lambda g, n: (n, 0))
    glb = pl.BlockSpec((nb * 8, hb * HEAD_DIM), lambda g, n: (n, g))
    return hb, nb, rows, wide, sq, gts, glb


def _gdn_intra_fwd(q, k, v, gates, nh, name, comm=None):
    t, inner = q.shape
    hb, nb, rows, wide, sq, gts, glb = _gdn_specs(nh, inner, t)

    def body(q_ref, k_ref, v_ref, g_ref, qe_ref, kel_ref, wb_ref, w_ref, u_ref, qk_ref, tm_ref, gl_ref):
        first = pl.program_id(0) * hb
        selg, selb = _head_select(first, hb, 0), _head_select(first, hb, nh)
        i, j = _iota2(CHUNK, CHUNK)
        units, rs, cs, (qv, kv, vv), gcol, grow, bcol = _unit_inputs(
            (q_ref, k_ref, v_ref), g_ref, selg, selb, hb, nb)
        e, el, gl, decay = _decay_terms(gcol, grow)
        kb = kv * bcol
        qbf, kbf = _bf(qv), _bf(kv)
        a = jnp.where(j < i, _dot(_bf(kb), kbf, BNT) * decay, 0.0)
        tm = _unit_lower_inverse(a)
        uw = _dot3(tm, jnp.concatenate([vv * bcol, kb * e], axis=2), BNN)
        qk = _bf(_dot(qbf, kbf, BNT) * decay)
        qe, kel = _bf(qv * e), _bf(kv * el)
        for n, (c, h) in enumerate(units):
            w = uw[n, :, HEAD_DIM:]
            qe_ref[rs(c), cs(h)] = qe[n]
            kel_ref[rs(c), cs(h)] = kel[n]
            wb_ref[rs(c), cs(h)] = _bf(w)
            w_ref[rs(c), cs(h)] = w
            u_ref[rs(c), cs(h)] = uw[n, :, :HEAD_DIM]
            qk_ref[h, rs(c), :] = qk[n]
            tm_ref[h, rs(c), :] = tm[n]
            gl_ref[c * 8:(c + 1) * 8, cs(h)] = jnp.broadcast_to(gl[n], (8, HEAD_DIM))

    big = lambda dt: jax.ShapeDtypeStruct((t, inner), dt)
    return _grid_call(
        body, name=name, grid=(nh // hb, t // rows),
        in_specs=[wide, wide, wide, gts],
        out_specs=[wide] * 5 + [sq, sq, glb],
        out_shape=[big(BF16), big(BF16), big(BF16), big(F32), big(F32),
                   jax.ShapeDtypeStruct((nh, t, CHUNK), BF16), jax.ShapeDtypeStruct((nh, t, CHUNK), F32),
                   jax.ShapeDtypeStruct((t // CHUNK * 8, inner), F32)],
        args=(q, k, v, gates), semantics=("parallel", "parallel"), comm=comm)


def _gdn_scan_fwd(qe, kel, wb, u, qk, glb, nh, name):
    t, inner = u.shape
    hb, nb, rows, wide, sq, _, glb_spec = _gdn_specs(nh, inner, t, SCAN_HB, SCAN_NB)

    def body(qe_ref, kel_ref, wb_ref, u_ref, qk_ref, gl_ref, o_ref, vn_ref, sall_ref, s_ref):
        @pl.when(pl.program_id(1) == 0)
        def _():
            s_ref[...] = jnp.zeros(s_ref.shape, F32)

        cs = lambda h: slice(h * HEAD_DIM, (h + 1) * HEAD_DIM)
        for c in range(nb):
            rs = slice(c * CHUNK, (c + 1) * CHUNK)
            heads = lambda ref: jnp.stack([ref[rs, cs(h)] for h in range(hb)])
            s = s_ref[...]
            sall_ref[c] = s
            sb = _bf(s)
            vn = heads(u_ref) - _dot(heads(wb_ref), sb, BNN)
            vnb = _bf(vn)
            o = _dot(heads(qe_ref), sb, BNN) + _dot(qk_ref[:, rs, :], vnb, BNN)
            gl = jnp.stack([gl_ref[c * 8:c * 8 + 1, cs(h)] for h in range(hb)])
            s_ref[...] = s * gl + _dot(heads(kel_ref), vnb, BTN)
            for h in range(hb):
                vn_ref[rs, cs(h)] = vn[h]
                o_ref[rs, cs(h)] = o[h]

    return pl.pallas_call(
        body, name=name, grid=(nh // hb, t // rows),
        in_specs=[wide, wide, wide, wide, sq, glb_spec],
        out_specs=[wide, wide, pl.BlockSpec((nb, hb, HEAD_DIM, HEAD_DIM), lambda g, n: (n, g, 0, 0))],
        out_shape=[jax.ShapeDtypeStruct((t, inner), F32), jax.ShapeDtypeStruct((t, inner), F32),
                   jax.ShapeDtypeStruct((t // CHUNK, nh, HEAD_DIM, HEAD_DIM), F32)],
        scratch_shapes=[pltpu.VMEM((hb, HEAD_DIM, HEAD_DIM), F32)],
        compiler_params=_cparams("parallel", "arbitrary"),
    )(qe, kel, wb, u, qk, glb)


def _gdn_scan_bwd(do, qe, kel, wb, vn, qk, glb, sall, nh, name):
    t, inner = do.shape
    hb, nb, rows, _, _, _, _ = _gdn_specs(nh, inner, t, SCAN_HB, SCAN_NB)
    last = t // rows - 1
    wide = pl.BlockSpec((rows, hb * HEAD_DIM), lambda g, n: (last - n, g))
    sq = pl.BlockSpec((hb, rows, CHUNK), lambda g, n: (g, last - n, 0))
    glb_spec = pl.BlockSpec((nb * 8, hb * HEAD_DIM), lambda g, n: (last - n, g))

    def body(do_ref, qe_ref, kel_ref, wb_ref, vn_ref, qk_ref, gl_ref, sall_ref,
             dvn_ref, dw_ref, dqe_ref, dkel_ref, dqk_ref, dgl_ref, ds_ref):
        @pl.when(pl.program_id(1) == 0)
        def _():
            ds_ref[...] = jnp.zeros(ds_ref.shape, F32)

        cs = lambda h: slice(h * HEAD_DIM, (h + 1) * HEAD_DIM)
        for c in reversed(range(nb)):
            rs = slice(c * CHUNK, (c + 1) * CHUNK)
            heads = lambda ref: jnp.stack([ref[rs, cs(h)] for h in range(hb)])
            ds, s = ds_ref[...], sall_ref[c]
            dsb, sb = _bf(ds), _bf(s)
            dob, vnb = _bf(heads(do_ref)), _bf(heads(vn_ref))
            dvn = _dot(qk_ref[:, rs, :], dob, BTN) + _dot(heads(kel_ref), dsb, BNN)
            dvnb = _bf(dvn)
            dw = -_dot(dvnb, sb, BNT)
            dqe = _dot(dob, sb, BNT)
            dkel = _dot(vnb, dsb, BNT)
            dqk_ref[:, rs, :] = _dot(dob, vnb, BNT)
            dgl = jnp.sum(jnp.sum(ds * s, axis=2, keepdims=True), axis=1, keepdims=True)
            gl = jnp.stack([gl_ref[c * 8:c * 8 + 1, cs(h)] for h in range(hb)])
            ds_ref[...] = ds * gl + _dot(heads(qe_ref), dob, BTN) - _dot(heads(wb_ref), dvnb, BTN)
            for h in range(hb):
                dvn_ref[rs, cs(h)] = dvn[h]
                dw_ref[rs, cs(h)] = dw[h]
                dqe_ref[rs, cs(h)] = dqe[h]
                dkel_ref[rs, cs(h)] = dkel[h]
                dgl_ref[c * 8:(c + 1) * 8, cs(h)] = jnp.broadcast_to(dgl[h], (8, HEAD_DIM))

    big = jax.ShapeDtypeStruct((t, inner), F32)
    return pl.pallas_call(
        body, name=name, grid=(nh // hb, t // rows),
        in_specs=[wide, wide, wide, wide, wide, sq, glb_spec,
                  pl.BlockSpec((nb, hb, HEAD_DIM, HEAD_DIM), lambda g, n: (last - n, g, 0, 0))],
        out_specs=[wide] * 4 + [sq, glb_spec],
        out_shape=[big] * 4 + [jax.ShapeDtypeStruct((nh, t, CHUNK), F32),
                               jax.ShapeDtypeStruct((t // CHUNK * 8, inner), F32)],
        scratch_shapes=[pltpu.VMEM((hb, HEAD_DIM, HEAD_DIM), F32)],
        compiler_params=_cparams("parallel", "arbitrary"),
    )(do, qe, kel, wb, vn, qk, glb, sall)


def _gdn_intra_bwd(q, k, v, gates, tm, w, u, dvn, dw, dqe, dkel, dqk, dglb, nh, name, comm=None):
    t, inner = q.shape
    hb, nb, rows, wide, sq, gts, glb = _gdn_specs(nh, inner, t)

    def body(q_ref, k_ref, v_ref, g_ref, tm_ref, w_ref, u_ref, dvn_ref, dw_ref, dqe_ref, dkel_ref, dqk_ref,
             dgl_ref, dq_ref, dk_ref, dv_ref, dg_ref):
        first = pl.program_id(0) * hb
        selg, selb = _head_select(first, hb, 0), _head_select(first, hb, nh)
        i, j = _iota2(CHUNK, CHUNK)
        lane8 = lax.broadcasted_iota(jnp.int32, (CHUNK, 8), 1)
        row = lax.broadcasted_iota(jnp.int32, (CHUNK, 1), 0)
        lower = jnp.where(j <= i, 1.0, 0.0).astype(F32)
        rsum = lambda x: jnp.sum(x, axis=-1, keepdims=True)
        units, rs, cs, (qv, kv, vv, wv, uv, dvn, dw, dqe, dkel), gcol, grow, bcol = _unit_inputs(
            (q_ref, k_ref, v_ref, w_ref, u_ref, dvn_ref, dw_ref, dqe_ref, dkel_ref), g_ref, selg, selb, hb, nb)
        nu = len(units)
        tmv = jnp.stack([tm_ref[h, rs(c), :] for c, h in units])
        dqk = jnp.where(j <= i, jnp.stack([dqk_ref[h, rs(c), :] for c, h in units]), 0.0)
        dgl = jnp.stack([dgl_ref[c * 8:c * 8 + 1, h * HEAD_DIM:h * HEAD_DIM + 1] for c, h in units])
        e, el, gl, decay = _decay_terms(gcol, grow)
        kb = kv * bcol
        qb, kbf, kbb = _bf(qv), _bf(kv), _bf(kb)
        dqkr = _bf(dqk * decay)
        dq = dqe * e + _dot(dqkr, kbf, BNN)
        dk = dkel * el + _dot(dqkr, qb, BTN)
        de = rsum(dqe * qv)
        del_ = rsum(dkel * kv)
        mq = dqk * _dot(qb, kbf, BNT) * decay
        dsol = _dot3(tmv, jnp.concatenate([dvn, dw], axis=2), BTN)
        dvb, dkbe = dsol[:, :, :HEAD_DIM], dsol[:, :, HEAD_DIM:]
        da = -jnp.where(j < i, _dot3(dsol, jnp.concatenate([uv, wv], axis=2), BNT), 0.0)
        dkk = _bf(da * decay)
        ma = da * _dot(kbb, kbf, BNT) * decay
        dkb = dkbe * e + _dot(dkk, kbf, BNN)
        de = de + rsum(dkbe * kb)
        dk = dk + _dot(dkk, kbb, BTN) + dkb * bcol
        dv = dvb * bcol
        dbeta = rsum(dkb * kv) + rsum(dvb * vv)
        m = mq + ma
        ones = jnp.ones((nu, CHUNK, LANES), F32)
        dgc = rsum(m) - _dot_exact(m, ones, BTN, False)[:, :, 0:1] + de * e - del_ * el
        tail = jnp.sum(del_ * el, axis=1, keepdims=True) + dgl * gl
        dgc = dgc + jnp.where(row == CHUNK - 1, tail, 0.0)
        for n, (c, h) in enumerate(units):
            dq_ref[rs(c), cs(h)] = dq[n]
            dk_ref[rs(c), cs(h)] = dk[n]
            dv_ref[rs(c), cs(h)] = dv[n]
        for c in range(nb):
            dgc_cols = jnp.zeros((CHUNK, 8), F32)
            dbeta_cols = jnp.zeros((CHUNK, 8), F32)
            for h in range(hb):
                dgc_cols = jnp.where(lane8 == h, dgc[c * hb + h], dgc_cols)
                dbeta_cols = jnp.where(lane8 == h, dbeta[c * hb + h], dbeta_cols)
            dg_cols = _dot_exact(lower, dgc_cols, TN, True)
            dg_ref[rs(c), :] = _dot_exact(dg_cols, selg, NN, False) + _dot_exact(dbeta_cols, selb, NN, False)

    big = jax.ShapeDtypeStruct((t, inner), F32)
    return _grid_call(
        body, name=name, grid=(nh // hb, t // rows),
        in_specs=[wide, wide, wide, gts, sq, wide, wide, wide, wide, wide, wide, sq, glb],
        out_specs=[wide, wide, wide, pl.BlockSpec((None, rows, LANES), lambda g, n: (g, n, 0))],
        out_shape=[big, big, big, jax.ShapeDtypeStruct((nh // hb, t, LANES), F32)],
        args=(q, k, v, gates, tm, w, u, dvn, dw, dqe, dkel, dqk, dglb), semantics=("parallel", "parallel"),
        comm=comm)


def _layer_a_fwd(x, hn, w_in_t, w_ab_t, conv_w, alog_row, dt_row, onw, nh, comm, w_out_of):
    t, d = x.shape
    inner = nh * HEAD_DIM
    proj = _mm(hn, w_in_t, "nt", t, 4 * inner, d, out_dtype=F32, name="a_proj")
    ab = _mm(hn, w_ab_t, "nt", t, LANES, d, out_dtype=F32, name="a_proj_ab")
    gates = _gates_fwd(ab, alog_row, dt_row, nh, "a_gates")
    q = _conv_fwd(proj, conv_w, 0, inner, "q", "a_conv_q")
    k = _conv_fwd(proj, conv_w, inner, inner, "k", "a_conv_k")
    v = _conv_fwd(proj, conv_w, 2 * inner, inner, "v", "a_conv_v")
    qe, kel, wb, w, u, qk, tm, glb, *carried = _gdn_intra_fwd(q, k, v, gates, nh, "a_intra", comm)
    o, vn, sall = _gdn_scan_fwd(qe, kel, wb, u, qk, glb, nh, "a_scan")
    g = _gate_fwd(o, proj, 3 * inner, inner, "a_gate", norm_w=onw)
    w_out = w_out_of(carried)
    h1 = _mm(g, w_out, "nn", t, d, inner, out_dtype=F32, name="a_out", res=x)
    return h1, (hn, proj, ab, gates, q, k, v, qe, kel, wb, w, u, qk, tm, glb, o, vn, sall, g), w_out, carried


def _layer_a_bwd(dh1, dh1b, x, nw, w_in_t, w_ab_t, conv_w, alog_row, dt_row, onw, w_out, nh, saved, comm_of,
                 own_comm):
    hn, proj, ab, gates, q, k, v, qe, kel, wb, w, u, qk, tm, glb, o, vn, sall, g = saved
    t, d = x.shape
    inner = w_out.shape[0]
    dg = _mm(dh1b, w_out, "nt", t, inner, d, out_dtype=F32, name="a_dgate")
    dw_out = _mm(g, dh1b, "tn", inner, d, t, out_dtype=BF16, name="a_dwout")
    comm = comm_of(dw_out)
    do, dproj, donw = _gate_bwd(dg, o, proj, 3 * inner, inner, "a_gate_bwd", F32, norm_w=onw,
                                dest=(None, 3 * inner, 4 * inner))
    dvn, dw, dqe, dkel, dqk, dglb = _gdn_scan_bwd(do, qe, kel, wb, vn, qk, glb, sall, nh, "a_scan_bwd")
    dq, dk, dv, dgates, *carried = _gdn_intra_bwd(q, k, v, gates, tm, w, u, dvn, dw, dqe, dkel, dqk, dglb, nh,
                                                  "a_intra_bwd", comm)
    dproj, dcq = _conv_bwd(dq, proj, conv_w, 0, inner, "q", "a_conv_q_bwd", dest=(dproj, 0, 4 * inner))
    dproj, dck = _conv_bwd(dk, proj, conv_w, inner, inner, "k", "a_conv_k_bwd", dest=(dproj, inner, 4 * inner))
    dproj, dcv = _conv_bwd(dv, proj, conv_w, 2 * inner, inner, "v", "a_conv_v_bwd",
                           dest=(dproj, 2 * inner, 4 * inner))
    dab, dsmall = _gates_bwd(ab, alog_row, dt_row, dgates, nh, "a_gates_bwd")
    dw_in_t = _mm(dproj, hn, "tn", 4 * inner, d, t, out_dtype=BF16, name="a_dwin")
    dw_ab_t = _mm(dab, hn, "tn", LANES, d, t, out_dtype=BF16, name="a_dwin_ab")
    dconv = jnp.concatenate([dcq[:CONV_K], dck[:CONV_K], dcv[:CONV_K]], axis=1)
    dhn = _mm(dab, w_ab_t, "nn", t, d, LANES, out_dtype=F32, name="a_dhn_ab")
    own = own_comm(dw_in_t, dw_ab_t, dconv)
    dhn = _mm(dproj, w_in_t, "nn", t, d, 4 * inner, out_dtype=F32, name="a_dhn", res=dhn, comm=own)
    dhn, carried_own = dhn if own is not None else (dhn, [])
    dx, _, dnw = _rms_bwd(x, nw, dhn, dh1, "a_rms_bwd")
    return dx, dnw, dsmall, donw, carried, carried_own


def _rows_of(a, rows):
    flat = a.reshape(-1)
    return jnp.pad(flat, (0, rows * LANES - flat.shape[0])).reshape(rows, LANES)


def _to_slabs(g, axis):
    shape = g.shape[:axis] + (N_DEV, g.shape[axis] // N_DEV) + g.shape[axis + 1:]
    return jnp.moveaxis(g.reshape(shape), axis, 0)


def _from_slabs(s, axis):
    m = jnp.moveaxis(s, 0, axis)
    return m.reshape(m.shape[:axis] + (m.shape[axis] * m.shape[axis + 1],) + m.shape[axis + 2:])


def kernel(x, norm_w, a_w_in, a_conv_w, a_a_log, a_dt_bias, a_out_norm_w, a_w_out, b_w_in, b_q_norm_w, b_k_norm_w, b_rel_bias, b_w_out, loss_target, m_norm_w, m_a_w_in, m_a_conv_w, m_a_a_log, m_a_dt_bias, m_a_out_norm_w, m_a_w_out, m_b_w_in, m_b_q_norm_w, m_b_k_norm_w, m_b_rel_bias, m_b_w_out, v_norm_w, v_a_w_in, v_a_conv_w, v_a_a_log, v_a_dt_bias, v_a_out_norm_w, v_a_w_out, v_b_w_in, v_b_q_norm_w, v_b_k_norm_w, v_b_rel_bias, v_b_w_out):
    xs, target = x[0], loss_target[0]
    nh = a_a_log.shape[-1]
    inner = N_DEV * a_w_out.shape[1]

    d = xs.shape[1]
    nw0, nw1 = norm_w[0:1], norm_w[1:2]
    hn0, (ga_in, g_conv) = _rms_fwd(
        xs, nw0, "a_rms", comm=_RoutedGather([a_w_in[0].T.astype(BF16), a_conv_w[0]]))
    wa_in_t = ga_in.reshape(-1, d)
    wa_ab_t = jnp.pad(wa_in_t[4 * inner:], ((0, LANES - 2 * nh), (0, 0)))
    conv_w = _from_slabs(g_conv, 1)
    alog_row = jnp.pad(a_a_log, ((0, 0), (0, LANES - nh)))
    dt_row = jnp.pad(a_dt_bias, ((0, 0), (0, LANES - nh)))

    h1, saved_a, wa_out, (gb_in, gb_out, _) = _layer_a_fwd(
        xs, hn0, wa_in_t, wa_ab_t, conv_w, alog_row, dt_row, a_out_norm_w, nh,
        _Comm("gather", [b_w_in[0].T.astype(BF16), b_w_out[0].astype(BF16), a_w_out[0].astype(BF16)]),
        lambda gathered: _from_slabs(gathered[2], 0))
    wb_in_t = gb_in.reshape(-1, d)
    wb_out = _from_slabs(gb_out, 0)
    bias = _bias_tiles(_pad_rel_bias(b_rel_bias[0]), "b_bias_tiles")
    (dh2, dh2b, loss_row), saved_b = _layer_b_fwd(h1, nw1, wb_in_t, b_q_norm_w, b_k_norm_w, bias, wb_out, target)

    dh1, dh1b, dnw1, dwb_in_t, dqw, dkw, drb, dwb_out = _layer_b_bwd(
        dh2, dh2b, h1, nw1, wb_in_t, b_q_norm_w, b_k_norm_w, bias, wb_out, saved_b)

    def exchange_early(dwa_out):
        return _Comm("exchange", [dwb_in_t.reshape(N_DEV, -1, d).astype(BF16), _to_slabs(dwb_out, 0).astype(BF16),
                                  _to_slabs(dwa_out, 0).astype(BF16)])

    def exchange_last(dwa_in_t, dwa_ab_t, dconv):
        full = jnp.concatenate([dwa_in_t, dwa_ab_t[:2 * nh]], axis=0)
        return _Comm("exchange", [full.reshape(N_DEV, -1, d).astype(BF16), _to_slabs(dconv, 1)])

    dx, dnw0, dsmall, donw, (pb_in, pb_out, pa_out), (pa_in, p_conv) = _layer_a_bwd(
        dh1, dh1b, xs, nw0, wa_in_t, wa_ab_t, conv_w, alog_row, dt_row, a_out_norm_w, wa_out, nh, saved_a,
        exchange_early, exchange_last)
    big = {}
    for name, p, w, m, v in (("a_w_in", pa_in, a_w_in, m_a_w_in, v_a_w_in),
                             ("a_w_out", pa_out, a_w_out, m_a_w_out, v_a_w_out),
                             ("b_w_in", pb_in, b_w_in, m_b_w_in, v_b_w_in),
                             ("b_w_out", pb_out, b_w_out, m_b_w_out, v_b_w_out),
                             ("a_conv_w", p_conv, a_conv_w, m_a_conv_w, v_a_conv_w)):
        big[name] = _adamw(p, w, m, v, "adamw_" + name, transposed=name in ("a_w_in", "b_w_in"))

    small = (("norm_w", norm_w, m_norm_w, v_norm_w, jnp.concatenate([dnw0, dnw1], axis=0)),
             ("a_a_log", a_a_log, m_a_a_log, v_a_a_log, dsmall[0:1, :nh]),
             ("a_dt_bias", a_dt_bias, m_a_dt_bias, v_a_dt_bias, dsmall[1:2, :nh]),
             ("a_out_norm_w", a_out_norm_w, m_a_out_norm_w, v_a_out_norm_w, donw),
             ("b_q_norm_w", b_q_norm_w, m_b_q_norm_w, v_b_q_norm_w, dqw),
             ("b_k_norm_w", b_k_norm_w, m_b_k_norm_w, v_b_k_norm_w, dkw),
             ("b_rel_bias", b_rel_bias, m_b_rel_bias, v_b_rel_bias, drb))
    rows = [8 * (-(-w.size // (8 * LANES))) for _, w, _, _, _ in small]
    pack = lambda arrs: jnp.concatenate([_rows_of(a, r) for a, r in zip(arrs, rows)] + [jnp.zeros((8, LANES), F32)], axis=0)
    g_pack = jnp.concatenate([_rows_of(g, r) for (_, _, _, _, g), r in zip(small, rows)]
                             + [jnp.broadcast_to(loss_row, (8, LANES))], axis=0)
    (g_all,) = _comm_call(_Comm("gather", [g_pack]), "gather_small_grads")
    outs_small = [o[0] for o in _adamw(g_all, pack([s[1] for s in small])[None], pack([s[2] for s in small])[None],
                                       pack([s[3] for s in small])[None], "adamw_small")]
    start = 0
    for (name, w, _, _, _), r in zip(small, rows):
        big[name] = [o[start:start + r].reshape(-1)[:w.size].reshape(w.shape) for o in outs_small]
        start += r
    loss = outs_small[0][start, 0]

    order = ("norm_w", "a_w_in", "a_conv_w", "a_a_log", "a_dt_bias", "a_out_norm_w", "a_w_out", "b_w_in",
             "b_q_norm_w", "b_k_norm_w", "b_rel_bias", "b_w_out")
    return (loss, dx[None]) + tuple(big[n][i] for i in range(4) for n in order)
```

```python
import functools
import math

import jax
import jax.numpy as jnp
from jax import lax
from jax.experimental import pallas as pl
from jax.experimental.pallas import tpu as pltpu

F32 = jnp.float32
BF16 = jnp.bfloat16
MESH_IDS = pl.DeviceIdType.MESH
N_DEV = 8
CHUNK = 64
HEAD_DIM = 128
EPS = 1e-6
CONV_K = 4
LEFT_CHUNKS = 8
REL_CLIP = 256
Q_TILE = LEFT_CHUNKS * CHUNK
ADAM_LR = 0.001
ADAM_B1 = 0.9
ADAM_B2 = 0.999
ADAM_EPS = 1e-08
ADAM_WD = 0.01
ADAM_STEP = 10
NEG_BIG = -1e30
VMEM_LIMIT_BYTES = 56 * 1024 * 1024
HIGHEST = lax.Precision.HIGHEST
ANY = pl.BlockSpec(memory_space=pl.ANY)


def _cparams(*sem):
    return pltpu.CompilerParams(dimension_semantics=tuple(sem), vmem_limit_bytes=VMEM_LIMIT_BYTES)


NN, NT, TN = (((1,), (0,)), ((), ())), (((1,), (1,)), ((), ())), (((0,), (0,)), ((), ()))
BNN, BNT, BTN = (((2,), (1,)), ((0,), (0,))), (((2,), (2,)), ((0,), (0,))), (((1,), (1,)), ((0,), (0,)))


def _dot(a, b, dims, precision=None):
    return lax.dot_general(a, b, dims, preferred_element_type=F32, precision=precision)


def _nn(a, b, precision=None):
    return _dot(a, b, NN, precision)


def _nt(a, b, precision=None):
    return _dot(a, b, NT, precision)


def _tn(a, b, precision=None):
    return _dot(a, b, TN, precision)


def _bf(x):
    return x.astype(BF16)


def _split(x, pieces=2):
    out = []
    for _ in range(pieces - 1):
        hi = x.astype(BF16)
        out.append(hi)
        x = x - hi.astype(F32)
    return out + [x.astype(BF16)]


def _dot3(a, b, dims):
    (ah, al), (bh, bl) = _split(a), _split(b)
    return _dot(ah, bh, dims) + (_dot(ah, bl, dims) + _dot(al, bh, dims))


def _dot_exact(a, b, dims, split_b):
    if split_b:
        a = a.astype(BF16)
        parts = [_dot(a, p, dims) for p in _split(b, 3)]
    else:
        b = b.astype(BF16)
        parts = [_dot(p, b, dims) for p in _split(a, 3)]
    return parts[0] + (parts[1] + parts[2])


def _sigmoid(x):
    return 0.5 * jnp.tanh(0.5 * x) + 0.5


def _silu(x):
    return x * _sigmoid(x)


def _dsilu(x):
    s = _sigmoid(x)
    return s * (1.0 + x * (1.0 - s))


def _my_pos():
    return lax.axis_index("x"), lax.axis_index("y"), lax.axis_index("c")


def _peers(x, y, c):
    def flip(v, f):
        return 1 - v if f else v

    return [(flip(x, kx), flip(y, ky), flip(c, kc)) for kx in (0, 1) for ky in (0, 1) for kc in (0, 1)][1:]


def _lin(p):
    return 4 * p[0] + 2 * p[1] + p[2]


class _Comm:
    def __init__(self, kind, arrays):
        self.kind, self.arrays, self.n = kind, list(arrays), len(arrays)

    def out_shape(self):
        lead = (N_DEV,) if self.kind == "gather" else ()
        return [jax.ShapeDtypeStruct(lead + a.shape, a.dtype) for a in self.arrays]

    def scratch(self):
        return [pltpu.SemaphoreType.DMA((7 * self.n,)), pltpu.SemaphoreType.DMA((7 * self.n,)),
                pltpu.SemaphoreType.DMA((self.n,))]

    def _copies(self, ins, outs, sems, arrivals):
        send_sems, recv_sems, local_sems = sems
        x, y, c = _my_pos()
        me = _lin((x, y, c))
        gather = self.kind == "gather"
        mine = [ins[t] if gather else ins[t].at[me] for t in range(self.n)]
        remote = []
        for k, peer in enumerate(_peers(x, y, c)):
            for t in range(self.n):
                if arrivals:
                    src, dst = mine[t], outs[t].at[_lin(peer)]
                else:
                    src, dst = (ins[t] if gather else ins[t].at[_lin(peer)]), outs[t].at[me]
                remote.append(pltpu.make_async_remote_copy(
                    src_ref=src, dst_ref=dst, send_sem=send_sems.at[k * self.n + t],
                    recv_sem=recv_sems.at[k * self.n + t], device_id=peer, device_id_type=MESH_IDS))
        if arrivals:
            return remote
        return [pltpu.make_async_copy(mine[t], outs[t].at[me], local_sems.at[t]) for t in range(self.n)], remote

    def start(self, ins, outs, sems):
        local, sends = self._copies(ins, outs, sems, False)
        for cp in local + sends:
            cp.start()

    def finish(self, ins, outs, sems):
        for cp in self._copies(ins, outs, sems, True):
            cp.wait_recv()
        local, sends = self._copies(ins, outs, sems, False)
        for cp in sends:
            cp.wait_send()
        for cp in local:
            cp.wait()


def _xor(a, b):
    return a + b - 2 * a * b


class _RoutedGather(_Comm):
    def __init__(self, arrays):
        super().__init__("gather", arrays)

    def _plan(self, outs, sems):
        send_sems, recv_sems, _ = sems
        x, y, c = _my_pos()
        sib, xn, yn, dg = (x, y, 1 - c), (1 - x, y, c), (x, 1 - y, c), (1 - x, 1 - y, c)
        via = (_xor(x, 1 - c), _xor(y, c), c)
        onto = (_xor(x, c), _xor(y, 1 - c), c)
        routes = [(None, sib, sib), (None, xn, xn), (None, yn, yn), (via, onto, dg),
                  (xn, sib, (1 - x, y, 1 - c)), (yn, sib, (x, 1 - y, 1 - c)), (dg, sib, (1 - x, 1 - y, 1 - c))]

        def copy(k, t, src, slot, target):
            return pltpu.make_async_remote_copy(
                src_ref=src, dst_ref=outs[t].at[slot], send_sem=send_sems.at[k * self.n + t],
                recv_sem=recv_sems.at[k * self.n + t], device_id=target, device_id_type=MESH_IDS)

        return (x, y, c), routes, copy

    def start(self, ins, outs, sems):
        me, routes, copy = self._plan(outs, sems)
        for t in range(self.n):
            pltpu.make_async_copy(ins[t], outs[t].at[_lin(me)], sems[2].at[t]).start()
            for k in range(3):
                copy(k, t, ins[t], _lin(me), routes[k][1]).start()

    def finish(self, ins, outs, sems):
        me, routes, copy = self._plan(outs, sems)

        def arrived(k):
            for t in range(self.n):
                copy(k, t, ins[t], _lin(routes[k][2]), me).wait_recv()

        def pass_on(k):
            for t in range(self.n):
                copy(k, t, outs[t].at[_lin(routes[k][0])], _lin(routes[k][0]), routes[k][1]).start()

        arrived(1)
        arrived(2)
        for k in (3, 4, 5):
            pass_on(k)
        arrived(3)
        pass_on(6)
        for k in (0, 4, 5, 6):
            arrived(k)
        for t in range(self.n):
            for k in range(7):
                src = ins[t] if k < 3 else outs[t].at[_lin(routes[k][0])]
                copy(k, t, src, _lin(me), routes[k][1]).wait_send()
            pltpu.make_async_copy(ins[t], outs[t].at[_lin(me)], sems[2].at[t]).wait()


def _comm_call(comm, name):
    n = comm.n

    def body(*refs):
        ins, outs, sems = refs[:n], refs[n:2 * n], refs[2 * n:]
        comm.start(ins, outs, sems)
        comm.finish(ins, outs, sems)

    return pl.pallas_call(
        body, name=name, out_shape=comm.out_shape(), in_specs=[ANY] * n, out_specs=[ANY] * n,
        scratch_shapes=comm.scratch(),
    )(*comm.arrays)


def _grid_call(body, *, name, grid, in_specs, out_specs, out_shape, args, scratch_shapes=(), semantics=None, comm=None):
    if comm is None:
        return pl.pallas_call(
            body, name=name, grid=grid, in_specs=in_specs, out_specs=out_specs, out_shape=out_shape,
            scratch_shapes=list(scratch_shapes), compiler_params=_cparams(*semantics),
        )(*args)
    n_in, n_out, n_sc, n = len(in_specs), len(out_specs), len(scratch_shapes), comm.n

    def full(*refs):
        ins, refs = refs[:n_in], refs[n_in:]
        cins, refs = refs[:n], refs[n:]
        outs, refs = refs[:n_out], refs[n_out:]
        couts, refs = refs[:n], refs[n:]
        scratch, sems = refs[:n_sc], refs[n_sc:]
        ids = [pl.program_id(a) for a in range(len(grid))]
        first = functools.reduce(jnp.logical_and, [i == 0 for i in ids])
        last = functools.reduce(jnp.logical_and, [i == g - 1 for i, g in zip(ids, grid)])

        @pl.when(first)
        def _():
            comm.start(cins, couts, sems)

        body(*ins, *outs, *scratch)

        @pl.when(last)
        def _():
            comm.finish(cins, couts, sems)

    return pl.pallas_call(
        full, name=name, grid=grid, in_specs=list(in_specs) + [ANY] * n, out_specs=list(out_specs) + [ANY] * n,
        out_shape=list(out_shape) + comm.out_shape(), scratch_shapes=list(scratch_shapes) + comm.scratch(),
        compiler_params=_cparams(*(["arbitrary"] * len(grid))),
    )(*(list(args) + comm.arrays))


def _mm(a, b, mode, m, n, k, *, out_dtype, name, tm=1024, tn=1024, tk=2048,
        a_m0=0, a_k0=0, b_n0=0, b_k0=0, res=None, comm=None, loss_target=None):
    tm, tn, tk = min(tm, m), min(tn, n), min(tk, k)
    nm, nn, nk = m // tm, n // tn, k // tk
    assert nm * tm == m and nn * tn == n and nk * tk == k
    am, ak, bn, bk = a_m0 // tm, a_k0 // tk, b_n0 // tn, b_k0 // tk
    assert am * tm == a_m0 and ak * tk == a_k0 and bn * tn == b_n0 and bk * tk == b_k0
    if mode == "tn":
        a_spec = pl.BlockSpec((tk, tm), lambda i, j, q: (q + ak, i + am))
        a_dims = (0,)
    else:
        a_spec = pl.BlockSpec((tm, tk), lambda i, j, q: (i + am, q + ak))
        a_dims = (1,)
    if mode == "nt":
        b_spec = pl.BlockSpec((tn, tk), lambda i, j, q: (j + bn, q + bk))
        b_dims = (1,)
    else:
        b_spec = pl.BlockSpec((tk, tn), lambda i, j, q: (q + bk, j + bn))
        b_dims = (0,)
    o_spec = pl.BlockSpec((tm, tn), lambda i, j, q: (i, j))
    has_res = res is not None
    has_loss = loss_target is not None
    n_in = 2 + has_res + has_loss
    n_out = 3 if has_loss else 1

    def body(*refs):
        a_ref, b_ref = refs[0], refs[1]
        res_ref = refs[2] if has_res else None
        o_ref = refs[n_in]
        p = _dot(a_ref[...], b_ref[...], ((a_dims, b_dims), ((), ())))

        def finish(total):
            if has_res:
                total = total + res_ref[...].astype(F32)
            if not has_loss:
                o_ref[...] = total.astype(out_dtype)
                return
            err = total - refs[n_in - 1][...]
            grad = err * (1.0 / n)
            o_ref[...] = grad
            refs[n_in + 1][...] = grad.astype(BF16)
            l_ref = refs[n_in + 2]
            part = jnp.zeros((1, LANES), F32) + 0.5 * jnp.sum(err * err) * (1.0 / n)
            first = (pl.program_id(0) == 0) & (pl.program_id(1) == 0)

            @pl.when(first)
            def _():
                l_ref[...] = part

            @pl.when(jnp.logical_not(first))
            def _():
                l_ref[...] += part

        if nk == 1:
            finish(p)
        else:
            acc_ref = refs[n_in + n_out]
            q = pl.program_id(2)

            @pl.when(q == 0)
            def _():
                acc_ref[...] = p

            @pl.when(q > 0)
            def _():
                acc_ref[...] += p

            @pl.when(q == nk - 1)
            def _():
                finish(acc_ref[...])

    extra_in = ([res] if has_res else []) + ([loss_target] if has_loss else [])
    if has_loss:
        return _grid_call(
            body, name=name, grid=(nm, nn, nk), in_specs=[a_spec, b_spec] + [o_spec] * len(extra_in),
            out_specs=[o_spec, o_spec, pl.BlockSpec((1, LANES), lambda i, j, q: (0, 0))],
            out_shape=[jax.ShapeDtypeStruct((m, n), F32), jax.ShapeDtypeStruct((m, n), BF16),
                       jax.ShapeDtypeStruct((1, LANES), F32)],
            scratch_shapes=[pltpu.VMEM((tm, tn), F32)] if nk > 1 else [],
            args=[a, b] + extra_in, semantics=("arbitrary", "arbitrary", "arbitrary"))
    out, *carried = _grid_call(
        body, name=name, grid=(nm, nn, nk),
        in_specs=[a_spec, b_spec] + [o_spec] * len(extra_in),
        out_specs=[o_spec], out_shape=[jax.ShapeDtypeStruct((m, n), out_dtype)],
        scratch_shapes=[pltpu.VMEM((tm, tn), F32)] if nk > 1 else [],
        args=[a, b] + extra_in, semantics=("parallel", "parallel", "arbitrary"), comm=comm)
    return out if comm is None else (out, carried)


def _rms_fwd(x, w, name, tr=512, comm=None):
    t, d = x.shape
    tr = min(tr, t)

    def body(x_ref, w_ref, o_ref):
        xv = x_ref[...]
        r = lax.rsqrt(jnp.mean(xv * xv, axis=-1, keepdims=True) + EPS)
        o_ref[...] = (xv * r * w_ref[...]).astype(BF16)

    out, *carried = _grid_call(
        body, name=name, grid=(t // tr,),
        in_specs=[pl.BlockSpec((tr, d), lambda i: (i, 0)), pl.BlockSpec((1, d), lambda i: (0, 0))],
        out_specs=[pl.BlockSpec((tr, d), lambda i: (i, 0))],
        out_shape=[jax.ShapeDtypeStruct((t, d), BF16)], args=(x, w), semantics=("parallel",), comm=comm)
    return out if comm is None else (out, carried)


def _rms_bwd(x, w, dy, dres, name, tr=256):
    t, d = x.shape
    tr = min(tr, t)

    def body(x_ref, w_ref, dy_ref, dres_ref, dx_ref, dxb_ref, dw_ref):
        xv = x_ref[...]
        dyv = dy_ref[...].astype(F32)
        r = lax.rsqrt(jnp.mean(xv * xv, axis=-1, keepdims=True) + EPS)
        gy = dyv * w_ref[...]
        proj = jnp.sum(gy * xv, axis=-1, keepdims=True) * (1.0 / d)
        dx = dres_ref[...] + r * gy - xv * (r * r * r) * proj
        dx_ref[...] = dx
        dxb_ref[...] = dx.astype(BF16)
        part = jnp.sum(dyv * xv * r, axis=0, keepdims=True)

        @pl.when(pl.program_id(0) == 0)
        def _():
            dw_ref[...] = part

        @pl.when(pl.program_id(0) > 0)
        def _():
            dw_ref[...] += part

    row = pl.BlockSpec((tr, d), lambda i: (i, 0))
    vec = pl.BlockSpec((1, d), lambda i: (0, 0))
    return pl.pallas_call(
        body, name=name, grid=(t // tr,),
        in_specs=[row, vec, row, row], out_specs=[row, row, vec],
        out_shape=[jax.ShapeDtypeStruct((t, d), F32), jax.ShapeDtypeStruct((t, d), BF16),
                   jax.ShapeDtypeStruct((1, d), F32)],
        compiler_params=_cparams("arbitrary"),
    )(x, w, dy, dres)


def _adamw(parts, w, m, v, name, tr=128, transposed=False):
    r, c = w.shape
    tr = tr if r % tr == 0 else r
    c1 = 1.0 - ADAM_B1 ** ADAM_STEP
    c2 = 1.0 - ADAM_B2 ** ADAM_STEP

    def body(p_ref, w_ref, m_ref, v_ref, g_ref, d_ref, nm_ref, nv_ref):
        g = p_ref[0].astype(F32)
        for s in range(1, N_DEV):
            g = g + p_ref[s].astype(F32)
        if transposed:
            i, j = lax.broadcasted_iota(jnp.int32, (tr, tr), 0), lax.broadcasted_iota(jnp.int32, (tr, tr), 1)
            g = _dot_exact(jnp.where(i == j, 1.0, 0.0), g, NT, True)
        nm = ADAM_B1 * m_ref[...] + (1.0 - ADAM_B1) * g
        nv = ADAM_B2 * v_ref[...] + (1.0 - ADAM_B2) * (g * g)
        m_hat = nm / c1
        v_hat = nv / c2
        g_ref[...] = g
        d_ref[...] = -ADAM_LR * (m_hat / (jnp.sqrt(v_hat) + ADAM_EPS) + ADAM_WD * w_ref[...])
        nm_ref[...] = nm
        nv_ref[...] = nv

    blk = pl.BlockSpec((tr, c), lambda i: (i, 0))
    p_spec = (pl.BlockSpec((N_DEV, c, tr), lambda i: (0, 0, i)) if transposed
              else pl.BlockSpec((N_DEV, tr, c), lambda i: (0, i, 0)))
    return pl.pallas_call(
        body, name=name, grid=(r // tr,),
        in_specs=[p_spec, blk, blk, blk],
        out_specs=[blk] * 4, out_shape=[jax.ShapeDtypeStruct((r, c), F32)] * 4,
        compiler_params=_cparams("parallel"),
    )(parts, w, m, v)


ROW_TILE, ROW_HEADS = 256, 16
CONV_ROWS, CONV_HEADS = 512, 8


def _window(dest, inner, n_in, out_index):
    if dest is None:
        return inner, 0, [], [], {}
    buf, col0, total = dest
    if buf is None:
        return total, col0, [], [], {}
    return total, col0, [buf], [ANY], {n_in: out_index}


def _skip_ref(body, at, count):
    return body if count == 0 else (lambda *refs: body(*refs[:at], *refs[at + count:]))


def _heads_of(x, nh):
    return [x[:, h * HEAD_DIM:(h + 1) * HEAD_DIM] for h in range(nh)]


def _headnorm_fwd(proj, w, col0, inner, name, tr=ROW_TILE, hb=ROW_HEADS):
    t = proj.shape[0]
    tr = min(tr, t)
    hb = min(hb, inner // HEAD_DIM)
    wc = hb * HEAD_DIM
    c0 = col0 // wc

    def body(x_ref, w_ref, o_ref):
        outs = []
        for xh in _heads_of(x_ref[...], hb):
            r = lax.rsqrt(jnp.mean(xh * xh, axis=-1, keepdims=True) + EPS)
            outs.append((xh * r * w_ref[...]).astype(BF16))
        o_ref[...] = jnp.concatenate(outs, axis=1)

    return pl.pallas_call(
        body, name=name, grid=(t // tr, inner // wc),
        in_specs=[pl.BlockSpec((tr, wc), lambda i, j: (i, j + c0)), pl.BlockSpec((1, HEAD_DIM), lambda i, j: (0, 0))],
        out_specs=pl.BlockSpec((tr, wc), lambda i, j: (i, j)),
        out_shape=jax.ShapeDtypeStruct((t, inner), BF16),
        compiler_params=_cparams("parallel", "parallel"),
    )(proj, w)


def _headnorm_bwd(dy, proj, w, col0, inner, name, tr=ROW_TILE, hb=ROW_HEADS, dest=None):
    t = proj.shape[0]
    tr = min(tr, t)
    hb = min(hb, inner // HEAD_DIM)
    wc = hb * HEAD_DIM
    c0 = col0 // wc
    width, out0, more, more_specs, aliases = _window(dest, inner, 3, 0)

    def body(dy_ref, x_ref, w_ref, dx_ref, dw_ref):
        outs = []
        part = jnp.zeros((1, HEAD_DIM), F32)
        for dyh, xh in zip(_heads_of(dy_ref[...], hb), _heads_of(x_ref[...], hb)):
            r = lax.rsqrt(jnp.mean(xh * xh, axis=-1, keepdims=True) + EPS)
            gy = dyh * w_ref[...]
            pr = jnp.sum(gy * xh, axis=-1, keepdims=True) * (1.0 / HEAD_DIM)
            outs.append((r * gy - xh * (r * r * r) * pr).astype(BF16))
            part = part + jnp.sum(dyh * xh * r, axis=0, keepdims=True)
        dx_ref[...] = jnp.concatenate(outs, axis=1)
        first = (pl.program_id(0) == 0) & (pl.program_id(1) == 0)

        @pl.when(first)
        def _():
            dw_ref[...] = part

        @pl.when(jnp.logical_not(first))
        def _():
            dw_ref[...] += part

    blk = pl.BlockSpec((tr, wc), lambda i, j: (i, j))
    return pl.pallas_call(
        _skip_ref(body, 3, len(more)), name=name, grid=(t // tr, inner // wc),
        in_specs=[blk, pl.BlockSpec((tr, wc), lambda i, j: (i, j + c0)),
                  pl.BlockSpec((1, HEAD_DIM), lambda i, j: (0, 0))] + more_specs,
        out_specs=[pl.BlockSpec((tr, wc), lambda i, j: (i, j + out0 // wc)),
                   pl.BlockSpec((1, HEAD_DIM), lambda i, j: (0, 0))],
        out_shape=[jax.ShapeDtypeStruct((t, width), BF16), jax.ShapeDtypeStruct((1, HEAD_DIM), F32)],
        input_output_aliases=aliases, compiler_params=_cparams("arbitrary", "arbitrary"),
    )(dy, proj, w, *more)


def _gate_fwd(o, proj, zcol0, inner, name, norm_w=None, tr=ROW_TILE, hb=ROW_HEADS):
    t = o.shape[0]
    tr = min(tr, t)
    hb = min(hb, inner // HEAD_DIM)
    wc = hb * HEAD_DIM
    c0 = zcol0 // wc
    has_w = norm_w is not None

    def body(*refs):
        o_ref, z_ref = refs[0], refs[1]
        out_ref = refs[2 + has_w]
        outs = []
        for oh, zh in zip(_heads_of(o_ref[...], hb), _heads_of(z_ref[...], hb)):
            if has_w:
                r = lax.rsqrt(jnp.mean(oh * oh, axis=-1, keepdims=True) + EPS)
                oh = oh * r * refs[2][...]
            outs.append((oh * _silu(zh)).astype(BF16))
        out_ref[...] = jnp.concatenate(outs, axis=1)

    blk = pl.BlockSpec((tr, wc), lambda i, j: (i, j))
    vec = pl.BlockSpec((1, HEAD_DIM), lambda i, j: (0, 0))
    return pl.pallas_call(
        body, name=name, grid=(t // tr, inner // wc),
        in_specs=[blk, pl.BlockSpec((tr, wc), lambda i, j: (i, j + c0))] + ([vec] if has_w else []),
        out_specs=blk, out_shape=jax.ShapeDtypeStruct((t, inner), BF16),
        compiler_params=_cparams("parallel", "parallel"),
    )(*([o, proj] + ([norm_w] if has_w else [])))


def _gate_bwd(dg, o, proj, zcol0, inner, name, do_dtype, norm_w=None, tr=ROW_TILE, hb=ROW_HEADS, dest=None):
    t = o.shape[0]
    tr = min(tr, t)
    hb = min(hb, inner // HEAD_DIM)
    wc = hb * HEAD_DIM
    c0 = zcol0 // wc
    has_w = norm_w is not None
    width, out0, more, more_specs, aliases = _window(dest, inner, 3 + has_w, 1)

    def body(*refs):
        dg_ref, o_ref, z_ref = refs[0], refs[1], refs[2]
        do_ref, dz_ref = refs[3 + has_w], refs[4 + has_w]
        dos, dzs = [], []
        part = jnp.zeros((1, HEAD_DIM), F32)
        for dgh, oh, zh in zip(_heads_of(dg_ref[...], hb), _heads_of(o_ref[...], hb), _heads_of(z_ref[...], hb)):
            dy = dgh * _silu(zh)
            if has_w:
                w = refs[3][...]
                r = lax.rsqrt(jnp.mean(oh * oh, axis=-1, keepdims=True) + EPS)
                on = oh * r
                dzs.append((dgh * on * w * _dsilu(zh)).astype(BF16))
                gy = dy * w
                pr = jnp.sum(gy * oh, axis=-1, keepdims=True) * (1.0 / HEAD_DIM)
                dos.append((r * gy - oh * (r * r * r) * pr).astype(do_dtype))
                part = part + jnp.sum(dy * on, axis=0, keepdims=True)
            else:
                dzs.append((dgh * oh * _dsilu(zh)).astype(BF16))
                dos.append(dy.astype(do_dtype))
        do_ref[...] = jnp.concatenate(dos, axis=1)
        dz_ref[...] = jnp.concatenate(dzs, axis=1)
        if has_w:
            dw_ref = refs[6]
            first = (pl.program_id(0) == 0) & (pl.program_id(1) == 0)

            @pl.when(first)
            def _():
                dw_ref[...] = part

            @pl.when(jnp.logical_not(first))
            def _():
                dw_ref[...] += part

    blk = pl.BlockSpec((tr, wc), lambda i, j: (i, j))
    vec = pl.BlockSpec((1, HEAD_DIM), lambda i, j: (0, 0))
    return pl.pallas_call(
        _skip_ref(body, 3 + has_w, len(more)), name=name, grid=(t // tr, inner // wc),
        in_specs=[blk, blk, pl.BlockSpec((tr, wc), lambda i, j: (i, j + c0))] + ([vec] if has_w else []) + more_specs,
        out_specs=[blk, pl.BlockSpec((tr, wc), lambda i, j: (i, j + out0 // wc))] + ([vec] if has_w else []),
        out_shape=[jax.ShapeDtypeStruct((t, inner), do_dtype), jax.ShapeDtypeStruct((t, width), BF16)]
        + ([jax.ShapeDtypeStruct((1, HEAD_DIM), F32)] if has_w else []),
        input_output_aliases=aliases, compiler_params=_cparams("arbitrary", "arbitrary"),
    )(*([dg, o, proj] + ([norm_w] if has_w else []) + more))


N_REL = 2 * REL_CLIP + 1
REL_PAD = 640
WIN = 2 * Q_TILE


def _diag_onehot():
    i = lax.broadcasted_iota(jnp.int32, (REL_PAD, WIN), 0)
    j = lax.broadcasted_iota(jnp.int32, (REL_PAD, WIN), 1)
    rel = jnp.where(j < Q_TILE + CHUNK, Q_TILE - j, Q_TILE + WIN - j)
    used = (j < Q_TILE + CHUNK) | (j > WIN - CHUNK)
    idx = jnp.clip(rel, -REL_CLIP, REL_CLIP) + REL_CLIP
    return jnp.where(used & (i == idx), 1.0, 0.0).astype(F32)


def _band_mask():
    r = lax.broadcasted_iota(jnp.int32, (Q_TILE, WIN), 0) // CHUNK
    kc = lax.broadcasted_iota(jnp.int32, (Q_TILE, WIN), 1) // CHUNK - LEFT_CHUNKS
    return (kc <= r) & (kc >= r - LEFT_CHUNKS)


def _bias_tiles(rel_bias_pad, name):
    nh = rel_bias_pad.shape[0]

    def body(rb_ref, o_ref):
        dvec = _nn(rb_ref[...], _diag_onehot(), HIGHEST)[0:1, :]
        tile = pltpu.roll(jnp.broadcast_to(dvec, (Q_TILE, WIN)), 0, 1, stride=1, stride_axis=0)
        o_ref[...] = jnp.where(_band_mask(), tile, NEG_BIG)

    return pl.pallas_call(
        body, name=name, grid=(nh,),
        in_specs=[pl.BlockSpec((None, 8, REL_PAD), lambda h: (h, 0, 0))],
        out_specs=pl.BlockSpec((None, Q_TILE, WIN), lambda h: (h, 0, 0)),
        out_shape=jax.ShapeDtypeStruct((nh, Q_TILE, WIN), F32),
        compiler_params=_cparams("parallel"),
    )(rel_bias_pad)


def _bias_grad(dtile, name):
    nh = dtile.shape[0]

    def body(d_ref, o_ref):
        ri = lax.broadcasted_iota(jnp.int32, (Q_TILE, Q_TILE), 0)
        ci = lax.broadcasted_iota(jnp.int32, (Q_TILE, Q_TILE), 1)
        flip = jnp.where(ri + ci == Q_TILE - 1, 1.0, 0.0).astype(F32)
        rev = _dot_exact(flip, d_ref[...], NN, True)
        rolled = pltpu.roll(rev, WIN - (Q_TILE - 1), 1, stride=1, stride_axis=0)
        diag = jnp.broadcast_to(jnp.sum(rolled, axis=0, keepdims=True), (8, WIN))
        o_ref[...] = _nt(diag, _diag_onehot(), HIGHEST)

    return pl.pallas_call(
        body, name=name, grid=(nh,),
        in_specs=[pl.BlockSpec((None, Q_TILE, WIN), lambda h: (h, 0, 0))],
        out_specs=pl.BlockSpec((None, 8, REL_PAD), lambda h: (h, 0, 0)),
        out_shape=jax.ShapeDtypeStruct((nh, 8, REL_PAD), F32),
        compiler_params=_cparams("parallel"),
    )(dtile)


GROUP = 2 * CHUNK
BAND = Q_TILE + GROUP


def _band_rows(r0_ref, r1_ref, g):
    return jnp.concatenate([r0_ref[GROUP * g:, :], r1_ref[:GROUP * (g + 1), :]], axis=0)


N_GROUPS = Q_TILE // GROUP
ATTN_BATCH = 4


GROUP_SETS = tuple(tuple(range(g, g + ATTN_BATCH)) for g in range(0, N_GROUPS, ATTN_BATCH))


def _groups(ref, gs):
    return jnp.stack([ref[GROUP * g:GROUP * (g + 1), :] for g in gs])


def _bands(r0_ref, r1_ref, gs):
    return jnp.stack([_band_rows(r0_ref, r1_ref, g) for g in gs])


def _group_probs(q, kw, b_ref, gs, first_tile):
    bias = jnp.stack([b_ref[GROUP * g:GROUP * (g + 1), GROUP * g:GROUP * g + BAND] for g in gs])
    s = _dot(q, kw, BNT) * (HEAD_DIM ** -0.5) + bias
    col = jnp.stack([lax.broadcasted_iota(jnp.int32, (GROUP, BAND), 1) + GROUP * g for g in gs])
    s = jnp.where(first_tile & (col < Q_TILE), NEG_BIG, s)
    p = jnp.exp(s - jnp.max(s, axis=-1, keepdims=True))
    return p * (1.0 / jnp.sum(p, axis=-1, keepdims=True))


def _attn_fwd(q, k, v, v_col0, bias, name):
    t, inner = q.shape
    vh = v_col0 // HEAD_DIM
    nh, nt = inner // HEAD_DIM, t // Q_TILE

    def body(q_ref, k0_ref, k1_ref, v0_ref, v1_ref, b_ref, o_ref):
        for gs in GROUP_SETS:
            p = _group_probs(_groups(q_ref, gs), _bands(k0_ref, k1_ref, gs), b_ref, gs, pl.program_id(1) == 0)
            o = _dot(_bf(p), _bf(_bands(v0_ref, v1_ref, gs)), BNN)
            for n, g in enumerate(gs):
                o_ref[GROUP * g:GROUP * (g + 1), :] = o[n]

    cur = pl.BlockSpec((Q_TILE, HEAD_DIM), lambda h, i: (i, h))
    prev = pl.BlockSpec((Q_TILE, HEAD_DIM), lambda h, i: (jnp.maximum(i - 1, 0), h))
    v_cur = pl.BlockSpec((Q_TILE, HEAD_DIM), lambda h, i: (i, h + vh))
    v_prev = pl.BlockSpec((Q_TILE, HEAD_DIM), lambda h, i: (jnp.maximum(i - 1, 0), h + vh))
    return pl.pallas_call(
        body, name=name, grid=(nh, nt),
        in_specs=[cur, prev, cur, v_prev, v_cur, pl.BlockSpec((None, Q_TILE, WIN), lambda h, i: (h, 0, 0))],
        out_specs=cur, out_shape=jax.ShapeDtypeStruct((t, inner), F32),
        compiler_params=_cparams("parallel", "parallel"),
    )(q, k, k, v, v, bias)


def _attn_bwd(q, k, v, v_col0, do, bias, name, dest=None):
    t, inner = q.shape
    nh, nt = inner // HEAD_DIM, t // Q_TILE
    scale = HEAD_DIM ** -0.5

    def body(q_ref, k0_ref, k1_ref, v0_ref, v1_ref, do_ref, b_ref, dq_ref, dk_ref, dv_ref, db_ref,
             ck_ref, cv_ref, wk_ref, wv_ref):
        i = pl.program_id(1)

        @pl.when(i == 0)
        def _():
            ck_ref[...] = jnp.zeros(blk, F32)
            cv_ref[...] = jnp.zeros(blk, F32)
            db_ref[...] = jnp.zeros((Q_TILE, WIN), F32)

        @pl.when(i < nt)
        def _():
            wk_ref[...] = jnp.zeros((WIN, HEAD_DIM), F32)
            wv_ref[...] = jnp.zeros((WIN, HEAD_DIM), F32)
            for gs in GROUP_SETS:
                qv, dov = _groups(q_ref, gs), _groups(do_ref, gs)
                kw, vw = _bands(k0_ref, k1_ref, gs), _bf(_bands(v0_ref, v1_ref, gs))
                p = _group_probs(qv, kw, b_ref, gs, i == 0)
                dp = _dot(dov, vw, BNT)
                ds = p * (dp - jnp.sum(p * dp, axis=-1, keepdims=True))
                pb, dsb = _bf(p), _bf(ds)
                dq = _dot(dsb, kw, BNN) * scale
                dkw = _dot(dsb, qv, BTN) * scale
                dvw = _dot(pb, dov, BTN)
                for n, g in enumerate(gs):
                    rows, cols = slice(GROUP * g, GROUP * (g + 1)), slice(GROUP * g, GROUP * g + BAND)
                    db_ref[rows, cols] += ds[n]
                    dq_ref[rows, :] = dq[n]
                    wk_ref[cols, :] += dkw[n]
                    wv_ref[cols, :] += dvw[n]
            dk_ref[...] = ck_ref[...] + wk_ref[:Q_TILE, :]
            dv_ref[...] = (cv_ref[...] + wv_ref[:Q_TILE, :]).astype(BF16)
            ck_ref[...] = wk_ref[Q_TILE:, :]
            cv_ref[...] = wv_ref[Q_TILE:, :]

        @pl.when(i == nt)
        def _():
            dk_ref[...] = ck_ref[...]
            dv_ref[...] = cv_ref[...].astype(BF16)

    blk = (Q_TILE, HEAD_DIM)
    cur = pl.BlockSpec(blk, lambda h, i: (jnp.minimum(i, nt - 1), h))
    prev = pl.BlockSpec(blk, lambda h, i: (jnp.clip(i - 1, 0, nt - 1), h))
    lag = pl.BlockSpec(blk, lambda h, i: (jnp.maximum(i - 1, 0), h))
    vh = v_col0 // HEAD_DIM
    v_cur = pl.BlockSpec(blk, lambda h, i: (jnp.minimum(i, nt - 1), h + vh))
    v_prev = pl.BlockSpec(blk, lambda h, i: (jnp.clip(i - 1, 0, nt - 1), h + vh))
    tile = pl.BlockSpec((None, Q_TILE, WIN), lambda h, i: (h, 0, 0))
    width, out0, more, more_specs, aliases = _window(dest, inner, 7, 2)
    return pl.pallas_call(
        _skip_ref(body, 7, len(more)), name=name, grid=(nh, nt + 1),
        in_specs=[cur, prev, cur, v_prev, v_cur, cur, tile] + more_specs,
        out_specs=[cur, lag, pl.BlockSpec(blk, lambda h, i: (jnp.maximum(i - 1, 0), h + out0 // HEAD_DIM)), tile],
        out_shape=[jax.ShapeDtypeStruct((t, inner), F32)] * 2 + [jax.ShapeDtypeStruct((t, width), BF16),
                                                                 jax.ShapeDtypeStruct((nh, Q_TILE, WIN), F32)],
        scratch_shapes=[pltpu.VMEM(blk, F32), pltpu.VMEM(blk, F32),
                        pltpu.VMEM((WIN, HEAD_DIM), F32), pltpu.VMEM((WIN, HEAD_DIM), F32)],
        input_output_aliases=aliases, compiler_params=_cparams("arbitrary", "arbitrary"),
    )(q, k, k, v, v, do, bias, *more)


def _pad_rel_bias(rel_bias):
    nh = rel_bias.shape[0]
    return jnp.broadcast_to(jnp.pad(rel_bias, ((0, 0), (0, REL_PAD - N_REL)))[:, None, :], (nh, 8, REL_PAD))


def _layer_b_fwd(h1, nw, w_in_t, qw, kw, bias, w_out, target):
    t, d = h1.shape
    inner = w_out.shape[0]
    hn = _rms_fwd(h1, nw, "b_rms")
    proj = _mm(hn, w_in_t, "nt", t, 4 * inner, d, out_dtype=F32, name="b_proj")
    qn = _headnorm_fwd(proj, qw, 0, inner, "b_qnorm")
    kn = _headnorm_fwd(proj, kw, inner, inner, "b_knorm")
    o = _attn_fwd(qn, kn, proj, 2 * inner, bias, "b_attn")
    g = _gate_fwd(o, proj, 3 * inner, inner, "b_gate")
    loss_parts = _mm(g, w_out, "nn", t, d, inner, out_dtype=F32, name="b_out", res=h1, loss_target=target)
    return loss_parts, (hn, proj, qn, kn, o, g)


def _layer_b_bwd(dh2, dh2b, h1, nw, w_in_t, qw, kw, bias, w_out, saved):
    hn, proj, qn, kn, o, g = saved
    t, d = h1.shape
    inner = w_out.shape[0]
    dg = _mm(dh2b, w_out, "nt", t, inner, d, out_dtype=F32, name="b_dgate")
    dw_out = _mm(g, dh2b, "tn", inner, d, t, out_dtype=BF16, name="b_dwout")
    do, dproj = _gate_bwd(dg, o, proj, 3 * inner, inner, "b_gate_bwd", BF16, dest=(None, 3 * inner, 4 * inner))
    dq, dk, dproj, dtile = _attn_bwd(qn, kn, proj, 2 * inner, do, bias, "b_attn_bwd",
                                     dest=(dproj, 2 * inner, 4 * inner))
    dproj, dqw = _headnorm_bwd(dq, proj, qw, 0, inner, "b_qnorm_bwd", dest=(dproj, 0, 4 * inner))
    dproj, dkw = _headnorm_bwd(dk, proj, kw, inner, inner, "b_knorm_bwd", dest=(dproj, inner, 4 * inner))
    dhn = _mm(dproj, w_in_t, "nn", t, d, 4 * inner, out_dtype=F32, name="b_dhn")
    dw_in_t = _mm(dproj, hn, "tn", 4 * inner, d, t, out_dtype=BF16, name="b_dwin")
    dh1, dh1b, dnw = _rms_bwd(h1, nw, dhn, dh2, "b_rms_bwd")
    drb = _bias_grad(dtile, "b_bias_grad")[:, 0, :N_REL]
    return dh1, dh1b, dnw, dw_in_t, dqw, dkw, drb, dw_out


LANES = 128


def _softplus(x):
    return jnp.maximum(x, 0.0) + jnp.log1p(jnp.exp(-jnp.abs(x)))


def _gates_fwd(ab, alog_row, dt_row, nh, name, tr=1024):
    t = ab.shape[0]
    tr = min(tr, t)

    def body(x_ref, al_ref, dt_ref, o_ref):
        x = x_ref[...]
        lane = lax.broadcasted_iota(jnp.int32, x.shape, 1)
        g = -jnp.exp(al_ref[...]) * _softplus(x + dt_ref[...])
        o_ref[...] = jnp.where(lane < nh, g, jnp.where(lane < 2 * nh, _sigmoid(x), 0.0))

    row = pl.BlockSpec((tr, LANES), lambda i: (i, 0))
    vec = pl.BlockSpec((1, LANES), lambda i: (0, 0))
    return pl.pallas_call(
        body, name=name, grid=(t // tr,), in_specs=[row, vec, vec], out_specs=row,
        out_shape=jax.ShapeDtypeStruct((t, LANES), F32), compiler_params=_cparams("parallel"),
    )(ab, alog_row, dt_row)


def _gates_bwd(ab, alog_row, dt_row, dgates, nh, name, tr=1024):
    t = ab.shape[0]
    tr = min(tr, t)
    npart = dgates.shape[0]

    def body(x_ref, al_ref, dt_ref, dg_ref, dx_ref, s_ref):
        x = x_ref[...]
        lane = lax.broadcasted_iota(jnp.int32, x.shape, 1)
        dgt = dg_ref[0]
        for p in range(1, npart):
            dgt = dgt + dg_ref[p]
        ea = jnp.exp(al_ref[...])
        xa = x + dt_ref[...]
        da = jnp.where(lane < nh, dgt * (-ea) * _sigmoid(xa), 0.0)
        beta = _sigmoid(x)
        db = jnp.where((lane >= nh) & (lane < 2 * nh), dgt * beta * (1.0 - beta), 0.0)
        dx_ref[...] = (da + db).astype(BF16)
        dal = jnp.sum(jnp.where(lane < nh, dgt * (-ea) * _softplus(xa), 0.0), axis=0, keepdims=True)
        ddt = jnp.sum(da, axis=0, keepdims=True)
        r8 = lax.broadcasted_iota(jnp.int32, (8, LANES), 0)
        part = jnp.where(r8 == 0, dal, jnp.where(r8 == 1, ddt, 0.0))

        @pl.when(pl.program_id(0) == 0)
        def _():
            s_ref[...] = part

        @pl.when(pl.program_id(0) > 0)
        def _():
            s_ref[...] += part

    row = pl.BlockSpec((tr, LANES), lambda i: (i, 0))
    vec = pl.BlockSpec((1, LANES), lambda i: (0, 0))
    return pl.pallas_call(
        body, name=name, grid=(t // tr,),
        in_specs=[row, vec, vec, pl.BlockSpec((npart, tr, LANES), lambda i: (0, i, 0))],
        out_specs=[row, pl.BlockSpec((8, LANES), lambda i: (0, 0))],
        out_shape=[jax.ShapeDtypeStruct((t, LANES), BF16), jax.ShapeDtypeStruct((8, LANES), F32)],
        compiler_params=_cparams("arbitrary"),
    )(ab, alog_row, dt_row, dgates)


HALO = 8


def _conv_taps(ext, w, rows):
    acc = ext[HALO:HALO + rows] * w[CONV_K - 1:CONV_K]
    for s in range(1, CONV_K):
        acc = acc + pltpu.roll(ext, s, 0)[HALO:HALO + rows] * w[CONV_K - 1 - s:CONV_K - s]
    return acc


def _conv_fwd(proj, conv_w, col0, inner, mode, name, tt=CONV_ROWS, hb=CONV_HEADS):
    t = proj.shape[0]
    tt = min(tt, t)
    hb = min(hb, inner // HEAD_DIM)
    wc = hb * HEAD_DIM
    c0 = col0 // wc
    hpb = tt // HALO

    def body(x_ref, halo_ref, w_ref, o_ref):
        halo = jnp.where(pl.program_id(1) == 0, 0.0, halo_ref[...])
        s = _silu(_conv_taps(jnp.concatenate([halo, x_ref[...]], axis=0), w_ref[...], tt))
        if mode == "v":
            o_ref[...] = s
        else:
            mul = HEAD_DIM ** -0.5 if mode == "q" else 1.0
            o_ref[...] = jnp.concatenate(
                [sh * (lax.rsqrt(jnp.sum(sh * sh, axis=-1, keepdims=True) + EPS) * mul) for sh in _heads_of(s, hb)], axis=1)

    return pl.pallas_call(
        body, name=name, grid=(inner // wc, t // tt),
        in_specs=[pl.BlockSpec((tt, wc), lambda j, i: (i, j + c0)),
                  pl.BlockSpec((HALO, wc), lambda j, i: (jnp.maximum(i * hpb - 1, 0), j + c0)),
                  pl.BlockSpec((CONV_K, wc), lambda j, i: (0, j + c0))],
        out_specs=pl.BlockSpec((tt, wc), lambda j, i: (i, j)),
        out_shape=jax.ShapeDtypeStruct((t, inner), F32),
        compiler_params=_cparams("parallel", "parallel"),
    )(proj, proj, conv_w)


def _conv_bwd(dy, proj, conv_w, col0, inner, mode, name, tt=CONV_ROWS, hb=CONV_HEADS, dest=None):
    t = proj.shape[0]
    tt = min(tt, t)
    nt = t // tt
    hb = min(hb, inner // HEAD_DIM)
    wc = hb * HEAD_DIM
    c0 = col0 // wc
    hpb = tt // HALO
    rows = tt + HALO

    def body(dy_ref, dyn_ref, x_ref, xp_ref, xn_ref, w_ref, dx_ref, dw_ref):
        i = pl.program_id(1)
        w = w_ref[...]
        xprev = jnp.where(i == 0, 0.0, xp_ref[...])
        ext = jnp.concatenate([xprev, x_ref[...], xn_ref[...]], axis=0)
        c = _conv_taps(ext, w, rows)
        dyv = jnp.concatenate([dy_ref[...], jnp.where(i == nt - 1, 0.0, dyn_ref[...])], axis=0)
        sg = _sigmoid(c)
        s = c * sg
        if mode == "v":
            ds = dyv
        else:
            mul = HEAD_DIM ** -0.5 if mode == "q" else 1.0
            parts = []
            for dyh, sh in zip(_heads_of(dyv, hb), _heads_of(s, hb)):
                r = lax.rsqrt(jnp.sum(sh * sh, axis=-1, keepdims=True) + EPS)
                parts.append(mul * (r * dyh - sh * (r * r * r) * jnp.sum(dyh * sh, axis=-1, keepdims=True)))
            ds = jnp.concatenate(parts, axis=1)
        dc = ds * (sg * (1.0 + c * (1.0 - sg)))
        dx = dc[:tt] * w[CONV_K - 1:CONV_K]
        for sft in range(1, CONV_K):
            dx = dx + pltpu.roll(dc, rows - sft, 0)[:tt] * w[CONV_K - 1 - sft:CONV_K - sft]
        dx_ref[...] = dx.astype(BF16)
        r8 = lax.broadcasted_iota(jnp.int32, (8, wc), 0)
        part = jnp.zeros((8, wc), F32)
        for sft in range(CONV_K):
            xs = ext[HALO:HALO + tt] if sft == 0 else pltpu.roll(ext, sft, 0)[HALO:HALO + tt]
            part = part + jnp.where(r8 == CONV_K - 1 - sft, jnp.sum(dc[:tt] * xs, axis=0, keepdims=True), 0.0)

        @pl.when(i == 0)
        def _():
            dw_ref[...] = part

        @pl.when(i > 0)
        def _():
            dw_ref[...] += part

    cur = lambda off: pl.BlockSpec((tt, wc), lambda j, i: (i, j + off))
    nxt = lambda off: pl.BlockSpec((HALO, wc), lambda j, i: (jnp.minimum((i + 1) * hpb, t // HALO - 1), j + off))
    width, out0, more, more_specs, aliases = _window(dest, inner, 6, 0)
    return pl.pallas_call(
        _skip_ref(body, 6, len(more)), name=name, grid=(inner // wc, nt),
        in_specs=[cur(0), nxt(0), cur(c0),
                  pl.BlockSpec((HALO, wc), lambda j, i: (jnp.maximum(i * hpb - 1, 0), j + c0)), nxt(c0),
                  pl.BlockSpec((CONV_K, wc), lambda j, i: (0, j + c0))] + more_specs,
        out_specs=[pl.BlockSpec((tt, wc), lambda j, i: (i, j + out0 // wc)),
                   pl.BlockSpec((8, wc), lambda j, i: (0, j))],
        out_shape=[jax.ShapeDtypeStruct((t, width), BF16), jax.ShapeDtypeStruct((8, inner), F32)],
        input_output_aliases=aliases, compiler_params=_cparams("parallel", "arbitrary"),
    )(dy, dy, proj, proj, proj, conv_w, *more)


GDN_HB = 4
GDN_NB = 8
SCAN_HB = 16
SCAN_NB = 2


def _iota2(n, m):
    return lax.broadcasted_iota(jnp.int32, (n, m), 0), lax.broadcasted_iota(jnp.int32, (n, m), 1)


def _head_select(first_head, hb, lane0):
    r, lane = _iota2(8, LANES)
    return jnp.where((r < hb) & (lane == lane0 + first_head + r), 1.0, 0.0).astype(F32)


def _chunk_gates(gt, selg, selb):
    i, j = _iota2(CHUNK, CHUNK)
    gc_all = _dot_exact(jnp.where(j <= i, 1.0, 0.0), gt, NN, True)
    return (_dot_exact(gc_all, selg, NT, False), _dot_exact(selg, gc_all, NT, True),
            _dot_exact(gt, selb, NT, False))


def _decay_terms(gcol, grow):
    i, j = _iota2(CHUNK, CHUNK)
    glast = gcol[:, CHUNK - 1:CHUNK, :]
    decay = jnp.exp(jnp.where(j <= i, gcol - grow, NEG_BIG))
    return jnp.exp(gcol), jnp.exp(glast - gcol), jnp.exp(glast), decay


def _unit_lower_inverse(a):
    i, j = _iota2(CHUNK, CHUNK)
    same16 = (i // 16) == (j // 16)
    same32 = (i // 32) == (j // 32)
    m = jnp.where(same16, -a, 0.0)
    x = jnp.where(i == j, 1.0, 0.0) + m
    for _ in range(3):
        m = _dot3(m, m, BNN)
        x = x + _dot3(x, m, BNN)
    for off in (jnp.where(same32 & jnp.logical_not(same16), a, 0.0), jnp.where(same32, 0.0, a)):
        x = x - _dot3(_dot3(x, off, BNN), x, BNN)
    return x


def _unit_inputs(refs, g_ref, selg, selb, hb, nb):
    units = [(c, h) for c in range(nb) for h in range(hb)]
    rs = lambda c: slice(c * CHUNK, (c + 1) * CHUNK)
    cs = lambda h: slice(h * HEAD_DIM, (h + 1) * HEAD_DIM)
    gates = [_chunk_gates(g_ref[rs(c), :], selg, selb) for c in range(nb)]
    stacked = [jnp.stack([r[rs(c), cs(h)] for c, h in units]) for r in refs]
    gcol = jnp.stack([gates[c][0][:, h:h + 1] for c, h in units])
    grow = jnp.stack([gates[c][1][h:h + 1, :] for c, h in units])
    bcol = jnp.stack([gates[c][2][:, h:h + 1] for c, h in units])
    return units, rs, cs, stacked, gcol, grow, bcol


def _gdn_specs(nh, inner, t, heads=GDN_HB, chunks=GDN_NB):
    hb, nb = min(heads, nh), chunks
    rows = nb * CHUNK
    wide = pl.BlockSpec((rows, hb * HEAD_DIM), lambda g, n: (n, g))
    sq = pl.BlockSpec((hb, rows, CHUNK), lambda g, n: (g, n, 0))
    gts = pl.BlockSpec((rows, LANES), lambda g, n: (n, 0))
    glb = pl.BlockSpec((nb * 8, hb * HEAD_DIM), lambda g, n: (n, g))
    return hb, nb, rows, wide, sq, gts, glb


def _gdn_intra_fwd(q, k, v, gates, nh, name, comm=None):
    t, inner = q.shape
    hb, nb, rows, wide, sq, gts, glb = _gdn_specs(nh, inner, t)

    def body(q_ref, k_ref, v_ref, g_ref, qe_ref, kel_ref, wb_ref, w_ref, u_ref, qk_ref, tm_ref, gl_ref):
        first = pl.program_id(0) * hb
        selg, selb = _head_select(first, hb, 0), _head_select(first, hb, nh)
        i, j = _iota2(CHUNK, CHUNK)
        units, rs, cs, (qv, kv, vv), gcol, grow, bcol = _unit_inputs(
            (q_ref, k_ref, v_ref), g_ref, selg, selb, hb, nb)
        e, el, gl, decay = _decay_terms(gcol, grow)
        kb = kv * bcol
        qbf, kbf = _bf(qv), _bf(kv)
        a = jnp.where(j < i, _dot(_bf(kb), kbf, BNT) * decay, 0.0)
        tm = _unit_lower_inverse(a)
        uw = _dot3(tm, jnp.concatenate([vv * bcol, kb * e], axis=2), BNN)
        qk = _bf(_dot(qbf, kbf, BNT) * decay)
        qe, kel = _bf(qv * e), _bf(kv * el)
        for n, (c, h) in enumerate(units):
            w = uw[n, :, HEAD_DIM:]
            qe_ref[rs(c), cs(h)] = qe[n]
            kel_ref[rs(c), cs(h)] = kel[n]
            wb_ref[rs(c), cs(h)] = _bf(w)
            w_ref[rs(c), cs(h)] = w
            u_ref[rs(c), cs(h)] = uw[n, :, :HEAD_DIM]
            qk_ref[h, rs(c), :] = qk[n]
            tm_ref[h, rs(c), :] = tm[n]
            gl_ref[c * 8:(c + 1) * 8, cs(h)] = jnp.broadcast_to(gl[n], (8, HEAD_DIM))

    big = lambda dt: jax.ShapeDtypeStruct((t, inner), dt)
    return _grid_call(
        body, name=name, grid=(nh // hb, t // rows),
        in_specs=[wide, wide, wide, gts],
        out_specs=[wide] * 5 + [sq, sq, glb],
        out_shape=[big(BF16), big(BF16), big(BF16), big(F32), big(F32),
                   jax.ShapeDtypeStruct((nh, t, CHUNK), BF16), jax.ShapeDtypeStruct((nh, t, CHUNK), F32),
                   jax.ShapeDtypeStruct((t // CHUNK * 8, inner), F32)],
        args=(q, k, v, gates), semantics=("parallel", "parallel"), comm=comm)


def _gdn_scan_fwd(qe, kel, wb, u, qk, glb, nh, name):
    t, inner = u.shape
    hb, nb, rows, wide, sq, _, glb_spec = _gdn_specs(nh, inner, t, SCAN_HB, SCAN_NB)

    def body(qe_ref, kel_ref, wb_ref, u_ref, qk_ref, gl_ref, o_ref, vn_ref, sall_ref, s_ref):
        @pl.when(pl.program_id(1) == 0)
        def _():
            s_ref[...] = jnp.zeros(s_ref.shape, F32)

        cs = lambda h: slice(h * HEAD_DIM, (h + 1) * HEAD_DIM)
        for c in range(nb):
            rs = slice(c * CHUNK, (c + 1) * CHUNK)
            heads = lambda ref: jnp.stack([ref[rs, cs(h)] for h in range(hb)])
            s = s_ref[...]
            sall_ref[c] = s
            sb = _bf(s)
            vn = heads(u_ref) - _dot(heads(wb_ref), sb, BNN)
            vnb = _bf(vn)
            o = _dot(heads(qe_ref), sb, BNN) + _dot(qk_ref[:, rs, :], vnb, BNN)
            gl = jnp.stack([gl_ref[c * 8:c * 8 + 1, cs(h)] for h in range(hb)])
            s_ref[...] = s * gl + _dot(heads(kel_ref), vnb, BTN)
            for h in range(hb):
                vn_ref[rs, cs(h)] = vn[h]
                o_ref[rs, cs(h)] = o[h]

    return pl.pallas_call(
        body, name=name, grid=(nh // hb, t // rows),
        in_specs=[wide, wide, wide, wide, sq, glb_spec],
        out_specs=[wide, wide, pl.BlockSpec((nb, hb, HEAD_DIM, HEAD_DIM), lambda g, n: (n, g, 0, 0))],
        out_shape=[jax.ShapeDtypeStruct((t, inner), F32), jax.ShapeDtypeStruct((t, inner), F32),
                   jax.ShapeDtypeStruct((t // CHUNK, nh, HEAD_DIM, HEAD_DIM), F32)],
        scratch_shapes=[pltpu.VMEM((hb, HEAD_DIM, HEAD_DIM), F32)],
        compiler_params=_cparams("parallel", "arbitrary"),
    )(qe, kel, wb, u, qk, glb)


def _gdn_scan_bwd(do, qe, kel, wb, vn, qk, glb, sall, nh, name, comm=None):
    t, inner = do.shape
    hb, nb, rows, _, _, _, _ = _gdn_specs(nh, inner, t, SCAN_HB, SCAN_NB)
    last = t // rows - 1
    wide = pl.BlockSpec((rows, hb * HEAD_DIM), lambda g, n: (last - n, g))
    sq = pl.BlockSpec((hb, rows, CHUNK), lambda g, n: (g, last - n, 0))
    glb_spec = pl.BlockSpec((nb * 8, hb * HEAD_DIM), lambda g, n: (last - n, g))

    def body(do_ref, qe_ref, kel_ref, wb_ref, vn_ref, qk_ref, gl_ref, sall_ref,
             dvn_ref, dw_ref, dqe_ref, dkel_ref, dqk_ref, dgl_ref, ds_ref):
        @pl.when(pl.program_id(1) == 0)
        def _():
            ds_ref[...] = jnp.zeros(ds_ref.shape, F32)

        cs = lambda h: slice(h * HEAD_DIM, (h + 1) * HEAD_DIM)
        for c in reversed(range(nb)):
            rs = slice(c * CHUNK, (c + 1) * CHUNK)
            heads = lambda ref: jnp.stack([ref[rs, cs(h)] for h in range(hb)])
            ds, s = ds_ref[...], sall_ref[c]
            dsb, sb = _bf(ds), _bf(s)
            dob, vnb = _bf(heads(do_ref)), _bf(heads(vn_ref))
            dvn = _dot(qk_ref[:, rs, :], dob, BTN) + _dot(heads(kel_ref), dsb, BNN)
            dvnb = _bf(dvn)
            dw = -_dot(dvnb, sb, BNT)
            dqe = _dot(dob, sb, BNT)
            dkel = _dot(vnb, dsb, BNT)
            dqk_ref[:, rs, :] = _dot(dob, vnb, BNT)
            dgl = jnp.sum(jnp.sum(ds * s, axis=2, keepdims=True), axis=1, keepdims=True)
            gl = jnp.stack([gl_ref[c * 8:c * 8 + 1, cs(h)] for h in range(hb)])
            ds_ref[...] = ds * gl + _dot(heads(qe_ref), dob, BTN) - _dot(heads(wb_ref), dvnb, BTN)
            for h in range(hb):
                dvn_ref[rs, cs(h)] = dvn[h]
                dw_ref[rs, cs(h)] = dw[h]
                dqe_ref[rs, cs(h)] = dqe[h]
                dkel_ref[rs, cs(h)] = dkel[h]
                dgl_ref[c * 8:(c + 1) * 8, cs(h)] = jnp.broadcast_to(dgl[h], (8, HEAD_DIM))

    big = jax.ShapeDtypeStruct((t, inner), F32)
    return _grid_call(
        body, name=name, grid=(nh // hb, t // rows),
        in_specs=[wide, wide, wide, wide, wide, sq, glb_spec,
                  pl.BlockSpec((nb, hb, HEAD_DIM, HEAD_DIM), lambda g, n: (last - n, g, 0, 0))],
        out_specs=[wide] * 4 + [sq, glb_spec],
        out_shape=[big] * 4 + [jax.ShapeDtypeStruct((nh, t, CHUNK), F32),
                               jax.ShapeDtypeStruct((t // CHUNK * 8, inner), F32)],
        scratch_shapes=[pltpu.VMEM((hb, HEAD_DIM, HEAD_DIM), F32)],
        args=(do, qe, kel, wb, vn, qk, glb, sall), semantics=("parallel", "arbitrary"), comm=comm)


def _gdn_intra_bwd(q, k, v, gates, tm, w, u, dvn, dw, dqe, dkel, dqk, dglb, nh, name, comm=None):
    t, inner = q.shape
    hb, nb, rows, wide, sq, gts, glb = _gdn_specs(nh, inner, t)

    def body(q_ref, k_ref, v_ref, g_ref, tm_ref, w_ref, u_ref, dvn_ref, dw_ref, dqe_ref, dkel_ref, dqk_ref,
             dgl_ref, dq_ref, dk_ref, dv_ref, dg_ref):
        first = pl.program_id(0) * hb
        selg, selb = _head_select(first, hb, 0), _head_select(first, hb, nh)
        i, j = _iota2(CHUNK, CHUNK)
        lane8 = lax.broadcasted_iota(jnp.int32, (CHUNK, 8), 1)
        row = lax.broadcasted_iota(jnp.int32, (CHUNK, 1), 0)
        lower = jnp.where(j <= i, 1.0, 0.0).astype(F32)
        rsum = lambda x: jnp.sum(x, axis=-1, keepdims=True)
        units, rs, cs, (qv, kv, vv, wv, uv, dvn, dw, dqe, dkel), gcol, grow, bcol = _unit_inputs(
            (q_ref, k_ref, v_ref, w_ref, u_ref, dvn_ref, dw_ref, dqe_ref, dkel_ref), g_ref, selg, selb, hb, nb)
        nu = len(units)
        tmv = jnp.stack([tm_ref[h, rs(c), :] for c, h in units])
        dqk = jnp.where(j <= i, jnp.stack([dqk_ref[h, rs(c), :] for c, h in units]), 0.0)
        dgl = jnp.stack([dgl_ref[c * 8:c * 8 + 1, h * HEAD_DIM:h * HEAD_DIM + 1] for c, h in units])
        e, el, gl, decay = _decay_terms(gcol, grow)
        kb = kv * bcol
        qb, kbf, kbb = _bf(qv), _bf(kv), _bf(kb)
        dqkr = _bf(dqk * decay)
        dq = dqe * e + _dot(dqkr, kbf, BNN)
        dk = dkel * el + _dot(dqkr, qb, BTN)
        de = rsum(dqe * qv)
        del_ = rsum(dkel * kv)
        mq = dqk * _dot(qb, kbf, BNT) * decay
        dsol = _dot3(tmv, jnp.concatenate([dvn, dw], axis=2), BTN)
        dvb, dkbe = dsol[:, :, :HEAD_DIM], dsol[:, :, HEAD_DIM:]
        da = -jnp.where(j < i, _dot3(dsol, jnp.concatenate([uv, wv], axis=2), BNT), 0.0)
        dkk = _bf(da * decay)
        ma = da * _dot(kbb, kbf, BNT) * decay
        dkb = dkbe * e + _dot(dkk, kbf, BNN)
        de = de + rsum(dkbe * kb)
        dk = dk + _dot(dkk, kbb, BTN) + dkb * bcol
        dv = dvb * bcol
        dbeta = rsum(dkb * kv) + rsum(dvb * vv)
        m = mq + ma
        ones = jnp.ones((nu, CHUNK, LANES), F32)
        dgc = rsum(m) - _dot_exact(m, ones, BTN, False)[:, :, 0:1] + de * e - del_ * el
        tail = jnp.sum(del_ * el, axis=1, keepdims=True) + dgl * gl
        dgc = dgc + jnp.where(row == CHUNK - 1, tail, 0.0)
        for n, (c, h) in enumerate(units):
            dq_ref[rs(c), cs(h)] = dq[n]
            dk_ref[rs(c), cs(h)] = dk[n]
            dv_ref[rs(c), cs(h)] = dv[n]
        for c in range(nb):
            dgc_cols = jnp.zeros((CHUNK, 8), F32)
            dbeta_cols = jnp.zeros((CHUNK, 8), F32)
            for h in range(hb):
                dgc_cols = jnp.where(lane8 == h, dgc[c * hb + h], dgc_cols)
                dbeta_cols = jnp.where(lane8 == h, dbeta[c * hb + h], dbeta_cols)
            dg_cols = _dot_exact(lower, dgc_cols, TN, True)
            dg_ref[rs(c), :] = _dot_exact(dg_cols, selg, NN, False) + _dot_exact(dbeta_cols, selb, NN, False)

    big = jax.ShapeDtypeStruct((t, inner), F32)
    return _grid_call(
        body, name=name, grid=(nh // hb, t // rows),
        in_specs=[wide, wide, wide, gts, sq, wide, wide, wide, wide, wide, wide, sq, glb],
        out_specs=[wide, wide, wide, pl.BlockSpec((None, rows, LANES), lambda g, n: (g, n, 0))],
        out_shape=[big, big, big, jax.ShapeDtypeStruct((nh // hb, t, LANES), F32)],
        args=(q, k, v, gates, tm, w, u, dvn, dw, dqe, dkel, dqk, dglb), semantics=("parallel", "parallel"),
        comm=comm)


def _layer_a_fwd(x, hn, w_in_t, w_ab_t, conv_w, alog_row, dt_row, onw, nh, comm, w_out_of):
    t, d = x.shape
    inner = nh * HEAD_DIM
    proj = _mm(hn, w_in_t, "nt", t, 4 * inner, d, out_dtype=F32, name="a_proj")
    ab = _mm(hn, w_ab_t, "nt", t, LANES, d, out_dtype=F32, name="a_proj_ab")
    gates = _gates_fwd(ab, alog_row, dt_row, nh, "a_gates")
    q = _conv_fwd(proj, conv_w, 0, inner, "q", "a_conv_q")
    k = _conv_fwd(proj, conv_w, inner, inner, "k", "a_conv_k")
    v = _conv_fwd(proj, conv_w, 2 * inner, inner, "v", "a_conv_v")
    qe, kel, wb, w, u, qk, tm, glb, *carried = _gdn_intra_fwd(q, k, v, gates, nh, "a_intra", comm)
    o, vn, sall = _gdn_scan_fwd(qe, kel, wb, u, qk, glb, nh, "a_scan")
    g = _gate_fwd(o, proj, 3 * inner, inner, "a_gate", norm_w=onw)
    w_out = w_out_of(carried)
    h1 = _mm(g, w_out, "nn", t, d, inner, out_dtype=F32, name="a_out", res=x)
    return h1, (hn, proj, ab, gates, q, k, v, qe, kel, wb, w, u, qk, tm, glb, o, vn, sall, g), w_out, carried


def _layer_a_bwd(dh1, dh1b, x, nw, w_in_t, w_ab_t, conv_w, alog_row, dt_row, onw, w_out, nh, saved, comms_of,
                 own_comm):
    hn, proj, ab, gates, q, k, v, qe, kel, wb, w, u, qk, tm, glb, o, vn, sall, g = saved
    t, d = x.shape
    inner = w_out.shape[0]
    dg = _mm(dh1b, w_out, "nt", t, inner, d, out_dtype=F32, name="a_dgate")
    dw_out = _mm(g, dh1b, "tn", inner, d, t, out_dtype=BF16, name="a_dwout")
    comm_scan, comm_intra = comms_of(dw_out)
    do, dproj, donw = _gate_bwd(dg, o, proj, 3 * inner, inner, "a_gate_bwd", F32, norm_w=onw,
                                dest=(None, 3 * inner, 4 * inner))
    dvn, dw, dqe, dkel, dqk, dglb, *carried_scan = _gdn_scan_bwd(do, qe, kel, wb, vn, qk, glb, sall, nh,
                                                                 "a_scan_bwd", comm_scan)
    dq, dk, dv, dgates, *carried = _gdn_intra_bwd(q, k, v, gates, tm, w, u, dvn, dw, dqe, dkel, dqk, dglb, nh,
                                                  "a_intra_bwd", comm_intra)
    carried = carried_scan + carried
    dproj, dcq = _conv_bwd(dq, proj, conv_w, 0, inner, "q", "a_conv_q_bwd", dest=(dproj, 0, 4 * inner))
    dproj, dck = _conv_bwd(dk, proj, conv_w, inner, inner, "k", "a_conv_k_bwd", dest=(dproj, inner, 4 * inner))
    dproj, dcv = _conv_bwd(dv, proj, conv_w, 2 * inner, inner, "v", "a_conv_v_bwd",
                           dest=(dproj, 2 * inner, 4 * inner))
    dab, dsmall = _gates_bwd(ab, alog_row, dt_row, dgates, nh, "a_gates_bwd")
    dw_in_t = _mm(dproj, hn, "tn", 4 * inner, d, t, out_dtype=BF16, name="a_dwin")
    dw_ab_t = _mm(dab, hn, "tn", LANES, d, t, out_dtype=BF16, name="a_dwin_ab")
    dconv = jnp.concatenate([dcq[:CONV_K], dck[:CONV_K], dcv[:CONV_K]], axis=1)
    dhn = _mm(dab, w_ab_t, "nn", t, d, LANES, out_dtype=F32, name="a_dhn_ab")
    own = own_comm(dw_in_t, dw_ab_t, dconv)
    dhn = _mm(dproj, w_in_t, "nn", t, d, 4 * inner, out_dtype=F32, name="a_dhn", res=dhn, comm=own)
    dhn, carried_own = dhn if own is not None else (dhn, [])
    dx, _, dnw = _rms_bwd(x, nw, dhn, dh1, "a_rms_bwd")
    return dx, dnw, dsmall, donw, carried, carried_own


def _rows_of(a, rows):
    flat = a.reshape(-1)
    return jnp.pad(flat, (0, rows * LANES - flat.shape[0])).reshape(rows, LANES)


def _to_slabs(g, axis):
    shape = g.shape[:axis] + (N_DEV, g.shape[axis] // N_DEV) + g.shape[axis + 1:]
    return jnp.moveaxis(g.reshape(shape), axis, 0)


def _from_slabs(s, axis):
    m = jnp.moveaxis(s, 0, axis)
    return m.reshape(m.shape[:axis] + (m.shape[axis] * m.shape[axis + 1],) + m.shape[axis + 2:])


def kernel(x, norm_w, a_w_in, a_conv_w, a_a_log, a_dt_bias, a_out_norm_w, a_w_out, b_w_in, b_q_norm_w, b_k_norm_w, b_rel_bias, b_w_out, loss_target, m_norm_w, m_a_w_in, m_a_conv_w, m_a_a_log, m_a_dt_bias, m_a_out_norm_w, m_a_w_out, m_b_w_in, m_b_q_norm_w, m_b_k_norm_w, m_b_rel_bias, m_b_w_out, v_norm_w, v_a_w_in, v_a_conv_w, v_a_a_log, v_a_dt_bias, v_a_out_norm_w, v_a_w_out, v_b_w_in, v_b_q_norm_w, v_b_k_norm_w, v_b_rel_bias, v_b_w_out):
    xs, target = x[0], loss_target[0]
    nh = a_a_log.shape[-1]
    inner = N_DEV * a_w_out.shape[1]

    d = xs.shape[1]
    nw0, nw1 = norm_w[0:1], norm_w[1:2]
    hn0, (ga_in, g_conv) = _rms_fwd(
        xs, nw0, "a_rms", comm=_RoutedGather([a_w_in[0].T.astype(BF16), a_conv_w[0]]))
    wa_in_t = ga_in.reshape(-1, d)
    wa_ab_t = jnp.pad(wa_in_t[4 * inner:], ((0, LANES - 2 * nh), (0, 0)))
    conv_w = _from_slabs(g_conv, 1)
    alog_row = jnp.pad(a_a_log, ((0, 0), (0, LANES - nh)))
    dt_row = jnp.pad(a_dt_bias, ((0, 0), (0, LANES - nh)))

    h1, saved_a, wa_out, (gb_in, gb_out, _) = _layer_a_fwd(
        xs, hn0, wa_in_t, wa_ab_t, conv_w, alog_row, dt_row, a_out_norm_w, nh,
        _Comm("gather", [b_w_in[0].T.astype(BF16), b_w_out[0].astype(BF16), a_w_out[0].astype(BF16)]),
        lambda gathered: _from_slabs(gathered[2], 0))
    wb_in_t = gb_in.reshape(-1, d)
    wb_out = _from_slabs(gb_out, 0)
    bias = _bias_tiles(_pad_rel_bias(b_rel_bias[0]), "b_bias_tiles")
    (dh2, dh2b, loss_row), saved_b = _layer_b_fwd(h1, nw1, wb_in_t, b_q_norm_w, b_k_norm_w, bias, wb_out, target)

    dh1, dh1b, dnw1, dwb_in_t, dqw, dkw, drb, dwb_out = _layer_b_bwd(
        dh2, dh2b, h1, nw1, wb_in_t, b_q_norm_w, b_k_norm_w, bias, wb_out, saved_b)

    def exchange_early(dwa_out):
        return (_Comm("exchange", [_to_slabs(dwb_out, 0).astype(BF16), _to_slabs(dwa_out, 0).astype(BF16)]),
                _Comm("exchange", [dwb_in_t.reshape(N_DEV, -1, d).astype(BF16)]))

    def exchange_last(dwa_in_t, dwa_ab_t, dconv):
        full = jnp.concatenate([dwa_in_t, dwa_ab_t[:2 * nh]], axis=0)
        return _Comm("exchange", [full.reshape(N_DEV, -1, d).astype(BF16), _to_slabs(dconv, 1)])

    dx, dnw0, dsmall, donw, (pb_out, pa_out, pb_in), (pa_in, p_conv) = _layer_a_bwd(
        dh1, dh1b, xs, nw0, wa_in_t, wa_ab_t, conv_w, alog_row, dt_row, a_out_norm_w, wa_out, nh, saved_a,
        exchange_early, exchange_last)
    big = {}
    for name, p, w, m, v in (("a_w_in", pa_in, a_w_in, m_a_w_in, v_a_w_in),
                             ("a_w_out", pa_out, a_w_out, m_a_w_out, v_a_w_out),
                             ("b_w_in", pb_in, b_w_in, m_b_w_in, v_b_w_in),
                             ("b_w_out", pb_out, b_w_out, m_b_w_out, v_b_w_out),
                             ("a_conv_w", p_conv, a_conv_w, m_a_conv_w, v_a_conv_w)):
        big[name] = [o[None] for o in _adamw(p, w[0], m[0], v[0], "adamw_" + name,
                                             transposed=name in ("a_w_in", "b_w_in"))]

    small = (("norm_w", norm_w, m_norm_w, v_norm_w, jnp.concatenate([dnw0, dnw1], axis=0)),
             ("a_a_log", a_a_log, m_a_a_log, v_a_a_log, dsmall[0:1, :nh]),
             ("a_dt_bias", a_dt_bias, m_a_dt_bias, v_a_dt_bias, dsmall[1:2, :nh]),
             ("a_out_norm_w", a_out_norm_w, m_a_out_norm_w, v_a_out_norm_w, donw),
             ("b_q_norm_w", b_q_norm_w, m_b_q_norm_w, v_b_q_norm_w, dqw),
             ("b_k_norm_w", b_k_norm_w, m_b_k_norm_w, v_b_k_norm_w, dkw),
             ("b_rel_bias", b_rel_bias, m_b_rel_bias, v_b_rel_bias, drb))
    rows = [8 * (-(-w.size // (8 * LANES))) for _, w, _, _, _ in small]
    pack = lambda arrs: jnp.concatenate([_rows_of(a, r) for a, r in zip(arrs, rows)] + [jnp.zeros((8, LANES), F32)], axis=0)
    g_pack = jnp.concatenate([_rows_of(g, r) for (_, _, _, _, g), r in zip(small, rows)]
                             + [jnp.broadcast_to(loss_row, (8, LANES))], axis=0)
    (g_all,) = _comm_call(_Comm("gather", [g_pack]), "gather_small_grads")
    outs_small = _adamw(g_all, pack([s[1] for s in small]), pack([s[2] for s in small]),
                        pack([s[3] for s in small]), "adamw_small")
    start = 0
    for (name, w, _, _, _), r in zip(small, rows):
        big[name] = [o[start:start + r].reshape(-1)[:w.size].reshape(w.shape) for o in outs_small]
        start += r
    loss = outs_small[0][start, 0]

    order = ("norm_w", "a_w_in", "a_conv_w", "a_a_log", "a_dt_bias", "a_out_norm_w", "a_w_out", "b_w_in",
             "b_q_norm_w", "b_k_norm_w", "b_rel_bias", "b_w_out")
    return (loss, dx[None]) + tuple(big[n][i] for i in range(4) for n in order)
```

```python
import functools

import jax
import jax.numpy as jnp
from jax import lax
from jax.experimental import pallas as pl
from jax.experimental.pallas import tpu as pltpu

F32 = jnp.float32
BF16 = jnp.bfloat16
MESH_IDS = pl.DeviceIdType.MESH
N_DEV = 8
CHUNK = 64
HEAD_DIM = 128
EPS = 1e-6
CONV_K = 4
LEFT_CHUNKS = 8
REL_CLIP = 256
Q_TILE = LEFT_CHUNKS * CHUNK
ADAM_LR = 0.001
ADAM_B1 = 0.9
ADAM_B2 = 0.999
ADAM_EPS = 1e-08
ADAM_WD = 0.01
ADAM_STEP = 10
NEG_BIG = -1e30
VMEM_LIMIT_BYTES = 56 * 1024 * 1024
HIGHEST = lax.Precision.HIGHEST
ANY = pl.BlockSpec(memory_space=pl.ANY)


def _cparams(*sem):
    return pltpu.CompilerParams(dimension_semantics=tuple(sem), vmem_limit_bytes=VMEM_LIMIT_BYTES)


NN, NT, TN = (((1,), (0,)), ((), ())), (((1,), (1,)), ((), ())), (((0,), (0,)), ((), ()))
BNN, BNT, BTN = (((2,), (1,)), ((0,), (0,))), (((2,), (2,)), ((0,), (0,))), (((1,), (1,)), ((0,), (0,)))


def _dot(a, b, dims, precision=None):
    return lax.dot_general(a, b, dims, preferred_element_type=F32, precision=precision)


def _nn(a, b, precision=None):
    return _dot(a, b, NN, precision)


def _nt(a, b, precision=None):
    return _dot(a, b, NT, precision)


def _tn(a, b, precision=None):
    return _dot(a, b, TN, precision)


def _bf(x):
    return x.astype(BF16)


def _split(x, pieces=2):
    out = []
    for _ in range(pieces - 1):
        hi = x.astype(BF16)
        out.append(hi)
        x = x - hi.astype(F32)
    return out + [x.astype(BF16)]


def _dot3(a, b, dims):
    (ah, al), (bh, bl) = _split(a), _split(b)
    return _dot(ah, bh, dims) + (_dot(ah, bl, dims) + _dot(al, bh, dims))


def _dot_exact(a, b, dims, split_b):
    if split_b:
        a = a.astype(BF16)
        parts = [_dot(a, p, dims) for p in _split(b, 3)]
    else:
        b = b.astype(BF16)
        parts = [_dot(p, b, dims) for p in _split(a, 3)]
    return parts[0] + (parts[1] + parts[2])


def _sigmoid(x):
    return 0.5 * jnp.tanh(0.5 * x) + 0.5


def _silu(x):
    return x * _sigmoid(x)


def _dsilu(x):
    s = _sigmoid(x)
    return s * (1.0 + x * (1.0 - s))


def _my_pos():
    return lax.axis_index("x"), lax.axis_index("y"), lax.axis_index("c")


def _peers(x, y, c):
    def flip(v, f):
        return 1 - v if f else v

    return [(flip(x, kx), flip(y, ky), flip(c, kc)) for kx in (0, 1) for ky in (0, 1) for kc in (0, 1)][1:]


def _lin(p):
    return 4 * p[0] + 2 * p[1] + p[2]


class _Comm:
    def __init__(self, kind, arrays):
        self.kind, self.arrays, self.n = kind, list(arrays), len(arrays)

    def out_shape(self):
        lead = (N_DEV,) if self.kind == "gather" else ()
        return [jax.ShapeDtypeStruct(lead + a.shape, a.dtype) for a in self.arrays]

    def scratch(self):
        return [pltpu.SemaphoreType.DMA((7 * self.n,)), pltpu.SemaphoreType.DMA((7 * self.n,)),
                pltpu.SemaphoreType.DMA((self.n,))]

    def _copies(self, ins, outs, sems, arrivals):
        send_sems, recv_sems, local_sems = sems
        x, y, c = _my_pos()
        me = _lin((x, y, c))
        gather = self.kind == "gather"
        mine = [ins[t] if gather else ins[t].at[me] for t in range(self.n)]
        remote = []
        for k, peer in enumerate(_peers(x, y, c)):
            for t in range(self.n):
                if arrivals:
                    src, dst = mine[t], outs[t].at[_lin(peer)]
                else:
                    src, dst = (ins[t] if gather else ins[t].at[_lin(peer)]), outs[t].at[me]
                remote.append(pltpu.make_async_remote_copy(
                    src_ref=src, dst_ref=dst, send_sem=send_sems.at[k * self.n + t],
                    recv_sem=recv_sems.at[k * self.n + t], device_id=peer, device_id_type=MESH_IDS))
        if arrivals:
            return remote
        return [pltpu.make_async_copy(mine[t], outs[t].at[me], local_sems.at[t]) for t in range(self.n)], remote

    def start(self, ins, outs, sems):
        local, sends = self._copies(ins, outs, sems, False)
        for cp in local + sends:
            cp.start()

    def finish(self, ins, outs, sems):
        for cp in self._copies(ins, outs, sems, True):
            cp.wait_recv()
        local, sends = self._copies(ins, outs, sems, False)
        for cp in sends:
            cp.wait_send()
        for cp in local:
            cp.wait()


def _xor(a, b):
    return a + b - 2 * a * b


class _RoutedGather(_Comm):
    def __init__(self, arrays):
        super().__init__("gather", arrays)

    def _plan(self, outs, sems):
        send_sems, recv_sems, _ = sems
        x, y, c = _my_pos()
        sib, xn, yn, dg = (x, y, 1 - c), (1 - x, y, c), (x, 1 - y, c), (1 - x, 1 - y, c)
        via = (_xor(x, 1 - c), _xor(y, c), c)
        onto = (_xor(x, c), _xor(y, 1 - c), c)
        routes = [(None, sib, sib), (None, xn, xn), (None, yn, yn), (via, onto, dg),
                  (xn, sib, (1 - x, y, 1 - c)), (yn, sib, (x, 1 - y, 1 - c)), (dg, sib, (1 - x, 1 - y, 1 - c))]

        def copy(k, t, src, slot, target):
            return pltpu.make_async_remote_copy(
                src_ref=src, dst_ref=outs[t].at[slot], send_sem=send_sems.at[k * self.n + t],
                recv_sem=recv_sems.at[k * self.n + t], device_id=target, device_id_type=MESH_IDS)

        return (x, y, c), routes, copy

    def start(self, ins, outs, sems):
        me, routes, copy = self._plan(outs, sems)
        for t in range(self.n):
            pltpu.make_async_copy(ins[t], outs[t].at[_lin(me)], sems[2].at[t]).start()
            for k in range(3):
                copy(k, t, ins[t], _lin(me), routes[k][1]).start()

    def finish(self, ins, outs, sems):
        me, routes, copy = self._plan(outs, sems)

        def arrived(k):
            for t in range(self.n):
                copy(k, t, ins[t], _lin(routes[k][2]), me).wait_recv()

        def pass_on(k):
            for t in range(self.n):
                copy(k, t, outs[t].at[_lin(routes[k][0])], _lin(routes[k][0]), routes[k][1]).start()

        arrived(1)
        arrived(2)
        for k in (3, 4, 5):
            pass_on(k)
        arrived(3)
        pass_on(6)
        for k in (0, 4, 5, 6):
            arrived(k)
        for t in range(self.n):
            for k in range(7):
                src = ins[t] if k < 3 else outs[t].at[_lin(routes[k][0])]
                copy(k, t, src, _lin(me), routes[k][1]).wait_send()
            pltpu.make_async_copy(ins[t], outs[t].at[_lin(me)], sems[2].at[t]).wait()


def _comm_call(comm, name):
    n = comm.n

    def body(*refs):
        ins, outs, sems = refs[:n], refs[n:2 * n], refs[2 * n:]
        comm.start(ins, outs, sems)
        comm.finish(ins, outs, sems)

    return pl.pallas_call(
        body, name=name, out_shape=comm.out_shape(), in_specs=[ANY] * n, out_specs=[ANY] * n,
        scratch_shapes=comm.scratch(),
    )(*comm.arrays)


def _grid_call(body, *, name, grid, in_specs, out_specs, out_shape, args, scratch_shapes=(), semantics=None, comm=None):
    if comm is None:
        return pl.pallas_call(
            body, name=name, grid=grid, in_specs=in_specs, out_specs=out_specs, out_shape=out_shape,
            scratch_shapes=list(scratch_shapes), compiler_params=_cparams(*semantics),
        )(*args)
    n_in, n_out, n_sc, n = len(in_specs), len(out_specs), len(scratch_shapes), comm.n

    def full(*refs):
        ins, refs = refs[:n_in], refs[n_in:]
        cins, refs = refs[:n], refs[n:]
        outs, refs = refs[:n_out], refs[n_out:]
        couts, refs = refs[:n], refs[n:]
        scratch, sems = refs[:n_sc], refs[n_sc:]
        ids = [pl.program_id(a) for a in range(len(grid))]
        first = functools.reduce(jnp.logical_and, [i == 0 for i in ids])
        last = functools.reduce(jnp.logical_and, [i == g - 1 for i, g in zip(ids, grid)])

        @pl.when(first)
        def _():
            comm.start(cins, couts, sems)

        body(*ins, *outs, *scratch)

        @pl.when(last)
        def _():
            comm.finish(cins, couts, sems)

    return pl.pallas_call(
        full, name=name, grid=grid, in_specs=list(in_specs) + [ANY] * n, out_specs=list(out_specs) + [ANY] * n,
        out_shape=list(out_shape) + comm.out_shape(), scratch_shapes=list(scratch_shapes) + comm.scratch(),
        compiler_params=_cparams(*(["arbitrary"] * len(grid))),
    )(*(list(args) + comm.arrays))


def _mm(a, b, mode, m, n, k, *, out_dtype, name, tm=1024, tn=1024, tk=2048,
        a_m0=0, a_k0=0, b_n0=0, b_k0=0, res=None, comm=None, loss_target=None):
    tm, tn, tk = min(tm, m), min(tn, n), min(tk, k)
    nm, nn, nk = m // tm, n // tn, k // tk
    assert nm * tm == m and nn * tn == n and nk * tk == k
    am, ak, bn, bk = a_m0 // tm, a_k0 // tk, b_n0 // tn, b_k0 // tk
    assert am * tm == a_m0 and ak * tk == a_k0 and bn * tn == b_n0 and bk * tk == b_k0
    if mode == "tn":
        a_spec = pl.BlockSpec((tk, tm), lambda i, j, q: (q + ak, i + am))
        a_dims = (0,)
    else:
        a_spec = pl.BlockSpec((tm, tk), lambda i, j, q: (i + am, q + ak))
        a_dims = (1,)
    if mode == "nt":
        b_spec = pl.BlockSpec((tn, tk), lambda i, j, q: (j + bn, q + bk))
        b_dims = (1,)
    else:
        b_spec = pl.BlockSpec((tk, tn), lambda i, j, q: (q + bk, j + bn))
        b_dims = (0,)
    o_spec = pl.BlockSpec((tm, tn), lambda i, j, q: (i, j))
    has_res = res is not None
    has_loss = loss_target is not None
    n_in = 2 + has_res + has_loss
    n_out = 3 if has_loss else 1

    def body(*refs):
        a_ref, b_ref = refs[0], refs[1]
        res_ref = refs[2] if has_res else None
        o_ref = refs[n_in]
        p = _dot(a_ref[...], b_ref[...], ((a_dims, b_dims), ((), ())))

        def finish(total):
            if has_res:
                total = total + res_ref[...].astype(F32)
            if not has_loss:
                o_ref[...] = total.astype(out_dtype)
                return
            err = total - refs[n_in - 1][...]
            grad = err * (1.0 / n)
            o_ref[...] = grad
            refs[n_in + 1][...] = grad.astype(BF16)
            l_ref = refs[n_in + 2]
            part = jnp.zeros((1, LANES), F32) + 0.5 * jnp.sum(err * err) * (1.0 / n)
            first = (pl.program_id(0) == 0) & (pl.program_id(1) == 0)

            @pl.when(first)
            def _():
                l_ref[...] = part

            @pl.when(jnp.logical_not(first))
            def _():
                l_ref[...] += part

        if nk == 1:
            finish(p)
        else:
            acc_ref = refs[n_in + n_out]
            q = pl.program_id(2)

            @pl.when(q == 0)
            def _():
                acc_ref[...] = p

            @pl.when(q > 0)
            def _():
                acc_ref[...] += p

            @pl.when(q == nk - 1)
            def _():
                finish(acc_ref[...])

    extra_in = ([res] if has_res else []) + ([loss_target] if has_loss else [])
    if has_loss:
        return _grid_call(
            body, name=name, grid=(nm, nn, nk), in_specs=[a_spec, b_spec] + [o_spec] * len(extra_in),
            out_specs=[o_spec, o_spec, pl.BlockSpec((1, LANES), lambda i, j, q: (0, 0))],
            out_shape=[jax.ShapeDtypeStruct((m, n), F32), jax.ShapeDtypeStruct((m, n), BF16),
                       jax.ShapeDtypeStruct((1, LANES), F32)],
            scratch_shapes=[pltpu.VMEM((tm, tn), F32)] if nk > 1 else [],
            args=[a, b] + extra_in, semantics=("arbitrary", "arbitrary", "arbitrary"))
    out, *carried = _grid_call(
        body, name=name, grid=(nm, nn, nk),
        in_specs=[a_spec, b_spec] + [o_spec] * len(extra_in),
        out_specs=[o_spec], out_shape=[jax.ShapeDtypeStruct((m, n), out_dtype)],
        scratch_shapes=[pltpu.VMEM((tm, tn), F32)] if nk > 1 else [],
        args=[a, b] + extra_in, semantics=("parallel", "parallel", "arbitrary"), comm=comm)
    return out if comm is None else (out, carried)


def _rms_fwd(x, w, name, tr=512, comm=None):
    t, d = x.shape
    tr = min(tr, t)

    def body(x_ref, w_ref, o_ref):
        xv = x_ref[...]
        r = lax.rsqrt(jnp.mean(xv * xv, axis=-1, keepdims=True) + EPS)
        o_ref[...] = (xv * r * w_ref[...]).astype(BF16)

    out, *carried = _grid_call(
        body, name=name, grid=(t // tr,),
        in_specs=[pl.BlockSpec((tr, d), lambda i: (i, 0)), pl.BlockSpec((1, d), lambda i: (0, 0))],
        out_specs=[pl.BlockSpec((tr, d), lambda i: (i, 0))],
        out_shape=[jax.ShapeDtypeStruct((t, d), BF16)], args=(x, w), semantics=("parallel",), comm=comm)
    return out if comm is None else (out, carried)


def _rms_bwd(x, w, dy, dres, name, tr=256):
    t, d = x.shape
    tr = min(tr, t)

    def body(x_ref, w_ref, dy_ref, dres_ref, dx_ref, dxb_ref, dw_ref):
        xv = x_ref[...]
        dyv = dy_ref[...].astype(F32)
        r = lax.rsqrt(jnp.mean(xv * xv, axis=-1, keepdims=True) + EPS)
        gy = dyv * w_ref[...]
        proj = jnp.sum(gy * xv, axis=-1, keepdims=True) * (1.0 / d)
        dx = dres_ref[...] + r * gy - xv * (r * r * r) * proj
        dx_ref[...] = dx
        dxb_ref[...] = dx.astype(BF16)
        part = jnp.sum(dyv * xv * r, axis=0, keepdims=True)

        @pl.when(pl.program_id(0) == 0)
        def _():
            dw_ref[...] = part

        @pl.when(pl.program_id(0) > 0)
        def _():
            dw_ref[...] += part

    row = pl.BlockSpec((tr, d), lambda i: (i, 0))
    vec = pl.BlockSpec((1, d), lambda i: (0, 0))
    return pl.pallas_call(
        body, name=name, grid=(t // tr,),
        in_specs=[row, vec, row, row], out_specs=[row, row, vec],
        out_shape=[jax.ShapeDtypeStruct((t, d), F32), jax.ShapeDtypeStruct((t, d), BF16),
                   jax.ShapeDtypeStruct((1, d), F32)],
        compiler_params=_cparams("arbitrary"),
    )(x, w, dy, dres)


def _adamw(parts, w, m, v, name, tr=128, transposed=False):
    r, c = w.shape
    tr = tr if r % tr == 0 else r
    c1 = 1.0 - ADAM_B1 ** ADAM_STEP
    c2 = 1.0 - ADAM_B2 ** ADAM_STEP

    def body(p_ref, w_ref, m_ref, v_ref, g_ref, d_ref, nm_ref, nv_ref):
        g = p_ref[0].astype(F32)
        for s in range(1, N_DEV):
            g = g + p_ref[s].astype(F32)
        if transposed:
            i, j = lax.broadcasted_iota(jnp.int32, (tr, tr), 0), lax.broadcasted_iota(jnp.int32, (tr, tr), 1)
            g = _dot_exact(jnp.where(i == j, 1.0, 0.0), g, NT, True)
        nm = ADAM_B1 * m_ref[...] + (1.0 - ADAM_B1) * g
        nv = ADAM_B2 * v_ref[...] + (1.0 - ADAM_B2) * (g * g)
        m_hat = nm / c1
        v_hat = nv / c2
        g_ref[...] = g
        d_ref[...] = -ADAM_LR * (m_hat / (jnp.sqrt(v_hat) + ADAM_EPS) + ADAM_WD * w_ref[...])
        nm_ref[...] = nm
        nv_ref[...] = nv

    blk = pl.BlockSpec((tr, c), lambda i: (i, 0))
    p_spec = (pl.BlockSpec((N_DEV, c, tr), lambda i: (0, 0, i)) if transposed
              else pl.BlockSpec((N_DEV, tr, c), lambda i: (0, i, 0)))
    return pl.pallas_call(
        body, name=name, grid=(r // tr,),
        in_specs=[p_spec, blk, blk, blk],
        out_specs=[blk] * 4, out_shape=[jax.ShapeDtypeStruct((r, c), F32)] * 4,
        compiler_params=_cparams("parallel"),
    )(parts, w, m, v)


ROW_TILE, ROW_HEADS = 256, 16
CONV_ROWS, CONV_HEADS = 512, 8


def _window(dest, inner, n_in, out_index):
    if dest is None:
        return inner, 0, [], [], {}
    buf, col0, total = dest
    if buf is None:
        return total, col0, [], [], {}
    return total, col0, [buf], [ANY], {n_in: out_index}


def _skip_ref(body, at, count):
    return body if count == 0 else (lambda *refs: body(*refs[:at], *refs[at + count:]))


def _heads_of(x, nh):
    return [x[:, h * HEAD_DIM:(h + 1) * HEAD_DIM] for h in range(nh)]


def _headnorm_fwd(proj, w, col0, inner, name, tr=ROW_TILE, hb=ROW_HEADS):
    t = proj.shape[0]
    tr = min(tr, t)
    hb = min(hb, inner // HEAD_DIM)
    wc = hb * HEAD_DIM
    c0 = col0 // wc

    def body(x_ref, w_ref, o_ref):
        outs = []
        for xh in _heads_of(x_ref[...], hb):
            r = lax.rsqrt(jnp.mean(xh * xh, axis=-1, keepdims=True) + EPS)
            outs.append((xh * r * w_ref[...]).astype(BF16))
        o_ref[...] = jnp.concatenate(outs, axis=1)

    return pl.pallas_call(
        body, name=name, grid=(t // tr, inner // wc),
        in_specs=[pl.BlockSpec((tr, wc), lambda i, j: (i, j + c0)), pl.BlockSpec((1, HEAD_DIM), lambda i, j: (0, 0))],
        out_specs=pl.BlockSpec((tr, wc), lambda i, j: (i, j)),
        out_shape=jax.ShapeDtypeStruct((t, inner), BF16),
        compiler_params=_cparams("parallel", "parallel"),
    )(proj, w)


def _headnorm_bwd(dy, proj, w, col0, inner, name, tr=ROW_TILE, hb=ROW_HEADS, dest=None):
    t = proj.shape[0]
    tr = min(tr, t)
    hb = min(hb, inner // HEAD_DIM)
    wc = hb * HEAD_DIM
    c0 = col0 // wc
    width, out0, more, more_specs, aliases = _window(dest, inner, 3, 0)

    def body(dy_ref, x_ref, w_ref, dx_ref, dw_ref):
        outs = []
        part = jnp.zeros((1, HEAD_DIM), F32)
        for dyh, xh in zip(_heads_of(dy_ref[...], hb), _heads_of(x_ref[...], hb)):
            r = lax.rsqrt(jnp.mean(xh * xh, axis=-1, keepdims=True) + EPS)
            gy = dyh * w_ref[...]
            pr = jnp.sum(gy * xh, axis=-1, keepdims=True) * (1.0 / HEAD_DIM)
            outs.append((r * gy - xh * (r * r * r) * pr).astype(BF16))
            part = part + jnp.sum(dyh * xh * r, axis=0, keepdims=True)
        dx_ref[...] = jnp.concatenate(outs, axis=1)
        first = (pl.program_id(0) == 0) & (pl.program_id(1) == 0)

        @pl.when(first)
        def _():
            dw_ref[...] = part

        @pl.when(jnp.logical_not(first))
        def _():
            dw_ref[...] += part

    blk = pl.BlockSpec((tr, wc), lambda i, j: (i, j))
    return pl.pallas_call(
        _skip_ref(body, 3, len(more)), name=name, grid=(t // tr, inner // wc),
        in_specs=[blk, pl.BlockSpec((tr, wc), lambda i, j: (i, j + c0)),
                  pl.BlockSpec((1, HEAD_DIM), lambda i, j: (0, 0))] + more_specs,
        out_specs=[pl.BlockSpec((tr, wc), lambda i, j: (i, j + out0 // wc)),
                   pl.BlockSpec((1, HEAD_DIM), lambda i, j: (0, 0))],
        out_shape=[jax.ShapeDtypeStruct((t, width), BF16), jax.ShapeDtypeStruct((1, HEAD_DIM), F32)],
        input_output_aliases=aliases, compiler_params=_cparams("arbitrary", "arbitrary"),
    )(dy, proj, w, *more)


def _gate_fwd(o, proj, zcol0, inner, name, norm_w=None, tr=ROW_TILE, hb=ROW_HEADS):
    t = o.shape[0]
    tr = min(tr, t)
    hb = min(hb, inner // HEAD_DIM)
    wc = hb * HEAD_DIM
    c0 = zcol0 // wc
    has_w = norm_w is not None

    def body(*refs):
        o_ref, z_ref = refs[0], refs[1]
        out_ref = refs[2 + has_w]
        outs = []
        for oh, zh in zip(_heads_of(o_ref[...], hb), _heads_of(z_ref[...], hb)):
            if has_w:
                r = lax.rsqrt(jnp.mean(oh * oh, axis=-1, keepdims=True) + EPS)
                oh = oh * r * refs[2][...]
            outs.append((oh * _silu(zh)).astype(BF16))
        out_ref[...] = jnp.concatenate(outs, axis=1)

    blk = pl.BlockSpec((tr, wc), lambda i, j: (i, j))
    vec = pl.BlockSpec((1, HEAD_DIM), lambda i, j: (0, 0))
    return pl.pallas_call(
        body, name=name, grid=(t // tr, inner // wc),
        in_specs=[blk, pl.BlockSpec((tr, wc), lambda i, j: (i, j + c0))] + ([vec] if has_w else []),
        out_specs=blk, out_shape=jax.ShapeDtypeStruct((t, inner), BF16),
        compiler_params=_cparams("parallel", "parallel"),
    )(*([o, proj] + ([norm_w] if has_w else [])))


def _gate_bwd(dg, o, proj, zcol0, inner, name, do_dtype, norm_w=None, tr=ROW_TILE, hb=ROW_HEADS, dest=None):
    t = o.shape[0]
    tr = min(tr, t)
    hb = min(hb, inner // HEAD_DIM)
    wc = hb * HEAD_DIM
    c0 = zcol0 // wc
    has_w = norm_w is not None
    width, out0, more, more_specs, aliases = _window(dest, inner, 3 + has_w, 1)

    def body(*refs):
        dg_ref, o_ref, z_ref = refs[0], refs[1], refs[2]
        do_ref, dz_ref = refs[3 + has_w], refs[4 + has_w]
        dos, dzs = [], []
        part = jnp.zeros((1, HEAD_DIM), F32)
        for dgh, oh, zh in zip(_heads_of(dg_ref[...], hb), _heads_of(o_ref[...], hb), _heads_of(z_ref[...], hb)):
            dy = dgh * _silu(zh)
            if has_w:
                w = refs[3][...]
                r = lax.rsqrt(jnp.mean(oh * oh, axis=-1, keepdims=True) + EPS)
                on = oh * r
                dzs.append((dgh * on * w * _dsilu(zh)).astype(BF16))
                gy = dy * w
                pr = jnp.sum(gy * oh, axis=-1, keepdims=True) * (1.0 / HEAD_DIM)
                dos.append((r * gy - oh * (r * r * r) * pr).astype(do_dtype))
                part = part + jnp.sum(dy * on, axis=0, keepdims=True)
            else:
                dzs.append((dgh * oh * _dsilu(zh)).astype(BF16))
                dos.append(dy.astype(do_dtype))
        do_ref[...] = jnp.concatenate(dos, axis=1)
        dz_ref[...] = jnp.concatenate(dzs, axis=1)
        if has_w:
            dw_ref = refs[6]
            first = (pl.program_id(0) == 0) & (pl.program_id(1) == 0)

            @pl.when(first)
            def _():
                dw_ref[...] = part

            @pl.when(jnp.logical_not(first))
            def _():
                dw_ref[...] += part

    blk = pl.BlockSpec((tr, wc), lambda i, j: (i, j))
    vec = pl.BlockSpec((1, HEAD_DIM), lambda i, j: (0, 0))
    return pl.pallas_call(
        _skip_ref(body, 3 + has_w, len(more)), name=name, grid=(t // tr, inner // wc),
        in_specs=[blk, blk, pl.BlockSpec((tr, wc), lambda i, j: (i, j + c0))] + ([vec] if has_w else []) + more_specs,
        out_specs=[blk, pl.BlockSpec((tr, wc), lambda i, j: (i, j + out0 // wc))] + ([vec] if has_w else []),
        out_shape=[jax.ShapeDtypeStruct((t, inner), do_dtype), jax.ShapeDtypeStruct((t, width), BF16)]
        + ([jax.ShapeDtypeStruct((1, HEAD_DIM), F32)] if has_w else []),
        input_output_aliases=aliases, compiler_params=_cparams("arbitrary", "arbitrary"),
    )(*([dg, o, proj] + ([norm_w] if has_w else []) + more))


N_REL = 2 * REL_CLIP + 1
REL_PAD = 640
WIN = 2 * Q_TILE


def _diag_onehot():
    i = lax.broadcasted_iota(jnp.int32, (REL_PAD, WIN), 0)
    j = lax.broadcasted_iota(jnp.int32, (REL_PAD, WIN), 1)
    rel = jnp.where(j < Q_TILE + CHUNK, Q_TILE - j, Q_TILE + WIN - j)
    used = (j < Q_TILE + CHUNK) | (j > WIN - CHUNK)
    idx = jnp.clip(rel, -REL_CLIP, REL_CLIP) + REL_CLIP
    return jnp.where(used & (i == idx), 1.0, 0.0).astype(F32)


def _band_mask():
    r = lax.broadcasted_iota(jnp.int32, (Q_TILE, WIN), 0) // CHUNK
    kc = lax.broadcasted_iota(jnp.int32, (Q_TILE, WIN), 1) // CHUNK - LEFT_CHUNKS
    return (kc <= r) & (kc >= r - LEFT_CHUNKS)


def _bias_tiles(rel_bias_pad, name):
    nh = rel_bias_pad.shape[0]

    def body(rb_ref, o_ref):
        dvec = _nn(rb_ref[...], _diag_onehot(), HIGHEST)[0:1, :]
        tile = pltpu.roll(jnp.broadcast_to(dvec, (Q_TILE, WIN)), 0, 1, stride=1, stride_axis=0)
        o_ref[...] = jnp.where(_band_mask(), tile, NEG_BIG)

    return pl.pallas_call(
        body, name=name, grid=(nh,),
        in_specs=[pl.BlockSpec((None, 8, REL_PAD), lambda h: (h, 0, 0))],
        out_specs=pl.BlockSpec((None, Q_TILE, WIN), lambda h: (h, 0, 0)),
        out_shape=jax.ShapeDtypeStruct((nh, Q_TILE, WIN), F32),
        compiler_params=_cparams("parallel"),
    )(rel_bias_pad)


def _bias_grad(dtile, name):
    nh = dtile.shape[0]

    def body(d_ref, o_ref):
        ri = lax.broadcasted_iota(jnp.int32, (Q_TILE, Q_TILE), 0)
        ci = lax.broadcasted_iota(jnp.int32, (Q_TILE, Q_TILE), 1)
        flip = jnp.where(ri + ci == Q_TILE - 1, 1.0, 0.0).astype(F32)
        rev = _dot_exact(flip, d_ref[...], NN, True)
        rolled = pltpu.roll(rev, WIN - (Q_TILE - 1), 1, stride=1, stride_axis=0)
        diag = jnp.broadcast_to(jnp.sum(rolled, axis=0, keepdims=True), (8, WIN))
        o_ref[...] = _nt(diag, _diag_onehot(), HIGHEST)

    return pl.pallas_call(
        body, name=name, grid=(nh,),
        in_specs=[pl.BlockSpec((None, Q_TILE, WIN), lambda h: (h, 0, 0))],
        out_specs=pl.BlockSpec((None, 8, REL_PAD), lambda h: (h, 0, 0)),
        out_shape=jax.ShapeDtypeStruct((nh, 8, REL_PAD), F32),
        compiler_params=_cparams("parallel"),
    )(dtile)


GROUP = 2 * CHUNK
BAND = Q_TILE + GROUP


N_GROUPS = Q_TILE // GROUP
ATTN_HB = 2


def _head_cols(j):
    return slice(j * HEAD_DIM, (j + 1) * HEAD_DIM)


def _groups(ref, units):
    return jnp.stack([ref[GROUP * g:GROUP * (g + 1), _head_cols(j)] for j, g in units])


def _bands(r0_ref, r1_ref, units):
    return jnp.stack([jnp.concatenate([r0_ref[GROUP * g:, _head_cols(j)], r1_ref[:GROUP * (g + 1), _head_cols(j)]],
                                      axis=0) for j, g in units])


def _group_probs(q, kw, b_ref, units, first_tile):
    bias = jnp.stack([b_ref[j, GROUP * g:GROUP * (g + 1), GROUP * g:GROUP * g + BAND] for j, g in units])
    s = _dot(q, kw, BNT) * (HEAD_DIM ** -0.5) + bias
    col = jnp.stack([lax.broadcasted_iota(jnp.int32, (GROUP, BAND), 1) + GROUP * g for _, g in units])
    s = jnp.where(first_tile & (col < Q_TILE), NEG_BIG, s)
    p = jnp.exp(s - jnp.max(s, axis=-1, keepdims=True))
    return p * (1.0 / jnp.sum(p, axis=-1, keepdims=True))


def _attn_fwd(q, k, v, v_col0, bias, name):
    t, inner = q.shape
    nh, nt = inner // HEAD_DIM, t // Q_TILE
    hb = min(ATTN_HB, nh)
    wc = hb * HEAD_DIM
    vh = v_col0 // wc
    units = [(j, g) for j in range(hb) for g in range(N_GROUPS)]

    def body(q_ref, k0_ref, k1_ref, v0_ref, v1_ref, b_ref, o_ref):
        p = _group_probs(_groups(q_ref, units), _bands(k0_ref, k1_ref, units), b_ref, units, pl.program_id(1) == 0)
        o = _dot(_bf(p), _bf(_bands(v0_ref, v1_ref, units)), BNN)
        for n, (j, g) in enumerate(units):
            o_ref[GROUP * g:GROUP * (g + 1), _head_cols(j)] = o[n]

    cur = pl.BlockSpec((Q_TILE, wc), lambda h, i: (i, h))
    prev = pl.BlockSpec((Q_TILE, wc), lambda h, i: (jnp.maximum(i - 1, 0), h))
    v_cur = pl.BlockSpec((Q_TILE, wc), lambda h, i: (i, h + vh))
    v_prev = pl.BlockSpec((Q_TILE, wc), lambda h, i: (jnp.maximum(i - 1, 0), h + vh))
    return pl.pallas_call(
        body, name=name, grid=(nh // hb, nt),
        in_specs=[cur, prev, cur, v_prev, v_cur, pl.BlockSpec((hb, Q_TILE, WIN), lambda h, i: (h, 0, 0))],
        out_specs=cur, out_shape=jax.ShapeDtypeStruct((t, inner), F32),
        compiler_params=_cparams("parallel", "parallel"),
    )(q, k, k, v, v, bias)


def _attn_bwd(q, k, v, v_col0, do, bias, name, dest=None):
    t, inner = q.shape
    nh, nt = inner // HEAD_DIM, t // Q_TILE
    scale = HEAD_DIM ** -0.5
    hb = min(ATTN_HB, nh)
    wc = hb * HEAD_DIM
    units = [(j, g) for j in range(hb) for g in range(N_GROUPS)]

    def body(q_ref, k0_ref, k1_ref, v0_ref, v1_ref, do_ref, b_ref, dq_ref, dk_ref, dv_ref, db_ref,
             ck_ref, cv_ref, wk_ref, wv_ref):
        i = pl.program_id(1)

        @pl.when(i == 0)
        def _():
            ck_ref[...] = jnp.zeros(blk, F32)
            cv_ref[...] = jnp.zeros(blk, F32)
            db_ref[...] = jnp.zeros((hb, Q_TILE, WIN), F32)

        @pl.when(i < nt)
        def _():
            wk_ref[...] = jnp.zeros((WIN, wc), F32)
            wv_ref[...] = jnp.zeros((WIN, wc), F32)
            qv, dov = _groups(q_ref, units), _groups(do_ref, units)
            kw, vw = _bands(k0_ref, k1_ref, units), _bf(_bands(v0_ref, v1_ref, units))
            p = _group_probs(qv, kw, b_ref, units, i == 0)
            dp = _dot(dov, vw, BNT)
            ds = p * (dp - jnp.sum(p * dp, axis=-1, keepdims=True))
            pb, dsb = _bf(p), _bf(ds)
            dq = _dot(dsb, kw, BNN) * scale
            dkw = _dot(dsb, qv, BTN) * scale
            dvw = _dot(pb, dov, BTN)
            for n, (j, g) in enumerate(units):
                rows, cols = slice(GROUP * g, GROUP * (g + 1)), slice(GROUP * g, GROUP * g + BAND)
                db_ref[j, rows, cols] += ds[n]
                dq_ref[rows, _head_cols(j)] = dq[n]
                wk_ref[cols, _head_cols(j)] += dkw[n]
                wv_ref[cols, _head_cols(j)] += dvw[n]
            dk_ref[...] = ck_ref[...] + wk_ref[:Q_TILE, :]
            dv_ref[...] = (cv_ref[...] + wv_ref[:Q_TILE, :]).astype(BF16)
            ck_ref[...] = wk_ref[Q_TILE:, :]
            cv_ref[...] = wv_ref[Q_TILE:, :]

        @pl.when(i == nt)
        def _():
            dk_ref[...] = ck_ref[...]
            dv_ref[...] = cv_ref[...].astype(BF16)

    blk = (Q_TILE, wc)
    cur = pl.BlockSpec(blk, lambda h, i: (jnp.minimum(i, nt - 1), h))
    prev = pl.BlockSpec(blk, lambda h, i: (jnp.clip(i - 1, 0, nt - 1), h))
    lag = pl.BlockSpec(blk, lambda h, i: (jnp.maximum(i - 1, 0), h))
    vh = v_col0 // wc
    v_cur = pl.BlockSpec(blk, lambda h, i: (jnp.minimum(i, nt - 1), h + vh))
    v_prev = pl.BlockSpec(blk, lambda h, i: (jnp.clip(i - 1, 0, nt - 1), h + vh))
    tile = pl.BlockSpec((hb, Q_TILE, WIN), lambda h, i: (h, 0, 0))
    width, out0, more, more_specs, aliases = _window(dest, inner, 7, 2)
    return pl.pallas_call(
        _skip_ref(body, 7, len(more)), name=name, grid=(nh // hb, nt + 1),
        in_specs=[cur, prev, cur, v_prev, v_cur, cur, tile] + more_specs,
        out_specs=[cur, lag, pl.BlockSpec(blk, lambda h, i: (jnp.maximum(i - 1, 0), h + out0 // wc)), tile],
        out_shape=[jax.ShapeDtypeStruct((t, inner), F32)] * 2 + [jax.ShapeDtypeStruct((t, width), BF16),
                                                                 jax.ShapeDtypeStruct((nh, Q_TILE, WIN), F32)],
        scratch_shapes=[pltpu.VMEM(blk, F32), pltpu.VMEM(blk, F32),
                        pltpu.VMEM((WIN, wc), F32), pltpu.VMEM((WIN, wc), F32)],
        input_output_aliases=aliases, compiler_params=_cparams("arbitrary", "arbitrary"),
    )(q, k, k, v, v, do, bias, *more)


def _pad_rel_bias(rel_bias):
    nh = rel_bias.shape[0]
    return jnp.broadcast_to(jnp.pad(rel_bias, ((0, 0), (0, REL_PAD - N_REL)))[:, None, :], (nh, 8, REL_PAD))


def _layer_b_fwd(h1, nw, w_in_t, qw, kw, bias, w_out, target):
    t, d = h1.shape
    inner = w_out.shape[0]
    hn = _rms_fwd(h1, nw, "b_rms")
    proj = _mm(hn, w_in_t, "nt", t, 4 * inner, d, out_dtype=F32, name="b_proj")
    qn = _headnorm_fwd(proj, qw, 0, inner, "b_qnorm")
    kn = _headnorm_fwd(proj, kw, inner, inner, "b_knorm")
    o = _attn_fwd(qn, kn, proj, 2 * inner, bias, "b_attn")
    g = _gate_fwd(o, proj, 3 * inner, inner, "b_gate")
    loss_parts = _mm(g, w_out, "nn", t, d, inner, out_dtype=F32, name="b_out", res=h1, loss_target=target)
    return loss_parts, (hn, proj, qn, kn, o, g)


def _layer_b_bwd(dh2, dh2b, h1, nw, w_in_t, qw, kw, bias, w_out, saved):
    hn, proj, qn, kn, o, g = saved
    t, d = h1.shape
    inner = w_out.shape[0]
    dg = _mm(dh2b, w_out, "nt", t, inner, d, out_dtype=F32, name="b_dgate")
    dw_out = _mm(g, dh2b, "tn", inner, d, t, out_dtype=BF16, name="b_dwout")
    do, dproj = _gate_bwd(dg, o, proj, 3 * inner, inner, "b_gate_bwd", BF16, dest=(None, 3 * inner, 4 * inner))
    dq, dk, dproj, dtile = _attn_bwd(qn, kn, proj, 2 * inner, do, bias, "b_attn_bwd",
                                     dest=(dproj, 2 * inner, 4 * inner))
    dproj, dqw = _headnorm_bwd(dq, proj, qw, 0, inner, "b_qnorm_bwd", dest=(dproj, 0, 4 * inner))
    dproj, dkw = _headnorm_bwd(dk, proj, kw, inner, inner, "b_knorm_bwd", dest=(dproj, inner, 4 * inner))
    dhn = _mm(dproj, w_in_t, "nn", t, d, 4 * inner, out_dtype=F32, name="b_dhn")
    dw_in_t = _mm(dproj, hn, "tn", 4 * inner, d, t, out_dtype=BF16, name="b_dwin")
    dh1, dh1b, dnw = _rms_bwd(h1, nw, dhn, dh2, "b_rms_bwd")
    drb = _bias_grad(dtile, "b_bias_grad")[:, 0, :N_REL]
    return dh1, dh1b, dnw, dw_in_t, dqw, dkw, drb, dw_out


LANES = 128


def _softplus(x):
    return jnp.maximum(x, 0.0) + jnp.log1p(jnp.exp(-jnp.abs(x)))


def _gates_fwd(ab, alog_row, dt_row, nh, name, tr=1024):
    t = ab.shape[0]
    tr = min(tr, t)

    def body(x_ref, al_ref, dt_ref, o_ref):
        x = x_ref[...]
        lane = lax.broadcasted_iota(jnp.int32, x.shape, 1)
        g = -jnp.exp(al_ref[...]) * _softplus(x + dt_ref[...])
        o_ref[...] = jnp.where(lane < nh, g, jnp.where(lane < 2 * nh, _sigmoid(x), 0.0))

    row = pl.BlockSpec((tr, LANES), lambda i: (i, 0))
    vec = pl.BlockSpec((1, LANES), lambda i: (0, 0))
    return pl.pallas_call(
        body, name=name, grid=(t // tr,), in_specs=[row, vec, vec], out_specs=row,
        out_shape=jax.ShapeDtypeStruct((t, LANES), F32), compiler_params=_cparams("parallel"),
    )(ab, alog_row, dt_row)


def _gates_bwd(ab, alog_row, dt_row, dgates, nh, name, tr=1024):
    t = ab.shape[0]
    tr = min(tr, t)
    npart = dgates.shape[0]

    def body(x_ref, al_ref, dt_ref, dg_ref, dx_ref, s_ref):
        x = x_ref[...]
        lane = lax.broadcasted_iota(jnp.int32, x.shape, 1)
        dgt = dg_ref[0]
        for p in range(1, npart):
            dgt = dgt + dg_ref[p]
        ea = jnp.exp(al_ref[...])
        xa = x + dt_ref[...]
        da = jnp.where(lane < nh, dgt * (-ea) * _sigmoid(xa), 0.0)
        beta = _sigmoid(x)
        db = jnp.where((lane >= nh) & (lane < 2 * nh), dgt * beta * (1.0 - beta), 0.0)
        dx_ref[...] = (da + db).astype(BF16)
        dal = jnp.sum(jnp.where(lane < nh, dgt * (-ea) * _softplus(xa), 0.0), axis=0, keepdims=True)
        ddt = jnp.sum(da, axis=0, keepdims=True)
        r8 = lax.broadcasted_iota(jnp.int32, (8, LANES), 0)
        part = jnp.where(r8 == 0, dal, jnp.where(r8 == 1, ddt, 0.0))

        @pl.when(pl.program_id(0) == 0)
        def _():
            s_ref[...] = part

        @pl.when(pl.program_id(0) > 0)
        def _():
            s_ref[...] += part

    row = pl.BlockSpec((tr, LANES), lambda i: (i, 0))
    vec = pl.BlockSpec((1, LANES), lambda i: (0, 0))
    return pl.pallas_call(
        body, name=name, grid=(t // tr,),
        in_specs=[row, vec, vec, pl.BlockSpec((npart, tr, LANES), lambda i: (0, i, 0))],
        out_specs=[row, pl.BlockSpec((8, LANES), lambda i: (0, 0))],
        out_shape=[jax.ShapeDtypeStruct((t, LANES), BF16), jax.ShapeDtypeStruct((8, LANES), F32)],
        compiler_params=_cparams("arbitrary"),
    )(ab, alog_row, dt_row, dgates)


HALO = 8


def _conv_taps(ext, w, rows):
    acc = ext[HALO:HALO + rows] * w[CONV_K - 1:CONV_K]
    for s in range(1, CONV_K):
        acc = acc + pltpu.roll(ext, s, 0)[HALO:HALO + rows] * w[CONV_K - 1 - s:CONV_K - s]
    return acc


def _conv_fwd(proj, conv_w, col0, inner, mode, name, tt=CONV_ROWS, hb=CONV_HEADS):
    t = proj.shape[0]
    tt = min(tt, t)
    hb = min(hb, inner // HEAD_DIM)
    wc = hb * HEAD_DIM
    c0 = col0 // wc
    hpb = tt // HALO

    def body(x_ref, halo_ref, w_ref, o_ref):
        halo = jnp.where(pl.program_id(1) == 0, 0.0, halo_ref[...])
        s = _silu(_conv_taps(jnp.concatenate([halo, x_ref[...]], axis=0), w_ref[...], tt))
        if mode == "v":
            o_ref[...] = s
        else:
            mul = HEAD_DIM ** -0.5 if mode == "q" else 1.0
            o_ref[...] = jnp.concatenate(
                [sh * (lax.rsqrt(jnp.sum(sh * sh, axis=-1, keepdims=True) + EPS) * mul) for sh in _heads_of(s, hb)], axis=1)

    return pl.pallas_call(
        body, name=name, grid=(inner // wc, t // tt),
        in_specs=[pl.BlockSpec((tt, wc), lambda j, i: (i, j + c0)),
                  pl.BlockSpec((HALO, wc), lambda j, i: (jnp.maximum(i * hpb - 1, 0), j + c0)),
                  pl.BlockSpec((CONV_K, wc), lambda j, i: (0, j + c0))],
        out_specs=pl.BlockSpec((tt, wc), lambda j, i: (i, j)),
        out_shape=jax.ShapeDtypeStruct((t, inner), F32),
        compiler_params=_cparams("parallel", "parallel"),
    )(proj, proj, conv_w)


def _conv_bwd(dy, proj, conv_w, col0, inner, mode, name, tt=CONV_ROWS, hb=CONV_HEADS, dest=None):
    t = proj.shape[0]
    tt = min(tt, t)
    nt = t // tt
    hb = min(hb, inner // HEAD_DIM)
    wc = hb * HEAD_DIM
    c0 = col0 // wc
    hpb = tt // HALO
    rows = tt + HALO

    def body(dy_ref, dyn_ref, x_ref, xp_ref, xn_ref, w_ref, dx_ref, dw_ref):
        i = pl.program_id(1)
        w = w_ref[...]
        xprev = jnp.where(i == 0, 0.0, xp_ref[...])
        ext = jnp.concatenate([xprev, x_ref[...], xn_ref[...]], axis=0)
        c = _conv_taps(ext, w, rows)
        dyv = jnp.concatenate([dy_ref[...], jnp.where(i == nt - 1, 0.0, dyn_ref[...])], axis=0)
        sg = _sigmoid(c)
        s = c * sg
        if mode == "v":
            ds = dyv
        else:
            mul = HEAD_DIM ** -0.5 if mode == "q" else 1.0
            parts = []
            for dyh, sh in zip(_heads_of(dyv, hb), _heads_of(s, hb)):
                r = lax.rsqrt(jnp.sum(sh * sh, axis=-1, keepdims=True) + EPS)
                parts.append(mul * (r * dyh - sh * (r * r * r) * jnp.sum(dyh * sh, axis=-1, keepdims=True)))
            ds = jnp.concatenate(parts, axis=1)
        dc = ds * (sg * (1.0 + c * (1.0 - sg)))
        dx = dc[:tt] * w[CONV_K - 1:CONV_K]
        for sft in range(1, CONV_K):
            dx = dx + pltpu.roll(dc, rows - sft, 0)[:tt] * w[CONV_K - 1 - sft:CONV_K - sft]
        dx_ref[...] = dx.astype(BF16)
        r8 = lax.broadcasted_iota(jnp.int32, (8, wc), 0)
        part = jnp.zeros((8, wc), F32)
        for sft in range(CONV_K):
            xs = ext[HALO:HALO + tt] if sft == 0 else pltpu.roll(ext, sft, 0)[HALO:HALO + tt]
            part = part + jnp.where(r8 == CONV_K - 1 - sft, jnp.sum(dc[:tt] * xs, axis=0, keepdims=True), 0.0)

        @pl.when(i == 0)
        def _():
            dw_ref[...] = part

        @pl.when(i > 0)
        def _():
            dw_ref[...] += part

    cur = lambda off: pl.BlockSpec((tt, wc), lambda j, i: (i, j + off))
    nxt = lambda off: pl.BlockSpec((HALO, wc), lambda j, i: (jnp.minimum((i + 1) * hpb, t // HALO - 1), j + off))
    width, out0, more, more_specs, aliases = _window(dest, inner, 6, 0)
    return pl.pallas_call(
        _skip_ref(body, 6, len(more)), name=name, grid=(inner // wc, nt),
        in_specs=[cur(0), nxt(0), cur(c0),
                  pl.BlockSpec((HALO, wc), lambda j, i: (jnp.maximum(i * hpb - 1, 0), j + c0)), nxt(c0),
                  pl.BlockSpec((CONV_K, wc), lambda j, i: (0, j + c0))] + more_specs,
        out_specs=[pl.BlockSpec((tt, wc), lambda j, i: (i, j + out0 // wc)),
                   pl.BlockSpec((8, wc), lambda j, i: (0, j))],
        out_shape=[jax.ShapeDtypeStruct((t, width), BF16), jax.ShapeDtypeStruct((8, inner), F32)],
        input_output_aliases=aliases, compiler_params=_cparams("parallel", "arbitrary"),
    )(dy, dy, proj, proj, proj, conv_w, *more)


GDN_HB = 4
GDN_NB = 8
SCAN_HB = 16
SCAN_NB = 4


def _iota2(n, m):
    return lax.broadcasted_iota(jnp.int32, (n, m), 0), lax.broadcasted_iota(jnp.int32, (n, m), 1)


def _head_select(first_head, hb, lane0):
    r, lane = _iota2(8, LANES)
    return jnp.where((r < hb) & (lane == lane0 + first_head + r), 1.0, 0.0).astype(F32)


def _chunk_gates(gt, selg, selb):
    i, j = _iota2(CHUNK, CHUNK)
    gc_all = _dot_exact(jnp.where(j <= i, 1.0, 0.0), gt, NN, True)
    return (_dot_exact(gc_all, selg, NT, False), _dot_exact(selg, gc_all, NT, True),
            _dot_exact(gt, selb, NT, False))


def _decay_terms(gcol, grow):
    i, j = _iota2(CHUNK, CHUNK)
    glast = gcol[:, CHUNK - 1:CHUNK, :]
    decay = jnp.exp(jnp.where(j <= i, gcol - grow, NEG_BIG))
    return jnp.exp(gcol), jnp.exp(glast - gcol), jnp.exp(glast), decay


def _unit_lower_inverse(a):
    i, j = _iota2(CHUNK, CHUNK)
    same16 = (i // 16) == (j // 16)
    same32 = (i // 32) == (j // 32)
    m = jnp.where(same16, -a, 0.0)
    x = jnp.where(i == j, 1.0, 0.0) + m
    for _ in range(3):
        m = _dot3(m, m, BNN)
        x = x + _dot3(x, m, BNN)
    for off in (jnp.where(same32 & jnp.logical_not(same16), a, 0.0), jnp.where(same32, 0.0, a)):
        x = x - _dot3(_dot3(x, off, BNN), x, BNN)
    return x


def _unit_inputs(refs, g_ref, selg, selb, hb, nb):
    units = [(c, h) for c in range(nb) for h in range(hb)]
    rs = lambda c: slice(c * CHUNK, (c + 1) * CHUNK)
    cs = lambda h: slice(h * HEAD_DIM, (h + 1) * HEAD_DIM)
    gates = [_chunk_gates(g_ref[rs(c), :], selg, selb) for c in range(nb)]
    stacked = [jnp.stack([r[rs(c), cs(h)] for c, h in units]) for r in refs]
    gcol = jnp.stack([gates[c][0][:, h:h + 1] for c, h in units])
    grow = jnp.stack([gates[c][1][h:h + 1, :] for c, h in units])
    bcol = jnp.stack([gates[c][2][:, h:h + 1] for c, h in units])
    return units, rs, cs, stacked, gcol, grow, bcol


def _gdn_specs(nh, inner, t, heads=GDN_HB, chunks=GDN_NB):
    hb, nb = min(heads, nh), chunks
    rows = nb * CHUNK
    wide = pl.BlockSpec((rows, hb * HEAD_DIM), lambda g, n: (n, g))
    sq = pl.BlockSpec((hb, rows, CHUNK), lambda g, n: (g, n, 0))
    gts = pl.BlockSpec((rows, LANES), lambda g, n: (n, 0))
    glb = pl.BlockSpec((nb * 8, hb * HEAD_DIM), lambda g, n: (n, g))
    return hb, nb, rows, wide, sq, gts, glb


def _gdn_intra_fwd(q, k, v, gates, nh, name, comm=None):
    t, inner = q.shape
    hb, nb, rows, wide, sq, gts, glb = _gdn_specs(nh, inner, t)

    def body(q_ref, k_ref, v_ref, g_ref, qe_ref, kel_ref, wb_ref, w_ref, u_ref, qk_ref, tm_ref, gl_ref):
        first = pl.program_id(0) * hb
        selg, selb = _head_select(first, hb, 0), _head_select(first, hb, nh)
        i, j = _iota2(CHUNK, CHUNK)
        units, rs, cs, (qv, kv, vv), gcol, grow, bcol = _unit_inputs(
            (q_ref, k_ref, v_ref), g_ref, selg, selb, hb, nb)
        e, el, gl, decay = _decay_terms(gcol, grow)
        kb = kv * bcol
        qbf, kbf = _bf(qv), _bf(kv)
        a = jnp.where(j < i, _dot(_bf(kb), kbf, BNT) * decay, 0.0)
        tm = _unit_lower_inverse(a)
        uw = _dot3(tm, jnp.concatenate([vv * bcol, kb * e], axis=2), BNN)
        qk = _bf(_dot(qbf, kbf, BNT) * decay)
        qe, kel = _bf(qv * e), _bf(kv * el)
        for n, (c, h) in enumerate(units):
            w = uw[n, :, HEAD_DIM:]
            qe_ref[rs(c), cs(h)] = qe[n]
            kel_ref[rs(c), cs(h)] = kel[n]
            wb_ref[rs(c), cs(h)] = _bf(w)
            w_ref[rs(c), cs(h)] = w
            u_ref[rs(c), cs(h)] = uw[n, :, :HEAD_DIM]
            qk_ref[h, rs(c), :] = qk[n]
            tm_ref[h, rs(c), :] = tm[n]
            gl_ref[c * 8:(c + 1) * 8, cs(h)] = jnp.broadcast_to(gl[n], (8, HEAD_DIM))

    big = lambda dt: jax.ShapeDtypeStruct((t, inner), dt)
    return _grid_call(
        body, name=name, grid=(nh // hb, t // rows),
        in_specs=[wide, wide, wide, gts],
        out_specs=[wide] * 5 + [sq, sq, glb],
        out_shape=[big(BF16), big(BF16), big(BF16), big(F32), big(F32),
                   jax.ShapeDtypeStruct((nh, t, CHUNK), BF16), jax.ShapeDtypeStruct((nh, t, CHUNK), F32),
                   jax.ShapeDtypeStruct((t // CHUNK * 8, inner), F32)],
        args=(q, k, v, gates), semantics=("parallel", "parallel"), comm=comm)


def _gdn_scan_fwd(qe, kel, wb, u, qk, glb, nh, name):
    t, inner = u.shape
    hb, nb, rows, wide, sq, _, glb_spec = _gdn_specs(nh, inner, t, SCAN_HB, SCAN_NB)

    def body(qe_ref, kel_ref, wb_ref, u_ref, qk_ref, gl_ref, o_ref, vn_ref, sall_ref, s_ref):
        @pl.when(pl.program_id(1) == 0)
        def _():
            s_ref[...] = jnp.zeros(s_ref.shape, F32)

        cs = lambda h: slice(h * HEAD_DIM, (h + 1) * HEAD_DIM)
        for c in range(nb):
            rs = slice(c * CHUNK, (c + 1) * CHUNK)
            heads = lambda ref: jnp.stack([ref[rs, cs(h)] for h in range(hb)])
            s = s_ref[...]
            sall_ref[c] = s
            sb = _bf(s)
            vn = heads(u_ref) - _dot(heads(wb_ref), sb, BNN)
            vnb = _bf(vn)
            o = _dot(heads(qe_ref), sb, BNN) + _dot(qk_ref[:, rs, :], vnb, BNN)
            gl = jnp.stack([gl_ref[c * 8:c * 8 + 1, cs(h)] for h in range(hb)])
            s_ref[...] = s * gl + _dot(heads(kel_ref), vnb, BTN)
            for h in range(hb):
                vn_ref[rs, cs(h)] = vn[h]
                o_ref[rs, cs(h)] = o[h]

    return pl.pallas_call(
        body, name=name, grid=(nh // hb, t // rows),
        in_specs=[wide, wide, wide, wide, sq, glb_spec],
        out_specs=[wide, wide, pl.BlockSpec((nb, hb, HEAD_DIM, HEAD_DIM), lambda g, n: (n, g, 0, 0))],
        out_shape=[jax.ShapeDtypeStruct((t, inner), F32), jax.ShapeDtypeStruct((t, inner), F32),
                   jax.ShapeDtypeStruct((t // CHUNK, nh, HEAD_DIM, HEAD_DIM), F32)],
        scratch_shapes=[pltpu.VMEM((hb, HEAD_DIM, HEAD_DIM), F32)],
        compiler_params=_cparams("parallel", "arbitrary"),
    )(qe, kel, wb, u, qk, glb)


def _gdn_scan_bwd(do, qe, kel, wb, vn, qk, glb, sall, nh, name, comm=None):
    t, inner = do.shape
    hb, nb, rows, _, _, _, _ = _gdn_specs(nh, inner, t, SCAN_HB, SCAN_NB)
    last = t // rows - 1
    wide = pl.BlockSpec((rows, hb * HEAD_DIM), lambda g, n: (last - n, g))
    sq = pl.BlockSpec((hb, rows, CHUNK), lambda g, n: (g, last - n, 0))
    glb_spec = pl.BlockSpec((nb * 8, hb * HEAD_DIM), lambda g, n: (last - n, g))

    def body(do_ref, qe_ref, kel_ref, wb_ref, vn_ref, qk_ref, gl_ref, sall_ref,
             dvn_ref, dw_ref, dqe_ref, dkel_ref, dqk_ref, dgl_ref, ds_ref):
        @pl.when(pl.program_id(1) == 0)
        def _():
            ds_ref[...] = jnp.zeros(ds_ref.shape, F32)

        cs = lambda h: slice(h * HEAD_DIM, (h + 1) * HEAD_DIM)
        for c in reversed(range(nb)):
            rs = slice(c * CHUNK, (c + 1) * CHUNK)
            heads = lambda ref: jnp.stack([ref[rs, cs(h)] for h in range(hb)])
            ds, s = ds_ref[...], sall_ref[c]
            dsb, sb = _bf(ds), _bf(s)
            dob, vnb = _bf(heads(do_ref)), _bf(heads(vn_ref))
            dvn = _dot(qk_ref[:, rs, :], dob, BTN) + _dot(heads(kel_ref), dsb, BNN)
            dvnb = _bf(dvn)
            dw = -_dot(dvnb, sb, BNT)
            dqe = _dot(dob, sb, BNT)
            dkel = _dot(vnb, dsb, BNT)
            dqk_ref[:, rs, :] = _dot(dob, vnb, BNT)
            dgl = jnp.sum(jnp.sum(ds * s, axis=2, keepdims=True), axis=1, keepdims=True)
            gl = jnp.stack([gl_ref[c * 8:c * 8 + 1, cs(h)] for h in range(hb)])
            ds_ref[...] = ds * gl + _dot(heads(qe_ref), dob, BTN) - _dot(heads(wb_ref), dvnb, BTN)
            for h in range(hb):
                dvn_ref[rs, cs(h)] = dvn[h]
                dw_ref[rs, cs(h)] = dw[h]
                dqe_ref[rs, cs(h)] = dqe[h]
                dkel_ref[rs, cs(h)] = dkel[h]
                dgl_ref[c * 8:(c + 1) * 8, cs(h)] = jnp.broadcast_to(dgl[h], (8, HEAD_DIM))

    big = jax.ShapeDtypeStruct((t, inner), F32)
    return _grid_call(
        body, name=name, grid=(nh // hb, t // rows),
        in_specs=[wide, wide, wide, wide, wide, sq, glb_spec,
                  pl.BlockSpec((nb, hb, HEAD_DIM, HEAD_DIM), lambda g, n: (last - n, g, 0, 0))],
        out_specs=[wide] * 4 + [sq, glb_spec],
        out_shape=[big] * 4 + [jax.ShapeDtypeStruct((nh, t, CHUNK), F32),
                               jax.ShapeDtypeStruct((t // CHUNK * 8, inner), F32)],
        scratch_shapes=[pltpu.VMEM((hb, HEAD_DIM, HEAD_DIM), F32)],
        args=(do, qe, kel, wb, vn, qk, glb, sall), semantics=("parallel", "arbitrary"), comm=comm)


def _gdn_intra_bwd(q, k, v, gates, tm, w, u, dvn, dw, dqe, dkel, dqk, dglb, nh, name, comm=None):
    t, inner = q.shape
    hb, nb, rows, wide, sq, gts, glb = _gdn_specs(nh, inner, t)

    def body(q_ref, k_ref, v_ref, g_ref, tm_ref, w_ref, u_ref, dvn_ref, dw_ref, dqe_ref, dkel_ref, dqk_ref,
             dgl_ref, dq_ref, dk_ref, dv_ref, dg_ref):
        first = pl.program_id(0) * hb
        selg, selb = _head_select(first, hb, 0), _head_select(first, hb, nh)
        i, j = _iota2(CHUNK, CHUNK)
        lane8 = lax.broadcasted_iota(jnp.int32, (CHUNK, 8), 1)
        row = lax.broadcasted_iota(jnp.int32, (CHUNK, 1), 0)
        lower = jnp.where(j <= i, 1.0, 0.0).astype(F32)
        rsum = lambda x: jnp.sum(x, axis=-1, keepdims=True)
        units, rs, cs, (qv, kv, vv, wv, uv, dvn, dw, dqe, dkel), gcol, grow, bcol = _unit_inputs(
            (q_ref, k_ref, v_ref, w_ref, u_ref, dvn_ref, dw_ref, dqe_ref, dkel_ref), g_ref, selg, selb, hb, nb)
        nu = len(units)
        tmv = jnp.stack([tm_ref[h, rs(c), :] for c, h in units])
        dqk = jnp.where(j <= i, jnp.stack([dqk_ref[h, rs(c), :] for c, h in units]), 0.0)
        dgl = jnp.stack([dgl_ref[c * 8:c * 8 + 1, h * HEAD_DIM:h * HEAD_DIM + 1] for c, h in units])
        e, el, gl, decay = _decay_terms(gcol, grow)
        kb = kv * bcol
        qb, kbf, kbb = _bf(qv), _bf(kv), _bf(kb)
        dqkr = _bf(dqk * decay)
        dq = dqe * e + _dot(dqkr, kbf, BNN)
        dk = dkel * el + _dot(dqkr, qb, BTN)
        de = rsum(dqe * qv)
        del_ = rsum(dkel * kv)
        mq = dqk * _dot(qb, kbf, BNT) * decay
        dsol = _dot3(tmv, jnp.concatenate([dvn, dw], axis=2), BTN)
        dvb, dkbe = dsol[:, :, :HEAD_DIM], dsol[:, :, HEAD_DIM:]
        da = -jnp.where(j < i, _dot3(dsol, jnp.concatenate([uv, wv], axis=2), BNT), 0.0)
        dkk = _bf(da * decay)
        ma = da * _dot(kbb, kbf, BNT) * decay
        dkb = dkbe * e + _dot(dkk, kbf, BNN)
        de = de + rsum(dkbe * kb)
        dk = dk + _dot(dkk, kbb, BTN) + dkb * bcol
        dv = dvb * bcol
        dbeta = rsum(dkb * kv) + rsum(dvb * vv)
        m = mq + ma
        ones = jnp.ones((nu, CHUNK, LANES), F32)
        dgc = rsum(m) - _dot_exact(m, ones, BTN, False)[:, :, 0:1] + de * e - del_ * el
        tail = jnp.sum(del_ * el, axis=1, keepdims=True) + dgl * gl
        dgc = dgc + jnp.where(row == CHUNK - 1, tail, 0.0)
        for n, (c, h) in enumerate(units):
            dq_ref[rs(c), cs(h)] = dq[n]
            dk_ref[rs(c), cs(h)] = dk[n]
            dv_ref[rs(c), cs(h)] = dv[n]
        for c in range(nb):
            dgc_cols = jnp.zeros((CHUNK, 8), F32)
            dbeta_cols = jnp.zeros((CHUNK, 8), F32)
            for h in range(hb):
                dgc_cols = jnp.where(lane8 == h, dgc[c * hb + h], dgc_cols)
                dbeta_cols = jnp.where(lane8 == h, dbeta[c * hb + h], dbeta_cols)
            dg_cols = _dot_exact(lower, dgc_cols, TN, True)
            dg_ref[rs(c), :] = _dot_exact(dg_cols, selg, NN, False) + _dot_exact(dbeta_cols, selb, NN, False)

    big = jax.ShapeDtypeStruct((t, inner), F32)
    return _grid_call(
        body, name=name, grid=(nh // hb, t // rows),
        in_specs=[wide, wide, wide, gts, sq, wide, wide, wide, wide, wide, wide, sq, glb],
        out_specs=[wide, wide, wide, pl.BlockSpec((None, rows, LANES), lambda g, n: (g, n, 0))],
        out_shape=[big, big, big, jax.ShapeDtypeStruct((nh // hb, t, LANES), F32)],
        args=(q, k, v, gates, tm, w, u, dvn, dw, dqe, dkel, dqk, dglb), semantics=("parallel", "parallel"),
        comm=comm)


def _layer_a_fwd(x, hn, w_in_t, w_ab_t, conv_w, alog_row, dt_row, onw, nh, comm, w_out_of):
    t, d = x.shape
    inner = nh * HEAD_DIM
    proj = _mm(hn, w_in_t, "nt", t, 4 * inner, d, out_dtype=F32, name="a_proj")
    ab = _mm(hn, w_ab_t, "nt", t, LANES, d, out_dtype=F32, name="a_proj_ab")
    gates = _gates_fwd(ab, alog_row, dt_row, nh, "a_gates")
    q = _conv_fwd(proj, conv_w, 0, inner, "q", "a_conv_q")
    k = _conv_fwd(proj, conv_w, inner, inner, "k", "a_conv_k")
    v = _conv_fwd(proj, conv_w, 2 * inner, inner, "v", "a_conv_v")
    qe, kel, wb, w, u, qk, tm, glb, *carried = _gdn_intra_fwd(q, k, v, gates, nh, "a_intra", comm)
    o, vn, sall = _gdn_scan_fwd(qe, kel, wb, u, qk, glb, nh, "a_scan")
    g = _gate_fwd(o, proj, 3 * inner, inner, "a_gate", norm_w=onw)
    w_out = w_out_of(carried)
    h1 = _mm(g, w_out, "nn", t, d, inner, out_dtype=F32, name="a_out", res=x)
    return h1, (hn, proj, ab, gates, q, k, v, qe, kel, wb, w, u, qk, tm, glb, o, vn, sall, g), w_out, carried


def _layer_a_bwd(dh1, dh1b, x, nw, w_in_t, w_ab_t, conv_w, alog_row, dt_row, onw, w_out, nh, saved, comms_of,
                 own_comm):
    hn, proj, ab, gates, q, k, v, qe, kel, wb, w, u, qk, tm, glb, o, vn, sall, g = saved
    t, d = x.shape
    inner = w_out.shape[0]
    dg = _mm(dh1b, w_out, "nt", t, inner, d, out_dtype=F32, name="a_dgate")
    dw_out = _mm(g, dh1b, "tn", inner, d, t, out_dtype=BF16, name="a_dwout")
    comm_scan, comm_intra = comms_of(dw_out)
    do, dproj, donw = _gate_bwd(dg, o, proj, 3 * inner, inner, "a_gate_bwd", F32, norm_w=onw,
                                dest=(None, 3 * inner, 4 * inner))
    dvn, dw, dqe, dkel, dqk, dglb, *carried_scan = _gdn_scan_bwd(do, qe, kel, wb, vn, qk, glb, sall, nh,
                                                                 "a_scan_bwd", comm_scan)
    dq, dk, dv, dgates, *carried = _gdn_intra_bwd(q, k, v, gates, tm, w, u, dvn, dw, dqe, dkel, dqk, dglb, nh,
                                                  "a_intra_bwd", comm_intra)
    carried = carried_scan + carried
    dproj, dcq = _conv_bwd(dq, proj, conv_w, 0, inner, "q", "a_conv_q_bwd", dest=(dproj, 0, 4 * inner))
    dproj, dck = _conv_bwd(dk, proj, conv_w, inner, inner, "k", "a_conv_k_bwd", dest=(dproj, inner, 4 * inner))
    dproj, dcv = _conv_bwd(dv, proj, conv_w, 2 * inner, inner, "v", "a_conv_v_bwd",
                           dest=(dproj, 2 * inner, 4 * inner))
    dab, dsmall = _gates_bwd(ab, alog_row, dt_row, dgates, nh, "a_gates_bwd")
    dw_in_t = _mm(dproj, hn, "tn", 4 * inner, d, t, out_dtype=BF16, name="a_dwin")
    dw_ab_t = _mm(dab, hn, "tn", LANES, d, t, out_dtype=BF16, name="a_dwin_ab")
    dconv = jnp.concatenate([dcq[:CONV_K], dck[:CONV_K], dcv[:CONV_K]], axis=1)
    dhn = _mm(dab, w_ab_t, "nn", t, d, LANES, out_dtype=F32, name="a_dhn_ab")
    own = own_comm(dw_in_t, dw_ab_t, dconv)
    dhn = _mm(dproj, w_in_t, "nn", t, d, 4 * inner, out_dtype=F32, name="a_dhn", res=dhn, comm=own)
    dhn, carried_own = dhn if own is not None else (dhn, [])
    dx, _, dnw = _rms_bwd(x, nw, dhn, dh1, "a_rms_bwd")
    return dx, dnw, dsmall, donw, carried, carried_own


def _rows_of(a, rows):
    flat = a.reshape(-1)
    return jnp.pad(flat, (0, rows * LANES - flat.shape[0])).reshape(rows, LANES)


def _to_slabs(g, axis):
    shape = g.shape[:axis] + (N_DEV, g.shape[axis] // N_DEV) + g.shape[axis + 1:]
    return jnp.moveaxis(g.reshape(shape), axis, 0)


def _from_slabs(s, axis):
    m = jnp.moveaxis(s, 0, axis)
    return m.reshape(m.shape[:axis] + (m.shape[axis] * m.shape[axis + 1],) + m.shape[axis + 2:])


def kernel(x, norm_w, a_w_in, a_conv_w, a_a_log, a_dt_bias, a_out_norm_w, a_w_out, b_w_in, b_q_norm_w, b_k_norm_w, b_rel_bias, b_w_out, loss_target, m_norm_w, m_a_w_in, m_a_conv_w, m_a_a_log, m_a_dt_bias, m_a_out_norm_w, m_a_w_out, m_b_w_in, m_b_q_norm_w, m_b_k_norm_w, m_b_rel_bias, m_b_w_out, v_norm_w, v_a_w_in, v_a_conv_w, v_a_a_log, v_a_dt_bias, v_a_out_norm_w, v_a_w_out, v_b_w_in, v_b_q_norm_w, v_b_k_norm_w, v_b_rel_bias, v_b_w_out):
    xs, target = x[0], loss_target[0]
    nh = a_a_log.shape[-1]
    inner = N_DEV * a_w_out.shape[1]

    d = xs.shape[1]
    nw0, nw1 = norm_w[0:1], norm_w[1:2]
    hn0, (ga_in, g_conv) = _rms_fwd(
        xs, nw0, "a_rms", comm=_RoutedGather([a_w_in[0].T.astype(BF16), a_conv_w[0]]))
    wa_in_t = ga_in.reshape(-1, d)
    wa_ab_t = jnp.pad(wa_in_t[4 * inner:], ((0, LANES - 2 * nh), (0, 0)))
    conv_w = _from_slabs(g_conv, 1)
    alog_row = jnp.pad(a_a_log, ((0, 0), (0, LANES - nh)))
    dt_row = jnp.pad(a_dt_bias, ((0, 0), (0, LANES - nh)))

    h1, saved_a, wa_out, (gb_in, gb_out, _) = _layer_a_fwd(
        xs, hn0, wa_in_t, wa_ab_t, conv_w, alog_row, dt_row, a_out_norm_w, nh,
        _Comm("gather", [b_w_in[0].T.astype(BF16), b_w_out[0].astype(BF16), a_w_out[0].astype(BF16)]),
        lambda gathered: _from_slabs(gathered[2], 0))
    wb_in_t = gb_in.reshape(-1, d)
    wb_out = _from_slabs(gb_out, 0)
    bias = _bias_tiles(_pad_rel_bias(b_rel_bias[0]), "b_bias_tiles")
    (dh2, dh2b, loss_row), saved_b = _layer_b_fwd(h1, nw1, wb_in_t, b_q_norm_w, b_k_norm_w, bias, wb_out, target)

    dh1, dh1b, dnw1, dwb_in_t, dqw, dkw, drb, dwb_out = _layer_b_bwd(
        dh2, dh2b, h1, nw1, wb_in_t, b_q_norm_w, b_k_norm_w, bias, wb_out, saved_b)

    def exchange_early(dwa_out):
        return (_Comm("exchange", [_to_slabs(dwb_out, 0).astype(BF16), _to_slabs(dwa_out, 0).astype(BF16)]),
                _Comm("exchange", [dwb_in_t.reshape(N_DEV, -1, d).astype(BF16)]))

    def exchange_last(dwa_in_t, dwa_ab_t, dconv):
        full = jnp.concatenate([dwa_in_t, dwa_ab_t[:2 * nh]], axis=0)
        return _Comm("exchange", [full.reshape(N_DEV, -1, d).astype(BF16), _to_slabs(dconv, 1)])

    dx, dnw0, dsmall, donw, (pb_out, pa_out, pb_in), (pa_in, p_conv) = _layer_a_bwd(
        dh1, dh1b, xs, nw0, wa_in_t, wa_ab_t, conv_w, alog_row, dt_row, a_out_norm_w, wa_out, nh, saved_a,
        exchange_early, exchange_last)
    big = {}
    for name, p, w, m, v in (("a_w_in", pa_in, a_w_in, m_a_w_in, v_a_w_in),
                             ("a_w_out", pa_out, a_w_out, m_a_w_out, v_a_w_out),
                             ("b_w_in", pb_in, b_w_in, m_b_w_in, v_b_w_in),
                             ("b_w_out", pb_out, b_w_out, m_b_w_out, v_b_w_out),
                             ("a_conv_w", p_conv, a_conv_w, m_a_conv_w, v_a_conv_w)):
        big[name] = [o[None] for o in _adamw(p, w[0], m[0], v[0], "adamw_" + name,
                                             transposed=name in ("a_w_in", "b_w_in"))]

    small = (("norm_w", norm_w, m_norm_w, v_norm_w, jnp.concatenate([dnw0, dnw1], axis=0)),
             ("a_a_log", a_a_log, m_a_a_log, v_a_a_log, dsmall[0:1, :nh]),
             ("a_dt_bias", a_dt_bias, m_a_dt_bias, v_a_dt_bias, dsmall[1:2, :nh]),
             ("a_out_norm_w", a_out_norm_w, m_a_out_norm_w, v_a_out_norm_w, donw),
             ("b_q_norm_w", b_q_norm_w, m_b_q_norm_w, v_b_q_norm_w, dqw),
             ("b_k_norm_w", b_k_norm_w, m_b_k_norm_w, v_b_k_norm_w, dkw),
             ("b_rel_bias", b_rel_bias, m_b_rel_bias, v_b_rel_bias, drb))
    rows = [8 * (-(-w.size // (8 * LANES))) for _, w, _, _, _ in small]
    pack = lambda arrs: jnp.concatenate([_rows_of(a, r) for a, r in zip(arrs, rows)] + [jnp.zeros((8, LANES), F32)], axis=0)
    g_pack = jnp.concatenate([_rows_of(g, r) for (_, _, _, _, g), r in zip(small, rows)]
                             + [jnp.broadcast_to(loss_row, (8, LANES))], axis=0)
    (g_all,) = _comm_call(_Comm("gather", [g_pack]), "gather_small_grads")
    outs_small = _adamw(g_all, pack([s[1] for s in small]), pack([s[2] for s in small]),
                        pack([s[3] for s in small]), "adamw_small")
    start = 0
    for (name, w, _, _, _), r in zip(small, rows):
        big[name] = [o[start:start + r].reshape(-1)[:w.size].reshape(w.shape) for o in outs_small]
        start += r
    loss = outs_small[0][start, 0]

    order = ("norm_w", "a_w_in", "a_conv_w", "a_a_log", "a_dt_bias", "a_out_norm_w", "a_w_out", "b_w_in",
             "b_q_norm_w", "b_k_norm_w", "b_rel_bias", "b_w_out")
    return (loss, dx[None]) + tuple(big[n][i] for i in range(4) for n in order)
```

```python
import functools

import jax
import jax.numpy as jnp
from jax import lax
from jax.experimental import pallas as pl
from jax.experimental.pallas import tpu as pltpu

F32 = jnp.float32
BF16 = jnp.bfloat16
MESH_IDS = pl.DeviceIdType.MESH
N_DEV = 8
CHUNK = 64
HEAD_DIM = 128
EPS = 1e-6
CONV_K = 4
LEFT_CHUNKS = 8
REL_CLIP = 256
Q_TILE = LEFT_CHUNKS * CHUNK
ADAM_LR = 0.001
ADAM_B1 = 0.9
ADAM_B2 = 0.999
ADAM_EPS = 1e-08
ADAM_WD = 0.01
ADAM_STEP = 10
NEG_BIG = -1e30
VMEM_LIMIT_BYTES = 56 * 1024 * 1024
HIGHEST = lax.Precision.HIGHEST
ANY = pl.BlockSpec(memory_space=pl.ANY)


def _cparams(*sem):
    return pltpu.CompilerParams(dimension_semantics=tuple(sem), vmem_limit_bytes=VMEM_LIMIT_BYTES)


NN, NT, TN = (((1,), (0,)), ((), ())), (((1,), (1,)), ((), ())), (((0,), (0,)), ((), ()))
BNN, BNT, BTN = (((2,), (1,)), ((0,), (0,))), (((2,), (2,)), ((0,), (0,))), (((1,), (1,)), ((0,), (0,)))


def _dot(a, b, dims, precision=None):
    return lax.dot_general(a, b, dims, preferred_element_type=F32, precision=precision)


def _nn(a, b, precision=None):
    return _dot(a, b, NN, precision)


def _nt(a, b, precision=None):
    return _dot(a, b, NT, precision)


def _tn(a, b, precision=None):
    return _dot(a, b, TN, precision)


def _bf(x):
    return x.astype(BF16)


def _split(x, pieces=2):
    out = []
    for _ in range(pieces - 1):
        hi = x.astype(BF16)
        out.append(hi)
        x = x - hi.astype(F32)
    return out + [x.astype(BF16)]


def _dot3(a, b, dims):
    (ah, al), (bh, bl) = _split(a), _split(b)
    return _dot(ah, bh, dims) + (_dot(ah, bl, dims) + _dot(al, bh, dims))


def _dot_exact(a, b, dims, split_b):
    if split_b:
        a = a.astype(BF16)
        parts = [_dot(a, p, dims) for p in _split(b, 3)]
    else:
        b = b.astype(BF16)
        parts = [_dot(p, b, dims) for p in _split(a, 3)]
    return parts[0] + (parts[1] + parts[2])


def _sigmoid(x):
    return 0.5 * jnp.tanh(0.5 * x) + 0.5


def _silu(x):
    return x * _sigmoid(x)


def _dsilu(x):
    s = _sigmoid(x)
    return s * (1.0 + x * (1.0 - s))


def _my_pos():
    return lax.axis_index("x"), lax.axis_index("y"), lax.axis_index("c")


def _peers(x, y, c):
    def flip(v, f):
        return 1 - v if f else v

    return [(flip(x, kx), flip(y, ky), flip(c, kc)) for kx in (0, 1) for ky in (0, 1) for kc in (0, 1)][1:]


def _lin(p):
    return 4 * p[0] + 2 * p[1] + p[2]


class _Comm:
    def __init__(self, kind, arrays):
        self.kind, self.arrays, self.n = kind, list(arrays), len(arrays)

    def out_shape(self):
        lead = (N_DEV,) if self.kind == "gather" else ()
        return [jax.ShapeDtypeStruct(lead + a.shape, a.dtype) for a in self.arrays]

    def scratch(self):
        return [pltpu.SemaphoreType.DMA((7 * self.n,)), pltpu.SemaphoreType.DMA((7 * self.n,)),
                pltpu.SemaphoreType.DMA((self.n,))]

    def _copies(self, ins, outs, sems, arrivals):
        send_sems, recv_sems, local_sems = sems
        x, y, c = _my_pos()
        me = _lin((x, y, c))
        gather = self.kind == "gather"
        mine = [ins[t] if gather else ins[t].at[me] for t in range(self.n)]
        remote = []
        for k, peer in enumerate(_peers(x, y, c)):
            for t in range(self.n):
                if arrivals:
                    src, dst = mine[t], outs[t].at[_lin(peer)]
                else:
                    src, dst = (ins[t] if gather else ins[t].at[_lin(peer)]), outs[t].at[me]
                remote.append(pltpu.make_async_remote_copy(
                    src_ref=src, dst_ref=dst, send_sem=send_sems.at[k * self.n + t],
                    recv_sem=recv_sems.at[k * self.n + t], device_id=peer, device_id_type=MESH_IDS))
        if arrivals:
            return remote
        return [pltpu.make_async_copy(mine[t], outs[t].at[me], local_sems.at[t]) for t in range(self.n)], remote

    def start(self, ins, outs, sems):
        local, sends = self._copies(ins, outs, sems, False)
        for cp in local + sends:
            cp.start()

    def finish(self, ins, outs, sems):
        for cp in self._copies(ins, outs, sems, True):
            cp.wait_recv()
        local, sends = self._copies(ins, outs, sems, False)
        for cp in sends:
            cp.wait_send()
        for cp in local:
            cp.wait()


def _xor(a, b):
    return a + b - 2 * a * b


class _RoutedGather(_Comm):
    def __init__(self, arrays):
        super().__init__("gather", arrays)

    def _plan(self, outs, sems):
        send_sems, recv_sems, _ = sems
        x, y, c = _my_pos()
        sib, xn, yn, dg = (x, y, 1 - c), (1 - x, y, c), (x, 1 - y, c), (1 - x, 1 - y, c)
        via = (_xor(x, 1 - c), _xor(y, c), c)
        onto = (_xor(x, c), _xor(y, 1 - c), c)
        routes = [(None, sib, sib), (None, xn, xn), (None, yn, yn), (via, onto, dg),
                  (xn, sib, (1 - x, y, 1 - c)), (yn, sib, (x, 1 - y, 1 - c)), (dg, sib, (1 - x, 1 - y, 1 - c))]

        def copy(k, t, src, slot, target):
            return pltpu.make_async_remote_copy(
                src_ref=src, dst_ref=outs[t].at[slot], send_sem=send_sems.at[k * self.n + t],
                recv_sem=recv_sems.at[k * self.n + t], device_id=target, device_id_type=MESH_IDS)

        return (x, y, c), routes, copy

    def start(self, ins, outs, sems):
        me, routes, copy = self._plan(outs, sems)
        for t in range(self.n):
            pltpu.make_async_copy(ins[t], outs[t].at[_lin(me)], sems[2].at[t]).start()
            for k in range(3):
                copy(k, t, ins[t], _lin(me), routes[k][1]).start()

    def finish(self, ins, outs, sems):
        me, routes, copy = self._plan(outs, sems)

        def arrived(k):
            for t in range(self.n):
                copy(k, t, ins[t], _lin(routes[k][2]), me).wait_recv()

        def pass_on(k):
            for t in range(self.n):
                copy(k, t, outs[t].at[_lin(routes[k][0])], _lin(routes[k][0]), routes[k][1]).start()

        arrived(1)
        arrived(2)
        for k in (3, 4, 5):
            pass_on(k)
        arrived(3)
        pass_on(6)
        for k in (0, 4, 5, 6):
            arrived(k)
        for t in range(self.n):
            for k in range(7):
                src = ins[t] if k < 3 else outs[t].at[_lin(routes[k][0])]
                copy(k, t, src, _lin(me), routes[k][1]).wait_send()
            pltpu.make_async_copy(ins[t], outs[t].at[_lin(me)], sems[2].at[t]).wait()


def _comm_call(comm, name):
    n = comm.n

    def body(*refs):
        ins, outs, sems = refs[:n], refs[n:2 * n], refs[2 * n:]
        comm.start(ins, outs, sems)
        comm.finish(ins, outs, sems)

    return pl.pallas_call(
        body, name=name, out_shape=comm.out_shape(), in_specs=[ANY] * n, out_specs=[ANY] * n,
        scratch_shapes=comm.scratch(),
    )(*comm.arrays)


def _grid_call(body, *, name, grid, in_specs, out_specs, out_shape, args, scratch_shapes=(), semantics=None, comm=None):
    if comm is None:
        return pl.pallas_call(
            body, name=name, grid=grid, in_specs=in_specs, out_specs=out_specs, out_shape=out_shape,
            scratch_shapes=list(scratch_shapes), compiler_params=_cparams(*semantics),
        )(*args)
    n_in, n_out, n_sc, n = len(in_specs), len(out_specs), len(scratch_shapes), comm.n

    def full(*refs):
        ins, refs = refs[:n_in], refs[n_in:]
        cins, refs = refs[:n], refs[n:]
        outs, refs = refs[:n_out], refs[n_out:]
        couts, refs = refs[:n], refs[n:]
        scratch, sems = refs[:n_sc], refs[n_sc:]
        ids = [pl.program_id(a) for a in range(len(grid))]
        first = functools.reduce(jnp.logical_and, [i == 0 for i in ids])
        last = functools.reduce(jnp.logical_and, [i == g - 1 for i, g in zip(ids, grid)])

        @pl.when(first)
        def _():
            comm.start(cins, couts, sems)

        body(*ins, *outs, *scratch)

        @pl.when(last)
        def _():
            comm.finish(cins, couts, sems)

    return pl.pallas_call(
        full, name=name, grid=grid, in_specs=list(in_specs) + [ANY] * n, out_specs=list(out_specs) + [ANY] * n,
        out_shape=list(out_shape) + comm.out_shape(), scratch_shapes=list(scratch_shapes) + comm.scratch(),
        compiler_params=_cparams(*(["arbitrary"] * len(grid))),
    )(*(list(args) + comm.arrays))


def _mm(a, b, mode, m, n, k, *, out_dtype, name, tm=1024, tn=1024, tk=2048,
        a_m0=0, a_k0=0, b_n0=0, b_k0=0, res=None, comm=None, loss_target=None):
    tm, tn, tk = min(tm, m), min(tn, n), min(tk, k)
    nm, nn, nk = m // tm, n // tn, k // tk
    assert nm * tm == m and nn * tn == n and nk * tk == k
    am, ak, bn, bk = a_m0 // tm, a_k0 // tk, b_n0 // tn, b_k0 // tk
    assert am * tm == a_m0 and ak * tk == a_k0 and bn * tn == b_n0 and bk * tk == b_k0
    if mode == "tn":
        a_spec = pl.BlockSpec((tk, tm), lambda i, j, q: (q + ak, i + am))
        a_dims = (0,)
    else:
        a_spec = pl.BlockSpec((tm, tk), lambda i, j, q: (i + am, q + ak))
        a_dims = (1,)
    if mode == "nt":
        b_spec = pl.BlockSpec((tn, tk), lambda i, j, q: (j + bn, q + bk))
        b_dims = (1,)
    else:
        b_spec = pl.BlockSpec((tk, tn), lambda i, j, q: (q + bk, j + bn))
        b_dims = (0,)
    o_spec = pl.BlockSpec((tm, tn), lambda i, j, q: (i, j))
    has_res = res is not None
    has_loss = loss_target is not None
    n_in = 2 + has_res + has_loss
    n_out = 3 if has_loss else 1

    def body(*refs):
        a_ref, b_ref = refs[0], refs[1]
        res_ref = refs[2] if has_res else None
        o_ref = refs[n_in]
        p = _dot(a_ref[...], b_ref[...], ((a_dims, b_dims), ((), ())))

        def finish(total):
            if has_res:
                total = total + res_ref[...].astype(F32)
            if not has_loss:
                o_ref[...] = total.astype(out_dtype)
                return
            err = total - refs[n_in - 1][...]
            grad = err * (1.0 / n)
            o_ref[...] = grad
            refs[n_in + 1][...] = grad.astype(BF16)
            l_ref = refs[n_in + 2]
            part = jnp.zeros((1, LANES), F32) + 0.5 * jnp.sum(err * err) * (1.0 / n)
            first = (pl.program_id(0) == 0) & (pl.program_id(1) == 0)

            @pl.when(first)
            def _():
                l_ref[...] = part

            @pl.when(jnp.logical_not(first))
            def _():
                l_ref[...] += part

        if nk == 1:
            finish(p)
        else:
            acc_ref = refs[n_in + n_out]
            q = pl.program_id(2)

            @pl.when(q == 0)
            def _():
                acc_ref[...] = p

            @pl.when(q > 0)
            def _():
                acc_ref[...] += p

            @pl.when(q == nk - 1)
            def _():
                finish(acc_ref[...])

    extra_in = ([res] if has_res else []) + ([loss_target] if has_loss else [])
    if has_loss:
        return _grid_call(
            body, name=name, grid=(nm, nn, nk), in_specs=[a_spec, b_spec] + [o_spec] * len(extra_in),
            out_specs=[o_spec, o_spec, pl.BlockSpec((1, LANES), lambda i, j, q: (0, 0))],
            out_shape=[jax.ShapeDtypeStruct((m, n), F32), jax.ShapeDtypeStruct((m, n), BF16),
                       jax.ShapeDtypeStruct((1, LANES), F32)],
            scratch_shapes=[pltpu.VMEM((tm, tn), F32)] if nk > 1 else [],
            args=[a, b] + extra_in, semantics=("arbitrary", "arbitrary", "arbitrary"))
    out, *carried = _grid_call(
        body, name=name, grid=(nm, nn, nk),
        in_specs=[a_spec, b_spec] + [o_spec] * len(extra_in),
        out_specs=[o_spec], out_shape=[jax.ShapeDtypeStruct((m, n), out_dtype)],
        scratch_shapes=[pltpu.VMEM((tm, tn), F32)] if nk > 1 else [],
        args=[a, b] + extra_in, semantics=("parallel", "parallel", "arbitrary"), comm=comm)
    return out if comm is None else (out, carried)


def _rms_fwd(x, w, name, tr=512, comm=None):
    t, d = x.shape
    tr = min(tr, t)

    def body(x_ref, w_ref, o_ref):
        xv = x_ref[...]
        r = lax.rsqrt(jnp.mean(xv * xv, axis=-1, keepdims=True) + EPS)
        o_ref[...] = (xv * r * w_ref[...]).astype(BF16)

    out, *carried = _grid_call(
        body, name=name, grid=(t // tr,),
        in_specs=[pl.BlockSpec((tr, d), lambda i: (i, 0)), pl.BlockSpec((1, d), lambda i: (0, 0))],
        out_specs=[pl.BlockSpec((tr, d), lambda i: (i, 0))],
        out_shape=[jax.ShapeDtypeStruct((t, d), BF16)], args=(x, w), semantics=("parallel",), comm=comm)
    return out if comm is None else (out, carried)


def _rms_bwd(x, w, dy, dres, name, tr=256):
    t, d = x.shape
    tr = min(tr, t)

    def body(x_ref, w_ref, dy_ref, dres_ref, dx_ref, dxb_ref, dw_ref):
        xv = x_ref[...]
        dyv = dy_ref[...].astype(F32)
        r = lax.rsqrt(jnp.mean(xv * xv, axis=-1, keepdims=True) + EPS)
        gy = dyv * w_ref[...]
        proj = jnp.sum(gy * xv, axis=-1, keepdims=True) * (1.0 / d)
        dx = dres_ref[...] + r * gy - xv * (r * r * r) * proj
        dx_ref[...] = dx
        dxb_ref[...] = dx.astype(BF16)
        part = jnp.sum(dyv * xv * r, axis=0, keepdims=True)

        @pl.when(pl.program_id(0) == 0)
        def _():
            dw_ref[...] = part

        @pl.when(pl.program_id(0) > 0)
        def _():
            dw_ref[...] += part

    row = pl.BlockSpec((tr, d), lambda i: (i, 0))
    vec = pl.BlockSpec((1, d), lambda i: (0, 0))
    return pl.pallas_call(
        body, name=name, grid=(t // tr,),
        in_specs=[row, vec, row, row], out_specs=[row, row, vec],
        out_shape=[jax.ShapeDtypeStruct((t, d), F32), jax.ShapeDtypeStruct((t, d), BF16),
                   jax.ShapeDtypeStruct((1, d), F32)],
        compiler_params=_cparams("arbitrary"),
    )(x, w, dy, dres)


def _adamw(parts, w, m, v, name, tr=128, transposed=False):
    r, c = w.shape
    tr = tr if r % tr == 0 else r
    c1 = 1.0 - ADAM_B1 ** ADAM_STEP
    c2 = 1.0 - ADAM_B2 ** ADAM_STEP

    def body(p_ref, w_ref, m_ref, v_ref, g_ref, d_ref, nm_ref, nv_ref):
        g = p_ref[0].astype(F32)
        for s in range(1, N_DEV):
            g = g + p_ref[s].astype(F32)
        if transposed:
            i, j = lax.broadcasted_iota(jnp.int32, (tr, tr), 0), lax.broadcasted_iota(jnp.int32, (tr, tr), 1)
            g = _dot_exact(jnp.where(i == j, 1.0, 0.0), g, NT, True)
        nm = ADAM_B1 * m_ref[...] + (1.0 - ADAM_B1) * g
        nv = ADAM_B2 * v_ref[...] + (1.0 - ADAM_B2) * (g * g)
        m_hat = nm / c1
        v_hat = nv / c2
        g_ref[...] = g
        d_ref[...] = -ADAM_LR * (m_hat / (jnp.sqrt(v_hat) + ADAM_EPS) + ADAM_WD * w_ref[...])
        nm_ref[...] = nm
        nv_ref[...] = nv

    blk = pl.BlockSpec((tr, c), lambda i: (i, 0))
    p_spec = (pl.BlockSpec((N_DEV, c, tr), lambda i: (0, 0, i)) if transposed
              else pl.BlockSpec((N_DEV, tr, c), lambda i: (0, i, 0)))
    return pl.pallas_call(
        body, name=name, grid=(r // tr,),
        in_specs=[p_spec, blk, blk, blk],
        out_specs=[blk] * 4, out_shape=[jax.ShapeDtypeStruct((r, c), F32)] * 4,
        compiler_params=_cparams("parallel"),
    )(parts, w, m, v)


ROW_TILE, ROW_HEADS = 256, 16
CONV_ROWS, CONV_HEADS = 512, 8


def _window(dest, inner, n_in, out_index):
    if dest is None:
        return inner, 0, [], [], {}
    buf, col0, total = dest
    if buf is None:
        return total, col0, [], [], {}
    return total, col0, [buf], [ANY], {n_in: out_index}


def _skip_ref(body, at, count):
    return body if count == 0 else (lambda *refs: body(*refs[:at], *refs[at + count:]))


def _heads_of(x, nh):
    return [x[:, h * HEAD_DIM:(h + 1) * HEAD_DIM] for h in range(nh)]


def _headnorm_fwd(proj, w, col0, inner, name, tr=ROW_TILE, hb=ROW_HEADS):
    t = proj.shape[0]
    tr = min(tr, t)
    hb = min(hb, inner // HEAD_DIM)
    wc = hb * HEAD_DIM
    c0 = col0 // wc

    def body(x_ref, w_ref, o_ref):
        outs = []
        for xh in _heads_of(x_ref[...], hb):
            r = lax.rsqrt(jnp.mean(xh * xh, axis=-1, keepdims=True) + EPS)
            outs.append((xh * r * w_ref[...]).astype(BF16))
        o_ref[...] = jnp.concatenate(outs, axis=1)

    return pl.pallas_call(
        body, name=name, grid=(t // tr, inner // wc),
        in_specs=[pl.BlockSpec((tr, wc), lambda i, j: (i, j + c0)), pl.BlockSpec((1, HEAD_DIM), lambda i, j: (0, 0))],
        out_specs=pl.BlockSpec((tr, wc), lambda i, j: (i, j)),
        out_shape=jax.ShapeDtypeStruct((t, inner), BF16),
        compiler_params=_cparams("parallel", "parallel"),
    )(proj, w)


def _headnorm_bwd(dy, proj, w, col0, inner, name, tr=ROW_TILE, hb=ROW_HEADS, dest=None):
    t = proj.shape[0]
    tr = min(tr, t)
    hb = min(hb, inner // HEAD_DIM)
    wc = hb * HEAD_DIM
    c0 = col0 // wc
    width, out0, more, more_specs, aliases = _window(dest, inner, 3, 0)

    def body(dy_ref, x_ref, w_ref, dx_ref, dw_ref):
        outs = []
        part = jnp.zeros((1, HEAD_DIM), F32)
        for dyh, xh in zip(_heads_of(dy_ref[...], hb), _heads_of(x_ref[...], hb)):
            r = lax.rsqrt(jnp.mean(xh * xh, axis=-1, keepdims=True) + EPS)
            gy = dyh * w_ref[...]
            pr = jnp.sum(gy * xh, axis=-1, keepdims=True) * (1.0 / HEAD_DIM)
            outs.append((r * gy - xh * (r * r * r) * pr).astype(BF16))
            part = part + jnp.sum(dyh * xh * r, axis=0, keepdims=True)
        dx_ref[...] = jnp.concatenate(outs, axis=1)
        first = (pl.program_id(0) == 0) & (pl.program_id(1) == 0)

        @pl.when(first)
        def _():
            dw_ref[...] = part

        @pl.when(jnp.logical_not(first))
        def _():
            dw_ref[...] += part

    blk = pl.BlockSpec((tr, wc), lambda i, j: (i, j))
    return pl.pallas_call(
        _skip_ref(body, 3, len(more)), name=name, grid=(t // tr, inner // wc),
        in_specs=[blk, pl.BlockSpec((tr, wc), lambda i, j: (i, j + c0)),
                  pl.BlockSpec((1, HEAD_DIM), lambda i, j: (0, 0))] + more_specs,
        out_specs=[pl.BlockSpec((tr, wc), lambda i, j: (i, j + out0 // wc)),
                   pl.BlockSpec((1, HEAD_DIM), lambda i, j: (0, 0))],
        out_shape=[jax.ShapeDtypeStruct((t, width), BF16), jax.ShapeDtypeStruct((1, HEAD_DIM), F32)],
        input_output_aliases=aliases, compiler_params=_cparams("arbitrary", "arbitrary"),
    )(dy, proj, w, *more)


def _gate_fwd(o, proj, zcol0, inner, name, norm_w=None, tr=ROW_TILE, hb=ROW_HEADS):
    t = o.shape[0]
    tr = min(tr, t)
    hb = min(hb, inner // HEAD_DIM)
    wc = hb * HEAD_DIM
    c0 = zcol0 // wc
    has_w = norm_w is not None

    def body(*refs):
        o_ref, z_ref = refs[0], refs[1]
        out_ref = refs[2 + has_w]
        outs = []
        for oh, zh in zip(_heads_of(o_ref[...], hb), _heads_of(z_ref[...], hb)):
            if has_w:
                r = lax.rsqrt(jnp.mean(oh * oh, axis=-1, keepdims=True) + EPS)
                oh = oh * r * refs[2][...]
            outs.append((oh * _silu(zh)).astype(BF16))
        out_ref[...] = jnp.concatenate(outs, axis=1)

    blk = pl.BlockSpec((tr, wc), lambda i, j: (i, j))
    vec = pl.BlockSpec((1, HEAD_DIM), lambda i, j: (0, 0))
    return pl.pallas_call(
        body, name=name, grid=(t // tr, inner // wc),
        in_specs=[blk, pl.BlockSpec((tr, wc), lambda i, j: (i, j + c0))] + ([vec] if has_w else []),
        out_specs=blk, out_shape=jax.ShapeDtypeStruct((t, inner), BF16),
        compiler_params=_cparams("parallel", "parallel"),
    )(*([o, proj] + ([norm_w] if has_w else [])))


def _gate_bwd(dg, o, proj, zcol0, inner, name, do_dtype, norm_w=None, tr=ROW_TILE, hb=ROW_HEADS, dest=None):
    t = o.shape[0]
    tr = min(tr, t)
    hb = min(hb, inner // HEAD_DIM)
    wc = hb * HEAD_DIM
    c0 = zcol0 // wc
    has_w = norm_w is not None
    width, out0, more, more_specs, aliases = _window(dest, inner, 3 + has_w, 1)

    def body(*refs):
        dg_ref, o_ref, z_ref = refs[0], refs[1], refs[2]
        do_ref, dz_ref = refs[3 + has_w], refs[4 + has_w]
        dos, dzs = [], []
        part = jnp.zeros((1, HEAD_DIM), F32)
        for dgh, oh, zh in zip(_heads_of(dg_ref[...], hb), _heads_of(o_ref[...], hb), _heads_of(z_ref[...], hb)):
            dy = dgh * _silu(zh)
            if has_w:
                w = refs[3][...]
                r = lax.rsqrt(jnp.mean(oh * oh, axis=-1, keepdims=True) + EPS)
                on = oh * r
                dzs.append((dgh * on * w * _dsilu(zh)).astype(BF16))
                gy = dy * w
                pr = jnp.sum(gy * oh, axis=-1, keepdims=True) * (1.0 / HEAD_DIM)
                dos.append((r * gy - oh * (r * r * r) * pr).astype(do_dtype))
                part = part + jnp.sum(dy * on, axis=0, keepdims=True)
            else:
                dzs.append((dgh * oh * _dsilu(zh)).astype(BF16))
                dos.append(dy.astype(do_dtype))
        do_ref[...] = jnp.concatenate(dos, axis=1)
        dz_ref[...] = jnp.concatenate(dzs, axis=1)
        if has_w:
            dw_ref = refs[6]
            first = (pl.program_id(0) == 0) & (pl.program_id(1) == 0)

            @pl.when(first)
            def _():
                dw_ref[...] = part

            @pl.when(jnp.logical_not(first))
            def _():
                dw_ref[...] += part

    blk = pl.BlockSpec((tr, wc), lambda i, j: (i, j))
    vec = pl.BlockSpec((1, HEAD_DIM), lambda i, j: (0, 0))
    return pl.pallas_call(
        _skip_ref(body, 3 + has_w, len(more)), name=name, grid=(t // tr, inner // wc),
        in_specs=[blk, blk, pl.BlockSpec((tr, wc), lambda i, j: (i, j + c0))] + ([vec] if has_w else []) + more_specs,
        out_specs=[blk, pl.BlockSpec((tr, wc), lambda i, j: (i, j + out0 // wc))] + ([vec] if has_w else []),
        out_shape=[jax.ShapeDtypeStruct((t, inner), do_dtype), jax.ShapeDtypeStruct((t, width), BF16)]
        + ([jax.ShapeDtypeStruct((1, HEAD_DIM), F32)] if has_w else []),
        input_output_aliases=aliases, compiler_params=_cparams("arbitrary", "arbitrary"),
    )(*([dg, o, proj] + ([norm_w] if has_w else []) + more))


N_REL = 2 * REL_CLIP + 1
REL_PAD = 640
WIN = 2 * Q_TILE


def _diag_onehot():
    i = lax.broadcasted_iota(jnp.int32, (REL_PAD, WIN), 0)
    j = lax.broadcasted_iota(jnp.int32, (REL_PAD, WIN), 1)
    rel = jnp.where(j < Q_TILE + CHUNK, Q_TILE - j, Q_TILE + WIN - j)
    used = (j < Q_TILE + CHUNK) | (j > WIN - CHUNK)
    idx = jnp.clip(rel, -REL_CLIP, REL_CLIP) + REL_CLIP
    return jnp.where(used & (i == idx), 1.0, 0.0).astype(F32)


def _band_mask():
    r = lax.broadcasted_iota(jnp.int32, (Q_TILE, WIN), 0) // CHUNK
    kc = lax.broadcasted_iota(jnp.int32, (Q_TILE, WIN), 1) // CHUNK - LEFT_CHUNKS
    return (kc <= r) & (kc >= r - LEFT_CHUNKS)


def _bias_tiles(rel_bias_pad, name):
    nh = rel_bias_pad.shape[0]

    def body(rb_ref, o_ref):
        dvec = _nn(rb_ref[...], _diag_onehot(), HIGHEST)[0:1, :]
        tile = pltpu.roll(jnp.broadcast_to(dvec, (Q_TILE, WIN)), 0, 1, stride=1, stride_axis=0)
        o_ref[...] = jnp.where(_band_mask(), tile, NEG_BIG)

    return pl.pallas_call(
        body, name=name, grid=(nh,),
        in_specs=[pl.BlockSpec((None, 8, REL_PAD), lambda h: (h, 0, 0))],
        out_specs=pl.BlockSpec((None, Q_TILE, WIN), lambda h: (h, 0, 0)),
        out_shape=jax.ShapeDtypeStruct((nh, Q_TILE, WIN), F32),
        compiler_params=_cparams("parallel"),
    )(rel_bias_pad)


def _bias_grad(dtile, name):
    nh = dtile.shape[0]

    def body(d_ref, o_ref):
        ri = lax.broadcasted_iota(jnp.int32, (Q_TILE, Q_TILE), 0)
        ci = lax.broadcasted_iota(jnp.int32, (Q_TILE, Q_TILE), 1)
        flip = jnp.where(ri + ci == Q_TILE - 1, 1.0, 0.0).astype(F32)
        rev = _dot_exact(flip, d_ref[...], NN, True)
        rolled = pltpu.roll(rev, WIN - (Q_TILE - 1), 1, stride=1, stride_axis=0)
        diag = jnp.broadcast_to(jnp.sum(rolled, axis=0, keepdims=True), (8, WIN))
        o_ref[...] = _nt(diag, _diag_onehot(), HIGHEST)

    return pl.pallas_call(
        body, name=name, grid=(nh,),
        in_specs=[pl.BlockSpec((None, Q_TILE, WIN), lambda h: (h, 0, 0))],
        out_specs=pl.BlockSpec((None, 8, REL_PAD), lambda h: (h, 0, 0)),
        out_shape=jax.ShapeDtypeStruct((nh, 8, REL_PAD), F32),
        compiler_params=_cparams("parallel"),
    )(dtile)


GROUP = 2 * CHUNK
BAND = Q_TILE + GROUP


N_GROUPS = Q_TILE // GROUP
ATTN_HB = 2


def _head_cols(j):
    return slice(j * HEAD_DIM, (j + 1) * HEAD_DIM)


def _groups(ref, units):
    return jnp.stack([ref[GROUP * g:GROUP * (g + 1), _head_cols(j)] for j, g in units])


def _bands(r0_ref, r1_ref, units):
    return jnp.stack([jnp.concatenate([r0_ref[GROUP * g:, _head_cols(j)], r1_ref[:GROUP * (g + 1), _head_cols(j)]],
                                      axis=0) for j, g in units])


def _group_probs(q, kw, b_ref, units, first_tile):
    bias = jnp.stack([b_ref[j, GROUP * g:GROUP * (g + 1), GROUP * g:GROUP * g + BAND] for j, g in units])
    s = _dot(q, kw, BNT) * (HEAD_DIM ** -0.5) + bias
    col = jnp.stack([lax.broadcasted_iota(jnp.int32, (GROUP, BAND), 1) + GROUP * g for _, g in units])
    s = jnp.where(first_tile & (col < Q_TILE), NEG_BIG, s)
    p = jnp.exp(s - jnp.max(s, axis=-1, keepdims=True))
    return p * (1.0 / jnp.sum(p, axis=-1, keepdims=True))


def _attn_fwd(q, k, v, v_col0, bias, name):
    t, inner = q.shape
    nh, nt = inner // HEAD_DIM, t // Q_TILE
    hb = min(ATTN_HB, nh)
    wc = hb * HEAD_DIM
    vh = v_col0 // wc
    units = [(j, g) for j in range(hb) for g in range(N_GROUPS)]

    def body(q_ref, k0_ref, k1_ref, v0_ref, v1_ref, b_ref, o_ref):
        p = _group_probs(_groups(q_ref, units), _bands(k0_ref, k1_ref, units), b_ref, units, pl.program_id(1) == 0)
        o = _dot(_bf(p), _bf(_bands(v0_ref, v1_ref, units)), BNN)
        for n, (j, g) in enumerate(units):
            o_ref[GROUP * g:GROUP * (g + 1), _head_cols(j)] = o[n]

    cur = pl.BlockSpec((Q_TILE, wc), lambda h, i: (i, h))
    prev = pl.BlockSpec((Q_TILE, wc), lambda h, i: (jnp.maximum(i - 1, 0), h))
    v_cur = pl.BlockSpec((Q_TILE, wc), lambda h, i: (i, h + vh))
    v_prev = pl.BlockSpec((Q_TILE, wc), lambda h, i: (jnp.maximum(i - 1, 0), h + vh))
    return pl.pallas_call(
        body, name=name, grid=(nh // hb, nt),
        in_specs=[cur, prev, cur, v_prev, v_cur, pl.BlockSpec((hb, Q_TILE, WIN), lambda h, i: (h, 0, 0))],
        out_specs=cur, out_shape=jax.ShapeDtypeStruct((t, inner), F32),
        compiler_params=_cparams("parallel", "parallel"),
    )(q, k, k, v, v, bias)


def _attn_bwd(q, k, v, v_col0, do, bias, name, dest=None):
    t, inner = q.shape
    nh, nt = inner // HEAD_DIM, t // Q_TILE
    scale = HEAD_DIM ** -0.5
    hb = min(ATTN_HB, nh)
    wc = hb * HEAD_DIM
    units = [(j, g) for j in range(hb) for g in range(N_GROUPS)]

    def body(q_ref, k0_ref, k1_ref, v0_ref, v1_ref, do_ref, b_ref, dq_ref, dk_ref, dv_ref, db_ref,
             ck_ref, cv_ref, wk_ref, wv_ref):
        i = pl.program_id(1)

        @pl.when(i == 0)
        def _():
            ck_ref[...] = jnp.zeros(blk, F32)
            cv_ref[...] = jnp.zeros(blk, F32)
            db_ref[...] = jnp.zeros((hb, Q_TILE, WIN), F32)

        @pl.when(i < nt)
        def _():
            wk_ref[...] = jnp.zeros((WIN, wc), F32)
            wv_ref[...] = jnp.zeros((WIN, wc), F32)
            qv, dov = _groups(q_ref, units), _groups(do_ref, units)
            kw, vw = _bands(k0_ref, k1_ref, units), _bf(_bands(v0_ref, v1_ref, units))
            p = _group_probs(qv, kw, b_ref, units, i == 0)
            dp = _dot(dov, vw, BNT)
            ds = p * (dp - jnp.sum(p * dp, axis=-1, keepdims=True))
            pb, dsb = _bf(p), _bf(ds)
            dq = _dot(dsb, kw, BNN) * scale
            dkw = _dot(dsb, qv, BTN) * scale
            dvw = _dot(pb, dov, BTN)
            for n, (j, g) in enumerate(units):
                rows, cols = slice(GROUP * g, GROUP * (g + 1)), slice(GROUP * g, GROUP * g + BAND)
                db_ref[j, rows, cols] += ds[n]
                dq_ref[rows, _head_cols(j)] = dq[n]
                wk_ref[cols, _head_cols(j)] += dkw[n]
                wv_ref[cols, _head_cols(j)] += dvw[n]
            dk_ref[...] = ck_ref[...] + wk_ref[:Q_TILE, :]
            dv_ref[...] = (cv_ref[...] + wv_ref[:Q_TILE, :]).astype(BF16)
            ck_ref[...] = wk_ref[Q_TILE:, :]
            cv_ref[...] = wv_ref[Q_TILE:, :]

        @pl.when(i == nt)
        def _():
            dk_ref[...] = ck_ref[...]
            dv_ref[...] = cv_ref[...].astype(BF16)

    blk = (Q_TILE, wc)
    cur = pl.BlockSpec(blk, lambda h, i: (jnp.minimum(i, nt - 1), h))
    prev = pl.BlockSpec(blk, lambda h, i: (jnp.clip(i - 1, 0, nt - 1), h))
    lag = pl.BlockSpec(blk, lambda h, i: (jnp.maximum(i - 1, 0), h))
    vh = v_col0 // wc
    v_cur = pl.BlockSpec(blk, lambda h, i: (jnp.minimum(i, nt - 1), h + vh))
    v_prev = pl.BlockSpec(blk, lambda h, i: (jnp.clip(i - 1, 0, nt - 1), h + vh))
    tile = pl.BlockSpec((hb, Q_TILE, WIN), lambda h, i: (h, 0, 0))
    width, out0, more, more_specs, aliases = _window(dest, inner, 7, 2)
    return pl.pallas_call(
        _skip_ref(body, 7, len(more)), name=name, grid=(nh // hb, nt + 1),
        in_specs=[cur, prev, cur, v_prev, v_cur, cur, tile] + more_specs,
        out_specs=[cur, lag, pl.BlockSpec(blk, lambda h, i: (jnp.maximum(i - 1, 0), h + out0 // wc)), tile],
        out_shape=[jax.ShapeDtypeStruct((t, inner), F32)] * 2 + [jax.ShapeDtypeStruct((t, width), BF16),
                                                                 jax.ShapeDtypeStruct((nh, Q_TILE, WIN), F32)],
        scratch_shapes=[pltpu.VMEM(blk, F32), pltpu.VMEM(blk, F32),
                        pltpu.VMEM((WIN, wc), F32), pltpu.VMEM((WIN, wc), F32)],
        input_output_aliases=aliases, compiler_params=_cparams("arbitrary", "arbitrary"),
    )(q, k, k, v, v, do, bias, *more)


def _pad_rel_bias(rel_bias):
    nh = rel_bias.shape[0]
    return jnp.broadcast_to(jnp.pad(rel_bias, ((0, 0), (0, REL_PAD - N_REL)))[:, None, :], (nh, 8, REL_PAD))


def _layer_b_fwd(h1, nw, w_in_t, qw, kw, bias, w_out, target):
    t, d = h1.shape
    inner = w_out.shape[0]
    hn = _rms_fwd(h1, nw, "b_rms")
    proj = _mm(hn, w_in_t, "nt", t, 4 * inner, d, out_dtype=F32, name="b_proj")
    qn = _headnorm_fwd(proj, qw, 0, inner, "b_qnorm")
    kn = _headnorm_fwd(proj, kw, inner, inner, "b_knorm")
    o = _attn_fwd(qn, kn, proj, 2 * inner, bias, "b_attn")
    g = _gate_fwd(o, proj, 3 * inner, inner, "b_gate")
    loss_parts = _mm(g, w_out, "nn", t, d, inner, out_dtype=F32, name="b_out", res=h1, loss_target=target)
    return loss_parts, (hn, proj, qn, kn, o, g)


def _layer_b_bwd(dh2, dh2b, h1, nw, w_in_t, qw, kw, bias, w_out, saved):
    hn, proj, qn, kn, o, g = saved
    t, d = h1.shape
    inner = w_out.shape[0]
    dg = _mm(dh2b, w_out, "nt", t, inner, d, out_dtype=F32, name="b_dgate")
    dw_out = _mm(g, dh2b, "tn", inner, d, t, out_dtype=BF16, name="b_dwout")
    do, dproj = _gate_bwd(dg, o, proj, 3 * inner, inner, "b_gate_bwd", BF16, dest=(None, 3 * inner, 4 * inner))
    dq, dk, dproj, dtile = _attn_bwd(qn, kn, proj, 2 * inner, do, bias, "b_attn_bwd",
                                     dest=(dproj, 2 * inner, 4 * inner))
    dproj, dqw = _headnorm_bwd(dq, proj, qw, 0, inner, "b_qnorm_bwd", dest=(dproj, 0, 4 * inner))
    dproj, dkw = _headnorm_bwd(dk, proj, kw, inner, inner, "b_knorm_bwd", dest=(dproj, inner, 4 * inner))
    dhn = _mm(dproj, w_in_t, "nn", t, d, 4 * inner, out_dtype=F32, name="b_dhn")
    dw_in_t = _mm(dproj, hn, "tn", 4 * inner, d, t, out_dtype=BF16, name="b_dwin")
    dh1, dh1b, dnw = _rms_bwd(h1, nw, dhn, dh2, "b_rms_bwd")
    drb = _bias_grad(dtile, "b_bias_grad")[:, 0, :N_REL]
    return dh1, dh1b, dnw, dw_in_t, dqw, dkw, drb, dw_out


LANES = 128


def _softplus(x):
    return jnp.maximum(x, 0.0) + jnp.log1p(jnp.exp(-jnp.abs(x)))


def _gates_fwd(ab, alog_row, dt_row, nh, name, tr=1024):
    t = ab.shape[0]
    tr = min(tr, t)

    def body(x_ref, al_ref, dt_ref, o_ref):
        x = x_ref[...]
        lane = lax.broadcasted_iota(jnp.int32, x.shape, 1)
        g = -jnp.exp(al_ref[...]) * _softplus(x + dt_ref[...])
        o_ref[...] = jnp.where(lane < nh, g, jnp.where(lane < 2 * nh, _sigmoid(x), 0.0))

    row = pl.BlockSpec((tr, LANES), lambda i: (i, 0))
    vec = pl.BlockSpec((1, LANES), lambda i: (0, 0))
    return pl.pallas_call(
        body, name=name, grid=(t // tr,), in_specs=[row, vec, vec], out_specs=row,
        out_shape=jax.ShapeDtypeStruct((t, LANES), F32), compiler_params=_cparams("parallel"),
    )(ab, alog_row, dt_row)


def _gates_bwd(ab, alog_row, dt_row, dgates, nh, name, tr=1024):
    t = ab.shape[0]
    tr = min(tr, t)
    npart = dgates.shape[0]

    def body(x_ref, al_ref, dt_ref, dg_ref, dx_ref, s_ref):
        x = x_ref[...]
        lane = lax.broadcasted_iota(jnp.int32, x.shape, 1)
        dgt = dg_ref[0]
        for p in range(1, npart):
            dgt = dgt + dg_ref[p]
        ea = jnp.exp(al_ref[...])
        xa = x + dt_ref[...]
        da = jnp.where(lane < nh, dgt * (-ea) * _sigmoid(xa), 0.0)
        beta = _sigmoid(x)
        db = jnp.where((lane >= nh) & (lane < 2 * nh), dgt * beta * (1.0 - beta), 0.0)
        dx_ref[...] = (da + db).astype(BF16)
        dal = jnp.sum(jnp.where(lane < nh, dgt * (-ea) * _softplus(xa), 0.0), axis=0, keepdims=True)
        ddt = jnp.sum(da, axis=0, keepdims=True)
        r8 = lax.broadcasted_iota(jnp.int32, (8, LANES), 0)
        part = jnp.where(r8 == 0, dal, jnp.where(r8 == 1, ddt, 0.0))

        @pl.when(pl.program_id(0) == 0)
        def _():
            s_ref[...] = part

        @pl.when(pl.program_id(0) > 0)
        def _():
            s_ref[...] += part

    row = pl.BlockSpec((tr, LANES), lambda i: (i, 0))
    vec = pl.BlockSpec((1, LANES), lambda i: (0, 0))
    return pl.pallas_call(
        body, name=name, grid=(t // tr,),
        in_specs=[row, vec, vec, pl.BlockSpec((npart, tr, LANES), lambda i: (0, i, 0))],
        out_specs=[row, pl.BlockSpec((8, LANES), lambda i: (0, 0))],
        out_shape=[jax.ShapeDtypeStruct((t, LANES), BF16), jax.ShapeDtypeStruct((8, LANES), F32)],
        compiler_params=_cparams("arbitrary"),
    )(ab, alog_row, dt_row, dgates)


HALO = 8


def _delayed(ext, rows):
    return [ext[HALO:HALO + rows]] + [pltpu.roll(ext, s, 0)[HALO:HALO + rows] for s in range(1, CONV_K)]


def _conv_taps(delayed, w):
    acc = delayed[0] * w[CONV_K - 1:CONV_K]
    for s in range(1, CONV_K):
        acc = acc + delayed[s] * w[CONV_K - 1 - s:CONV_K - s]
    return acc


def _conv_fwd(proj, conv_w, col0, inner, mode, name, tt=CONV_ROWS, hb=CONV_HEADS):
    t = proj.shape[0]
    tt = min(tt, t)
    hb = min(hb, inner // HEAD_DIM)
    wc = hb * HEAD_DIM
    c0 = col0 // wc
    hpb = tt // HALO

    def body(x_ref, halo_ref, w_ref, o_ref):
        halo = jnp.where(pl.program_id(1) == 0, 0.0, halo_ref[...])
        s = _silu(_conv_taps(_delayed(jnp.concatenate([halo, x_ref[...]], axis=0), tt), w_ref[...]))
        if mode == "v":
            o_ref[...] = s
        else:
            mul = HEAD_DIM ** -0.5 if mode == "q" else 1.0
            o_ref[...] = jnp.concatenate(
                [sh * (lax.rsqrt(jnp.sum(sh * sh, axis=-1, keepdims=True) + EPS) * mul) for sh in _heads_of(s, hb)], axis=1)

    return pl.pallas_call(
        body, name=name, grid=(inner // wc, t // tt),
        in_specs=[pl.BlockSpec((tt, wc), lambda j, i: (i, j + c0)),
                  pl.BlockSpec((HALO, wc), lambda j, i: (jnp.maximum(i * hpb - 1, 0), j + c0)),
                  pl.BlockSpec((CONV_K, wc), lambda j, i: (0, j + c0))],
        out_specs=pl.BlockSpec((tt, wc), lambda j, i: (i, j)),
        out_shape=jax.ShapeDtypeStruct((t, inner), F32),
        compiler_params=_cparams("parallel", "parallel"),
    )(proj, proj, conv_w)


def _conv_bwd(dy, proj, conv_w, col0, inner, mode, name, tt=CONV_ROWS, hb=CONV_HEADS, dest=None):
    t = proj.shape[0]
    tt = min(tt, t)
    nt = t // tt
    hb = min(hb, inner // HEAD_DIM)
    wc = hb * HEAD_DIM
    c0 = col0 // wc
    hpb = tt // HALO
    rows = tt + HALO

    def body(dy_ref, dyn_ref, x_ref, xp_ref, xn_ref, w_ref, dx_ref, dw_ref):
        i = pl.program_id(1)
        w = w_ref[...]
        xprev = jnp.where(i == 0, 0.0, xp_ref[...])
        delayed = _delayed(jnp.concatenate([xprev, x_ref[...], xn_ref[...]], axis=0), rows)
        c = _conv_taps(delayed, w)
        dyv = jnp.concatenate([dy_ref[...], jnp.where(i == nt - 1, 0.0, dyn_ref[...])], axis=0)
        sg = _sigmoid(c)
        s = c * sg
        if mode == "v":
            ds = dyv
        else:
            mul = HEAD_DIM ** -0.5 if mode == "q" else 1.0
            parts = []
            for dyh, sh in zip(_heads_of(dyv, hb), _heads_of(s, hb)):
                r = lax.rsqrt(jnp.sum(sh * sh, axis=-1, keepdims=True) + EPS)
                parts.append(mul * (r * dyh - sh * (r * r * r) * jnp.sum(dyh * sh, axis=-1, keepdims=True)))
            ds = jnp.concatenate(parts, axis=1)
        dc = ds * (sg * (1.0 + c * (1.0 - sg)))
        dx = dc[:tt] * w[CONV_K - 1:CONV_K]
        for sft in range(1, CONV_K):
            dx = dx + pltpu.roll(dc, rows - sft, 0)[:tt] * w[CONV_K - 1 - sft:CONV_K - sft]
        dx_ref[...] = dx.astype(BF16)
        r8 = lax.broadcasted_iota(jnp.int32, (8, wc), 0)
        part = jnp.zeros((8, wc), F32)
        for sft in range(CONV_K):
            part = part + jnp.where(r8 == CONV_K - 1 - sft,
                                    jnp.sum(dc[:tt] * delayed[sft][:tt], axis=0, keepdims=True), 0.0)

        @pl.when(i == 0)
        def _():
            dw_ref[...] = part

        @pl.when(i > 0)
        def _():
            dw_ref[...] += part

    cur = lambda off: pl.BlockSpec((tt, wc), lambda j, i: (i, j + off))
    nxt = lambda off: pl.BlockSpec((HALO, wc), lambda j, i: (jnp.minimum((i + 1) * hpb, t // HALO - 1), j + off))
    width, out0, more, more_specs, aliases = _window(dest, inner, 6, 0)
    return pl.pallas_call(
        _skip_ref(body, 6, len(more)), name=name, grid=(inner // wc, nt),
        in_specs=[cur(0), nxt(0), cur(c0),
                  pl.BlockSpec((HALO, wc), lambda j, i: (jnp.maximum(i * hpb - 1, 0), j + c0)), nxt(c0),
                  pl.BlockSpec((CONV_K, wc), lambda j, i: (0, j + c0))] + more_specs,
        out_specs=[pl.BlockSpec((tt, wc), lambda j, i: (i, j + out0 // wc)),
                   pl.BlockSpec((8, wc), lambda j, i: (0, j))],
        out_shape=[jax.ShapeDtypeStruct((t, width), BF16), jax.ShapeDtypeStruct((8, inner), F32)],
        input_output_aliases=aliases, compiler_params=_cparams("parallel", "arbitrary"),
    )(dy, dy, proj, proj, proj, conv_w, *more)


GDN_HB = 4
GDN_NB = 8
SCAN_HB = 16
SCAN_NB = 4


def _iota2(n, m):
    return lax.broadcasted_iota(jnp.int32, (n, m), 0), lax.broadcasted_iota(jnp.int32, (n, m), 1)


def _head_select(first_head, hb, lane0):
    r, lane = _iota2(8, LANES)
    return jnp.where((r < hb) & (lane == lane0 + first_head + r), 1.0, 0.0).astype(F32)


def _chunk_gates(gt, selg, selb):
    i, j = _iota2(CHUNK, CHUNK)
    gc_all = _dot_exact(jnp.where(j <= i, 1.0, 0.0), gt, NN, True)
    return (_dot_exact(gc_all, selg, NT, False), _dot_exact(selg, gc_all, NT, True),
            _dot_exact(gt, selb, NT, False))


def _decay_terms(gcol, grow):
    i, j = _iota2(CHUNK, CHUNK)
    glast = gcol[:, CHUNK - 1:CHUNK, :]
    decay = jnp.exp(jnp.where(j <= i, gcol - grow, NEG_BIG))
    return jnp.exp(gcol), jnp.exp(glast - gcol), jnp.exp(glast), decay


def _unit_lower_inverse(a):
    i, j = _iota2(CHUNK, CHUNK)
    same16 = (i // 16) == (j // 16)
    same32 = (i // 32) == (j // 32)
    m = jnp.where(same16, -a, 0.0)
    x = jnp.where(i == j, 1.0, 0.0) + m
    for _ in range(3):
        m = _dot3(m, m, BNN)
        x = x + _dot3(x, m, BNN)
    for off in (jnp.where(same32 & jnp.logical_not(same16), a, 0.0), jnp.where(same32, 0.0, a)):
        x = x - _dot3(_dot3(x, off, BNN), x, BNN)
    return x


def _unit_inputs(refs, g_ref, selg, selb, hb, nb):
    units = [(c, h) for c in range(nb) for h in range(hb)]
    rs = lambda c: slice(c * CHUNK, (c + 1) * CHUNK)
    cs = lambda h: slice(h * HEAD_DIM, (h + 1) * HEAD_DIM)
    gates = [_chunk_gates(g_ref[rs(c), :], selg, selb) for c in range(nb)]
    stacked = [jnp.stack([r[rs(c), cs(h)] for c, h in units]) for r in refs]
    gcol = jnp.stack([gates[c][0][:, h:h + 1] for c, h in units])
    grow = jnp.stack([gates[c][1][h:h + 1, :] for c, h in units])
    bcol = jnp.stack([gates[c][2][:, h:h + 1] for c, h in units])
    return units, rs, cs, stacked, gcol, grow, bcol


def _gdn_specs(nh, inner, t, heads=GDN_HB, chunks=GDN_NB):
    hb, nb = min(heads, nh), chunks
    rows = nb * CHUNK
    wide = pl.BlockSpec((rows, hb * HEAD_DIM), lambda g, n: (n, g))
    sq = pl.BlockSpec((hb, rows, CHUNK), lambda g, n: (g, n, 0))
    gts = pl.BlockSpec((rows, LANES), lambda g, n: (n, 0))
    glb = pl.BlockSpec((nb * 8, hb * HEAD_DIM), lambda g, n: (n, g))
    return hb, nb, rows, wide, sq, gts, glb


def _gdn_intra_fwd(q, k, v, gates, nh, name, comm=None):
    t, inner = q.shape
    hb, nb, rows, wide, sq, gts, glb = _gdn_specs(nh, inner, t)

    def body(q_ref, k_ref, v_ref, g_ref, qe_ref, kel_ref, wb_ref, w_ref, u_ref, qk_ref, tm_ref, gl_ref):
        first = pl.program_id(0) * hb
        selg, selb = _head_select(first, hb, 0), _head_select(first, hb, nh)
        i, j = _iota2(CHUNK, CHUNK)
        units, rs, cs, (qv, kv, vv), gcol, grow, bcol = _unit_inputs(
            (q_ref, k_ref, v_ref), g_ref, selg, selb, hb, nb)
        e, el, gl, decay = _decay_terms(gcol, grow)
        kb = kv * bcol
        qbf, kbf = _bf(qv), _bf(kv)
        a = jnp.where(j < i, _dot(_bf(kb), kbf, BNT) * decay, 0.0)
        tm = _unit_lower_inverse(a)
        uw = _dot3(tm, jnp.concatenate([vv * bcol, kb * e], axis=2), BNN)
        qk = _bf(_dot(qbf, kbf, BNT) * decay)
        qe, kel = _bf(qv * e), _bf(kv * el)
        for n, (c, h) in enumerate(units):
            w = uw[n, :, HEAD_DIM:]
            qe_ref[rs(c), cs(h)] = qe[n]
            kel_ref[rs(c), cs(h)] = kel[n]
            wb_ref[rs(c), cs(h)] = _bf(w)
            w_ref[rs(c), cs(h)] = w
            u_ref[rs(c), cs(h)] = uw[n, :, :HEAD_DIM]
            qk_ref[h, rs(c), :] = qk[n]
            tm_ref[h, rs(c), :] = tm[n]
            gl_ref[c * 8:(c + 1) * 8, cs(h)] = jnp.broadcast_to(gl[n], (8, HEAD_DIM))

    big = lambda dt: jax.ShapeDtypeStruct((t, inner), dt)
    return _grid_call(
        body, name=name, grid=(nh // hb, t // rows),
        in_specs=[wide, wide, wide, gts],
        out_specs=[wide] * 5 + [sq, sq, glb],
        out_shape=[big(BF16), big(BF16), big(BF16), big(F32), big(F32),
                   jax.ShapeDtypeStruct((nh, t, CHUNK), BF16), jax.ShapeDtypeStruct((nh, t, CHUNK), F32),
                   jax.ShapeDtypeStruct((t // CHUNK * 8, inner), F32)],
        args=(q, k, v, gates), semantics=("parallel", "parallel"), comm=comm)


def _gdn_scan_fwd(qe, kel, wb, u, qk, glb, nh, name):
    t, inner = u.shape
    hb, nb, rows, wide, sq, _, glb_spec = _gdn_specs(nh, inner, t, SCAN_HB, SCAN_NB)

    def body(qe_ref, kel_ref, wb_ref, u_ref, qk_ref, gl_ref, o_ref, vn_ref, sall_ref, s_ref):
        @pl.when(pl.program_id(1) == 0)
        def _():
            s_ref[...] = jnp.zeros(s_ref.shape, F32)

        cs = lambda h: slice(h * HEAD_DIM, (h + 1) * HEAD_DIM)
        for c in range(nb):
            rs = slice(c * CHUNK, (c + 1) * CHUNK)
            heads = lambda ref: jnp.stack([ref[rs, cs(h)] for h in range(hb)])
            s = s_ref[...]
            sall_ref[c] = s
            sb = _bf(s)
            vn = heads(u_ref) - _dot(heads(wb_ref), sb, BNN)
            vnb = _bf(vn)
            o = _dot(heads(qe_ref), sb, BNN) + _dot(qk_ref[:, rs, :], vnb, BNN)
            gl = jnp.stack([gl_ref[c * 8:c * 8 + 1, cs(h)] for h in range(hb)])
            s_ref[...] = s * gl + _dot(heads(kel_ref), vnb, BTN)
            for h in range(hb):
                vn_ref[rs, cs(h)] = vn[h]
                o_ref[rs, cs(h)] = o[h]

    return pl.pallas_call(
        body, name=name, grid=(nh // hb, t // rows),
        in_specs=[wide, wide, wide, wide, sq, glb_spec],
        out_specs=[wide, wide, pl.BlockSpec((nb, hb, HEAD_DIM, HEAD_DIM), lambda g, n: (n, g, 0, 0))],
        out_shape=[jax.ShapeDtypeStruct((t, inner), F32), jax.ShapeDtypeStruct((t, inner), F32),
                   jax.ShapeDtypeStruct((t // CHUNK, nh, HEAD_DIM, HEAD_DIM), F32)],
        scratch_shapes=[pltpu.VMEM((hb, HEAD_DIM, HEAD_DIM), F32)],
        compiler_params=_cparams("parallel", "arbitrary"),
    )(qe, kel, wb, u, qk, glb)


def _gdn_scan_bwd(do, qe, kel, wb, vn, qk, glb, sall, nh, name, comm=None):
    t, inner = do.shape
    hb, nb, rows, _, _, _, _ = _gdn_specs(nh, inner, t, SCAN_HB, SCAN_NB)
    last = t // rows - 1
    wide = pl.BlockSpec((rows, hb * HEAD_DIM), lambda g, n: (last - n, g))
    sq = pl.BlockSpec((hb, rows, CHUNK), lambda g, n: (g, last - n, 0))
    glb_spec = pl.BlockSpec((nb * 8, hb * HEAD_DIM), lambda g, n: (last - n, g))

    def body(do_ref, qe_ref, kel_ref, wb_ref, vn_ref, qk_ref, gl_ref, sall_ref,
             dvn_ref, dw_ref, dqe_ref, dkel_ref, dqk_ref, dgl_ref, ds_ref):
        @pl.when(pl.program_id(1) == 0)
        def _():
            ds_ref[...] = jnp.zeros(ds_ref.shape, F32)

        cs = lambda h: slice(h * HEAD_DIM, (h + 1) * HEAD_DIM)
        for c in reversed(range(nb)):
            rs = slice(c * CHUNK, (c + 1) * CHUNK)
            heads = lambda ref: jnp.stack([ref[rs, cs(h)] for h in range(hb)])
            ds, s = ds_ref[...], sall_ref[c]
            dsb, sb = _bf(ds), _bf(s)
            dob, vnb = _bf(heads(do_ref)), _bf(heads(vn_ref))
            dvn = _dot(qk_ref[:, rs, :], dob, BTN) + _dot(heads(kel_ref), dsb, BNN)
            dvnb = _bf(dvn)
            dw = -_dot(dvnb, sb, BNT)
            dqe = _dot(dob, sb, BNT)
            dkel = _dot(vnb, dsb, BNT)
            dqk_ref[:, rs, :] = _dot(dob, vnb, BNT)
            dgl = jnp.sum(jnp.sum(ds * s, axis=2, keepdims=True), axis=1, keepdims=True)
            gl = jnp.stack([gl_ref[c * 8:c * 8 + 1, cs(h)] for h in range(hb)])
            ds_ref[...] = ds * gl + _dot(heads(qe_ref), dob, BTN) - _dot(heads(wb_ref), dvnb, BTN)
            for h in range(hb):
                dvn_ref[rs, cs(h)] = dvn[h]
                dw_ref[rs, cs(h)] = dw[h]
                dqe_ref[rs, cs(h)] = dqe[h]
                dkel_ref[rs, cs(h)] = dkel[h]
                dgl_ref[c * 8:(c + 1) * 8, cs(h)] = jnp.broadcast_to(dgl[h], (8, HEAD_DIM))

    big = jax.ShapeDtypeStruct((t, inner), F32)
    return _grid_call(
        body, name=name, grid=(nh // hb, t // rows),
        in_specs=[wide, wide, wide, wide, wide, sq, glb_spec,
                  pl.BlockSpec((nb, hb, HEAD_DIM, HEAD_DIM), lambda g, n: (last - n, g, 0, 0))],
        out_specs=[wide] * 4 + [sq, glb_spec],
        out_shape=[big] * 4 + [jax.ShapeDtypeStruct((nh, t, CHUNK), F32),
                               jax.ShapeDtypeStruct((t // CHUNK * 8, inner), F32)],
        scratch_shapes=[pltpu.VMEM((hb, HEAD_DIM, HEAD_DIM), F32)],
        args=(do, qe, kel, wb, vn, qk, glb, sall), semantics=("parallel", "arbitrary"), comm=comm)


def _gdn_intra_bwd(q, k, v, gates, tm, w, u, dvn, dw, dqe, dkel, dqk, dglb, nh, name, comm=None):
    t, inner = q.shape
    hb, nb, rows, wide, sq, gts, glb = _gdn_specs(nh, inner, t)

    def body(q_ref, k_ref, v_ref, g_ref, tm_ref, w_ref, u_ref, dvn_ref, dw_ref, dqe_ref, dkel_ref, dqk_ref,
             dgl_ref, dq_ref, dk_ref, dv_ref, dg_ref):
        first = pl.program_id(0) * hb
        selg, selb = _head_select(first, hb, 0), _head_select(first, hb, nh)
        i, j = _iota2(CHUNK, CHUNK)
        lane8 = lax.broadcasted_iota(jnp.int32, (CHUNK, 8), 1)
        row = lax.broadcasted_iota(jnp.int32, (CHUNK, 1), 0)
        lower = jnp.where(j <= i, 1.0, 0.0).astype(F32)
        rsum = lambda x: jnp.sum(x, axis=-1, keepdims=True)
        units, rs, cs, (qv, kv, vv, wv, uv, dvn, dw, dqe, dkel), gcol, grow, bcol = _unit_inputs(
            (q_ref, k_ref, v_ref, w_ref, u_ref, dvn_ref, dw_ref, dqe_ref, dkel_ref), g_ref, selg, selb, hb, nb)
        nu = len(units)
        tmv = jnp.stack([tm_ref[h, rs(c), :] for c, h in units])
        dqk = jnp.where(j <= i, jnp.stack([dqk_ref[h, rs(c), :] for c, h in units]), 0.0)
        dgl = jnp.stack([dgl_ref[c * 8:c * 8 + 1, h * HEAD_DIM:h * HEAD_DIM + 1] for c, h in units])
        e, el, gl, decay = _decay_terms(gcol, grow)
        kb = kv * bcol
        qb, kbf, kbb = _bf(qv), _bf(kv), _bf(kb)
        dqkr = _bf(dqk * decay)
        dq = dqe * e + _dot(dqkr, kbf, BNN)
        dk = dkel * el + _dot(dqkr, qb, BTN)
        de = rsum(dqe * qv)
        del_ = rsum(dkel * kv)
        mq = dqk * _dot(qb, kbf, BNT) * decay
        dsol = _dot3(tmv, jnp.concatenate([dvn, dw], axis=2), BTN)
        dvb, dkbe = dsol[:, :, :HEAD_DIM], dsol[:, :, HEAD_DIM:]
        da = -jnp.where(j < i, _dot3(dsol, jnp.concatenate([uv, wv], axis=2), BNT), 0.0)
        dkk = _bf(da * decay)
        ma = da * _dot(kbb, kbf, BNT) * decay
        dkb = dkbe * e + _dot(dkk, kbf, BNN)
        de = de + rsum(dkbe * kb)
        dk = dk + _dot(dkk, kbb, BTN) + dkb * bcol
        dv = dvb * bcol
        dbeta = rsum(dkb * kv) + rsum(dvb * vv)
        m = mq + ma
        ones = jnp.ones((nu, CHUNK, LANES), F32)
        dgc = rsum(m) - _dot_exact(m, ones, BTN, False)[:, :, 0:1] + de * e - del_ * el
        tail = jnp.sum(del_ * el, axis=1, keepdims=True) + dgl * gl
        dgc = dgc + jnp.where(row == CHUNK - 1, tail, 0.0)
        for n, (c, h) in enumerate(units):
            dq_ref[rs(c), cs(h)] = dq[n]
            dk_ref[rs(c), cs(h)] = dk[n]
            dv_ref[rs(c), cs(h)] = dv[n]
        for c in range(nb):
            dgc_cols = jnp.zeros((CHUNK, 8), F32)
            dbeta_cols = jnp.zeros((CHUNK, 8), F32)
            for h in range(hb):
                dgc_cols = jnp.where(lane8 == h, dgc[c * hb + h], dgc_cols)
                dbeta_cols = jnp.where(lane8 == h, dbeta[c * hb + h], dbeta_cols)
            dg_cols = _dot_exact(lower, dgc_cols, TN, True)
            dg_ref[rs(c), :] = _dot_exact(dg_cols, selg, NN, False) + _dot_exact(dbeta_cols, selb, NN, False)

    big = jax.ShapeDtypeStruct((t, inner), F32)
    return _grid_call(
        body, name=name, grid=(nh // hb, t // rows),
        in_specs=[wide, wide, wide, gts, sq, wide, wide, wide, wide, wide, wide, sq, glb],
        out_specs=[wide, wide, wide, pl.BlockSpec((None, rows, LANES), lambda g, n: (g, n, 0))],
        out_shape=[big, big, big, jax.ShapeDtypeStruct((nh // hb, t, LANES), F32)],
        args=(q, k, v, gates, tm, w, u, dvn, dw, dqe, dkel, dqk, dglb), semantics=("parallel", "parallel"),
        comm=comm)


def _layer_a_fwd(x, hn, w_in_t, w_ab_t, conv_w, alog_row, dt_row, onw, nh, comm, w_out_of):
    t, d = x.shape
    inner = nh * HEAD_DIM
    proj = _mm(hn, w_in_t, "nt", t, 4 * inner, d, out_dtype=F32, name="a_proj")
    ab = _mm(hn, w_ab_t, "nt", t, LANES, d, out_dtype=F32, name="a_proj_ab")
    gates = _gates_fwd(ab, alog_row, dt_row, nh, "a_gates")
    q = _conv_fwd(proj, conv_w, 0, inner, "q", "a_conv_q")
    k = _conv_fwd(proj, conv_w, inner, inner, "k", "a_conv_k")
    v = _conv_fwd(proj, conv_w, 2 * inner, inner, "v", "a_conv_v")
    qe, kel, wb, w, u, qk, tm, glb, *carried = _gdn_intra_fwd(q, k, v, gates, nh, "a_intra", comm)
    o, vn, sall = _gdn_scan_fwd(qe, kel, wb, u, qk, glb, nh, "a_scan")
    g = _gate_fwd(o, proj, 3 * inner, inner, "a_gate", norm_w=onw)
    w_out = w_out_of(carried)
    h1 = _mm(g, w_out, "nn", t, d, inner, out_dtype=F32, name="a_out", res=x)
    return h1, (hn, proj, ab, gates, q, k, v, qe, kel, wb, w, u, qk, tm, glb, o, vn, sall, g), w_out, carried


def _layer_a_bwd(dh1, dh1b, x, nw, w_in_t, w_ab_t, conv_w, alog_row, dt_row, onw, w_out, nh, saved, comms_of,
                 own_comm):
    hn, proj, ab, gates, q, k, v, qe, kel, wb, w, u, qk, tm, glb, o, vn, sall, g = saved
    t, d = x.shape
    inner = w_out.shape[0]
    dg = _mm(dh1b, w_out, "nt", t, inner, d, out_dtype=F32, name="a_dgate")
    dw_out = _mm(g, dh1b, "tn", inner, d, t, out_dtype=BF16, name="a_dwout")
    comm_scan, comm_intra = comms_of(dw_out)
    do, dproj, donw = _gate_bwd(dg, o, proj, 3 * inner, inner, "a_gate_bwd", F32, norm_w=onw,
                                dest=(None, 3 * inner, 4 * inner))
    dvn, dw, dqe, dkel, dqk, dglb, *carried_scan = _gdn_scan_bwd(do, qe, kel, wb, vn, qk, glb, sall, nh,
                                                                 "a_scan_bwd", comm_scan)
    dq, dk, dv, dgates, *carried = _gdn_intra_bwd(q, k, v, gates, tm, w, u, dvn, dw, dqe, dkel, dqk, dglb, nh,
                                                  "a_intra_bwd", comm_intra)
    carried = carried_scan + carried
    dproj, dcq = _conv_bwd(dq, proj, conv_w, 0, inner, "q", "a_conv_q_bwd", dest=(dproj, 0, 4 * inner))
    dproj, dck = _conv_bwd(dk, proj, conv_w, inner, inner, "k", "a_conv_k_bwd", dest=(dproj, inner, 4 * inner))
    dproj, dcv = _conv_bwd(dv, proj, conv_w, 2 * inner, inner, "v", "a_conv_v_bwd",
                           dest=(dproj, 2 * inner, 4 * inner))
    dab, dsmall = _gates_bwd(ab, alog_row, dt_row, dgates, nh, "a_gates_bwd")
    dw_in_t = _mm(dproj, hn, "tn", 4 * inner, d, t, out_dtype=BF16, name="a_dwin")
    dw_ab_t = _mm(dab, hn, "tn", LANES, d, t, out_dtype=BF16, name="a_dwin_ab")
    dconv = jnp.concatenate([dcq[:CONV_K], dck[:CONV_K], dcv[:CONV_K]], axis=1)
    dhn = _mm(dab, w_ab_t, "nn", t, d, LANES, out_dtype=F32, name="a_dhn_ab")
    own = own_comm(dw_in_t, dw_ab_t, dconv)
    dhn = _mm(dproj, w_in_t, "nn", t, d, 4 * inner, out_dtype=F32, name="a_dhn", res=dhn, comm=own)
    dhn, carried_own = dhn if own is not None else (dhn, [])
    dx, _, dnw = _rms_bwd(x, nw, dhn, dh1, "a_rms_bwd")
    return dx, dnw, dsmall, donw, carried, carried_own


def _rows_of(a, rows):
    flat = a.reshape(-1)
    return jnp.pad(flat, (0, rows * LANES - flat.shape[0])).reshape(rows, LANES)


def _to_slabs(g, axis):
    shape = g.shape[:axis] + (N_DEV, g.shape[axis] // N_DEV) + g.shape[axis + 1:]
    return jnp.moveaxis(g.reshape(shape), axis, 0)


def _from_slabs(s, axis):
    m = jnp.moveaxis(s, 0, axis)
    return m.reshape(m.shape[:axis] + (m.shape[axis] * m.shape[axis + 1],) + m.shape[axis + 2:])


def kernel(x, norm_w, a_w_in, a_conv_w, a_a_log, a_dt_bias, a_out_norm_w, a_w_out, b_w_in, b_q_norm_w, b_k_norm_w, b_rel_bias, b_w_out, loss_target, m_norm_w, m_a_w_in, m_a_conv_w, m_a_a_log, m_a_dt_bias, m_a_out_norm_w, m_a_w_out, m_b_w_in, m_b_q_norm_w, m_b_k_norm_w, m_b_rel_bias, m_b_w_out, v_norm_w, v_a_w_in, v_a_conv_w, v_a_a_log, v_a_dt_bias, v_a_out_norm_w, v_a_w_out, v_b_w_in, v_b_q_norm_w, v_b_k_norm_w, v_b_rel_bias, v_b_w_out):
    xs, target = x[0], loss_target[0]
    nh = a_a_log.shape[-1]
    inner = N_DEV * a_w_out.shape[1]

    d = xs.shape[1]
    nw0, nw1 = norm_w[0:1], norm_w[1:2]
    hn0, (ga_in, g_conv) = _rms_fwd(
        xs, nw0, "a_rms", comm=_RoutedGather([a_w_in[0].T.astype(BF16), a_conv_w[0]]))
    wa_in_t = ga_in.reshape(-1, d)
    wa_ab_t = jnp.pad(wa_in_t[4 * inner:], ((0, LANES - 2 * nh), (0, 0)))
    conv_w = _from_slabs(g_conv, 1)
    alog_row = jnp.pad(a_a_log, ((0, 0), (0, LANES - nh)))
    dt_row = jnp.pad(a_dt_bias, ((0, 0), (0, LANES - nh)))

    h1, saved_a, wa_out, (gb_in, gb_out, _) = _layer_a_fwd(
        xs, hn0, wa_in_t, wa_ab_t, conv_w, alog_row, dt_row, a_out_norm_w, nh,
        _Comm("gather", [b_w_in[0].T.astype(BF16), b_w_out[0].astype(BF16), a_w_out[0].astype(BF16)]),
        lambda gathered: _from_slabs(gathered[2], 0))
    wb_in_t = gb_in.reshape(-1, d)
    wb_out = _from_slabs(gb_out, 0)
    bias = _bias_tiles(_pad_rel_bias(b_rel_bias[0]), "b_bias_tiles")
    (dh2, dh2b, loss_row), saved_b = _layer_b_fwd(h1, nw1, wb_in_t, b_q_norm_w, b_k_norm_w, bias, wb_out, target)

    dh1, dh1b, dnw1, dwb_in_t, dqw, dkw, drb, dwb_out = _layer_b_bwd(
        dh2, dh2b, h1, nw1, wb_in_t, b_q_norm_w, b_k_norm_w, bias, wb_out, saved_b)

    def exchange_early(dwa_out):
        return (_Comm("exchange", [_to_slabs(dwb_out, 0).astype(BF16), _to_slabs(dwa_out, 0).astype(BF16)]),
                _Comm("exchange", [dwb_in_t.reshape(N_DEV, -1, d).astype(BF16)]))

    def exchange_last(dwa_in_t, dwa_ab_t, dconv):
        full = jnp.concatenate([dwa_in_t, dwa_ab_t[:2 * nh]], axis=0)
        return _Comm("exchange", [full.reshape(N_DEV, -1, d).astype(BF16), _to_slabs(dconv, 1)])

    dx, dnw0, dsmall, donw, (pb_out, pa_out, pb_in), (pa_in, p_conv) = _layer_a_bwd(
        dh1, dh1b, xs, nw0, wa_in_t, wa_ab_t, conv_w, alog_row, dt_row, a_out_norm_w, wa_out, nh, saved_a,
        exchange_early, exchange_last)
    big = {}
    for name, p, w, m, v in (("a_w_in", pa_in, a_w_in, m_a_w_in, v_a_w_in),
                             ("a_w_out", pa_out, a_w_out, m_a_w_out, v_a_w_out),
                             ("b_w_in", pb_in, b_w_in, m_b_w_in, v_b_w_in),
                             ("b_w_out", pb_out, b_w_out, m_b_w_out, v_b_w_out),
                             ("a_conv_w", p_conv, a_conv_w, m_a_conv_w, v_a_conv_w)):
        big[name] = [o[None] for o in _adamw(p, w[0], m[0], v[0], "adamw_" + name,
                                             transposed=name in ("a_w_in", "b_w_in"))]

    small = (("norm_w", norm_w, m_norm_w, v_norm_w, jnp.concatenate([dnw0, dnw1], axis=0)),
             ("a_a_log", a_a_log, m_a_a_log, v_a_a_log, dsmall[0:1, :nh]),
             ("a_dt_bias", a_dt_bias, m_a_dt_bias, v_a_dt_bias, dsmall[1:2, :nh]),
             ("a_out_norm_w", a_out_norm_w, m_a_out_norm_w, v_a_out_norm_w, donw),
             ("b_q_norm_w", b_q_norm_w, m_b_q_norm_w, v_b_q_norm_w, dqw),
             ("b_k_norm_w", b_k_norm_w, m_b_k_norm_w, v_b_k_norm_w, dkw),
             ("b_rel_bias", b_rel_bias, m_b_rel_bias, v_b_rel_bias, drb))
    rows = [8 * (-(-w.size // (8 * LANES))) for _, w, _, _, _ in small]
    pack = lambda arrs: jnp.concatenate([_rows_of(a, r) for a, r in zip(arrs, rows)] + [jnp.zeros((8, LANES), F32)], axis=0)
    g_pack = jnp.concatenate([_rows_of(g, r) for (_, _, _, _, g), r in zip(small, rows)]
                             + [jnp.broadcast_to(loss_row, (8, LANES))], axis=0)
    (g_all,) = _comm_call(_Comm("gather", [g_pack]), "gather_small_grads")
    outs_small = _adamw(g_all, pack([s[1] for s in small]), pack([s[2] for s in small]),
                        pack([s[3] for s in small]), "adamw_small")
    start = 0
    for (name, w, _, _, _), r in zip(small, rows):
        big[name] = [o[start:start + r].reshape(-1)[:w.size].reshape(w.shape) for o in outs_small]
        start += r
    loss = outs_small[0][start, 0]

    order = ("norm_w", "a_w_in", "a_conv_w", "a_a_log", "a_dt_bias", "a_out_norm_w", "a_w_out", "b_w_in",
             "b_q_norm_w", "b_k_norm_w", "b_rel_bias", "b_w_out")
    return (loss, dx[None]) + tuple(big[n][i] for i in range(4) for n in order)
```

```python
import functools

import jax
import jax.numpy as jnp
from jax import lax
from jax.experimental import pallas as pl
from jax.experimental.pallas import tpu as pltpu

F32 = jnp.float32
BF16 = jnp.bfloat16
MESH_IDS = pl.DeviceIdType.MESH
N_DEV = 8
CHUNK = 64
HEAD_DIM = 128
EPS = 1e-6
CONV_K = 4
LEFT_CHUNKS = 8
REL_CLIP = 256
Q_TILE = LEFT_CHUNKS * CHUNK
ADAM_LR = 0.001
ADAM_B1 = 0.9
ADAM_B2 = 0.999
ADAM_EPS = 1e-08
ADAM_WD = 0.01
ADAM_STEP = 10
NEG_BIG = -1e30
VMEM_LIMIT_BYTES = 56 * 1024 * 1024
HIGHEST = lax.Precision.HIGHEST
ANY = pl.BlockSpec(memory_space=pl.ANY)


def _cparams(*sem):
    return pltpu.CompilerParams(dimension_semantics=tuple(sem), vmem_limit_bytes=VMEM_LIMIT_BYTES)


NN, NT, TN = (((1,), (0,)), ((), ())), (((1,), (1,)), ((), ())), (((0,), (0,)), ((), ()))
BNN, BNT, BTN = (((2,), (1,)), ((0,), (0,))), (((2,), (2,)), ((0,), (0,))), (((1,), (1,)), ((0,), (0,)))


def _dot(a, b, dims, precision=None):
    return lax.dot_general(a, b, dims, preferred_element_type=F32, precision=precision)


def _nn(a, b, precision=None):
    return _dot(a, b, NN, precision)


def _nt(a, b, precision=None):
    return _dot(a, b, NT, precision)


def _tn(a, b, precision=None):
    return _dot(a, b, TN, precision)


def _bf(x):
    return x.astype(BF16)


def _split(x, pieces=2):
    out = []
    for _ in range(pieces - 1):
        hi = x.astype(BF16)
        out.append(hi)
        x = x - hi.astype(F32)
    return out + [x.astype(BF16)]


def _dot3(a, b, dims):
    (ah, al), (bh, bl) = _split(a), _split(b)
    return _dot(ah, bh, dims) + (_dot(ah, bl, dims) + _dot(al, bh, dims))


def _dot_exact(a, b, dims, split_b):
    if split_b:
        a = a.astype(BF16)
        parts = [_dot(a, p, dims) for p in _split(b, 3)]
    else:
        b = b.astype(BF16)
        parts = [_dot(p, b, dims) for p in _split(a, 3)]
    return parts[0] + (parts[1] + parts[2])


def _sigmoid(x):
    return 0.5 * jnp.tanh(0.5 * x) + 0.5


def _silu(x):
    return x * _sigmoid(x)


def _dsilu(x):
    s = _sigmoid(x)
    return s * (1.0 + x * (1.0 - s))


def _my_pos():
    return lax.axis_index("x"), lax.axis_index("y"), lax.axis_index("c")


def _peers(x, y, c):
    def flip(v, f):
        return 1 - v if f else v

    return [(flip(x, kx), flip(y, ky), flip(c, kc)) for kx in (0, 1) for ky in (0, 1) for kc in (0, 1)][1:]


def _lin(p):
    return 4 * p[0] + 2 * p[1] + p[2]


class _Comm:
    def __init__(self, kind, arrays):
        self.kind, self.arrays, self.n = kind, list(arrays), len(arrays)

    def out_shape(self):
        lead = (N_DEV,) if self.kind == "gather" else ()
        return [jax.ShapeDtypeStruct(lead + a.shape, a.dtype) for a in self.arrays]

    def scratch(self):
        return [pltpu.SemaphoreType.DMA((7 * self.n,)), pltpu.SemaphoreType.DMA((7 * self.n,)),
                pltpu.SemaphoreType.DMA((self.n,))]

    def _copies(self, ins, outs, sems, arrivals):
        send_sems, recv_sems, local_sems = sems
        x, y, c = _my_pos()
        me = _lin((x, y, c))
        gather = self.kind == "gather"
        mine = [ins[t] if gather else ins[t].at[me] for t in range(self.n)]
        remote = []
        for k, peer in enumerate(_peers(x, y, c)):
            for t in range(self.n):
                if arrivals:
                    src, dst = mine[t], outs[t].at[_lin(peer)]
                else:
                    src, dst = (ins[t] if gather else ins[t].at[_lin(peer)]), outs[t].at[me]
                remote.append(pltpu.make_async_remote_copy(
                    src_ref=src, dst_ref=dst, send_sem=send_sems.at[k * self.n + t],
                    recv_sem=recv_sems.at[k * self.n + t], device_id=peer, device_id_type=MESH_IDS))
        if arrivals:
            return remote
        return [pltpu.make_async_copy(mine[t], outs[t].at[me], local_sems.at[t]) for t in range(self.n)], remote

    def start(self, ins, outs, sems):
        local, sends = self._copies(ins, outs, sems, False)
        for cp in local + sends:
            cp.start()

    def finish(self, ins, outs, sems):
        for cp in self._copies(ins, outs, sems, True):
            cp.wait_recv()
        local, sends = self._copies(ins, outs, sems, False)
        for cp in sends:
            cp.wait_send()
        for cp in local:
            cp.wait()


def _xor(a, b):
    return a + b - 2 * a * b


class _RoutedGather(_Comm):
    def __init__(self, arrays):
        super().__init__("gather", arrays)

    def _plan(self, outs, sems):
        send_sems, recv_sems, _ = sems
        x, y, c = _my_pos()
        sib, xn, yn, dg = (x, y, 1 - c), (1 - x, y, c), (x, 1 - y, c), (1 - x, 1 - y, c)
        via = (_xor(x, 1 - c), _xor(y, c), c)
        onto = (_xor(x, c), _xor(y, 1 - c), c)
        routes = [(None, sib, sib), (None, xn, xn), (None, yn, yn), (via, onto, dg),
                  (xn, sib, (1 - x, y, 1 - c)), (yn, sib, (x, 1 - y, 1 - c)), (dg, sib, (1 - x, 1 - y, 1 - c))]

        def copy(k, t, src, slot, target):
            return pltpu.make_async_remote_copy(
                src_ref=src, dst_ref=outs[t].at[slot], send_sem=send_sems.at[k * self.n + t],
                recv_sem=recv_sems.at[k * self.n + t], device_id=target, device_id_type=MESH_IDS)

        return (x, y, c), routes, copy

    def start(self, ins, outs, sems):
        me, routes, copy = self._plan(outs, sems)
        for t in range(self.n):
            pltpu.make_async_copy(ins[t], outs[t].at[_lin(me)], sems[2].at[t]).start()
            for k in range(3):
                copy(k, t, ins[t], _lin(me), routes[k][1]).start()

    def finish(self, ins, outs, sems):
        me, routes, copy = self._plan(outs, sems)

        def arrived(k):
            for t in range(self.n):
                copy(k, t, ins[t], _lin(routes[k][2]), me).wait_recv()

        def pass_on(k):
            for t in range(self.n):
                copy(k, t, outs[t].at[_lin(routes[k][0])], _lin(routes[k][0]), routes[k][1]).start()

        arrived(1)
        arrived(2)
        for k in (3, 4, 5):
            pass_on(k)
        arrived(3)
        pass_on(6)
        for k in (0, 4, 5, 6):
            arrived(k)
        for t in range(self.n):
            for k in range(7):
                src = ins[t] if k < 3 else outs[t].at[_lin(routes[k][0])]
                copy(k, t, src, _lin(me), routes[k][1]).wait_send()
            pltpu.make_async_copy(ins[t], outs[t].at[_lin(me)], sems[2].at[t]).wait()


def _comm_call(comm, name):
    n = comm.n

    def body(*refs):
        ins, outs, sems = refs[:n], refs[n:2 * n], refs[2 * n:]
        comm.start(ins, outs, sems)
        comm.finish(ins, outs, sems)

    return pl.pallas_call(
        body, name=name, out_shape=comm.out_shape(), in_specs=[ANY] * n, out_specs=[ANY] * n,
        scratch_shapes=comm.scratch(),
    )(*comm.arrays)


def _grid_call(body, *, name, grid, in_specs, out_specs, out_shape, args, scratch_shapes=(), semantics=None, comm=None):
    if comm is None:
        return pl.pallas_call(
            body, name=name, grid=grid, in_specs=in_specs, out_specs=out_specs, out_shape=out_shape,
            scratch_shapes=list(scratch_shapes), compiler_params=_cparams(*semantics),
        )(*args)
    n_in, n_out, n_sc, n = len(in_specs), len(out_specs), len(scratch_shapes), comm.n

    def full(*refs):
        ins, refs = refs[:n_in], refs[n_in:]
        cins, refs = refs[:n], refs[n:]
        outs, refs = refs[:n_out], refs[n_out:]
        couts, refs = refs[:n], refs[n:]
        scratch, sems = refs[:n_sc], refs[n_sc:]
        ids = [pl.program_id(a) for a in range(len(grid))]
        first = functools.reduce(jnp.logical_and, [i == 0 for i in ids])
        last = functools.reduce(jnp.logical_and, [i == g - 1 for i, g in zip(ids, grid)])

        @pl.when(first)
        def _():
            comm.start(cins, couts, sems)

        body(*ins, *outs, *scratch)

        @pl.when(last)
        def _():
            comm.finish(cins, couts, sems)

    return pl.pallas_call(
        full, name=name, grid=grid, in_specs=list(in_specs) + [ANY] * n, out_specs=list(out_specs) + [ANY] * n,
        out_shape=list(out_shape) + comm.out_shape(), scratch_shapes=list(scratch_shapes) + comm.scratch(),
        compiler_params=_cparams(*(["arbitrary"] * len(grid))),
    )(*(list(args) + comm.arrays))


def _mm(a, b, mode, m, n, k, *, out_dtype, name, tm=1024, tn=1024, tk=2048,
        a_m0=0, a_k0=0, b_n0=0, b_k0=0, res=None, comm=None, loss_target=None):
    tm, tn, tk = min(tm, m), min(tn, n), min(tk, k)
    nm, nn, nk = m // tm, n // tn, k // tk
    assert nm * tm == m and nn * tn == n and nk * tk == k
    am, ak, bn, bk = a_m0 // tm, a_k0 // tk, b_n0 // tn, b_k0 // tk
    assert am * tm == a_m0 and ak * tk == a_k0 and bn * tn == b_n0 and bk * tk == b_k0
    if mode == "tn":
        a_spec = pl.BlockSpec((tk, tm), lambda i, j, q: (q + ak, i + am))
        a_dims = (0,)
    else:
        a_spec = pl.BlockSpec((tm, tk), lambda i, j, q: (i + am, q + ak))
        a_dims = (1,)
    if mode == "nt":
        b_spec = pl.BlockSpec((tn, tk), lambda i, j, q: (j + bn, q + bk))
        b_dims = (1,)
    else:
        b_spec = pl.BlockSpec((tk, tn), lambda i, j, q: (q + bk, j + bn))
        b_dims = (0,)
    o_spec = pl.BlockSpec((tm, tn), lambda i, j, q: (i, j))
    has_res = res is not None
    has_loss = loss_target is not None
    n_in = 2 + has_res + has_loss
    n_out = 3 if has_loss else 1

    def body(*refs):
        a_ref, b_ref = refs[0], refs[1]
        res_ref = refs[2] if has_res else None
        o_ref = refs[n_in]
        p = _dot(a_ref[...], b_ref[...], ((a_dims, b_dims), ((), ())))

        def finish(total):
            if has_res:
                total = total + res_ref[...].astype(F32)
            if not has_loss:
                o_ref[...] = total.astype(out_dtype)
                return
            err = total - refs[n_in - 1][...]
            grad = err * (1.0 / n)
            o_ref[...] = grad
            refs[n_in + 1][...] = grad.astype(BF16)
            l_ref = refs[n_in + 2]
            part = jnp.zeros((1, LANES), F32) + 0.5 * jnp.sum(err * err) * (1.0 / n)
            first = (pl.program_id(0) == 0) & (pl.program_id(1) == 0)

            @pl.when(first)
            def _():
                l_ref[...] = part

            @pl.when(jnp.logical_not(first))
            def _():
                l_ref[...] += part

        if nk == 1:
            finish(p)
        else:
            acc_ref = refs[n_in + n_out]
            q = pl.program_id(2)

            @pl.when(q == 0)
            def _():
                acc_ref[...] = p

            @pl.when(q > 0)
            def _():
                acc_ref[...] += p

            @pl.when(q == nk - 1)
            def _():
                finish(acc_ref[...])

    extra_in = ([res] if has_res else []) + ([loss_target] if has_loss else [])
    if has_loss:
        return _grid_call(
            body, name=name, grid=(nm, nn, nk), in_specs=[a_spec, b_spec] + [o_spec] * len(extra_in),
            out_specs=[o_spec, o_spec, pl.BlockSpec((1, LANES), lambda i, j, q: (0, 0))],
            out_shape=[jax.ShapeDtypeStruct((m, n), F32), jax.ShapeDtypeStruct((m, n), BF16),
                       jax.ShapeDtypeStruct((1, LANES), F32)],
            scratch_shapes=[pltpu.VMEM((tm, tn), F32)] if nk > 1 else [],
            args=[a, b] + extra_in, semantics=("arbitrary", "arbitrary", "arbitrary"))
    out, *carried = _grid_call(
        body, name=name, grid=(nm, nn, nk),
        in_specs=[a_spec, b_spec] + [o_spec] * len(extra_in),
        out_specs=[o_spec], out_shape=[jax.ShapeDtypeStruct((m, n), out_dtype)],
        scratch_shapes=[pltpu.VMEM((tm, tn), F32)] if nk > 1 else [],
        args=[a, b] + extra_in, semantics=("parallel", "parallel", "arbitrary"), comm=comm)
    return out if comm is None else (out, carried)


def _rms_fwd(x, w, name, tr=512, comm=None):
    t, d = x.shape
    tr = min(tr, t)

    def body(x_ref, w_ref, o_ref):
        xv = x_ref[...]
        r = lax.rsqrt(jnp.mean(xv * xv, axis=-1, keepdims=True) + EPS)
        o_ref[...] = (xv * r * w_ref[...]).astype(BF16)

    out, *carried = _grid_call(
        body, name=name, grid=(t // tr,),
        in_specs=[pl.BlockSpec((tr, d), lambda i: (i, 0)), pl.BlockSpec((1, d), lambda i: (0, 0))],
        out_specs=[pl.BlockSpec((tr, d), lambda i: (i, 0))],
        out_shape=[jax.ShapeDtypeStruct((t, d), BF16)], args=(x, w), semantics=("parallel",), comm=comm)
    return out if comm is None else (out, carried)


def _rms_bwd(x, w, dy, dres, name, tr=512):
    t, d = x.shape
    tr = min(tr, t)

    def body(x_ref, w_ref, dy_ref, dres_ref, dx_ref, dxb_ref, dw_ref):
        xv = x_ref[...]
        dyv = dy_ref[...].astype(F32)
        r = lax.rsqrt(jnp.mean(xv * xv, axis=-1, keepdims=True) + EPS)
        gy = dyv * w_ref[...]
        proj = jnp.sum(gy * xv, axis=-1, keepdims=True) * (1.0 / d)
        dx = dres_ref[...] + r * gy - xv * (r * r * r) * proj
        dx_ref[...] = dx
        dxb_ref[...] = dx.astype(BF16)
        part = jnp.sum(dyv * xv * r, axis=0, keepdims=True)

        @pl.when(pl.program_id(0) == 0)
        def _():
            dw_ref[...] = part

        @pl.when(pl.program_id(0) > 0)
        def _():
            dw_ref[...] += part

    row = pl.BlockSpec((tr, d), lambda i: (i, 0))
    vec = pl.BlockSpec((1, d), lambda i: (0, 0))
    return pl.pallas_call(
        body, name=name, grid=(t // tr,),
        in_specs=[row, vec, row, row], out_specs=[row, row, vec],
        out_shape=[jax.ShapeDtypeStruct((t, d), F32), jax.ShapeDtypeStruct((t, d), BF16),
                   jax.ShapeDtypeStruct((1, d), F32)],
        compiler_params=_cparams("arbitrary"),
    )(x, w, dy, dres)


def _adamw(parts, w, m, v, name, tr=128, transposed=False):
    r, c = w.shape
    tr = tr if r % tr == 0 else r
    c1 = 1.0 - ADAM_B1 ** ADAM_STEP
    c2 = 1.0 - ADAM_B2 ** ADAM_STEP

    def body(p_ref, w_ref, m_ref, v_ref, g_ref, d_ref, nm_ref, nv_ref):
        g = p_ref[0].astype(F32)
        for s in range(1, N_DEV):
            g = g + p_ref[s].astype(F32)
        if transposed:
            i, j = lax.broadcasted_iota(jnp.int32, (tr, tr), 0), lax.broadcasted_iota(jnp.int32, (tr, tr), 1)
            g = _dot_exact(jnp.where(i == j, 1.0, 0.0), g, NT, True)
        nm = ADAM_B1 * m_ref[...] + (1.0 - ADAM_B1) * g
        nv = ADAM_B2 * v_ref[...] + (1.0 - ADAM_B2) * (g * g)
        m_hat = nm / c1
        v_hat = nv / c2
        g_ref[...] = g
        d_ref[...] = -ADAM_LR * (m_hat / (jnp.sqrt(v_hat) + ADAM_EPS) + ADAM_WD * w_ref[...])
        nm_ref[...] = nm
        nv_ref[...] = nv

    blk = pl.BlockSpec((tr, c), lambda i: (i, 0))
    p_spec = (pl.BlockSpec((N_DEV, c, tr), lambda i: (0, 0, i)) if transposed
              else pl.BlockSpec((N_DEV, tr, c), lambda i: (0, i, 0)))
    return pl.pallas_call(
        body, name=name, grid=(r // tr,),
        in_specs=[p_spec, blk, blk, blk],
        out_specs=[blk] * 4, out_shape=[jax.ShapeDtypeStruct((r, c), F32)] * 4,
        compiler_params=_cparams("parallel"),
    )(parts, w, m, v)


ROW_TILE, ROW_HEADS = 512, 16
CONV_ROWS, CONV_HEADS = 512, 8


def _window(dest, inner, n_in, out_index):
    if dest is None:
        return inner, 0, [], [], {}
    buf, col0, total = dest
    if buf is None:
        return total, col0, [], [], {}
    return total, col0, [buf], [ANY], {n_in: out_index}


def _skip_ref(body, at, count):
    return body if count == 0 else (lambda *refs: body(*refs[:at], *refs[at + count:]))


def _heads_of(x, nh):
    return [x[:, h * HEAD_DIM:(h + 1) * HEAD_DIM] for h in range(nh)]


def _headnorm_fwd(proj, w, col0, inner, name, tr=ROW_TILE, hb=ROW_HEADS):
    t = proj.shape[0]
    tr = min(tr, t)
    hb = min(hb, inner // HEAD_DIM)
    wc = hb * HEAD_DIM
    c0 = col0 // wc

    def body(x_ref, w_ref, o_ref):
        outs = []
        for xh in _heads_of(x_ref[...], hb):
            r = lax.rsqrt(jnp.mean(xh * xh, axis=-1, keepdims=True) + EPS)
            outs.append((xh * r * w_ref[...]).astype(BF16))
        o_ref[...] = jnp.concatenate(outs, axis=1)

    return pl.pallas_call(
        body, name=name, grid=(t // tr, inner // wc),
        in_specs=[pl.BlockSpec((tr, wc), lambda i, j: (i, j + c0)), pl.BlockSpec((1, HEAD_DIM), lambda i, j: (0, 0))],
        out_specs=pl.BlockSpec((tr, wc), lambda i, j: (i, j)),
        out_shape=jax.ShapeDtypeStruct((t, inner), BF16),
        compiler_params=_cparams("parallel", "parallel"),
    )(proj, w)


def _headnorm_bwd(dy, proj, w, col0, inner, name, tr=ROW_TILE, hb=ROW_HEADS, dest=None):
    t = proj.shape[0]
    tr = min(tr, t)
    hb = min(hb, inner // HEAD_DIM)
    wc = hb * HEAD_DIM
    c0 = col0 // wc
    width, out0, more, more_specs, aliases = _window(dest, inner, 3, 0)

    def body(dy_ref, x_ref, w_ref, dx_ref, dw_ref):
        outs = []
        part = jnp.zeros((1, HEAD_DIM), F32)
        for dyh, xh in zip(_heads_of(dy_ref[...], hb), _heads_of(x_ref[...], hb)):
            r = lax.rsqrt(jnp.mean(xh * xh, axis=-1, keepdims=True) + EPS)
            gy = dyh * w_ref[...]
            pr = jnp.sum(gy * xh, axis=-1, keepdims=True) * (1.0 / HEAD_DIM)
            outs.append((r * gy - xh * (r * r * r) * pr).astype(BF16))
            part = part + jnp.sum(dyh * xh * r, axis=0, keepdims=True)
        dx_ref[...] = jnp.concatenate(outs, axis=1)
        first = (pl.program_id(0) == 0) & (pl.program_id(1) == 0)

        @pl.when(first)
        def _():
            dw_ref[...] = part

        @pl.when(jnp.logical_not(first))
        def _():
            dw_ref[...] += part

    blk = pl.BlockSpec((tr, wc), lambda i, j: (i, j))
    return pl.pallas_call(
        _skip_ref(body, 3, len(more)), name=name, grid=(t // tr, inner // wc),
        in_specs=[blk, pl.BlockSpec((tr, wc), lambda i, j: (i, j + c0)),
                  pl.BlockSpec((1, HEAD_DIM), lambda i, j: (0, 0))] + more_specs,
        out_specs=[pl.BlockSpec((tr, wc), lambda i, j: (i, j + out0 // wc)),
                   pl.BlockSpec((1, HEAD_DIM), lambda i, j: (0, 0))],
        out_shape=[jax.ShapeDtypeStruct((t, width), BF16), jax.ShapeDtypeStruct((1, HEAD_DIM), F32)],
        input_output_aliases=aliases, compiler_params=_cparams("arbitrary", "arbitrary"),
    )(dy, proj, w, *more)


def _gate_fwd(o, proj, zcol0, inner, name, norm_w=None, tr=ROW_TILE, hb=ROW_HEADS):
    t = o.shape[0]
    tr = min(tr, t)
    hb = min(hb, inner // HEAD_DIM)
    wc = hb * HEAD_DIM
    c0 = zcol0 // wc
    has_w = norm_w is not None

    def body(*refs):
        o_ref, z_ref = refs[0], refs[1]
        out_ref = refs[2 + has_w]
        outs = []
        for oh, zh in zip(_heads_of(o_ref[...], hb), _heads_of(z_ref[...], hb)):
            if has_w:
                r = lax.rsqrt(jnp.mean(oh * oh, axis=-1, keepdims=True) + EPS)
                oh = oh * r * refs[2][...]
            outs.append((oh * _silu(zh)).astype(BF16))
        out_ref[...] = jnp.concatenate(outs, axis=1)

    blk = pl.BlockSpec((tr, wc), lambda i, j: (i, j))
    vec = pl.BlockSpec((1, HEAD_DIM), lambda i, j: (0, 0))
    return pl.pallas_call(
        body, name=name, grid=(t // tr, inner // wc),
        in_specs=[blk, pl.BlockSpec((tr, wc), lambda i, j: (i, j + c0))] + ([vec] if has_w else []),
        out_specs=blk, out_shape=jax.ShapeDtypeStruct((t, inner), BF16),
        compiler_params=_cparams("parallel", "parallel"),
    )(*([o, proj] + ([norm_w] if has_w else [])))


def _gate_bwd(dg, o, proj, zcol0, inner, name, do_dtype, norm_w=None, tr=ROW_TILE, hb=ROW_HEADS, dest=None):
    t = o.shape[0]
    tr = min(tr, t)
    hb = min(hb, inner // HEAD_DIM)
    wc = hb * HEAD_DIM
    c0 = zcol0 // wc
    has_w = norm_w is not None
    width, out0, more, more_specs, aliases = _window(dest, inner, 3 + has_w, 1)

    def body(*refs):
        dg_ref, o_ref, z_ref = refs[0], refs[1], refs[2]
        do_ref, dz_ref = refs[3 + has_w], refs[4 + has_w]
        dos, dzs = [], []
        part = jnp.zeros((1, HEAD_DIM), F32)
        for dgh, oh, zh in zip(_heads_of(dg_ref[...], hb), _heads_of(o_ref[...], hb), _heads_of(z_ref[...], hb)):
            dy = dgh * _silu(zh)
            if has_w:
                w = refs[3][...]
                r = lax.rsqrt(jnp.mean(oh * oh, axis=-1, keepdims=True) + EPS)
                on = oh * r
                dzs.append((dgh * on * w * _dsilu(zh)).astype(BF16))
                gy = dy * w
                pr = jnp.sum(gy * oh, axis=-1, keepdims=True) * (1.0 / HEAD_DIM)
                dos.append((r * gy - oh * (r * r * r) * pr).astype(do_dtype))
                part = part + jnp.sum(dy * on, axis=0, keepdims=True)
            else:
                dzs.append((dgh * oh * _dsilu(zh)).astype(BF16))
                dos.append(dy.astype(do_dtype))
        do_ref[...] = jnp.concatenate(dos, axis=1)
        dz_ref[...] = jnp.concatenate(dzs, axis=1)
        if has_w:
            dw_ref = refs[6]
            first = (pl.program_id(0) == 0) & (pl.program_id(1) == 0)

            @pl.when(first)
            def _():
                dw_ref[...] = part

            @pl.when(jnp.logical_not(first))
            def _():
                dw_ref[...] += part

    blk = pl.BlockSpec((tr, wc), lambda i, j: (i, j))
    vec = pl.BlockSpec((1, HEAD_DIM), lambda i, j: (0, 0))
    return pl.pallas_call(
        _skip_ref(body, 3 + has_w, len(more)), name=name, grid=(t // tr, inner // wc),
        in_specs=[blk, blk, pl.BlockSpec((tr, wc), lambda i, j: (i, j + c0))] + ([vec] if has_w else []) + more_specs,
        out_specs=[blk, pl.BlockSpec((tr, wc), lambda i, j: (i, j + out0 // wc))] + ([vec] if has_w else []),
        out_shape=[jax.ShapeDtypeStruct((t, inner), do_dtype), jax.ShapeDtypeStruct((t, width), BF16)]
        + ([jax.ShapeDtypeStruct((1, HEAD_DIM), F32)] if has_w else []),
        input_output_aliases=aliases, compiler_params=_cparams("arbitrary", "arbitrary"),
    )(*([dg, o, proj] + ([norm_w] if has_w else []) + more))


N_REL = 2 * REL_CLIP + 1
REL_PAD = 640
WIN = 2 * Q_TILE


def _diag_onehot():
    i = lax.broadcasted_iota(jnp.int32, (REL_PAD, WIN), 0)
    j = lax.broadcasted_iota(jnp.int32, (REL_PAD, WIN), 1)
    rel = jnp.where(j < Q_TILE + CHUNK, Q_TILE - j, Q_TILE + WIN - j)
    used = (j < Q_TILE + CHUNK) | (j > WIN - CHUNK)
    idx = jnp.clip(rel, -REL_CLIP, REL_CLIP) + REL_CLIP
    return jnp.where(used & (i == idx), 1.0, 0.0).astype(F32)


def _band_mask():
    r = lax.broadcasted_iota(jnp.int32, (Q_TILE, WIN), 0) // CHUNK
    kc = lax.broadcasted_iota(jnp.int32, (Q_TILE, WIN), 1) // CHUNK - LEFT_CHUNKS
    return (kc <= r) & (kc >= r - LEFT_CHUNKS)


def _bias_tiles(rel_bias_pad, name):
    nh = rel_bias_pad.shape[0]

    def body(rb_ref, o_ref):
        dvec = _nn(rb_ref[...], _diag_onehot(), HIGHEST)[0:1, :]
        tile = pltpu.roll(jnp.broadcast_to(dvec, (Q_TILE, WIN)), 0, 1, stride=1, stride_axis=0)
        o_ref[...] = jnp.where(_band_mask(), tile, NEG_BIG)

    return pl.pallas_call(
        body, name=name, grid=(nh,),
        in_specs=[pl.BlockSpec((None, 8, REL_PAD), lambda h: (h, 0, 0))],
        out_specs=pl.BlockSpec((None, Q_TILE, WIN), lambda h: (h, 0, 0)),
        out_shape=jax.ShapeDtypeStruct((nh, Q_TILE, WIN), F32),
        compiler_params=_cparams("parallel"),
    )(rel_bias_pad)


def _bias_grad(dtile, name):
    nh = dtile.shape[0]

    def body(d_ref, o_ref):
        ri = lax.broadcasted_iota(jnp.int32, (Q_TILE, Q_TILE), 0)
        ci = lax.broadcasted_iota(jnp.int32, (Q_TILE, Q_TILE), 1)
        flip = jnp.where(ri + ci == Q_TILE - 1, 1.0, 0.0).astype(F32)
        rev = _dot_exact(flip, d_ref[...], NN, True)
        rolled = pltpu.roll(rev, WIN - (Q_TILE - 1), 1, stride=1, stride_axis=0)
        diag = jnp.broadcast_to(jnp.sum(rolled, axis=0, keepdims=True), (8, WIN))
        o_ref[...] = _nt(diag, _diag_onehot(), HIGHEST)

    return pl.pallas_call(
        body, name=name, grid=(nh,),
        in_specs=[pl.BlockSpec((None, Q_TILE, WIN), lambda h: (h, 0, 0))],
        out_specs=pl.BlockSpec((None, 8, REL_PAD), lambda h: (h, 0, 0)),
        out_shape=jax.ShapeDtypeStruct((nh, 8, REL_PAD), F32),
        compiler_params=_cparams("parallel"),
    )(dtile)


GROUP = 2 * CHUNK
BAND = Q_TILE + GROUP


N_GROUPS = Q_TILE // GROUP
ATTN_HB = 2


def _head_cols(j):
    return slice(j * HEAD_DIM, (j + 1) * HEAD_DIM)


def _groups(ref, units):
    return jnp.stack([ref[GROUP * g:GROUP * (g + 1), _head_cols(j)] for j, g in units])


def _bands(r0_ref, r1_ref, units):
    return jnp.stack([jnp.concatenate([r0_ref[GROUP * g:, _head_cols(j)], r1_ref[:GROUP * (g + 1), _head_cols(j)]],
                                      axis=0) for j, g in units])


def _group_probs(q, kw, b_ref, units, first_tile):
    bias = jnp.stack([b_ref[j, GROUP * g:GROUP * (g + 1), GROUP * g:GROUP * g + BAND] for j, g in units])
    s = _dot(q, kw, BNT) * (HEAD_DIM ** -0.5) + bias
    col = jnp.stack([lax.broadcasted_iota(jnp.int32, (GROUP, BAND), 1) + GROUP * g for _, g in units])
    s = jnp.where(first_tile & (col < Q_TILE), NEG_BIG, s)
    p = jnp.exp(s - jnp.max(s, axis=-1, keepdims=True))
    return p * (1.0 / jnp.sum(p, axis=-1, keepdims=True))


def _attn_fwd(q, k, v, v_col0, bias, name):
    t, inner = q.shape
    nh, nt = inner // HEAD_DIM, t // Q_TILE
    hb = min(ATTN_HB, nh)
    wc = hb * HEAD_DIM
    vh = v_col0 // wc
    units = [(j, g) for j in range(hb) for g in range(N_GROUPS)]

    def body(q_ref, k0_ref, k1_ref, v0_ref, v1_ref, b_ref, o_ref):
        p = _group_probs(_groups(q_ref, units), _bands(k0_ref, k1_ref, units), b_ref, units, pl.program_id(1) == 0)
        o = _dot(_bf(p), _bf(_bands(v0_ref, v1_ref, units)), BNN)
        for n, (j, g) in enumerate(units):
            o_ref[GROUP * g:GROUP * (g + 1), _head_cols(j)] = o[n]

    cur = pl.BlockSpec((Q_TILE, wc), lambda h, i: (i, h))
    prev = pl.BlockSpec((Q_TILE, wc), lambda h, i: (jnp.maximum(i - 1, 0), h))
    v_cur = pl.BlockSpec((Q_TILE, wc), lambda h, i: (i, h + vh))
    v_prev = pl.BlockSpec((Q_TILE, wc), lambda h, i: (jnp.maximum(i - 1, 0), h + vh))
    return pl.pallas_call(
        body, name=name, grid=(nh // hb, nt),
        in_specs=[cur, prev, cur, v_prev, v_cur, pl.BlockSpec((hb, Q_TILE, WIN), lambda h, i: (h, 0, 0))],
        out_specs=cur, out_shape=jax.ShapeDtypeStruct((t, inner), F32),
        compiler_params=_cparams("parallel", "parallel"),
    )(q, k, k, v, v, bias)


def _attn_bwd(q, k, v, v_col0, do, bias, name, dest=None):
    t, inner = q.shape
    nh, nt = inner // HEAD_DIM, t // Q_TILE
    scale = HEAD_DIM ** -0.5
    hb = min(ATTN_HB, nh)
    wc = hb * HEAD_DIM
    units = [(j, g) for j in range(hb) for g in range(N_GROUPS)]

    def body(q_ref, k0_ref, k1_ref, v0_ref, v1_ref, do_ref, b_ref, dq_ref, dk_ref, dv_ref, db_ref,
             ck_ref, cv_ref, wk_ref, wv_ref):
        i = pl.program_id(1)

        @pl.when(i == 0)
        def _():
            ck_ref[...] = jnp.zeros(blk, F32)
            cv_ref[...] = jnp.zeros(blk, F32)
            db_ref[...] = jnp.zeros((hb, Q_TILE, WIN), F32)

        @pl.when(i < nt)
        def _():
            wk_ref[...] = jnp.zeros((WIN, wc), F32)
            wv_ref[...] = jnp.zeros((WIN, wc), F32)
            qv, dov = _groups(q_ref, units), _groups(do_ref, units)
            kw, vw = _bands(k0_ref, k1_ref, units), _bf(_bands(v0_ref, v1_ref, units))
            p = _group_probs(qv, kw, b_ref, units, i == 0)
            dp = _dot(dov, vw, BNT)
            ds = p * (dp - jnp.sum(p * dp, axis=-1, keepdims=True))
            pb, dsb = _bf(p), _bf(ds)
            dq = _dot(dsb, kw, BNN) * scale
            dkw = _dot(dsb, qv, BTN) * scale
            dvw = _dot(pb, dov, BTN)
            for n, (j, g) in enumerate(units):
                rows, cols = slice(GROUP * g, GROUP * (g + 1)), slice(GROUP * g, GROUP * g + BAND)
                db_ref[j, rows, cols] += ds[n]
                dq_ref[rows, _head_cols(j)] = dq[n]
                wk_ref[cols, _head_cols(j)] += dkw[n]
                wv_ref[cols, _head_cols(j)] += dvw[n]
            dk_ref[...] = ck_ref[...] + wk_ref[:Q_TILE, :]
            dv_ref[...] = (cv_ref[...] + wv_ref[:Q_TILE, :]).astype(BF16)
            ck_ref[...] = wk_ref[Q_TILE:, :]
            cv_ref[...] = wv_ref[Q_TILE:, :]

        @pl.when(i == nt)
        def _():
            dk_ref[...] = ck_ref[...]
            dv_ref[...] = cv_ref[...].astype(BF16)

    blk = (Q_TILE, wc)
    cur = pl.BlockSpec(blk, lambda h, i: (jnp.minimum(i, nt - 1), h))
    prev = pl.BlockSpec(blk, lambda h, i: (jnp.clip(i - 1, 0, nt - 1), h))
    lag = pl.BlockSpec(blk, lambda h, i: (jnp.maximum(i - 1, 0), h))
    vh = v_col0 // wc
    v_cur = pl.BlockSpec(blk, lambda h, i: (jnp.minimum(i, nt - 1), h + vh))
    v_prev = pl.BlockSpec(blk, lambda h, i: (jnp.clip(i - 1, 0, nt - 1), h + vh))
    tile = pl.BlockSpec((hb, Q_TILE, WIN), lambda h, i: (h, 0, 0))
    width, out0, more, more_specs, aliases = _window(dest, inner, 7, 2)
    return pl.pallas_call(
        _skip_ref(body, 7, len(more)), name=name, grid=(nh // hb, nt + 1),
        in_specs=[cur, prev, cur, v_prev, v_cur, cur, tile] + more_specs,
        out_specs=[cur, lag, pl.BlockSpec(blk, lambda h, i: (jnp.maximum(i - 1, 0), h + out0 // wc)), tile],
        out_shape=[jax.ShapeDtypeStruct((t, inner), F32)] * 2 + [jax.ShapeDtypeStruct((t, width), BF16),
                                                                 jax.ShapeDtypeStruct((nh, Q_TILE, WIN), F32)],
        scratch_shapes=[pltpu.VMEM(blk, F32), pltpu.VMEM(blk, F32),
                        pltpu.VMEM((WIN, wc), F32), pltpu.VMEM((WIN, wc), F32)],
        input_output_aliases=aliases, compiler_params=_cparams("arbitrary", "arbitrary"),
    )(q, k, k, v, v, do, bias, *more)


def _pad_rel_bias(rel_bias):
    nh = rel_bias.shape[0]
    return jnp.broadcast_to(jnp.pad(rel_bias, ((0, 0), (0, REL_PAD - N_REL)))[:, None, :], (nh, 8, REL_PAD))


def _layer_b_fwd(h1, nw, w_in_t, qw, kw, bias, w_out, target):
    t, d = h1.shape
    inner = w_out.shape[0]
    hn = _rms_fwd(h1, nw, "b_rms")
    proj = _mm(hn, w_in_t, "nt", t, 4 * inner, d, out_dtype=F32, name="b_proj")
    qn = _headnorm_fwd(proj, qw, 0, inner, "b_qnorm")
    kn = _headnorm_fwd(proj, kw, inner, inner, "b_knorm")
    o = _attn_fwd(qn, kn, proj, 2 * inner, bias, "b_attn")
    g = _gate_fwd(o, proj, 3 * inner, inner, "b_gate")
    loss_parts = _mm(g, w_out, "nn", t, d, inner, out_dtype=F32, name="b_out", res=h1, loss_target=target)
    return loss_parts, (hn, proj, qn, kn, o, g)


def _layer_b_bwd(dh2, dh2b, h1, nw, w_in_t, qw, kw, bias, w_out, saved):
    hn, proj, qn, kn, o, g = saved
    t, d = h1.shape
    inner = w_out.shape[0]
    dg = _mm(dh2b, w_out, "nt", t, inner, d, out_dtype=F32, name="b_dgate")
    dw_out = _mm(g, dh2b, "tn", inner, d, t, out_dtype=BF16, name="b_dwout")
    do, dproj = _gate_bwd(dg, o, proj, 3 * inner, inner, "b_gate_bwd", BF16, dest=(None, 3 * inner, 4 * inner))
    dq, dk, dproj, dtile = _attn_bwd(qn, kn, proj, 2 * inner, do, bias, "b_attn_bwd",
                                     dest=(dproj, 2 * inner, 4 * inner))
    dproj, dqw = _headnorm_bwd(dq, proj, qw, 0, inner, "b_qnorm_bwd", dest=(dproj, 0, 4 * inner))
    dproj, dkw = _headnorm_bwd(dk, proj, kw, inner, inner, "b_knorm_bwd", dest=(dproj, inner, 4 * inner))
    dhn = _mm(dproj, w_in_t, "nn", t, d, 4 * inner, out_dtype=F32, name="b_dhn")
    dw_in_t = _mm(dproj, hn, "tn", 4 * inner, d, t, out_dtype=BF16, name="b_dwin")
    dh1, dh1b, dnw = _rms_bwd(h1, nw, dhn, dh2, "b_rms_bwd")
    drb = _bias_grad(dtile, "b_bias_grad")[:, 0, :N_REL]
    return dh1, dh1b, dnw, dw_in_t, dqw, dkw, drb, dw_out


LANES = 128


def _softplus(x):
    return jnp.maximum(x, 0.0) + jnp.log1p(jnp.exp(-jnp.abs(x)))


def _gates_fwd(ab, alog_row, dt_row, nh, name, tr=1024):
    t = ab.shape[0]
    tr = min(tr, t)

    def body(x_ref, al_ref, dt_ref, o_ref):
        x = x_ref[...]
        lane = lax.broadcasted_iota(jnp.int32, x.shape, 1)
        g = -jnp.exp(al_ref[...]) * _softplus(x + dt_ref[...])
        o_ref[...] = jnp.where(lane < nh, g, jnp.where(lane < 2 * nh, _sigmoid(x), 0.0))

    row = pl.BlockSpec((tr, LANES), lambda i: (i, 0))
    vec = pl.BlockSpec((1, LANES), lambda i: (0, 0))
    return pl.pallas_call(
        body, name=name, grid=(t // tr,), in_specs=[row, vec, vec], out_specs=row,
        out_shape=jax.ShapeDtypeStruct((t, LANES), F32), compiler_params=_cparams("parallel"),
    )(ab, alog_row, dt_row)


def _gates_bwd(ab, alog_row, dt_row, dgates, nh, name, tr=1024):
    t = ab.shape[0]
    tr = min(tr, t)
    npart = dgates.shape[0]

    def body(x_ref, al_ref, dt_ref, dg_ref, dx_ref, s_ref):
        x = x_ref[...]
        lane = lax.broadcasted_iota(jnp.int32, x.shape, 1)
        dgt = dg_ref[0]
        for p in range(1, npart):
            dgt = dgt + dg_ref[p]
        ea = jnp.exp(al_ref[...])
        xa = x + dt_ref[...]
        da = jnp.where(lane < nh, dgt * (-ea) * _sigmoid(xa), 0.0)
        beta = _sigmoid(x)
        db = jnp.where((lane >= nh) & (lane < 2 * nh), dgt * beta * (1.0 - beta), 0.0)
        dx_ref[...] = (da + db).astype(BF16)
        dal = jnp.sum(jnp.where(lane < nh, dgt * (-ea) * _softplus(xa), 0.0), axis=0, keepdims=True)
        ddt = jnp.sum(da, axis=0, keepdims=True)
        r8 = lax.broadcasted_iota(jnp.int32, (8, LANES), 0)
        part = jnp.where(r8 == 0, dal, jnp.where(r8 == 1, ddt, 0.0))

        @pl.when(pl.program_id(0) == 0)
        def _():
            s_ref[...] = part

        @pl.when(pl.program_id(0) > 0)
        def _():
            s_ref[...] += part

    row = pl.BlockSpec((tr, LANES), lambda i: (i, 0))
    vec = pl.BlockSpec((1, LANES), lambda i: (0, 0))
    return pl.pallas_call(
        body, name=name, grid=(t // tr,),
        in_specs=[row, vec, vec, pl.BlockSpec((npart, tr, LANES), lambda i: (0, i, 0))],
        out_specs=[row, pl.BlockSpec((8, LANES), lambda i: (0, 0))],
        out_shape=[jax.ShapeDtypeStruct((t, LANES), BF16), jax.ShapeDtypeStruct((8, LANES), F32)],
        compiler_params=_cparams("arbitrary"),
    )(ab, alog_row, dt_row, dgates)


HALO = 8


def _delayed(ext, rows):
    return [ext[HALO:HALO + rows]] + [pltpu.roll(ext, s, 0)[HALO:HALO + rows] for s in range(1, CONV_K)]


def _conv_taps(delayed, w):
    acc = delayed[0] * w[CONV_K - 1:CONV_K]
    for s in range(1, CONV_K):
        acc = acc + delayed[s] * w[CONV_K - 1 - s:CONV_K - s]
    return acc


def _conv_fwd(proj, conv_w, col0, inner, mode, name, tt=CONV_ROWS, hb=CONV_HEADS):
    t = proj.shape[0]
    tt = min(tt, t)
    hb = min(hb, inner // HEAD_DIM)
    wc = hb * HEAD_DIM
    c0 = col0 // wc
    hpb = tt // HALO

    def body(x_ref, halo_ref, w_ref, o_ref):
        halo = jnp.where(pl.program_id(1) == 0, 0.0, halo_ref[...])
        s = _silu(_conv_taps(_delayed(jnp.concatenate([halo, x_ref[...]], axis=0), tt), w_ref[...]))
        if mode == "v":
            o_ref[...] = s
        else:
            mul = HEAD_DIM ** -0.5 if mode == "q" else 1.0
            o_ref[...] = jnp.concatenate(
                [sh * (lax.rsqrt(jnp.sum(sh * sh, axis=-1, keepdims=True) + EPS) * mul) for sh in _heads_of(s, hb)], axis=1)

    return pl.pallas_call(
        body, name=name, grid=(inner // wc, t // tt),
        in_specs=[pl.BlockSpec((tt, wc), lambda j, i: (i, j + c0)),
                  pl.BlockSpec((HALO, wc), lambda j, i: (jnp.maximum(i * hpb - 1, 0), j + c0)),
                  pl.BlockSpec((CONV_K, wc), lambda j, i: (0, j + c0))],
        out_specs=pl.BlockSpec((tt, wc), lambda j, i: (i, j)),
        out_shape=jax.ShapeDtypeStruct((t, inner), F32),
        compiler_params=_cparams("parallel", "parallel"),
    )(proj, proj, conv_w)


def _conv_bwd(dy, proj, conv_w, col0, inner, mode, name, tt=CONV_ROWS, hb=CONV_HEADS, dest=None):
    t = proj.shape[0]
    tt = min(tt, t)
    nt = t // tt
    hb = min(hb, inner // HEAD_DIM)
    wc = hb * HEAD_DIM
    c0 = col0 // wc
    hpb = tt // HALO
    rows = tt + HALO

    def body(dy_ref, dyn_ref, x_ref, xp_ref, xn_ref, w_ref, dx_ref, dw_ref):
        i = pl.program_id(1)
        w = w_ref[...]
        xprev = jnp.where(i == 0, 0.0, xp_ref[...])
        delayed = _delayed(jnp.concatenate([xprev, x_ref[...], xn_ref[...]], axis=0), rows)
        c = _conv_taps(delayed, w)
        dyv = jnp.concatenate([dy_ref[...], jnp.where(i == nt - 1, 0.0, dyn_ref[...])], axis=0)
        sg = _sigmoid(c)
        s = c * sg
        if mode == "v":
            ds = dyv
        else:
            mul = HEAD_DIM ** -0.5 if mode == "q" else 1.0
            parts = []
            for dyh, sh in zip(_heads_of(dyv, hb), _heads_of(s, hb)):
                r = lax.rsqrt(jnp.sum(sh * sh, axis=-1, keepdims=True) + EPS)
                parts.append(mul * (r * dyh - sh * (r * r * r) * jnp.sum(dyh * sh, axis=-1, keepdims=True)))
            ds = jnp.concatenate(parts, axis=1)
        dc = ds * (sg * (1.0 + c * (1.0 - sg)))
        dx = dc[:tt] * w[CONV_K - 1:CONV_K]
        for sft in range(1, CONV_K):
            dx = dx + pltpu.roll(dc, rows - sft, 0)[:tt] * w[CONV_K - 1 - sft:CONV_K - sft]
        dx_ref[...] = dx.astype(BF16)
        r8 = lax.broadcasted_iota(jnp.int32, (8, wc), 0)
        part = jnp.zeros((8, wc), F32)
        for sft in range(CONV_K):
            part = part + jnp.where(r8 == CONV_K - 1 - sft,
                                    jnp.sum(dc[:tt] * delayed[sft][:tt], axis=0, keepdims=True), 0.0)

        @pl.when(i == 0)
        def _():
            dw_ref[...] = part

        @pl.when(i > 0)
        def _():
            dw_ref[...] += part

    cur = lambda off: pl.BlockSpec((tt, wc), lambda j, i: (i, j + off))
    nxt = lambda off: pl.BlockSpec((HALO, wc), lambda j, i: (jnp.minimum((i + 1) * hpb, t // HALO - 1), j + off))
    width, out0, more, more_specs, aliases = _window(dest, inner, 6, 0)
    return pl.pallas_call(
        _skip_ref(body, 6, len(more)), name=name, grid=(inner // wc, nt),
        in_specs=[cur(0), nxt(0), cur(c0),
                  pl.BlockSpec((HALO, wc), lambda j, i: (jnp.maximum(i * hpb - 1, 0), j + c0)), nxt(c0),
                  pl.BlockSpec((CONV_K, wc), lambda j, i: (0, j + c0))] + more_specs,
        out_specs=[pl.BlockSpec((tt, wc), lambda j, i: (i, j + out0 // wc)),
                   pl.BlockSpec((8, wc), lambda j, i: (0, j))],
        out_shape=[jax.ShapeDtypeStruct((t, width), BF16), jax.ShapeDtypeStruct((8, inner), F32)],
        input_output_aliases=aliases, compiler_params=_cparams("parallel", "arbitrary"),
    )(dy, dy, proj, proj, proj, conv_w, *more)


GDN_HB = 4
GDN_NB = 8
SCAN_HB = 16
SCAN_NB = 4


def _iota2(n, m):
    return lax.broadcasted_iota(jnp.int32, (n, m), 0), lax.broadcasted_iota(jnp.int32, (n, m), 1)


def _head_select(first_head, hb, lane0):
    r, lane = _iota2(8, LANES)
    return jnp.where((r < hb) & (lane == lane0 + first_head + r), 1.0, 0.0).astype(F32)


def _chunk_gates(gt, selg, selb):
    i, j = _iota2(CHUNK, CHUNK)
    gc_all = _dot_exact(jnp.where(j <= i, 1.0, 0.0), gt, NN, True)
    return (_dot_exact(gc_all, selg, NT, False), _dot_exact(selg, gc_all, NT, True),
            _dot_exact(gt, selb, NT, False))


def _decay_terms(gcol, grow):
    i, j = _iota2(CHUNK, CHUNK)
    glast = gcol[:, CHUNK - 1:CHUNK, :]
    decay = jnp.exp(jnp.where(j <= i, gcol - grow, NEG_BIG))
    return jnp.exp(gcol), jnp.exp(glast - gcol), jnp.exp(glast), decay


def _unit_lower_inverse(a):
    i, j = _iota2(CHUNK, CHUNK)
    same16 = (i // 16) == (j // 16)
    same32 = (i // 32) == (j // 32)
    m = jnp.where(same16, -a, 0.0)
    x = jnp.where(i == j, 1.0, 0.0) + m
    for _ in range(3):
        m = _dot3(m, m, BNN)
        x = x + _dot3(x, m, BNN)
    for off in (jnp.where(same32 & jnp.logical_not(same16), a, 0.0), jnp.where(same32, 0.0, a)):
        x = x - _dot3(_dot3(x, off, BNN), x, BNN)
    return x


def _unit_inputs(refs, g_ref, selg, selb, hb, nb):
    units = [(c, h) for c in range(nb) for h in range(hb)]
    rs = lambda c: slice(c * CHUNK, (c + 1) * CHUNK)
    cs = lambda h: slice(h * HEAD_DIM, (h + 1) * HEAD_DIM)
    gates = [_chunk_gates(g_ref[rs(c), :], selg, selb) for c in range(nb)]
    stacked = [jnp.stack([r[rs(c), cs(h)] for c, h in units]) for r in refs]
    gcol = jnp.stack([gates[c][0][:, h:h + 1] for c, h in units])
    grow = jnp.stack([gates[c][1][h:h + 1, :] for c, h in units])
    bcol = jnp.stack([gates[c][2][:, h:h + 1] for c, h in units])
    return units, rs, cs, stacked, gcol, grow, bcol


def _gdn_specs(nh, inner, t, heads=GDN_HB, chunks=GDN_NB):
    hb, nb = min(heads, nh), chunks
    rows = nb * CHUNK
    wide = pl.BlockSpec((rows, hb * HEAD_DIM), lambda g, n: (n, g))
    sq = pl.BlockSpec((hb, rows, CHUNK), lambda g, n: (g, n, 0))
    gts = pl.BlockSpec((rows, LANES), lambda g, n: (n, 0))
    glb = pl.BlockSpec((nb * 8, hb * HEAD_DIM), lambda g, n: (n, g))
    return hb, nb, rows, wide, sq, gts, glb


def _gdn_intra_fwd(q, k, v, gates, nh, name, comm=None):
    t, inner = q.shape
    hb, nb, rows, wide, sq, gts, glb = _gdn_specs(nh, inner, t)

    def body(q_ref, k_ref, v_ref, g_ref, qe_ref, kel_ref, wb_ref, w_ref, u_ref, qk_ref, tm_ref, gl_ref):
        first = pl.program_id(0) * hb
        selg, selb = _head_select(first, hb, 0), _head_select(first, hb, nh)
        i, j = _iota2(CHUNK, CHUNK)
        units, rs, cs, (qv, kv, vv), gcol, grow, bcol = _unit_inputs(
            (q_ref, k_ref, v_ref), g_ref, selg, selb, hb, nb)
        e, el, gl, decay = _decay_terms(gcol, grow)
        kb = kv * bcol
        qbf, kbf = _bf(qv), _bf(kv)
        a = jnp.where(j < i, _dot(_bf(kb), kbf, BNT) * decay, 0.0)
        tm = _unit_lower_inverse(a)
        uw = _dot3(tm, jnp.concatenate([vv * bcol, kb * e], axis=2), BNN)
        qk = _bf(_dot(qbf, kbf, BNT) * decay)
        qe, kel = _bf(qv * e), _bf(kv * el)
        for n, (c, h) in enumerate(units):
            w = uw[n, :, HEAD_DIM:]
            qe_ref[rs(c), cs(h)] = qe[n]
            kel_ref[rs(c), cs(h)] = kel[n]
            wb_ref[rs(c), cs(h)] = _bf(w)
            w_ref[rs(c), cs(h)] = w
            u_ref[rs(c), cs(h)] = uw[n, :, :HEAD_DIM]
            qk_ref[h, rs(c), :] = qk[n]
            tm_ref[h, rs(c), :] = tm[n]
            gl_ref[c * 8:(c + 1) * 8, cs(h)] = jnp.broadcast_to(gl[n], (8, HEAD_DIM))

    big = lambda dt: jax.ShapeDtypeStruct((t, inner), dt)
    return _grid_call(
        body, name=name, grid=(nh // hb, t // rows),
        in_specs=[wide, wide, wide, gts],
        out_specs=[wide] * 5 + [sq, sq, glb],
        out_shape=[big(BF16), big(BF16), big(BF16), big(F32), big(F32),
                   jax.ShapeDtypeStruct((nh, t, CHUNK), BF16), jax.ShapeDtypeStruct((nh, t, CHUNK), F32),
                   jax.ShapeDtypeStruct((t // CHUNK * 8, inner), F32)],
        args=(q, k, v, gates), semantics=("parallel", "parallel"), comm=comm)


def _gdn_scan_fwd(qe, kel, wb, u, qk, glb, nh, name):
    t, inner = u.shape
    hb, nb, rows, wide, sq, _, glb_spec = _gdn_specs(nh, inner, t, SCAN_HB, SCAN_NB)

    def body(qe_ref, kel_ref, wb_ref, u_ref, qk_ref, gl_ref, o_ref, vn_ref, sall_ref, s_ref):
        @pl.when(pl.program_id(1) == 0)
        def _():
            s_ref[...] = jnp.zeros(s_ref.shape, F32)

        cs = lambda h: slice(h * HEAD_DIM, (h + 1) * HEAD_DIM)
        for c in range(nb):
            rs = slice(c * CHUNK, (c + 1) * CHUNK)
            heads = lambda ref: jnp.stack([ref[rs, cs(h)] for h in range(hb)])
            s = s_ref[...]
            sall_ref[c] = s
            sb = _bf(s)
            vn = heads(u_ref) - _dot(heads(wb_ref), sb, BNN)
            vnb = _bf(vn)
            o = _dot(heads(qe_ref), sb, BNN) + _dot(qk_ref[:, rs, :], vnb, BNN)
            gl = jnp.stack([gl_ref[c * 8:c * 8 + 1, cs(h)] for h in range(hb)])
            s_ref[...] = s * gl + _dot(heads(kel_ref), vnb, BTN)
            for h in range(hb):
                vn_ref[rs, cs(h)] = vn[h]
                o_ref[rs, cs(h)] = o[h]

    return pl.pallas_call(
        body, name=name, grid=(nh // hb, t // rows),
        in_specs=[wide, wide, wide, wide, sq, glb_spec],
        out_specs=[wide, wide, pl.BlockSpec((nb, hb, HEAD_DIM, HEAD_DIM), lambda g, n: (n, g, 0, 0))],
        out_shape=[jax.ShapeDtypeStruct((t, inner), F32), jax.ShapeDtypeStruct((t, inner), F32),
                   jax.ShapeDtypeStruct((t // CHUNK, nh, HEAD_DIM, HEAD_DIM), F32)],
        scratch_shapes=[pltpu.VMEM((hb, HEAD_DIM, HEAD_DIM), F32)],
        compiler_params=_cparams("parallel", "arbitrary"),
    )(qe, kel, wb, u, qk, glb)


def _gdn_scan_bwd(do, qe, kel, wb, vn, qk, glb, sall, nh, name, comm=None):
    t, inner = do.shape
    hb, nb, rows, _, _, _, _ = _gdn_specs(nh, inner, t, SCAN_HB, SCAN_NB)
    last = t // rows - 1
    wide = pl.BlockSpec((rows, hb * HEAD_DIM), lambda g, n: (last - n, g))
    sq = pl.BlockSpec((hb, rows, CHUNK), lambda g, n: (g, last - n, 0))
    glb_spec = pl.BlockSpec((nb * 8, hb * HEAD_DIM), lambda g, n: (last - n, g))

    def body(do_ref, qe_ref, kel_ref, wb_ref, vn_ref, qk_ref, gl_ref, sall_ref,
             dvn_ref, dw_ref, dqe_ref, dkel_ref, dqk_ref, dgl_ref, ds_ref):
        @pl.when(pl.program_id(1) == 0)
        def _():
            ds_ref[...] = jnp.zeros(ds_ref.shape, F32)

        cs = lambda h: slice(h * HEAD_DIM, (h + 1) * HEAD_DIM)
        for c in reversed(range(nb)):
            rs = slice(c * CHUNK, (c + 1) * CHUNK)
            heads = lambda ref: jnp.stack([ref[rs, cs(h)] for h in range(hb)])
            ds, s = ds_ref[...], sall_ref[c]
            dsb, sb = _bf(ds), _bf(s)
            dob, vnb = _bf(heads(do_ref)), _bf(heads(vn_ref))
            dvn = _dot(qk_ref[:, rs, :], dob, BTN) + _dot(heads(kel_ref), dsb, BNN)
            dvnb = _bf(dvn)
            dw = -_dot(dvnb, sb, BNT)
            dqe = _dot(dob, sb, BNT)
            dkel = _dot(vnb, dsb, BNT)
            dqk_ref[:, rs, :] = _dot(dob, vnb, BNT)
            dgl = jnp.sum(jnp.sum(ds * s, axis=2, keepdims=True), axis=1, keepdims=True)
            gl = jnp.stack([gl_ref[c * 8:c * 8 + 1, cs(h)] for h in range(hb)])
            ds_ref[...] = ds * gl + _dot(heads(qe_ref), dob, BTN) - _dot(heads(wb_ref), dvnb, BTN)
            for h in range(hb):
                dvn_ref[rs, cs(h)] = dvn[h]
                dw_ref[rs, cs(h)] = dw[h]
                dqe_ref[rs, cs(h)] = dqe[h]
                dkel_ref[rs, cs(h)] = dkel[h]
                dgl_ref[c * 8:(c + 1) * 8, cs(h)] = jnp.broadcast_to(dgl[h], (8, HEAD_DIM))

    big = jax.ShapeDtypeStruct((t, inner), F32)
    return _grid_call(
        body, name=name, grid=(nh // hb, t // rows),
        in_specs=[wide, wide, wide, wide, wide, sq, glb_spec,
                  pl.BlockSpec((nb, hb, HEAD_DIM, HEAD_DIM), lambda g, n: (last - n, g, 0, 0))],
        out_specs=[wide] * 4 + [sq, glb_spec],
        out_shape=[big] * 4 + [jax.ShapeDtypeStruct((nh, t, CHUNK), F32),
                               jax.ShapeDtypeStruct((t // CHUNK * 8, inner), F32)],
        scratch_shapes=[pltpu.VMEM((hb, HEAD_DIM, HEAD_DIM), F32)],
        args=(do, qe, kel, wb, vn, qk, glb, sall), semantics=("parallel", "arbitrary"), comm=comm)


def _gdn_intra_bwd(q, k, v, gates, tm, w, u, dvn, dw, dqe, dkel, dqk, dglb, nh, name, comm=None):
    t, inner = q.shape
    hb, nb, rows, wide, sq, gts, glb = _gdn_specs(nh, inner, t)

    def body(q_ref, k_ref, v_ref, g_ref, tm_ref, w_ref, u_ref, dvn_ref, dw_ref, dqe_ref, dkel_ref, dqk_ref,
             dgl_ref, dq_ref, dk_ref, dv_ref, dg_ref):
        first = pl.program_id(0) * hb
        selg, selb = _head_select(first, hb, 0), _head_select(first, hb, nh)
        i, j = _iota2(CHUNK, CHUNK)
        lane8 = lax.broadcasted_iota(jnp.int32, (CHUNK, 8), 1)
        row = lax.broadcasted_iota(jnp.int32, (CHUNK, 1), 0)
        lower = jnp.where(j <= i, 1.0, 0.0).astype(F32)
        rsum = lambda x: jnp.sum(x, axis=-1, keepdims=True)
        units, rs, cs, (qv, kv, vv, wv, uv, dvn, dw, dqe, dkel), gcol, grow, bcol = _unit_inputs(
            (q_ref, k_ref, v_ref, w_ref, u_ref, dvn_ref, dw_ref, dqe_ref, dkel_ref), g_ref, selg, selb, hb, nb)
        nu = len(units)
        tmv = jnp.stack([tm_ref[h, rs(c), :] for c, h in units])
        dqk = jnp.where(j <= i, jnp.stack([dqk_ref[h, rs(c), :] for c, h in units]), 0.0)
        dgl = jnp.stack([dgl_ref[c * 8:c * 8 + 1, h * HEAD_DIM:h * HEAD_DIM + 1] for c, h in units])
        e, el, gl, decay = _decay_terms(gcol, grow)
        kb = kv * bcol
        qb, kbf, kbb = _bf(qv), _bf(kv), _bf(kb)
        dqkr = _bf(dqk * decay)
        dq = dqe * e + _dot(dqkr, kbf, BNN)
        dk = dkel * el + _dot(dqkr, qb, BTN)
        de = rsum(dqe * qv)
        del_ = rsum(dkel * kv)
        mq = dqk * _dot(qb, kbf, BNT) * decay
        dsol = _dot3(tmv, jnp.concatenate([dvn, dw], axis=2), BTN)
        dvb, dkbe = dsol[:, :, :HEAD_DIM], dsol[:, :, HEAD_DIM:]
        da = -jnp.where(j < i, _dot3(dsol, jnp.concatenate([uv, wv], axis=2), BNT), 0.0)
        dkk = _bf(da * decay)
        ma = da * _dot(kbb, kbf, BNT) * decay
        dkb = dkbe * e + _dot(dkk, kbf, BNN)
        de = de + rsum(dkbe * kb)
        dk = dk + _dot(dkk, kbb, BTN) + dkb * bcol
        dv = dvb * bcol
        dbeta = rsum(dkb * kv) + rsum(dvb * vv)
        m = mq + ma
        ones = jnp.ones((nu, CHUNK, LANES), F32)
        dgc = rsum(m) - _dot_exact(m, ones, BTN, False)[:, :, 0:1] + de * e - del_ * el
        tail = jnp.sum(del_ * el, axis=1, keepdims=True) + dgl * gl
        dgc = dgc + jnp.where(row == CHUNK - 1, tail, 0.0)
        for n, (c, h) in enumerate(units):
            dq_ref[rs(c), cs(h)] = dq[n]
            dk_ref[rs(c), cs(h)] = dk[n]
            dv_ref[rs(c), cs(h)] = dv[n]
        for c in range(nb):
            dgc_cols = jnp.zeros((CHUNK, 8), F32)
            dbeta_cols = jnp.zeros((CHUNK, 8), F32)
            for h in range(hb):
                dgc_cols = jnp.where(lane8 == h, dgc[c * hb + h], dgc_cols)
                dbeta_cols = jnp.where(lane8 == h, dbeta[c * hb + h], dbeta_cols)
            dg_cols = _dot_exact(lower, dgc_cols, TN, True)
            dg_ref[rs(c), :] = _dot_exact(dg_cols, selg, NN, False) + _dot_exact(dbeta_cols, selb, NN, False)

    big = jax.ShapeDtypeStruct((t, inner), F32)
    return _grid_call(
        body, name=name, grid=(nh // hb, t // rows),
        in_specs=[wide, wide, wide, gts, sq, wide, wide, wide, wide, wide, wide, sq, glb],
        out_specs=[wide, wide, wide, pl.BlockSpec((None, rows, LANES), lambda g, n: (g, n, 0))],
        out_shape=[big, big, big, jax.ShapeDtypeStruct((nh // hb, t, LANES), F32)],
        args=(q, k, v, gates, tm, w, u, dvn, dw, dqe, dkel, dqk, dglb), semantics=("parallel", "parallel"),
        comm=comm)


def _layer_a_fwd(x, hn, w_in_t, w_ab_t, conv_w, alog_row, dt_row, onw, nh, comm, w_out_of):
    t, d = x.shape
    inner = nh * HEAD_DIM
    proj = _mm(hn, w_in_t, "nt", t, 4 * inner, d, out_dtype=F32, name="a_proj")
    ab = _mm(hn, w_ab_t, "nt", t, LANES, d, out_dtype=F32, name="a_proj_ab")
    gates = _gates_fwd(ab, alog_row, dt_row, nh, "a_gates")
    q = _conv_fwd(proj, conv_w, 0, inner, "q", "a_conv_q")
    k = _conv_fwd(proj, conv_w, inner, inner, "k", "a_conv_k")
    v = _conv_fwd(proj, conv_w, 2 * inner, inner, "v", "a_conv_v")
    qe, kel, wb, w, u, qk, tm, glb, *carried = _gdn_intra_fwd(q, k, v, gates, nh, "a_intra", comm)
    o, vn, sall = _gdn_scan_fwd(qe, kel, wb, u, qk, glb, nh, "a_scan")
    g = _gate_fwd(o, proj, 3 * inner, inner, "a_gate", norm_w=onw)
    w_out = w_out_of(carried)
    h1 = _mm(g, w_out, "nn", t, d, inner, out_dtype=F32, name="a_out", res=x)
    return h1, (hn, proj, ab, gates, q, k, v, qe, kel, wb, w, u, qk, tm, glb, o, vn, sall, g), w_out, carried


def _layer_a_bwd(dh1, dh1b, x, nw, w_in_t, w_ab_t, conv_w, alog_row, dt_row, onw, w_out, nh, saved, comms_of,
                 own_comm):
    hn, proj, ab, gates, q, k, v, qe, kel, wb, w, u, qk, tm, glb, o, vn, sall, g = saved
    t, d = x.shape
    inner = w_out.shape[0]
    dg = _mm(dh1b, w_out, "nt", t, inner, d, out_dtype=F32, name="a_dgate")
    dw_out = _mm(g, dh1b, "tn", inner, d, t, out_dtype=BF16, name="a_dwout")
    comm_scan, comm_intra = comms_of(dw_out)
    do, dproj, donw = _gate_bwd(dg, o, proj, 3 * inner, inner, "a_gate_bwd", F32, norm_w=onw,
                                dest=(None, 3 * inner, 4 * inner))
    dvn, dw, dqe, dkel, dqk, dglb, *carried_scan = _gdn_scan_bwd(do, qe, kel, wb, vn, qk, glb, sall, nh,
                                                                 "a_scan_bwd", comm_scan)
    dq, dk, dv, dgates, *carried = _gdn_intra_bwd(q, k, v, gates, tm, w, u, dvn, dw, dqe, dkel, dqk, dglb, nh,
                                                  "a_intra_bwd", comm_intra)
    carried = carried_scan + carried
    dproj, dcq = _conv_bwd(dq, proj, conv_w, 0, inner, "q", "a_conv_q_bwd", dest=(dproj, 0, 4 * inner))
    dproj, dck = _conv_bwd(dk, proj, conv_w, inner, inner, "k", "a_conv_k_bwd", dest=(dproj, inner, 4 * inner))
    dproj, dcv = _conv_bwd(dv, proj, conv_w, 2 * inner, inner, "v", "a_conv_v_bwd",
                           dest=(dproj, 2 * inner, 4 * inner))
    dab, dsmall = _gates_bwd(ab, alog_row, dt_row, dgates, nh, "a_gates_bwd")
    dw_in_t = _mm(dproj, hn, "tn", 4 * inner, d, t, out_dtype=BF16, name="a_dwin")
    dw_ab_t = _mm(dab, hn, "tn", LANES, d, t, out_dtype=BF16, name="a_dwin_ab")
    dconv = jnp.concatenate([dcq[:CONV_K], dck[:CONV_K], dcv[:CONV_K]], axis=1)
    dhn = _mm(dab, w_ab_t, "nn", t, d, LANES, out_dtype=F32, name="a_dhn_ab")
    own = own_comm(dw_in_t, dw_ab_t, dconv)
    dhn = _mm(dproj, w_in_t, "nn", t, d, 4 * inner, out_dtype=F32, name="a_dhn", res=dhn, comm=own)
    dhn, carried_own = dhn if own is not None else (dhn, [])
    dx, _, dnw = _rms_bwd(x, nw, dhn, dh1, "a_rms_bwd")
    return dx, dnw, dsmall, donw, carried, carried_own


def _rows_of(a, rows):
    flat = a.reshape(-1)
    return jnp.pad(flat, (0, rows * LANES - flat.shape[0])).reshape(rows, LANES)


def _to_slabs(g, axis):
    shape = g.shape[:axis] + (N_DEV, g.shape[axis] // N_DEV) + g.shape[axis + 1:]
    return jnp.moveaxis(g.reshape(shape), axis, 0)


def _from_slabs(s, axis):
    m = jnp.moveaxis(s, 0, axis)
    return m.reshape(m.shape[:axis] + (m.shape[axis] * m.shape[axis + 1],) + m.shape[axis + 2:])


def kernel(x, norm_w, a_w_in, a_conv_w, a_a_log, a_dt_bias, a_out_norm_w, a_w_out, b_w_in, b_q_norm_w, b_k_norm_w, b_rel_bias, b_w_out, loss_target, m_norm_w, m_a_w_in, m_a_conv_w, m_a_a_log, m_a_dt_bias, m_a_out_norm_w, m_a_w_out, m_b_w_in, m_b_q_norm_w, m_b_k_norm_w, m_b_rel_bias, m_b_w_out, v_norm_w, v_a_w_in, v_a_conv_w, v_a_a_log, v_a_dt_bias, v_a_out_norm_w, v_a_w_out, v_b_w_in, v_b_q_norm_w, v_b_k_norm_w, v_b_rel_bias, v_b_w_out):
    xs, target = x[0], loss_target[0]
    nh = a_a_log.shape[-1]
    inner = N_DEV * a_w_out.shape[1]

    d = xs.shape[1]
    nw0, nw1 = norm_w[0:1], norm_w[1:2]
    hn0, (ga_in, g_conv) = _rms_fwd(
        xs, nw0, "a_rms", comm=_RoutedGather([a_w_in[0].T.astype(BF16), a_conv_w[0]]))
    wa_in_t = ga_in.reshape(-1, d)
    wa_ab_t = jnp.pad(wa_in_t[4 * inner:], ((0, LANES - 2 * nh), (0, 0)))
    conv_w = _from_slabs(g_conv, 1)
    alog_row = jnp.pad(a_a_log, ((0, 0), (0, LANES - nh)))
    dt_row = jnp.pad(a_dt_bias, ((0, 0), (0, LANES - nh)))

    h1, saved_a, wa_out, (gb_in, gb_out, _) = _layer_a_fwd(
        xs, hn0, wa_in_t, wa_ab_t, conv_w, alog_row, dt_row, a_out_norm_w, nh,
        _Comm("gather", [b_w_in[0].T.astype(BF16), b_w_out[0].astype(BF16), a_w_out[0].astype(BF16)]),
        lambda gathered: _from_slabs(gathered[2], 0))
    wb_in_t = gb_in.reshape(-1, d)
    wb_out = _from_slabs(gb_out, 0)
    bias = _bias_tiles(_pad_rel_bias(b_rel_bias[0]), "b_bias_tiles")
    (dh2, dh2b, loss_row), saved_b = _layer_b_fwd(h1, nw1, wb_in_t, b_q_norm_w, b_k_norm_w, bias, wb_out, target)

    dh1, dh1b, dnw1, dwb_in_t, dqw, dkw, drb, dwb_out = _layer_b_bwd(
        dh2, dh2b, h1, nw1, wb_in_t, b_q_norm_w, b_k_norm_w, bias, wb_out, saved_b)

    def exchange_early(dwa_out):
        return (_Comm("exchange", [_to_slabs(dwb_out, 0).astype(BF16), _to_slabs(dwa_out, 0).astype(BF16)]),
                _Comm("exchange", [dwb_in_t.reshape(N_DEV, -1, d).astype(BF16)]))

    def exchange_last(dwa_in_t, dwa_ab_t, dconv):
        full = jnp.concatenate([dwa_in_t, dwa_ab_t[:2 * nh]], axis=0)
        return _Comm("exchange", [full.reshape(N_DEV, -1, d).astype(BF16), _to_slabs(dconv, 1)])

    dx, dnw0, dsmall, donw, (pb_out, pa_out, pb_in), (pa_in, p_conv) = _layer_a_bwd(
        dh1, dh1b, xs, nw0, wa_in_t, wa_ab_t, conv_w, alog_row, dt_row, a_out_norm_w, wa_out, nh, saved_a,
        exchange_early, exchange_last)
    big = {}
    for name, p, w, m, v in (("a_w_in", pa_in, a_w_in, m_a_w_in, v_a_w_in),
                             ("a_w_out", pa_out, a_w_out, m_a_w_out, v_a_w_out),
                             ("b_w_in", pb_in, b_w_in, m_b_w_in, v_b_w_in),
                             ("b_w_out", pb_out, b_w_out, m_b_w_out, v_b_w_out),
                             ("a_conv_w", p_conv, a_conv_w, m_a_conv_w, v_a_conv_w)):
        big[name] = [o[None] for o in _adamw(p, w[0], m[0], v[0], "adamw_" + name,
                                             transposed=name in ("a_w_in", "b_w_in"))]

    small = (("norm_w", norm_w, m_norm_w, v_norm_w, jnp.concatenate([dnw0, dnw1], axis=0)),
             ("a_a_log", a_a_log, m_a_a_log, v_a_a_log, dsmall[0:1, :nh]),
             ("a_dt_bias", a_dt_bias, m_a_dt_bias, v_a_dt_bias, dsmall[1:2, :nh]),
             ("a_out_norm_w", a_out_norm_w, m_a_out_norm_w, v_a_out_norm_w, donw),
             ("b_q_norm_w", b_q_norm_w, m_b_q_norm_w, v_b_q_norm_w, dqw),
             ("b_k_norm_w", b_k_norm_w, m_b_k_norm_w, v_b_k_norm_w, dkw),
             ("b_rel_bias", b_rel_bias, m_b_rel_bias, v_b_rel_bias, drb))
    rows = [8 * (-(-w.size // (8 * LANES))) for _, w, _, _, _ in small]
    pack = lambda arrs: jnp.concatenate([_rows_of(a, r) for a, r in zip(arrs, rows)] + [jnp.zeros((8, LANES), F32)], axis=0)
    g_pack = jnp.concatenate([_rows_of(g, r) for (_, _, _, _, g), r in zip(small, rows)]
                             + [jnp.broadcast_to(loss_row, (8, LANES))], axis=0)
    (g_all,) = _comm_call(_Comm("gather", [g_pack]), "gather_small_grads")
    outs_small = _adamw(g_all, pack([s[1] for s in small]), pack([s[2] for s in small]),
                        pack([s[3] for s in small]), "adamw_small")
    start = 0
    for (name, w, _, _, _), r in zip(small, rows):
        big[name] = [o[start:start + r].reshape(-1)[:w.size].reshape(w.shape) for o in outs_small]
        start += r
    loss = outs_small[0][start, 0]

    order = ("norm_w", "a_w_in", "a_conv_w", "a_a_log", "a_dt_bias", "a_out_norm_w", "a_w_out", "b_w_in",
             "b_q_norm_w", "b_k_norm_w", "b_rel_bias", "b_w_out")
    return (loss, dx[None]) + tuple(big[n][i] for i in range(4) for n in order)
```

```python
import functools

import jax
import jax.numpy as jnp
from jax import lax
from jax.experimental import pallas as pl
from jax.experimental.pallas import tpu as pltpu

F32 = jnp.float32
BF16 = jnp.bfloat16
MESH_IDS = pl.DeviceIdType.MESH
N_DEV = 8
CHUNK = 64
HEAD_DIM = 128
EPS = 1e-6
CONV_K = 4
LEFT_CHUNKS = 8
REL_CLIP = 256
Q_TILE = LEFT_CHUNKS * CHUNK
ADAM_LR = 0.001
ADAM_B1 = 0.9
ADAM_B2 = 0.999
ADAM_EPS = 1e-08
ADAM_WD = 0.01
ADAM_STEP = 10
NEG_BIG = -1e30
VMEM_LIMIT_BYTES = 56 * 1024 * 1024
HIGHEST = lax.Precision.HIGHEST
ANY = pl.BlockSpec(memory_space=pl.ANY)


def _cparams(*sem):
    return pltpu.CompilerParams(dimension_semantics=tuple(sem), vmem_limit_bytes=VMEM_LIMIT_BYTES)


NN, NT, TN = (((1,), (0,)), ((), ())), (((1,), (1,)), ((), ())), (((0,), (0,)), ((), ()))
BNN, BNT, BTN = (((2,), (1,)), ((0,), (0,))), (((2,), (2,)), ((0,), (0,))), (((1,), (1,)), ((0,), (0,)))


def _dot(a, b, dims, precision=None):
    return lax.dot_general(a, b, dims, preferred_element_type=F32, precision=precision)


def _nn(a, b, precision=None):
    return _dot(a, b, NN, precision)


def _nt(a, b, precision=None):
    return _dot(a, b, NT, precision)


def _tn(a, b, precision=None):
    return _dot(a, b, TN, precision)


def _bf(x):
    return x.astype(BF16)


def _split(x, pieces=2):
    out = []
    for _ in range(pieces - 1):
        hi = x.astype(BF16)
        out.append(hi)
        x = x - hi.astype(F32)
    return out + [x.astype(BF16)]


def _dot3(a, b, dims):
    (ah, al), (bh, bl) = _split(a), _split(b)
    return _dot(ah, bh, dims) + (_dot(ah, bl, dims) + _dot(al, bh, dims))


def _dot_exact(a, b, dims, split_b):
    if split_b:
        a = a.astype(BF16)
        parts = [_dot(a, p, dims) for p in _split(b, 3)]
    else:
        b = b.astype(BF16)
        parts = [_dot(p, b, dims) for p in _split(a, 3)]
    return parts[0] + (parts[1] + parts[2])


def _sigmoid(x):
    return 0.5 * jnp.tanh(0.5 * x) + 0.5


def _silu(x):
    return x * _sigmoid(x)


def _dsilu(x):
    s = _sigmoid(x)
    return s * (1.0 + x * (1.0 - s))


def _my_pos():
    return lax.axis_index("x"), lax.axis_index("y"), lax.axis_index("c")


def _peers(x, y, c):
    def flip(v, f):
        return 1 - v if f else v

    return [(flip(x, kx), flip(y, ky), flip(c, kc)) for kx in (0, 1) for ky in (0, 1) for kc in (0, 1)][1:]


def _lin(p):
    return 4 * p[0] + 2 * p[1] + p[2]


class _Comm:
    def __init__(self, kind, arrays):
        self.kind, self.arrays, self.n = kind, list(arrays), len(arrays)

    def out_shape(self):
        lead = (N_DEV,) if self.kind == "gather" else ()
        return [jax.ShapeDtypeStruct(lead + a.shape, a.dtype) for a in self.arrays]

    def scratch(self):
        return [pltpu.SemaphoreType.DMA((7 * self.n,)), pltpu.SemaphoreType.DMA((7 * self.n,)),
                pltpu.SemaphoreType.DMA((self.n,))]

    def _copies(self, ins, outs, sems, arrivals):
        send_sems, recv_sems, local_sems = sems
        x, y, c = _my_pos()
        me = _lin((x, y, c))
        gather = self.kind == "gather"
        mine = [ins[t] if gather else ins[t].at[me] for t in range(self.n)]
        remote = []
        for k, peer in enumerate(_peers(x, y, c)):
            for t in range(self.n):
                if arrivals:
                    src, dst = mine[t], outs[t].at[_lin(peer)]
                else:
                    src, dst = (ins[t] if gather else ins[t].at[_lin(peer)]), outs[t].at[me]
                remote.append(pltpu.make_async_remote_copy(
                    src_ref=src, dst_ref=dst, send_sem=send_sems.at[k * self.n + t],
                    recv_sem=recv_sems.at[k * self.n + t], device_id=peer, device_id_type=MESH_IDS))
        if arrivals:
            return remote
        return [pltpu.make_async_copy(mine[t], outs[t].at[me], local_sems.at[t]) for t in range(self.n)], remote

    def start(self, ins, outs, sems):
        local, sends = self._copies(ins, outs, sems, False)
        for cp in local + sends:
            cp.start()

    def finish(self, ins, outs, sems):
        for cp in self._copies(ins, outs, sems, True):
            cp.wait_recv()
        local, sends = self._copies(ins, outs, sems, False)
        for cp in sends:
            cp.wait_send()
        for cp in local:
            cp.wait()


def _xor(a, b):
    return a + b - 2 * a * b


class _RoutedGather(_Comm):
    def __init__(self, arrays):
        super().__init__("gather", arrays)

    def _plan(self, outs, sems):
        send_sems, recv_sems, _ = sems
        x, y, c = _my_pos()
        sib, xn, yn, dg = (x, y, 1 - c), (1 - x, y, c), (x, 1 - y, c), (1 - x, 1 - y, c)
        via = (_xor(x, 1 - c), _xor(y, c), c)
        onto = (_xor(x, c), _xor(y, 1 - c), c)
        routes = [(None, sib, sib), (None, xn, xn), (None, yn, yn), (via, onto, dg),
                  (xn, sib, (1 - x, y, 1 - c)), (yn, sib, (x, 1 - y, 1 - c)), (dg, sib, (1 - x, 1 - y, 1 - c))]

        def copy(k, t, src, slot, target):
            return pltpu.make_async_remote_copy(
                src_ref=src, dst_ref=outs[t].at[slot], send_sem=send_sems.at[k * self.n + t],
                recv_sem=recv_sems.at[k * self.n + t], device_id=target, device_id_type=MESH_IDS)

        return (x, y, c), routes, copy

    def start(self, ins, outs, sems):
        me, routes, copy = self._plan(outs, sems)
        for t in range(self.n):
            pltpu.make_async_copy(ins[t], outs[t].at[_lin(me)], sems[2].at[t]).start()
            for k in range(3):
                copy(k, t, ins[t], _lin(me), routes[k][1]).start()

    def finish(self, ins, outs, sems):
        me, routes, copy = self._plan(outs, sems)

        def arrived(k):
            for t in range(self.n):
                copy(k, t, ins[t], _lin(routes[k][2]), me).wait_recv()

        def pass_on(k):
            for t in range(self.n):
                copy(k, t, outs[t].at[_lin(routes[k][0])], _lin(routes[k][0]), routes[k][1]).start()

        arrived(1)
        arrived(2)
        for k in (3, 4, 5):
            pass_on(k)
        arrived(3)
        pass_on(6)
        for k in (0, 4, 5, 6):
            arrived(k)
        for t in range(self.n):
            for k in range(7):
                src = ins[t] if k < 3 else outs[t].at[_lin(routes[k][0])]
                copy(k, t, src, _lin(me), routes[k][1]).wait_send()
            pltpu.make_async_copy(ins[t], outs[t].at[_lin(me)], sems[2].at[t]).wait()


def _comm_call(comm, name):
    n = comm.n

    def body(*refs):
        ins, outs, sems = refs[:n], refs[n:2 * n], refs[2 * n:]
        comm.start(ins, outs, sems)
        comm.finish(ins, outs, sems)

    return pl.pallas_call(
        body, name=name, out_shape=comm.out_shape(), in_specs=[ANY] * n, out_specs=[ANY] * n,
        scratch_shapes=comm.scratch(),
    )(*comm.arrays)


def _grid_call(body, *, name, grid, in_specs, out_specs, out_shape, args, scratch_shapes=(), semantics=None, comm=None):
    if comm is None:
        return pl.pallas_call(
            body, name=name, grid=grid, in_specs=in_specs, out_specs=out_specs, out_shape=out_shape,
            scratch_shapes=list(scratch_shapes), compiler_params=_cparams(*semantics),
        )(*args)
    n_in, n_out, n_sc, n = len(in_specs), len(out_specs), len(scratch_shapes), comm.n

    def full(*refs):
        ins, refs = refs[:n_in], refs[n_in:]
        cins, refs = refs[:n], refs[n:]
        outs, refs = refs[:n_out], refs[n_out:]
        couts, refs = refs[:n], refs[n:]
        scratch, sems = refs[:n_sc], refs[n_sc:]
        ids = [pl.program_id(a) for a in range(len(grid))]
        first = functools.reduce(jnp.logical_and, [i == 0 for i in ids])
        last = functools.reduce(jnp.logical_and, [i == g - 1 for i, g in zip(ids, grid)])

        @pl.when(first)
        def _():
            comm.start(cins, couts, sems)

        body(*ins, *outs, *scratch)

        @pl.when(last)
        def _():
            comm.finish(cins, couts, sems)

    return pl.pallas_call(
        full, name=name, grid=grid, in_specs=list(in_specs) + [ANY] * n, out_specs=list(out_specs) + [ANY] * n,
        out_shape=list(out_shape) + comm.out_shape(), scratch_shapes=list(scratch_shapes) + comm.scratch(),
        compiler_params=_cparams(*(["arbitrary"] * len(grid))),
    )(*(list(args) + comm.arrays))


def _mm(a, b, mode, m, n, k, *, out_dtype, name, tm=1024, tn=1024, tk=2048,
        a_m0=0, a_k0=0, b_n0=0, b_k0=0, res=None, comm=None, loss_target=None):
    tm, tn, tk = min(tm, m), min(tn, n), min(tk, k)
    nm, nn, nk = m // tm, n // tn, k // tk
    assert nm * tm == m and nn * tn == n and nk * tk == k
    am, ak, bn, bk = a_m0 // tm, a_k0 // tk, b_n0 // tn, b_k0 // tk
    assert am * tm == a_m0 and ak * tk == a_k0 and bn * tn == b_n0 and bk * tk == b_k0
    if mode == "tn":
        a_spec = pl.BlockSpec((tk, tm), lambda i, j, q: (q + ak, i + am))
        a_dims = (0,)
    else:
        a_spec = pl.BlockSpec((tm, tk), lambda i, j, q: (i + am, q + ak))
        a_dims = (1,)
    if mode == "nt":
        b_spec = pl.BlockSpec((tn, tk), lambda i, j, q: (j + bn, q + bk))
        b_dims = (1,)
    else:
        b_spec = pl.BlockSpec((tk, tn), lambda i, j, q: (q + bk, j + bn))
        b_dims = (0,)
    o_spec = pl.BlockSpec((tm, tn), lambda i, j, q: (i, j))
    has_res = res is not None
    has_loss = loss_target is not None
    n_in = 2 + has_res + has_loss
    n_out = 3 if has_loss else 1

    def body(*refs):
        a_ref, b_ref = refs[0], refs[1]
        res_ref = refs[2] if has_res else None
        o_ref = refs[n_in]
        p = _dot(a_ref[...], b_ref[...], ((a_dims, b_dims), ((), ())))

        def finish(total):
            if has_res:
                total = total + res_ref[...].astype(F32)
            if not has_loss:
                o_ref[...] = total.astype(out_dtype)
                return
            err = total - refs[n_in - 1][...]
            grad = err * (1.0 / n)
            o_ref[...] = grad
            refs[n_in + 1][...] = grad.astype(BF16)
            l_ref = refs[n_in + 2]
            part = jnp.zeros((1, LANES), F32) + 0.5 * jnp.sum(err * err) * (1.0 / n)
            first = (pl.program_id(0) == 0) & (pl.program_id(1) == 0)

            @pl.when(first)
            def _():
                l_ref[...] = part

            @pl.when(jnp.logical_not(first))
            def _():
                l_ref[...] += part

        if nk == 1:
            finish(p)
        else:
            acc_ref = refs[n_in + n_out]
            q = pl.program_id(2)

            @pl.when(q == 0)
            def _():
                acc_ref[...] = p

            @pl.when(q > 0)
            def _():
                acc_ref[...] += p

            @pl.when(q == nk - 1)
            def _():
                finish(acc_ref[...])

    extra_in = ([res] if has_res else []) + ([loss_target] if has_loss else [])
    if has_loss:
        return _grid_call(
            body, name=name, grid=(nm, nn, nk), in_specs=[a_spec, b_spec] + [o_spec] * len(extra_in),
            out_specs=[o_spec, o_spec, pl.BlockSpec((1, LANES), lambda i, j, q: (0, 0))],
            out_shape=[jax.ShapeDtypeStruct((m, n), F32), jax.ShapeDtypeStruct((m, n), BF16),
                       jax.ShapeDtypeStruct((1, LANES), F32)],
            scratch_shapes=[pltpu.VMEM((tm, tn), F32)] if nk > 1 else [],
            args=[a, b] + extra_in, semantics=("arbitrary", "arbitrary", "arbitrary"))
    out, *carried = _grid_call(
        body, name=name, grid=(nm, nn, nk),
        in_specs=[a_spec, b_spec] + [o_spec] * len(extra_in),
        out_specs=[o_spec], out_shape=[jax.ShapeDtypeStruct((m, n), out_dtype)],
        scratch_shapes=[pltpu.VMEM((tm, tn), F32)] if nk > 1 else [],
        args=[a, b] + extra_in, semantics=("parallel", "parallel", "arbitrary"), comm=comm)
    return out if comm is None else (out, carried)


def _rms_fwd(x, w, name, tr=512, comm=None):
    t, d = x.shape
    tr = min(tr, t)

    def body(x_ref, w_ref, o_ref):
        xv = x_ref[...]
        r = lax.rsqrt(jnp.mean(xv * xv, axis=-1, keepdims=True) + EPS)
        o_ref[...] = (xv * r * w_ref[...]).astype(BF16)

    out, *carried = _grid_call(
        body, name=name, grid=(t // tr,),
        in_specs=[pl.BlockSpec((tr, d), lambda i: (i, 0)), pl.BlockSpec((1, d), lambda i: (0, 0))],
        out_specs=[pl.BlockSpec((tr, d), lambda i: (i, 0))],
        out_shape=[jax.ShapeDtypeStruct((t, d), BF16)], args=(x, w), semantics=("parallel",), comm=comm)
    return out if comm is None else (out, carried)


def _rms_bwd(x, w, dy, dres, name, tr=512):
    t, d = x.shape
    tr = min(tr, t)

    def body(x_ref, w_ref, dy_ref, dres_ref, dx_ref, dxb_ref, dw_ref):
        xv = x_ref[...]
        dyv = dy_ref[...].astype(F32)
        r = lax.rsqrt(jnp.mean(xv * xv, axis=-1, keepdims=True) + EPS)
        gy = dyv * w_ref[...]
        proj = jnp.sum(gy * xv, axis=-1, keepdims=True) * (1.0 / d)
        dx = dres_ref[...] + r * gy - xv * (r * r * r) * proj
        dx_ref[...] = dx
        dxb_ref[...] = dx.astype(BF16)
        part = jnp.sum(dyv * xv * r, axis=0, keepdims=True)

        @pl.when(pl.program_id(0) == 0)
        def _():
            dw_ref[...] = part

        @pl.when(pl.program_id(0) > 0)
        def _():
            dw_ref[...] += part

    row = pl.BlockSpec((tr, d), lambda i: (i, 0))
    vec = pl.BlockSpec((1, d), lambda i: (0, 0))
    return pl.pallas_call(
        body, name=name, grid=(t // tr,),
        in_specs=[row, vec, row, row], out_specs=[row, row, vec],
        out_shape=[jax.ShapeDtypeStruct((t, d), F32), jax.ShapeDtypeStruct((t, d), BF16),
                   jax.ShapeDtypeStruct((1, d), F32)],
        compiler_params=_cparams("arbitrary"),
    )(x, w, dy, dres)


def _adamw(parts, w, m, v, name, tr=128, transposed=False):
    r, c = w.shape
    tr = tr if r % tr == 0 else r
    c1 = 1.0 - ADAM_B1 ** ADAM_STEP
    c2 = 1.0 - ADAM_B2 ** ADAM_STEP

    def body(p_ref, w_ref, m_ref, v_ref, g_ref, d_ref, nm_ref, nv_ref):
        g = p_ref[0].astype(F32)
        for s in range(1, N_DEV):
            g = g + p_ref[s].astype(F32)
        if transposed:
            i, j = lax.broadcasted_iota(jnp.int32, (tr, tr), 0), lax.broadcasted_iota(jnp.int32, (tr, tr), 1)
            g = _dot_exact(jnp.where(i == j, 1.0, 0.0), g, NT, True)
        nm = ADAM_B1 * m_ref[...] + (1.0 - ADAM_B1) * g
        nv = ADAM_B2 * v_ref[...] + (1.0 - ADAM_B2) * (g * g)
        m_hat = nm / c1
        v_hat = nv / c2
        g_ref[...] = g
        d_ref[...] = -ADAM_LR * (m_hat / (jnp.sqrt(v_hat) + ADAM_EPS) + ADAM_WD * w_ref[...])
        nm_ref[...] = nm
        nv_ref[...] = nv

    blk = pl.BlockSpec((tr, c), lambda i: (i, 0))
    p_spec = (pl.BlockSpec((N_DEV, c, tr), lambda i: (0, 0, i)) if transposed
              else pl.BlockSpec((N_DEV, tr, c), lambda i: (0, i, 0)))
    return pl.pallas_call(
        body, name=name, grid=(r // tr,),
        in_specs=[p_spec, blk, blk, blk],
        out_specs=[blk] * 4, out_shape=[jax.ShapeDtypeStruct((r, c), F32)] * 4,
        compiler_params=_cparams("parallel"),
    )(parts, w, m, v)


ROW_TILE, ROW_HEADS = 512, 16
CONV_ROWS, CONV_HEADS = 512, 8


def _window(dest, inner, n_in, out_index):
    if dest is None:
        return inner, 0, [], [], {}
    buf, col0, total = dest
    if buf is None:
        return total, col0, [], [], {}
    return total, col0, [buf], [ANY], {n_in: out_index}


def _skip_ref(body, at, count):
    return body if count == 0 else (lambda *refs: body(*refs[:at], *refs[at + count:]))


def _heads_of(x, nh):
    return [x[:, h * HEAD_DIM:(h + 1) * HEAD_DIM] for h in range(nh)]


def _headnorm_fwd(proj, w, col0, inner, name, tr=ROW_TILE, hb=ROW_HEADS):
    t = proj.shape[0]
    tr = min(tr, t)
    hb = min(hb, inner // HEAD_DIM)
    wc = hb * HEAD_DIM
    c0 = col0 // wc

    def body(x_ref, w_ref, o_ref):
        outs = []
        for xh in _heads_of(x_ref[...], hb):
            r = lax.rsqrt(jnp.mean(xh * xh, axis=-1, keepdims=True) + EPS)
            outs.append((xh * r * w_ref[...]).astype(BF16))
        o_ref[...] = jnp.concatenate(outs, axis=1)

    return pl.pallas_call(
        body, name=name, grid=(t // tr, inner // wc),
        in_specs=[pl.BlockSpec((tr, wc), lambda i, j: (i, j + c0)), pl.BlockSpec((1, HEAD_DIM), lambda i, j: (0, 0))],
        out_specs=pl.BlockSpec((tr, wc), lambda i, j: (i, j)),
        out_shape=jax.ShapeDtypeStruct((t, inner), BF16),
        compiler_params=_cparams("parallel", "parallel"),
    )(proj, w)


def _headnorm_bwd(dy, proj, w, col0, inner, name, tr=ROW_TILE, hb=ROW_HEADS, dest=None):
    t = proj.shape[0]
    tr = min(tr, t)
    hb = min(hb, inner // HEAD_DIM)
    wc = hb * HEAD_DIM
    c0 = col0 // wc
    width, out0, more, more_specs, aliases = _window(dest, inner, 3, 0)

    def body(dy_ref, x_ref, w_ref, dx_ref, dw_ref):
        outs = []
        part = jnp.zeros((1, HEAD_DIM), F32)
        for dyh, xh in zip(_heads_of(dy_ref[...], hb), _heads_of(x_ref[...], hb)):
            r = lax.rsqrt(jnp.mean(xh * xh, axis=-1, keepdims=True) + EPS)
            gy = dyh * w_ref[...]
            pr = jnp.sum(gy * xh, axis=-1, keepdims=True) * (1.0 / HEAD_DIM)
            outs.append((r * gy - xh * (r * r * r) * pr).astype(BF16))
            part = part + jnp.sum(dyh * xh * r, axis=0, keepdims=True)
        dx_ref[...] = jnp.concatenate(outs, axis=1)
        first = (pl.program_id(0) == 0) & (pl.program_id(1) == 0)

        @pl.when(first)
        def _():
            dw_ref[...] = part

        @pl.when(jnp.logical_not(first))
        def _():
            dw_ref[...] += part

    blk = pl.BlockSpec((tr, wc), lambda i, j: (i, j))
    return pl.pallas_call(
        _skip_ref(body, 3, len(more)), name=name, grid=(t // tr, inner // wc),
        in_specs=[blk, pl.BlockSpec((tr, wc), lambda i, j: (i, j + c0)),
                  pl.BlockSpec((1, HEAD_DIM), lambda i, j: (0, 0))] + more_specs,
        out_specs=[pl.BlockSpec((tr, wc), lambda i, j: (i, j + out0 // wc)),
                   pl.BlockSpec((1, HEAD_DIM), lambda i, j: (0, 0))],
        out_shape=[jax.ShapeDtypeStruct((t, width), BF16), jax.ShapeDtypeStruct((1, HEAD_DIM), F32)],
        input_output_aliases=aliases, compiler_params=_cparams("arbitrary", "arbitrary"),
    )(dy, proj, w, *more)


def _gate_fwd(o, proj, zcol0, inner, name, norm_w=None, tr=ROW_TILE, hb=ROW_HEADS):
    t = o.shape[0]
    tr = min(tr, t)
    hb = min(hb, inner // HEAD_DIM)
    wc = hb * HEAD_DIM
    c0 = zcol0 // wc
    has_w = norm_w is not None

    def body(*refs):
        o_ref, z_ref = refs[0], refs[1]
        out_ref = refs[2 + has_w]
        outs = []
        for oh, zh in zip(_heads_of(o_ref[...], hb), _heads_of(z_ref[...], hb)):
            if has_w:
                r = lax.rsqrt(jnp.mean(oh * oh, axis=-1, keepdims=True) + EPS)
                oh = oh * r * refs[2][...]
            outs.append((oh * _silu(zh)).astype(BF16))
        out_ref[...] = jnp.concatenate(outs, axis=1)

    blk = pl.BlockSpec((tr, wc), lambda i, j: (i, j))
    vec = pl.BlockSpec((1, HEAD_DIM), lambda i, j: (0, 0))
    return pl.pallas_call(
        body, name=name, grid=(t // tr, inner // wc),
        in_specs=[blk, pl.BlockSpec((tr, wc), lambda i, j: (i, j + c0))] + ([vec] if has_w else []),
        out_specs=blk, out_shape=jax.ShapeDtypeStruct((t, inner), BF16),
        compiler_params=_cparams("parallel", "parallel"),
    )(*([o, proj] + ([norm_w] if has_w else [])))


def _gate_bwd(dg, o, proj, zcol0, inner, name, do_dtype, norm_w=None, tr=ROW_TILE, hb=ROW_HEADS, dest=None):
    t = o.shape[0]
    tr = min(tr, t)
    hb = min(hb, inner // HEAD_DIM)
    wc = hb * HEAD_DIM
    c0 = zcol0 // wc
    has_w = norm_w is not None
    width, out0, more, more_specs, aliases = _window(dest, inner, 3 + has_w, 1)

    def body(*refs):
        dg_ref, o_ref, z_ref = refs[0], refs[1], refs[2]
        do_ref, dz_ref = refs[3 + has_w], refs[4 + has_w]
        dos, dzs = [], []
        part = jnp.zeros((1, HEAD_DIM), F32)
        for dgh, oh, zh in zip(_heads_of(dg_ref[...], hb), _heads_of(o_ref[...], hb), _heads_of(z_ref[...], hb)):
            dy = dgh * _silu(zh)
            if has_w:
                w = refs[3][...]
                r = lax.rsqrt(jnp.mean(oh * oh, axis=-1, keepdims=True) + EPS)
                on = oh * r
                dzs.append((dgh * on * w * _dsilu(zh)).astype(BF16))
                gy = dy * w
                pr = jnp.sum(gy * oh, axis=-1, keepdims=True) * (1.0 / HEAD_DIM)
                dos.append((r * gy - oh * (r * r * r) * pr).astype(do_dtype))
                part = part + jnp.sum(dy * on, axis=0, keepdims=True)
            else:
                dzs.append((dgh * oh * _dsilu(zh)).astype(BF16))
                dos.append(dy.astype(do_dtype))
        do_ref[...] = jnp.concatenate(dos, axis=1)
        dz_ref[...] = jnp.concatenate(dzs, axis=1)
        if has_w:
            dw_ref = refs[6]
            first = (pl.program_id(0) == 0) & (pl.program_id(1) == 0)

            @pl.when(first)
            def _():
                dw_ref[...] = part

            @pl.when(jnp.logical_not(first))
            def _():
                dw_ref[...] += part

    blk = pl.BlockSpec((tr, wc), lambda i, j: (i, j))
    vec = pl.BlockSpec((1, HEAD_DIM), lambda i, j: (0, 0))
    return pl.pallas_call(
        _skip_ref(body, 3 + has_w, len(more)), name=name, grid=(t // tr, inner // wc),
        in_specs=[blk, blk, pl.BlockSpec((tr, wc), lambda i, j: (i, j + c0))] + ([vec] if has_w else []) + more_specs,
        out_specs=[blk, pl.BlockSpec((tr, wc), lambda i, j: (i, j + out0 // wc))] + ([vec] if has_w else []),
        out_shape=[jax.ShapeDtypeStruct((t, inner), do_dtype), jax.ShapeDtypeStruct((t, width), BF16)]
        + ([jax.ShapeDtypeStruct((1, HEAD_DIM), F32)] if has_w else []),
        input_output_aliases=aliases, compiler_params=_cparams("arbitrary", "arbitrary"),
    )(*([dg, o, proj] + ([norm_w] if has_w else []) + more))


N_REL = 2 * REL_CLIP + 1
REL_PAD = 640
WIN = 2 * Q_TILE


def _diag_onehot():
    i = lax.broadcasted_iota(jnp.int32, (REL_PAD, WIN), 0)
    j = lax.broadcasted_iota(jnp.int32, (REL_PAD, WIN), 1)
    rel = jnp.where(j < Q_TILE + CHUNK, Q_TILE - j, Q_TILE + WIN - j)
    used = (j < Q_TILE + CHUNK) | (j > WIN - CHUNK)
    idx = jnp.clip(rel, -REL_CLIP, REL_CLIP) + REL_CLIP
    return jnp.where(used & (i == idx), 1.0, 0.0).astype(F32)


def _band_mask():
    r = lax.broadcasted_iota(jnp.int32, (Q_TILE, WIN), 0) // CHUNK
    kc = lax.broadcasted_iota(jnp.int32, (Q_TILE, WIN), 1) // CHUNK - LEFT_CHUNKS
    return (kc <= r) & (kc >= r - LEFT_CHUNKS)


def _bias_tiles(rel_bias_pad, name):
    nh = rel_bias_pad.shape[0]

    def body(rb_ref, o_ref):
        dvec = _nn(rb_ref[...], _diag_onehot(), HIGHEST)[0:1, :]
        tile = pltpu.roll(jnp.broadcast_to(dvec, (Q_TILE, WIN)), 0, 1, stride=1, stride_axis=0)
        o_ref[...] = jnp.where(_band_mask(), tile, NEG_BIG)

    return pl.pallas_call(
        body, name=name, grid=(nh,),
        in_specs=[pl.BlockSpec((None, 8, REL_PAD), lambda h: (h, 0, 0))],
        out_specs=pl.BlockSpec((None, Q_TILE, WIN), lambda h: (h, 0, 0)),
        out_shape=jax.ShapeDtypeStruct((nh, Q_TILE, WIN), F32),
        compiler_params=_cparams("parallel"),
    )(rel_bias_pad)


def _bias_grad(dtile, name):
    nh = dtile.shape[0]

    def body(d_ref, o_ref):
        ri = lax.broadcasted_iota(jnp.int32, (Q_TILE, Q_TILE), 0)
        ci = lax.broadcasted_iota(jnp.int32, (Q_TILE, Q_TILE), 1)
        flip = jnp.where(ri + ci == Q_TILE - 1, 1.0, 0.0).astype(F32)
        rev = _dot_exact(flip, d_ref[...], NN, True)
        rolled = pltpu.roll(rev, WIN - (Q_TILE - 1), 1, stride=1, stride_axis=0)
        diag = jnp.broadcast_to(jnp.sum(rolled, axis=0, keepdims=True), (8, WIN))
        o_ref[...] = _nt(diag, _diag_onehot(), HIGHEST)

    return pl.pallas_call(
        body, name=name, grid=(nh,),
        in_specs=[pl.BlockSpec((None, Q_TILE, WIN), lambda h: (h, 0, 0))],
        out_specs=pl.BlockSpec((None, 8, REL_PAD), lambda h: (h, 0, 0)),
        out_shape=jax.ShapeDtypeStruct((nh, 8, REL_PAD), F32),
        compiler_params=_cparams("parallel"),
    )(dtile)


GROUP = 2 * CHUNK
BAND = Q_TILE + GROUP


N_GROUPS = Q_TILE // GROUP
ATTN_HB = 2
ATTN_HB_FWD = 4


def _head_cols(j):
    return slice(j * HEAD_DIM, (j + 1) * HEAD_DIM)


def _groups(ref, units):
    return jnp.stack([ref[GROUP * g:GROUP * (g + 1), _head_cols(j)] for j, g in units])


def _bands(r0_ref, r1_ref, units):
    return jnp.stack([jnp.concatenate([r0_ref[GROUP * g:, _head_cols(j)], r1_ref[:GROUP * (g + 1), _head_cols(j)]],
                                      axis=0) for j, g in units])


def _group_probs(q, kw, b_ref, units, first_tile):
    bias = jnp.stack([b_ref[j, GROUP * g:GROUP * (g + 1), GROUP * g:GROUP * g + BAND] for j, g in units])
    s = _dot(q, kw, BNT) * (HEAD_DIM ** -0.5) + bias
    col = jnp.stack([lax.broadcasted_iota(jnp.int32, (GROUP, BAND), 1) + GROUP * g for _, g in units])
    s = jnp.where(first_tile & (col < Q_TILE), NEG_BIG, s)
    p = jnp.exp(s - jnp.max(s, axis=-1, keepdims=True))
    return p * (1.0 / jnp.sum(p, axis=-1, keepdims=True))


def _attn_fwd(q, k, v, v_col0, bias, name):
    t, inner = q.shape
    nh, nt = inner // HEAD_DIM, t // Q_TILE
    hb = min(ATTN_HB_FWD, nh)
    wc = hb * HEAD_DIM
    vh = v_col0 // wc
    units = [(j, g) for j in range(hb) for g in range(N_GROUPS)]

    def body(q_ref, k0_ref, k1_ref, v0_ref, v1_ref, b_ref, o_ref):
        p = _group_probs(_groups(q_ref, units), _bands(k0_ref, k1_ref, units), b_ref, units, pl.program_id(1) == 0)
        o = _dot(_bf(p), _bf(_bands(v0_ref, v1_ref, units)), BNN)
        for n, (j, g) in enumerate(units):
            o_ref[GROUP * g:GROUP * (g + 1), _head_cols(j)] = o[n]

    cur = pl.BlockSpec((Q_TILE, wc), lambda h, i: (i, h))
    prev = pl.BlockSpec((Q_TILE, wc), lambda h, i: (jnp.maximum(i - 1, 0), h))
    v_cur = pl.BlockSpec((Q_TILE, wc), lambda h, i: (i, h + vh))
    v_prev = pl.BlockSpec((Q_TILE, wc), lambda h, i: (jnp.maximum(i - 1, 0), h + vh))
    return pl.pallas_call(
        body, name=name, grid=(nh // hb, nt),
        in_specs=[cur, prev, cur, v_prev, v_cur, pl.BlockSpec((hb, Q_TILE, WIN), lambda h, i: (h, 0, 0))],
        out_specs=cur, out_shape=jax.ShapeDtypeStruct((t, inner), F32),
        compiler_params=_cparams("parallel", "parallel"),
    )(q, k, k, v, v, bias)


def _attn_bwd(q, k, v, v_col0, do, bias, name, dest=None):
    t, inner = q.shape
    nh, nt = inner // HEAD_DIM, t // Q_TILE
    scale = HEAD_DIM ** -0.5
    hb = min(ATTN_HB, nh)
    wc = hb * HEAD_DIM
    units = [(j, g) for j in range(hb) for g in range(N_GROUPS)]

    def body(q_ref, k0_ref, k1_ref, v0_ref, v1_ref, do_ref, b_ref, dq_ref, dk_ref, dv_ref, db_ref,
             ck_ref, cv_ref, wk_ref, wv_ref):
        i = pl.program_id(1)

        @pl.when(i == 0)
        def _():
            ck_ref[...] = jnp.zeros(blk, F32)
            cv_ref[...] = jnp.zeros(blk, F32)
            db_ref[...] = jnp.zeros((hb, Q_TILE, WIN), F32)

        @pl.when(i < nt)
        def _():
            wk_ref[...] = jnp.zeros((WIN, wc), F32)
            wv_ref[...] = jnp.zeros((WIN, wc), F32)
            qv, dov = _groups(q_ref, units), _groups(do_ref, units)
            kw, vw = _bands(k0_ref, k1_ref, units), _bf(_bands(v0_ref, v1_ref, units))
            p = _group_probs(qv, kw, b_ref, units, i == 0)
            dp = _dot(dov, vw, BNT)
            ds = p * (dp - jnp.sum(p * dp, axis=-1, keepdims=True))
            pb, dsb = _bf(p), _bf(ds)
            dq = _dot(dsb, kw, BNN) * scale
            dkw = _dot(dsb, qv, BTN) * scale
            dvw = _dot(pb, dov, BTN)
            for n, (j, g) in enumerate(units):
                rows, cols = slice(GROUP * g, GROUP * (g + 1)), slice(GROUP * g, GROUP * g + BAND)
                db_ref[j, rows, cols] += ds[n]
                dq_ref[rows, _head_cols(j)] = dq[n]
                wk_ref[cols, _head_cols(j)] += dkw[n]
                wv_ref[cols, _head_cols(j)] += dvw[n]
            dk_ref[...] = ck_ref[...] + wk_ref[:Q_TILE, :]
            dv_ref[...] = (cv_ref[...] + wv_ref[:Q_TILE, :]).astype(BF16)
            ck_ref[...] = wk_ref[Q_TILE:, :]
            cv_ref[...] = wv_ref[Q_TILE:, :]

        @pl.when(i == nt)
        def _():
            dk_ref[...] = ck_ref[...]
            dv_ref[...] = cv_ref[...].astype(BF16)

    blk = (Q_TILE, wc)
    cur = pl.BlockSpec(blk, lambda h, i: (jnp.minimum(i, nt - 1), h))
    prev = pl.BlockSpec(blk, lambda h, i: (jnp.clip(i - 1, 0, nt - 1), h))
    lag = pl.BlockSpec(blk, lambda h, i: (jnp.maximum(i - 1, 0), h))
    vh = v_col0 // wc
    v_cur = pl.BlockSpec(blk, lambda h, i: (jnp.minimum(i, nt - 1), h + vh))
    v_prev = pl.BlockSpec(blk, lambda h, i: (jnp.clip(i - 1, 0, nt - 1), h + vh))
    tile = pl.BlockSpec((hb, Q_TILE, WIN), lambda h, i: (h, 0, 0))
    width, out0, more, more_specs, aliases = _window(dest, inner, 7, 2)
    return pl.pallas_call(
        _skip_ref(body, 7, len(more)), name=name, grid=(nh // hb, nt + 1),
        in_specs=[cur, prev, cur, v_prev, v_cur, cur, tile] + more_specs,
        out_specs=[cur, lag, pl.BlockSpec(blk, lambda h, i: (jnp.maximum(i - 1, 0), h + out0 // wc)), tile],
        out_shape=[jax.ShapeDtypeStruct((t, inner), F32)] * 2 + [jax.ShapeDtypeStruct((t, width), BF16),
                                                                 jax.ShapeDtypeStruct((nh, Q_TILE, WIN), F32)],
        scratch_shapes=[pltpu.VMEM(blk, F32), pltpu.VMEM(blk, F32),
                        pltpu.VMEM((WIN, wc), F32), pltpu.VMEM((WIN, wc), F32)],
        input_output_aliases=aliases, compiler_params=_cparams("arbitrary", "arbitrary"),
    )(q, k, k, v, v, do, bias, *more)


def _pad_rel_bias(rel_bias):
    nh = rel_bias.shape[0]
    return jnp.broadcast_to(jnp.pad(rel_bias, ((0, 0), (0, REL_PAD - N_REL)))[:, None, :], (nh, 8, REL_PAD))


def _layer_b_fwd(h1, nw, w_in_t, qw, kw, bias, w_out, target):
    t, d = h1.shape
    inner = w_out.shape[0]
    hn = _rms_fwd(h1, nw, "b_rms")
    proj = _mm(hn, w_in_t, "nt", t, 4 * inner, d, out_dtype=F32, name="b_proj")
    qn = _headnorm_fwd(proj, qw, 0, inner, "b_qnorm")
    kn = _headnorm_fwd(proj, kw, inner, inner, "b_knorm")
    o = _attn_fwd(qn, kn, proj, 2 * inner, bias, "b_attn")
    g = _gate_fwd(o, proj, 3 * inner, inner, "b_gate")
    loss_parts = _mm(g, w_out, "nn", t, d, inner, out_dtype=F32, name="b_out", res=h1, loss_target=target)
    return loss_parts, (hn, proj, qn, kn, o, g)


def _layer_b_bwd(dh2, dh2b, h1, nw, w_in_t, qw, kw, bias, w_out, saved):
    hn, proj, qn, kn, o, g = saved
    t, d = h1.shape
    inner = w_out.shape[0]
    dg = _mm(dh2b, w_out, "nt", t, inner, d, out_dtype=F32, name="b_dgate")
    dw_out = _mm(g, dh2b, "tn", inner, d, t, out_dtype=BF16, name="b_dwout")
    do, dproj = _gate_bwd(dg, o, proj, 3 * inner, inner, "b_gate_bwd", BF16, dest=(None, 3 * inner, 4 * inner))
    dq, dk, dproj, dtile = _attn_bwd(qn, kn, proj, 2 * inner, do, bias, "b_attn_bwd",
                                     dest=(dproj, 2 * inner, 4 * inner))
    dproj, dqw = _headnorm_bwd(dq, proj, qw, 0, inner, "b_qnorm_bwd", dest=(dproj, 0, 4 * inner))
    dproj, dkw = _headnorm_bwd(dk, proj, kw, inner, inner, "b_knorm_bwd", dest=(dproj, inner, 4 * inner))
    dhn = _mm(dproj, w_in_t, "nn", t, d, 4 * inner, out_dtype=F32, name="b_dhn")
    dw_in_t = _mm(dproj, hn, "tn", 4 * inner, d, t, out_dtype=BF16, name="b_dwin")
    dh1, dh1b, dnw = _rms_bwd(h1, nw, dhn, dh2, "b_rms_bwd")
    drb = _bias_grad(dtile, "b_bias_grad")[:, 0, :N_REL]
    return dh1, dh1b, dnw, dw_in_t, dqw, dkw, drb, dw_out


LANES = 128


def _softplus(x):
    return jnp.maximum(x, 0.0) + jnp.log1p(jnp.exp(-jnp.abs(x)))


def _gates_fwd(ab, alog_row, dt_row, nh, name, tr=1024):
    t = ab.shape[0]
    tr = min(tr, t)

    def body(x_ref, al_ref, dt_ref, o_ref):
        x = x_ref[...]
        lane = lax.broadcasted_iota(jnp.int32, x.shape, 1)
        g = -jnp.exp(al_ref[...]) * _softplus(x + dt_ref[...])
        o_ref[...] = jnp.where(lane < nh, g, jnp.where(lane < 2 * nh, _sigmoid(x), 0.0))

    row = pl.BlockSpec((tr, LANES), lambda i: (i, 0))
    vec = pl.BlockSpec((1, LANES), lambda i: (0, 0))
    return pl.pallas_call(
        body, name=name, grid=(t // tr,), in_specs=[row, vec, vec], out_specs=row,
        out_shape=jax.ShapeDtypeStruct((t, LANES), F32), compiler_params=_cparams("parallel"),
    )(ab, alog_row, dt_row)


def _gates_bwd(ab, alog_row, dt_row, dgates, nh, name, tr=1024):
    t = ab.shape[0]
    tr = min(tr, t)
    npart = dgates.shape[0]

    def body(x_ref, al_ref, dt_ref, dg_ref, dx_ref, s_ref):
        x = x_ref[...]
        lane = lax.broadcasted_iota(jnp.int32, x.shape, 1)
        dgt = dg_ref[0]
        for p in range(1, npart):
            dgt = dgt + dg_ref[p]
        ea = jnp.exp(al_ref[...])
        xa = x + dt_ref[...]
        da = jnp.where(lane < nh, dgt * (-ea) * _sigmoid(xa), 0.0)
        beta = _sigmoid(x)
        db = jnp.where((lane >= nh) & (lane < 2 * nh), dgt * beta * (1.0 - beta), 0.0)
        dx_ref[...] = (da + db).astype(BF16)
        dal = jnp.sum(jnp.where(lane < nh, dgt * (-ea) * _softplus(xa), 0.0), axis=0, keepdims=True)
        ddt = jnp.sum(da, axis=0, keepdims=True)
        r8 = lax.broadcasted_iota(jnp.int32, (8, LANES), 0)
        part = jnp.where(r8 == 0, dal, jnp.where(r8 == 1, ddt, 0.0))

        @pl.when(pl.program_id(0) == 0)
        def _():
            s_ref[...] = part

        @pl.when(pl.program_id(0) > 0)
        def _():
            s_ref[...] += part

    row = pl.BlockSpec((tr, LANES), lambda i: (i, 0))
    vec = pl.BlockSpec((1, LANES), lambda i: (0, 0))
    return pl.pallas_call(
        body, name=name, grid=(t // tr,),
        in_specs=[row, vec, vec, pl.BlockSpec((npart, tr, LANES), lambda i: (0, i, 0))],
        out_specs=[row, pl.BlockSpec((8, LANES), lambda i: (0, 0))],
        out_shape=[jax.ShapeDtypeStruct((t, LANES), BF16), jax.ShapeDtypeStruct((8, LANES), F32)],
        compiler_params=_cparams("arbitrary"),
    )(ab, alog_row, dt_row, dgates)


HALO = 8


def _delayed(ext, rows):
    return [ext[HALO:HALO + rows]] + [pltpu.roll(ext, s, 0)[HALO:HALO + rows] for s in range(1, CONV_K)]


def _conv_taps(delayed, w):
    acc = delayed[0] * w[CONV_K - 1:CONV_K]
    for s in range(1, CONV_K):
        acc = acc + delayed[s] * w[CONV_K - 1 - s:CONV_K - s]
    return acc


def _conv_fwd(proj, conv_w, col0, inner, mode, name, tt=CONV_ROWS, hb=CONV_HEADS):
    t = proj.shape[0]
    tt = min(tt, t)
    hb = min(hb, inner // HEAD_DIM)
    wc = hb * HEAD_DIM
    c0 = col0 // wc
    hpb = tt // HALO

    def body(x_ref, halo_ref, w_ref, o_ref):
        halo = jnp.where(pl.program_id(1) == 0, 0.0, halo_ref[...])
        s = _silu(_conv_taps(_delayed(jnp.concatenate([halo, x_ref[...]], axis=0), tt), w_ref[...]))
        if mode == "v":
            o_ref[...] = s
        else:
            mul = HEAD_DIM ** -0.5 if mode == "q" else 1.0
            o_ref[...] = jnp.concatenate(
                [sh * (lax.rsqrt(jnp.sum(sh * sh, axis=-1, keepdims=True) + EPS) * mul) for sh in _heads_of(s, hb)], axis=1)

    return pl.pallas_call(
        body, name=name, grid=(inner // wc, t // tt),
        in_specs=[pl.BlockSpec((tt, wc), lambda j, i: (i, j + c0)),
                  pl.BlockSpec((HALO, wc), lambda j, i: (jnp.maximum(i * hpb - 1, 0), j + c0)),
                  pl.BlockSpec((CONV_K, wc), lambda j, i: (0, j + c0))],
        out_specs=pl.BlockSpec((tt, wc), lambda j, i: (i, j)),
        out_shape=jax.ShapeDtypeStruct((t, inner), F32),
        compiler_params=_cparams("parallel", "parallel"),
    )(proj, proj, conv_w)


def _conv_bwd(dy, proj, conv_w, col0, inner, mode, name, tt=CONV_ROWS, hb=CONV_HEADS, dest=None):
    t = proj.shape[0]
    tt = min(tt, t)
    nt = t // tt
    hb = min(hb, inner // HEAD_DIM)
    wc = hb * HEAD_DIM
    c0 = col0 // wc
    hpb = tt // HALO
    rows = tt + HALO

    def body(dy_ref, dyn_ref, x_ref, xp_ref, xn_ref, w_ref, dx_ref, dw_ref):
        i = pl.program_id(1)
        w = w_ref[...]
        xprev = jnp.where(i == 0, 0.0, xp_ref[...])
        delayed = _delayed(jnp.concatenate([xprev, x_ref[...], xn_ref[...]], axis=0), rows)
        c = _conv_taps(delayed, w)
        dyv = jnp.concatenate([dy_ref[...], jnp.where(i == nt - 1, 0.0, dyn_ref[...])], axis=0)
        sg = _sigmoid(c)
        s = c * sg
        if mode == "v":
            ds = dyv
        else:
            mul = HEAD_DIM ** -0.5 if mode == "q" else 1.0
            parts = []
            for dyh, sh in zip(_heads_of(dyv, hb), _heads_of(s, hb)):
                r = lax.rsqrt(jnp.sum(sh * sh, axis=-1, keepdims=True) + EPS)
                parts.append(mul * (r * dyh - sh * (r * r * r) * jnp.sum(dyh * sh, axis=-1, keepdims=True)))
            ds = jnp.concatenate(parts, axis=1)
        dc = ds * (sg * (1.0 + c * (1.0 - sg)))
        dx = dc[:tt] * w[CONV_K - 1:CONV_K]
        for sft in range(1, CONV_K):
            dx = dx + pltpu.roll(dc, rows - sft, 0)[:tt] * w[CONV_K - 1 - sft:CONV_K - sft]
        dx_ref[...] = dx.astype(BF16)
        r8 = lax.broadcasted_iota(jnp.int32, (8, wc), 0)
        part = jnp.zeros((8, wc), F32)
        for sft in range(CONV_K):
            part = part + jnp.where(r8 == CONV_K - 1 - sft,
                                    jnp.sum(dc[:tt] * delayed[sft][:tt], axis=0, keepdims=True), 0.0)

        @pl.when(i == 0)
        def _():
            dw_ref[...] = part

        @pl.when(i > 0)
        def _():
            dw_ref[...] += part

    cur = lambda off: pl.BlockSpec((tt, wc), lambda j, i: (i, j + off))
    nxt = lambda off: pl.BlockSpec((HALO, wc), lambda j, i: (jnp.minimum((i + 1) * hpb, t // HALO - 1), j + off))
    width, out0, more, more_specs, aliases = _window(dest, inner, 6, 0)
    return pl.pallas_call(
        _skip_ref(body, 6, len(more)), name=name, grid=(inner // wc, nt),
        in_specs=[cur(0), nxt(0), cur(c0),
                  pl.BlockSpec((HALO, wc), lambda j, i: (jnp.maximum(i * hpb - 1, 0), j + c0)), nxt(c0),
                  pl.BlockSpec((CONV_K, wc), lambda j, i: (0, j + c0))] + more_specs,
        out_specs=[pl.BlockSpec((tt, wc), lambda j, i: (i, j + out0 // wc)),
                   pl.BlockSpec((8, wc), lambda j, i: (0, j))],
        out_shape=[jax.ShapeDtypeStruct((t, width), BF16), jax.ShapeDtypeStruct((8, inner), F32)],
        input_output_aliases=aliases, compiler_params=_cparams("parallel", "arbitrary"),
    )(dy, dy, proj, proj, proj, conv_w, *more)


GDN_HB = 4
GDN_NB = 8
SCAN_HB = 16
SCAN_NB = 4


def _iota2(n, m):
    return lax.broadcasted_iota(jnp.int32, (n, m), 0), lax.broadcasted_iota(jnp.int32, (n, m), 1)


def _head_select(first_head, hb, lane0):
    r, lane = _iota2(8, LANES)
    return jnp.where((r < hb) & (lane == lane0 + first_head + r), 1.0, 0.0).astype(F32)


def _chunk_gates(gt, selg, selb):
    i, j = _iota2(CHUNK, CHUNK)
    gc_all = _dot_exact(jnp.where(j <= i, 1.0, 0.0), gt, NN, True)
    return (_dot_exact(gc_all, selg, NT, False), _dot_exact(selg, gc_all, NT, True),
            _dot_exact(gt, selb, NT, False))


def _decay_terms(gcol, grow):
    i, j = _iota2(CHUNK, CHUNK)
    glast = gcol[:, CHUNK - 1:CHUNK, :]
    decay = jnp.exp(jnp.where(j <= i, gcol - grow, NEG_BIG))
    return jnp.exp(gcol), jnp.exp(glast - gcol), jnp.exp(glast), decay


def _unit_lower_inverse(a):
    i, j = _iota2(CHUNK, CHUNK)
    same16 = (i // 16) == (j // 16)
    same32 = (i // 32) == (j // 32)
    m = jnp.where(same16, -a, 0.0)
    x = jnp.where(i == j, 1.0, 0.0) + m
    for _ in range(3):
        m = _dot3(m, m, BNN)
        x = x + _dot3(x, m, BNN)
    for off in (jnp.where(same32 & jnp.logical_not(same16), a, 0.0), jnp.where(same32, 0.0, a)):
        x = x - _dot3(_dot3(x, off, BNN), x, BNN)
    return x


def _unit_inputs(refs, g_ref, selg, selb, hb, nb):
    units = [(c, h) for c in range(nb) for h in range(hb)]
    rs = lambda c: slice(c * CHUNK, (c + 1) * CHUNK)
    cs = lambda h: slice(h * HEAD_DIM, (h + 1) * HEAD_DIM)
    gates = [_chunk_gates(g_ref[rs(c), :], selg, selb) for c in range(nb)]
    stacked = [jnp.stack([r[rs(c), cs(h)] for c, h in units]) for r in refs]
    gcol = jnp.stack([gates[c][0][:, h:h + 1] for c, h in units])
    grow = jnp.stack([gates[c][1][h:h + 1, :] for c, h in units])
    bcol = jnp.stack([gates[c][2][:, h:h + 1] for c, h in units])
    return units, rs, cs, stacked, gcol, grow, bcol


def _gdn_specs(nh, inner, t, heads=GDN_HB, chunks=GDN_NB):
    hb, nb = min(heads, nh), chunks
    rows = nb * CHUNK
    wide = pl.BlockSpec((rows, hb * HEAD_DIM), lambda g, n: (n, g))
    sq = pl.BlockSpec((hb, rows, CHUNK), lambda g, n: (g, n, 0))
    gts = pl.BlockSpec((rows, LANES), lambda g, n: (n, 0))
    glb = pl.BlockSpec((nb * 8, hb * HEAD_DIM), lambda g, n: (n, g))
    return hb, nb, rows, wide, sq, gts, glb


def _gdn_intra_fwd(q, k, v, gates, nh, name, comm=None):
    t, inner = q.shape
    hb, nb, rows, wide, sq, gts, glb = _gdn_specs(nh, inner, t)

    def body(q_ref, k_ref, v_ref, g_ref, qe_ref, kel_ref, wb_ref, w_ref, u_ref, qk_ref, tm_ref, gl_ref):
        first = pl.program_id(0) * hb
        selg, selb = _head_select(first, hb, 0), _head_select(first, hb, nh)
        i, j = _iota2(CHUNK, CHUNK)
        units, rs, cs, (qv, kv, vv), gcol, grow, bcol = _unit_inputs(
            (q_ref, k_ref, v_ref), g_ref, selg, selb, hb, nb)
        e, el, gl, decay = _decay_terms(gcol, grow)
        kb = kv * bcol
        qbf, kbf = _bf(qv), _bf(kv)
        a = jnp.where(j < i, _dot(_bf(kb), kbf, BNT) * decay, 0.0)
        tm = _unit_lower_inverse(a)
        uw = _dot3(tm, jnp.concatenate([vv * bcol, kb * e], axis=2), BNN)
        qk = _bf(_dot(qbf, kbf, BNT) * decay)
        qe, kel = _bf(qv * e), _bf(kv * el)
        for n, (c, h) in enumerate(units):
            w = uw[n, :, HEAD_DIM:]
            qe_ref[rs(c), cs(h)] = qe[n]
            kel_ref[rs(c), cs(h)] = kel[n]
            wb_ref[rs(c), cs(h)] = _bf(w)
            w_ref[rs(c), cs(h)] = w
            u_ref[rs(c), cs(h)] = uw[n, :, :HEAD_DIM]
            qk_ref[h, rs(c), :] = qk[n]
            tm_ref[h, rs(c), :] = tm[n]
            gl_ref[c * 8:(c + 1) * 8, cs(h)] = jnp.broadcast_to(gl[n], (8, HEAD_DIM))

    big = lambda dt: jax.ShapeDtypeStruct((t, inner), dt)
    return _grid_call(
        body, name=name, grid=(nh // hb, t // rows),
        in_specs=[wide, wide, wide, gts],
        out_specs=[wide] * 5 + [sq, sq, glb],
        out_shape=[big(BF16), big(BF16), big(BF16), big(F32), big(F32),
                   jax.ShapeDtypeStruct((nh, t, CHUNK), BF16), jax.ShapeDtypeStruct((nh, t, CHUNK), F32),
                   jax.ShapeDtypeStruct((t // CHUNK * 8, inner), F32)],
        args=(q, k, v, gates), semantics=("parallel", "parallel"), comm=comm)


def _gdn_scan_fwd(qe, kel, wb, u, qk, glb, nh, name):
    t, inner = u.shape
    hb, nb, rows, wide, sq, _, glb_spec = _gdn_specs(nh, inner, t, SCAN_HB, SCAN_NB)

    def body(qe_ref, kel_ref, wb_ref, u_ref, qk_ref, gl_ref, o_ref, vn_ref, sall_ref, s_ref):
        @pl.when(pl.program_id(1) == 0)
        def _():
            s_ref[...] = jnp.zeros(s_ref.shape, F32)

        cs = lambda h: slice(h * HEAD_DIM, (h + 1) * HEAD_DIM)
        for c in range(nb):
            rs = slice(c * CHUNK, (c + 1) * CHUNK)
            heads = lambda ref: jnp.stack([ref[rs, cs(h)] for h in range(hb)])
            s = s_ref[...]
            sall_ref[c] = s
            sb = _bf(s)
            vn = heads(u_ref) - _dot(heads(wb_ref), sb, BNN)
            vnb = _bf(vn)
            o = _dot(heads(qe_ref), sb, BNN) + _dot(qk_ref[:, rs, :], vnb, BNN)
            gl = jnp.stack([gl_ref[c * 8:c * 8 + 1, cs(h)] for h in range(hb)])
            s_ref[...] = s * gl + _dot(heads(kel_ref), vnb, BTN)
            for h in range(hb):
                vn_ref[rs, cs(h)] = vn[h]
                o_ref[rs, cs(h)] = o[h]

    return pl.pallas_call(
        body, name=name, grid=(nh // hb, t // rows),
        in_specs=[wide, wide, wide, wide, sq, glb_spec],
        out_specs=[wide, wide, pl.BlockSpec((nb, hb, HEAD_DIM, HEAD_DIM), lambda g, n: (n, g, 0, 0))],
        out_shape=[jax.ShapeDtypeStruct((t, inner), F32), jax.ShapeDtypeStruct((t, inner), F32),
                   jax.ShapeDtypeStruct((t // CHUNK, nh, HEAD_DIM, HEAD_DIM), F32)],
        scratch_shapes=[pltpu.VMEM((hb, HEAD_DIM, HEAD_DIM), F32)],
        compiler_params=_cparams("parallel", "arbitrary"),
    )(qe, kel, wb, u, qk, glb)


def _gdn_scan_bwd(do, qe, kel, wb, vn, qk, glb, sall, nh, name, comm=None):
    t, inner = do.shape
    hb, nb, rows, _, _, _, _ = _gdn_specs(nh, inner, t, SCAN_HB, SCAN_NB)
    last = t // rows - 1
    wide = pl.BlockSpec((rows, hb * HEAD_DIM), lambda g, n: (last - n, g))
    sq = pl.BlockSpec((hb, rows, CHUNK), lambda g, n: (g, last - n, 0))
    glb_spec = pl.BlockSpec((nb * 8, hb * HEAD_DIM), lambda g, n: (last - n, g))

    def body(do_ref, qe_ref, kel_ref, wb_ref, vn_ref, qk_ref, gl_ref, sall_ref,
             dvn_ref, dw_ref, dqe_ref, dkel_ref, dqk_ref, dgl_ref, ds_ref):
        @pl.when(pl.program_id(1) == 0)
        def _():
            ds_ref[...] = jnp.zeros(ds_ref.shape, F32)

        cs = lambda h: slice(h * HEAD_DIM, (h + 1) * HEAD_DIM)
        for c in reversed(range(nb)):
            rs = slice(c * CHUNK, (c + 1) * CHUNK)
            heads = lambda ref: jnp.stack([ref[rs, cs(h)] for h in range(hb)])
            ds, s = ds_ref[...], sall_ref[c]
            dsb, sb = _bf(ds), _bf(s)
            dob, vnb = _bf(heads(do_ref)), _bf(heads(vn_ref))
            dvn = _dot(qk_ref[:, rs, :], dob, BTN) + _dot(heads(kel_ref), dsb, BNN)
            dvnb = _bf(dvn)
            dw = -_dot(dvnb, sb, BNT)
            dqe = _dot(dob, sb, BNT)
            dkel = _dot(vnb, dsb, BNT)
            dqk_ref[:, rs, :] = _dot(dob, vnb, BNT)
            dgl = jnp.sum(jnp.sum(ds * s, axis=2, keepdims=True), axis=1, keepdims=True)
            gl = jnp.stack([gl_ref[c * 8:c * 8 + 1, cs(h)] for h in range(hb)])
            ds_ref[...] = ds * gl + _dot(heads(qe_ref), dob, BTN) - _dot(heads(wb_ref), dvnb, BTN)
            for h in range(hb):
                dvn_ref[rs, cs(h)] = dvn[h]
                dw_ref[rs, cs(h)] = dw[h]
                dqe_ref[rs, cs(h)] = dqe[h]
                dkel_ref[rs, cs(h)] = dkel[h]
                dgl_ref[c * 8:(c + 1) * 8, cs(h)] = jnp.broadcast_to(dgl[h], (8, HEAD_DIM))

    big = jax.ShapeDtypeStruct((t, inner), F32)
    return _grid_call(
        body, name=name, grid=(nh // hb, t // rows),
        in_specs=[wide, wide, wide, wide, wide, sq, glb_spec,
                  pl.BlockSpec((nb, hb, HEAD_DIM, HEAD_DIM), lambda g, n: (last - n, g, 0, 0))],
        out_specs=[wide] * 4 + [sq, glb_spec],
        out_shape=[big] * 4 + [jax.ShapeDtypeStruct((nh, t, CHUNK), F32),
                               jax.ShapeDtypeStruct((t // CHUNK * 8, inner), F32)],
        scratch_shapes=[pltpu.VMEM((hb, HEAD_DIM, HEAD_DIM), F32)],
        args=(do, qe, kel, wb, vn, qk, glb, sall), semantics=("parallel", "arbitrary"), comm=comm)


def _gdn_intra_bwd(q, k, v, gates, tm, w, u, dvn, dw, dqe, dkel, dqk, dglb, nh, name, comm=None):
    t, inner = q.shape
    hb, nb, rows, wide, sq, gts, glb = _gdn_specs(nh, inner, t)

    def body(q_ref, k_ref, v_ref, g_ref, tm_ref, w_ref, u_ref, dvn_ref, dw_ref, dqe_ref, dkel_ref, dqk_ref,
             dgl_ref, dq_ref, dk_ref, dv_ref, dg_ref):
        first = pl.program_id(0) * hb
        selg, selb = _head_select(first, hb, 0), _head_select(first, hb, nh)
        i, j = _iota2(CHUNK, CHUNK)
        lane8 = lax.broadcasted_iota(jnp.int32, (CHUNK, 8), 1)
        row = lax.broadcasted_iota(jnp.int32, (CHUNK, 1), 0)
        lower = jnp.where(j <= i, 1.0, 0.0).astype(F32)
        rsum = lambda x: jnp.sum(x, axis=-1, keepdims=True)
        units, rs, cs, (qv, kv, vv, wv, uv, dvn, dw, dqe, dkel), gcol, grow, bcol = _unit_inputs(
            (q_ref, k_ref, v_ref, w_ref, u_ref, dvn_ref, dw_ref, dqe_ref, dkel_ref), g_ref, selg, selb, hb, nb)
        nu = len(units)
        tmv = jnp.stack([tm_ref[h, rs(c), :] for c, h in units])
        dqk = jnp.where(j <= i, jnp.stack([dqk_ref[h, rs(c), :] for c, h in units]), 0.0)
        dgl = jnp.stack([dgl_ref[c * 8:c * 8 + 1, h * HEAD_DIM:h * HEAD_DIM + 1] for c, h in units])
        e, el, gl, decay = _decay_terms(gcol, grow)
        kb = kv * bcol
        qb, kbf, kbb = _bf(qv), _bf(kv), _bf(kb)
        dqkr = _bf(dqk * decay)
        dq = dqe * e + _dot(dqkr, kbf, BNN)
        dk = dkel * el + _dot(dqkr, qb, BTN)
        de = rsum(dqe * qv)
        del_ = rsum(dkel * kv)
        mq = dqk * _dot(qb, kbf, BNT) * decay
        dsol = _dot3(tmv, jnp.concatenate([dvn, dw], axis=2), BTN)
        dvb, dkbe = dsol[:, :, :HEAD_DIM], dsol[:, :, HEAD_DIM:]
        da = -jnp.where(j < i, _dot3(dsol, jnp.concatenate([uv, wv], axis=2), BNT), 0.0)
        dkk = _bf(da * decay)
        ma = da * _dot(kbb, kbf, BNT) * decay
        dkb = dkbe * e + _dot(dkk, kbf, BNN)
        de = de + rsum(dkbe * kb)
        dk = dk + _dot(dkk, kbb, BTN) + dkb * bcol
        dv = dvb * bcol
        dbeta = rsum(dkb * kv) + rsum(dvb * vv)
        m = mq + ma
        ones = jnp.ones((nu, CHUNK, LANES), F32)
        dgc = rsum(m) - _dot_exact(m, ones, BTN, False)[:, :, 0:1] + de * e - del_ * el
        tail = jnp.sum(del_ * el, axis=1, keepdims=True) + dgl * gl
        dgc = dgc + jnp.where(row == CHUNK - 1, tail, 0.0)
        for n, (c, h) in enumerate(units):
            dq_ref[rs(c), cs(h)] = dq[n]
            dk_ref[rs(c), cs(h)] = dk[n]
            dv_ref[rs(c), cs(h)] = dv[n]
        for c in range(nb):
            dgc_cols = jnp.zeros((CHUNK, 8), F32)
            dbeta_cols = jnp.zeros((CHUNK, 8), F32)
            for h in range(hb):
                dgc_cols = jnp.where(lane8 == h, dgc[c * hb + h], dgc_cols)
                dbeta_cols = jnp.where(lane8 == h, dbeta[c * hb + h], dbeta_cols)
            dg_cols = _dot_exact(lower, dgc_cols, TN, True)
            dg_ref[rs(c), :] = _dot_exact(dg_cols, selg, NN, False) + _dot_exact(dbeta_cols, selb, NN, False)

    big = jax.ShapeDtypeStruct((t, inner), F32)
    return _grid_call(
        body, name=name, grid=(nh // hb, t // rows),
        in_specs=[wide, wide, wide, gts, sq, wide, wide, wide, wide, wide, wide, sq, glb],
        out_specs=[wide, wide, wide, pl.BlockSpec((None, rows, LANES), lambda g, n: (g, n, 0))],
        out_shape=[big, big, big, jax.ShapeDtypeStruct((nh // hb, t, LANES), F32)],
        args=(q, k, v, gates, tm, w, u, dvn, dw, dqe, dkel, dqk, dglb), semantics=("parallel", "parallel"),
        comm=comm)


def _layer_a_fwd(x, hn, w_in_t, w_ab_t, conv_w, alog_row, dt_row, onw, nh, comm, w_out_of):
    t, d = x.shape
    inner = nh * HEAD_DIM
    proj = _mm(hn, w_in_t, "nt", t, 4 * inner, d, out_dtype=F32, name="a_proj")
    ab = _mm(hn, w_ab_t, "nt", t, LANES, d, out_dtype=F32, name="a_proj_ab")
    gates = _gates_fwd(ab, alog_row, dt_row, nh, "a_gates")
    q = _conv_fwd(proj, conv_w, 0, inner, "q", "a_conv_q")
    k = _conv_fwd(proj, conv_w, inner, inner, "k", "a_conv_k")
    v = _conv_fwd(proj, conv_w, 2 * inner, inner, "v", "a_conv_v")
    qe, kel, wb, w, u, qk, tm, glb, *carried = _gdn_intra_fwd(q, k, v, gates, nh, "a_intra", comm)
    o, vn, sall = _gdn_scan_fwd(qe, kel, wb, u, qk, glb, nh, "a_scan")
    g = _gate_fwd(o, proj, 3 * inner, inner, "a_gate", norm_w=onw)
    w_out = w_out_of(carried)
    h1 = _mm(g, w_out, "nn", t, d, inner, out_dtype=F32, name="a_out", res=x)
    return h1, (hn, proj, ab, gates, q, k, v, qe, kel, wb, w, u, qk, tm, glb, o, vn, sall, g), w_out, carried


def _layer_a_bwd(dh1, dh1b, x, nw, w_in_t, w_ab_t, conv_w, alog_row, dt_row, onw, w_out, nh, saved, comms_of,
                 own_comm):
    hn, proj, ab, gates, q, k, v, qe, kel, wb, w, u, qk, tm, glb, o, vn, sall, g = saved
    t, d = x.shape
    inner = w_out.shape[0]
    dg = _mm(dh1b, w_out, "nt", t, inner, d, out_dtype=F32, name="a_dgate")
    dw_out = _mm(g, dh1b, "tn", inner, d, t, out_dtype=BF16, name="a_dwout")
    comm_scan, comm_intra = comms_of(dw_out)
    do, dproj, donw = _gate_bwd(dg, o, proj, 3 * inner, inner, "a_gate_bwd", F32, norm_w=onw,
                                dest=(None, 3 * inner, 4 * inner))
    dvn, dw, dqe, dkel, dqk, dglb, *carried_scan = _gdn_scan_bwd(do, qe, kel, wb, vn, qk, glb, sall, nh,
                                                                 "a_scan_bwd", comm_scan)
    dq, dk, dv, dgates, *carried = _gdn_intra_bwd(q, k, v, gates, tm, w, u, dvn, dw, dqe, dkel, dqk, dglb, nh,
                                                  "a_intra_bwd", comm_intra)
    carried = carried_scan + carried
    dproj, dcq = _conv_bwd(dq, proj, conv_w, 0, inner, "q", "a_conv_q_bwd", dest=(dproj, 0, 4 * inner))
    dproj, dck = _conv_bwd(dk, proj, conv_w, inner, inner, "k", "a_conv_k_bwd", dest=(dproj, inner, 4 * inner))
    dproj, dcv = _conv_bwd(dv, proj, conv_w, 2 * inner, inner, "v", "a_conv_v_bwd",
                           dest=(dproj, 2 * inner, 4 * inner))
    dab, dsmall = _gates_bwd(ab, alog_row, dt_row, dgates, nh, "a_gates_bwd")
    dw_in_t = _mm(dproj, hn, "tn", 4 * inner, d, t, out_dtype=BF16, name="a_dwin")
    dw_ab_t = _mm(dab, hn, "tn", LANES, d, t, out_dtype=BF16, name="a_dwin_ab")
    dconv = jnp.concatenate([dcq[:CONV_K], dck[:CONV_K], dcv[:CONV_K]], axis=1)
    dhn = _mm(dab, w_ab_t, "nn", t, d, LANES, out_dtype=F32, name="a_dhn_ab")
    own = own_comm(dw_in_t, dw_ab_t, dconv)
    dhn = _mm(dproj, w_in_t, "nn", t, d, 4 * inner, out_dtype=F32, name="a_dhn", res=dhn, comm=own)
    dhn, carried_own = dhn if own is not None else (dhn, [])
    dx, _, dnw = _rms_bwd(x, nw, dhn, dh1, "a_rms_bwd")
    return dx, dnw, dsmall, donw, carried, carried_own


def _rows_of(a, rows):
    flat = a.reshape(-1)
    return jnp.pad(flat, (0, rows * LANES - flat.shape[0])).reshape(rows, LANES)


def _to_slabs(g, axis):
    shape = g.shape[:axis] + (N_DEV, g.shape[axis] // N_DEV) + g.shape[axis + 1:]
    return jnp.moveaxis(g.reshape(shape), axis, 0)


def _from_slabs(s, axis):
    m = jnp.moveaxis(s, 0, axis)
    return m.reshape(m.shape[:axis] + (m.shape[axis] * m.shape[axis + 1],) + m.shape[axis + 2:])


def kernel(x, norm_w, a_w_in, a_conv_w, a_a_log, a_dt_bias, a_out_norm_w, a_w_out, b_w_in, b_q_norm_w, b_k_norm_w, b_rel_bias, b_w_out, loss_target, m_norm_w, m_a_w_in, m_a_conv_w, m_a_a_log, m_a_dt_bias, m_a_out_norm_w, m_a_w_out, m_b_w_in, m_b_q_norm_w, m_b_k_norm_w, m_b_rel_bias, m_b_w_out, v_norm_w, v_a_w_in, v_a_conv_w, v_a_a_log, v_a_dt_bias, v_a_out_norm_w, v_a_w_out, v_b_w_in, v_b_q_norm_w, v_b_k_norm_w, v_b_rel_bias, v_b_w_out):
    xs, target = x[0], loss_target[0]
    nh = a_a_log.shape[-1]
    inner = N_DEV * a_w_out.shape[1]

    d = xs.shape[1]
    nw0, nw1 = norm_w[0:1], norm_w[1:2]
    hn0, (ga_in, g_conv) = _rms_fwd(
        xs, nw0, "a_rms", comm=_RoutedGather([a_w_in[0].T.astype(BF16), a_conv_w[0]]))
    wa_in_t = ga_in.reshape(-1, d)
    wa_ab_t = jnp.pad(wa_in_t[4 * inner:], ((0, LANES - 2 * nh), (0, 0)))
    conv_w = _from_slabs(g_conv, 1)
    alog_row = jnp.pad(a_a_log, ((0, 0), (0, LANES - nh)))
    dt_row = jnp.pad(a_dt_bias, ((0, 0), (0, LANES - nh)))

    h1, saved_a, wa_out, (gb_in, gb_out, _) = _layer_a_fwd(
        xs, hn0, wa_in_t, wa_ab_t, conv_w, alog_row, dt_row, a_out_norm_w, nh,
        _Comm("gather", [b_w_in[0].T.astype(BF16), b_w_out[0].astype(BF16), a_w_out[0].astype(BF16)]),
        lambda gathered: _from_slabs(gathered[2], 0))
    wb_in_t = gb_in.reshape(-1, d)
    wb_out = _from_slabs(gb_out, 0)
    bias = _bias_tiles(_pad_rel_bias(b_rel_bias[0]), "b_bias_tiles")
    (dh2, dh2b, loss_row), saved_b = _layer_b_fwd(h1, nw1, wb_in_t, b_q_norm_w, b_k_norm_w, bias, wb_out, target)

    dh1, dh1b, dnw1, dwb_in_t, dqw, dkw, drb, dwb_out = _layer_b_bwd(
        dh2, dh2b, h1, nw1, wb_in_t, b_q_norm_w, b_k_norm_w, bias, wb_out, saved_b)

    def exchange_early(dwa_out):
        return (_Comm("exchange", [_to_slabs(dwb_out, 0).astype(BF16), _to_slabs(dwa_out, 0).astype(BF16)]),
                _Comm("exchange", [dwb_in_t.reshape(N_DEV, -1, d).astype(BF16)]))

    def exchange_last(dwa_in_t, dwa_ab_t, dconv):
        full = jnp.concatenate([dwa_in_t, dwa_ab_t[:2 * nh]], axis=0)
        return _Comm("exchange", [full.reshape(N_DEV, -1, d).astype(BF16), _to_slabs(dconv, 1)])

    dx, dnw0, dsmall, donw, (pb_out, pa_out, pb_in), (pa_in, p_conv) = _layer_a_bwd(
        dh1, dh1b, xs, nw0, wa_in_t, wa_ab_t, conv_w, alog_row, dt_row, a_out_norm_w, wa_out, nh, saved_a,
        exchange_early, exchange_last)
    big = {}
    for name, p, w, m, v in (("a_w_in", pa_in, a_w_in, m_a_w_in, v_a_w_in),
                             ("a_w_out", pa_out, a_w_out, m_a_w_out, v_a_w_out),
                             ("b_w_in", pb_in, b_w_in, m_b_w_in, v_b_w_in),
                             ("b_w_out", pb_out, b_w_out, m_b_w_out, v_b_w_out),
                             ("a_conv_w", p_conv, a_conv_w, m_a_conv_w, v_a_conv_w)):
        big[name] = [o[None] for o in _adamw(p, w[0], m[0], v[0], "adamw_" + name,
                                             transposed=name in ("a_w_in", "b_w_in"))]

    small = (("norm_w", norm_w, m_norm_w, v_norm_w, jnp.concatenate([dnw0, dnw1], axis=0)),
             ("a_a_log", a_a_log, m_a_a_log, v_a_a_log, dsmall[0:1, :nh]),
             ("a_dt_bias", a_dt_bias, m_a_dt_bias, v_a_dt_bias, dsmall[1:2, :nh]),
             ("a_out_norm_w", a_out_norm_w, m_a_out_norm_w, v_a_out_norm_w, donw),
             ("b_q_norm_w", b_q_norm_w, m_b_q_norm_w, v_b_q_norm_w, dqw),
             ("b_k_norm_w", b_k_norm_w, m_b_k_norm_w, v_b_k_norm_w, dkw),
             ("b_rel_bias", b_rel_bias, m_b_rel_bias, v_b_rel_bias, drb))
    rows = [8 * (-(-w.size // (8 * LANES))) for _, w, _, _, _ in small]
    pack = lambda arrs: jnp.concatenate([_rows_of(a, r) for a, r in zip(arrs, rows)] + [jnp.zeros((8, LANES), F32)], axis=0)
    g_pack = jnp.concatenate([_rows_of(g, r) for (_, _, _, _, g), r in zip(small, rows)]
                             + [jnp.broadcast_to(loss_row, (8, LANES))], axis=0)
    (g_all,) = _comm_call(_Comm("gather", [g_pack]), "gather_small_grads")
    outs_small = _adamw(g_all, pack([s[1] for s in small]), pack([s[2] for s in small]),
                        pack([s[3] for s in small]), "adamw_small")
    start = 0
    for (name, w, _, _, _), r in zip(small, rows):
        big[name] = [o[start:start + r].reshape(-1)[:w.size].reshape(w.shape) for o in outs_small]
        start += r
    loss = outs_small[0][start, 0]

    order = ("norm_w", "a_w_in", "a_conv_w", "a_a_log", "a_dt_bias", "a_out_norm_w", "a_w_out", "b_w_in",
             "b_q_norm_w", "b_k_norm_w", "b_rel_bias", "b_w_out")
    return (loss, dx[None]) + tuple(big[n][i] for i in range(4) for n in order)
```

```python
import functools

import jax
import jax.numpy as jnp
from jax import lax
from jax.experimental import pallas as pl
from jax.experimental.pallas import tpu as pltpu

F32 = jnp.float32
BF16 = jnp.bfloat16
MESH_IDS = pl.DeviceIdType.MESH
N_DEV = 8
CHUNK = 64
HEAD_DIM = 128
EPS = 1e-6
CONV_K = 4
LEFT_CHUNKS = 8
REL_CLIP = 256
Q_TILE = LEFT_CHUNKS * CHUNK
ADAM_LR = 0.001
ADAM_B1 = 0.9
ADAM_B2 = 0.999
ADAM_EPS = 1e-08
ADAM_WD = 0.01
ADAM_STEP = 10
NEG_BIG = -1e30
VMEM_LIMIT_BYTES = 56 * 1024 * 1024
HIGHEST = lax.Precision.HIGHEST
ANY = pl.BlockSpec(memory_space=pl.ANY)


def _cparams(*sem):
    return pltpu.CompilerParams(dimension_semantics=tuple(sem), vmem_limit_bytes=VMEM_LIMIT_BYTES)


NN, NT, TN = (((1,), (0,)), ((), ())), (((1,), (1,)), ((), ())), (((0,), (0,)), ((), ()))
BNN, BNT, BTN = (((2,), (1,)), ((0,), (0,))), (((2,), (2,)), ((0,), (0,))), (((1,), (1,)), ((0,), (0,)))


def _dot(a, b, dims, precision=None):
    return lax.dot_general(a, b, dims, preferred_element_type=F32, precision=precision)


def _nn(a, b, precision=None):
    return _dot(a, b, NN, precision)


def _nt(a, b, precision=None):
    return _dot(a, b, NT, precision)


def _tn(a, b, precision=None):
    return _dot(a, b, TN, precision)


def _bf(x):
    return x.astype(BF16)


def _split(x, pieces=2):
    out = []
    for _ in range(pieces - 1):
        hi = x.astype(BF16)
        out.append(hi)
        x = x - hi.astype(F32)
    return out + [x.astype(BF16)]


def _dot3(a, b, dims):
    (ah, al), (bh, bl) = _split(a), _split(b)
    return _dot(ah, bh, dims) + (_dot(ah, bl, dims) + _dot(al, bh, dims))


def _dot_exact(a, b, dims, split_b):
    if split_b:
        a = a.astype(BF16)
        parts = [_dot(a, p, dims) for p in _split(b, 3)]
    else:
        b = b.astype(BF16)
        parts = [_dot(p, b, dims) for p in _split(a, 3)]
    return parts[0] + (parts[1] + parts[2])


def _sigmoid(x):
    return 0.5 * jnp.tanh(0.5 * x) + 0.5


def _silu(x):
    return x * _sigmoid(x)


def _dsilu(x):
    s = _sigmoid(x)
    return s * (1.0 + x * (1.0 - s))


def _my_pos():
    return lax.axis_index("x"), lax.axis_index("y"), lax.axis_index("c")


def _peers(x, y, c):
    def flip(v, f):
        return 1 - v if f else v

    return [(flip(x, kx), flip(y, ky), flip(c, kc)) for kx in (0, 1) for ky in (0, 1) for kc in (0, 1)][1:]


def _lin(p):
    return 4 * p[0] + 2 * p[1] + p[2]


class _Comm:
    def __init__(self, kind, arrays):
        self.kind, self.arrays, self.n = kind, list(arrays), len(arrays)

    def out_shape(self):
        lead = (N_DEV,) if self.kind == "gather" else ()
        return [jax.ShapeDtypeStruct(lead + a.shape, a.dtype) for a in self.arrays]

    def scratch(self):
        return [pltpu.SemaphoreType.DMA((7 * self.n,)), pltpu.SemaphoreType.DMA((7 * self.n,)),
                pltpu.SemaphoreType.DMA((self.n,))]

    def _copies(self, ins, outs, sems, arrivals):
        send_sems, recv_sems, local_sems = sems
        x, y, c = _my_pos()
        me = _lin((x, y, c))
        gather = self.kind == "gather"
        mine = [ins[t] if gather else ins[t].at[me] for t in range(self.n)]
        remote = []
        for k, peer in enumerate(_peers(x, y, c)):
            for t in range(self.n):
                if arrivals:
                    src, dst = mine[t], outs[t].at[_lin(peer)]
                else:
                    src, dst = (ins[t] if gather else ins[t].at[_lin(peer)]), outs[t].at[me]
                remote.append(pltpu.make_async_remote_copy(
                    src_ref=src, dst_ref=dst, send_sem=send_sems.at[k * self.n + t],
                    recv_sem=recv_sems.at[k * self.n + t], device_id=peer, device_id_type=MESH_IDS))
        if arrivals:
            return remote
        return [pltpu.make_async_copy(mine[t], outs[t].at[me], local_sems.at[t]) for t in range(self.n)], remote

    def start(self, ins, outs, sems):
        local, sends = self._copies(ins, outs, sems, False)
        for cp in local + sends:
            cp.start()

    def finish(self, ins, outs, sems):
        for cp in self._copies(ins, outs, sems, True):
            cp.wait_recv()
        local, sends = self._copies(ins, outs, sems, False)
        for cp in sends:
            cp.wait_send()
        for cp in local:
            cp.wait()


def _xor(a, b):
    return a + b - 2 * a * b


class _RoutedGather(_Comm):
    def __init__(self, arrays):
        super().__init__("gather", arrays)

    def _plan(self, outs, sems):
        send_sems, recv_sems, _ = sems
        x, y, c = _my_pos()
        sib, xn, yn, dg = (x, y, 1 - c), (1 - x, y, c), (x, 1 - y, c), (1 - x, 1 - y, c)
        via = (_xor(x, 1 - c), _xor(y, c), c)
        onto = (_xor(x, c), _xor(y, 1 - c), c)
        routes = [(None, sib, sib), (None, xn, xn), (None, yn, yn), (via, onto, dg),
                  (xn, sib, (1 - x, y, 1 - c)), (yn, sib, (x, 1 - y, 1 - c)), (dg, sib, (1 - x, 1 - y, 1 - c))]

        def copy(k, t, src, slot, target):
            return pltpu.make_async_remote_copy(
                src_ref=src, dst_ref=outs[t].at[slot], send_sem=send_sems.at[k * self.n + t],
                recv_sem=recv_sems.at[k * self.n + t], device_id=target, device_id_type=MESH_IDS)

        return (x, y, c), routes, copy

    def start(self, ins, outs, sems):
        me, routes, copy = self._plan(outs, sems)
        for t in range(self.n):
            pltpu.make_async_copy(ins[t], outs[t].at[_lin(me)], sems[2].at[t]).start()
            for k in range(3):
                copy(k, t, ins[t], _lin(me), routes[k][1]).start()

    def finish(self, ins, outs, sems):
        me, routes, copy = self._plan(outs, sems)

        def arrived(k):
            for t in range(self.n):
                copy(k, t, ins[t], _lin(routes[k][2]), me).wait_recv()

        def pass_on(k):
            for t in range(self.n):
                copy(k, t, outs[t].at[_lin(routes[k][0])], _lin(routes[k][0]), routes[k][1]).start()

        arrived(1)
        arrived(2)
        for k in (3, 4, 5):
            pass_on(k)
        arrived(3)
        pass_on(6)
        for k in (0, 4, 5, 6):
            arrived(k)
        for t in range(self.n):
            for k in range(7):
                src = ins[t] if k < 3 else outs[t].at[_lin(routes[k][0])]
                copy(k, t, src, _lin(me), routes[k][1]).wait_send()
            pltpu.make_async_copy(ins[t], outs[t].at[_lin(me)], sems[2].at[t]).wait()


def _comm_call(comm, name):
    n = comm.n

    def body(*refs):
        ins, outs, sems = refs[:n], refs[n:2 * n], refs[2 * n:]
        comm.start(ins, outs, sems)
        comm.finish(ins, outs, sems)

    return pl.pallas_call(
        body, name=name, out_shape=comm.out_shape(), in_specs=[ANY] * n, out_specs=[ANY] * n,
        scratch_shapes=comm.scratch(),
    )(*comm.arrays)


def _grid_call(body, *, name, grid, in_specs, out_specs, out_shape, args, scratch_shapes=(), semantics=None, comm=None):
    if comm is None:
        return pl.pallas_call(
            body, name=name, grid=grid, in_specs=in_specs, out_specs=out_specs, out_shape=out_shape,
            scratch_shapes=list(scratch_shapes), compiler_params=_cparams(*semantics),
        )(*args)
    n_in, n_out, n_sc, n = len(in_specs), len(out_specs), len(scratch_shapes), comm.n

    def full(*refs):
        ins, refs = refs[:n_in], refs[n_in:]
        cins, refs = refs[:n], refs[n:]
        outs, refs = refs[:n_out], refs[n_out:]
        couts, refs = refs[:n], refs[n:]
        scratch, sems = refs[:n_sc], refs[n_sc:]
        ids = [pl.program_id(a) for a in range(len(grid))]
        first = functools.reduce(jnp.logical_and, [i == 0 for i in ids])
        last = functools.reduce(jnp.logical_and, [i == g - 1 for i, g in zip(ids, grid)])

        @pl.when(first)
        def _():
            comm.start(cins, couts, sems)

        body(*ins, *outs, *scratch)

        @pl.when(last)
        def _():
            comm.finish(cins, couts, sems)

    return pl.pallas_call(
        full, name=name, grid=grid, in_specs=list(in_specs) + [ANY] * n, out_specs=list(out_specs) + [ANY] * n,
        out_shape=list(out_shape) + comm.out_shape(), scratch_shapes=list(scratch_shapes) + comm.scratch(),
        compiler_params=_cparams(*(["arbitrary"] * len(grid))),
    )(*(list(args) + comm.arrays))


LONG_K_TILES = dict(tm=512, tn=512, tk=8192)


def _mm(a, b, mode, m, n, k, *, out_dtype, name, tm=1024, tn=1024, tk=2048,
        a_m0=0, a_k0=0, b_n0=0, b_k0=0, res=None, comm=None, loss_target=None):
    tm, tn, tk = min(tm, m), min(tn, n), min(tk, k)
    nm, nn, nk = m // tm, n // tn, k // tk
    assert nm * tm == m and nn * tn == n and nk * tk == k
    am, ak, bn, bk = a_m0 // tm, a_k0 // tk, b_n0 // tn, b_k0 // tk
    assert am * tm == a_m0 and ak * tk == a_k0 and bn * tn == b_n0 and bk * tk == b_k0
    if mode == "tn":
        a_spec = pl.BlockSpec((tk, tm), lambda i, j, q: (q + ak, i + am))
        a_dims = (0,)
    else:
        a_spec = pl.BlockSpec((tm, tk), lambda i, j, q: (i + am, q + ak))
        a_dims = (1,)
    if mode == "nt":
        b_spec = pl.BlockSpec((tn, tk), lambda i, j, q: (j + bn, q + bk))
        b_dims = (1,)
    else:
        b_spec = pl.BlockSpec((tk, tn), lambda i, j, q: (q + bk, j + bn))
        b_dims = (0,)
    o_spec = pl.BlockSpec((tm, tn), lambda i, j, q: (i, j))
    has_res = res is not None
    has_loss = loss_target is not None
    n_in = 2 + has_res + has_loss
    n_out = 3 if has_loss else 1

    def body(*refs):
        a_ref, b_ref = refs[0], refs[1]
        res_ref = refs[2] if has_res else None
        o_ref = refs[n_in]
        p = _dot(a_ref[...], b_ref[...], ((a_dims, b_dims), ((), ())))

        def finish(total):
            if has_res:
                total = total + res_ref[...].astype(F32)
            if not has_loss:
                o_ref[...] = total.astype(out_dtype)
                return
            err = total - refs[n_in - 1][...]
            grad = err * (1.0 / n)
            o_ref[...] = grad
            refs[n_in + 1][...] = grad.astype(BF16)
            l_ref = refs[n_in + 2]
            part = jnp.zeros((1, LANES), F32) + 0.5 * jnp.sum(err * err) * (1.0 / n)
            first = (pl.program_id(0) == 0) & (pl.program_id(1) == 0)

            @pl.when(first)
            def _():
                l_ref[...] = part

            @pl.when(jnp.logical_not(first))
            def _():
                l_ref[...] += part

        if nk == 1:
            finish(p)
        else:
            acc_ref = refs[n_in + n_out]
            q = pl.program_id(2)

            @pl.when(q == 0)
            def _():
                acc_ref[...] = p

            @pl.when(q > 0)
            def _():
                acc_ref[...] += p

            @pl.when(q == nk - 1)
            def _():
                finish(acc_ref[...])

    extra_in = ([res] if has_res else []) + ([loss_target] if has_loss else [])
    if has_loss:
        return _grid_call(
            body, name=name, grid=(nm, nn, nk), in_specs=[a_spec, b_spec] + [o_spec] * len(extra_in),
            out_specs=[o_spec, o_spec, pl.BlockSpec((1, LANES), lambda i, j, q: (0, 0))],
            out_shape=[jax.ShapeDtypeStruct((m, n), F32), jax.ShapeDtypeStruct((m, n), BF16),
                       jax.ShapeDtypeStruct((1, LANES), F32)],
            scratch_shapes=[pltpu.VMEM((tm, tn), F32)] if nk > 1 else [],
            args=[a, b] + extra_in, semantics=("arbitrary", "arbitrary", "arbitrary"))
    out, *carried = _grid_call(
        body, name=name, grid=(nm, nn, nk),
        in_specs=[a_spec, b_spec] + [o_spec] * len(extra_in),
        out_specs=[o_spec], out_shape=[jax.ShapeDtypeStruct((m, n), out_dtype)],
        scratch_shapes=[pltpu.VMEM((tm, tn), F32)] if nk > 1 else [],
        args=[a, b] + extra_in, semantics=("parallel", "parallel", "arbitrary"), comm=comm)
    return out if comm is None else (out, carried)


def _rms_fwd(x, w, name, tr=512, comm=None):
    t, d = x.shape
    tr = min(tr, t)

    def body(x_ref, w_ref, o_ref):
        xv = x_ref[...]
        r = lax.rsqrt(jnp.mean(xv * xv, axis=-1, keepdims=True) + EPS)
        o_ref[...] = (xv * r * w_ref[...]).astype(BF16)

    out, *carried = _grid_call(
        body, name=name, grid=(t // tr,),
        in_specs=[pl.BlockSpec((tr, d), lambda i: (i, 0)), pl.BlockSpec((1, d), lambda i: (0, 0))],
        out_specs=[pl.BlockSpec((tr, d), lambda i: (i, 0))],
        out_shape=[jax.ShapeDtypeStruct((t, d), BF16)], args=(x, w), semantics=("parallel",), comm=comm)
    return out if comm is None else (out, carried)


def _rms_bwd(x, w, dy, dres, name, tr=512):
    t, d = x.shape
    tr = min(tr, t)

    def body(x_ref, w_ref, dy_ref, dres_ref, dx_ref, dxb_ref, dw_ref):
        xv = x_ref[...]
        dyv = dy_ref[...].astype(F32)
        r = lax.rsqrt(jnp.mean(xv * xv, axis=-1, keepdims=True) + EPS)
        gy = dyv * w_ref[...]
        proj = jnp.sum(gy * xv, axis=-1, keepdims=True) * (1.0 / d)
        dx = dres_ref[...] + r * gy - xv * (r * r * r) * proj
        dx_ref[...] = dx
        dxb_ref[...] = dx.astype(BF16)
        part = jnp.sum(dyv * xv * r, axis=0, keepdims=True)

        @pl.when(pl.program_id(0) == 0)
        def _():
            dw_ref[...] = part

        @pl.when(pl.program_id(0) > 0)
        def _():
            dw_ref[...] += part

    row = pl.BlockSpec((tr, d), lambda i: (i, 0))
    vec = pl.BlockSpec((1, d), lambda i: (0, 0))
    return pl.pallas_call(
        body, name=name, grid=(t // tr,),
        in_specs=[row, vec, row, row], out_specs=[row, row, vec],
        out_shape=[jax.ShapeDtypeStruct((t, d), F32), jax.ShapeDtypeStruct((t, d), BF16),
                   jax.ShapeDtypeStruct((1, d), F32)],
        compiler_params=_cparams("arbitrary"),
    )(x, w, dy, dres)


def _adamw(parts, w, m, v, name, tr=128, transposed=False):
    r, c = w.shape
    tr = tr if r % tr == 0 else r
    c1 = 1.0 - ADAM_B1 ** ADAM_STEP
    c2 = 1.0 - ADAM_B2 ** ADAM_STEP

    def body(p_ref, w_ref, m_ref, v_ref, g_ref, d_ref, nm_ref, nv_ref):
        g = p_ref[0].astype(F32)
        for s in range(1, N_DEV):
            g = g + p_ref[s].astype(F32)
        if transposed:
            i, j = lax.broadcasted_iota(jnp.int32, (tr, tr), 0), lax.broadcasted_iota(jnp.int32, (tr, tr), 1)
            g = _dot_exact(jnp.where(i == j, 1.0, 0.0), g, NT, True)
        nm = ADAM_B1 * m_ref[...] + (1.0 - ADAM_B1) * g
        nv = ADAM_B2 * v_ref[...] + (1.0 - ADAM_B2) * (g * g)
        m_hat = nm / c1
        v_hat = nv / c2
        g_ref[...] = g
        d_ref[...] = -ADAM_LR * (m_hat / (jnp.sqrt(v_hat) + ADAM_EPS) + ADAM_WD * w_ref[...])
        nm_ref[...] = nm
        nv_ref[...] = nv

    blk = pl.BlockSpec((tr, c), lambda i: (i, 0))
    p_spec = (pl.BlockSpec((N_DEV, c, tr), lambda i: (0, 0, i)) if transposed
              else pl.BlockSpec((N_DEV, tr, c), lambda i: (0, i, 0)))
    return pl.pallas_call(
        body, name=name, grid=(r // tr,),
        in_specs=[p_spec, blk, blk, blk],
        out_specs=[blk] * 4, out_shape=[jax.ShapeDtypeStruct((r, c), F32)] * 4,
        compiler_params=_cparams("parallel"),
    )(parts, w, m, v)


ROW_TILE, ROW_HEADS = 512, 16
CONV_ROWS, CONV_HEADS = 512, 8


def _window(dest, inner, n_in, out_index):
    if dest is None:
        return inner, 0, [], [], {}
    buf, col0, total = dest
    if buf is None:
        return total, col0, [], [], {}
    return total, col0, [buf], [ANY], {n_in: out_index}


def _skip_ref(body, at, count):
    return body if count == 0 else (lambda *refs: body(*refs[:at], *refs[at + count:]))


def _heads_of(x, nh):
    return [x[:, h * HEAD_DIM:(h + 1) * HEAD_DIM] for h in range(nh)]


def _headnorm_fwd(proj, w, col0, inner, name, tr=ROW_TILE, hb=ROW_HEADS):
    t = proj.shape[0]
    tr = min(tr, t)
    hb = min(hb, inner // HEAD_DIM)
    wc = hb * HEAD_DIM
    c0 = col0 // wc

    def body(x_ref, w_ref, o_ref):
        outs = []
        for xh in _heads_of(x_ref[...], hb):
            r = lax.rsqrt(jnp.mean(xh * xh, axis=-1, keepdims=True) + EPS)
            outs.append((xh * r * w_ref[...]).astype(BF16))
        o_ref[...] = jnp.concatenate(outs, axis=1)

    return pl.pallas_call(
        body, name=name, grid=(t // tr, inner // wc),
        in_specs=[pl.BlockSpec((tr, wc), lambda i, j: (i, j + c0)), pl.BlockSpec((1, HEAD_DIM), lambda i, j: (0, 0))],
        out_specs=pl.BlockSpec((tr, wc), lambda i, j: (i, j)),
        out_shape=jax.ShapeDtypeStruct((t, inner), BF16),
        compiler_params=_cparams("parallel", "parallel"),
    )(proj, w)


def _headnorm_bwd(dy, proj, w, col0, inner, name, tr=ROW_TILE, hb=ROW_HEADS, dest=None):
    t = proj.shape[0]
    tr = min(tr, t)
    hb = min(hb, inner // HEAD_DIM)
    wc = hb * HEAD_DIM
    c0 = col0 // wc
    width, out0, more, more_specs, aliases = _window(dest, inner, 3, 0)

    def body(dy_ref, x_ref, w_ref, dx_ref, dw_ref):
        outs = []
        part = jnp.zeros((1, HEAD_DIM), F32)
        for dyh, xh in zip(_heads_of(dy_ref[...], hb), _heads_of(x_ref[...], hb)):
            r = lax.rsqrt(jnp.mean(xh * xh, axis=-1, keepdims=True) + EPS)
            gy = dyh * w_ref[...]
            pr = jnp.sum(gy * xh, axis=-1, keepdims=True) * (1.0 / HEAD_DIM)
            outs.append((r * gy - xh * (r * r * r) * pr).astype(BF16))
            part = part + jnp.sum(dyh * xh * r, axis=0, keepdims=True)
        dx_ref[...] = jnp.concatenate(outs, axis=1)
        first = (pl.program_id(0) == 0) & (pl.program_id(1) == 0)

        @pl.when(first)
        def _():
            dw_ref[...] = part

        @pl.when(jnp.logical_not(first))
        def _():
            dw_ref[...] += part

    blk = pl.BlockSpec((tr, wc), lambda i, j: (i, j))
    return pl.pallas_call(
        _skip_ref(body, 3, len(more)), name=name, grid=(t // tr, inner // wc),
        in_specs=[blk, pl.BlockSpec((tr, wc), lambda i, j: (i, j + c0)),
                  pl.BlockSpec((1, HEAD_DIM), lambda i, j: (0, 0))] + more_specs,
        out_specs=[pl.BlockSpec((tr, wc), lambda i, j: (i, j + out0 // wc)),
                   pl.BlockSpec((1, HEAD_DIM), lambda i, j: (0, 0))],
        out_shape=[jax.ShapeDtypeStruct((t, width), BF16), jax.ShapeDtypeStruct((1, HEAD_DIM), F32)],
        input_output_aliases=aliases, compiler_params=_cparams("arbitrary", "arbitrary"),
    )(dy, proj, w, *more)


def _gate_fwd(o, proj, zcol0, inner, name, norm_w=None, tr=ROW_TILE, hb=ROW_HEADS):
    t = o.shape[0]
    tr = min(tr, t)
    hb = min(hb, inner // HEAD_DIM)
    wc = hb * HEAD_DIM
    c0 = zcol0 // wc
    has_w = norm_w is not None

    def body(*refs):
        o_ref, z_ref = refs[0], refs[1]
        out_ref = refs[2 + has_w]
        outs = []
        for oh, zh in zip(_heads_of(o_ref[...], hb), _heads_of(z_ref[...], hb)):
            if has_w:
                r = lax.rsqrt(jnp.mean(oh * oh, axis=-1, keepdims=True) + EPS)
                oh = oh * r * refs[2][...]
            outs.append((oh * _silu(zh)).astype(BF16))
        out_ref[...] = jnp.concatenate(outs, axis=1)

    blk = pl.BlockSpec((tr, wc), lambda i, j: (i, j))
    vec = pl.BlockSpec((1, HEAD_DIM), lambda i, j: (0, 0))
    return pl.pallas_call(
        body, name=name, grid=(t // tr, inner // wc),
        in_specs=[blk, pl.BlockSpec((tr, wc), lambda i, j: (i, j + c0))] + ([vec] if has_w else []),
        out_specs=blk, out_shape=jax.ShapeDtypeStruct((t, inner), BF16),
        compiler_params=_cparams("parallel", "parallel"),
    )(*([o, proj] + ([norm_w] if has_w else [])))


def _gate_bwd(dg, o, proj, zcol0, inner, name, do_dtype, norm_w=None, tr=ROW_TILE, hb=ROW_HEADS, dest=None):
    t = o.shape[0]
    tr = min(tr, t)
    hb = min(hb, inner // HEAD_DIM)
    wc = hb * HEAD_DIM
    c0 = zcol0 // wc
    has_w = norm_w is not None
    width, out0, more, more_specs, aliases = _window(dest, inner, 3 + has_w, 1)

    def body(*refs):
        dg_ref, o_ref, z_ref = refs[0], refs[1], refs[2]
        do_ref, dz_ref = refs[3 + has_w], refs[4 + has_w]
        dos, dzs = [], []
        part = jnp.zeros((1, HEAD_DIM), F32)
        for dgh, oh, zh in zip(_heads_of(dg_ref[...], hb), _heads_of(o_ref[...], hb), _heads_of(z_ref[...], hb)):
            dy = dgh * _silu(zh)
            if has_w:
                w = refs[3][...]
                r = lax.rsqrt(jnp.mean(oh * oh, axis=-1, keepdims=True) + EPS)
                on = oh * r
                dzs.append((dgh * on * w * _dsilu(zh)).astype(BF16))
                gy = dy * w
                pr = jnp.sum(gy * oh, axis=-1, keepdims=True) * (1.0 / HEAD_DIM)
                dos.append((r * gy - oh * (r * r * r) * pr).astype(do_dtype))
                part = part + jnp.sum(dy * on, axis=0, keepdims=True)
            else:
                dzs.append((dgh * oh * _dsilu(zh)).astype(BF16))
                dos.append(dy.astype(do_dtype))
        do_ref[...] = jnp.concatenate(dos, axis=1)
        dz_ref[...] = jnp.concatenate(dzs, axis=1)
        if has_w:
            dw_ref = refs[6]
            first = (pl.program_id(0) == 0) & (pl.program_id(1) == 0)

            @pl.when(first)
            def _():
                dw_ref[...] = part

            @pl.when(jnp.logical_not(first))
            def _():
                dw_ref[...] += part

    blk = pl.BlockSpec((tr, wc), lambda i, j: (i, j))
    vec = pl.BlockSpec((1, HEAD_DIM), lambda i, j: (0, 0))
    return pl.pallas_call(
        _skip_ref(body, 3 + has_w, len(more)), name=name, grid=(t // tr, inner // wc),
        in_specs=[blk, blk, pl.BlockSpec((tr, wc), lambda i, j: (i, j + c0))] + ([vec] if has_w else []) + more_specs,
        out_specs=[blk, pl.BlockSpec((tr, wc), lambda i, j: (i, j + out0 // wc))] + ([vec] if has_w else []),
        out_shape=[jax.ShapeDtypeStruct((t, inner), do_dtype), jax.ShapeDtypeStruct((t, width), BF16)]
        + ([jax.ShapeDtypeStruct((1, HEAD_DIM), F32)] if has_w else []),
        input_output_aliases=aliases, compiler_params=_cparams("arbitrary", "arbitrary"),
    )(*([dg, o, proj] + ([norm_w] if has_w else []) + more))


N_REL = 2 * REL_CLIP + 1
REL_PAD = 640
WIN = 2 * Q_TILE


def _diag_onehot():
    i = lax.broadcasted_iota(jnp.int32, (REL_PAD, WIN), 0)
    j = lax.broadcasted_iota(jnp.int32, (REL_PAD, WIN), 1)
    rel = jnp.where(j < Q_TILE + CHUNK, Q_TILE - j, Q_TILE + WIN - j)
    used = (j < Q_TILE + CHUNK) | (j > WIN - CHUNK)
    idx = jnp.clip(rel, -REL_CLIP, REL_CLIP) + REL_CLIP
    return jnp.where(used & (i == idx), 1.0, 0.0).astype(F32)


def _band_mask():
    r = lax.broadcasted_iota(jnp.int32, (Q_TILE, WIN), 0) // CHUNK
    kc = lax.broadcasted_iota(jnp.int32, (Q_TILE, WIN), 1) // CHUNK - LEFT_CHUNKS
    return (kc <= r) & (kc >= r - LEFT_CHUNKS)


def _bias_tiles(rel_bias_pad, name):
    nh = rel_bias_pad.shape[0]

    def body(rb_ref, o_ref):
        dvec = _nn(rb_ref[...], _diag_onehot(), HIGHEST)[0:1, :]
        tile = pltpu.roll(jnp.broadcast_to(dvec, (Q_TILE, WIN)), 0, 1, stride=1, stride_axis=0)
        o_ref[...] = jnp.where(_band_mask(), tile, NEG_BIG)

    return pl.pallas_call(
        body, name=name, grid=(nh,),
        in_specs=[pl.BlockSpec((None, 8, REL_PAD), lambda h: (h, 0, 0))],
        out_specs=pl.BlockSpec((None, Q_TILE, WIN), lambda h: (h, 0, 0)),
        out_shape=jax.ShapeDtypeStruct((nh, Q_TILE, WIN), F32),
        compiler_params=_cparams("parallel"),
    )(rel_bias_pad)


def _bias_grad(dtile, name):
    nh = dtile.shape[0]

    def body(d_ref, o_ref):
        ri = lax.broadcasted_iota(jnp.int32, (Q_TILE, Q_TILE), 0)
        ci = lax.broadcasted_iota(jnp.int32, (Q_TILE, Q_TILE), 1)
        flip = jnp.where(ri + ci == Q_TILE - 1, 1.0, 0.0).astype(F32)
        rev = _dot_exact(flip, d_ref[...], NN, True)
        rolled = pltpu.roll(rev, WIN - (Q_TILE - 1), 1, stride=1, stride_axis=0)
        diag = jnp.broadcast_to(jnp.sum(rolled, axis=0, keepdims=True), (8, WIN))
        o_ref[...] = _nt(diag, _diag_onehot(), HIGHEST)

    return pl.pallas_call(
        body, name=name, grid=(nh,),
        in_specs=[pl.BlockSpec((None, Q_TILE, WIN), lambda h: (h, 0, 0))],
        out_specs=pl.BlockSpec((None, 8, REL_PAD), lambda h: (h, 0, 0)),
        out_shape=jax.ShapeDtypeStruct((nh, 8, REL_PAD), F32),
        compiler_params=_cparams("parallel"),
    )(dtile)


GROUP = 2 * CHUNK
BAND = Q_TILE + GROUP


N_GROUPS = Q_TILE // GROUP
ATTN_HB = 2
ATTN_HB_FWD = 4


def _head_cols(j):
    return slice(j * HEAD_DIM, (j + 1) * HEAD_DIM)


def _groups(ref, units):
    return jnp.stack([ref[GROUP * g:GROUP * (g + 1), _head_cols(j)] for j, g in units])


def _bands(r0_ref, r1_ref, units):
    return jnp.stack([jnp.concatenate([r0_ref[GROUP * g:, _head_cols(j)], r1_ref[:GROUP * (g + 1), _head_cols(j)]],
                                      axis=0) for j, g in units])


def _group_probs(q, kw, b_ref, units, first_tile):
    bias = jnp.stack([b_ref[j, GROUP * g:GROUP * (g + 1), GROUP * g:GROUP * g + BAND] for j, g in units])
    s = _dot(q, kw, BNT) * (HEAD_DIM ** -0.5) + bias
    col = jnp.stack([lax.broadcasted_iota(jnp.int32, (GROUP, BAND), 1) + GROUP * g for _, g in units])
    s = jnp.where(first_tile & (col < Q_TILE), NEG_BIG, s)
    p = jnp.exp(s - jnp.max(s, axis=-1, keepdims=True))
    return p * (1.0 / jnp.sum(p, axis=-1, keepdims=True))


def _attn_fwd(q, k, v, v_col0, bias, name):
    t, inner = q.shape
    nh, nt = inner // HEAD_DIM, t // Q_TILE
    hb = min(ATTN_HB_FWD, nh)
    wc = hb * HEAD_DIM
    vh = v_col0 // wc
    units = [(j, g) for j in range(hb) for g in range(N_GROUPS)]

    def body(q_ref, k0_ref, k1_ref, v0_ref, v1_ref, b_ref, o_ref):
        p = _group_probs(_groups(q_ref, units), _bands(k0_ref, k1_ref, units), b_ref, units, pl.program_id(1) == 0)
        o = _dot(_bf(p), _bf(_bands(v0_ref, v1_ref, units)), BNN)
        for n, (j, g) in enumerate(units):
            o_ref[GROUP * g:GROUP * (g + 1), _head_cols(j)] = o[n]

    cur = pl.BlockSpec((Q_TILE, wc), lambda h, i: (i, h))
    prev = pl.BlockSpec((Q_TILE, wc), lambda h, i: (jnp.maximum(i - 1, 0), h))
    v_cur = pl.BlockSpec((Q_TILE, wc), lambda h, i: (i, h + vh))
    v_prev = pl.BlockSpec((Q_TILE, wc), lambda h, i: (jnp.maximum(i - 1, 0), h + vh))
    return pl.pallas_call(
        body, name=name, grid=(nh // hb, nt),
        in_specs=[cur, prev, cur, v_prev, v_cur, pl.BlockSpec((hb, Q_TILE, WIN), lambda h, i: (h, 0, 0))],
        out_specs=cur, out_shape=jax.ShapeDtypeStruct((t, inner), F32),
        compiler_params=_cparams("parallel", "parallel"),
    )(q, k, k, v, v, bias)


def _attn_bwd(q, k, v, v_col0, do, bias, name, dest=None):
    t, inner = q.shape
    nh, nt = inner // HEAD_DIM, t // Q_TILE
    scale = HEAD_DIM ** -0.5
    hb = min(ATTN_HB, nh)
    wc = hb * HEAD_DIM
    units = [(j, g) for j in range(hb) for g in range(N_GROUPS)]

    def body(q_ref, k0_ref, k1_ref, v0_ref, v1_ref, do_ref, b_ref, dq_ref, dk_ref, dv_ref, db_ref,
             ck_ref, cv_ref, wk_ref, wv_ref):
        i = pl.program_id(1)

        @pl.when(i == 0)
        def _():
            ck_ref[...] = jnp.zeros(blk, F32)
            cv_ref[...] = jnp.zeros(blk, F32)
            db_ref[...] = jnp.zeros((hb, Q_TILE, WIN), F32)

        @pl.when(i < nt)
        def _():
            wk_ref[...] = jnp.zeros((WIN, wc), F32)
            wv_ref[...] = jnp.zeros((WIN, wc), F32)
            qv, dov = _groups(q_ref, units), _groups(do_ref, units)
            kw, vw = _bands(k0_ref, k1_ref, units), _bf(_bands(v0_ref, v1_ref, units))
            p = _group_probs(qv, kw, b_ref, units, i == 0)
            dp = _dot(dov, vw, BNT)
            ds = p * (dp - jnp.sum(p * dp, axis=-1, keepdims=True))
            pb, dsb = _bf(p), _bf(ds)
            dq = _dot(dsb, kw, BNN) * scale
            dkw = _dot(dsb, qv, BTN) * scale
            dvw = _dot(pb, dov, BTN)
            for n, (j, g) in enumerate(units):
                rows, cols = slice(GROUP * g, GROUP * (g + 1)), slice(GROUP * g, GROUP * g + BAND)
                db_ref[j, rows, cols] += ds[n]
                dq_ref[rows, _head_cols(j)] = dq[n]
                wk_ref[cols, _head_cols(j)] += dkw[n]
                wv_ref[cols, _head_cols(j)] += dvw[n]
            dk_ref[...] = ck_ref[...] + wk_ref[:Q_TILE, :]
            dv_ref[...] = (cv_ref[...] + wv_ref[:Q_TILE, :]).astype(BF16)
            ck_ref[...] = wk_ref[Q_TILE:, :]
            cv_ref[...] = wv_ref[Q_TILE:, :]

        @pl.when(i == nt)
        def _():
            dk_ref[...] = ck_ref[...]
            dv_ref[...] = cv_ref[...].astype(BF16)

    blk = (Q_TILE, wc)
    cur = pl.BlockSpec(blk, lambda h, i: (jnp.minimum(i, nt - 1), h))
    prev = pl.BlockSpec(blk, lambda h, i: (jnp.clip(i - 1, 0, nt - 1), h))
    lag = pl.BlockSpec(blk, lambda h, i: (jnp.maximum(i - 1, 0), h))
    vh = v_col0 // wc
    v_cur = pl.BlockSpec(blk, lambda h, i: (jnp.minimum(i, nt - 1), h + vh))
    v_prev = pl.BlockSpec(blk, lambda h, i: (jnp.clip(i - 1, 0, nt - 1), h + vh))
    tile = pl.BlockSpec((hb, Q_TILE, WIN), lambda h, i: (h, 0, 0))
    width, out0, more, more_specs, aliases = _window(dest, inner, 7, 2)
    return pl.pallas_call(
        _skip_ref(body, 7, len(more)), name=name, grid=(nh // hb, nt + 1),
        in_specs=[cur, prev, cur, v_prev, v_cur, cur, tile] + more_specs,
        out_specs=[cur, lag, pl.BlockSpec(blk, lambda h, i: (jnp.maximum(i - 1, 0), h + out0 // wc)), tile],
        out_shape=[jax.ShapeDtypeStruct((t, inner), F32)] * 2 + [jax.ShapeDtypeStruct((t, width), BF16),
                                                                 jax.ShapeDtypeStruct((nh, Q_TILE, WIN), F32)],
        scratch_shapes=[pltpu.VMEM(blk, F32), pltpu.VMEM(blk, F32),
                        pltpu.VMEM((WIN, wc), F32), pltpu.VMEM((WIN, wc), F32)],
        input_output_aliases=aliases, compiler_params=_cparams("arbitrary", "arbitrary"),
    )(q, k, k, v, v, do, bias, *more)


def _pad_rel_bias(rel_bias):
    nh = rel_bias.shape[0]
    return jnp.broadcast_to(jnp.pad(rel_bias, ((0, 0), (0, REL_PAD - N_REL)))[:, None, :], (nh, 8, REL_PAD))


def _layer_b_fwd(h1, nw, w_in_t, qw, kw, bias, w_out, target):
    t, d = h1.shape
    inner = w_out.shape[0]
    hn = _rms_fwd(h1, nw, "b_rms")
    proj = _mm(hn, w_in_t, "nt", t, 4 * inner, d, out_dtype=F32, name="b_proj")
    qn = _headnorm_fwd(proj, qw, 0, inner, "b_qnorm")
    kn = _headnorm_fwd(proj, kw, inner, inner, "b_knorm")
    o = _attn_fwd(qn, kn, proj, 2 * inner, bias, "b_attn")
    g = _gate_fwd(o, proj, 3 * inner, inner, "b_gate")
    loss_parts = _mm(g, w_out, "nn", t, d, inner, out_dtype=F32, name="b_out", res=h1, loss_target=target)
    return loss_parts, (hn, proj, qn, kn, o, g)


def _layer_b_bwd(dh2, dh2b, h1, nw, w_in_t, qw, kw, bias, w_out, saved):
    hn, proj, qn, kn, o, g = saved
    t, d = h1.shape
    inner = w_out.shape[0]
    dg = _mm(dh2b, w_out, "nt", t, inner, d, out_dtype=F32, name="b_dgate")
    dw_out = _mm(g, dh2b, "tn", inner, d, t, out_dtype=BF16, name="b_dwout")
    do, dproj = _gate_bwd(dg, o, proj, 3 * inner, inner, "b_gate_bwd", BF16, dest=(None, 3 * inner, 4 * inner))
    dq, dk, dproj, dtile = _attn_bwd(qn, kn, proj, 2 * inner, do, bias, "b_attn_bwd",
                                     dest=(dproj, 2 * inner, 4 * inner))
    dproj, dqw = _headnorm_bwd(dq, proj, qw, 0, inner, "b_qnorm_bwd", dest=(dproj, 0, 4 * inner))
    dproj, dkw = _headnorm_bwd(dk, proj, kw, inner, inner, "b_knorm_bwd", dest=(dproj, inner, 4 * inner))
    dhn = _mm(dproj, w_in_t, "nn", t, d, 4 * inner, out_dtype=F32, name="b_dhn", **LONG_K_TILES)
    dw_in_t = _mm(dproj, hn, "tn", 4 * inner, d, t, out_dtype=BF16, name="b_dwin", **LONG_K_TILES)
    dh1, dh1b, dnw = _rms_bwd(h1, nw, dhn, dh2, "b_rms_bwd")
    drb = _bias_grad(dtile, "b_bias_grad")[:, 0, :N_REL]
    return dh1, dh1b, dnw, dw_in_t, dqw, dkw, drb, dw_out


LANES = 128


def _softplus(x):
    return jnp.maximum(x, 0.0) + jnp.log1p(jnp.exp(-jnp.abs(x)))


def _gates_fwd(ab, alog_row, dt_row, nh, name, tr=1024):
    t = ab.shape[0]
    tr = min(tr, t)

    def body(x_ref, al_ref, dt_ref, o_ref):
        x = x_ref[...]
        lane = lax.broadcasted_iota(jnp.int32, x.shape, 1)
        g = -jnp.exp(al_ref[...]) * _softplus(x + dt_ref[...])
        o_ref[...] = jnp.where(lane < nh, g, jnp.where(lane < 2 * nh, _sigmoid(x), 0.0))

    row = pl.BlockSpec((tr, LANES), lambda i: (i, 0))
    vec = pl.BlockSpec((1, LANES), lambda i: (0, 0))
    return pl.pallas_call(
        body, name=name, grid=(t // tr,), in_specs=[row, vec, vec], out_specs=row,
        out_shape=jax.ShapeDtypeStruct((t, LANES), F32), compiler_params=_cparams("parallel"),
    )(ab, alog_row, dt_row)


def _gates_bwd(ab, alog_row, dt_row, dgates, nh, name, tr=1024):
    t = ab.shape[0]
    tr = min(tr, t)
    npart = dgates.shape[0]

    def body(x_ref, al_ref, dt_ref, dg_ref, dx_ref, s_ref):
        x = x_ref[...]
        lane = lax.broadcasted_iota(jnp.int32, x.shape, 1)
        dgt = dg_ref[0]
        for p in range(1, npart):
            dgt = dgt + dg_ref[p]
        ea = jnp.exp(al_ref[...])
        xa = x + dt_ref[...]
        da = jnp.where(lane < nh, dgt * (-ea) * _sigmoid(xa), 0.0)
        beta = _sigmoid(x)
        db = jnp.where((lane >= nh) & (lane < 2 * nh), dgt * beta * (1.0 - beta), 0.0)
        dx_ref[...] = (da + db).astype(BF16)
        dal = jnp.sum(jnp.where(lane < nh, dgt * (-ea) * _softplus(xa), 0.0), axis=0, keepdims=True)
        ddt = jnp.sum(da, axis=0, keepdims=True)
        r8 = lax.broadcasted_iota(jnp.int32, (8, LANES), 0)
        part = jnp.where(r8 == 0, dal, jnp.where(r8 == 1, ddt, 0.0))

        @pl.when(pl.program_id(0) == 0)
        def _():
            s_ref[...] = part

        @pl.when(pl.program_id(0) > 0)
        def _():
            s_ref[...] += part

    row = pl.BlockSpec((tr, LANES), lambda i: (i, 0))
    vec = pl.BlockSpec((1, LANES), lambda i: (0, 0))
    return pl.pallas_call(
        body, name=name, grid=(t // tr,),
        in_specs=[row, vec, vec, pl.BlockSpec((npart, tr, LANES), lambda i: (0, i, 0))],
        out_specs=[row, pl.BlockSpec((8, LANES), lambda i: (0, 0))],
        out_shape=[jax.ShapeDtypeStruct((t, LANES), BF16), jax.ShapeDtypeStruct((8, LANES), F32)],
        compiler_params=_cparams("arbitrary"),
    )(ab, alog_row, dt_row, dgates)


HALO = 8


def _delayed(ext, rows):
    return [ext[HALO:HALO + rows]] + [pltpu.roll(ext, s, 0)[HALO:HALO + rows] for s in range(1, CONV_K)]


def _conv_taps(delayed, w):
    acc = delayed[0] * w[CONV_K - 1:CONV_K]
    for s in range(1, CONV_K):
        acc = acc + delayed[s] * w[CONV_K - 1 - s:CONV_K - s]
    return acc


def _conv_fwd(proj, conv_w, col0, inner, mode, name, tt=CONV_ROWS, hb=CONV_HEADS):
    t = proj.shape[0]
    tt = min(tt, t)
    hb = min(hb, inner // HEAD_DIM)
    wc = hb * HEAD_DIM
    c0 = col0 // wc
    hpb = tt // HALO

    def body(x_ref, halo_ref, w_ref, o_ref):
        halo = jnp.where(pl.program_id(1) == 0, 0.0, halo_ref[...])
        s = _silu(_conv_taps(_delayed(jnp.concatenate([halo, x_ref[...]], axis=0), tt), w_ref[...]))
        if mode == "v":
            o_ref[...] = s
        else:
            mul = HEAD_DIM ** -0.5 if mode == "q" else 1.0
            o_ref[...] = jnp.concatenate(
                [sh * (lax.rsqrt(jnp.sum(sh * sh, axis=-1, keepdims=True) + EPS) * mul) for sh in _heads_of(s, hb)], axis=1)

    return pl.pallas_call(
        body, name=name, grid=(inner // wc, t // tt),
        in_specs=[pl.BlockSpec((tt, wc), lambda j, i: (i, j + c0)),
                  pl.BlockSpec((HALO, wc), lambda j, i: (jnp.maximum(i * hpb - 1, 0), j + c0)),
                  pl.BlockSpec((CONV_K, wc), lambda j, i: (0, j + c0))],
        out_specs=pl.BlockSpec((tt, wc), lambda j, i: (i, j)),
        out_shape=jax.ShapeDtypeStruct((t, inner), F32),
        compiler_params=_cparams("parallel", "parallel"),
    )(proj, proj, conv_w)


def _conv_bwd(dy, proj, conv_w, col0, inner, mode, name, tt=CONV_ROWS, hb=CONV_HEADS, dest=None):
    t = proj.shape[0]
    tt = min(tt, t)
    nt = t // tt
    hb = min(hb, inner // HEAD_DIM)
    wc = hb * HEAD_DIM
    c0 = col0 // wc
    hpb = tt // HALO
    rows = tt + HALO

    def body(dy_ref, dyn_ref, x_ref, xp_ref, xn_ref, w_ref, dx_ref, dw_ref):
        i = pl.program_id(1)
        w = w_ref[...]
        xprev = jnp.where(i == 0, 0.0, xp_ref[...])
        delayed = _delayed(jnp.concatenate([xprev, x_ref[...], xn_ref[...]], axis=0), rows)
        c = _conv_taps(delayed, w)
        dyv = jnp.concatenate([dy_ref[...], jnp.where(i == nt - 1, 0.0, dyn_ref[...])], axis=0)
        sg = _sigmoid(c)
        s = c * sg
        if mode == "v":
            ds = dyv
        else:
            mul = HEAD_DIM ** -0.5 if mode == "q" else 1.0
            parts = []
            for dyh, sh in zip(_heads_of(dyv, hb), _heads_of(s, hb)):
                r = lax.rsqrt(jnp.sum(sh * sh, axis=-1, keepdims=True) + EPS)
                parts.append(mul * (r * dyh - sh * (r * r * r) * jnp.sum(dyh * sh, axis=-1, keepdims=True)))
            ds = jnp.concatenate(parts, axis=1)
        dc = ds * (sg * (1.0 + c * (1.0 - sg)))
        dx = dc[:tt] * w[CONV_K - 1:CONV_K]
        for sft in range(1, CONV_K):
            dx = dx + pltpu.roll(dc, rows - sft, 0)[:tt] * w[CONV_K - 1 - sft:CONV_K - sft]
        dx_ref[...] = dx.astype(BF16)
        r8 = lax.broadcasted_iota(jnp.int32, (8, wc), 0)
        part = jnp.zeros((8, wc), F32)
        for sft in range(CONV_K):
            part = part + jnp.where(r8 == CONV_K - 1 - sft,
                                    jnp.sum(dc[:tt] * delayed[sft][:tt], axis=0, keepdims=True), 0.0)

        @pl.when(i == 0)
        def _():
            dw_ref[...] = part

        @pl.when(i > 0)
        def _():
            dw_ref[...] += part

    cur = lambda off: pl.BlockSpec((tt, wc), lambda j, i: (i, j + off))
    nxt = lambda off: pl.BlockSpec((HALO, wc), lambda j, i: (jnp.minimum((i + 1) * hpb, t // HALO - 1), j + off))
    width, out0, more, more_specs, aliases = _window(dest, inner, 6, 0)
    return pl.pallas_call(
        _skip_ref(body, 6, len(more)), name=name, grid=(inner // wc, nt),
        in_specs=[cur(0), nxt(0), cur(c0),
                  pl.BlockSpec((HALO, wc), lambda j, i: (jnp.maximum(i * hpb - 1, 0), j + c0)), nxt(c0),
                  pl.BlockSpec((CONV_K, wc), lambda j, i: (0, j + c0))] + more_specs,
        out_specs=[pl.BlockSpec((tt, wc), lambda j, i: (i, j + out0 // wc)),
                   pl.BlockSpec((8, wc), lambda j, i: (0, j))],
        out_shape=[jax.ShapeDtypeStruct((t, width), BF16), jax.ShapeDtypeStruct((8, inner), F32)],
        input_output_aliases=aliases, compiler_params=_cparams("parallel", "arbitrary"),
    )(dy, dy, proj, proj, proj, conv_w, *more)


GDN_HB = 4
GDN_NB = 8
SCAN_HB = 16
SCAN_NB = 4


def _iota2(n, m):
    return lax.broadcasted_iota(jnp.int32, (n, m), 0), lax.broadcasted_iota(jnp.int32, (n, m), 1)


def _head_select(first_head, hb, lane0):
    r, lane = _iota2(8, LANES)
    return jnp.where((r < hb) & (lane == lane0 + first_head + r), 1.0, 0.0).astype(F32)


def _chunk_gates(gt, selg, selb):
    i, j = _iota2(CHUNK, CHUNK)
    gc_all = _dot_exact(jnp.where(j <= i, 1.0, 0.0), gt, NN, True)
    return (_dot_exact(gc_all, selg, NT, False), _dot_exact(selg, gc_all, NT, True),
            _dot_exact(gt, selb, NT, False))


def _decay_terms(gcol, grow):
    i, j = _iota2(CHUNK, CHUNK)
    glast = gcol[:, CHUNK - 1:CHUNK, :]
    decay = jnp.exp(jnp.where(j <= i, gcol - grow, NEG_BIG))
    return jnp.exp(gcol), jnp.exp(glast - gcol), jnp.exp(glast), decay


def _unit_lower_inverse(a):
    i, j = _iota2(CHUNK, CHUNK)
    same16 = (i // 16) == (j // 16)
    same32 = (i // 32) == (j // 32)
    m = jnp.where(same16, -a, 0.0)
    x = jnp.where(i == j, 1.0, 0.0) + m
    for _ in range(3):
        m = _dot3(m, m, BNN)
        x = x + _dot3(x, m, BNN)
    for off in (jnp.where(same32 & jnp.logical_not(same16), a, 0.0), jnp.where(same32, 0.0, a)):
        x = x - _dot3(_dot3(x, off, BNN), x, BNN)
    return x


def _unit_inputs(refs, g_ref, selg, selb, hb, nb):
    units = [(c, h) for c in range(nb) for h in range(hb)]
    rs = lambda c: slice(c * CHUNK, (c + 1) * CHUNK)
    cs = lambda h: slice(h * HEAD_DIM, (h + 1) * HEAD_DIM)
    gates = [_chunk_gates(g_ref[rs(c), :], selg, selb) for c in range(nb)]
    stacked = [jnp.stack([r[rs(c), cs(h)] for c, h in units]) for r in refs]
    gcol = jnp.stack([gates[c][0][:, h:h + 1] for c, h in units])
    grow = jnp.stack([gates[c][1][h:h + 1, :] for c, h in units])
    bcol = jnp.stack([gates[c][2][:, h:h + 1] for c, h in units])
    return units, rs, cs, stacked, gcol, grow, bcol


def _gdn_specs(nh, inner, t, heads=GDN_HB, chunks=GDN_NB):
    hb, nb = min(heads, nh), chunks
    rows = nb * CHUNK
    wide = pl.BlockSpec((rows, hb * HEAD_DIM), lambda g, n: (n, g))
    sq = pl.BlockSpec((hb, rows, CHUNK), lambda g, n: (g, n, 0))
    gts = pl.BlockSpec((rows, LANES), lambda g, n: (n, 0))
    glb = pl.BlockSpec((nb * 8, hb * HEAD_DIM), lambda g, n: (n, g))
    return hb, nb, rows, wide, sq, gts, glb


def _gdn_intra_fwd(q, k, v, gates, nh, name, comm=None):
    t, inner = q.shape
    hb, nb, rows, wide, sq, gts, glb = _gdn_specs(nh, inner, t)

    def body(q_ref, k_ref, v_ref, g_ref, qe_ref, kel_ref, wb_ref, w_ref, u_ref, qk_ref, tm_ref, gl_ref):
        first = pl.program_id(0) * hb
        selg, selb = _head_select(first, hb, 0), _head_select(first, hb, nh)
        i, j = _iota2(CHUNK, CHUNK)
        units, rs, cs, (qv, kv, vv), gcol, grow, bcol = _unit_inputs(
            (q_ref, k_ref, v_ref), g_ref, selg, selb, hb, nb)
        e, el, gl, decay = _decay_terms(gcol, grow)
        kb = kv * bcol
        qbf, kbf = _bf(qv), _bf(kv)
        a = jnp.where(j < i, _dot(_bf(kb), kbf, BNT) * decay, 0.0)
        tm = _unit_lower_inverse(a)
        uw = _dot3(tm, jnp.concatenate([vv * bcol, kb * e], axis=2), BNN)
        qk = _bf(_dot(qbf, kbf, BNT) * decay)
        qe, kel = _bf(qv * e), _bf(kv * el)
        for n, (c, h) in enumerate(units):
            w = uw[n, :, HEAD_DIM:]
            qe_ref[rs(c), cs(h)] = qe[n]
            kel_ref[rs(c), cs(h)] = kel[n]
            wb_ref[rs(c), cs(h)] = _bf(w)
            w_ref[rs(c), cs(h)] = w
            u_ref[rs(c), cs(h)] = uw[n, :, :HEAD_DIM]
            qk_ref[h, rs(c), :] = qk[n]
            tm_ref[h, rs(c), :] = tm[n]
            gl_ref[c * 8:(c + 1) * 8, cs(h)] = jnp.broadcast_to(gl[n], (8, HEAD_DIM))

    big = lambda dt: jax.ShapeDtypeStruct((t, inner), dt)
    return _grid_call(
        body, name=name, grid=(nh // hb, t // rows),
        in_specs=[wide, wide, wide, gts],
        out_specs=[wide] * 5 + [sq, sq, glb],
        out_shape=[big(BF16), big(BF16), big(BF16), big(F32), big(F32),
                   jax.ShapeDtypeStruct((nh, t, CHUNK), BF16), jax.ShapeDtypeStruct((nh, t, CHUNK), F32),
                   jax.ShapeDtypeStruct((t // CHUNK * 8, inner), F32)],
        args=(q, k, v, gates), semantics=("parallel", "parallel"), comm=comm)


def _gdn_scan_fwd(qe, kel, wb, u, qk, glb, nh, name):
    t, inner = u.shape
    hb, nb, rows, wide, sq, _, glb_spec = _gdn_specs(nh, inner, t, SCAN_HB, SCAN_NB)

    def body(qe_ref, kel_ref, wb_ref, u_ref, qk_ref, gl_ref, o_ref, vn_ref, sall_ref, s_ref):
        @pl.when(pl.program_id(1) == 0)
        def _():
            s_ref[...] = jnp.zeros(s_ref.shape, F32)

        cs = lambda h: slice(h * HEAD_DIM, (h + 1) * HEAD_DIM)
        for c in range(nb):
            rs = slice(c * CHUNK, (c + 1) * CHUNK)
            heads = lambda ref: jnp.stack([ref[rs, cs(h)] for h in range(hb)])
            s = s_ref[...]
            sall_ref[c] = s
            sb = _bf(s)
            vn = heads(u_ref) - _dot(heads(wb_ref), sb, BNN)
            vnb = _bf(vn)
            o = _dot(heads(qe_ref), sb, BNN) + _dot(qk_ref[:, rs, :], vnb, BNN)
            gl = jnp.stack([gl_ref[c * 8:c * 8 + 1, cs(h)] for h in range(hb)])
            s_ref[...] = s * gl + _dot(heads(kel_ref), vnb, BTN)
            for h in range(hb):
                vn_ref[rs, cs(h)] = vn[h]
                o_ref[rs, cs(h)] = o[h]

    return pl.pallas_call(
        body, name=name, grid=(nh // hb, t // rows),
        in_specs=[wide, wide, wide, wide, sq, glb_spec],
        out_specs=[wide, wide, pl.BlockSpec((nb, hb, HEAD_DIM, HEAD_DIM), lambda g, n: (n, g, 0, 0))],
        out_shape=[jax.ShapeDtypeStruct((t, inner), F32), jax.ShapeDtypeStruct((t, inner), F32),
                   jax.ShapeDtypeStruct((t // CHUNK, nh, HEAD_DIM, HEAD_DIM), F32)],
        scratch_shapes=[pltpu.VMEM((hb, HEAD_DIM, HEAD_DIM), F32)],
        compiler_params=_cparams("parallel", "arbitrary"),
    )(qe, kel, wb, u, qk, glb)


def _gdn_scan_bwd(do, qe, kel, wb, vn, qk, glb, sall, nh, name, comm=None):
    t, inner = do.shape
    hb, nb, rows, _, _, _, _ = _gdn_specs(nh, inner, t, SCAN_HB, SCAN_NB)
    last = t // rows - 1
    wide = pl.BlockSpec((rows, hb * HEAD_DIM), lambda g, n: (last - n, g))
    sq = pl.BlockSpec((hb, rows, CHUNK), lambda g, n: (g, last - n, 0))
    glb_spec = pl.BlockSpec((nb * 8, hb * HEAD_DIM), lambda g, n: (last - n, g))

    def body(do_ref, qe_ref, kel_ref, wb_ref, vn_ref, qk_ref, gl_ref, sall_ref,
             dvn_ref, dw_ref, dqe_ref, dkel_ref, dqk_ref, dgl_ref, ds_ref):
        @pl.when(pl.program_id(1) == 0)
        def _():
            ds_ref[...] = jnp.zeros(ds_ref.shape, F32)

        cs = lambda h: slice(h * HEAD_DIM, (h + 1) * HEAD_DIM)
        for c in reversed(range(nb)):
            rs = slice(c * CHUNK, (c + 1) * CHUNK)
            heads = lambda ref: jnp.stack([ref[rs, cs(h)] for h in range(hb)])
            ds, s = ds_ref[...], sall_ref[c]
            dsb, sb = _bf(ds), _bf(s)
            dob, vnb = _bf(heads(do_ref)), _bf(heads(vn_ref))
            dvn = _dot(qk_ref[:, rs, :], dob, BTN) + _dot(heads(kel_ref), dsb, BNN)
            dvnb = _bf(dvn)
            dw = -_dot(dvnb, sb, BNT)
            dqe = _dot(dob, sb, BNT)
            dkel = _dot(vnb, dsb, BNT)
            dqk_ref[:, rs, :] = _dot(dob, vnb, BNT)
            dgl = jnp.sum(jnp.sum(ds * s, axis=2, keepdims=True), axis=1, keepdims=True)
            gl = jnp.stack([gl_ref[c * 8:c * 8 + 1, cs(h)] for h in range(hb)])
            ds_ref[...] = ds * gl + _dot(heads(qe_ref), dob, BTN) - _dot(heads(wb_ref), dvnb, BTN)
            for h in range(hb):
                dvn_ref[rs, cs(h)] = dvn[h]
                dw_ref[rs, cs(h)] = dw[h]
                dqe_ref[rs, cs(h)] = dqe[h]
                dkel_ref[rs, cs(h)] = dkel[h]
                dgl_ref[c * 8:(c + 1) * 8, cs(h)] = jnp.broadcast_to(dgl[h], (8, HEAD_DIM))

    big = jax.ShapeDtypeStruct((t, inner), F32)
    return _grid_call(
        body, name=name, grid=(nh // hb, t // rows),
        in_specs=[wide, wide, wide, wide, wide, sq, glb_spec,
                  pl.BlockSpec((nb, hb, HEAD_DIM, HEAD_DIM), lambda g, n: (last - n, g, 0, 0))],
        out_specs=[wide] * 4 + [sq, glb_spec],
        out_shape=[big] * 4 + [jax.ShapeDtypeStruct((nh, t, CHUNK), F32),
                               jax.ShapeDtypeStruct((t // CHUNK * 8, inner), F32)],
        scratch_shapes=[pltpu.VMEM((hb, HEAD_DIM, HEAD_DIM), F32)],
        args=(do, qe, kel, wb, vn, qk, glb, sall), semantics=("parallel", "arbitrary"), comm=comm)


def _gdn_intra_bwd(q, k, v, gates, tm, w, u, dvn, dw, dqe, dkel, dqk, dglb, nh, name, comm=None):
    t, inner = q.shape
    hb, nb, rows, wide, sq, gts, glb = _gdn_specs(nh, inner, t)

    def body(q_ref, k_ref, v_ref, g_ref, tm_ref, w_ref, u_ref, dvn_ref, dw_ref, dqe_ref, dkel_ref, dqk_ref,
             dgl_ref, dq_ref, dk_ref, dv_ref, dg_ref):
        first = pl.program_id(0) * hb
        selg, selb = _head_select(first, hb, 0), _head_select(first, hb, nh)
        i, j = _iota2(CHUNK, CHUNK)
        lane8 = lax.broadcasted_iota(jnp.int32, (CHUNK, 8), 1)
        row = lax.broadcasted_iota(jnp.int32, (CHUNK, 1), 0)
        lower = jnp.where(j <= i, 1.0, 0.0).astype(F32)
        rsum = lambda x: jnp.sum(x, axis=-1, keepdims=True)
        units, rs, cs, (qv, kv, vv, wv, uv, dvn, dw, dqe, dkel), gcol, grow, bcol = _unit_inputs(
            (q_ref, k_ref, v_ref, w_ref, u_ref, dvn_ref, dw_ref, dqe_ref, dkel_ref), g_ref, selg, selb, hb, nb)
        nu = len(units)
        tmv = jnp.stack([tm_ref[h, rs(c), :] for c, h in units])
        dqk = jnp.where(j <= i, jnp.stack([dqk_ref[h, rs(c), :] for c, h in units]), 0.0)
        dgl = jnp.stack([dgl_ref[c * 8:c * 8 + 1, h * HEAD_DIM:h * HEAD_DIM + 1] for c, h in units])
        e, el, gl, decay = _decay_terms(gcol, grow)
        kb = kv * bcol
        qb, kbf, kbb = _bf(qv), _bf(kv), _bf(kb)
        dqkr = _bf(dqk * decay)
        dq = dqe * e + _dot(dqkr, kbf, BNN)
        dk = dkel * el + _dot(dqkr, qb, BTN)
        de = rsum(dqe * qv)
        del_ = rsum(dkel * kv)
        mq = dqk * _dot(qb, kbf, BNT) * decay
        dsol = _dot3(tmv, jnp.concatenate([dvn, dw], axis=2), BTN)
        dvb, dkbe = dsol[:, :, :HEAD_DIM], dsol[:, :, HEAD_DIM:]
        da = -jnp.where(j < i, _dot3(dsol, jnp.concatenate([uv, wv], axis=2), BNT), 0.0)
        dkk = _bf(da * decay)
        ma = da * _dot(kbb, kbf, BNT) * decay
        dkb = dkbe * e + _dot(dkk, kbf, BNN)
        de = de + rsum(dkbe * kb)
        dk = dk + _dot(dkk, kbb, BTN) + dkb * bcol
        dv = dvb * bcol
        dbeta = rsum(dkb * kv) + rsum(dvb * vv)
        m = mq + ma
        ones = jnp.ones((nu, CHUNK, LANES), F32)
        dgc = rsum(m) - _dot_exact(m, ones, BTN, False)[:, :, 0:1] + de * e - del_ * el
        tail = jnp.sum(del_ * el, axis=1, keepdims=True) + dgl * gl
        dgc = dgc + jnp.where(row == CHUNK - 1, tail, 0.0)
        for n, (c, h) in enumerate(units):
            dq_ref[rs(c), cs(h)] = dq[n]
            dk_ref[rs(c), cs(h)] = dk[n]
            dv_ref[rs(c), cs(h)] = dv[n]
        for c in range(nb):
            dgc_cols = jnp.zeros((CHUNK, 8), F32)
            dbeta_cols = jnp.zeros((CHUNK, 8), F32)
            for h in range(hb):
                dgc_cols = jnp.where(lane8 == h, dgc[c * hb + h], dgc_cols)
                dbeta_cols = jnp.where(lane8 == h, dbeta[c * hb + h], dbeta_cols)
            dg_cols = _dot_exact(lower, dgc_cols, TN, True)
            dg_ref[rs(c), :] = _dot_exact(dg_cols, selg, NN, False) + _dot_exact(dbeta_cols, selb, NN, False)

    big = jax.ShapeDtypeStruct((t, inner), F32)
    return _grid_call(
        body, name=name, grid=(nh // hb, t // rows),
        in_specs=[wide, wide, wide, gts, sq, wide, wide, wide, wide, wide, wide, sq, glb],
        out_specs=[wide, wide, wide, pl.BlockSpec((None, rows, LANES), lambda g, n: (g, n, 0))],
        out_shape=[big, big, big, jax.ShapeDtypeStruct((nh // hb, t, LANES), F32)],
        args=(q, k, v, gates, tm, w, u, dvn, dw, dqe, dkel, dqk, dglb), semantics=("parallel", "parallel"),
        comm=comm)


def _layer_a_fwd(x, hn, w_in_t, w_ab_t, conv_w, alog_row, dt_row, onw, nh, comm, w_out_of):
    t, d = x.shape
    inner = nh * HEAD_DIM
    proj = _mm(hn, w_in_t, "nt", t, 4 * inner, d, out_dtype=F32, name="a_proj")
    ab = _mm(hn, w_ab_t, "nt", t, LANES, d, out_dtype=F32, name="a_proj_ab")
    gates = _gates_fwd(ab, alog_row, dt_row, nh, "a_gates")
    q = _conv_fwd(proj, conv_w, 0, inner, "q", "a_conv_q")
    k = _conv_fwd(proj, conv_w, inner, inner, "k", "a_conv_k")
    v = _conv_fwd(proj, conv_w, 2 * inner, inner, "v", "a_conv_v")
    qe, kel, wb, w, u, qk, tm, glb, *carried = _gdn_intra_fwd(q, k, v, gates, nh, "a_intra", comm)
    o, vn, sall = _gdn_scan_fwd(qe, kel, wb, u, qk, glb, nh, "a_scan")
    g = _gate_fwd(o, proj, 3 * inner, inner, "a_gate", norm_w=onw)
    w_out = w_out_of(carried)
    h1 = _mm(g, w_out, "nn", t, d, inner, out_dtype=F32, name="a_out", res=x)
    return h1, (hn, proj, ab, gates, q, k, v, qe, kel, wb, w, u, qk, tm, glb, o, vn, sall, g), w_out, carried


def _layer_a_bwd(dh1, dh1b, x, nw, w_in_t, w_ab_t, conv_w, alog_row, dt_row, onw, w_out, nh, saved, comms_of,
                 own_comm):
    hn, proj, ab, gates, q, k, v, qe, kel, wb, w, u, qk, tm, glb, o, vn, sall, g = saved
    t, d = x.shape
    inner = w_out.shape[0]
    dg = _mm(dh1b, w_out, "nt", t, inner, d, out_dtype=F32, name="a_dgate")
    dw_out = _mm(g, dh1b, "tn", inner, d, t, out_dtype=BF16, name="a_dwout")
    comm_scan, comm_intra = comms_of(dw_out)
    do, dproj, donw = _gate_bwd(dg, o, proj, 3 * inner, inner, "a_gate_bwd", F32, norm_w=onw,
                                dest=(None, 3 * inner, 4 * inner))
    dvn, dw, dqe, dkel, dqk, dglb, *carried_scan = _gdn_scan_bwd(do, qe, kel, wb, vn, qk, glb, sall, nh,
                                                                 "a_scan_bwd", comm_scan)
    dq, dk, dv, dgates, *carried = _gdn_intra_bwd(q, k, v, gates, tm, w, u, dvn, dw, dqe, dkel, dqk, dglb, nh,
                                                  "a_intra_bwd", comm_intra)
    carried = carried_scan + carried
    dproj, dcq = _conv_bwd(dq, proj, conv_w, 0, inner, "q", "a_conv_q_bwd", dest=(dproj, 0, 4 * inner))
    dproj, dck = _conv_bwd(dk, proj, conv_w, inner, inner, "k", "a_conv_k_bwd", dest=(dproj, inner, 4 * inner))
    dproj, dcv = _conv_bwd(dv, proj, conv_w, 2 * inner, inner, "v", "a_conv_v_bwd",
                           dest=(dproj, 2 * inner, 4 * inner))
    dab, dsmall = _gates_bwd(ab, alog_row, dt_row, dgates, nh, "a_gates_bwd")
    dw_in_t = _mm(dproj, hn, "tn", 4 * inner, d, t, out_dtype=BF16, name="a_dwin", **LONG_K_TILES)
    dw_ab_t = _mm(dab, hn, "tn", LANES, d, t, out_dtype=BF16, name="a_dwin_ab")
    dconv = jnp.concatenate([dcq[:CONV_K], dck[:CONV_K], dcv[:CONV_K]], axis=1)
    dhn = _mm(dab, w_ab_t, "nn", t, d, LANES, out_dtype=F32, name="a_dhn_ab")
    own = own_comm(dw_in_t, dw_ab_t, dconv)
    dhn = _mm(dproj, w_in_t, "nn", t, d, 4 * inner, out_dtype=F32, name="a_dhn", res=dhn, comm=own,
              **LONG_K_TILES)
    dhn, carried_own = dhn if own is not None else (dhn, [])
    dx, _, dnw = _rms_bwd(x, nw, dhn, dh1, "a_rms_bwd")
    return dx, dnw, dsmall, donw, carried, carried_own


def _rows_of(a, rows):
    flat = a.reshape(-1)
    return jnp.pad(flat, (0, rows * LANES - flat.shape[0])).reshape(rows, LANES)


def _to_slabs(g, axis):
    shape = g.shape[:axis] + (N_DEV, g.shape[axis] // N_DEV) + g.shape[axis + 1:]
    return jnp.moveaxis(g.reshape(shape), axis, 0)


def _from_slabs(s, axis):
    m = jnp.moveaxis(s, 0, axis)
    return m.reshape(m.shape[:axis] + (m.shape[axis] * m.shape[axis + 1],) + m.shape[axis + 2:])


def kernel(x, norm_w, a_w_in, a_conv_w, a_a_log, a_dt_bias, a_out_norm_w, a_w_out, b_w_in, b_q_norm_w, b_k_norm_w, b_rel_bias, b_w_out, loss_target, m_norm_w, m_a_w_in, m_a_conv_w, m_a_a_log, m_a_dt_bias, m_a_out_norm_w, m_a_w_out, m_b_w_in, m_b_q_norm_w, m_b_k_norm_w, m_b_rel_bias, m_b_w_out, v_norm_w, v_a_w_in, v_a_conv_w, v_a_a_log, v_a_dt_bias, v_a_out_norm_w, v_a_w_out, v_b_w_in, v_b_q_norm_w, v_b_k_norm_w, v_b_rel_bias, v_b_w_out):
    xs, target = x[0], loss_target[0]
    nh = a_a_log.shape[-1]
    inner = N_DEV * a_w_out.shape[1]

    d = xs.shape[1]
    nw0, nw1 = norm_w[0:1], norm_w[1:2]
    hn0, (ga_in, g_conv) = _rms_fwd(
        xs, nw0, "a_rms", comm=_RoutedGather([a_w_in[0].T.astype(BF16), a_conv_w[0]]))
    wa_in_t = ga_in.reshape(-1, d)
    wa_ab_t = jnp.pad(wa_in_t[4 * inner:], ((0, LANES - 2 * nh), (0, 0)))
    conv_w = _from_slabs(g_conv, 1)
    alog_row = jnp.pad(a_a_log, ((0, 0), (0, LANES - nh)))
    dt_row = jnp.pad(a_dt_bias, ((0, 0), (0, LANES - nh)))

    h1, saved_a, wa_out, (gb_in, gb_out, _) = _layer_a_fwd(
        xs, hn0, wa_in_t, wa_ab_t, conv_w, alog_row, dt_row, a_out_norm_w, nh,
        _Comm("gather", [b_w_in[0].T.astype(BF16), b_w_out[0].astype(BF16), a_w_out[0].astype(BF16)]),
        lambda gathered: _from_slabs(gathered[2], 0))
    wb_in_t = gb_in.reshape(-1, d)
    wb_out = _from_slabs(gb_out, 0)
    bias = _bias_tiles(_pad_rel_bias(b_rel_bias[0]), "b_bias_tiles")
    (dh2, dh2b, loss_row), saved_b = _layer_b_fwd(h1, nw1, wb_in_t, b_q_norm_w, b_k_norm_w, bias, wb_out, target)

    dh1, dh1b, dnw1, dwb_in_t, dqw, dkw, drb, dwb_out = _layer_b_bwd(
        dh2, dh2b, h1, nw1, wb_in_t, b_q_norm_w, b_k_norm_w, bias, wb_out, saved_b)

    def exchange_early(dwa_out):
        return (_Comm("exchange", [_to_slabs(dwb_out, 0).astype(BF16), _to_slabs(dwa_out, 0).astype(BF16)]),
                _Comm("exchange", [dwb_in_t.reshape(N_DEV, -1, d).astype(BF16)]))

    def exchange_last(dwa_in_t, dwa_ab_t, dconv):
        full = jnp.concatenate([dwa_in_t, dwa_ab_t[:2 * nh]], axis=0)
        return _Comm("exchange", [full.reshape(N_DEV, -1, d).astype(BF16), _to_slabs(dconv, 1)])

    dx, dnw0, dsmall, donw, (pb_out, pa_out, pb_in), (pa_in, p_conv) = _layer_a_bwd(
        dh1, dh1b, xs, nw0, wa_in_t, wa_ab_t, conv_w, alog_row, dt_row, a_out_norm_w, wa_out, nh, saved_a,
        exchange_early, exchange_last)
    big = {}
    for name, p, w, m, v in (("a_w_in", pa_in, a_w_in, m_a_w_in, v_a_w_in),
                             ("a_w_out", pa_out, a_w_out, m_a_w_out, v_a_w_out),
                             ("b_w_in", pb_in, b_w_in, m_b_w_in, v_b_w_in),
                             ("b_w_out", pb_out, b_w_out, m_b_w_out, v_b_w_out),
                             ("a_conv_w", p_conv, a_conv_w, m_a_conv_w, v_a_conv_w)):
        big[name] = [o[None] for o in _adamw(p, w[0], m[0], v[0], "adamw_" + name,
                                             transposed=name in ("a_w_in", "b_w_in"))]

    small = (("norm_w", norm_w, m_norm_w, v_norm_w, jnp.concatenate([dnw0, dnw1], axis=0)),
             ("a_a_log", a_a_log, m_a_a_log, v_a_a_log, dsmall[0:1, :nh]),
             ("a_dt_bias", a_dt_bias, m_a_dt_bias, v_a_dt_bias, dsmall[1:2, :nh]),
             ("a_out_norm_w", a_out_norm_w, m_a_out_norm_w, v_a_out_norm_w, donw),
             ("b_q_norm_w", b_q_norm_w, m_b_q_norm_w, v_b_q_norm_w, dqw),
             ("b_k_norm_w", b_k_norm_w, m_b_k_norm_w, v_b_k_norm_w, dkw),
             ("b_rel_bias", b_rel_bias, m_b_rel_bias, v_b_rel_bias, drb))
    rows = [8 * (-(-w.size // (8 * LANES))) for _, w, _, _, _ in small]
    pack = lambda arrs: jnp.concatenate([_rows_of(a, r) for a, r in zip(arrs, rows)] + [jnp.zeros((8, LANES), F32)], axis=0)
    g_pack = jnp.concatenate([_rows_of(g, r) for (_, _, _, _, g), r in zip(small, rows)]
                             + [jnp.broadcast_to(loss_row, (8, LANES))], axis=0)
    (g_all,) = _comm_call(_Comm("gather", [g_pack]), "gather_small_grads")
    outs_small = _adamw(g_all, pack([s[1] for s in small]), pack([s[2] for s in small]),
                        pack([s[3] for s in small]), "adamw_small")
    start = 0
    for (name, w, _, _, _), r in zip(small, rows):
        big[name] = [o[start:start + r].reshape(-1)[:w.size].reshape(w.shape) for o in outs_small]
        start += r
    loss = outs_small[0][start, 0]

    order = ("norm_w", "a_w_in", "a_conv_w", "a_a_log", "a_dt_bias", "a_out_norm_w", "a_w_out", "b_w_in",
             "b_q_norm_w", "b_k_norm_w", "b_rel_bias", "b_w_out")
    return (loss, dx[None]) + tuple(big[n][i] for i in range(4) for n in order)
```

```python
import functools

import jax
import jax.numpy as jnp
from jax import lax
from jax.experimental import pallas as pl
from jax.experimental.pallas import tpu as pltpu

F32 = jnp.float32
BF16 = jnp.bfloat16
MESH_IDS = pl.DeviceIdType.MESH
N_DEV = 8
CHUNK = 64
HEAD_DIM = 128
EPS = 1e-6
CONV_K = 4
LEFT_CHUNKS = 8
REL_CLIP = 256
Q_TILE = LEFT_CHUNKS * CHUNK
ADAM_LR = 0.001
ADAM_B1 = 0.9
ADAM_B2 = 0.999
ADAM_EPS = 1e-08
ADAM_WD = 0.01
ADAM_STEP = 10
NEG_BIG = -1e30
VMEM_LIMIT_BYTES = 56 * 1024 * 1024
HIGHEST = lax.Precision.HIGHEST
ANY = pl.BlockSpec(memory_space=pl.ANY)


def _cparams(*sem):
    return pltpu.CompilerParams(dimension_semantics=tuple(sem), vmem_limit_bytes=VMEM_LIMIT_BYTES)


NN, NT, TN = (((1,), (0,)), ((), ())), (((1,), (1,)), ((), ())), (((0,), (0,)), ((), ()))
BNN, BNT, BTN = (((2,), (1,)), ((0,), (0,))), (((2,), (2,)), ((0,), (0,))), (((1,), (1,)), ((0,), (0,)))


def _dot(a, b, dims, precision=None):
    return lax.dot_general(a, b, dims, preferred_element_type=F32, precision=precision)


def _nn(a, b, precision=None):
    return _dot(a, b, NN, precision)


def _nt(a, b, precision=None):
    return _dot(a, b, NT, precision)


def _tn(a, b, precision=None):
    return _dot(a, b, TN, precision)


def _bf(x):
    return x.astype(BF16)


def _split(x, pieces=2):
    out = []
    for _ in range(pieces - 1):
        hi = x.astype(BF16)
        out.append(hi)
        x = x - hi.astype(F32)
    return out + [x.astype(BF16)]


def _dot3(a, b, dims):
    (ah, al), (bh, bl) = _split(a), _split(b)
    return _dot(ah, bh, dims) + (_dot(ah, bl, dims) + _dot(al, bh, dims))


def _dot_exact(a, b, dims, split_b):
    if split_b:
        a = a.astype(BF16)
        parts = [_dot(a, p, dims) for p in _split(b, 3)]
    else:
        b = b.astype(BF16)
        parts = [_dot(p, b, dims) for p in _split(a, 3)]
    return parts[0] + (parts[1] + parts[2])


def _sigmoid(x):
    return 0.5 * jnp.tanh(0.5 * x) + 0.5


def _silu(x):
    return x * _sigmoid(x)


def _dsilu(x):
    s = _sigmoid(x)
    return s * (1.0 + x * (1.0 - s))


def _my_pos():
    return lax.axis_index("x"), lax.axis_index("y"), lax.axis_index("c")


def _peers(x, y, c):
    def flip(v, f):
        return 1 - v if f else v

    return [(flip(x, kx), flip(y, ky), flip(c, kc)) for kx in (0, 1) for ky in (0, 1) for kc in (0, 1)][1:]


def _lin(p):
    return 4 * p[0] + 2 * p[1] + p[2]


class _Comm:
    def __init__(self, kind, arrays):
        self.kind, self.arrays, self.n = kind, list(arrays), len(arrays)

    def out_shape(self):
        lead = (N_DEV,) if self.kind == "gather" else ()
        return [jax.ShapeDtypeStruct(lead + a.shape, a.dtype) for a in self.arrays]

    def scratch(self):
        return [pltpu.SemaphoreType.DMA((7 * self.n,)), pltpu.SemaphoreType.DMA((7 * self.n,)),
                pltpu.SemaphoreType.DMA((self.n,))]

    def _copies(self, ins, outs, sems, arrivals):
        send_sems, recv_sems, local_sems = sems
        x, y, c = _my_pos()
        me = _lin((x, y, c))
        gather = self.kind == "gather"
        mine = [ins[t] if gather else ins[t].at[me] for t in range(self.n)]
        remote = []
        for k, peer in enumerate(_peers(x, y, c)):
            for t in range(self.n):
                if arrivals:
                    src, dst = mine[t], outs[t].at[_lin(peer)]
                else:
                    src, dst = (ins[t] if gather else ins[t].at[_lin(peer)]), outs[t].at[me]
                remote.append(pltpu.make_async_remote_copy(
                    src_ref=src, dst_ref=dst, send_sem=send_sems.at[k * self.n + t],
                    recv_sem=recv_sems.at[k * self.n + t], device_id=peer, device_id_type=MESH_IDS))
        if arrivals:
            return remote
        return [pltpu.make_async_copy(mine[t], outs[t].at[me], local_sems.at[t]) for t in range(self.n)], remote

    def start(self, ins, outs, sems):
        local, sends = self._copies(ins, outs, sems, False)
        for cp in local + sends:
            cp.start()

    def finish(self, ins, outs, sems):
        for cp in self._copies(ins, outs, sems, True):
            cp.wait_recv()
        local, sends = self._copies(ins, outs, sems, False)
        for cp in sends:
            cp.wait_send()
        for cp in local:
            cp.wait()


def _xor(a, b):
    return a + b - 2 * a * b


class _RoutedGather(_Comm):
    def __init__(self, arrays):
        super().__init__("gather", arrays)

    def _plan(self, outs, sems):
        send_sems, recv_sems, _ = sems
        x, y, c = _my_pos()
        sib, xn, yn, dg = (x, y, 1 - c), (1 - x, y, c), (x, 1 - y, c), (1 - x, 1 - y, c)
        via = (_xor(x, 1 - c), _xor(y, c), c)
        onto = (_xor(x, c), _xor(y, 1 - c), c)
        routes = [(None, sib, sib), (None, xn, xn), (None, yn, yn), (via, onto, dg),
                  (xn, sib, (1 - x, y, 1 - c)), (yn, sib, (x, 1 - y, 1 - c)), (dg, sib, (1 - x, 1 - y, 1 - c))]

        def copy(k, t, src, slot, target):
            return pltpu.make_async_remote_copy(
                src_ref=src, dst_ref=outs[t].at[slot], send_sem=send_sems.at[k * self.n + t],
                recv_sem=recv_sems.at[k * self.n + t], device_id=target, device_id_type=MESH_IDS)

        return (x, y, c), routes, copy

    def start(self, ins, outs, sems):
        me, routes, copy = self._plan(outs, sems)
        for t in range(self.n):
            pltpu.make_async_copy(ins[t], outs[t].at[_lin(me)], sems[2].at[t]).start()
            for k in range(3):
                copy(k, t, ins[t], _lin(me), routes[k][1]).start()

    def finish(self, ins, outs, sems):
        me, routes, copy = self._plan(outs, sems)

        def arrived(k):
            for t in range(self.n):
                copy(k, t, ins[t], _lin(routes[k][2]), me).wait_recv()

        def pass_on(k):
            for t in range(self.n):
                copy(k, t, outs[t].at[_lin(routes[k][0])], _lin(routes[k][0]), routes[k][1]).start()

        arrived(1)
        arrived(2)
        for k in (3, 4, 5):
            pass_on(k)
        arrived(3)
        pass_on(6)
        for k in (0, 4, 5, 6):
            arrived(k)
        for t in range(self.n):
            for k in range(7):
                src = ins[t] if k < 3 else outs[t].at[_lin(routes[k][0])]
                copy(k, t, src, _lin(me), routes[k][1]).wait_send()
            pltpu.make_async_copy(ins[t], outs[t].at[_lin(me)], sems[2].at[t]).wait()


def _comm_call(comm, name):
    n = comm.n

    def body(*refs):
        ins, outs, sems = refs[:n], refs[n:2 * n], refs[2 * n:]
        comm.start(ins, outs, sems)
        comm.finish(ins, outs, sems)

    return pl.pallas_call(
        body, name=name, out_shape=comm.out_shape(), in_specs=[ANY] * n, out_specs=[ANY] * n,
        scratch_shapes=comm.scratch(),
    )(*comm.arrays)


def _grid_call(body, *, name, grid, in_specs, out_specs, out_shape, args, scratch_shapes=(), semantics=None, comm=None):
    if comm is None:
        return pl.pallas_call(
            body, name=name, grid=grid, in_specs=in_specs, out_specs=out_specs, out_shape=out_shape,
            scratch_shapes=list(scratch_shapes), compiler_params=_cparams(*semantics),
        )(*args)
    n_in, n_out, n_sc, n = len(in_specs), len(out_specs), len(scratch_shapes), comm.n

    def full(*refs):
        ins, refs = refs[:n_in], refs[n_in:]
        cins, refs = refs[:n], refs[n:]
        outs, refs = refs[:n_out], refs[n_out:]
        couts, refs = refs[:n], refs[n:]
        scratch, sems = refs[:n_sc], refs[n_sc:]
        ids = [pl.program_id(a) for a in range(len(grid))]
        first = functools.reduce(jnp.logical_and, [i == 0 for i in ids])
        last = functools.reduce(jnp.logical_and, [i == g - 1 for i, g in zip(ids, grid)])

        @pl.when(first)
        def _():
            comm.start(cins, couts, sems)

        body(*ins, *outs, *scratch)

        @pl.when(last)
        def _():
            comm.finish(cins, couts, sems)

    return pl.pallas_call(
        full, name=name, grid=grid, in_specs=list(in_specs) + [ANY] * n, out_specs=list(out_specs) + [ANY] * n,
        out_shape=list(out_shape) + comm.out_shape(), scratch_shapes=list(scratch_shapes) + comm.scratch(),
        compiler_params=_cparams(*(["arbitrary"] * len(grid))),
    )(*(list(args) + comm.arrays))


LONG_K_TILES = dict(tm=512, tn=512, tk=8192)


def _mm(a, b, mode, m, n, k, *, out_dtype, name, tm=1024, tn=1024, tk=2048,
        a_m0=0, a_k0=0, b_n0=0, b_k0=0, res=None, comm=None, loss_target=None):
    tm, tn, tk = min(tm, m), min(tn, n), min(tk, k)
    nm, nn, nk = m // tm, n // tn, k // tk
    assert nm * tm == m and nn * tn == n and nk * tk == k
    am, ak, bn, bk = a_m0 // tm, a_k0 // tk, b_n0 // tn, b_k0 // tk
    assert am * tm == a_m0 and ak * tk == a_k0 and bn * tn == b_n0 and bk * tk == b_k0
    if mode == "tn":
        a_spec = pl.BlockSpec((tk, tm), lambda i, j, q: (q + ak, i + am))
        a_dims = (0,)
    else:
        a_spec = pl.BlockSpec((tm, tk), lambda i, j, q: (i + am, q + ak))
        a_dims = (1,)
    if mode == "nt":
        b_spec = pl.BlockSpec((tn, tk), lambda i, j, q: (j + bn, q + bk))
        b_dims = (1,)
    else:
        b_spec = pl.BlockSpec((tk, tn), lambda i, j, q: (q + bk, j + bn))
        b_dims = (0,)
    o_spec = pl.BlockSpec((tm, tn), lambda i, j, q: (i, j))
    has_res = res is not None
    has_loss = loss_target is not None
    n_in = 2 + has_res + has_loss
    n_out = 3 if has_loss else 1

    def body(*refs):
        a_ref, b_ref = refs[0], refs[1]
        res_ref = refs[2] if has_res else None
        o_ref = refs[n_in]
        p = _dot(a_ref[...], b_ref[...], ((a_dims, b_dims), ((), ())))

        def finish(total):
            if has_res:
                total = total + res_ref[...].astype(F32)
            if not has_loss:
                o_ref[...] = total.astype(out_dtype)
                return
            err = total - refs[n_in - 1][...]
            grad = err * (1.0 / n)
            o_ref[...] = grad
            refs[n_in + 1][...] = grad.astype(BF16)
            l_ref = refs[n_in + 2]
            part = jnp.zeros((1, LANES), F32) + 0.5 * jnp.sum(err * err) * (1.0 / n)
            first = (pl.program_id(0) == 0) & (pl.program_id(1) == 0)

            @pl.when(first)
            def _():
                l_ref[...] = part

            @pl.when(jnp.logical_not(first))
            def _():
                l_ref[...] += part

        if nk == 1:
            finish(p)
        else:
            acc_ref = refs[n_in + n_out]
            q = pl.program_id(2)

            @pl.when(q == 0)
            def _():
                acc_ref[...] = p

            @pl.when(q > 0)
            def _():
                acc_ref[...] += p

            @pl.when(q == nk - 1)
            def _():
                finish(acc_ref[...])

    extra_in = ([res] if has_res else []) + ([loss_target] if has_loss else [])
    if has_loss:
        return _grid_call(
            body, name=name, grid=(nm, nn, nk), in_specs=[a_spec, b_spec] + [o_spec] * len(extra_in),
            out_specs=[o_spec, o_spec, pl.BlockSpec((1, LANES), lambda i, j, q: (0, 0))],
            out_shape=[jax.ShapeDtypeStruct((m, n), F32), jax.ShapeDtypeStruct((m, n), BF16),
                       jax.ShapeDtypeStruct((1, LANES), F32)],
            scratch_shapes=[pltpu.VMEM((tm, tn), F32)] if nk > 1 else [],
            args=[a, b] + extra_in, semantics=("arbitrary", "arbitrary", "arbitrary"))
    out, *carried = _grid_call(
        body, name=name, grid=(nm, nn, nk),
        in_specs=[a_spec, b_spec] + [o_spec] * len(extra_in),
        out_specs=[o_spec], out_shape=[jax.ShapeDtypeStruct((m, n), out_dtype)],
        scratch_shapes=[pltpu.VMEM((tm, tn), F32)] if nk > 1 else [],
        args=[a, b] + extra_in, semantics=("parallel", "parallel", "arbitrary"), comm=comm)
    return out if comm is None else (out, carried)


def _rms_fwd(x, w, name, tr=512, comm=None):
    t, d = x.shape
    tr = min(tr, t)

    def body(x_ref, w_ref, o_ref):
        xv = x_ref[...]
        r = lax.rsqrt(jnp.mean(xv * xv, axis=-1, keepdims=True) + EPS)
        o_ref[...] = (xv * r * w_ref[...]).astype(BF16)

    out, *carried = _grid_call(
        body, name=name, grid=(t // tr,),
        in_specs=[pl.BlockSpec((tr, d), lambda i: (i, 0)), pl.BlockSpec((1, d), lambda i: (0, 0))],
        out_specs=[pl.BlockSpec((tr, d), lambda i: (i, 0))],
        out_shape=[jax.ShapeDtypeStruct((t, d), BF16)], args=(x, w), semantics=("parallel",), comm=comm)
    return out if comm is None else (out, carried)


def _rms_bwd(x, w, dy, dres, name, tr=512):
    t, d = x.shape
    tr = min(tr, t)

    def body(x_ref, w_ref, dy_ref, dres_ref, dx_ref, dxb_ref, dw_ref):
        xv = x_ref[...]
        dyv = dy_ref[...].astype(F32)
        r = lax.rsqrt(jnp.mean(xv * xv, axis=-1, keepdims=True) + EPS)
        gy = dyv * w_ref[...]
        proj = jnp.sum(gy * xv, axis=-1, keepdims=True) * (1.0 / d)
        dx = dres_ref[...] + r * gy - xv * (r * r * r) * proj
        dx_ref[...] = dx
        dxb_ref[...] = dx.astype(BF16)
        part = jnp.sum(dyv * xv * r, axis=0, keepdims=True)

        @pl.when(pl.program_id(0) == 0)
        def _():
            dw_ref[...] = part

        @pl.when(pl.program_id(0) > 0)
        def _():
            dw_ref[...] += part

    row = pl.BlockSpec((tr, d), lambda i: (i, 0))
    vec = pl.BlockSpec((1, d), lambda i: (0, 0))
    return pl.pallas_call(
        body, name=name, grid=(t // tr,),
        in_specs=[row, vec, row, row], out_specs=[row, row, vec],
        out_shape=[jax.ShapeDtypeStruct((t, d), F32), jax.ShapeDtypeStruct((t, d), BF16),
                   jax.ShapeDtypeStruct((1, d), F32)],
        compiler_params=_cparams("arbitrary"),
    )(x, w, dy, dres)


def _adamw(parts, w, m, v, name, tr=128, transposed=False):
    r, c = w.shape
    tr = tr if r % tr == 0 else r
    c1 = 1.0 - ADAM_B1 ** ADAM_STEP
    c2 = 1.0 - ADAM_B2 ** ADAM_STEP

    def body(p_ref, w_ref, m_ref, v_ref, g_ref, d_ref, nm_ref, nv_ref):
        g = p_ref[0].astype(F32)
        for s in range(1, N_DEV):
            g = g + p_ref[s].astype(F32)
        if transposed:
            i, j = lax.broadcasted_iota(jnp.int32, (tr, tr), 0), lax.broadcasted_iota(jnp.int32, (tr, tr), 1)
            g = _dot_exact(jnp.where(i == j, 1.0, 0.0), g, NT, True)
        nm = ADAM_B1 * m_ref[...] + (1.0 - ADAM_B1) * g
        nv = ADAM_B2 * v_ref[...] + (1.0 - ADAM_B2) * (g * g)
        m_hat = nm / c1
        v_hat = nv / c2
        g_ref[...] = g
        d_ref[...] = -ADAM_LR * (m_hat / (jnp.sqrt(v_hat) + ADAM_EPS) + ADAM_WD * w_ref[...])
        nm_ref[...] = nm
        nv_ref[...] = nv

    blk = pl.BlockSpec((tr, c), lambda i: (i, 0))
    p_spec = (pl.BlockSpec((N_DEV, c, tr), lambda i: (0, 0, i)) if transposed
              else pl.BlockSpec((N_DEV, tr, c), lambda i: (0, i, 0)))
    return pl.pallas_call(
        body, name=name, grid=(r // tr,),
        in_specs=[p_spec, blk, blk, blk],
        out_specs=[blk] * 4, out_shape=[jax.ShapeDtypeStruct((r, c), F32)] * 4,
        compiler_params=_cparams("parallel"),
    )(parts, w, m, v)


ROW_TILE, ROW_HEADS = 512, 16
CONV_ROWS, CONV_HEADS = 512, 8


def _window(dest, inner, n_in, out_index):
    if dest is None:
        return inner, 0, [], [], {}
    buf, col0, total = dest
    if buf is None:
        return total, col0, [], [], {}
    return total, col0, [buf], [ANY], {n_in: out_index}


def _skip_ref(body, at, count):
    return body if count == 0 else (lambda *refs: body(*refs[:at], *refs[at + count:]))


def _heads_of(x, nh):
    return [x[:, h * HEAD_DIM:(h + 1) * HEAD_DIM] for h in range(nh)]


def _headnorm_fwd(proj, w, col0, inner, name, tr=ROW_TILE, hb=ROW_HEADS):
    t = proj.shape[0]
    tr = min(tr, t)
    hb = min(hb, inner // HEAD_DIM)
    wc = hb * HEAD_DIM
    c0 = col0 // wc

    def body(x_ref, w_ref, o_ref):
        outs = []
        for xh in _heads_of(x_ref[...], hb):
            r = lax.rsqrt(jnp.mean(xh * xh, axis=-1, keepdims=True) + EPS)
            outs.append((xh * r * w_ref[...]).astype(BF16))
        o_ref[...] = jnp.concatenate(outs, axis=1)

    return pl.pallas_call(
        body, name=name, grid=(t // tr, inner // wc),
        in_specs=[pl.BlockSpec((tr, wc), lambda i, j: (i, j + c0)), pl.BlockSpec((1, HEAD_DIM), lambda i, j: (0, 0))],
        out_specs=pl.BlockSpec((tr, wc), lambda i, j: (i, j)),
        out_shape=jax.ShapeDtypeStruct((t, inner), BF16),
        compiler_params=_cparams("parallel", "parallel"),
    )(proj, w)


def _headnorm_bwd(dy, proj, w, col0, inner, name, tr=ROW_TILE, hb=ROW_HEADS, dest=None):
    t = proj.shape[0]
    tr = min(tr, t)
    hb = min(hb, inner // HEAD_DIM)
    wc = hb * HEAD_DIM
    c0 = col0 // wc
    width, out0, more, more_specs, aliases = _window(dest, inner, 3, 0)

    def body(dy_ref, x_ref, w_ref, dx_ref, dw_ref):
        outs = []
        part = jnp.zeros((1, HEAD_DIM), F32)
        for dyh, xh in zip(_heads_of(dy_ref[...], hb), _heads_of(x_ref[...], hb)):
            r = lax.rsqrt(jnp.mean(xh * xh, axis=-1, keepdims=True) + EPS)
            gy = dyh * w_ref[...]
            pr = jnp.sum(gy * xh, axis=-1, keepdims=True) * (1.0 / HEAD_DIM)
            outs.append((r * gy - xh * (r * r * r) * pr).astype(BF16))
            part = part + jnp.sum(dyh * xh * r, axis=0, keepdims=True)
        dx_ref[...] = jnp.concatenate(outs, axis=1)
        first = (pl.program_id(0) == 0) & (pl.program_id(1) == 0)

        @pl.when(first)
        def _():
            dw_ref[...] = part

        @pl.when(jnp.logical_not(first))
        def _():
            dw_ref[...] += part

    blk = pl.BlockSpec((tr, wc), lambda i, j: (i, j))
    return pl.pallas_call(
        _skip_ref(body, 3, len(more)), name=name, grid=(t // tr, inner // wc),
        in_specs=[blk, pl.BlockSpec((tr, wc), lambda i, j: (i, j + c0)),
                  pl.BlockSpec((1, HEAD_DIM), lambda i, j: (0, 0))] + more_specs,
        out_specs=[pl.BlockSpec((tr, wc), lambda i, j: (i, j + out0 // wc)),
                   pl.BlockSpec((1, HEAD_DIM), lambda i, j: (0, 0))],
        out_shape=[jax.ShapeDtypeStruct((t, width), BF16), jax.ShapeDtypeStruct((1, HEAD_DIM), F32)],
        input_output_aliases=aliases, compiler_params=_cparams("arbitrary", "arbitrary"),
    )(dy, proj, w, *more)


def _gate_fwd(o, proj, zcol0, inner, name, norm_w=None, tr=ROW_TILE, hb=ROW_HEADS):
    t = o.shape[0]
    tr = min(tr, t)
    hb = min(hb, inner // HEAD_DIM)
    wc = hb * HEAD_DIM
    c0 = zcol0 // wc
    has_w = norm_w is not None

    def body(*refs):
        o_ref, z_ref = refs[0], refs[1]
        out_ref = refs[2 + has_w]
        outs = []
        for oh, zh in zip(_heads_of(o_ref[...], hb), _heads_of(z_ref[...], hb)):
            if has_w:
                r = lax.rsqrt(jnp.mean(oh * oh, axis=-1, keepdims=True) + EPS)
                oh = oh * r * refs[2][...]
            outs.append((oh * _silu(zh)).astype(BF16))
        out_ref[...] = jnp.concatenate(outs, axis=1)

    blk = pl.BlockSpec((tr, wc), lambda i, j: (i, j))
    vec = pl.BlockSpec((1, HEAD_DIM), lambda i, j: (0, 0))
    return pl.pallas_call(
        body, name=name, grid=(t // tr, inner // wc),
        in_specs=[blk, pl.BlockSpec((tr, wc), lambda i, j: (i, j + c0))] + ([vec] if has_w else []),
        out_specs=blk, out_shape=jax.ShapeDtypeStruct((t, inner), BF16),
        compiler_params=_cparams("parallel", "parallel"),
    )(*([o, proj] + ([norm_w] if has_w else [])))


def _gate_bwd(dg, o, proj, zcol0, inner, name, do_dtype, norm_w=None, tr=ROW_TILE, hb=ROW_HEADS, dest=None):
    t = o.shape[0]
    tr = min(tr, t)
    hb = min(hb, inner // HEAD_DIM)
    wc = hb * HEAD_DIM
    c0 = zcol0 // wc
    has_w = norm_w is not None
    width, out0, more, more_specs, aliases = _window(dest, inner, 3 + has_w, 1)

    def body(*refs):
        dg_ref, o_ref, z_ref = refs[0], refs[1], refs[2]
        do_ref, dz_ref = refs[3 + has_w], refs[4 + has_w]
        dos, dzs = [], []
        part = jnp.zeros((1, HEAD_DIM), F32)
        for dgh, oh, zh in zip(_heads_of(dg_ref[...], hb), _heads_of(o_ref[...], hb), _heads_of(z_ref[...], hb)):
            dy = dgh * _silu(zh)
            if has_w:
                w = refs[3][...]
                r = lax.rsqrt(jnp.mean(oh * oh, axis=-1, keepdims=True) + EPS)
                on = oh * r
                dzs.append((dgh * on * w * _dsilu(zh)).astype(BF16))
                gy = dy * w
                pr = jnp.sum(gy * oh, axis=-1, keepdims=True) * (1.0 / HEAD_DIM)
                dos.append((r * gy - oh * (r * r * r) * pr).astype(do_dtype))
                part = part + jnp.sum(dy * on, axis=0, keepdims=True)
            else:
                dzs.append((dgh * oh * _dsilu(zh)).astype(BF16))
                dos.append(dy.astype(do_dtype))
        do_ref[...] = jnp.concatenate(dos, axis=1)
        dz_ref[...] = jnp.concatenate(dzs, axis=1)
        if has_w:
            dw_ref = refs[6]
            first = (pl.program_id(0) == 0) & (pl.program_id(1) == 0)

            @pl.when(first)
            def _():
                dw_ref[...] = part

            @pl.when(jnp.logical_not(first))
            def _():
                dw_ref[...] += part

    blk = pl.BlockSpec((tr, wc), lambda i, j: (i, j))
    vec = pl.BlockSpec((1, HEAD_DIM), lambda i, j: (0, 0))
    return pl.pallas_call(
        _skip_ref(body, 3 + has_w, len(more)), name=name, grid=(t // tr, inner // wc),
        in_specs=[blk, blk, pl.BlockSpec((tr, wc), lambda i, j: (i, j + c0))] + ([vec] if has_w else []) + more_specs,
        out_specs=[blk, pl.BlockSpec((tr, wc), lambda i, j: (i, j + out0 // wc))] + ([vec] if has_w else []),
        out_shape=[jax.ShapeDtypeStruct((t, inner), do_dtype), jax.ShapeDtypeStruct((t, width), BF16)]
        + ([jax.ShapeDtypeStruct((1, HEAD_DIM), F32)] if has_w else []),
        input_output_aliases=aliases, compiler_params=_cparams("arbitrary", "arbitrary"),
    )(*([dg, o, proj] + ([norm_w] if has_w else []) + more))


N_REL = 2 * REL_CLIP + 1
REL_PAD = 640
WIN = 2 * Q_TILE


def _diag_onehot():
    i = lax.broadcasted_iota(jnp.int32, (REL_PAD, WIN), 0)
    j = lax.broadcasted_iota(jnp.int32, (REL_PAD, WIN), 1)
    rel = jnp.where(j < Q_TILE + CHUNK, Q_TILE - j, Q_TILE + WIN - j)
    used = (j < Q_TILE + CHUNK) | (j > WIN - CHUNK)
    idx = jnp.clip(rel, -REL_CLIP, REL_CLIP) + REL_CLIP
    return jnp.where(used & (i == idx), 1.0, 0.0).astype(F32)


def _band_mask():
    r = lax.broadcasted_iota(jnp.int32, (Q_TILE, WIN), 0) // CHUNK
    kc = lax.broadcasted_iota(jnp.int32, (Q_TILE, WIN), 1) // CHUNK - LEFT_CHUNKS
    return (kc <= r) & (kc >= r - LEFT_CHUNKS)


def _bias_tiles(rel_bias_pad, name):
    nh = rel_bias_pad.shape[0]

    def body(rb_ref, o_ref):
        dvec = _nn(rb_ref[...], _diag_onehot(), HIGHEST)[0:1, :]
        tile = pltpu.roll(jnp.broadcast_to(dvec, (Q_TILE, WIN)), 0, 1, stride=1, stride_axis=0)
        o_ref[...] = jnp.where(_band_mask(), tile, NEG_BIG)

    return pl.pallas_call(
        body, name=name, grid=(nh,),
        in_specs=[pl.BlockSpec((None, 8, REL_PAD), lambda h: (h, 0, 0))],
        out_specs=pl.BlockSpec((None, Q_TILE, WIN), lambda h: (h, 0, 0)),
        out_shape=jax.ShapeDtypeStruct((nh, Q_TILE, WIN), F32),
        compiler_params=_cparams("parallel"),
    )(rel_bias_pad)


def _bias_grad(dtile, name):
    nh = dtile.shape[0]

    def body(d_ref, o_ref):
        ri = lax.broadcasted_iota(jnp.int32, (Q_TILE, Q_TILE), 0)
        ci = lax.broadcasted_iota(jnp.int32, (Q_TILE, Q_TILE), 1)
        flip = jnp.where(ri + ci == Q_TILE - 1, 1.0, 0.0).astype(F32)
        rev = _dot_exact(flip, d_ref[...], NN, True)
        rolled = pltpu.roll(rev, WIN - (Q_TILE - 1), 1, stride=1, stride_axis=0)
        diag = jnp.broadcast_to(jnp.sum(rolled, axis=0, keepdims=True), (8, WIN))
        o_ref[...] = _nt(diag, _diag_onehot(), HIGHEST)

    return pl.pallas_call(
        body, name=name, grid=(nh,),
        in_specs=[pl.BlockSpec((None, Q_TILE, WIN), lambda h: (h, 0, 0))],
        out_specs=pl.BlockSpec((None, 8, REL_PAD), lambda h: (h, 0, 0)),
        out_shape=jax.ShapeDtypeStruct((nh, 8, REL_PAD), F32),
        compiler_params=_cparams("parallel"),
    )(dtile)


GROUP = 2 * CHUNK
BAND = Q_TILE + GROUP


N_GROUPS = Q_TILE // GROUP
ATTN_HB = 2
ATTN_HB_FWD = 4


def _head_cols(j):
    return slice(j * HEAD_DIM, (j + 1) * HEAD_DIM)


def _groups(ref, units):
    return jnp.stack([ref[GROUP * g:GROUP * (g + 1), _head_cols(j)] for j, g in units])


def _bands(r0_ref, r1_ref, units):
    return jnp.stack([jnp.concatenate([r0_ref[GROUP * g:, _head_cols(j)], r1_ref[:GROUP * (g + 1), _head_cols(j)]],
                                      axis=0) for j, g in units])


def _group_probs(q, kw, b_ref, units, first_tile):
    bias = jnp.stack([b_ref[j, GROUP * g:GROUP * (g + 1), GROUP * g:GROUP * g + BAND] for j, g in units])
    s = _dot(q, kw, BNT) * (HEAD_DIM ** -0.5) + bias
    col = jnp.stack([lax.broadcasted_iota(jnp.int32, (GROUP, BAND), 1) + GROUP * g for _, g in units])
    s = jnp.where(first_tile & (col < Q_TILE), NEG_BIG, s)
    p = jnp.exp(s - jnp.max(s, axis=-1, keepdims=True))
    return p * (1.0 / jnp.sum(p, axis=-1, keepdims=True))


def _attn_fwd(q, k, v, v_col0, bias, name):
    t, inner = q.shape
    nh, nt = inner // HEAD_DIM, t // Q_TILE
    hb = min(ATTN_HB_FWD, nh)
    wc = hb * HEAD_DIM
    vh = v_col0 // wc
    units = [(j, g) for j in range(hb) for g in range(N_GROUPS)]

    def body(q_ref, k0_ref, k1_ref, v0_ref, v1_ref, b_ref, o_ref):
        p = _group_probs(_groups(q_ref, units), _bands(k0_ref, k1_ref, units), b_ref, units, pl.program_id(1) == 0)
        o = _dot(_bf(p), _bf(_bands(v0_ref, v1_ref, units)), BNN)
        for n, (j, g) in enumerate(units):
            o_ref[GROUP * g:GROUP * (g + 1), _head_cols(j)] = o[n]

    cur = pl.BlockSpec((Q_TILE, wc), lambda h, i: (i, h))
    prev = pl.BlockSpec((Q_TILE, wc), lambda h, i: (jnp.maximum(i - 1, 0), h))
    v_cur = pl.BlockSpec((Q_TILE, wc), lambda h, i: (i, h + vh))
    v_prev = pl.BlockSpec((Q_TILE, wc), lambda h, i: (jnp.maximum(i - 1, 0), h + vh))
    return pl.pallas_call(
        body, name=name, grid=(nh // hb, nt),
        in_specs=[cur, prev, cur, v_prev, v_cur, pl.BlockSpec((hb, Q_TILE, WIN), lambda h, i: (h, 0, 0))],
        out_specs=cur, out_shape=jax.ShapeDtypeStruct((t, inner), F32),
        compiler_params=_cparams("parallel", "parallel"),
    )(q, k, k, v, v, bias)


def _attn_bwd(q, k, v, v_col0, do, bias, name, dest=None):
    t, inner = q.shape
    nh, nt = inner // HEAD_DIM, t // Q_TILE
    scale = HEAD_DIM ** -0.5
    hb = min(ATTN_HB, nh)
    wc = hb * HEAD_DIM
    units = [(j, g) for j in range(hb) for g in range(N_GROUPS)]

    def body(q_ref, k0_ref, k1_ref, v0_ref, v1_ref, do_ref, b_ref, dq_ref, dk_ref, dv_ref, db_ref,
             ck_ref, cv_ref, wk_ref, wv_ref):
        i = pl.program_id(1)

        @pl.when(i == 0)
        def _():
            ck_ref[...] = jnp.zeros(blk, F32)
            cv_ref[...] = jnp.zeros(blk, F32)
            db_ref[...] = jnp.zeros((hb, Q_TILE, WIN), F32)

        @pl.when(i < nt)
        def _():
            wk_ref[...] = jnp.zeros((WIN, wc), F32)
            wv_ref[...] = jnp.zeros((WIN, wc), F32)
            qv, dov = _groups(q_ref, units), _groups(do_ref, units)
            kw, vw = _bands(k0_ref, k1_ref, units), _bf(_bands(v0_ref, v1_ref, units))
            p = _group_probs(qv, kw, b_ref, units, i == 0)
            dp = _dot(dov, vw, BNT)
            ds = p * (dp - jnp.sum(p * dp, axis=-1, keepdims=True))
            pb, dsb = _bf(p), _bf(ds)
            dq = _dot(dsb, kw, BNN) * scale
            dkw = _dot(dsb, qv, BTN) * scale
            dvw = _dot(pb, dov, BTN)
            for n, (j, g) in enumerate(units):
                rows, cols = slice(GROUP * g, GROUP * (g + 1)), slice(GROUP * g, GROUP * g + BAND)
                db_ref[j, rows, cols] += ds[n]
                dq_ref[rows, _head_cols(j)] = dq[n]
                wk_ref[cols, _head_cols(j)] += dkw[n]
                wv_ref[cols, _head_cols(j)] += dvw[n]
            dk_ref[...] = ck_ref[...] + wk_ref[:Q_TILE, :]
            dv_ref[...] = (cv_ref[...] + wv_ref[:Q_TILE, :]).astype(BF16)
            ck_ref[...] = wk_ref[Q_TILE:, :]
            cv_ref[...] = wv_ref[Q_TILE:, :]

        @pl.when(i == nt)
        def _():
            dk_ref[...] = ck_ref[...]
            dv_ref[...] = cv_ref[...].astype(BF16)

    blk = (Q_TILE, wc)
    cur = pl.BlockSpec(blk, lambda h, i: (jnp.minimum(i, nt - 1), h))
    prev = pl.BlockSpec(blk, lambda h, i: (jnp.clip(i - 1, 0, nt - 1), h))
    lag = pl.BlockSpec(blk, lambda h, i: (jnp.maximum(i - 1, 0), h))
    vh = v_col0 // wc
    v_cur = pl.BlockSpec(blk, lambda h, i: (jnp.minimum(i, nt - 1), h + vh))
    v_prev = pl.BlockSpec(blk, lambda h, i: (jnp.clip(i - 1, 0, nt - 1), h + vh))
    tile = pl.BlockSpec((hb, Q_TILE, WIN), lambda h, i: (h, 0, 0))
    width, out0, more, more_specs, aliases = _window(dest, inner, 7, 2)
    return pl.pallas_call(
        _skip_ref(body, 7, len(more)), name=name, grid=(nh // hb, nt + 1),
        in_specs=[cur, prev, cur, v_prev, v_cur, cur, tile] + more_specs,
        out_specs=[cur, lag, pl.BlockSpec(blk, lambda h, i: (jnp.maximum(i - 1, 0), h + out0 // wc)), tile],
        out_shape=[jax.ShapeDtypeStruct((t, inner), F32)] * 2 + [jax.ShapeDtypeStruct((t, width), BF16),
                                                                 jax.ShapeDtypeStruct((nh, Q_TILE, WIN), F32)],
        scratch_shapes=[pltpu.VMEM(blk, F32), pltpu.VMEM(blk, F32),
                        pltpu.VMEM((WIN, wc), F32), pltpu.VMEM((WIN, wc), F32)],
        input_output_aliases=aliases, compiler_params=_cparams("arbitrary", "arbitrary"),
    )(q, k, k, v, v, do, bias, *more)


def _pad_rel_bias(rel_bias):
    nh = rel_bias.shape[0]
    return jnp.broadcast_to(jnp.pad(rel_bias, ((0, 0), (0, REL_PAD - N_REL)))[:, None, :], (nh, 8, REL_PAD))


def _layer_b_fwd(h1, nw, w_in_t, qw, kw, bias, w_out, target):
    t, d = h1.shape
    inner = w_out.shape[0]
    hn = _rms_fwd(h1, nw, "b_rms")
    proj = _mm(hn, w_in_t, "nt", t, 4 * inner, d, out_dtype=F32, name="b_proj")
    qn = _headnorm_fwd(proj, qw, 0, inner, "b_qnorm")
    kn = _headnorm_fwd(proj, kw, inner, inner, "b_knorm")
    o = _attn_fwd(qn, kn, proj, 2 * inner, bias, "b_attn")
    g = _gate_fwd(o, proj, 3 * inner, inner, "b_gate")
    loss_parts = _mm(g, w_out, "nn", t, d, inner, out_dtype=F32, name="b_out", res=h1, loss_target=target)
    return loss_parts, (hn, proj, qn, kn, o, g)


def _layer_b_bwd(dh2, dh2b, h1, nw, w_in_t, qw, kw, bias, w_out, saved):
    hn, proj, qn, kn, o, g = saved
    t, d = h1.shape
    inner = w_out.shape[0]
    dg = _mm(dh2b, w_out, "nt", t, inner, d, out_dtype=F32, name="b_dgate")
    dw_out = _mm(g, dh2b, "tn", inner, d, t, out_dtype=BF16, name="b_dwout")
    do, dproj = _gate_bwd(dg, o, proj, 3 * inner, inner, "b_gate_bwd", BF16, dest=(None, 3 * inner, 4 * inner))
    dq, dk, dproj, dtile = _attn_bwd(qn, kn, proj, 2 * inner, do, bias, "b_attn_bwd",
                                     dest=(dproj, 2 * inner, 4 * inner))
    dproj, dqw = _headnorm_bwd(dq, proj, qw, 0, inner, "b_qnorm_bwd", dest=(dproj, 0, 4 * inner))
    dproj, dkw = _headnorm_bwd(dk, proj, kw, inner, inner, "b_knorm_bwd", dest=(dproj, inner, 4 * inner))
    dhn = _mm(dproj, w_in_t, "nn", t, d, 4 * inner, out_dtype=F32, name="b_dhn", **LONG_K_TILES)
    dw_in_t = _mm(dproj, hn, "tn", 4 * inner, d, t, out_dtype=BF16, name="b_dwin", **LONG_K_TILES)
    dh1, dh1b, dnw = _rms_bwd(h1, nw, dhn, dh2, "b_rms_bwd")
    drb = _bias_grad(dtile, "b_bias_grad")[:, 0, :N_REL]
    return dh1, dh1b, dnw, dw_in_t, dqw, dkw, drb, dw_out


LANES = 128


def _softplus(x):
    return jnp.maximum(x, 0.0) + jnp.log1p(jnp.exp(-jnp.abs(x)))


def _gates_fwd(ab, alog_row, dt_row, nh, name, tr=1024):
    t = ab.shape[0]
    tr = min(tr, t)

    def body(x_ref, al_ref, dt_ref, o_ref):
        x = x_ref[...]
        lane = lax.broadcasted_iota(jnp.int32, x.shape, 1)
        g = -jnp.exp(al_ref[...]) * _softplus(x + dt_ref[...])
        o_ref[...] = jnp.where(lane < nh, g, jnp.where(lane < 2 * nh, _sigmoid(x), 0.0))

    row = pl.BlockSpec((tr, LANES), lambda i: (i, 0))
    vec = pl.BlockSpec((1, LANES), lambda i: (0, 0))
    return pl.pallas_call(
        body, name=name, grid=(t // tr,), in_specs=[row, vec, vec], out_specs=row,
        out_shape=jax.ShapeDtypeStruct((t, LANES), F32), compiler_params=_cparams("parallel"),
    )(ab, alog_row, dt_row)


def _gates_bwd(ab, alog_row, dt_row, dgates, nh, name, tr=1024):
    t = ab.shape[0]
    tr = min(tr, t)
    npart = dgates.shape[0]

    def body(x_ref, al_ref, dt_ref, dg_ref, dx_ref, s_ref):
        x = x_ref[...]
        lane = lax.broadcasted_iota(jnp.int32, x.shape, 1)
        dgt = dg_ref[0]
        for p in range(1, npart):
            dgt = dgt + dg_ref[p]
        ea = jnp.exp(al_ref[...])
        xa = x + dt_ref[...]
        da = jnp.where(lane < nh, dgt * (-ea) * _sigmoid(xa), 0.0)
        beta = _sigmoid(x)
        db = jnp.where((lane >= nh) & (lane < 2 * nh), dgt * beta * (1.0 - beta), 0.0)
        dx_ref[...] = (da + db).astype(BF16)
        dal = jnp.sum(jnp.where(lane < nh, dgt * (-ea) * _softplus(xa), 0.0), axis=0, keepdims=True)
        ddt = jnp.sum(da, axis=0, keepdims=True)
        r8 = lax.broadcasted_iota(jnp.int32, (8, LANES), 0)
        part = jnp.where(r8 == 0, dal, jnp.where(r8 == 1, ddt, 0.0))

        @pl.when(pl.program_id(0) == 0)
        def _():
            s_ref[...] = part

        @pl.when(pl.program_id(0) > 0)
        def _():
            s_ref[...] += part

    row = pl.BlockSpec((tr, LANES), lambda i: (i, 0))
    vec = pl.BlockSpec((1, LANES), lambda i: (0, 0))
    return pl.pallas_call(
        body, name=name, grid=(t // tr,),
        in_specs=[row, vec, vec, pl.BlockSpec((npart, tr, LANES), lambda i: (0, i, 0))],
        out_specs=[row, pl.BlockSpec((8, LANES), lambda i: (0, 0))],
        out_shape=[jax.ShapeDtypeStruct((t, LANES), BF16), jax.ShapeDtypeStruct((8, LANES), F32)],
        compiler_params=_cparams("arbitrary"),
    )(ab, alog_row, dt_row, dgates)


HALO = 8


def _delayed(ext, rows):
    return [ext[HALO:HALO + rows]] + [pltpu.roll(ext, s, 0)[HALO:HALO + rows] for s in range(1, CONV_K)]


def _conv_taps(delayed, w):
    acc = delayed[0] * w[CONV_K - 1:CONV_K]
    for s in range(1, CONV_K):
        acc = acc + delayed[s] * w[CONV_K - 1 - s:CONV_K - s]
    return acc


def _conv_fwd(proj, conv_w, col0, inner, mode, name, tt=CONV_ROWS, hb=CONV_HEADS):
    t = proj.shape[0]
    tt = min(tt, t)
    hb = min(hb, inner // HEAD_DIM)
    wc = hb * HEAD_DIM
    c0 = col0 // wc
    hpb = tt // HALO

    def body(x_ref, halo_ref, w_ref, o_ref):
        halo = jnp.where(pl.program_id(1) == 0, 0.0, halo_ref[...])
        s = _silu(_conv_taps(_delayed(jnp.concatenate([halo, x_ref[...]], axis=0), tt), w_ref[...]))
        if mode == "v":
            o_ref[...] = s
        else:
            mul = HEAD_DIM ** -0.5 if mode == "q" else 1.0
            o_ref[...] = jnp.concatenate(
                [sh * (lax.rsqrt(jnp.sum(sh * sh, axis=-1, keepdims=True) + EPS) * mul) for sh in _heads_of(s, hb)], axis=1)

    return pl.pallas_call(
        body, name=name, grid=(inner // wc, t // tt),
        in_specs=[pl.BlockSpec((tt, wc), lambda j, i: (i, j + c0)),
                  pl.BlockSpec((HALO, wc), lambda j, i: (jnp.maximum(i * hpb - 1, 0), j + c0)),
                  pl.BlockSpec((CONV_K, wc), lambda j, i: (0, j + c0))],
        out_specs=pl.BlockSpec((tt, wc), lambda j, i: (i, j)),
        out_shape=jax.ShapeDtypeStruct((t, inner), F32),
        compiler_params=_cparams("parallel", "parallel"),
    )(proj, proj, conv_w)


def _conv_bwd(dy, proj, conv_w, col0, inner, mode, name, tt=CONV_ROWS, hb=CONV_HEADS, dest=None):
    t = proj.shape[0]
    tt = min(tt, t)
    nt = t // tt
    hb = min(hb, inner // HEAD_DIM)
    wc = hb * HEAD_DIM
    c0 = col0 // wc
    hpb = tt // HALO
    rows = tt + HALO

    def body(dy_ref, dyn_ref, x_ref, xp_ref, xn_ref, w_ref, dx_ref, dw_ref):
        i = pl.program_id(1)
        w = w_ref[...]
        xprev = jnp.where(i == 0, 0.0, xp_ref[...])
        delayed = _delayed(jnp.concatenate([xprev, x_ref[...], xn_ref[...]], axis=0), rows)
        c = _conv_taps(delayed, w)
        dyv = jnp.concatenate([dy_ref[...], jnp.where(i == nt - 1, 0.0, dyn_ref[...])], axis=0)
        sg = _sigmoid(c)
        s = c * sg
        if mode == "v":
            ds = dyv
        else:
            mul = HEAD_DIM ** -0.5 if mode == "q" else 1.0
            parts = []
            for dyh, sh in zip(_heads_of(dyv, hb), _heads_of(s, hb)):
                r = lax.rsqrt(jnp.sum(sh * sh, axis=-1, keepdims=True) + EPS)
                parts.append(mul * (r * dyh - sh * (r * r * r) * jnp.sum(dyh * sh, axis=-1, keepdims=True)))
            ds = jnp.concatenate(parts, axis=1)
        dc = ds * (sg * (1.0 + c * (1.0 - sg)))
        dx = dc[:tt] * w[CONV_K - 1:CONV_K]
        for sft in range(1, CONV_K):
            dx = dx + pltpu.roll(dc, rows - sft, 0)[:tt] * w[CONV_K - 1 - sft:CONV_K - sft]
        dx_ref[...] = dx.astype(BF16)
        r8 = lax.broadcasted_iota(jnp.int32, (8, wc), 0)
        part = jnp.zeros((8, wc), F32)
        for sft in range(CONV_K):
            part = part + jnp.where(r8 == CONV_K - 1 - sft,
                                    jnp.sum(dc[:tt] * delayed[sft][:tt], axis=0, keepdims=True), 0.0)

        @pl.when(i == 0)
        def _():
            dw_ref[...] = part

        @pl.when(i > 0)
        def _():
            dw_ref[...] += part

    cur = lambda off: pl.BlockSpec((tt, wc), lambda j, i: (i, j + off))
    nxt = lambda off: pl.BlockSpec((HALO, wc), lambda j, i: (jnp.minimum((i + 1) * hpb, t // HALO - 1), j + off))
    width, out0, more, more_specs, aliases = _window(dest, inner, 6, 0)
    return pl.pallas_call(
        _skip_ref(body, 6, len(more)), name=name, grid=(inner // wc, nt),
        in_specs=[cur(0), nxt(0), cur(c0),
                  pl.BlockSpec((HALO, wc), lambda j, i: (jnp.maximum(i * hpb - 1, 0), j + c0)), nxt(c0),
                  pl.BlockSpec((CONV_K, wc), lambda j, i: (0, j + c0))] + more_specs,
        out_specs=[pl.BlockSpec((tt, wc), lambda j, i: (i, j + out0 // wc)),
                   pl.BlockSpec((8, wc), lambda j, i: (0, j))],
        out_shape=[jax.ShapeDtypeStruct((t, width), BF16), jax.ShapeDtypeStruct((8, inner), F32)],
        input_output_aliases=aliases, compiler_params=_cparams("parallel", "arbitrary"),
    )(dy, dy, proj, proj, proj, conv_w, *more)


GDN_HB = 4
GDN_NB = 8
SCAN_HB = 16
SCAN_NB = 4


def _iota2(n, m):
    return lax.broadcasted_iota(jnp.int32, (n, m), 0), lax.broadcasted_iota(jnp.int32, (n, m), 1)


def _head_select(first_head, hb, lane0):
    r, lane = _iota2(8, LANES)
    return jnp.where((r < hb) & (lane == lane0 + first_head + r), 1.0, 0.0).astype(F32)


def _chunk_gates(gt, selg, selb):
    i, j = _iota2(CHUNK, CHUNK)
    gc_all = _dot_exact(jnp.where(j <= i, 1.0, 0.0), gt, NN, True)
    return (_dot_exact(gc_all, selg, NT, False), _dot_exact(selg, gc_all, NT, True),
            _dot_exact(gt, selb, NT, False))


def _decay_terms(gcol, grow):
    i, j = _iota2(CHUNK, CHUNK)
    glast = gcol[:, CHUNK - 1:CHUNK, :]
    decay = jnp.exp(jnp.where(j <= i, gcol - grow, NEG_BIG))
    return jnp.exp(gcol), jnp.exp(glast - gcol), jnp.exp(glast), decay


def _unit_lower_inverse(a):
    i, j = _iota2(CHUNK, CHUNK)
    same16 = (i // 16) == (j // 16)
    same32 = (i // 32) == (j // 32)
    m = jnp.where(same16, -a, 0.0)
    x = jnp.where(i == j, 1.0, 0.0) + m
    for _ in range(3):
        m = _dot3(m, m, BNN)
        x = x + _dot3(x, m, BNN)
    for off in (jnp.where(same32 & jnp.logical_not(same16), a, 0.0), jnp.where(same32, 0.0, a)):
        x = x - _dot3(_dot3(x, off, BNN), x, BNN)
    return x


def _unit_inputs(refs, g_ref, selg, selb, hb, nb):
    units = [(c, h) for c in range(nb) for h in range(hb)]
    rs = lambda c: slice(c * CHUNK, (c + 1) * CHUNK)
    cs = lambda h: slice(h * HEAD_DIM, (h + 1) * HEAD_DIM)
    gates = [_chunk_gates(g_ref[rs(c), :], selg, selb) for c in range(nb)]
    stacked = [jnp.stack([r[rs(c), cs(h)] for c, h in units]) for r in refs]
    gcol = jnp.stack([gates[c][0][:, h:h + 1] for c, h in units])
    grow = jnp.stack([gates[c][1][h:h + 1, :] for c, h in units])
    bcol = jnp.stack([gates[c][2][:, h:h + 1] for c, h in units])
    return units, rs, cs, stacked, gcol, grow, bcol


def _gdn_specs(nh, inner, t, heads=GDN_HB, chunks=GDN_NB):
    hb, nb = min(heads, nh), chunks
    rows = nb * CHUNK
    wide = pl.BlockSpec((rows, hb * HEAD_DIM), lambda g, n: (n, g))
    sq = pl.BlockSpec((hb, rows, CHUNK), lambda g, n: (g, n, 0))
    gts = pl.BlockSpec((rows, LANES), lambda g, n: (n, 0))
    glb = pl.BlockSpec((nb * 8, hb * HEAD_DIM), lambda g, n: (n, g))
    return hb, nb, rows, wide, sq, gts, glb


def _gdn_intra_fwd(q, k, v, gates, nh, name, comm=None):
    t, inner = q.shape
    hb, nb, rows, wide, sq, gts, glb = _gdn_specs(nh, inner, t)

    def body(q_ref, k_ref, v_ref, g_ref, qe_ref, kel_ref, wb_ref, w_ref, u_ref, qk_ref, tm_ref, gl_ref):
        first = pl.program_id(0) * hb
        selg, selb = _head_select(first, hb, 0), _head_select(first, hb, nh)
        i, j = _iota2(CHUNK, CHUNK)
        units, rs, cs, (qv, kv, vv), gcol, grow, bcol = _unit_inputs(
            (q_ref, k_ref, v_ref), g_ref, selg, selb, hb, nb)
        e, el, gl, decay = _decay_terms(gcol, grow)
        kb = kv * bcol
        qbf, kbf = _bf(qv), _bf(kv)
        a = jnp.where(j < i, _dot(_bf(kb), kbf, BNT) * decay, 0.0)
        tm = _unit_lower_inverse(a)
        uw = _dot3(tm, jnp.concatenate([vv * bcol, kb * e], axis=2), BNN)
        qk = _bf(_dot(qbf, kbf, BNT) * decay)
        qe, kel = _bf(qv * e), _bf(kv * el)
        for n, (c, h) in enumerate(units):
            w = uw[n, :, HEAD_DIM:]
            qe_ref[rs(c), cs(h)] = qe[n]
            kel_ref[rs(c), cs(h)] = kel[n]
            wb_ref[rs(c), cs(h)] = _bf(w)
            w_ref[rs(c), cs(h)] = w
            u_ref[rs(c), cs(h)] = uw[n, :, :HEAD_DIM]
            qk_ref[h, rs(c), :] = qk[n]
            tm_ref[h, rs(c), :] = tm[n]
            gl_ref[c * 8:(c + 1) * 8, cs(h)] = jnp.broadcast_to(gl[n], (8, HEAD_DIM))

    big = lambda dt: jax.ShapeDtypeStruct((t, inner), dt)
    return _grid_call(
        body, name=name, grid=(nh // hb, t // rows),
        in_specs=[wide, wide, wide, gts],
        out_specs=[wide] * 5 + [sq, sq, glb],
        out_shape=[big(BF16), big(BF16), big(BF16), big(F32), big(F32),
                   jax.ShapeDtypeStruct((nh, t, CHUNK), BF16), jax.ShapeDtypeStruct((nh, t, CHUNK), F32),
                   jax.ShapeDtypeStruct((t // CHUNK * 8, inner), F32)],
        args=(q, k, v, gates), semantics=("parallel", "parallel"), comm=comm)


def _gdn_scan_fwd(qe, kel, wb, u, qk, glb, nh, name):
    t, inner = u.shape
    hb, nb, rows, wide, sq, _, glb_spec = _gdn_specs(nh, inner, t, SCAN_HB, SCAN_NB)

    def body(qe_ref, kel_ref, wb_ref, u_ref, qk_ref, gl_ref, o_ref, vn_ref, sall_ref, s_ref):
        @pl.when(pl.program_id(1) == 0)
        def _():
            s_ref[...] = jnp.zeros(s_ref.shape, F32)

        cs = lambda h: slice(h * HEAD_DIM, (h + 1) * HEAD_DIM)
        for c in range(nb):
            rs = slice(c * CHUNK, (c + 1) * CHUNK)
            heads = lambda ref: jnp.stack([ref[rs, cs(h)] for h in range(hb)])
            s = s_ref[...]
            sall_ref[c] = s
            sb = _bf(s)
            vn = heads(u_ref) - _dot(heads(wb_ref), sb, BNN)
            vnb = _bf(vn)
            o = _dot(heads(qe_ref), sb, BNN) + _dot(qk_ref[:, rs, :], vnb, BNN)
            gl = jnp.stack([gl_ref[c * 8:c * 8 + 1, cs(h)] for h in range(hb)])
            s_ref[...] = s * gl + _dot(heads(kel_ref), vnb, BTN)
            for h in range(hb):
                vn_ref[rs, cs(h)] = vnb[h]
                o_ref[rs, cs(h)] = o[h]

    return pl.pallas_call(
        body, name=name, grid=(nh // hb, t // rows),
        in_specs=[wide, wide, wide, wide, sq, glb_spec],
        out_specs=[wide, wide, pl.BlockSpec((nb, hb, HEAD_DIM, HEAD_DIM), lambda g, n: (n, g, 0, 0))],
        out_shape=[jax.ShapeDtypeStruct((t, inner), F32), jax.ShapeDtypeStruct((t, inner), BF16),
                   jax.ShapeDtypeStruct((t // CHUNK, nh, HEAD_DIM, HEAD_DIM), F32)],
        scratch_shapes=[pltpu.VMEM((hb, HEAD_DIM, HEAD_DIM), F32)],
        compiler_params=_cparams("parallel", "arbitrary"),
    )(qe, kel, wb, u, qk, glb)


def _gdn_scan_bwd(do, qe, kel, wb, vn, qk, glb, sall, nh, name, comm=None):
    t, inner = do.shape
    hb, nb, rows, _, _, _, _ = _gdn_specs(nh, inner, t, SCAN_HB, SCAN_NB)
    last = t // rows - 1
    wide = pl.BlockSpec((rows, hb * HEAD_DIM), lambda g, n: (last - n, g))
    sq = pl.BlockSpec((hb, rows, CHUNK), lambda g, n: (g, last - n, 0))
    glb_spec = pl.BlockSpec((nb * 8, hb * HEAD_DIM), lambda g, n: (last - n, g))

    def body(do_ref, qe_ref, kel_ref, wb_ref, vn_ref, qk_ref, gl_ref, sall_ref,
             dvn_ref, dw_ref, dqe_ref, dkel_ref, dqk_ref, dgl_ref, ds_ref):
        @pl.when(pl.program_id(1) == 0)
        def _():
            ds_ref[...] = jnp.zeros(ds_ref.shape, F32)

        cs = lambda h: slice(h * HEAD_DIM, (h + 1) * HEAD_DIM)
        for c in reversed(range(nb)):
            rs = slice(c * CHUNK, (c + 1) * CHUNK)
            heads = lambda ref: jnp.stack([ref[rs, cs(h)] for h in range(hb)])
            ds, s = ds_ref[...], sall_ref[c]
            dsb, sb = _bf(ds), _bf(s)
            dob, vnb = _bf(heads(do_ref)), _bf(heads(vn_ref))
            dvn = _dot(qk_ref[:, rs, :], dob, BTN) + _dot(heads(kel_ref), dsb, BNN)
            dvnb = _bf(dvn)
            dw = -_dot(dvnb, sb, BNT)
            dqe = _dot(dob, sb, BNT)
            dkel = _dot(vnb, dsb, BNT)
            dqk_ref[:, rs, :] = _dot(dob, vnb, BNT)
            dgl = jnp.sum(jnp.sum(ds * s, axis=2, keepdims=True), axis=1, keepdims=True)
            gl = jnp.stack([gl_ref[c * 8:c * 8 + 1, cs(h)] for h in range(hb)])
            ds_ref[...] = ds * gl + _dot(heads(qe_ref), dob, BTN) - _dot(heads(wb_ref), dvnb, BTN)
            for h in range(hb):
                dvn_ref[rs, cs(h)] = dvn[h]
                dw_ref[rs, cs(h)] = dw[h]
                dqe_ref[rs, cs(h)] = dqe[h]
                dkel_ref[rs, cs(h)] = dkel[h]
                dgl_ref[c * 8:(c + 1) * 8, cs(h)] = jnp.broadcast_to(dgl[h], (8, HEAD_DIM))

    big = jax.ShapeDtypeStruct((t, inner), F32)
    return _grid_call(
        body, name=name, grid=(nh // hb, t // rows),
        in_specs=[wide, wide, wide, wide, wide, sq, glb_spec,
                  pl.BlockSpec((nb, hb, HEAD_DIM, HEAD_DIM), lambda g, n: (last - n, g, 0, 0))],
        out_specs=[wide] * 4 + [sq, glb_spec],
        out_shape=[big] * 4 + [jax.ShapeDtypeStruct((nh, t, CHUNK), F32),
                               jax.ShapeDtypeStruct((t // CHUNK * 8, inner), F32)],
        scratch_shapes=[pltpu.VMEM((hb, HEAD_DIM, HEAD_DIM), F32)],
        args=(do, qe, kel, wb, vn, qk, glb, sall), semantics=("parallel", "arbitrary"), comm=comm)


def _gdn_intra_bwd(q, k, v, gates, tm, w, u, dvn, dw, dqe, dkel, dqk, dglb, nh, name, comm=None):
    t, inner = q.shape
    hb, nb, rows, wide, sq, gts, glb = _gdn_specs(nh, inner, t)

    def body(q_ref, k_ref, v_ref, g_ref, tm_ref, w_ref, u_ref, dvn_ref, dw_ref, dqe_ref, dkel_ref, dqk_ref,
             dgl_ref, dq_ref, dk_ref, dv_ref, dg_ref):
        first = pl.program_id(0) * hb
        selg, selb = _head_select(first, hb, 0), _head_select(first, hb, nh)
        i, j = _iota2(CHUNK, CHUNK)
        lane8 = lax.broadcasted_iota(jnp.int32, (CHUNK, 8), 1)
        row = lax.broadcasted_iota(jnp.int32, (CHUNK, 1), 0)
        lower = jnp.where(j <= i, 1.0, 0.0).astype(F32)
        rsum = lambda x: jnp.sum(x, axis=-1, keepdims=True)
        units, rs, cs, (qv, kv, vv, wv, uv, dvn, dw, dqe, dkel), gcol, grow, bcol = _unit_inputs(
            (q_ref, k_ref, v_ref, w_ref, u_ref, dvn_ref, dw_ref, dqe_ref, dkel_ref), g_ref, selg, selb, hb, nb)
        nu = len(units)
        tmv = jnp.stack([tm_ref[h, rs(c), :] for c, h in units])
        dqk = jnp.where(j <= i, jnp.stack([dqk_ref[h, rs(c), :] for c, h in units]), 0.0)
        dgl = jnp.stack([dgl_ref[c * 8:c * 8 + 1, h * HEAD_DIM:h * HEAD_DIM + 1] for c, h in units])
        e, el, gl, decay = _decay_terms(gcol, grow)
        kb = kv * bcol
        qb, kbf, kbb = _bf(qv), _bf(kv), _bf(kb)
        dqkr = _bf(dqk * decay)
        dq = dqe * e + _dot(dqkr, kbf, BNN)
        dk = dkel * el + _dot(dqkr, qb, BTN)
        de = rsum(dqe * qv)
        del_ = rsum(dkel * kv)
        mq = dqk * _dot(qb, kbf, BNT) * decay
        dsol = _dot3(tmv, jnp.concatenate([dvn, dw], axis=2), BTN)
        dvb, dkbe = dsol[:, :, :HEAD_DIM], dsol[:, :, HEAD_DIM:]
        da = -jnp.where(j < i, _dot3(dsol, jnp.concatenate([uv, wv], axis=2), BNT), 0.0)
        dkk = _bf(da * decay)
        ma = da * _dot(kbb, kbf, BNT) * decay
        dkb = dkbe * e + _dot(dkk, kbf, BNN)
        de = de + rsum(dkbe * kb)
        dk = dk + _dot(dkk, kbb, BTN) + dkb * bcol
        dv = dvb * bcol
        dbeta = rsum(dkb * kv) + rsum(dvb * vv)
        m = mq + ma
        ones = jnp.ones((nu, CHUNK, LANES), F32)
        dgc = rsum(m) - _dot_exact(m, ones, BTN, False)[:, :, 0:1] + de * e - del_ * el
        tail = jnp.sum(del_ * el, axis=1, keepdims=True) + dgl * gl
        dgc = dgc + jnp.where(row == CHUNK - 1, tail, 0.0)
        for n, (c, h) in enumerate(units):
            dq_ref[rs(c), cs(h)] = dq[n]
            dk_ref[rs(c), cs(h)] = dk[n]
            dv_ref[rs(c), cs(h)] = dv[n]
        for c in range(nb):
            dgc_cols = jnp.zeros((CHUNK, 8), F32)
            dbeta_cols = jnp.zeros((CHUNK, 8), F32)
            for h in range(hb):
                dgc_cols = jnp.where(lane8 == h, dgc[c * hb + h], dgc_cols)
                dbeta_cols = jnp.where(lane8 == h, dbeta[c * hb + h], dbeta_cols)
            dg_cols = _dot_exact(lower, dgc_cols, TN, True)
            dg_ref[rs(c), :] = _dot_exact(dg_cols, selg, NN, False) + _dot_exact(dbeta_cols, selb, NN, False)

    big = jax.ShapeDtypeStruct((t, inner), F32)
    return _grid_call(
        body, name=name, grid=(nh // hb, t // rows),
        in_specs=[wide, wide, wide, gts, sq, wide, wide, wide, wide, wide, wide, sq, glb],
        out_specs=[wide, wide, wide, pl.BlockSpec((None, rows, LANES), lambda g, n: (g, n, 0))],
        out_shape=[big, big, big, jax.ShapeDtypeStruct((nh // hb, t, LANES), F32)],
        args=(q, k, v, gates, tm, w, u, dvn, dw, dqe, dkel, dqk, dglb), semantics=("parallel", "parallel"),
        comm=comm)


def _layer_a_fwd(x, hn, w_in_t, w_ab_t, conv_w, alog_row, dt_row, onw, nh, comm, w_out_of):
    t, d = x.shape
    inner = nh * HEAD_DIM
    proj = _mm(hn, w_in_t, "nt", t, 4 * inner, d, out_dtype=F32, name="a_proj")
    ab = _mm(hn, w_ab_t, "nt", t, LANES, d, out_dtype=F32, name="a_proj_ab")
    gates = _gates_fwd(ab, alog_row, dt_row, nh, "a_gates")
    q = _conv_fwd(proj, conv_w, 0, inner, "q", "a_conv_q")
    k = _conv_fwd(proj, conv_w, inner, inner, "k", "a_conv_k")
    v = _conv_fwd(proj, conv_w, 2 * inner, inner, "v", "a_conv_v")
    qe, kel, wb, w, u, qk, tm, glb, *carried = _gdn_intra_fwd(q, k, v, gates, nh, "a_intra", comm)
    o, vn, sall = _gdn_scan_fwd(qe, kel, wb, u, qk, glb, nh, "a_scan")
    g = _gate_fwd(o, proj, 3 * inner, inner, "a_gate", norm_w=onw)
    w_out = w_out_of(carried)
    h1 = _mm(g, w_out, "nn", t, d, inner, out_dtype=F32, name="a_out", res=x)
    return h1, (hn, proj, ab, gates, q, k, v, qe, kel, wb, w, u, qk, tm, glb, o, vn, sall, g), w_out, carried


def _layer_a_bwd(dh1, dh1b, x, nw, w_in_t, w_ab_t, conv_w, alog_row, dt_row, onw, w_out, nh, saved, comms_of,
                 own_comm):
    hn, proj, ab, gates, q, k, v, qe, kel, wb, w, u, qk, tm, glb, o, vn, sall, g = saved
    t, d = x.shape
    inner = w_out.shape[0]
    dg = _mm(dh1b, w_out, "nt", t, inner, d, out_dtype=F32, name="a_dgate")
    dw_out = _mm(g, dh1b, "tn", inner, d, t, out_dtype=BF16, name="a_dwout")
    comm_scan, comm_intra = comms_of(dw_out)
    do, dproj, donw = _gate_bwd(dg, o, proj, 3 * inner, inner, "a_gate_bwd", BF16, norm_w=onw,
                                dest=(None, 3 * inner, 4 * inner))
    dvn, dw, dqe, dkel, dqk, dglb, *carried_scan = _gdn_scan_bwd(do, qe, kel, wb, vn, qk, glb, sall, nh,
                                                                 "a_scan_bwd", comm_scan)
    dq, dk, dv, dgates, *carried = _gdn_intra_bwd(q, k, v, gates, tm, w, u, dvn, dw, dqe, dkel, dqk, dglb, nh,
                                                  "a_intra_bwd", comm_intra)
    carried = carried_scan + carried
    dproj, dcq = _conv_bwd(dq, proj, conv_w, 0, inner, "q", "a_conv_q_bwd", dest=(dproj, 0, 4 * inner))
    dproj, dck = _conv_bwd(dk, proj, conv_w, inner, inner, "k", "a_conv_k_bwd", dest=(dproj, inner, 4 * inner))
    dproj, dcv = _conv_bwd(dv, proj, conv_w, 2 * inner, inner, "v", "a_conv_v_bwd",
                           dest=(dproj, 2 * inner, 4 * inner))
    dab, dsmall = _gates_bwd(ab, alog_row, dt_row, dgates, nh, "a_gates_bwd")
    dw_in_t = _mm(dproj, hn, "tn", 4 * inner, d, t, out_dtype=BF16, name="a_dwin", **LONG_K_TILES)
    dw_ab_t = _mm(dab, hn, "tn", LANES, d, t, out_dtype=BF16, name="a_dwin_ab")
    dconv = jnp.concatenate([dcq[:CONV_K], dck[:CONV_K], dcv[:CONV_K]], axis=1)
    dhn = _mm(dab, w_ab_t, "nn", t, d, LANES, out_dtype=F32, name="a_dhn_ab")
    own = own_comm(dw_in_t, dw_ab_t, dconv)
    dhn = _mm(dproj, w_in_t, "nn", t, d, 4 * inner, out_dtype=F32, name="a_dhn", res=dhn, comm=own,
              **LONG_K_TILES)
    dhn, carried_own = dhn if own is not None else (dhn, [])
    dx, _, dnw = _rms_bwd(x, nw, dhn, dh1, "a_rms_bwd")
    return dx, dnw, dsmall, donw, carried, carried_own


def _rows_of(a, rows):
    flat = a.reshape(-1)
    return jnp.pad(flat, (0, rows * LANES - flat.shape[0])).reshape(rows, LANES)


def _to_slabs(g, axis):
    shape = g.shape[:axis] + (N_DEV, g.shape[axis] // N_DEV) + g.shape[axis + 1:]
    return jnp.moveaxis(g.reshape(shape), axis, 0)


def _from_slabs(s, axis):
    m = jnp.moveaxis(s, 0, axis)
    return m.reshape(m.shape[:axis] + (m.shape[axis] * m.shape[axis + 1],) + m.shape[axis + 2:])


def kernel(x, norm_w, a_w_in, a_conv_w, a_a_log, a_dt_bias, a_out_norm_w, a_w_out, b_w_in, b_q_norm_w, b_k_norm_w, b_rel_bias, b_w_out, loss_target, m_norm_w, m_a_w_in, m_a_conv_w, m_a_a_log, m_a_dt_bias, m_a_out_norm_w, m_a_w_out, m_b_w_in, m_b_q_norm_w, m_b_k_norm_w, m_b_rel_bias, m_b_w_out, v_norm_w, v_a_w_in, v_a_conv_w, v_a_a_log, v_a_dt_bias, v_a_out_norm_w, v_a_w_out, v_b_w_in, v_b_q_norm_w, v_b_k_norm_w, v_b_rel_bias, v_b_w_out):
    xs, target = x[0], loss_target[0]
    nh = a_a_log.shape[-1]
    inner = N_DEV * a_w_out.shape[1]

    d = xs.shape[1]
    nw0, nw1 = norm_w[0:1], norm_w[1:2]
    hn0, (ga_in, g_conv) = _rms_fwd(
        xs, nw0, "a_rms", comm=_RoutedGather([a_w_in[0].T.astype(BF16), a_conv_w[0]]))
    wa_in_t = ga_in.reshape(-1, d)
    wa_ab_t = jnp.pad(wa_in_t[4 * inner:], ((0, LANES - 2 * nh), (0, 0)))
    conv_w = _from_slabs(g_conv, 1)
    alog_row = jnp.pad(a_a_log, ((0, 0), (0, LANES - nh)))
    dt_row = jnp.pad(a_dt_bias, ((0, 0), (0, LANES - nh)))

    h1, saved_a, wa_out, (gb_in, gb_out, _) = _layer_a_fwd(
        xs, hn0, wa_in_t, wa_ab_t, conv_w, alog_row, dt_row, a_out_norm_w, nh,
        _Comm("gather", [b_w_in[0].T.astype(BF16), b_w_out[0].astype(BF16), a_w_out[0].astype(BF16)]),
        lambda gathered: _from_slabs(gathered[2], 0))
    wb_in_t = gb_in.reshape(-1, d)
    wb_out = _from_slabs(gb_out, 0)
    bias = _bias_tiles(_pad_rel_bias(b_rel_bias[0]), "b_bias_tiles")
    (dh2, dh2b, loss_row), saved_b = _layer_b_fwd(h1, nw1, wb_in_t, b_q_norm_w, b_k_norm_w, bias, wb_out, target)

    dh1, dh1b, dnw1, dwb_in_t, dqw, dkw, drb, dwb_out = _layer_b_bwd(
        dh2, dh2b, h1, nw1, wb_in_t, b_q_norm_w, b_k_norm_w, bias, wb_out, saved_b)

    def exchange_early(dwa_out):
        return (_Comm("exchange", [_to_slabs(dwb_out, 0).astype(BF16), _to_slabs(dwa_out, 0).astype(BF16)]),
                _Comm("exchange", [dwb_in_t.reshape(N_DEV, -1, d).astype(BF16)]))

    def exchange_last(dwa_in_t, dwa_ab_t, dconv):
        full = jnp.concatenate([dwa_in_t, dwa_ab_t[:2 * nh]], axis=0)
        return _Comm("exchange", [full.reshape(N_DEV, -1, d).astype(BF16), _to_slabs(dconv, 1)])

    dx, dnw0, dsmall, donw, (pb_out, pa_out, pb_in), (pa_in, p_conv) = _layer_a_bwd(
        dh1, dh1b, xs, nw0, wa_in_t, wa_ab_t, conv_w, alog_row, dt_row, a_out_norm_w, wa_out, nh, saved_a,
        exchange_early, exchange_last)
    big = {}
    for name, p, w, m, v in (("a_w_in", pa_in, a_w_in, m_a_w_in, v_a_w_in),
                             ("a_w_out", pa_out, a_w_out, m_a_w_out, v_a_w_out),
                             ("b_w_in", pb_in, b_w_in, m_b_w_in, v_b_w_in),
                             ("b_w_out", pb_out, b_w_out, m_b_w_out, v_b_w_out),
                             ("a_conv_w", p_conv, a_conv_w, m_a_conv_w, v_a_conv_w)):
        big[name] = [o[None] for o in _adamw(p, w[0], m[0], v[0], "adamw_" + name,
                                             transposed=name in ("a_w_in", "b_w_in"))]

    small = (("norm_w", norm_w, m_norm_w, v_norm_w, jnp.concatenate([dnw0, dnw1], axis=0)),
             ("a_a_log", a_a_log, m_a_a_log, v_a_a_log, dsmall[0:1, :nh]),
             ("a_dt_bias", a_dt_bias, m_a_dt_bias, v_a_dt_bias, dsmall[1:2, :nh]),
             ("a_out_norm_w", a_out_norm_w, m_a_out_norm_w, v_a_out_norm_w, donw),
             ("b_q_norm_w", b_q_norm_w, m_b_q_norm_w, v_b_q_norm_w, dqw),
             ("b_k_norm_w", b_k_norm_w, m_b_k_norm_w, v_b_k_norm_w, dkw),
             ("b_rel_bias", b_rel_bias, m_b_rel_bias, v_b_rel_bias, drb))
    rows = [8 * (-(-w.size // (8 * LANES))) for _, w, _, _, _ in small]
    pack = lambda arrs: jnp.concatenate([_rows_of(a, r) for a, r in zip(arrs, rows)] + [jnp.zeros((8, LANES), F32)], axis=0)
    g_pack = jnp.concatenate([_rows_of(g, r) for (_, _, _, _, g), r in zip(small, rows)]
                             + [jnp.broadcast_to(loss_row, (8, LANES))], axis=0)
    (g_all,) = _comm_call(_Comm("gather", [g_pack]), "gather_small_grads")
    outs_small = _adamw(g_all, pack([s[1] for s in small]), pack([s[2] for s in small]),
                        pack([s[3] for s in small]), "adamw_small")
    start = 0
    for (name, w, _, _, _), r in zip(small, rows):
        big[name] = [o[start:start + r].reshape(-1)[:w.size].reshape(w.shape) for o in outs_small]
        start += r
    loss = outs_small[0][start, 0]

    order = ("norm_w", "a_w_in", "a_conv_w", "a_a_log", "a_dt_bias", "a_out_norm_w", "a_w_out", "b_w_in",
             "b_q_norm_w", "b_k_norm_w", "b_rel_bias", "b_w_out")
    return (loss, dx[None]) + tuple(big[n][i] for i in range(4) for n in order)
```

```python
import functools

import jax
import jax.numpy as jnp
from jax import lax
from jax.experimental import pallas as pl
from jax.experimental.pallas import tpu as pltpu

F32 = jnp.float32
BF16 = jnp.bfloat16
MESH_IDS = pl.DeviceIdType.MESH
N_DEV = 8
CHUNK = 64
HEAD_DIM = 128
EPS = 1e-6
CONV_K = 4
LEFT_CHUNKS = 8
REL_CLIP = 256
Q_TILE = LEFT_CHUNKS * CHUNK
ADAM_LR = 0.001
ADAM_B1 = 0.9
ADAM_B2 = 0.999
ADAM_EPS = 1e-08
ADAM_WD = 0.01
ADAM_STEP = 10
NEG_BIG = -1e30
VMEM_LIMIT_BYTES = 56 * 1024 * 1024
HIGHEST = lax.Precision.HIGHEST
ANY = pl.BlockSpec(memory_space=pl.ANY)


def _cparams(*sem):
    return pltpu.CompilerParams(dimension_semantics=tuple(sem), vmem_limit_bytes=VMEM_LIMIT_BYTES)


NN, NT, TN = (((1,), (0,)), ((), ())), (((1,), (1,)), ((), ())), (((0,), (0,)), ((), ()))
BNN, BNT, BTN = (((2,), (1,)), ((0,), (0,))), (((2,), (2,)), ((0,), (0,))), (((1,), (1,)), ((0,), (0,)))


def _dot(a, b, dims, precision=None):
    return lax.dot_general(a, b, dims, preferred_element_type=F32, precision=precision)


def _nn(a, b, precision=None):
    return _dot(a, b, NN, precision)


def _nt(a, b, precision=None):
    return _dot(a, b, NT, precision)


def _tn(a, b, precision=None):
    return _dot(a, b, TN, precision)


def _bf(x):
    return x.astype(BF16)


def _split(x, pieces=2):
    out = []
    for _ in range(pieces - 1):
        hi = x.astype(BF16)
        out.append(hi)
        x = x - hi.astype(F32)
    return out + [x.astype(BF16)]


def _dot3(a, b, dims):
    (ah, al), (bh, bl) = _split(a), _split(b)
    return _dot(ah, bh, dims) + (_dot(ah, bl, dims) + _dot(al, bh, dims))


def _dot_exact(a, b, dims, split_b):
    if split_b:
        a = a.astype(BF16)
        parts = [_dot(a, p, dims) for p in _split(b, 3)]
    else:
        b = b.astype(BF16)
        parts = [_dot(p, b, dims) for p in _split(a, 3)]
    return parts[0] + (parts[1] + parts[2])


def _sigmoid(x):
    return 0.5 * jnp.tanh(0.5 * x) + 0.5


def _silu(x):
    return x * _sigmoid(x)


def _dsilu(x):
    s = _sigmoid(x)
    return s * (1.0 + x * (1.0 - s))


def _my_pos():
    return lax.axis_index("x"), lax.axis_index("y"), lax.axis_index("c")


def _peers(x, y, c):
    def flip(v, f):
        return 1 - v if f else v

    return [(flip(x, kx), flip(y, ky), flip(c, kc)) for kx in (0, 1) for ky in (0, 1) for kc in (0, 1)][1:]


def _lin(p):
    return 4 * p[0] + 2 * p[1] + p[2]


class _Comm:
    def __init__(self, kind, arrays):
        self.kind, self.arrays, self.n = kind, list(arrays), len(arrays)

    def out_shape(self):
        lead = (N_DEV,) if self.kind == "gather" else ()
        return [jax.ShapeDtypeStruct(lead + a.shape, a.dtype) for a in self.arrays]

    def scratch(self):
        return [pltpu.SemaphoreType.DMA((7 * self.n,)), pltpu.SemaphoreType.DMA((7 * self.n,)),
                pltpu.SemaphoreType.DMA((self.n,))]

    def _copies(self, ins, outs, sems, arrivals):
        send_sems, recv_sems, local_sems = sems
        x, y, c = _my_pos()
        me = _lin((x, y, c))
        gather = self.kind == "gather"
        mine = [ins[t] if gather else ins[t].at[me] for t in range(self.n)]
        remote = []
        for k, peer in enumerate(_peers(x, y, c)):
            for t in range(self.n):
                if arrivals:
                    src, dst = mine[t], outs[t].at[_lin(peer)]
                else:
                    src, dst = (ins[t] if gather else ins[t].at[_lin(peer)]), outs[t].at[me]
                remote.append(pltpu.make_async_remote_copy(
                    src_ref=src, dst_ref=dst, send_sem=send_sems.at[k * self.n + t],
                    recv_sem=recv_sems.at[k * self.n + t], device_id=peer, device_id_type=MESH_IDS))
        if arrivals:
            return remote
        return [pltpu.make_async_copy(mine[t], outs[t].at[me], local_sems.at[t]) for t in range(self.n)], remote

    def start(self, ins, outs, sems):
        local, sends = self._copies(ins, outs, sems, False)
        for cp in local + sends:
            cp.start()

    def finish(self, ins, outs, sems):
        for cp in self._copies(ins, outs, sems, True):
            cp.wait_recv()
        local, sends = self._copies(ins, outs, sems, False)
        for cp in sends:
            cp.wait_send()
        for cp in local:
            cp.wait()


def _xor(a, b):
    return a + b - 2 * a * b


class _RoutedGather(_Comm):
    def __init__(self, arrays):
        super().__init__("gather", arrays)

    def _plan(self, outs, sems):
        send_sems, recv_sems, _ = sems
        x, y, c = _my_pos()
        sib, xn, yn, dg = (x, y, 1 - c), (1 - x, y, c), (x, 1 - y, c), (1 - x, 1 - y, c)
        via = (_xor(x, 1 - c), _xor(y, c), c)
        onto = (_xor(x, c), _xor(y, 1 - c), c)
        routes = [(None, sib, sib), (None, xn, xn), (None, yn, yn), (via, onto, dg),
                  (xn, sib, (1 - x, y, 1 - c)), (yn, sib, (x, 1 - y, 1 - c)), (dg, sib, (1 - x, 1 - y, 1 - c))]

        def copy(k, t, src, slot, target):
            return pltpu.make_async_remote_copy(
                src_ref=src, dst_ref=outs[t].at[slot], send_sem=send_sems.at[k * self.n + t],
                recv_sem=recv_sems.at[k * self.n + t], device_id=target, device_id_type=MESH_IDS)

        return (x, y, c), routes, copy

    def start(self, ins, outs, sems):
        me, routes, copy = self._plan(outs, sems)
        for t in range(self.n):
            pltpu.make_async_copy(ins[t], outs[t].at[_lin(me)], sems[2].at[t]).start()
            for k in range(3):
                copy(k, t, ins[t], _lin(me), routes[k][1]).start()

    def finish(self, ins, outs, sems):
        me, routes, copy = self._plan(outs, sems)

        def arrived(k):
            for t in range(self.n):
                copy(k, t, ins[t], _lin(routes[k][2]), me).wait_recv()

        def pass_on(k):
            for t in range(self.n):
                copy(k, t, outs[t].at[_lin(routes[k][0])], _lin(routes[k][0]), routes[k][1]).start()

        arrived(1)
        arrived(2)
        for k in (3, 4, 5):
            pass_on(k)
        arrived(3)
        pass_on(6)
        for k in (0, 4, 5, 6):
            arrived(k)
        for t in range(self.n):
            for k in range(7):
                src = ins[t] if k < 3 else outs[t].at[_lin(routes[k][0])]
                copy(k, t, src, _lin(me), routes[k][1]).wait_send()
            pltpu.make_async_copy(ins[t], outs[t].at[_lin(me)], sems[2].at[t]).wait()


def _comm_call(comm, name):
    n = comm.n

    def body(*refs):
        ins, outs, sems = refs[:n], refs[n:2 * n], refs[2 * n:]
        comm.start(ins, outs, sems)
        comm.finish(ins, outs, sems)

    return pl.pallas_call(
        body, name=name, out_shape=comm.out_shape(), in_specs=[ANY] * n, out_specs=[ANY] * n,
        scratch_shapes=comm.scratch(),
    )(*comm.arrays)


def _grid_call(body, *, name, grid, in_specs, out_specs, out_shape, args, scratch_shapes=(), semantics=None, comm=None):
    if comm is None:
        return pl.pallas_call(
            body, name=name, grid=grid, in_specs=in_specs, out_specs=out_specs, out_shape=out_shape,
            scratch_shapes=list(scratch_shapes), compiler_params=_cparams(*semantics),
        )(*args)
    n_in, n_out, n_sc, n = len(in_specs), len(out_specs), len(scratch_shapes), comm.n

    def full(*refs):
        ins, refs = refs[:n_in], refs[n_in:]
        cins, refs = refs[:n], refs[n:]
        outs, refs = refs[:n_out], refs[n_out:]
        couts, refs = refs[:n], refs[n:]
        scratch, sems = refs[:n_sc], refs[n_sc:]
        ids = [pl.program_id(a) for a in range(len(grid))]
        first = functools.reduce(jnp.logical_and, [i == 0 for i in ids])
        last = functools.reduce(jnp.logical_and, [i == g - 1 for i, g in zip(ids, grid)])

        @pl.when(first)
        def _():
            comm.start(cins, couts, sems)

        body(*ins, *outs, *scratch)

        @pl.when(last)
        def _():
            comm.finish(cins, couts, sems)

    return pl.pallas_call(
        full, name=name, grid=grid, in_specs=list(in_specs) + [ANY] * n, out_specs=list(out_specs) + [ANY] * n,
        out_shape=list(out_shape) + comm.out_shape(), scratch_shapes=list(scratch_shapes) + comm.scratch(),
        compiler_params=_cparams(*(["arbitrary"] * len(grid))),
    )(*(list(args) + comm.arrays))


LONG_K_TILES = dict(tm=512, tn=512, tk=8192)


def _mm(a, b, mode, m, n, k, *, out_dtype, name, tm=1024, tn=1024, tk=2048,
        a_m0=0, a_k0=0, b_n0=0, b_k0=0, res=None, comm=None, loss_target=None):
    tm, tn, tk = min(tm, m), min(tn, n), min(tk, k)
    nm, nn, nk = m // tm, n // tn, k // tk
    assert nm * tm == m and nn * tn == n and nk * tk == k
    am, ak, bn, bk = a_m0 // tm, a_k0 // tk, b_n0 // tn, b_k0 // tk
    assert am * tm == a_m0 and ak * tk == a_k0 and bn * tn == b_n0 and bk * tk == b_k0
    if mode == "tn":
        a_spec = pl.BlockSpec((tk, tm), lambda i, j, q: (q + ak, i + am))
        a_dims = (0,)
    else:
        a_spec = pl.BlockSpec((tm, tk), lambda i, j, q: (i + am, q + ak))
        a_dims = (1,)
    if mode == "nt":
        b_spec = pl.BlockSpec((tn, tk), lambda i, j, q: (j + bn, q + bk))
        b_dims = (1,)
    else:
        b_spec = pl.BlockSpec((tk, tn), lambda i, j, q: (q + bk, j + bn))
        b_dims = (0,)
    o_spec = pl.BlockSpec((tm, tn), lambda i, j, q: (i, j))
    has_res = res is not None
    has_loss = loss_target is not None
    n_in = 2 + has_res + has_loss
    n_out = 3 if has_loss else 1

    def body(*refs):
        a_ref, b_ref = refs[0], refs[1]
        res_ref = refs[2] if has_res else None
        o_ref = refs[n_in]
        p = _dot(a_ref[...], b_ref[...], ((a_dims, b_dims), ((), ())))

        def finish(total):
            if has_res:
                total = total + res_ref[...].astype(F32)
            if not has_loss:
                o_ref[...] = total.astype(out_dtype)
                return
            err = total - refs[n_in - 1][...]
            grad = err * (1.0 / n)
            o_ref[...] = grad
            refs[n_in + 1][...] = grad.astype(BF16)
            l_ref = refs[n_in + 2]
            part = jnp.zeros((1, LANES), F32) + 0.5 * jnp.sum(err * err) * (1.0 / n)
            first = (pl.program_id(0) == 0) & (pl.program_id(1) == 0)

            @pl.when(first)
            def _():
                l_ref[...] = part

            @pl.when(jnp.logical_not(first))
            def _():
                l_ref[...] += part

        if nk == 1:
            finish(p)
        else:
            acc_ref = refs[n_in + n_out]
            q = pl.program_id(2)

            @pl.when(q == 0)
            def _():
                acc_ref[...] = p

            @pl.when(q > 0)
            def _():
                acc_ref[...] += p

            @pl.when(q == nk - 1)
            def _():
                finish(acc_ref[...])

    extra_in = ([res] if has_res else []) + ([loss_target] if has_loss else [])
    if has_loss:
        return _grid_call(
            body, name=name, grid=(nm, nn, nk), in_specs=[a_spec, b_spec] + [o_spec] * len(extra_in),
            out_specs=[o_spec, o_spec, pl.BlockSpec((1, LANES), lambda i, j, q: (0, 0))],
            out_shape=[jax.ShapeDtypeStruct((m, n), F32), jax.ShapeDtypeStruct((m, n), BF16),
                       jax.ShapeDtypeStruct((1, LANES), F32)],
            scratch_shapes=[pltpu.VMEM((tm, tn), F32)] if nk > 1 else [],
            args=[a, b] + extra_in, semantics=("arbitrary", "arbitrary", "arbitrary"))
    out, *carried = _grid_call(
        body, name=name, grid=(nm, nn, nk),
        in_specs=[a_spec, b_spec] + [o_spec] * len(extra_in),
        out_specs=[o_spec], out_shape=[jax.ShapeDtypeStruct((m, n), out_dtype)],
        scratch_shapes=[pltpu.VMEM((tm, tn), F32)] if nk > 1 else [],
        args=[a, b] + extra_in, semantics=("parallel", "parallel", "arbitrary"), comm=comm)
    return out if comm is None else (out, carried)


def _rms_fwd(x, w, name, tr=512, comm=None):
    t, d = x.shape
    tr = min(tr, t)

    def body(x_ref, w_ref, o_ref):
        xv = x_ref[...]
        r = lax.rsqrt(jnp.mean(xv * xv, axis=-1, keepdims=True) + EPS)
        o_ref[...] = (xv * r * w_ref[...]).astype(BF16)

    out, *carried = _grid_call(
        body, name=name, grid=(t // tr,),
        in_specs=[pl.BlockSpec((tr, d), lambda i: (i, 0)), pl.BlockSpec((1, d), lambda i: (0, 0))],
        out_specs=[pl.BlockSpec((tr, d), lambda i: (i, 0))],
        out_shape=[jax.ShapeDtypeStruct((t, d), BF16)], args=(x, w), semantics=("parallel",), comm=comm)
    return out if comm is None else (out, carried)


def _rms_bwd(x, w, dy, dres, name, tr=512):
    t, d = x.shape
    tr = min(tr, t)

    def body(x_ref, w_ref, dy_ref, dres_ref, dx_ref, dxb_ref, dw_ref):
        xv = x_ref[...]
        dyv = dy_ref[...].astype(F32)
        r = lax.rsqrt(jnp.mean(xv * xv, axis=-1, keepdims=True) + EPS)
        gy = dyv * w_ref[...]
        proj = jnp.sum(gy * xv, axis=-1, keepdims=True) * (1.0 / d)
        dx = dres_ref[...] + r * gy - xv * (r * r * r) * proj
        dx_ref[...] = dx
        dxb_ref[...] = dx.astype(BF16)
        part = jnp.sum(dyv * xv * r, axis=0, keepdims=True)

        @pl.when(pl.program_id(0) == 0)
        def _():
            dw_ref[...] = part

        @pl.when(pl.program_id(0) > 0)
        def _():
            dw_ref[...] += part

    row = pl.BlockSpec((tr, d), lambda i: (i, 0))
    vec = pl.BlockSpec((1, d), lambda i: (0, 0))
    return pl.pallas_call(
        body, name=name, grid=(t // tr,),
        in_specs=[row, vec, row, row], out_specs=[row, row, vec],
        out_shape=[jax.ShapeDtypeStruct((t, d), F32), jax.ShapeDtypeStruct((t, d), BF16),
                   jax.ShapeDtypeStruct((1, d), F32)],
        compiler_params=_cparams("arbitrary"),
    )(x, w, dy, dres)


def _adamw(parts, w, m, v, name, tr=128, transposed=False):
    r, c = w.shape
    tr = tr if r % tr == 0 else r
    c1 = 1.0 - ADAM_B1 ** ADAM_STEP
    c2 = 1.0 - ADAM_B2 ** ADAM_STEP

    def body(p_ref, w_ref, m_ref, v_ref, g_ref, d_ref, nm_ref, nv_ref):
        g = p_ref[0].astype(F32)
        for s in range(1, N_DEV):
            g = g + p_ref[s].astype(F32)
        if transposed:
            i, j = lax.broadcasted_iota(jnp.int32, (tr, tr), 0), lax.broadcasted_iota(jnp.int32, (tr, tr), 1)
            g = _dot_exact(jnp.where(i == j, 1.0, 0.0), g, NT, True)
        nm = ADAM_B1 * m_ref[...] + (1.0 - ADAM_B1) * g
        nv = ADAM_B2 * v_ref[...] + (1.0 - ADAM_B2) * (g * g)
        m_hat = nm / c1
        v_hat = nv / c2
        g_ref[...] = g
        d_ref[...] = -ADAM_LR * (m_hat / (jnp.sqrt(v_hat) + ADAM_EPS) + ADAM_WD * w_ref[...])
        nm_ref[...] = nm
        nv_ref[...] = nv

    blk = pl.BlockSpec((tr, c), lambda i: (i, 0))
    p_spec = (pl.BlockSpec((N_DEV, c, tr), lambda i: (0, 0, i)) if transposed
              else pl.BlockSpec((N_DEV, tr, c), lambda i: (0, i, 0)))
    return pl.pallas_call(
        body, name=name, grid=(r // tr,),
        in_specs=[p_spec, blk, blk, blk],
        out_specs=[blk] * 4, out_shape=[jax.ShapeDtypeStruct((r, c), F32)] * 4,
        compiler_params=_cparams("parallel"),
    )(parts, w, m, v)


ROW_TILE, ROW_HEADS = 512, 16
CONV_ROWS, CONV_HEADS = 512, 8


def _window(dest, inner, n_in, out_index):
    if dest is None:
        return inner, 0, [], [], {}
    buf, col0, total = dest
    if buf is None:
        return total, col0, [], [], {}
    return total, col0, [buf], [ANY], {n_in: out_index}


def _skip_ref(body, at, count):
    return body if count == 0 else (lambda *refs: body(*refs[:at], *refs[at + count:]))


def _heads_of(x, nh):
    return [x[:, h * HEAD_DIM:(h + 1) * HEAD_DIM] for h in range(nh)]


def _headnorm_fwd(proj, w, col0, inner, name, tr=ROW_TILE, hb=ROW_HEADS):
    t = proj.shape[0]
    tr = min(tr, t)
    hb = min(hb, inner // HEAD_DIM)
    wc = hb * HEAD_DIM
    c0 = col0 // wc

    def body(x_ref, w_ref, o_ref):
        outs = []
        for xh in _heads_of(x_ref[...], hb):
            r = lax.rsqrt(jnp.mean(xh * xh, axis=-1, keepdims=True) + EPS)
            outs.append((xh * r * w_ref[...]).astype(BF16))
        o_ref[...] = jnp.concatenate(outs, axis=1)

    return pl.pallas_call(
        body, name=name, grid=(t // tr, inner // wc),
        in_specs=[pl.BlockSpec((tr, wc), lambda i, j: (i, j + c0)), pl.BlockSpec((1, HEAD_DIM), lambda i, j: (0, 0))],
        out_specs=pl.BlockSpec((tr, wc), lambda i, j: (i, j)),
        out_shape=jax.ShapeDtypeStruct((t, inner), BF16),
        compiler_params=_cparams("parallel", "parallel"),
    )(proj, w)


def _headnorm_bwd(dy, proj, w, col0, inner, name, tr=ROW_TILE, hb=ROW_HEADS, dest=None):
    t = proj.shape[0]
    tr = min(tr, t)
    hb = min(hb, inner // HEAD_DIM)
    wc = hb * HEAD_DIM
    c0 = col0 // wc
    width, out0, more, more_specs, aliases = _window(dest, inner, 3, 0)

    def body(dy_ref, x_ref, w_ref, dx_ref, dw_ref):
        outs = []
        part = jnp.zeros((1, HEAD_DIM), F32)
        for dyh, xh in zip(_heads_of(dy_ref[...], hb), _heads_of(x_ref[...], hb)):
            r = lax.rsqrt(jnp.mean(xh * xh, axis=-1, keepdims=True) + EPS)
            gy = dyh * w_ref[...]
            pr = jnp.sum(gy * xh, axis=-1, keepdims=True) * (1.0 / HEAD_DIM)
            outs.append((r * gy - xh * (r * r * r) * pr).astype(BF16))
            part = part + jnp.sum(dyh * xh * r, axis=0, keepdims=True)
        dx_ref[...] = jnp.concatenate(outs, axis=1)
        first = (pl.program_id(0) == 0) & (pl.program_id(1) == 0)

        @pl.when(first)
        def _():
            dw_ref[...] = part

        @pl.when(jnp.logical_not(first))
        def _():
            dw_ref[...] += part

    blk = pl.BlockSpec((tr, wc), lambda i, j: (i, j))
    return pl.pallas_call(
        _skip_ref(body, 3, len(more)), name=name, grid=(t // tr, inner // wc),
        in_specs=[blk, pl.BlockSpec((tr, wc), lambda i, j: (i, j + c0)),
                  pl.BlockSpec((1, HEAD_DIM), lambda i, j: (0, 0))] + more_specs,
        out_specs=[pl.BlockSpec((tr, wc), lambda i, j: (i, j + out0 // wc)),
                   pl.BlockSpec((1, HEAD_DIM), lambda i, j: (0, 0))],
        out_shape=[jax.ShapeDtypeStruct((t, width), BF16), jax.ShapeDtypeStruct((1, HEAD_DIM), F32)],
        input_output_aliases=aliases, compiler_params=_cparams("arbitrary", "arbitrary"),
    )(dy, proj, w, *more)


def _gate_fwd(o, proj, zcol0, inner, name, norm_w=None, tr=ROW_TILE, hb=ROW_HEADS):
    t = o.shape[0]
    tr = min(tr, t)
    hb = min(hb, inner // HEAD_DIM)
    wc = hb * HEAD_DIM
    c0 = zcol0 // wc
    has_w = norm_w is not None

    def body(*refs):
        o_ref, z_ref = refs[0], refs[1]
        out_ref = refs[2 + has_w]
        outs = []
        for oh, zh in zip(_heads_of(o_ref[...], hb), _heads_of(z_ref[...], hb)):
            if has_w:
                r = lax.rsqrt(jnp.mean(oh * oh, axis=-1, keepdims=True) + EPS)
                oh = oh * r * refs[2][...]
            outs.append((oh * _silu(zh)).astype(BF16))
        out_ref[...] = jnp.concatenate(outs, axis=1)

    blk = pl.BlockSpec((tr, wc), lambda i, j: (i, j))
    vec = pl.BlockSpec((1, HEAD_DIM), lambda i, j: (0, 0))
    return pl.pallas_call(
        body, name=name, grid=(t // tr, inner // wc),
        in_specs=[blk, pl.BlockSpec((tr, wc), lambda i, j: (i, j + c0))] + ([vec] if has_w else []),
        out_specs=blk, out_shape=jax.ShapeDtypeStruct((t, inner), BF16),
        compiler_params=_cparams("parallel", "parallel"),
    )(*([o, proj] + ([norm_w] if has_w else [])))


def _gate_bwd(dg, o, proj, zcol0, inner, name, do_dtype, norm_w=None, tr=ROW_TILE, hb=ROW_HEADS, dest=None):
    t = o.shape[0]
    tr = min(tr, t)
    hb = min(hb, inner // HEAD_DIM)
    wc = hb * HEAD_DIM
    c0 = zcol0 // wc
    has_w = norm_w is not None
    width, out0, more, more_specs, aliases = _window(dest, inner, 3 + has_w, 1)

    def body(*refs):
        dg_ref, o_ref, z_ref = refs[0], refs[1], refs[2]
        do_ref, dz_ref = refs[3 + has_w], refs[4 + has_w]
        dos, dzs = [], []
        part = jnp.zeros((1, HEAD_DIM), F32)
        for dgh, oh, zh in zip(_heads_of(dg_ref[...], hb), _heads_of(o_ref[...], hb), _heads_of(z_ref[...], hb)):
            dy = dgh * _silu(zh)
            if has_w:
                w = refs[3][...]
                r = lax.rsqrt(jnp.mean(oh * oh, axis=-1, keepdims=True) + EPS)
                on = oh * r
                dzs.append((dgh * on * w * _dsilu(zh)).astype(BF16))
                gy = dy * w
                pr = jnp.sum(gy * oh, axis=-1, keepdims=True) * (1.0 / HEAD_DIM)
                dos.append((r * gy - oh * (r * r * r) * pr).astype(do_dtype))
                part = part + jnp.sum(dy * on, axis=0, keepdims=True)
            else:
                dzs.append((dgh * oh * _dsilu(zh)).astype(BF16))
                dos.append(dy.astype(do_dtype))
        do_ref[...] = jnp.concatenate(dos, axis=1)
        dz_ref[...] = jnp.concatenate(dzs, axis=1)
        if has_w:
            dw_ref = refs[6]
            first = (pl.program_id(0) == 0) & (pl.program_id(1) == 0)

            @pl.when(first)
            def _():
                dw_ref[...] = part

            @pl.when(jnp.logical_not(first))
            def _():
                dw_ref[...] += part

    blk = pl.BlockSpec((tr, wc), lambda i, j: (i, j))
    vec = pl.BlockSpec((1, HEAD_DIM), lambda i, j: (0, 0))
    return pl.pallas_call(
        _skip_ref(body, 3 + has_w, len(more)), name=name, grid=(t // tr, inner // wc),
        in_specs=[blk, blk, pl.BlockSpec((tr, wc), lambda i, j: (i, j + c0))] + ([vec] if has_w else []) + more_specs,
        out_specs=[blk, pl.BlockSpec((tr, wc), lambda i, j: (i, j + out0 // wc))] + ([vec] if has_w else []),
        out_shape=[jax.ShapeDtypeStruct((t, inner), do_dtype), jax.ShapeDtypeStruct((t, width), BF16)]
        + ([jax.ShapeDtypeStruct((1, HEAD_DIM), F32)] if has_w else []),
        input_output_aliases=aliases, compiler_params=_cparams("arbitrary", "arbitrary"),
    )(*([dg, o, proj] + ([norm_w] if has_w else []) + more))


N_REL = 2 * REL_CLIP + 1
REL_PAD = 640
WIN = 2 * Q_TILE


def _diag_onehot():
    i = lax.broadcasted_iota(jnp.int32, (REL_PAD, WIN), 0)
    j = lax.broadcasted_iota(jnp.int32, (REL_PAD, WIN), 1)
    rel = jnp.where(j < Q_TILE + CHUNK, Q_TILE - j, Q_TILE + WIN - j)
    used = (j < Q_TILE + CHUNK) | (j > WIN - CHUNK)
    idx = jnp.clip(rel, -REL_CLIP, REL_CLIP) + REL_CLIP
    return jnp.where(used & (i == idx), 1.0, 0.0).astype(F32)


def _band_mask():
    r = lax.broadcasted_iota(jnp.int32, (Q_TILE, WIN), 0) // CHUNK
    kc = lax.broadcasted_iota(jnp.int32, (Q_TILE, WIN), 1) // CHUNK - LEFT_CHUNKS
    return (kc <= r) & (kc >= r - LEFT_CHUNKS)


def _bias_tiles(rel_bias_pad, name):
    nh = rel_bias_pad.shape[0]

    def body(rb_ref, o_ref):
        dvec = _nn(rb_ref[...], _diag_onehot(), HIGHEST)[0:1, :]
        tile = pltpu.roll(jnp.broadcast_to(dvec, (Q_TILE, WIN)), 0, 1, stride=1, stride_axis=0)
        o_ref[...] = jnp.where(_band_mask(), tile, NEG_BIG)

    return pl.pallas_call(
        body, name=name, grid=(nh,),
        in_specs=[pl.BlockSpec((None, 8, REL_PAD), lambda h: (h, 0, 0))],
        out_specs=pl.BlockSpec((None, Q_TILE, WIN), lambda h: (h, 0, 0)),
        out_shape=jax.ShapeDtypeStruct((nh, Q_TILE, WIN), F32),
        compiler_params=_cparams("parallel"),
    )(rel_bias_pad)


def _bias_grad(dtile, name):
    nh = dtile.shape[0]

    def body(d_ref, o_ref):
        ri = lax.broadcasted_iota(jnp.int32, (Q_TILE, Q_TILE), 0)
        ci = lax.broadcasted_iota(jnp.int32, (Q_TILE, Q_TILE), 1)
        flip = jnp.where(ri + ci == Q_TILE - 1, 1.0, 0.0).astype(F32)
        rev = _dot_exact(flip, d_ref[...], NN, True)
        rolled = pltpu.roll(rev, WIN - (Q_TILE - 1), 1, stride=1, stride_axis=0)
        diag = jnp.broadcast_to(jnp.sum(rolled, axis=0, keepdims=True), (8, WIN))
        o_ref[...] = _nt(diag, _diag_onehot(), HIGHEST)

    return pl.pallas_call(
        body, name=name, grid=(nh,),
        in_specs=[pl.BlockSpec((None, Q_TILE, WIN), lambda h: (h, 0, 0))],
        out_specs=pl.BlockSpec((None, 8, REL_PAD), lambda h: (h, 0, 0)),
        out_shape=jax.ShapeDtypeStruct((nh, 8, REL_PAD), F32),
        compiler_params=_cparams("parallel"),
    )(dtile)


GROUP = 2 * CHUNK
BAND = Q_TILE + GROUP


N_GROUPS = Q_TILE // GROUP
ATTN_HB = 2
ATTN_HB_FWD = 4


def _head_cols(j):
    return slice(j * HEAD_DIM, (j + 1) * HEAD_DIM)


def _groups(ref, units):
    return jnp.stack([ref[GROUP * g:GROUP * (g + 1), _head_cols(j)] for j, g in units])


def _bands(r0_ref, r1_ref, units):
    return jnp.stack([jnp.concatenate([r0_ref[GROUP * g:, _head_cols(j)], r1_ref[:GROUP * (g + 1), _head_cols(j)]],
                                      axis=0) for j, g in units])


def _group_probs(q, kw, b_ref, units, first_tile):
    bias = jnp.stack([b_ref[j, GROUP * g:GROUP * (g + 1), GROUP * g:GROUP * g + BAND] for j, g in units])
    s = _dot(q, kw, BNT) * (HEAD_DIM ** -0.5) + bias
    col = jnp.stack([lax.broadcasted_iota(jnp.int32, (GROUP, BAND), 1) + GROUP * g for _, g in units])
    s = jnp.where(first_tile & (col < Q_TILE), NEG_BIG, s)
    p = jnp.exp(s - jnp.max(s, axis=-1, keepdims=True))
    return p * (1.0 / jnp.sum(p, axis=-1, keepdims=True))


def _attn_fwd(q, k, v, v_col0, bias, name):
    t, inner = q.shape
    nh, nt = inner // HEAD_DIM, t // Q_TILE
    hb = min(ATTN_HB_FWD, nh)
    wc = hb * HEAD_DIM
    vh = v_col0 // wc
    units = [(j, g) for j in range(hb) for g in range(N_GROUPS)]

    def body(q_ref, k0_ref, k1_ref, v0_ref, v1_ref, b_ref, o_ref):
        p = _group_probs(_groups(q_ref, units), _bands(k0_ref, k1_ref, units), b_ref, units, pl.program_id(1) == 0)
        o = _dot(_bf(p), _bf(_bands(v0_ref, v1_ref, units)), BNN)
        for n, (j, g) in enumerate(units):
            o_ref[GROUP * g:GROUP * (g + 1), _head_cols(j)] = o[n]

    cur = pl.BlockSpec((Q_TILE, wc), lambda h, i: (i, h))
    prev = pl.BlockSpec((Q_TILE, wc), lambda h, i: (jnp.maximum(i - 1, 0), h))
    v_cur = pl.BlockSpec((Q_TILE, wc), lambda h, i: (i, h + vh))
    v_prev = pl.BlockSpec((Q_TILE, wc), lambda h, i: (jnp.maximum(i - 1, 0), h + vh))
    return pl.pallas_call(
        body, name=name, grid=(nh // hb, nt),
        in_specs=[cur, prev, cur, v_prev, v_cur, pl.BlockSpec((hb, Q_TILE, WIN), lambda h, i: (h, 0, 0))],
        out_specs=cur, out_shape=jax.ShapeDtypeStruct((t, inner), F32),
        compiler_params=_cparams("parallel", "parallel"),
    )(q, k, k, v, v, bias)


def _attn_bwd(q, k, v, v_col0, do, bias, name, dest=None):
    t, inner = q.shape
    nh, nt = inner // HEAD_DIM, t // Q_TILE
    scale = HEAD_DIM ** -0.5
    hb = min(ATTN_HB, nh)
    wc = hb * HEAD_DIM
    units = [(j, g) for j in range(hb) for g in range(N_GROUPS)]

    def body(q_ref, k0_ref, k1_ref, v0_ref, v1_ref, do_ref, b_ref, dq_ref, dk_ref, dv_ref, db_ref,
             ck_ref, cv_ref, wk_ref, wv_ref):
        i = pl.program_id(1)

        @pl.when(i == 0)
        def _():
            ck_ref[...] = jnp.zeros(blk, F32)
            cv_ref[...] = jnp.zeros(blk, F32)
            db_ref[...] = jnp.zeros((hb, Q_TILE, WIN), F32)

        @pl.when(i < nt)
        def _():
            wk_ref[...] = jnp.zeros((WIN, wc), F32)
            wv_ref[...] = jnp.zeros((WIN, wc), F32)
            qv, dov = _groups(q_ref, units), _groups(do_ref, units)
            kw, vw = _bands(k0_ref, k1_ref, units), _bf(_bands(v0_ref, v1_ref, units))
            p = _group_probs(qv, kw, b_ref, units, i == 0)
            dp = _dot(dov, vw, BNT)
            ds = p * (dp - jnp.sum(p * dp, axis=-1, keepdims=True))
            pb, dsb = _bf(p), _bf(ds)
            dq = _dot(dsb, kw, BNN) * scale
            dkw = _dot(dsb, qv, BTN) * scale
            dvw = _dot(pb, dov, BTN)
            for n, (j, g) in enumerate(units):
                rows, cols = slice(GROUP * g, GROUP * (g + 1)), slice(GROUP * g, GROUP * g + BAND)
                db_ref[j, rows, cols] += ds[n]
                dq_ref[rows, _head_cols(j)] = dq[n]
                wk_ref[cols, _head_cols(j)] += dkw[n]
                wv_ref[cols, _head_cols(j)] += dvw[n]
            dk_ref[...] = ck_ref[...] + wk_ref[:Q_TILE, :]
            dv_ref[...] = (cv_ref[...] + wv_ref[:Q_TILE, :]).astype(BF16)
            ck_ref[...] = wk_ref[Q_TILE:, :]
            cv_ref[...] = wv_ref[Q_TILE:, :]

        @pl.when(i == nt)
        def _():
            dk_ref[...] = ck_ref[...]
            dv_ref[...] = cv_ref[...].astype(BF16)

    blk = (Q_TILE, wc)
    cur = pl.BlockSpec(blk, lambda h, i: (jnp.minimum(i, nt - 1), h))
    prev = pl.BlockSpec(blk, lambda h, i: (jnp.clip(i - 1, 0, nt - 1), h))
    lag = pl.BlockSpec(blk, lambda h, i: (jnp.maximum(i - 1, 0), h))
    vh = v_col0 // wc
    v_cur = pl.BlockSpec(blk, lambda h, i: (jnp.minimum(i, nt - 1), h + vh))
    v_prev = pl.BlockSpec(blk, lambda h, i: (jnp.clip(i - 1, 0, nt - 1), h + vh))
    tile = pl.BlockSpec((hb, Q_TILE, WIN), lambda h, i: (h, 0, 0))
    width, out0, more, more_specs, aliases = _window(dest, inner, 7, 2)
    return pl.pallas_call(
        _skip_ref(body, 7, len(more)), name=name, grid=(nh // hb, nt + 1),
        in_specs=[cur, prev, cur, v_prev, v_cur, cur, tile] + more_specs,
        out_specs=[cur, lag, pl.BlockSpec(blk, lambda h, i: (jnp.maximum(i - 1, 0), h + out0 // wc)), tile],
        out_shape=[jax.ShapeDtypeStruct((t, inner), F32)] * 2 + [jax.ShapeDtypeStruct((t, width), BF16),
                                                                 jax.ShapeDtypeStruct((nh, Q_TILE, WIN), F32)],
        scratch_shapes=[pltpu.VMEM(blk, F32), pltpu.VMEM(blk, F32),
                        pltpu.VMEM((WIN, wc), F32), pltpu.VMEM((WIN, wc), F32)],
        input_output_aliases=aliases, compiler_params=_cparams("arbitrary", "arbitrary"),
    )(q, k, k, v, v, do, bias, *more)


def _pad_rel_bias(rel_bias):
    nh = rel_bias.shape[0]
    return jnp.broadcast_to(jnp.pad(rel_bias, ((0, 0), (0, REL_PAD - N_REL)))[:, None, :], (nh, 8, REL_PAD))


def _layer_b_fwd(h1, nw, w_in_t, qw, kw, bias, w_out, target):
    t, d = h1.shape
    inner = w_out.shape[0]
    hn = _rms_fwd(h1, nw, "b_rms")
    proj = _mm(hn, w_in_t, "nt", t, 4 * inner, d, out_dtype=F32, name="b_proj")
    qn = _headnorm_fwd(proj, qw, 0, inner, "b_qnorm")
    kn = _headnorm_fwd(proj, kw, inner, inner, "b_knorm")
    o = _attn_fwd(qn, kn, proj, 2 * inner, bias, "b_attn")
    g = _gate_fwd(o, proj, 3 * inner, inner, "b_gate")
    loss_parts = _mm(g, w_out, "nn", t, d, inner, out_dtype=F32, name="b_out", res=h1, loss_target=target)
    return loss_parts, (hn, proj, qn, kn, o, g)


def _layer_b_bwd(dh2, dh2b, h1, nw, w_in_t, qw, kw, bias, w_out, saved):
    hn, proj, qn, kn, o, g = saved
    t, d = h1.shape
    inner = w_out.shape[0]
    dg = _mm(dh2b, w_out, "nt", t, inner, d, out_dtype=F32, name="b_dgate")
    dw_out = _mm(g, dh2b, "tn", inner, d, t, out_dtype=BF16, name="b_dwout")
    do, dproj = _gate_bwd(dg, o, proj, 3 * inner, inner, "b_gate_bwd", BF16, dest=(None, 3 * inner, 4 * inner))
    dq, dk, dproj, dtile = _attn_bwd(qn, kn, proj, 2 * inner, do, bias, "b_attn_bwd",
                                     dest=(dproj, 2 * inner, 4 * inner))
    dproj, dqw = _headnorm_bwd(dq, proj, qw, 0, inner, "b_qnorm_bwd", dest=(dproj, 0, 4 * inner))
    dproj, dkw = _headnorm_bwd(dk, proj, kw, inner, inner, "b_knorm_bwd", dest=(dproj, inner, 4 * inner))
    dhn = _mm(dproj, w_in_t, "nn", t, d, 4 * inner, out_dtype=F32, name="b_dhn", **LONG_K_TILES)
    dw_in_t = _mm(dproj, hn, "tn", 4 * inner, d, t, out_dtype=BF16, name="b_dwin", **LONG_K_TILES)
    dh1, dh1b, dnw = _rms_bwd(h1, nw, dhn, dh2, "b_rms_bwd")
    drb = _bias_grad(dtile, "b_bias_grad")[:, 0, :N_REL]
    return dh1, dh1b, dnw, dw_in_t, dqw, dkw, drb, dw_out


LANES = 128


def _softplus(x):
    return jnp.maximum(x, 0.0) + jnp.log1p(jnp.exp(-jnp.abs(x)))


def _gates_fwd(ab, alog_row, dt_row, nh, name, tr=1024):
    t = ab.shape[0]
    tr = min(tr, t)

    def body(x_ref, al_ref, dt_ref, o_ref):
        x = x_ref[...]
        lane = lax.broadcasted_iota(jnp.int32, x.shape, 1)
        g = -jnp.exp(al_ref[...]) * _softplus(x + dt_ref[...])
        o_ref[...] = jnp.where(lane < nh, g, jnp.where(lane < 2 * nh, _sigmoid(x), 0.0))

    row = pl.BlockSpec((tr, LANES), lambda i: (i, 0))
    vec = pl.BlockSpec((1, LANES), lambda i: (0, 0))
    return pl.pallas_call(
        body, name=name, grid=(t // tr,), in_specs=[row, vec, vec], out_specs=row,
        out_shape=jax.ShapeDtypeStruct((t, LANES), F32), compiler_params=_cparams("parallel"),
    )(ab, alog_row, dt_row)


def _gates_bwd(ab, alog_row, dt_row, dgates, nh, name, tr=1024):
    t = ab.shape[0]
    tr = min(tr, t)
    npart = dgates.shape[0]

    def body(x_ref, al_ref, dt_ref, dg_ref, dx_ref, s_ref):
        x = x_ref[...]
        lane = lax.broadcasted_iota(jnp.int32, x.shape, 1)
        dgt = dg_ref[0]
        for p in range(1, npart):
            dgt = dgt + dg_ref[p]
        ea = jnp.exp(al_ref[...])
        xa = x + dt_ref[...]
        da = jnp.where(lane < nh, dgt * (-ea) * _sigmoid(xa), 0.0)
        beta = _sigmoid(x)
        db = jnp.where((lane >= nh) & (lane < 2 * nh), dgt * beta * (1.0 - beta), 0.0)
        dx_ref[...] = (da + db).astype(BF16)
        dal = jnp.sum(jnp.where(lane < nh, dgt * (-ea) * _softplus(xa), 0.0), axis=0, keepdims=True)
        ddt = jnp.sum(da, axis=0, keepdims=True)
        r8 = lax.broadcasted_iota(jnp.int32, (8, LANES), 0)
        part = jnp.where(r8 == 0, dal, jnp.where(r8 == 1, ddt, 0.0))

        @pl.when(pl.program_id(0) == 0)
        def _():
            s_ref[...] = part

        @pl.when(pl.program_id(0) > 0)
        def _():
            s_ref[...] += part

    row = pl.BlockSpec((tr, LANES), lambda i: (i, 0))
    vec = pl.BlockSpec((1, LANES), lambda i: (0, 0))
    return pl.pallas_call(
        body, name=name, grid=(t // tr,),
        in_specs=[row, vec, vec, pl.BlockSpec((npart, tr, LANES), lambda i: (0, i, 0))],
        out_specs=[row, pl.BlockSpec((8, LANES), lambda i: (0, 0))],
        out_shape=[jax.ShapeDtypeStruct((t, LANES), BF16), jax.ShapeDtypeStruct((8, LANES), F32)],
        compiler_params=_cparams("arbitrary"),
    )(ab, alog_row, dt_row, dgates)


HALO = 8


def _delayed(ext, rows):
    return [ext[HALO:HALO + rows]] + [pltpu.roll(ext, s, 0)[HALO:HALO + rows] for s in range(1, CONV_K)]


def _conv_taps(delayed, w):
    acc = delayed[0] * w[CONV_K - 1:CONV_K]
    for s in range(1, CONV_K):
        acc = acc + delayed[s] * w[CONV_K - 1 - s:CONV_K - s]
    return acc


def _conv_fwd(proj, conv_w, col0, inner, mode, name, tt=CONV_ROWS, hb=CONV_HEADS):
    t = proj.shape[0]
    tt = min(tt, t)
    hb = min(hb, inner // HEAD_DIM)
    wc = hb * HEAD_DIM
    c0 = col0 // wc
    hpb = tt // HALO

    def body(x_ref, halo_ref, w_ref, o_ref):
        halo = jnp.where(pl.program_id(1) == 0, 0.0, halo_ref[...])
        s = _silu(_conv_taps(_delayed(jnp.concatenate([halo, x_ref[...]], axis=0), tt), w_ref[...]))
        if mode == "v":
            o_ref[...] = s
        else:
            mul = HEAD_DIM ** -0.5 if mode == "q" else 1.0
            o_ref[...] = jnp.concatenate(
                [sh * (lax.rsqrt(jnp.sum(sh * sh, axis=-1, keepdims=True) + EPS) * mul) for sh in _heads_of(s, hb)], axis=1)

    return pl.pallas_call(
        body, name=name, grid=(inner // wc, t // tt),
        in_specs=[pl.BlockSpec((tt, wc), lambda j, i: (i, j + c0)),
                  pl.BlockSpec((HALO, wc), lambda j, i: (jnp.maximum(i * hpb - 1, 0), j + c0)),
                  pl.BlockSpec((CONV_K, wc), lambda j, i: (0, j + c0))],
        out_specs=pl.BlockSpec((tt, wc), lambda j, i: (i, j)),
        out_shape=jax.ShapeDtypeStruct((t, inner), F32),
        compiler_params=_cparams("parallel", "parallel"),
    )(proj, proj, conv_w)


def _conv_bwd(dy, proj, conv_w, col0, inner, mode, name, tt=CONV_ROWS, hb=CONV_HEADS, dest=None):
    t = proj.shape[0]
    tt = min(tt, t)
    nt = t // tt
    hb = min(hb, inner // HEAD_DIM)
    wc = hb * HEAD_DIM
    c0 = col0 // wc
    hpb = tt // HALO
    rows = tt + HALO

    def body(dy_ref, dyn_ref, x_ref, xp_ref, xn_ref, w_ref, dx_ref, dw_ref):
        i = pl.program_id(1)
        w = w_ref[...]
        xprev = jnp.where(i == 0, 0.0, xp_ref[...])
        delayed = _delayed(jnp.concatenate([xprev, x_ref[...], xn_ref[...]], axis=0), rows)
        c = _conv_taps(delayed, w)
        dyv = jnp.concatenate([dy_ref[...], jnp.where(i == nt - 1, 0.0, dyn_ref[...])], axis=0)
        sg = _sigmoid(c)
        s = c * sg
        if mode == "v":
            ds = dyv
        else:
            mul = HEAD_DIM ** -0.5 if mode == "q" else 1.0
            parts = []
            for dyh, sh in zip(_heads_of(dyv, hb), _heads_of(s, hb)):
                r = lax.rsqrt(jnp.sum(sh * sh, axis=-1, keepdims=True) + EPS)
                parts.append(mul * (r * dyh - sh * (r * r * r) * jnp.sum(dyh * sh, axis=-1, keepdims=True)))
            ds = jnp.concatenate(parts, axis=1)
        dc = ds * (sg * (1.0 + c * (1.0 - sg)))
        dx = dc[:tt] * w[CONV_K - 1:CONV_K]
        for sft in range(1, CONV_K):
            dx = dx + pltpu.roll(dc, rows - sft, 0)[:tt] * w[CONV_K - 1 - sft:CONV_K - sft]
        dx_ref[...] = dx.astype(BF16)
        r8 = lax.broadcasted_iota(jnp.int32, (8, wc), 0)
        part = jnp.zeros((8, wc), F32)
        for sft in range(CONV_K):
            part = part + jnp.where(r8 == CONV_K - 1 - sft,
                                    jnp.sum(dc[:tt] * delayed[sft][:tt], axis=0, keepdims=True), 0.0)

        @pl.when(i == 0)
        def _():
            dw_ref[...] = part

        @pl.when(i > 0)
        def _():
            dw_ref[...] += part

    cur = lambda off: pl.BlockSpec((tt, wc), lambda j, i: (i, j + off))
    nxt = lambda off: pl.BlockSpec((HALO, wc), lambda j, i: (jnp.minimum((i + 1) * hpb, t // HALO - 1), j + off))
    width, out0, more, more_specs, aliases = _window(dest, inner, 6, 0)
    return pl.pallas_call(
        _skip_ref(body, 6, len(more)), name=name, grid=(inner // wc, nt),
        in_specs=[cur(0), nxt(0), cur(c0),
                  pl.BlockSpec((HALO, wc), lambda j, i: (jnp.maximum(i * hpb - 1, 0), j + c0)), nxt(c0),
                  pl.BlockSpec((CONV_K, wc), lambda j, i: (0, j + c0))] + more_specs,
        out_specs=[pl.BlockSpec((tt, wc), lambda j, i: (i, j + out0 // wc)),
                   pl.BlockSpec((8, wc), lambda j, i: (0, j))],
        out_shape=[jax.ShapeDtypeStruct((t, width), BF16), jax.ShapeDtypeStruct((8, inner), F32)],
        input_output_aliases=aliases, compiler_params=_cparams("parallel", "arbitrary"),
    )(dy, dy, proj, proj, proj, conv_w, *more)


GDN_HB = 4
GDN_NB = 8
SCAN_HB = 16
SCAN_NB = 4


def _iota2(n, m):
    return lax.broadcasted_iota(jnp.int32, (n, m), 0), lax.broadcasted_iota(jnp.int32, (n, m), 1)


def _head_select(first_head, hb, lane0):
    r, lane = _iota2(8, LANES)
    return jnp.where((r < hb) & (lane == lane0 + first_head + r), 1.0, 0.0).astype(F32)


def _chunk_gates(gt, selg, selb):
    i, j = _iota2(CHUNK, CHUNK)
    gc_all = _dot_exact(jnp.where(j <= i, 1.0, 0.0), gt, NN, True)
    return (_dot_exact(gc_all, selg, NT, False), _dot_exact(selg, gc_all, NT, True),
            _dot_exact(gt, selb, NT, False))


def _decay_terms(gcol, grow):
    i, j = _iota2(CHUNK, CHUNK)
    glast = gcol[:, CHUNK - 1:CHUNK, :]
    decay = jnp.exp(jnp.where(j <= i, gcol - grow, NEG_BIG))
    return jnp.exp(gcol), jnp.exp(glast - gcol), jnp.exp(glast), decay


def _unit_lower_inverse(a):
    i, j = _iota2(CHUNK, CHUNK)
    same16 = (i // 16) == (j // 16)
    same32 = (i // 32) == (j // 32)
    m = jnp.where(same16, -a, 0.0)
    x = jnp.where(i == j, 1.0, 0.0) + m
    for _ in range(3):
        m = _dot3(m, m, BNN)
        x = x + _dot3(x, m, BNN)
    for off in (jnp.where(same32 & jnp.logical_not(same16), a, 0.0), jnp.where(same32, 0.0, a)):
        x = x - _dot3(_dot3(x, off, BNN), x, BNN)
    return x


def _unit_inputs(refs, g_ref, selg, selb, hb, nb):
    units = [(c, h) for c in range(nb) for h in range(hb)]
    rs = lambda c: slice(c * CHUNK, (c + 1) * CHUNK)
    cs = lambda h: slice(h * HEAD_DIM, (h + 1) * HEAD_DIM)
    gates = [_chunk_gates(g_ref[rs(c), :], selg, selb) for c in range(nb)]
    stacked = [jnp.stack([r[rs(c), cs(h)] for c, h in units]) for r in refs]
    gcol = jnp.stack([gates[c][0][:, h:h + 1] for c, h in units])
    grow = jnp.stack([gates[c][1][h:h + 1, :] for c, h in units])
    bcol = jnp.stack([gates[c][2][:, h:h + 1] for c, h in units])
    return units, rs, cs, stacked, gcol, grow, bcol


def _gdn_specs(nh, inner, t, heads=GDN_HB, chunks=GDN_NB):
    hb, nb = min(heads, nh), chunks
    rows = nb * CHUNK
    wide = pl.BlockSpec((rows, hb * HEAD_DIM), lambda g, n: (n, g))
    sq = pl.BlockSpec((hb, rows, CHUNK), lambda g, n: (g, n, 0))
    gts = pl.BlockSpec((rows, LANES), lambda g, n: (n, 0))
    glb = pl.BlockSpec((nb * 8, hb * HEAD_DIM), lambda g, n: (n, g))
    return hb, nb, rows, wide, sq, gts, glb


def _gdn_intra_fwd(q, k, v, gates, nh, name, comm=None):
    t, inner = q.shape
    hb, nb, rows, wide, sq, gts, glb = _gdn_specs(nh, inner, t)

    def body(q_ref, k_ref, v_ref, g_ref, qe_ref, kel_ref, wb_ref, w_ref, u_ref, qk_ref, tm_ref, gl_ref):
        first = pl.program_id(0) * hb
        selg, selb = _head_select(first, hb, 0), _head_select(first, hb, nh)
        i, j = _iota2(CHUNK, CHUNK)
        units, rs, cs, (qv, kv, vv), gcol, grow, bcol = _unit_inputs(
            (q_ref, k_ref, v_ref), g_ref, selg, selb, hb, nb)
        e, el, gl, decay = _decay_terms(gcol, grow)
        kb = kv * bcol
        qbf, kbf = _bf(qv), _bf(kv)
        a = jnp.where(j < i, _dot(_bf(kb), kbf, BNT) * decay, 0.0)
        tm = _unit_lower_inverse(a)
        uw = _dot3(tm, jnp.concatenate([vv * bcol, kb * e], axis=2), BNN)
        qk = _bf(_dot(qbf, kbf, BNT) * decay)
        qe, kel = _bf(qv * e), _bf(kv * el)
        for n, (c, h) in enumerate(units):
            w = uw[n, :, HEAD_DIM:]
            qe_ref[rs(c), cs(h)] = qe[n]
            kel_ref[rs(c), cs(h)] = kel[n]
            wb_ref[rs(c), cs(h)] = _bf(w)
            w_ref[rs(c), cs(h)] = w
            u_ref[rs(c), cs(h)] = uw[n, :, :HEAD_DIM]
            qk_ref[h, rs(c), :] = qk[n]
            tm_ref[h, rs(c), :] = tm[n]
            gl_ref[c * 8:(c + 1) * 8, cs(h)] = jnp.broadcast_to(gl[n], (8, HEAD_DIM))

    big = lambda dt: jax.ShapeDtypeStruct((t, inner), dt)
    return _grid_call(
        body, name=name, grid=(nh // hb, t // rows),
        in_specs=[wide, wide, wide, gts],
        out_specs=[wide] * 5 + [sq, sq, glb],
        out_shape=[big(BF16), big(BF16), big(BF16), big(F32), big(F32),
                   jax.ShapeDtypeStruct((nh, t, CHUNK), BF16), jax.ShapeDtypeStruct((nh, t, CHUNK), F32),
                   jax.ShapeDtypeStruct((t // CHUNK * 8, inner), F32)],
        args=(q, k, v, gates), semantics=("parallel", "parallel"), comm=comm)


def _gdn_scan_fwd(qe, kel, wb, u, qk, glb, nh, name):
    t, inner = u.shape
    hb, nb, rows, wide, sq, _, glb_spec = _gdn_specs(nh, inner, t, SCAN_HB, SCAN_NB)

    def body(qe_ref, kel_ref, wb_ref, u_ref, qk_ref, gl_ref, o_ref, vn_ref, sall_ref, s_ref):
        @pl.when(pl.program_id(1) == 0)
        def _():
            s_ref[...] = jnp.zeros(s_ref.shape, F32)

        cs = lambda h: slice(h * HEAD_DIM, (h + 1) * HEAD_DIM)
        for c in range(nb):
            rs = slice(c * CHUNK, (c + 1) * CHUNK)
            heads = lambda ref: jnp.stack([ref[rs, cs(h)] for h in range(hb)])
            s = s_ref[...]
            if c == 0:
                sall_ref[...] = s
            sb = _bf(s)
            vn = heads(u_ref) - _dot(heads(wb_ref), sb, BNN)
            vnb = _bf(vn)
            o = _dot(heads(qe_ref), sb, BNN) + _dot(qk_ref[:, rs, :], vnb, BNN)
            gl = jnp.stack([gl_ref[c * 8:c * 8 + 1, cs(h)] for h in range(hb)])
            s_ref[...] = s * gl + _dot(heads(kel_ref), vnb, BTN)
            for h in range(hb):
                vn_ref[rs, cs(h)] = vnb[h]
                o_ref[rs, cs(h)] = o[h]

    return pl.pallas_call(
        body, name=name, grid=(nh // hb, t // rows),
        in_specs=[wide, wide, wide, wide, sq, glb_spec],
        out_specs=[wide, wide, pl.BlockSpec((None, hb, HEAD_DIM, HEAD_DIM), lambda g, n: (n, g, 0, 0))],
        out_shape=[jax.ShapeDtypeStruct((t, inner), F32), jax.ShapeDtypeStruct((t, inner), BF16),
                   jax.ShapeDtypeStruct((t // rows, nh, HEAD_DIM, HEAD_DIM), F32)],
        scratch_shapes=[pltpu.VMEM((hb, HEAD_DIM, HEAD_DIM), F32)],
        compiler_params=_cparams("parallel", "arbitrary"),
    )(qe, kel, wb, u, qk, glb)


def _gdn_scan_bwd(do, qe, kel, wb, vn, qk, glb, sall, nh, name, comm=None):
    t, inner = do.shape
    hb, nb, rows, _, _, _, _ = _gdn_specs(nh, inner, t, SCAN_HB, SCAN_NB)
    last = t // rows - 1
    wide = pl.BlockSpec((rows, hb * HEAD_DIM), lambda g, n: (last - n, g))
    sq = pl.BlockSpec((hb, rows, CHUNK), lambda g, n: (g, last - n, 0))
    glb_spec = pl.BlockSpec((nb * 8, hb * HEAD_DIM), lambda g, n: (last - n, g))

    def body(do_ref, qe_ref, kel_ref, wb_ref, vn_ref, qk_ref, gl_ref, sall_ref,
             dvn_ref, dw_ref, dqe_ref, dkel_ref, dqk_ref, dgl_ref, ds_ref):
        @pl.when(pl.program_id(1) == 0)
        def _():
            ds_ref[...] = jnp.zeros(ds_ref.shape, F32)

        cs = lambda h: slice(h * HEAD_DIM, (h + 1) * HEAD_DIM)
        chunk = lambda ref, c: jnp.stack([ref[c * CHUNK:(c + 1) * CHUNK, cs(h)] for h in range(hb)])
        decay = lambda c: jnp.stack([gl_ref[c * 8:c * 8 + 1, cs(h)] for h in range(hb)])
        states = [sall_ref[...]]
        for c in range(nb - 1):
            states.append(states[c] * decay(c) + _dot(chunk(kel_ref, c), _bf(chunk(vn_ref, c)), BTN))
        for c in reversed(range(nb)):
            rs = slice(c * CHUNK, (c + 1) * CHUNK)
            heads = lambda ref: jnp.stack([ref[rs, cs(h)] for h in range(hb)])
            ds, s = ds_ref[...], states[c]
            dsb, sb = _bf(ds), _bf(s)
            dob, vnb = _bf(heads(do_ref)), _bf(heads(vn_ref))
            dvn = _dot(qk_ref[:, rs, :], dob, BTN) + _dot(heads(kel_ref), dsb, BNN)
            dvnb = _bf(dvn)
            dw = -_dot(dvnb, sb, BNT)
            dqe = _dot(dob, sb, BNT)
            dkel = _dot(vnb, dsb, BNT)
            dqk_ref[:, rs, :] = _dot(dob, vnb, BNT)
            dgl = jnp.sum(jnp.sum(ds * s, axis=2, keepdims=True), axis=1, keepdims=True)
            ds_ref[...] = ds * decay(c) + _dot(heads(qe_ref), dob, BTN) - _dot(heads(wb_ref), dvnb, BTN)
            for h in range(hb):
                dvn_ref[rs, cs(h)] = dvn[h]
                dw_ref[rs, cs(h)] = dw[h]
                dqe_ref[rs, cs(h)] = dqe[h]
                dkel_ref[rs, cs(h)] = dkel[h]
                dgl_ref[c * 8:(c + 1) * 8, cs(h)] = jnp.broadcast_to(dgl[h], (8, HEAD_DIM))

    big = jax.ShapeDtypeStruct((t, inner), F32)
    return _grid_call(
        body, name=name, grid=(nh // hb, t // rows),
        in_specs=[wide, wide, wide, wide, wide, sq, glb_spec,
                  pl.BlockSpec((None, hb, HEAD_DIM, HEAD_DIM), lambda g, n: (last - n, g, 0, 0))],
        out_specs=[wide] * 4 + [sq, glb_spec],
        out_shape=[big] * 4 + [jax.ShapeDtypeStruct((nh, t, CHUNK), F32),
                               jax.ShapeDtypeStruct((t // CHUNK * 8, inner), F32)],
        scratch_shapes=[pltpu.VMEM((hb, HEAD_DIM, HEAD_DIM), F32)],
        args=(do, qe, kel, wb, vn, qk, glb, sall), semantics=("parallel", "arbitrary"), comm=comm)


def _gdn_intra_bwd(q, k, v, gates, tm, w, u, dvn, dw, dqe, dkel, dqk, dglb, nh, name, comm=None):
    t, inner = q.shape
    hb, nb, rows, wide, sq, gts, glb = _gdn_specs(nh, inner, t)

    def body(q_ref, k_ref, v_ref, g_ref, tm_ref, w_ref, u_ref, dvn_ref, dw_ref, dqe_ref, dkel_ref, dqk_ref,
             dgl_ref, dq_ref, dk_ref, dv_ref, dg_ref):
        first = pl.program_id(0) * hb
        selg, selb = _head_select(first, hb, 0), _head_select(first, hb, nh)
        i, j = _iota2(CHUNK, CHUNK)
        lane8 = lax.broadcasted_iota(jnp.int32, (CHUNK, 8), 1)
        row = lax.broadcasted_iota(jnp.int32, (CHUNK, 1), 0)
        lower = jnp.where(j <= i, 1.0, 0.0).astype(F32)
        rsum = lambda x: jnp.sum(x, axis=-1, keepdims=True)
        units, rs, cs, (qv, kv, vv, wv, uv, dvn, dw, dqe, dkel), gcol, grow, bcol = _unit_inputs(
            (q_ref, k_ref, v_ref, w_ref, u_ref, dvn_ref, dw_ref, dqe_ref, dkel_ref), g_ref, selg, selb, hb, nb)
        nu = len(units)
        tmv = jnp.stack([tm_ref[h, rs(c), :] for c, h in units])
        dqk = jnp.where(j <= i, jnp.stack([dqk_ref[h, rs(c), :] for c, h in units]), 0.0)
        dgl = jnp.stack([dgl_ref[c * 8:c * 8 + 1, h * HEAD_DIM:h * HEAD_DIM + 1] for c, h in units])
        e, el, gl, decay = _decay_terms(gcol, grow)
        kb = kv * bcol
        qb, kbf, kbb = _bf(qv), _bf(kv), _bf(kb)
        dqkr = _bf(dqk * decay)
        dq = dqe * e + _dot(dqkr, kbf, BNN)
        dk = dkel * el + _dot(dqkr, qb, BTN)
        de = rsum(dqe * qv)
        del_ = rsum(dkel * kv)
        mq = dqk * _dot(qb, kbf, BNT) * decay
        dsol = _dot3(tmv, jnp.concatenate([dvn, dw], axis=2), BTN)
        dvb, dkbe = dsol[:, :, :HEAD_DIM], dsol[:, :, HEAD_DIM:]
        da = -jnp.where(j < i, _dot3(dsol, jnp.concatenate([uv, wv], axis=2), BNT), 0.0)
        dkk = _bf(da * decay)
        ma = da * _dot(kbb, kbf, BNT) * decay
        dkb = dkbe * e + _dot(dkk, kbf, BNN)
        de = de + rsum(dkbe * kb)
        dk = dk + _dot(dkk, kbb, BTN) + dkb * bcol
        dv = dvb * bcol
        dbeta = rsum(dkb * kv) + rsum(dvb * vv)
        m = mq + ma
        ones = jnp.ones((nu, CHUNK, LANES), F32)
        dgc = rsum(m) - _dot_exact(m, ones, BTN, False)[:, :, 0:1] + de * e - del_ * el
        tail = jnp.sum(del_ * el, axis=1, keepdims=True) + dgl * gl
        dgc = dgc + jnp.where(row == CHUNK - 1, tail, 0.0)
        for n, (c, h) in enumerate(units):
            dq_ref[rs(c), cs(h)] = dq[n]
            dk_ref[rs(c), cs(h)] = dk[n]
            dv_ref[rs(c), cs(h)] = dv[n]
        for c in range(nb):
            dgc_cols = jnp.zeros((CHUNK, 8), F32)
            dbeta_cols = jnp.zeros((CHUNK, 8), F32)
            for h in range(hb):
                dgc_cols = jnp.where(lane8 == h, dgc[c * hb + h], dgc_cols)
                dbeta_cols = jnp.where(lane8 == h, dbeta[c * hb + h], dbeta_cols)
            dg_cols = _dot_exact(lower, dgc_cols, TN, True)
            dg_ref[rs(c), :] = _dot_exact(dg_cols, selg, NN, False) + _dot_exact(dbeta_cols, selb, NN, False)

    big = jax.ShapeDtypeStruct((t, inner), F32)
    return _grid_call(
        body, name=name, grid=(nh // hb, t // rows),
        in_specs=[wide, wide, wide, gts, sq, wide, wide, wide, wide, wide, wide, sq, glb],
        out_specs=[wide, wide, wide, pl.BlockSpec((None, rows, LANES), lambda g, n: (g, n, 0))],
        out_shape=[big, big, big, jax.ShapeDtypeStruct((nh // hb, t, LANES), F32)],
        args=(q, k, v, gates, tm, w, u, dvn, dw, dqe, dkel, dqk, dglb), semantics=("parallel", "parallel"),
        comm=comm)


def _layer_a_fwd(x, hn, w_in_t, w_ab_t, conv_w, alog_row, dt_row, onw, nh, comm, w_out_of):
    t, d = x.shape
    inner = nh * HEAD_DIM
    proj = _mm(hn, w_in_t, "nt", t, 4 * inner, d, out_dtype=F32, name="a_proj")
    ab = _mm(hn, w_ab_t, "nt", t, LANES, d, out_dtype=F32, name="a_proj_ab")
    gates = _gates_fwd(ab, alog_row, dt_row, nh, "a_gates")
    q = _conv_fwd(proj, conv_w, 0, inner, "q", "a_conv_q")
    k = _conv_fwd(proj, conv_w, inner, inner, "k", "a_conv_k")
    v = _conv_fwd(proj, conv_w, 2 * inner, inner, "v", "a_conv_v")
    qe, kel, wb, w, u, qk, tm, glb, *carried = _gdn_intra_fwd(q, k, v, gates, nh, "a_intra", comm)
    o, vn, sall = _gdn_scan_fwd(qe, kel, wb, u, qk, glb, nh, "a_scan")
    g = _gate_fwd(o, proj, 3 * inner, inner, "a_gate", norm_w=onw)
    w_out = w_out_of(carried)
    h1 = _mm(g, w_out, "nn", t, d, inner, out_dtype=F32, name="a_out", res=x)
    return h1, (hn, proj, ab, gates, q, k, v, qe, kel, wb, w, u, qk, tm, glb, o, vn, sall, g), w_out, carried


def _layer_a_bwd(dh1, dh1b, x, nw, w_in_t, w_ab_t, conv_w, alog_row, dt_row, onw, w_out, nh, saved, comms_of,
                 own_comm):
    hn, proj, ab, gates, q, k, v, qe, kel, wb, w, u, qk, tm, glb, o, vn, sall, g = saved
    t, d = x.shape
    inner = w_out.shape[0]
    dg = _mm(dh1b, w_out, "nt", t, inner, d, out_dtype=F32, name="a_dgate")
    dw_out = _mm(g, dh1b, "tn", inner, d, t, out_dtype=BF16, name="a_dwout")
    comm_scan, comm_intra = comms_of(dw_out)
    do, dproj, donw = _gate_bwd(dg, o, proj, 3 * inner, inner, "a_gate_bwd", BF16, norm_w=onw,
                                dest=(None, 3 * inner, 4 * inner))
    dvn, dw, dqe, dkel, dqk, dglb, *carried_scan = _gdn_scan_bwd(do, qe, kel, wb, vn, qk, glb, sall, nh,
                                                                 "a_scan_bwd", comm_scan)
    dq, dk, dv, dgates, *carried = _gdn_intra_bwd(q, k, v, gates, tm, w, u, dvn, dw, dqe, dkel, dqk, dglb, nh,
                                                  "a_intra_bwd", comm_intra)
    carried = carried_scan + carried
    dproj, dcq = _conv_bwd(dq, proj, conv_w, 0, inner, "q", "a_conv_q_bwd", dest=(dproj, 0, 4 * inner))
    dproj, dck = _conv_bwd(dk, proj, conv_w, inner, inner, "k", "a_conv_k_bwd", dest=(dproj, inner, 4 * inner))
    dproj, dcv = _conv_bwd(dv, proj, conv_w, 2 * inner, inner, "v", "a_conv_v_bwd",
                           dest=(dproj, 2 * inner, 4 * inner))
    dab, dsmall = _gates_bwd(ab, alog_row, dt_row, dgates, nh, "a_gates_bwd")
    dw_in_t = _mm(dproj, hn, "tn", 4 * inner, d, t, out_dtype=BF16, name="a_dwin", **LONG_K_TILES)
    dw_ab_t = _mm(dab, hn, "tn", LANES, d, t, out_dtype=BF16, name="a_dwin_ab")
    dconv = jnp.concatenate([dcq[:CONV_K], dck[:CONV_K], dcv[:CONV_K]], axis=1)
    dhn = _mm(dab, w_ab_t, "nn", t, d, LANES, out_dtype=F32, name="a_dhn_ab")
    own = own_comm(dw_in_t, dw_ab_t, dconv)
    dhn = _mm(dproj, w_in_t, "nn", t, d, 4 * inner, out_dtype=F32, name="a_dhn", res=dhn, comm=own,
              **LONG_K_TILES)
    dhn, carried_own = dhn if own is not None else (dhn, [])
    dx, _, dnw = _rms_bwd(x, nw, dhn, dh1, "a_rms_bwd")
    return dx, dnw, dsmall, donw, carried, carried_own


def _rows_of(a, rows):
    flat = a.reshape(-1)
    return jnp.pad(flat, (0, rows * LANES - flat.shape[0])).reshape(rows, LANES)


def _to_slabs(g, axis):
    shape = g.shape[:axis] + (N_DEV, g.shape[axis] // N_DEV) + g.shape[axis + 1:]
    return jnp.moveaxis(g.reshape(shape), axis, 0)


def _from_slabs(s, axis):
    m = jnp.moveaxis(s, 0, axis)
    return m.reshape(m.shape[:axis] + (m.shape[axis] * m.shape[axis + 1],) + m.shape[axis + 2:])


def kernel(x, norm_w, a_w_in, a_conv_w, a_a_log, a_dt_bias, a_out_norm_w, a_w_out, b_w_in, b_q_norm_w, b_k_norm_w, b_rel_bias, b_w_out, loss_target, m_norm_w, m_a_w_in, m_a_conv_w, m_a_a_log, m_a_dt_bias, m_a_out_norm_w, m_a_w_out, m_b_w_in, m_b_q_norm_w, m_b_k_norm_w, m_b_rel_bias, m_b_w_out, v_norm_w, v_a_w_in, v_a_conv_w, v_a_a_log, v_a_dt_bias, v_a_out_norm_w, v_a_w_out, v_b_w_in, v_b_q_norm_w, v_b_k_norm_w, v_b_rel_bias, v_b_w_out):
    xs, target = x[0], loss_target[0]
    nh = a_a_log.shape[-1]
    inner = N_DEV * a_w_out.shape[1]

    d = xs.shape[1]
    nw0, nw1 = norm_w[0:1], norm_w[1:2]
    hn0, (ga_in, g_conv) = _rms_fwd(
        xs, nw0, "a_rms", comm=_RoutedGather([a_w_in[0].T.astype(BF16), a_conv_w[0]]))
    wa_in_t = ga_in.reshape(-1, d)
    wa_ab_t = jnp.pad(wa_in_t[4 * inner:], ((0, LANES - 2 * nh), (0, 0)))
    conv_w = _from_slabs(g_conv, 1)
    alog_row = jnp.pad(a_a_log, ((0, 0), (0, LANES - nh)))
    dt_row = jnp.pad(a_dt_bias, ((0, 0), (0, LANES - nh)))

    h1, saved_a, wa_out, (gb_in, gb_out, _) = _layer_a_fwd(
        xs, hn0, wa_in_t, wa_ab_t, conv_w, alog_row, dt_row, a_out_norm_w, nh,
        _Comm("gather", [b_w_in[0].T.astype(BF16), b_w_out[0].astype(BF16), a_w_out[0].astype(BF16)]),
        lambda gathered: _from_slabs(gathered[2], 0))
    wb_in_t = gb_in.reshape(-1, d)
    wb_out = _from_slabs(gb_out, 0)
    bias = _bias_tiles(_pad_rel_bias(b_rel_bias[0]), "b_bias_tiles")
    (dh2, dh2b, loss_row), saved_b = _layer_b_fwd(h1, nw1, wb_in_t, b_q_norm_w, b_k_norm_w, bias, wb_out, target)

    dh1, dh1b, dnw1, dwb_in_t, dqw, dkw, drb, dwb_out = _layer_b_bwd(
        dh2, dh2b, h1, nw1, wb_in_t, b_q_norm_w, b_k_norm_w, bias, wb_out, saved_b)

    def exchange_early(dwa_out):
        return (_Comm("exchange", [_to_slabs(dwb_out, 0).astype(BF16), _to_slabs(dwa_out, 0).astype(BF16)]),
                _Comm("exchange", [dwb_in_t.reshape(N_DEV, -1, d).astype(BF16)]))

    def exchange_last(dwa_in_t, dwa_ab_t, dconv):
        full = jnp.concatenate([dwa_in_t, dwa_ab_t[:2 * nh]], axis=0)
        return _Comm("exchange", [full.reshape(N_DEV, -1, d).astype(BF16), _to_slabs(dconv, 1)])

    dx, dnw0, dsmall, donw, (pb_out, pa_out, pb_in), (pa_in, p_conv) = _layer_a_bwd(
        dh1, dh1b, xs, nw0, wa_in_t, wa_ab_t, conv_w, alog_row, dt_row, a_out_norm_w, wa_out, nh, saved_a,
        exchange_early, exchange_last)
    big = {}
    for name, p, w, m, v in (("a_w_in", pa_in, a_w_in, m_a_w_in, v_a_w_in),
                             ("a_w_out", pa_out, a_w_out, m_a_w_out, v_a_w_out),
                             ("b_w_in", pb_in, b_w_in, m_b_w_in, v_b_w_in),
                             ("b_w_out", pb_out, b_w_out, m_b_w_out, v_b_w_out),
                             ("a_conv_w", p_conv, a_conv_w, m_a_conv_w, v_a_conv_w)):
        big[name] = [o[None] for o in _adamw(p, w[0], m[0], v[0], "adamw_" + name,
                                             transposed=name in ("a_w_in", "b_w_in"))]

    small = (("norm_w", norm_w, m_norm_w, v_norm_w, jnp.concatenate([dnw0, dnw1], axis=0)),
             ("a_a_log", a_a_log, m_a_a_log, v_a_a_log, dsmall[0:1, :nh]),
             ("a_dt_bias", a_dt_bias, m_a_dt_bias, v_a_dt_bias, dsmall[1:2, :nh]),
             ("a_out_norm_w", a_out_norm_w, m_a_out_norm_w, v_a_out_norm_w, donw),
             ("b_q_norm_w", b_q_norm_w, m_b_q_norm_w, v_b_q_norm_w, dqw),
             ("b_k_norm_w", b_k_norm_w, m_b_k_norm_w, v_b_k_norm_w, dkw),
             ("b_rel_bias", b_rel_bias, m_b_rel_bias, v_b_rel_bias, drb))
    rows = [8 * (-(-w.size // (8 * LANES))) for _, w, _, _, _ in small]
    pack = lambda arrs: jnp.concatenate([_rows_of(a, r) for a, r in zip(arrs, rows)] + [jnp.zeros((8, LANES), F32)], axis=0)
    g_pack = jnp.concatenate([_rows_of(g, r) for (_, _, _, _, g), r in zip(small, rows)]
                             + [jnp.broadcast_to(loss_row, (8, LANES))], axis=0)
    (g_all,) = _comm_call(_Comm("gather", [g_pack]), "gather_small_grads")
    outs_small = _adamw(g_all, pack([s[1] for s in small]), pack([s[2] for s in small]),
                        pack([s[3] for s in small]), "adamw_small")
    start = 0
    for (name, w, _, _, _), r in zip(small, rows):
        big[name] = [o[start:start + r].reshape(-1)[:w.size].reshape(w.shape) for o in outs_small]
        start += r
    loss = outs_small[0][start, 0]

    order = ("norm_w", "a_w_in", "a_conv_w", "a_a_log", "a_dt_bias", "a_out_norm_w", "a_w_out", "b_w_in",
             "b_q_norm_w", "b_k_norm_w", "b_rel_bias", "b_w_out")
    return (loss, dx[None]) + tuple(big[n][i] for i in range(4) for n in order)
```

```python
import functools

import jax
import jax.numpy as jnp
from jax import lax
from jax.experimental import pallas as pl
from jax.experimental.pallas import tpu as pltpu

F32 = jnp.float32
BF16 = jnp.bfloat16
MESH_IDS = pl.DeviceIdType.MESH
N_DEV = 8
CHUNK = 64
HEAD_DIM = 128
EPS = 1e-6
CONV_K = 4
LEFT_CHUNKS = 8
REL_CLIP = 256
Q_TILE = LEFT_CHUNKS * CHUNK
ADAM_LR = 0.001
ADAM_B1 = 0.9
ADAM_B2 = 0.999
ADAM_EPS = 1e-08
ADAM_WD = 0.01
ADAM_STEP = 10
NEG_BIG = -1e30
VMEM_LIMIT_BYTES = 56 * 1024 * 1024
HIGHEST = lax.Precision.HIGHEST
ANY = pl.BlockSpec(memory_space=pl.ANY)


def _cparams(*sem):
    return pltpu.CompilerParams(dimension_semantics=tuple(sem), vmem_limit_bytes=VMEM_LIMIT_BYTES)


NN, NT, TN = (((1,), (0,)), ((), ())), (((1,), (1,)), ((), ())), (((0,), (0,)), ((), ()))
BNN, BNT, BTN = (((2,), (1,)), ((0,), (0,))), (((2,), (2,)), ((0,), (0,))), (((1,), (1,)), ((0,), (0,)))


def _dot(a, b, dims, precision=None):
    return lax.dot_general(a, b, dims, preferred_element_type=F32, precision=precision)


def _nn(a, b, precision=None):
    return _dot(a, b, NN, precision)


def _nt(a, b, precision=None):
    return _dot(a, b, NT, precision)


def _tn(a, b, precision=None):
    return _dot(a, b, TN, precision)


def _bf(x):
    return x.astype(BF16)


def _split(x, pieces=2):
    out = []
    for _ in range(pieces - 1):
        hi = x.astype(BF16)
        out.append(hi)
        x = x - hi.astype(F32)
    return out + [x.astype(BF16)]


def _dot3(a, b, dims):
    (ah, al), (bh, bl) = _split(a), _split(b)
    return _dot(ah, bh, dims) + (_dot(ah, bl, dims) + _dot(al, bh, dims))


def _dot_exact(a, b, dims, split_b):
    if split_b:
        a = a.astype(BF16)
        parts = [_dot(a, p, dims) for p in _split(b, 3)]
    else:
        b = b.astype(BF16)
        parts = [_dot(p, b, dims) for p in _split(a, 3)]
    return parts[0] + (parts[1] + parts[2])


def _sigmoid(x):
    return 0.5 * jnp.tanh(0.5 * x) + 0.5


def _silu(x):
    return x * _sigmoid(x)


def _dsilu(x):
    s = _sigmoid(x)
    return s * (1.0 + x * (1.0 - s))


def _my_pos():
    return lax.axis_index("x"), lax.axis_index("y"), lax.axis_index("c")


def _peers(x, y, c):
    def flip(v, f):
        return 1 - v if f else v

    return [(flip(x, kx), flip(y, ky), flip(c, kc)) for kx in (0, 1) for ky in (0, 1) for kc in (0, 1)][1:]


def _lin(p):
    return 4 * p[0] + 2 * p[1] + p[2]


class _Comm:
    def __init__(self, kind, arrays):
        self.kind, self.arrays, self.n = kind, list(arrays), len(arrays)

    def out_shape(self):
        lead = (N_DEV,) if self.kind == "gather" else ()
        return [jax.ShapeDtypeStruct(lead + a.shape, a.dtype) for a in self.arrays]

    def scratch(self):
        return [pltpu.SemaphoreType.DMA((7 * self.n,)), pltpu.SemaphoreType.DMA((7 * self.n,)),
                pltpu.SemaphoreType.DMA((self.n,))]

    def _copies(self, ins, outs, sems, arrivals):
        send_sems, recv_sems, local_sems = sems
        x, y, c = _my_pos()
        me = _lin((x, y, c))
        gather = self.kind == "gather"
        mine = [ins[t] if gather else ins[t].at[me] for t in range(self.n)]
        remote = []
        for k, peer in enumerate(_peers(x, y, c)):
            for t in range(self.n):
                if arrivals:
                    src, dst = mine[t], outs[t].at[_lin(peer)]
                else:
                    src, dst = (ins[t] if gather else ins[t].at[_lin(peer)]), outs[t].at[me]
                remote.append(pltpu.make_async_remote_copy(
                    src_ref=src, dst_ref=dst, send_sem=send_sems.at[k * self.n + t],
                    recv_sem=recv_sems.at[k * self.n + t], device_id=peer, device_id_type=MESH_IDS))
        if arrivals:
            return remote
        return [pltpu.make_async_copy(mine[t], outs[t].at[me], local_sems.at[t]) for t in range(self.n)], remote

    def start(self, ins, outs, sems):
        local, sends = self._copies(ins, outs, sems, False)
        for cp in local + sends:
            cp.start()

    def finish(self, ins, outs, sems):
        for cp in self._copies(ins, outs, sems, True):
            cp.wait_recv()
        local, sends = self._copies(ins, outs, sems, False)
        for cp in sends:
            cp.wait_send()
        for cp in local:
            cp.wait()


def _xor(a, b):
    return a + b - 2 * a * b


class _RoutedGather(_Comm):
    def __init__(self, arrays):
        super().__init__("gather", arrays)

    def _plan(self, outs, sems):
        send_sems, recv_sems, _ = sems
        x, y, c = _my_pos()
        sib, xn, yn, dg = (x, y, 1 - c), (1 - x, y, c), (x, 1 - y, c), (1 - x, 1 - y, c)
        via = (_xor(x, 1 - c), _xor(y, c), c)
        onto = (_xor(x, c), _xor(y, 1 - c), c)
        routes = [(None, sib, sib), (None, xn, xn), (None, yn, yn), (via, onto, dg),
                  (xn, sib, (1 - x, y, 1 - c)), (yn, sib, (x, 1 - y, 1 - c)), (dg, sib, (1 - x, 1 - y, 1 - c))]

        def copy(k, t, src, slot, target):
            return pltpu.make_async_remote_copy(
                src_ref=src, dst_ref=outs[t].at[slot], send_sem=send_sems.at[k * self.n + t],
                recv_sem=recv_sems.at[k * self.n + t], device_id=target, device_id_type=MESH_IDS)

        return (x, y, c), routes, copy

    def start(self, ins, outs, sems):
        me, routes, copy = self._plan(outs, sems)
        for t in range(self.n):
            pltpu.make_async_copy(ins[t], outs[t].at[_lin(me)], sems[2].at[t]).start()
            for k in range(3):
                copy(k, t, ins[t], _lin(me), routes[k][1]).start()

    def finish(self, ins, outs, sems):
        me, routes, copy = self._plan(outs, sems)

        def arrived(k):
            for t in range(self.n):
                copy(k, t, ins[t], _lin(routes[k][2]), me).wait_recv()

        def pass_on(k):
            for t in range(self.n):
                copy(k, t, outs[t].at[_lin(routes[k][0])], _lin(routes[k][0]), routes[k][1]).start()

        arrived(1)
        arrived(2)
        for k in (3, 4, 5):
            pass_on(k)
        arrived(3)
        pass_on(6)
        for k in (0, 4, 5, 6):
            arrived(k)
        for t in range(self.n):
            for k in range(7):
                src = ins[t] if k < 3 else outs[t].at[_lin(routes[k][0])]
                copy(k, t, src, _lin(me), routes[k][1]).wait_send()
            pltpu.make_async_copy(ins[t], outs[t].at[_lin(me)], sems[2].at[t]).wait()


def _comm_call(comm, name):
    n = comm.n

    def body(*refs):
        ins, outs, sems = refs[:n], refs[n:2 * n], refs[2 * n:]
        comm.start(ins, outs, sems)
        comm.finish(ins, outs, sems)

    return pl.pallas_call(
        body, name=name, out_shape=comm.out_shape(), in_specs=[ANY] * n, out_specs=[ANY] * n,
        scratch_shapes=comm.scratch(),
    )(*comm.arrays)


def _grid_call(body, *, name, grid, in_specs, out_specs, out_shape, args, scratch_shapes=(), semantics=None, comm=None):
    if comm is None:
        return pl.pallas_call(
            body, name=name, grid=grid, in_specs=in_specs, out_specs=out_specs, out_shape=out_shape,
            scratch_shapes=list(scratch_shapes), compiler_params=_cparams(*semantics),
        )(*args)
    n_in, n_out, n_sc, n = len(in_specs), len(out_specs), len(scratch_shapes), comm.n

    def full(*refs):
        ins, refs = refs[:n_in], refs[n_in:]
        cins, refs = refs[:n], refs[n:]
        outs, refs = refs[:n_out], refs[n_out:]
        couts, refs = refs[:n], refs[n:]
        scratch, sems = refs[:n_sc], refs[n_sc:]
        ids = [pl.program_id(a) for a in range(len(grid))]
        first = functools.reduce(jnp.logical_and, [i == 0 for i in ids])
        last = functools.reduce(jnp.logical_and, [i == g - 1 for i, g in zip(ids, grid)])

        @pl.when(first)
        def _():
            comm.start(cins, couts, sems)

        body(*ins, *outs, *scratch)

        @pl.when(last)
        def _():
            comm.finish(cins, couts, sems)

    return pl.pallas_call(
        full, name=name, grid=grid, in_specs=list(in_specs) + [ANY] * n, out_specs=list(out_specs) + [ANY] * n,
        out_shape=list(out_shape) + comm.out_shape(), scratch_shapes=list(scratch_shapes) + comm.scratch(),
        compiler_params=_cparams(*(["arbitrary"] * len(grid))),
    )(*(list(args) + comm.arrays))


LONG_K_TILES = dict(tm=512, tn=512, tk=8192)


def _mm(a, b, mode, m, n, k, *, out_dtype, name, tm=1024, tn=1024, tk=2048,
        a_m0=0, a_k0=0, b_n0=0, b_k0=0, res=None, comm=None, loss_target=None):
    tm, tn, tk = min(tm, m), min(tn, n), min(tk, k)
    nm, nn, nk = m // tm, n // tn, k // tk
    assert nm * tm == m and nn * tn == n and nk * tk == k
    am, ak, bn, bk = a_m0 // tm, a_k0 // tk, b_n0 // tn, b_k0 // tk
    assert am * tm == a_m0 and ak * tk == a_k0 and bn * tn == b_n0 and bk * tk == b_k0
    if mode == "tn":
        a_spec = pl.BlockSpec((tk, tm), lambda i, j, q: (q + ak, i + am))
        a_dims = (0,)
    else:
        a_spec = pl.BlockSpec((tm, tk), lambda i, j, q: (i + am, q + ak))
        a_dims = (1,)
    if mode == "nt":
        b_spec = pl.BlockSpec((tn, tk), lambda i, j, q: (j + bn, q + bk))
        b_dims = (1,)
    else:
        b_spec = pl.BlockSpec((tk, tn), lambda i, j, q: (q + bk, j + bn))
        b_dims = (0,)
    o_spec = pl.BlockSpec((tm, tn), lambda i, j, q: (i, j))
    has_res = res is not None
    has_loss = loss_target is not None
    n_in = 2 + has_res + has_loss
    n_out = 3 if has_loss else 1

    def body(*refs):
        a_ref, b_ref = refs[0], refs[1]
        res_ref = refs[2] if has_res else None
        o_ref = refs[n_in]
        p = _dot(a_ref[...], b_ref[...], ((a_dims, b_dims), ((), ())))

        def finish(total):
            if has_res:
                total = total + res_ref[...].astype(F32)
            if not has_loss:
                o_ref[...] = total.astype(out_dtype)
                return
            err = total - refs[n_in - 1][...]
            grad = err * (1.0 / n)
            o_ref[...] = grad
            refs[n_in + 1][...] = grad.astype(BF16)
            l_ref = refs[n_in + 2]
            part = jnp.zeros((1, LANES), F32) + 0.5 * jnp.sum(err * err) * (1.0 / n)
            first = (pl.program_id(0) == 0) & (pl.program_id(1) == 0)

            @pl.when(first)
            def _():
                l_ref[...] = part

            @pl.when(jnp.logical_not(first))
            def _():
                l_ref[...] += part

        if nk == 1:
            finish(p)
        else:
            acc_ref = refs[n_in + n_out]
            q = pl.program_id(2)

            @pl.when(q == 0)
            def _():
                acc_ref[...] = p

            @pl.when(q > 0)
            def _():
                acc_ref[...] += p

            @pl.when(q == nk - 1)
            def _():
                finish(acc_ref[...])

    extra_in = ([res] if has_res else []) + ([loss_target] if has_loss else [])
    if has_loss:
        return _grid_call(
            body, name=name, grid=(nm, nn, nk), in_specs=[a_spec, b_spec] + [o_spec] * len(extra_in),
            out_specs=[o_spec, o_spec, pl.BlockSpec((1, LANES), lambda i, j, q: (0, 0))],
            out_shape=[jax.ShapeDtypeStruct((m, n), F32), jax.ShapeDtypeStruct((m, n), BF16),
                       jax.ShapeDtypeStruct((1, LANES), F32)],
            scratch_shapes=[pltpu.VMEM((tm, tn), F32)] if nk > 1 else [],
            args=[a, b] + extra_in, semantics=("arbitrary", "arbitrary", "arbitrary"))
    out, *carried = _grid_call(
        body, name=name, grid=(nm, nn, nk),
        in_specs=[a_spec, b_spec] + [o_spec] * len(extra_in),
        out_specs=[o_spec], out_shape=[jax.ShapeDtypeStruct((m, n), out_dtype)],
        scratch_shapes=[pltpu.VMEM((tm, tn), F32)] if nk > 1 else [],
        args=[a, b] + extra_in, semantics=("parallel", "parallel", "arbitrary"), comm=comm)
    return out if comm is None else (out, carried)


def _rms_fwd(x, w, name, tr=512, comm=None):
    t, d = x.shape
    tr = min(tr, t)

    def body(x_ref, w_ref, o_ref):
        xv = x_ref[...]
        r = lax.rsqrt(jnp.mean(xv * xv, axis=-1, keepdims=True) + EPS)
        o_ref[...] = (xv * r * w_ref[...]).astype(BF16)

    out, *carried = _grid_call(
        body, name=name, grid=(t // tr,),
        in_specs=[pl.BlockSpec((tr, d), lambda i: (i, 0)), pl.BlockSpec((1, d), lambda i: (0, 0))],
        out_specs=[pl.BlockSpec((tr, d), lambda i: (i, 0))],
        out_shape=[jax.ShapeDtypeStruct((t, d), BF16)], args=(x, w), semantics=("parallel",), comm=comm)
    return out if comm is None else (out, carried)


READ_AHEAD = 3


def _rms_bwd(x, w, dy, dres, name, tr=256):
    t, d = x.shape
    tr = min(tr, t)
    n = t // tr
    depth = min(READ_AHEAD, n)

    def body(x_hbm, w_ref, dy_hbm, dres_hbm, dx_ref, dxb_ref, dw_ref, x_buf, dy_buf, dres_buf, sems):
        i = pl.program_id(0)
        streams = ((x_hbm, x_buf), (dy_hbm, dy_buf), (dres_hbm, dres_buf))

        def reads(block, slot):
            rows = pl.ds(pl.multiple_of(block * tr, tr), tr)
            return [pltpu.make_async_copy(src.at[rows, :], buf.at[slot], sems.at[k, slot])
                    for k, (src, buf) in enumerate(streams)]

        @pl.when(i == 0)
        def _():
            for j in range(depth):
                for cp in reads(j, j):
                    cp.start()

        slot = i % depth
        for cp in reads(i, slot):
            cp.wait()
        xv, dyv = x_buf[slot], dy_buf[slot]
        r = lax.rsqrt(jnp.mean(xv * xv, axis=-1, keepdims=True) + EPS)
        gy = dyv * w_ref[...]
        proj = jnp.sum(gy * xv, axis=-1, keepdims=True) * (1.0 / d)
        dx = dres_buf[slot] + r * gy - xv * (r * r * r) * proj
        dx_ref[...] = dx
        dxb_ref[...] = dx.astype(BF16)
        part = jnp.sum(dyv * xv * r, axis=0, keepdims=True)

        @pl.when(i + depth < n)
        def _():
            for cp in reads(i + depth, slot):
                cp.start()

        @pl.when(i == 0)
        def _():
            dw_ref[...] = part

        @pl.when(i > 0)
        def _():
            dw_ref[...] += part

    row = pl.BlockSpec((tr, d), lambda i: (i, 0))
    vec = pl.BlockSpec((1, d), lambda i: (0, 0))
    return pl.pallas_call(
        body, name=name, grid=(n,),
        in_specs=[ANY, vec, ANY, ANY], out_specs=[row, row, vec],
        out_shape=[jax.ShapeDtypeStruct((t, d), F32), jax.ShapeDtypeStruct((t, d), BF16),
                   jax.ShapeDtypeStruct((1, d), F32)],
        scratch_shapes=[pltpu.VMEM((depth, tr, d), F32)] * 3 + [pltpu.SemaphoreType.DMA((3, depth))],
        compiler_params=_cparams("arbitrary"),
    )(x, w, dy, dres)


def _adamw(parts, w, m, v, name, tr=128, transposed=False):
    r, c = w.shape
    tr = tr if r % tr == 0 else r
    c1 = 1.0 - ADAM_B1 ** ADAM_STEP
    c2 = 1.0 - ADAM_B2 ** ADAM_STEP

    def body(p_ref, w_ref, m_ref, v_ref, g_ref, d_ref, nm_ref, nv_ref):
        g = p_ref[0].astype(F32)
        for s in range(1, N_DEV):
            g = g + p_ref[s].astype(F32)
        if transposed:
            i, j = lax.broadcasted_iota(jnp.int32, (tr, tr), 0), lax.broadcasted_iota(jnp.int32, (tr, tr), 1)
            g = _dot_exact(jnp.where(i == j, 1.0, 0.0), g, NT, True)
        nm = ADAM_B1 * m_ref[...] + (1.0 - ADAM_B1) * g
        nv = ADAM_B2 * v_ref[...] + (1.0 - ADAM_B2) * (g * g)
        m_hat = nm / c1
        v_hat = nv / c2
        g_ref[...] = g
        d_ref[...] = -ADAM_LR * (m_hat / (jnp.sqrt(v_hat) + ADAM_EPS) + ADAM_WD * w_ref[...])
        nm_ref[...] = nm
        nv_ref[...] = nv

    blk = pl.BlockSpec((tr, c), lambda i: (i, 0))
    p_spec = (pl.BlockSpec((N_DEV, c, tr), lambda i: (0, 0, i)) if transposed
              else pl.BlockSpec((N_DEV, tr, c), lambda i: (0, i, 0)))
    return pl.pallas_call(
        body, name=name, grid=(r // tr,),
        in_specs=[p_spec, blk, blk, blk],
        out_specs=[blk] * 4, out_shape=[jax.ShapeDtypeStruct((r, c), F32)] * 4,
        compiler_params=_cparams("parallel"),
    )(parts, w, m, v)


ROW_TILE, ROW_HEADS = 512, 16
CONV_ROWS, CONV_HEADS = 512, 8


def _window(dest, inner, n_in, out_index):
    if dest is None:
        return inner, 0, [], [], {}
    buf, col0, total = dest
    if buf is None:
        return total, col0, [], [], {}
    return total, col0, [buf], [ANY], {n_in: out_index}


def _skip_ref(body, at, count):
    return body if count == 0 else (lambda *refs: body(*refs[:at], *refs[at + count:]))


def _heads_of(x, nh):
    return [x[:, h * HEAD_DIM:(h + 1) * HEAD_DIM] for h in range(nh)]


def _headnorm_fwd(proj, w, col0, inner, name, tr=ROW_TILE, hb=ROW_HEADS):
    t = proj.shape[0]
    tr = min(tr, t)
    hb = min(hb, inner // HEAD_DIM)
    wc = hb * HEAD_DIM
    c0 = col0 // wc

    def body(x_ref, w_ref, o_ref):
        outs = []
        for xh in _heads_of(x_ref[...], hb):
            r = lax.rsqrt(jnp.mean(xh * xh, axis=-1, keepdims=True) + EPS)
            outs.append((xh * r * w_ref[...]).astype(BF16))
        o_ref[...] = jnp.concatenate(outs, axis=1)

    return pl.pallas_call(
        body, name=name, grid=(t // tr, inner // wc),
        in_specs=[pl.BlockSpec((tr, wc), lambda i, j: (i, j + c0)), pl.BlockSpec((1, HEAD_DIM), lambda i, j: (0, 0))],
        out_specs=pl.BlockSpec((tr, wc), lambda i, j: (i, j)),
        out_shape=jax.ShapeDtypeStruct((t, inner), BF16),
        compiler_params=_cparams("parallel", "parallel"),
    )(proj, w)


def _headnorm_bwd(dy, proj, w, col0, inner, name, tr=ROW_TILE, hb=ROW_HEADS, dest=None):
    t = proj.shape[0]
    tr = min(tr, t)
    hb = min(hb, inner // HEAD_DIM)
    wc = hb * HEAD_DIM
    c0 = col0 // wc
    width, out0, more, more_specs, aliases = _window(dest, inner, 3, 0)

    def body(dy_ref, x_ref, w_ref, dx_ref, dw_ref):
        outs = []
        part = jnp.zeros((1, HEAD_DIM), F32)
        for dyh, xh in zip(_heads_of(dy_ref[...], hb), _heads_of(x_ref[...], hb)):
            r = lax.rsqrt(jnp.mean(xh * xh, axis=-1, keepdims=True) + EPS)
            gy = dyh * w_ref[...]
            pr = jnp.sum(gy * xh, axis=-1, keepdims=True) * (1.0 / HEAD_DIM)
            outs.append((r * gy - xh * (r * r * r) * pr).astype(BF16))
            part = part + jnp.sum(dyh * xh * r, axis=0, keepdims=True)
        dx_ref[...] = jnp.concatenate(outs, axis=1)
        first = (pl.program_id(0) == 0) & (pl.program_id(1) == 0)

        @pl.when(first)
        def _():
            dw_ref[...] = part

        @pl.when(jnp.logical_not(first))
        def _():
            dw_ref[...] += part

    blk = pl.BlockSpec((tr, wc), lambda i, j: (i, j))
    return pl.pallas_call(
        _skip_ref(body, 3, len(more)), name=name, grid=(t // tr, inner // wc),
        in_specs=[blk, pl.BlockSpec((tr, wc), lambda i, j: (i, j + c0)),
                  pl.BlockSpec((1, HEAD_DIM), lambda i, j: (0, 0))] + more_specs,
        out_specs=[pl.BlockSpec((tr, wc), lambda i, j: (i, j + out0 // wc)),
                   pl.BlockSpec((1, HEAD_DIM), lambda i, j: (0, 0))],
        out_shape=[jax.ShapeDtypeStruct((t, width), BF16), jax.ShapeDtypeStruct((1, HEAD_DIM), F32)],
        input_output_aliases=aliases, compiler_params=_cparams("arbitrary", "arbitrary"),
    )(dy, proj, w, *more)


def _gate_fwd(o, proj, zcol0, inner, name, norm_w=None, tr=ROW_TILE, hb=ROW_HEADS):
    t = o.shape[0]
    tr = min(tr, t)
    hb = min(hb, inner // HEAD_DIM)
    wc = hb * HEAD_DIM
    c0 = zcol0 // wc
    has_w = norm_w is not None

    def body(*refs):
        o_ref, z_ref = refs[0], refs[1]
        out_ref = refs[2 + has_w]
        outs = []
        for oh, zh in zip(_heads_of(o_ref[...], hb), _heads_of(z_ref[...], hb)):
            if has_w:
                r = lax.rsqrt(jnp.mean(oh * oh, axis=-1, keepdims=True) + EPS)
                oh = oh * r * refs[2][...]
            outs.append((oh * _silu(zh)).astype(BF16))
        out_ref[...] = jnp.concatenate(outs, axis=1)

    blk = pl.BlockSpec((tr, wc), lambda i, j: (i, j))
    vec = pl.BlockSpec((1, HEAD_DIM), lambda i, j: (0, 0))
    return pl.pallas_call(
        body, name=name, grid=(t // tr, inner // wc),
        in_specs=[blk, pl.BlockSpec((tr, wc), lambda i, j: (i, j + c0))] + ([vec] if has_w else []),
        out_specs=blk, out_shape=jax.ShapeDtypeStruct((t, inner), BF16),
        compiler_params=_cparams("parallel", "parallel"),
    )(*([o, proj] + ([norm_w] if has_w else [])))


def _gate_bwd(dg, o, proj, zcol0, inner, name, do_dtype, norm_w=None, tr=ROW_TILE, hb=ROW_HEADS, dest=None):
    t = o.shape[0]
    tr = min(tr, t)
    hb = min(hb, inner // HEAD_DIM)
    wc = hb * HEAD_DIM
    c0 = zcol0 // wc
    has_w = norm_w is not None
    width, out0, more, more_specs, aliases = _window(dest, inner, 3 + has_w, 1)

    def body(*refs):
        dg_ref, o_ref, z_ref = refs[0], refs[1], refs[2]
        do_ref, dz_ref = refs[3 + has_w], refs[4 + has_w]
        dos, dzs = [], []
        part = jnp.zeros((1, HEAD_DIM), F32)
        for dgh, oh, zh in zip(_heads_of(dg_ref[...], hb), _heads_of(o_ref[...], hb), _heads_of(z_ref[...], hb)):
            dy = dgh * _silu(zh)
            if has_w:
                w = refs[3][...]
                r = lax.rsqrt(jnp.mean(oh * oh, axis=-1, keepdims=True) + EPS)
                on = oh * r
                dzs.append((dgh * on * w * _dsilu(zh)).astype(BF16))
                gy = dy * w
                pr = jnp.sum(gy * oh, axis=-1, keepdims=True) * (1.0 / HEAD_DIM)
                dos.append((r * gy - oh * (r * r * r) * pr).astype(do_dtype))
                part = part + jnp.sum(dy * on, axis=0, keepdims=True)
            else:
                dzs.append((dgh * oh * _dsilu(zh)).astype(BF16))
                dos.append(dy.astype(do_dtype))
        do_ref[...] = jnp.concatenate(dos, axis=1)
        dz_ref[...] = jnp.concatenate(dzs, axis=1)
        if has_w:
            dw_ref = refs[6]
            first = (pl.program_id(0) == 0) & (pl.program_id(1) == 0)

            @pl.when(first)
            def _():
                dw_ref[...] = part

            @pl.when(jnp.logical_not(first))
            def _():
                dw_ref[...] += part

    blk = pl.BlockSpec((tr, wc), lambda i, j: (i, j))
    vec = pl.BlockSpec((1, HEAD_DIM), lambda i, j: (0, 0))
    return pl.pallas_call(
        _skip_ref(body, 3 + has_w, len(more)), name=name, grid=(t // tr, inner // wc),
        in_specs=[blk, blk, pl.BlockSpec((tr, wc), lambda i, j: (i, j + c0))] + ([vec] if has_w else []) + more_specs,
        out_specs=[blk, pl.BlockSpec((tr, wc), lambda i, j: (i, j + out0 // wc))] + ([vec] if has_w else []),
        out_shape=[jax.ShapeDtypeStruct((t, inner), do_dtype), jax.ShapeDtypeStruct((t, width), BF16)]
        + ([jax.ShapeDtypeStruct((1, HEAD_DIM), F32)] if has_w else []),
        input_output_aliases=aliases, compiler_params=_cparams("arbitrary", "arbitrary"),
    )(*([dg, o, proj] + ([norm_w] if has_w else []) + more))


N_REL = 2 * REL_CLIP + 1
REL_PAD = 640
WIN = 2 * Q_TILE


def _diag_onehot():
    i = lax.broadcasted_iota(jnp.int32, (REL_PAD, WIN), 0)
    j = lax.broadcasted_iota(jnp.int32, (REL_PAD, WIN), 1)
    rel = jnp.where(j < Q_TILE + CHUNK, Q_TILE - j, Q_TILE + WIN - j)
    used = (j < Q_TILE + CHUNK) | (j > WIN - CHUNK)
    idx = jnp.clip(rel, -REL_CLIP, REL_CLIP) + REL_CLIP
    return jnp.where(used & (i == idx), 1.0, 0.0).astype(F32)


def _band_mask():
    r = lax.broadcasted_iota(jnp.int32, (Q_TILE, WIN), 0) // CHUNK
    kc = lax.broadcasted_iota(jnp.int32, (Q_TILE, WIN), 1) // CHUNK - LEFT_CHUNKS
    return (kc <= r) & (kc >= r - LEFT_CHUNKS)


def _bias_tiles(rel_bias_pad, name):
    nh = rel_bias_pad.shape[0]

    def body(rb_ref, o_ref):
        dvec = _nn(rb_ref[...], _diag_onehot(), HIGHEST)[0:1, :]
        tile = pltpu.roll(jnp.broadcast_to(dvec, (Q_TILE, WIN)), 0, 1, stride=1, stride_axis=0)
        o_ref[...] = jnp.where(_band_mask(), tile, NEG_BIG)

    return pl.pallas_call(
        body, name=name, grid=(nh,),
        in_specs=[pl.BlockSpec((None, 8, REL_PAD), lambda h: (h, 0, 0))],
        out_specs=pl.BlockSpec((None, Q_TILE, WIN), lambda h: (h, 0, 0)),
        out_shape=jax.ShapeDtypeStruct((nh, Q_TILE, WIN), F32),
        compiler_params=_cparams("parallel"),
    )(rel_bias_pad)


def _bias_grad(dtile, name):
    nh = dtile.shape[0]

    def body(d_ref, o_ref):
        ri = lax.broadcasted_iota(jnp.int32, (Q_TILE, Q_TILE), 0)
        ci = lax.broadcasted_iota(jnp.int32, (Q_TILE, Q_TILE), 1)
        flip = jnp.where(ri + ci == Q_TILE - 1, 1.0, 0.0).astype(F32)
        rev = _dot_exact(flip, d_ref[...], NN, True)
        rolled = pltpu.roll(rev, WIN - (Q_TILE - 1), 1, stride=1, stride_axis=0)
        diag = jnp.broadcast_to(jnp.sum(rolled, axis=0, keepdims=True), (8, WIN))
        o_ref[...] = _nt(diag, _diag_onehot(), HIGHEST)

    return pl.pallas_call(
        body, name=name, grid=(nh,),
        in_specs=[pl.BlockSpec((None, Q_TILE, WIN), lambda h: (h, 0, 0))],
        out_specs=pl.BlockSpec((None, 8, REL_PAD), lambda h: (h, 0, 0)),
        out_shape=jax.ShapeDtypeStruct((nh, 8, REL_PAD), F32),
        compiler_params=_cparams("parallel"),
    )(dtile)


GROUP = 2 * CHUNK
BAND = Q_TILE + GROUP


N_GROUPS = Q_TILE // GROUP
ATTN_HB = 2
ATTN_HB_FWD = 4


def _head_cols(j):
    return slice(j * HEAD_DIM, (j + 1) * HEAD_DIM)


def _groups(ref, units):
    return jnp.stack([ref[GROUP * g:GROUP * (g + 1), _head_cols(j)] for j, g in units])


def _bands(r0_ref, r1_ref, units):
    return jnp.stack([jnp.concatenate([r0_ref[GROUP * g:, _head_cols(j)], r1_ref[:GROUP * (g + 1), _head_cols(j)]],
                                      axis=0) for j, g in units])


def _group_probs(q, kw, b_ref, units, first_tile):
    bias = jnp.stack([b_ref[j, GROUP * g:GROUP * (g + 1), GROUP * g:GROUP * g + BAND] for j, g in units])
    s = _dot(q, kw, BNT) * (HEAD_DIM ** -0.5) + bias
    col = jnp.stack([lax.broadcasted_iota(jnp.int32, (GROUP, BAND), 1) + GROUP * g for _, g in units])
    s = jnp.where(first_tile & (col < Q_TILE), NEG_BIG, s)
    p = jnp.exp(s - jnp.max(s, axis=-1, keepdims=True))
    return p * (1.0 / jnp.sum(p, axis=-1, keepdims=True))


def _attn_fwd(q, k, v, v_col0, bias, name):
    t, inner = q.shape
    nh, nt = inner // HEAD_DIM, t // Q_TILE
    hb = min(ATTN_HB_FWD, nh)
    wc = hb * HEAD_DIM
    vh = v_col0 // wc
    units = [(j, g) for j in range(hb) for g in range(N_GROUPS)]

    def body(q_ref, k0_ref, k1_ref, v0_ref, v1_ref, b_ref, o_ref):
        p = _group_probs(_groups(q_ref, units), _bands(k0_ref, k1_ref, units), b_ref, units, pl.program_id(1) == 0)
        o = _dot(_bf(p), _bf(_bands(v0_ref, v1_ref, units)), BNN)
        for n, (j, g) in enumerate(units):
            o_ref[GROUP * g:GROUP * (g + 1), _head_cols(j)] = o[n]

    cur = pl.BlockSpec((Q_TILE, wc), lambda h, i: (i, h))
    prev = pl.BlockSpec((Q_TILE, wc), lambda h, i: (jnp.maximum(i - 1, 0), h))
    v_cur = pl.BlockSpec((Q_TILE, wc), lambda h, i: (i, h + vh))
    v_prev = pl.BlockSpec((Q_TILE, wc), lambda h, i: (jnp.maximum(i - 1, 0), h + vh))
    return pl.pallas_call(
        body, name=name, grid=(nh // hb, nt),
        in_specs=[cur, prev, cur, v_prev, v_cur, pl.BlockSpec((hb, Q_TILE, WIN), lambda h, i: (h, 0, 0))],
        out_specs=cur, out_shape=jax.ShapeDtypeStruct((t, inner), F32),
        compiler_params=_cparams("parallel", "parallel"),
    )(q, k, k, v, v, bias)


def _attn_bwd(q, k, v, v_col0, do, bias, name, dest=None):
    t, inner = q.shape
    nh, nt = inner // HEAD_DIM, t // Q_TILE
    scale = HEAD_DIM ** -0.5
    hb = min(ATTN_HB, nh)
    wc = hb * HEAD_DIM
    units = [(j, g) for j in range(hb) for g in range(N_GROUPS)]

    def body(q_ref, k0_ref, k1_ref, v0_ref, v1_ref, do_ref, b_ref, dq_ref, dk_ref, dv_ref, db_ref,
             ck_ref, cv_ref, wk_ref, wv_ref):
        i = pl.program_id(1)

        @pl.when(i == 0)
        def _():
            ck_ref[...] = jnp.zeros(blk, F32)
            cv_ref[...] = jnp.zeros(blk, F32)
            db_ref[...] = jnp.zeros((hb, Q_TILE, WIN), F32)

        @pl.when(i < nt)
        def _():
            wk_ref[...] = jnp.zeros((WIN, wc), F32)
            wv_ref[...] = jnp.zeros((WIN, wc), F32)
            qv, dov = _groups(q_ref, units), _groups(do_ref, units)
            kw, vw = _bands(k0_ref, k1_ref, units), _bf(_bands(v0_ref, v1_ref, units))
            p = _group_probs(qv, kw, b_ref, units, i == 0)
            dp = _dot(dov, vw, BNT)
            ds = p * (dp - jnp.sum(p * dp, axis=-1, keepdims=True))
            pb, dsb = _bf(p), _bf(ds)
            dq = _dot(dsb, kw, BNN) * scale
            dkw = _dot(dsb, qv, BTN) * scale
            dvw = _dot(pb, dov, BTN)
            for n, (j, g) in enumerate(units):
                rows, cols = slice(GROUP * g, GROUP * (g + 1)), slice(GROUP * g, GROUP * g + BAND)
                db_ref[j, rows, cols] += ds[n]
                dq_ref[rows, _head_cols(j)] = dq[n]
                wk_ref[cols, _head_cols(j)] += dkw[n]
                wv_ref[cols, _head_cols(j)] += dvw[n]
            dk_ref[...] = ck_ref[...] + wk_ref[:Q_TILE, :]
            dv_ref[...] = (cv_ref[...] + wv_ref[:Q_TILE, :]).astype(BF16)
            ck_ref[...] = wk_ref[Q_TILE:, :]
            cv_ref[...] = wv_ref[Q_TILE:, :]

        @pl.when(i == nt)
        def _():
            dk_ref[...] = ck_ref[...]
            dv_ref[...] = cv_ref[...].astype(BF16)

    blk = (Q_TILE, wc)
    cur = pl.BlockSpec(blk, lambda h, i: (jnp.minimum(i, nt - 1), h))
    prev = pl.BlockSpec(blk, lambda h, i: (jnp.clip(i - 1, 0, nt - 1), h))
    lag = pl.BlockSpec(blk, lambda h, i: (jnp.maximum(i - 1, 0), h))
    vh = v_col0 // wc
    v_cur = pl.BlockSpec(blk, lambda h, i: (jnp.minimum(i, nt - 1), h + vh))
    v_prev = pl.BlockSpec(blk, lambda h, i: (jnp.clip(i - 1, 0, nt - 1), h + vh))
    tile = pl.BlockSpec((hb, Q_TILE, WIN), lambda h, i: (h, 0, 0))
    width, out0, more, more_specs, aliases = _window(dest, inner, 7, 2)
    return pl.pallas_call(
        _skip_ref(body, 7, len(more)), name=name, grid=(nh // hb, nt + 1),
        in_specs=[cur, prev, cur, v_prev, v_cur, cur, tile] + more_specs,
        out_specs=[cur, lag, pl.BlockSpec(blk, lambda h, i: (jnp.maximum(i - 1, 0), h + out0 // wc)), tile],
        out_shape=[jax.ShapeDtypeStruct((t, inner), F32)] * 2 + [jax.ShapeDtypeStruct((t, width), BF16),
                                                                 jax.ShapeDtypeStruct((nh, Q_TILE, WIN), F32)],
        scratch_shapes=[pltpu.VMEM(blk, F32), pltpu.VMEM(blk, F32),
                        pltpu.VMEM((WIN, wc), F32), pltpu.VMEM((WIN, wc), F32)],
        input_output_aliases=aliases, compiler_params=_cparams("arbitrary", "arbitrary"),
    )(q, k, k, v, v, do, bias, *more)


def _pad_rel_bias(rel_bias):
    nh = rel_bias.shape[0]
    return jnp.broadcast_to(jnp.pad(rel_bias, ((0, 0), (0, REL_PAD - N_REL)))[:, None, :], (nh, 8, REL_PAD))


def _layer_b_fwd(h1, nw, w_in_t, qw, kw, bias, w_out, target):
    t, d = h1.shape
    inner = w_out.shape[0]
    hn = _rms_fwd(h1, nw, "b_rms")
    proj = _mm(hn, w_in_t, "nt", t, 4 * inner, d, out_dtype=F32, name="b_proj")
    qn = _headnorm_fwd(proj, qw, 0, inner, "b_qnorm")
    kn = _headnorm_fwd(proj, kw, inner, inner, "b_knorm")
    o = _attn_fwd(qn, kn, proj, 2 * inner, bias, "b_attn")
    g = _gate_fwd(o, proj, 3 * inner, inner, "b_gate")
    loss_parts = _mm(g, w_out, "nn", t, d, inner, out_dtype=F32, name="b_out", res=h1, loss_target=target)
    return loss_parts, (hn, proj, qn, kn, o, g)


def _layer_b_bwd(dh2, dh2b, h1, nw, w_in_t, qw, kw, bias, w_out, saved):
    hn, proj, qn, kn, o, g = saved
    t, d = h1.shape
    inner = w_out.shape[0]
    dg = _mm(dh2b, w_out, "nt", t, inner, d, out_dtype=F32, name="b_dgate")
    dw_out = _mm(g, dh2b, "tn", inner, d, t, out_dtype=BF16, name="b_dwout")
    do, dproj = _gate_bwd(dg, o, proj, 3 * inner, inner, "b_gate_bwd", BF16, dest=(None, 3 * inner, 4 * inner))
    dq, dk, dproj, dtile = _attn_bwd(qn, kn, proj, 2 * inner, do, bias, "b_attn_bwd",
                                     dest=(dproj, 2 * inner, 4 * inner))
    dproj, dqw = _headnorm_bwd(dq, proj, qw, 0, inner, "b_qnorm_bwd", dest=(dproj, 0, 4 * inner))
    dproj, dkw = _headnorm_bwd(dk, proj, kw, inner, inner, "b_knorm_bwd", dest=(dproj, inner, 4 * inner))
    dhn = _mm(dproj, w_in_t, "nn", t, d, 4 * inner, out_dtype=F32, name="b_dhn", **LONG_K_TILES)
    dw_in_t = _mm(dproj, hn, "tn", 4 * inner, d, t, out_dtype=BF16, name="b_dwin", **LONG_K_TILES)
    dh1, dh1b, dnw = _rms_bwd(h1, nw, dhn, dh2, "b_rms_bwd")
    drb = _bias_grad(dtile, "b_bias_grad")[:, 0, :N_REL]
    return dh1, dh1b, dnw, dw_in_t, dqw, dkw, drb, dw_out


LANES = 128


def _softplus(x):
    return jnp.maximum(x, 0.0) + jnp.log1p(jnp.exp(-jnp.abs(x)))


def _gates_fwd(ab, alog_row, dt_row, nh, name, tr=1024):
    t = ab.shape[0]
    tr = min(tr, t)

    def body(x_ref, al_ref, dt_ref, o_ref):
        x = x_ref[...]
        lane = lax.broadcasted_iota(jnp.int32, x.shape, 1)
        g = -jnp.exp(al_ref[...]) * _softplus(x + dt_ref[...])
        o_ref[...] = jnp.where(lane < nh, g, jnp.where(lane < 2 * nh, _sigmoid(x), 0.0))

    row = pl.BlockSpec((tr, LANES), lambda i: (i, 0))
    vec = pl.BlockSpec((1, LANES), lambda i: (0, 0))
    return pl.pallas_call(
        body, name=name, grid=(t // tr,), in_specs=[row, vec, vec], out_specs=row,
        out_shape=jax.ShapeDtypeStruct((t, LANES), F32), compiler_params=_cparams("parallel"),
    )(ab, alog_row, dt_row)


def _gates_bwd(ab, alog_row, dt_row, dgates, nh, name, tr=1024):
    t = ab.shape[0]
    tr = min(tr, t)
    npart = dgates.shape[0]

    def body(x_ref, al_ref, dt_ref, dg_ref, dx_ref, s_ref):
        x = x_ref[...]
        lane = lax.broadcasted_iota(jnp.int32, x.shape, 1)
        dgt = dg_ref[0]
        for p in range(1, npart):
            dgt = dgt + dg_ref[p]
        ea = jnp.exp(al_ref[...])
        xa = x + dt_ref[...]
        da = jnp.where(lane < nh, dgt * (-ea) * _sigmoid(xa), 0.0)
        beta = _sigmoid(x)
        db = jnp.where((lane >= nh) & (lane < 2 * nh), dgt * beta * (1.0 - beta), 0.0)
        dx_ref[...] = (da + db).astype(BF16)
        dal = jnp.sum(jnp.where(lane < nh, dgt * (-ea) * _softplus(xa), 0.0), axis=0, keepdims=True)
        ddt = jnp.sum(da, axis=0, keepdims=True)
        r8 = lax.broadcasted_iota(jnp.int32, (8, LANES), 0)
        part = jnp.where(r8 == 0, dal, jnp.where(r8 == 1, ddt, 0.0))

        @pl.when(pl.program_id(0) == 0)
        def _():
            s_ref[...] = part

        @pl.when(pl.program_id(0) > 0)
        def _():
            s_ref[...] += part

    row = pl.BlockSpec((tr, LANES), lambda i: (i, 0))
    vec = pl.BlockSpec((1, LANES), lambda i: (0, 0))
    return pl.pallas_call(
        body, name=name, grid=(t // tr,),
        in_specs=[row, vec, vec, pl.BlockSpec((npart, tr, LANES), lambda i: (0, i, 0))],
        out_specs=[row, pl.BlockSpec((8, LANES), lambda i: (0, 0))],
        out_shape=[jax.ShapeDtypeStruct((t, LANES), BF16), jax.ShapeDtypeStruct((8, LANES), F32)],
        compiler_params=_cparams("arbitrary"),
    )(ab, alog_row, dt_row, dgates)


HALO = 8


def _delayed(ext, rows):
    return [ext[HALO:HALO + rows]] + [pltpu.roll(ext, s, 0)[HALO:HALO + rows] for s in range(1, CONV_K)]


def _conv_taps(delayed, w):
    acc = delayed[0] * w[CONV_K - 1:CONV_K]
    for s in range(1, CONV_K):
        acc = acc + delayed[s] * w[CONV_K - 1 - s:CONV_K - s]
    return acc


def _conv_fwd(proj, conv_w, col0, inner, mode, name, tt=CONV_ROWS, hb=CONV_HEADS):
    t = proj.shape[0]
    tt = min(tt, t)
    hb = min(hb, inner // HEAD_DIM)
    wc = hb * HEAD_DIM
    c0 = col0 // wc
    hpb = tt // HALO

    def body(x_ref, halo_ref, w_ref, o_ref):
        halo = jnp.where(pl.program_id(1) == 0, 0.0, halo_ref[...])
        s = _silu(_conv_taps(_delayed(jnp.concatenate([halo, x_ref[...]], axis=0), tt), w_ref[...]))
        if mode == "v":
            o_ref[...] = s
        else:
            mul = HEAD_DIM ** -0.5 if mode == "q" else 1.0
            o_ref[...] = jnp.concatenate(
                [sh * (lax.rsqrt(jnp.sum(sh * sh, axis=-1, keepdims=True) + EPS) * mul) for sh in _heads_of(s, hb)], axis=1)

    return pl.pallas_call(
        body, name=name, grid=(inner // wc, t // tt),
        in_specs=[pl.BlockSpec((tt, wc), lambda j, i: (i, j + c0)),
                  pl.BlockSpec((HALO, wc), lambda j, i: (jnp.maximum(i * hpb - 1, 0), j + c0)),
                  pl.BlockSpec((CONV_K, wc), lambda j, i: (0, j + c0))],
        out_specs=pl.BlockSpec((tt, wc), lambda j, i: (i, j)),
        out_shape=jax.ShapeDtypeStruct((t, inner), F32),
        compiler_params=_cparams("parallel", "parallel"),
    )(proj, proj, conv_w)


def _conv_bwd(dy, proj, conv_w, col0, inner, mode, name, tt=CONV_ROWS, hb=CONV_HEADS, dest=None):
    t = proj.shape[0]
    tt = min(tt, t)
    nt = t // tt
    hb = min(hb, inner // HEAD_DIM)
    wc = hb * HEAD_DIM
    c0 = col0 // wc
    hpb = tt // HALO
    rows = tt + HALO

    def body(dy_ref, dyn_ref, x_ref, xp_ref, xn_ref, w_ref, dx_ref, dw_ref):
        i = pl.program_id(1)
        w = w_ref[...]
        xprev = jnp.where(i == 0, 0.0, xp_ref[...])
        delayed = _delayed(jnp.concatenate([xprev, x_ref[...], xn_ref[...]], axis=0), rows)
        c = _conv_taps(delayed, w)
        dyv = jnp.concatenate([dy_ref[...], jnp.where(i == nt - 1, 0.0, dyn_ref[...])], axis=0)
        sg = _sigmoid(c)
        s = c * sg
        if mode == "v":
            ds = dyv
        else:
            mul = HEAD_DIM ** -0.5 if mode == "q" else 1.0
            parts = []
            for dyh, sh in zip(_heads_of(dyv, hb), _heads_of(s, hb)):
                r = lax.rsqrt(jnp.sum(sh * sh, axis=-1, keepdims=True) + EPS)
                parts.append(mul * (r * dyh - sh * (r * r * r) * jnp.sum(dyh * sh, axis=-1, keepdims=True)))
            ds = jnp.concatenate(parts, axis=1)
        dc = ds * (sg * (1.0 + c * (1.0 - sg)))
        dx = dc[:tt] * w[CONV_K - 1:CONV_K]
        for sft in range(1, CONV_K):
            dx = dx + pltpu.roll(dc, rows - sft, 0)[:tt] * w[CONV_K - 1 - sft:CONV_K - sft]
        dx_ref[...] = dx.astype(BF16)
        r8 = lax.broadcasted_iota(jnp.int32, (8, wc), 0)
        part = jnp.zeros((8, wc), F32)
        for sft in range(CONV_K):
            part = part + jnp.where(r8 == CONV_K - 1 - sft,
                                    jnp.sum(dc[:tt] * delayed[sft][:tt], axis=0, keepdims=True), 0.0)

        @pl.when(i == 0)
        def _():
            dw_ref[...] = part

        @pl.when(i > 0)
        def _():
            dw_ref[...] += part

    cur = lambda off: pl.BlockSpec((tt, wc), lambda j, i: (i, j + off))
    nxt = lambda off: pl.BlockSpec((HALO, wc), lambda j, i: (jnp.minimum((i + 1) * hpb, t // HALO - 1), j + off))
    width, out0, more, more_specs, aliases = _window(dest, inner, 6, 0)
    return pl.pallas_call(
        _skip_ref(body, 6, len(more)), name=name, grid=(inner // wc, nt),
        in_specs=[cur(0), nxt(0), cur(c0),
                  pl.BlockSpec((HALO, wc), lambda j, i: (jnp.maximum(i * hpb - 1, 0), j + c0)), nxt(c0),
                  pl.BlockSpec((CONV_K, wc), lambda j, i: (0, j + c0))] + more_specs,
        out_specs=[pl.BlockSpec((tt, wc), lambda j, i: (i, j + out0 // wc)),
                   pl.BlockSpec((8, wc), lambda j, i: (0, j))],
        out_shape=[jax.ShapeDtypeStruct((t, width), BF16), jax.ShapeDtypeStruct((8, inner), F32)],
        input_output_aliases=aliases, compiler_params=_cparams("parallel", "arbitrary"),
    )(dy, dy, proj, proj, proj, conv_w, *more)


GDN_HB = 4
GDN_NB = 8
SCAN_HB = 16
SCAN_NB = 4


def _iota2(n, m):
    return lax.broadcasted_iota(jnp.int32, (n, m), 0), lax.broadcasted_iota(jnp.int32, (n, m), 1)


def _head_select(first_head, hb, lane0):
    r, lane = _iota2(8, LANES)
    return jnp.where((r < hb) & (lane == lane0 + first_head + r), 1.0, 0.0).astype(F32)


def _chunk_gates(gt, selg, selb):
    i, j = _iota2(CHUNK, CHUNK)
    gc_all = _dot_exact(jnp.where(j <= i, 1.0, 0.0), gt, NN, True)
    return (_dot_exact(gc_all, selg, NT, False), _dot_exact(selg, gc_all, NT, True),
            _dot_exact(gt, selb, NT, False))


def _decay_terms(gcol, grow):
    i, j = _iota2(CHUNK, CHUNK)
    glast = gcol[:, CHUNK - 1:CHUNK, :]
    decay = jnp.exp(jnp.where(j <= i, gcol - grow, NEG_BIG))
    return jnp.exp(gcol), jnp.exp(glast - gcol), jnp.exp(glast), decay


def _unit_lower_inverse(a):
    i, j = _iota2(CHUNK, CHUNK)
    same16 = (i // 16) == (j // 16)
    same32 = (i // 32) == (j // 32)
    m = jnp.where(same16, -a, 0.0)
    x = jnp.where(i == j, 1.0, 0.0) + m
    for _ in range(3):
        m = _dot3(m, m, BNN)
        x = x + _dot3(x, m, BNN)
    for off in (jnp.where(same32 & jnp.logical_not(same16), a, 0.0), jnp.where(same32, 0.0, a)):
        x = x - _dot3(_dot3(x, off, BNN), x, BNN)
    return x


def _unit_inputs(refs, g_ref, selg, selb, hb, nb):
    units = [(c, h) for c in range(nb) for h in range(hb)]
    rs = lambda c: slice(c * CHUNK, (c + 1) * CHUNK)
    cs = lambda h: slice(h * HEAD_DIM, (h + 1) * HEAD_DIM)
    gates = [_chunk_gates(g_ref[rs(c), :], selg, selb) for c in range(nb)]
    stacked = [jnp.stack([r[rs(c), cs(h)] for c, h in units]) for r in refs]
    gcol = jnp.stack([gates[c][0][:, h:h + 1] for c, h in units])
    grow = jnp.stack([gates[c][1][h:h + 1, :] for c, h in units])
    bcol = jnp.stack([gates[c][2][:, h:h + 1] for c, h in units])
    return units, rs, cs, stacked, gcol, grow, bcol


def _gdn_specs(nh, inner, t, heads=GDN_HB, chunks=GDN_NB):
    hb, nb = min(heads, nh), chunks
    rows = nb * CHUNK
    wide = pl.BlockSpec((rows, hb * HEAD_DIM), lambda g, n: (n, g))
    sq = pl.BlockSpec((hb, rows, CHUNK), lambda g, n: (g, n, 0))
    gts = pl.BlockSpec((rows, LANES), lambda g, n: (n, 0))
    glb = pl.BlockSpec((nb * 8, hb * HEAD_DIM), lambda g, n: (n, g))
    return hb, nb, rows, wide, sq, gts, glb


def _gdn_intra_fwd(q, k, v, gates, nh, name, comm=None):
    t, inner = q.shape
    hb, nb, rows, wide, sq, gts, glb = _gdn_specs(nh, inner, t)

    def body(q_ref, k_ref, v_ref, g_ref, qe_ref, kel_ref, wb_ref, w_ref, u_ref, qk_ref, tm_ref, gl_ref):
        first = pl.program_id(0) * hb
        selg, selb = _head_select(first, hb, 0), _head_select(first, hb, nh)
        i, j = _iota2(CHUNK, CHUNK)
        units, rs, cs, (qv, kv, vv), gcol, grow, bcol = _unit_inputs(
            (q_ref, k_ref, v_ref), g_ref, selg, selb, hb, nb)
        e, el, gl, decay = _decay_terms(gcol, grow)
        kb = kv * bcol
        qbf, kbf = _bf(qv), _bf(kv)
        a = jnp.where(j < i, _dot(_bf(kb), kbf, BNT) * decay, 0.0)
        tm = _unit_lower_inverse(a)
        uw = _dot3(tm, jnp.concatenate([vv * bcol, kb * e], axis=2), BNN)
        qk = _bf(_dot(qbf, kbf, BNT) * decay)
        qe, kel = _bf(qv * e), _bf(kv * el)
        for n, (c, h) in enumerate(units):
            w = uw[n, :, HEAD_DIM:]
            qe_ref[rs(c), cs(h)] = qe[n]
            kel_ref[rs(c), cs(h)] = kel[n]
            wb_ref[rs(c), cs(h)] = _bf(w)
            w_ref[rs(c), cs(h)] = w
            u_ref[rs(c), cs(h)] = uw[n, :, :HEAD_DIM]
            qk_ref[h, rs(c), :] = qk[n]
            tm_ref[h, rs(c), :] = tm[n]
            gl_ref[c * 8:(c + 1) * 8, cs(h)] = jnp.broadcast_to(gl[n], (8, HEAD_DIM))

    big = lambda dt: jax.ShapeDtypeStruct((t, inner), dt)
    return _grid_call(
        body, name=name, grid=(nh // hb, t // rows),
        in_specs=[wide, wide, wide, gts],
        out_specs=[wide] * 5 + [sq, sq, glb],
        out_shape=[big(BF16), big(BF16), big(BF16), big(F32), big(F32),
                   jax.ShapeDtypeStruct((nh, t, CHUNK), BF16), jax.ShapeDtypeStruct((nh, t, CHUNK), F32),
                   jax.ShapeDtypeStruct((t // CHUNK * 8, inner), F32)],
        args=(q, k, v, gates), semantics=("parallel", "parallel"), comm=comm)


def _gdn_scan_fwd(qe, kel, wb, u, qk, glb, nh, name):
    t, inner = u.shape
    hb, nb, rows, wide, sq, _, glb_spec = _gdn_specs(nh, inner, t, SCAN_HB, SCAN_NB)

    def body(qe_ref, kel_ref, wb_ref, u_ref, qk_ref, gl_ref, o_ref, vn_ref, sall_ref, s_ref):
        @pl.when(pl.program_id(1) == 0)
        def _():
            s_ref[...] = jnp.zeros(s_ref.shape, F32)

        cs = lambda h: slice(h * HEAD_DIM, (h + 1) * HEAD_DIM)
        for c in range(nb):
            rs = slice(c * CHUNK, (c + 1) * CHUNK)
            heads = lambda ref: jnp.stack([ref[rs, cs(h)] for h in range(hb)])
            s = s_ref[...]
            if c == 0:
                sall_ref[...] = s
            sb = _bf(s)
            vn = heads(u_ref) - _dot(heads(wb_ref), sb, BNN)
            vnb = _bf(vn)
            o = _dot(heads(qe_ref), sb, BNN) + _dot(qk_ref[:, rs, :], vnb, BNN)
            gl = jnp.stack([gl_ref[c * 8:c * 8 + 1, cs(h)] for h in range(hb)])
            s_ref[...] = s * gl + _dot(heads(kel_ref), vnb, BTN)
            for h in range(hb):
                vn_ref[rs, cs(h)] = vnb[h]
                o_ref[rs, cs(h)] = o[h]

    return pl.pallas_call(
        body, name=name, grid=(nh // hb, t // rows),
        in_specs=[wide, wide, wide, wide, sq, glb_spec],
        out_specs=[wide, wide, pl.BlockSpec((None, hb, HEAD_DIM, HEAD_DIM), lambda g, n: (n, g, 0, 0))],
        out_shape=[jax.ShapeDtypeStruct((t, inner), F32), jax.ShapeDtypeStruct((t, inner), BF16),
                   jax.ShapeDtypeStruct((t // rows, nh, HEAD_DIM, HEAD_DIM), F32)],
        scratch_shapes=[pltpu.VMEM((hb, HEAD_DIM, HEAD_DIM), F32)],
        compiler_params=_cparams("parallel", "arbitrary"),
    )(qe, kel, wb, u, qk, glb)


def _gdn_scan_bwd(do, qe, kel, wb, vn, qk, glb, sall, nh, name, comm=None):
    t, inner = do.shape
    hb, nb, rows, _, _, _, _ = _gdn_specs(nh, inner, t, SCAN_HB, SCAN_NB)
    last = t // rows - 1
    wide = pl.BlockSpec((rows, hb * HEAD_DIM), lambda g, n: (last - n, g))
    sq = pl.BlockSpec((hb, rows, CHUNK), lambda g, n: (g, last - n, 0))
    glb_spec = pl.BlockSpec((nb * 8, hb * HEAD_DIM), lambda g, n: (last - n, g))

    def body(do_ref, qe_ref, kel_ref, wb_ref, vn_ref, qk_ref, gl_ref, sall_ref,
             dvn_ref, dw_ref, dqe_ref, dkel_ref, dqk_ref, dgl_ref, ds_ref):
        @pl.when(pl.program_id(1) == 0)
        def _():
            ds_ref[...] = jnp.zeros(ds_ref.shape, F32)

        cs = lambda h: slice(h * HEAD_DIM, (h + 1) * HEAD_DIM)
        chunk = lambda ref, c: jnp.stack([ref[c * CHUNK:(c + 1) * CHUNK, cs(h)] for h in range(hb)])
        decay = lambda c: jnp.stack([gl_ref[c * 8:c * 8 + 1, cs(h)] for h in range(hb)])
        states = [sall_ref[...]]
        for c in range(nb - 1):
            states.append(states[c] * decay(c) + _dot(chunk(kel_ref, c), _bf(chunk(vn_ref, c)), BTN))
        for c in reversed(range(nb)):
            rs = slice(c * CHUNK, (c + 1) * CHUNK)
            heads = lambda ref: jnp.stack([ref[rs, cs(h)] for h in range(hb)])
            ds, s = ds_ref[...], states[c]
            dsb, sb = _bf(ds), _bf(s)
            dob, vnb = _bf(heads(do_ref)), _bf(heads(vn_ref))
            dvn = _dot(qk_ref[:, rs, :], dob, BTN) + _dot(heads(kel_ref), dsb, BNN)
            dvnb = _bf(dvn)
            dw = -_dot(dvnb, sb, BNT)
            dqe = _dot(dob, sb, BNT)
            dkel = _dot(vnb, dsb, BNT)
            dqk_ref[:, rs, :] = _dot(dob, vnb, BNT)
            dgl = jnp.sum(jnp.sum(ds * s, axis=2, keepdims=True), axis=1, keepdims=True)
            ds_ref[...] = ds * decay(c) + _dot(heads(qe_ref), dob, BTN) - _dot(heads(wb_ref), dvnb, BTN)
            for h in range(hb):
                dvn_ref[rs, cs(h)] = dvn[h]
                dw_ref[rs, cs(h)] = dw[h]
                dqe_ref[rs, cs(h)] = dqe[h]
                dkel_ref[rs, cs(h)] = dkel[h]
                dgl_ref[c * 8:(c + 1) * 8, cs(h)] = jnp.broadcast_to(dgl[h], (8, HEAD_DIM))

    big = jax.ShapeDtypeStruct((t, inner), F32)
    return _grid_call(
        body, name=name, grid=(nh // hb, t // rows),
        in_specs=[wide, wide, wide, wide, wide, sq, glb_spec,
                  pl.BlockSpec((None, hb, HEAD_DIM, HEAD_DIM), lambda g, n: (last - n, g, 0, 0))],
        out_specs=[wide] * 4 + [sq, glb_spec],
        out_shape=[big] * 4 + [jax.ShapeDtypeStruct((nh, t, CHUNK), F32),
                               jax.ShapeDtypeStruct((t // CHUNK * 8, inner), F32)],
        scratch_shapes=[pltpu.VMEM((hb, HEAD_DIM, HEAD_DIM), F32)],
        args=(do, qe, kel, wb, vn, qk, glb, sall), semantics=("parallel", "arbitrary"), comm=comm)


def _gdn_intra_bwd(q, k, v, gates, tm, w, u, dvn, dw, dqe, dkel, dqk, dglb, nh, name, comm=None):
    t, inner = q.shape
    hb, nb, rows, wide, sq, gts, glb = _gdn_specs(nh, inner, t)

    def body(q_ref, k_ref, v_ref, g_ref, tm_ref, w_ref, u_ref, dvn_ref, dw_ref, dqe_ref, dkel_ref, dqk_ref,
             dgl_ref, dq_ref, dk_ref, dv_ref, dg_ref):
        first = pl.program_id(0) * hb
        selg, selb = _head_select(first, hb, 0), _head_select(first, hb, nh)
        i, j = _iota2(CHUNK, CHUNK)
        lane8 = lax.broadcasted_iota(jnp.int32, (CHUNK, 8), 1)
        row = lax.broadcasted_iota(jnp.int32, (CHUNK, 1), 0)
        lower = jnp.where(j <= i, 1.0, 0.0).astype(F32)
        rsum = lambda x: jnp.sum(x, axis=-1, keepdims=True)
        units, rs, cs, (qv, kv, vv, wv, uv, dvn, dw, dqe, dkel), gcol, grow, bcol = _unit_inputs(
            (q_ref, k_ref, v_ref, w_ref, u_ref, dvn_ref, dw_ref, dqe_ref, dkel_ref), g_ref, selg, selb, hb, nb)
        nu = len(units)
        tmv = jnp.stack([tm_ref[h, rs(c), :] for c, h in units])
        dqk = jnp.where(j <= i, jnp.stack([dqk_ref[h, rs(c), :] for c, h in units]), 0.0)
        dgl = jnp.stack([dgl_ref[c * 8:c * 8 + 1, h * HEAD_DIM:h * HEAD_DIM + 1] for c, h in units])
        e, el, gl, decay = _decay_terms(gcol, grow)
        kb = kv * bcol
        qb, kbf, kbb = _bf(qv), _bf(kv), _bf(kb)
        dqkr = _bf(dqk * decay)
        dq = dqe * e + _dot(dqkr, kbf, BNN)
        dk = dkel * el + _dot(dqkr, qb, BTN)
        de = rsum(dqe * qv)
        del_ = rsum(dkel * kv)
        mq = dqk * _dot(qb, kbf, BNT) * decay
        dsol = _dot3(tmv, jnp.concatenate([dvn, dw], axis=2), BTN)
        dvb, dkbe = dsol[:, :, :HEAD_DIM], dsol[:, :, HEAD_DIM:]
        da = -jnp.where(j < i, _dot3(dsol, jnp.concatenate([uv, wv], axis=2), BNT), 0.0)
        dkk = _bf(da * decay)
        ma = da * _dot(kbb, kbf, BNT) * decay
        dkb = dkbe * e + _dot(dkk, kbf, BNN)
        de = de + rsum(dkbe * kb)
        dk = dk + _dot(dkk, kbb, BTN) + dkb * bcol
        dv = dvb * bcol
        dbeta = rsum(dkb * kv) + rsum(dvb * vv)
        m = mq + ma
        ones = jnp.ones((nu, CHUNK, LANES), F32)
        dgc = rsum(m) - _dot_exact(m, ones, BTN, False)[:, :, 0:1] + de * e - del_ * el
        tail = jnp.sum(del_ * el, axis=1, keepdims=True) + dgl * gl
        dgc = dgc + jnp.where(row == CHUNK - 1, tail, 0.0)
        for n, (c, h) in enumerate(units):
            dq_ref[rs(c), cs(h)] = dq[n]
            dk_ref[rs(c), cs(h)] = dk[n]
            dv_ref[rs(c), cs(h)] = dv[n]
        for c in range(nb):
            dgc_cols = jnp.zeros((CHUNK, 8), F32)
            dbeta_cols = jnp.zeros((CHUNK, 8), F32)
            for h in range(hb):
                dgc_cols = jnp.where(lane8 == h, dgc[c * hb + h], dgc_cols)
                dbeta_cols = jnp.where(lane8 == h, dbeta[c * hb + h], dbeta_cols)
            dg_cols = _dot_exact(lower, dgc_cols, TN, True)
            dg_ref[rs(c), :] = _dot_exact(dg_cols, selg, NN, False) + _dot_exact(dbeta_cols, selb, NN, False)

    big = jax.ShapeDtypeStruct((t, inner), F32)
    return _grid_call(
        body, name=name, grid=(nh // hb, t // rows),
        in_specs=[wide, wide, wide, gts, sq, wide, wide, wide, wide, wide, wide, sq, glb],
        out_specs=[wide, wide, wide, pl.BlockSpec((None, rows, LANES), lambda g, n: (g, n, 0))],
        out_shape=[big, big, big, jax.ShapeDtypeStruct((nh // hb, t, LANES), F32)],
        args=(q, k, v, gates, tm, w, u, dvn, dw, dqe, dkel, dqk, dglb), semantics=("parallel", "parallel"),
        comm=comm)


def _layer_a_fwd(x, hn, w_in_t, w_ab_t, conv_w, alog_row, dt_row, onw, nh, comm, w_out_of):
    t, d = x.shape
    inner = nh * HEAD_DIM
    proj = _mm(hn, w_in_t, "nt", t, 4 * inner, d, out_dtype=F32, name="a_proj")
    ab = _mm(hn, w_ab_t, "nt", t, LANES, d, out_dtype=F32, name="a_proj_ab")
    gates = _gates_fwd(ab, alog_row, dt_row, nh, "a_gates")
    q = _conv_fwd(proj, conv_w, 0, inner, "q", "a_conv_q")
    k = _conv_fwd(proj, conv_w, inner, inner, "k", "a_conv_k")
    v = _conv_fwd(proj, conv_w, 2 * inner, inner, "v", "a_conv_v")
    qe, kel, wb, w, u, qk, tm, glb, *carried = _gdn_intra_fwd(q, k, v, gates, nh, "a_intra", comm)
    o, vn, sall = _gdn_scan_fwd(qe, kel, wb, u, qk, glb, nh, "a_scan")
    g = _gate_fwd(o, proj, 3 * inner, inner, "a_gate", norm_w=onw)
    w_out = w_out_of(carried)
    h1 = _mm(g, w_out, "nn", t, d, inner, out_dtype=F32, name="a_out", res=x)
    return h1, (hn, proj, ab, gates, q, k, v, qe, kel, wb, w, u, qk, tm, glb, o, vn, sall, g), w_out, carried


def _layer_a_bwd(dh1, dh1b, x, nw, w_in_t, w_ab_t, conv_w, alog_row, dt_row, onw, w_out, nh, saved, comms_of,
                 own_comm):
    hn, proj, ab, gates, q, k, v, qe, kel, wb, w, u, qk, tm, glb, o, vn, sall, g = saved
    t, d = x.shape
    inner = w_out.shape[0]
    dg = _mm(dh1b, w_out, "nt", t, inner, d, out_dtype=F32, name="a_dgate")
    dw_out = _mm(g, dh1b, "tn", inner, d, t, out_dtype=BF16, name="a_dwout")
    comm_scan, comm_intra = comms_of(dw_out)
    do, dproj, donw = _gate_bwd(dg, o, proj, 3 * inner, inner, "a_gate_bwd", BF16, norm_w=onw,
                                dest=(None, 3 * inner, 4 * inner))
    dvn, dw, dqe, dkel, dqk, dglb, *carried_scan = _gdn_scan_bwd(do, qe, kel, wb, vn, qk, glb, sall, nh,
                                                                 "a_scan_bwd", comm_scan)
    dq, dk, dv, dgates, *carried = _gdn_intra_bwd(q, k, v, gates, tm, w, u, dvn, dw, dqe, dkel, dqk, dglb, nh,
                                                  "a_intra_bwd", comm_intra)
    carried = carried_scan + carried
    dproj, dcq = _conv_bwd(dq, proj, conv_w, 0, inner, "q", "a_conv_q_bwd", dest=(dproj, 0, 4 * inner))
    dproj, dck = _conv_bwd(dk, proj, conv_w, inner, inner, "k", "a_conv_k_bwd", dest=(dproj, inner, 4 * inner))
    dproj, dcv = _conv_bwd(dv, proj, conv_w, 2 * inner, inner, "v", "a_conv_v_bwd",
                           dest=(dproj, 2 * inner, 4 * inner))
    dab, dsmall = _gates_bwd(ab, alog_row, dt_row, dgates, nh, "a_gates_bwd")
    dw_in_t = _mm(dproj, hn, "tn", 4 * inner, d, t, out_dtype=BF16, name="a_dwin", **LONG_K_TILES)
    dw_ab_t = _mm(dab, hn, "tn", LANES, d, t, out_dtype=BF16, name="a_dwin_ab")
    dconv = jnp.concatenate([dcq[:CONV_K], dck[:CONV_K], dcv[:CONV_K]], axis=1)
    dhn = _mm(dab, w_ab_t, "nn", t, d, LANES, out_dtype=F32, name="a_dhn_ab")
    own = own_comm(dw_in_t, dw_ab_t, dconv)
    dhn = _mm(dproj, w_in_t, "nn", t, d, 4 * inner, out_dtype=F32, name="a_dhn", res=dhn, comm=own,
              **LONG_K_TILES)
    dhn, carried_own = dhn if own is not None else (dhn, [])
    dx, _, dnw = _rms_bwd(x, nw, dhn, dh1, "a_rms_bwd")
    return dx, dnw, dsmall, donw, carried, carried_own


def _rows_of(a, rows):
    flat = a.reshape(-1)
    return jnp.pad(flat, (0, rows * LANES - flat.shape[0])).reshape(rows, LANES)


def _to_slabs(g, axis):
    shape = g.shape[:axis] + (N_DEV, g.shape[axis] // N_DEV) + g.shape[axis + 1:]
    return jnp.moveaxis(g.reshape(shape), axis, 0)


def _from_slabs(s, axis):
    m = jnp.moveaxis(s, 0, axis)
    return m.reshape(m.shape[:axis] + (m.shape[axis] * m.shape[axis + 1],) + m.shape[axis + 2:])


def kernel(x, norm_w, a_w_in, a_conv_w, a_a_log, a_dt_bias, a_out_norm_w, a_w_out, b_w_in, b_q_norm_w, b_k_norm_w, b_rel_bias, b_w_out, loss_target, m_norm_w, m_a_w_in, m_a_conv_w, m_a_a_log, m_a_dt_bias, m_a_out_norm_w, m_a_w_out, m_b_w_in, m_b_q_norm_w, m_b_k_norm_w, m_b_rel_bias, m_b_w_out, v_norm_w, v_a_w_in, v_a_conv_w, v_a_a_log, v_a_dt_bias, v_a_out_norm_w, v_a_w_out, v_b_w_in, v_b_q_norm_w, v_b_k_norm_w, v_b_rel_bias, v_b_w_out):
    xs, target = x[0], loss_target[0]
    nh = a_a_log.shape[-1]
    inner = N_DEV * a_w_out.shape[1]

    d = xs.shape[1]
    nw0, nw1 = norm_w[0:1], norm_w[1:2]
    hn0, (ga_in, g_conv) = _rms_fwd(
        xs, nw0, "a_rms", comm=_RoutedGather([a_w_in[0].T.astype(BF16), a_conv_w[0]]))
    wa_in_t = ga_in.reshape(-1, d)
    wa_ab_t = jnp.pad(wa_in_t[4 * inner:], ((0, LANES - 2 * nh), (0, 0)))
    conv_w = _from_slabs(g_conv, 1)
    alog_row = jnp.pad(a_a_log, ((0, 0), (0, LANES - nh)))
    dt_row = jnp.pad(a_dt_bias, ((0, 0), (0, LANES - nh)))

    h1, saved_a, wa_out, (gb_in, gb_out, _) = _layer_a_fwd(
        xs, hn0, wa_in_t, wa_ab_t, conv_w, alog_row, dt_row, a_out_norm_w, nh,
        _Comm("gather", [b_w_in[0].T.astype(BF16), b_w_out[0].astype(BF16), a_w_out[0].astype(BF16)]),
        lambda gathered: _from_slabs(gathered[2], 0))
    wb_in_t = gb_in.reshape(-1, d)
    wb_out = _from_slabs(gb_out, 0)
    bias = _bias_tiles(_pad_rel_bias(b_rel_bias[0]), "b_bias_tiles")
    (dh2, dh2b, loss_row), saved_b = _layer_b_fwd(h1, nw1, wb_in_t, b_q_norm_w, b_k_norm_w, bias, wb_out, target)

    dh1, dh1b, dnw1, dwb_in_t, dqw, dkw, drb, dwb_out = _layer_b_bwd(
        dh2, dh2b, h1, nw1, wb_in_t, b_q_norm_w, b_k_norm_w, bias, wb_out, saved_b)

    def exchange_early(dwa_out):
        return (_Comm("exchange", [_to_slabs(dwb_out, 0).astype(BF16), _to_slabs(dwa_out, 0).astype(BF16)]),
                _Comm("exchange", [dwb_in_t.reshape(N_DEV, -1, d).astype(BF16)]))

    def exchange_last(dwa_in_t, dwa_ab_t, dconv):
        full = jnp.concatenate([dwa_in_t, dwa_ab_t[:2 * nh]], axis=0)
        return _Comm("exchange", [full.reshape(N_DEV, -1, d).astype(BF16), _to_slabs(dconv, 1)])

    dx, dnw0, dsmall, donw, (pb_out, pa_out, pb_in), (pa_in, p_conv) = _layer_a_bwd(
        dh1, dh1b, xs, nw0, wa_in_t, wa_ab_t, conv_w, alog_row, dt_row, a_out_norm_w, wa_out, nh, saved_a,
        exchange_early, exchange_last)
    big = {}
    for name, p, w, m, v in (("a_w_in", pa_in, a_w_in, m_a_w_in, v_a_w_in),
                             ("a_w_out", pa_out, a_w_out, m_a_w_out, v_a_w_out),
                             ("b_w_in", pb_in, b_w_in, m_b_w_in, v_b_w_in),
                             ("b_w_out", pb_out, b_w_out, m_b_w_out, v_b_w_out),
                             ("a_conv_w", p_conv, a_conv_w, m_a_conv_w, v_a_conv_w)):
        big[name] = [o[None] for o in _adamw(p, w[0], m[0], v[0], "adamw_" + name,
                                             transposed=name in ("a_w_in", "b_w_in"))]

    small = (("norm_w", norm_w, m_norm_w, v_norm_w, jnp.concatenate([dnw0, dnw1], axis=0)),
             ("a_a_log", a_a_log, m_a_a_log, v_a_a_log, dsmall[0:1, :nh]),
             ("a_dt_bias", a_dt_bias, m_a_dt_bias, v_a_dt_bias, dsmall[1:2, :nh]),
             ("a_out_norm_w", a_out_norm_w, m_a_out_norm_w, v_a_out_norm_w, donw),
             ("b_q_norm_w", b_q_norm_w, m_b_q_norm_w, v_b_q_norm_w, dqw),
             ("b_k_norm_w", b_k_norm_w, m_b_k_norm_w, v_b_k_norm_w, dkw),
             ("b_rel_bias", b_rel_bias, m_b_rel_bias, v_b_rel_bias, drb))
    rows = [8 * (-(-w.size // (8 * LANES))) for _, w, _, _, _ in small]
    pack = lambda arrs: jnp.concatenate([_rows_of(a, r) for a, r in zip(arrs, rows)] + [jnp.zeros((8, LANES), F32)], axis=0)
    g_pack = jnp.concatenate([_rows_of(g, r) for (_, _, _, _, g), r in zip(small, rows)]
                             + [jnp.broadcast_to(loss_row, (8, LANES))], axis=0)
    (g_all,) = _comm_call(_Comm("gather", [g_pack]), "gather_small_grads")
    outs_small = _adamw(g_all, pack([s[1] for s in small]), pack([s[2] for s in small]),
                        pack([s[3] for s in small]), "adamw_small")
    start = 0
    for (name, w, _, _, _), r in zip(small, rows):
        big[name] = [o[start:start + r].reshape(-1)[:w.size].reshape(w.shape) for o in outs_small]
        start += r
    loss = outs_small[0][start, 0]

    order = ("norm_w", "a_w_in", "a_conv_w", "a_a_log", "a_dt_bias", "a_out_norm_w", "a_w_out", "b_w_in",
             "b_q_norm_w", "b_k_norm_w", "b_rel_bias", "b_w_out")
    return (loss, dx[None]) + tuple(big[n][i] for i in range(4) for n in order)
```

```python
import functools

import jax
import jax.numpy as jnp
from jax import lax
from jax.experimental import pallas as pl
from jax.experimental.pallas import tpu as pltpu

F32 = jnp.float32
BF16 = jnp.bfloat16
MESH_IDS = pl.DeviceIdType.MESH
N_DEV = 8
CHUNK = 64
HEAD_DIM = 128
EPS = 1e-6
CONV_K = 4
LEFT_CHUNKS = 8
REL_CLIP = 256
Q_TILE = LEFT_CHUNKS * CHUNK
ADAM_LR = 0.001
ADAM_B1 = 0.9
ADAM_B2 = 0.999
ADAM_EPS = 1e-08
ADAM_WD = 0.01
ADAM_STEP = 10
NEG_BIG = -1e30
VMEM_LIMIT_BYTES = 56 * 1024 * 1024
HIGHEST = lax.Precision.HIGHEST
ANY = pl.BlockSpec(memory_space=pl.ANY)


def _cparams(*sem):
    return pltpu.CompilerParams(dimension_semantics=tuple(sem), vmem_limit_bytes=VMEM_LIMIT_BYTES)


NN, NT, TN = (((1,), (0,)), ((), ())), (((1,), (1,)), ((), ())), (((0,), (0,)), ((), ()))
BNN, BNT, BTN = (((2,), (1,)), ((0,), (0,))), (((2,), (2,)), ((0,), (0,))), (((1,), (1,)), ((0,), (0,)))


def _dot(a, b, dims, precision=None):
    return lax.dot_general(a, b, dims, preferred_element_type=F32, precision=precision)


def _nn(a, b, precision=None):
    return _dot(a, b, NN, precision)


def _nt(a, b, precision=None):
    return _dot(a, b, NT, precision)


def _tn(a, b, precision=None):
    return _dot(a, b, TN, precision)


def _bf(x):
    return x.astype(BF16)


def _split(x, pieces=2):
    out = []
    for _ in range(pieces - 1):
        hi = x.astype(BF16)
        out.append(hi)
        x = x - hi.astype(F32)
    return out + [x.astype(BF16)]


def _dot3(a, b, dims):
    (ah, al), (bh, bl) = _split(a), _split(b)
    return _dot(ah, bh, dims) + (_dot(ah, bl, dims) + _dot(al, bh, dims))


def _dot_exact(a, b, dims, split_b):
    if split_b:
        a = a.astype(BF16)
        parts = [_dot(a, p, dims) for p in _split(b, 3)]
    else:
        b = b.astype(BF16)
        parts = [_dot(p, b, dims) for p in _split(a, 3)]
    return parts[0] + (parts[1] + parts[2])


def _sigmoid(x):
    return 0.5 * jnp.tanh(0.5 * x) + 0.5


def _silu(x):
    return x * _sigmoid(x)


def _dsilu(x):
    s = _sigmoid(x)
    return s * (1.0 + x * (1.0 - s))


def _my_pos():
    return lax.axis_index("x"), lax.axis_index("y"), lax.axis_index("c")


def _peers(x, y, c):
    def flip(v, f):
        return 1 - v if f else v

    return [(flip(x, kx), flip(y, ky), flip(c, kc)) for kx in (0, 1) for ky in (0, 1) for kc in (0, 1)][1:]


def _lin(p):
    return 4 * p[0] + 2 * p[1] + p[2]


class _Comm:
    def __init__(self, kind, arrays):
        self.kind, self.arrays, self.n = kind, list(arrays), len(arrays)

    def out_shape(self):
        lead = (N_DEV,) if self.kind == "gather" else ()
        return [jax.ShapeDtypeStruct(lead + a.shape, a.dtype) for a in self.arrays]

    def scratch(self):
        return [pltpu.SemaphoreType.DMA((7 * self.n,)), pltpu.SemaphoreType.DMA((7 * self.n,)),
                pltpu.SemaphoreType.DMA((self.n,))]

    def _copies(self, ins, outs, sems, arrivals):
        send_sems, recv_sems, local_sems = sems
        x, y, c = _my_pos()
        me = _lin((x, y, c))
        gather = self.kind == "gather"
        mine = [ins[t] if gather else ins[t].at[me] for t in range(self.n)]
        remote = []
        for k, peer in enumerate(_peers(x, y, c)):
            for t in range(self.n):
                if arrivals:
                    src, dst = mine[t], outs[t].at[_lin(peer)]
                else:
                    src, dst = (ins[t] if gather else ins[t].at[_lin(peer)]), outs[t].at[me]
                remote.append(pltpu.make_async_remote_copy(
                    src_ref=src, dst_ref=dst, send_sem=send_sems.at[k * self.n + t],
                    recv_sem=recv_sems.at[k * self.n + t], device_id=peer, device_id_type=MESH_IDS))
        if arrivals:
            return remote
        return [pltpu.make_async_copy(mine[t], outs[t].at[me], local_sems.at[t]) for t in range(self.n)], remote

    def start(self, ins, outs, sems):
        local, sends = self._copies(ins, outs, sems, False)
        for cp in local + sends:
            cp.start()

    def finish(self, ins, outs, sems):
        for cp in self._copies(ins, outs, sems, True):
            cp.wait_recv()
        local, sends = self._copies(ins, outs, sems, False)
        for cp in sends:
            cp.wait_send()
        for cp in local:
            cp.wait()


def _xor(a, b):
    return a + b - 2 * a * b


class _RoutedGather(_Comm):
    def __init__(self, arrays):
        super().__init__("gather", arrays)

    def _plan(self, outs, sems):
        send_sems, recv_sems, _ = sems
        x, y, c = _my_pos()
        sib, xn, yn, dg = (x, y, 1 - c), (1 - x, y, c), (x, 1 - y, c), (1 - x, 1 - y, c)
        via = (_xor(x, 1 - c), _xor(y, c), c)
        onto = (_xor(x, c), _xor(y, 1 - c), c)
        routes = [(None, sib, sib), (None, xn, xn), (None, yn, yn), (via, onto, dg),
                  (xn, sib, (1 - x, y, 1 - c)), (yn, sib, (x, 1 - y, 1 - c)), (dg, sib, (1 - x, 1 - y, 1 - c))]

        def copy(k, t, src, slot, target):
            return pltpu.make_async_remote_copy(
                src_ref=src, dst_ref=outs[t].at[slot], send_sem=send_sems.at[k * self.n + t],
                recv_sem=recv_sems.at[k * self.n + t], device_id=target, device_id_type=MESH_IDS)

        return (x, y, c), routes, copy

    def start(self, ins, outs, sems):
        me, routes, copy = self._plan(outs, sems)
        for t in range(self.n):
            pltpu.make_async_copy(ins[t], outs[t].at[_lin(me)], sems[2].at[t]).start()
            for k in range(3):
                copy(k, t, ins[t], _lin(me), routes[k][1]).start()

    def finish(self, ins, outs, sems):
        me, routes, copy = self._plan(outs, sems)

        def arrived(k):
            for t in range(self.n):
                copy(k, t, ins[t], _lin(routes[k][2]), me).wait_recv()

        def pass_on(k):
            for t in range(self.n):
                copy(k, t, outs[t].at[_lin(routes[k][0])], _lin(routes[k][0]), routes[k][1]).start()

        arrived(1)
        arrived(2)
        for k in (3, 4, 5):
            pass_on(k)
        arrived(3)
        pass_on(6)
        for k in (0, 4, 5, 6):
            arrived(k)
        for t in range(self.n):
            for k in range(7):
                src = ins[t] if k < 3 else outs[t].at[_lin(routes[k][0])]
                copy(k, t, src, _lin(me), routes[k][1]).wait_send()
            pltpu.make_async_copy(ins[t], outs[t].at[_lin(me)], sems[2].at[t]).wait()


def _comm_call(comm, name):
    n = comm.n

    def body(*refs):
        ins, outs, sems = refs[:n], refs[n:2 * n], refs[2 * n:]
        comm.start(ins, outs, sems)
        comm.finish(ins, outs, sems)

    return pl.pallas_call(
        body, name=name, out_shape=comm.out_shape(), in_specs=[ANY] * n, out_specs=[ANY] * n,
        scratch_shapes=comm.scratch(),
    )(*comm.arrays)


def _grid_call(body, *, name, grid, in_specs, out_specs, out_shape, args, scratch_shapes=(), semantics=None, comm=None):
    if comm is None:
        return pl.pallas_call(
            body, name=name, grid=grid, in_specs=in_specs, out_specs=out_specs, out_shape=out_shape,
            scratch_shapes=list(scratch_shapes), compiler_params=_cparams(*semantics),
        )(*args)
    n_in, n_out, n_sc, n = len(in_specs), len(out_specs), len(scratch_shapes), comm.n

    def full(*refs):
        ins, refs = refs[:n_in], refs[n_in:]
        cins, refs = refs[:n], refs[n:]
        outs, refs = refs[:n_out], refs[n_out:]
        couts, refs = refs[:n], refs[n:]
        scratch, sems = refs[:n_sc], refs[n_sc:]
        ids = [pl.program_id(a) for a in range(len(grid))]
        first = functools.reduce(jnp.logical_and, [i == 0 for i in ids])
        last = functools.reduce(jnp.logical_and, [i == g - 1 for i, g in zip(ids, grid)])

        @pl.when(first)
        def _():
            comm.start(cins, couts, sems)

        body(*ins, *outs, *scratch)

        @pl.when(last)
        def _():
            comm.finish(cins, couts, sems)

    return pl.pallas_call(
        full, name=name, grid=grid, in_specs=list(in_specs) + [ANY] * n, out_specs=list(out_specs) + [ANY] * n,
        out_shape=list(out_shape) + comm.out_shape(), scratch_shapes=list(scratch_shapes) + comm.scratch(),
        compiler_params=_cparams(*(["arbitrary"] * len(grid))),
    )(*(list(args) + comm.arrays))


LONG_K_TILES = dict(tm=512, tn=512, tk=8192)


def _mm(a, b, mode, m, n, k, *, out_dtype, name, tm=1024, tn=1024, tk=2048,
        a_m0=0, a_k0=0, b_n0=0, b_k0=0, res=None, comm=None, loss_target=None):
    tm, tn, tk = min(tm, m), min(tn, n), min(tk, k)
    nm, nn, nk = m // tm, n // tn, k // tk
    assert nm * tm == m and nn * tn == n and nk * tk == k
    am, ak, bn, bk = a_m0 // tm, a_k0 // tk, b_n0 // tn, b_k0 // tk
    assert am * tm == a_m0 and ak * tk == a_k0 and bn * tn == b_n0 and bk * tk == b_k0
    if mode == "tn":
        a_spec = pl.BlockSpec((tk, tm), lambda i, j, q: (q + ak, i + am))
        a_dims = (0,)
    else:
        a_spec = pl.BlockSpec((tm, tk), lambda i, j, q: (i + am, q + ak))
        a_dims = (1,)
    if mode == "nt":
        b_spec = pl.BlockSpec((tn, tk), lambda i, j, q: (j + bn, q + bk))
        b_dims = (1,)
    else:
        b_spec = pl.BlockSpec((tk, tn), lambda i, j, q: (q + bk, j + bn))
        b_dims = (0,)
    o_spec = pl.BlockSpec((tm, tn), lambda i, j, q: (i, j))
    has_res = res is not None
    has_loss = loss_target is not None
    n_in = 2 + has_res + has_loss
    n_out = 3 if has_loss else 1

    def body(*refs):
        a_ref, b_ref = refs[0], refs[1]
        res_ref = refs[2] if has_res else None
        o_ref = refs[n_in]
        p = _dot(a_ref[...], b_ref[...], ((a_dims, b_dims), ((), ())))

        def finish(total):
            if has_res:
                total = total + res_ref[...].astype(F32)
            if not has_loss:
                o_ref[...] = total.astype(out_dtype)
                return
            err = total - refs[n_in - 1][...]
            grad = err * (1.0 / n)
            o_ref[...] = grad
            refs[n_in + 1][...] = grad.astype(BF16)
            l_ref = refs[n_in + 2]
            part = jnp.zeros((1, LANES), F32) + 0.5 * jnp.sum(err * err) * (1.0 / n)
            first = (pl.program_id(0) == 0) & (pl.program_id(1) == 0)

            @pl.when(first)
            def _():
                l_ref[...] = part

            @pl.when(jnp.logical_not(first))
            def _():
                l_ref[...] += part

        if nk == 1:
            finish(p)
        else:
            acc_ref = refs[n_in + n_out]
            q = pl.program_id(2)

            @pl.when(q == 0)
            def _():
                acc_ref[...] = p

            @pl.when(q > 0)
            def _():
                acc_ref[...] += p

            @pl.when(q == nk - 1)
            def _():
                finish(acc_ref[...])

    extra_in = ([res] if has_res else []) + ([loss_target] if has_loss else [])
    if has_loss:
        return _grid_call(
            body, name=name, grid=(nm, nn, nk), in_specs=[a_spec, b_spec] + [o_spec] * len(extra_in),
            out_specs=[o_spec, o_spec, pl.BlockSpec((1, LANES), lambda i, j, q: (0, 0))],
            out_shape=[jax.ShapeDtypeStruct((m, n), F32), jax.ShapeDtypeStruct((m, n), BF16),
                       jax.ShapeDtypeStruct((1, LANES), F32)],
            scratch_shapes=[pltpu.VMEM((tm, tn), F32)] if nk > 1 else [],
            args=[a, b] + extra_in, semantics=("arbitrary", "arbitrary", "arbitrary"))
    out, *carried = _grid_call(
        body, name=name, grid=(nm, nn, nk),
        in_specs=[a_spec, b_spec] + [o_spec] * len(extra_in),
        out_specs=[o_spec], out_shape=[jax.ShapeDtypeStruct((m, n), out_dtype)],
        scratch_shapes=[pltpu.VMEM((tm, tn), F32)] if nk > 1 else [],
        args=[a, b] + extra_in, semantics=("parallel", "parallel", "arbitrary"), comm=comm)
    return out if comm is None else (out, carried)


def _rms_fwd(x, w, name, tr=512, comm=None):
    t, d = x.shape
    tr = min(tr, t)

    def body(x_ref, w_ref, o_ref):
        xv = x_ref[...]
        r = lax.rsqrt(jnp.mean(xv * xv, axis=-1, keepdims=True) + EPS)
        o_ref[...] = (xv * r * w_ref[...]).astype(BF16)

    out, *carried = _grid_call(
        body, name=name, grid=(t // tr,),
        in_specs=[pl.BlockSpec((tr, d), lambda i: (i, 0)), pl.BlockSpec((1, d), lambda i: (0, 0))],
        out_specs=[pl.BlockSpec((tr, d), lambda i: (i, 0))],
        out_shape=[jax.ShapeDtypeStruct((t, d), BF16)], args=(x, w), semantics=("parallel",), comm=comm)
    return out if comm is None else (out, carried)


def _rms_bwd(x, w, dy, dres, name, tr=512):
    t, d = x.shape
    tr = min(tr, t)

    def body(x_ref, w_ref, dy_ref, dres_ref, dx_ref, dxb_ref, dw_ref):
        xv = x_ref[...]
        dyv = dy_ref[...].astype(F32)
        r = lax.rsqrt(jnp.mean(xv * xv, axis=-1, keepdims=True) + EPS)
        gy = dyv * w_ref[...]
        proj = jnp.sum(gy * xv, axis=-1, keepdims=True) * (1.0 / d)
        dx = dres_ref[...] + r * gy - xv * (r * r * r) * proj
        dx_ref[...] = dx
        dxb_ref[...] = dx.astype(BF16)
        part = jnp.sum(dyv * xv * r, axis=0, keepdims=True)

        @pl.when(pl.program_id(0) == 0)
        def _():
            dw_ref[...] = part

        @pl.when(pl.program_id(0) > 0)
        def _():
            dw_ref[...] += part

    row = pl.BlockSpec((tr, d), lambda i: (i, 0))
    vec = pl.BlockSpec((1, d), lambda i: (0, 0))
    return pl.pallas_call(
        body, name=name, grid=(t // tr,),
        in_specs=[row, vec, row, row], out_specs=[row, row, vec],
        out_shape=[jax.ShapeDtypeStruct((t, d), F32), jax.ShapeDtypeStruct((t, d), BF16),
                   jax.ShapeDtypeStruct((1, d), F32)],
        compiler_params=_cparams("arbitrary"),
    )(x, w, dy, dres)


def _adamw(parts, w, m, v, name, tr=128, transposed=False):
    r, c = w.shape
    tr = tr if r % tr == 0 else r
    c1 = 1.0 - ADAM_B1 ** ADAM_STEP
    c2 = 1.0 - ADAM_B2 ** ADAM_STEP

    def body(p_ref, w_ref, m_ref, v_ref, g_ref, d_ref, nm_ref, nv_ref):
        g = p_ref[0].astype(F32)
        for s in range(1, N_DEV):
            g = g + p_ref[s].astype(F32)
        if transposed:
            i, j = lax.broadcasted_iota(jnp.int32, (tr, tr), 0), lax.broadcasted_iota(jnp.int32, (tr, tr), 1)
            g = _dot_exact(jnp.where(i == j, 1.0, 0.0), g, NT, True)
        nm = ADAM_B1 * m_ref[...] + (1.0 - ADAM_B1) * g
        nv = ADAM_B2 * v_ref[...] + (1.0 - ADAM_B2) * (g * g)
        m_hat = nm / c1
        v_hat = nv / c2
        g_ref[...] = g
        d_ref[...] = -ADAM_LR * (m_hat / (jnp.sqrt(v_hat) + ADAM_EPS) + ADAM_WD * w_ref[...])
        nm_ref[...] = nm
        nv_ref[...] = nv

    blk = pl.BlockSpec((tr, c), lambda i: (i, 0))
    p_spec = (pl.BlockSpec((N_DEV, c, tr), lambda i: (0, 0, i)) if transposed
              else pl.BlockSpec((N_DEV, tr, c), lambda i: (0, i, 0)))
    return pl.pallas_call(
        body, name=name, grid=(r // tr,),
        in_specs=[p_spec, blk, blk, blk],
        out_specs=[blk] * 4, out_shape=[jax.ShapeDtypeStruct((r, c), F32)] * 4,
        compiler_params=_cparams("parallel"),
    )(parts, w, m, v)


ROW_TILE, ROW_HEADS = 512, 16
CONV_ROWS, CONV_HEADS = 512, 8


def _window(dest, inner, n_in, out_index):
    if dest is None:
        return inner, 0, [], [], {}
    buf, col0, total = dest
    if buf is None:
        return total, col0, [], [], {}
    return total, col0, [buf], [ANY], {n_in: out_index}


def _skip_ref(body, at, count):
    return body if count == 0 else (lambda *refs: body(*refs[:at], *refs[at + count:]))


def _heads_of(x, nh):
    return [x[:, h * HEAD_DIM:(h + 1) * HEAD_DIM] for h in range(nh)]


def _headnorm_fwd(proj, w, col0, inner, name, tr=ROW_TILE, hb=ROW_HEADS):
    t = proj.shape[0]
    tr = min(tr, t)
    hb = min(hb, inner // HEAD_DIM)
    wc = hb * HEAD_DIM
    c0 = col0 // wc

    def body(x_ref, w_ref, o_ref):
        outs = []
        for xh in _heads_of(x_ref[...], hb):
            r = lax.rsqrt(jnp.mean(xh * xh, axis=-1, keepdims=True) + EPS)
            outs.append((xh * r * w_ref[...]).astype(BF16))
        o_ref[...] = jnp.concatenate(outs, axis=1)

    return pl.pallas_call(
        body, name=name, grid=(t // tr, inner // wc),
        in_specs=[pl.BlockSpec((tr, wc), lambda i, j: (i, j + c0)), pl.BlockSpec((1, HEAD_DIM), lambda i, j: (0, 0))],
        out_specs=pl.BlockSpec((tr, wc), lambda i, j: (i, j)),
        out_shape=jax.ShapeDtypeStruct((t, inner), BF16),
        compiler_params=_cparams("parallel", "parallel"),
    )(proj, w)


def _headnorm_bwd(dy, proj, w, col0, inner, name, tr=ROW_TILE, hb=ROW_HEADS, dest=None):
    t = proj.shape[0]
    tr = min(tr, t)
    hb = min(hb, inner // HEAD_DIM)
    wc = hb * HEAD_DIM
    c0 = col0 // wc
    width, out0, more, more_specs, aliases = _window(dest, inner, 3, 0)

    def body(dy_ref, x_ref, w_ref, dx_ref, dw_ref):
        outs = []
        part = jnp.zeros((1, HEAD_DIM), F32)
        for dyh, xh in zip(_heads_of(dy_ref[...], hb), _heads_of(x_ref[...], hb)):
            r = lax.rsqrt(jnp.mean(xh * xh, axis=-1, keepdims=True) + EPS)
            gy = dyh * w_ref[...]
            pr = jnp.sum(gy * xh, axis=-1, keepdims=True) * (1.0 / HEAD_DIM)
            outs.append((r * gy - xh * (r * r * r) * pr).astype(BF16))
            part = part + jnp.sum(dyh * xh * r, axis=0, keepdims=True)
        dx_ref[...] = jnp.concatenate(outs, axis=1)
        first = (pl.program_id(0) == 0) & (pl.program_id(1) == 0)

        @pl.when(first)
        def _():
            dw_ref[...] = part

        @pl.when(jnp.logical_not(first))
        def _():
            dw_ref[...] += part

    blk = pl.BlockSpec((tr, wc), lambda i, j: (i, j))
    return pl.pallas_call(
        _skip_ref(body, 3, len(more)), name=name, grid=(t // tr, inner // wc),
        in_specs=[blk, pl.BlockSpec((tr, wc), lambda i, j: (i, j + c0)),
                  pl.BlockSpec((1, HEAD_DIM), lambda i, j: (0, 0))] + more_specs,
        out_specs=[pl.BlockSpec((tr, wc), lambda i, j: (i, j + out0 // wc)),
                   pl.BlockSpec((1, HEAD_DIM), lambda i, j: (0, 0))],
        out_shape=[jax.ShapeDtypeStruct((t, width), BF16), jax.ShapeDtypeStruct((1, HEAD_DIM), F32)],
        input_output_aliases=aliases, compiler_params=_cparams("arbitrary", "arbitrary"),
    )(dy, proj, w, *more)


def _gate_fwd(o, proj, zcol0, inner, name, norm_w=None, tr=ROW_TILE, hb=ROW_HEADS):
    t = o.shape[0]
    tr = min(tr, t)
    hb = min(hb, inner // HEAD_DIM)
    wc = hb * HEAD_DIM
    c0 = zcol0 // wc
    has_w = norm_w is not None

    def body(*refs):
        o_ref, z_ref = refs[0], refs[1]
        out_ref = refs[2 + has_w]
        outs = []
        for oh, zh in zip(_heads_of(o_ref[...], hb), _heads_of(z_ref[...], hb)):
            if has_w:
                r = lax.rsqrt(jnp.mean(oh * oh, axis=-1, keepdims=True) + EPS)
                oh = oh * r * refs[2][...]
            outs.append((oh * _silu(zh)).astype(BF16))
        out_ref[...] = jnp.concatenate(outs, axis=1)

    blk = pl.BlockSpec((tr, wc), lambda i, j: (i, j))
    vec = pl.BlockSpec((1, HEAD_DIM), lambda i, j: (0, 0))
    return pl.pallas_call(
        body, name=name, grid=(t // tr, inner // wc),
        in_specs=[blk, pl.BlockSpec((tr, wc), lambda i, j: (i, j + c0))] + ([vec] if has_w else []),
        out_specs=blk, out_shape=jax.ShapeDtypeStruct((t, inner), BF16),
        compiler_params=_cparams("parallel", "parallel"),
    )(*([o, proj] + ([norm_w] if has_w else [])))


def _gate_bwd(dg, o, proj, zcol0, inner, name, do_dtype, norm_w=None, tr=ROW_TILE, hb=ROW_HEADS, dest=None):
    t = o.shape[0]
    tr = min(tr, t)
    hb = min(hb, inner // HEAD_DIM)
    wc = hb * HEAD_DIM
    c0 = zcol0 // wc
    has_w = norm_w is not None
    width, out0, more, more_specs, aliases = _window(dest, inner, 3 + has_w, 1)

    def body(*refs):
        dg_ref, o_ref, z_ref = refs[0], refs[1], refs[2]
        do_ref, dz_ref = refs[3 + has_w], refs[4 + has_w]
        dos, dzs = [], []
        part = jnp.zeros((1, HEAD_DIM), F32)
        for dgh, oh, zh in zip(_heads_of(dg_ref[...], hb), _heads_of(o_ref[...], hb), _heads_of(z_ref[...], hb)):
            dy = dgh * _silu(zh)
            if has_w:
                w = refs[3][...]
                r = lax.rsqrt(jnp.mean(oh * oh, axis=-1, keepdims=True) + EPS)
                on = oh * r
                dzs.append((dgh * on * w * _dsilu(zh)).astype(BF16))
                gy = dy * w
                pr = jnp.sum(gy * oh, axis=-1, keepdims=True) * (1.0 / HEAD_DIM)
                dos.append((r * gy - oh * (r * r * r) * pr).astype(do_dtype))
                part = part + jnp.sum(dy * on, axis=0, keepdims=True)
            else:
                dzs.append((dgh * oh * _dsilu(zh)).astype(BF16))
                dos.append(dy.astype(do_dtype))
        do_ref[...] = jnp.concatenate(dos, axis=1)
        dz_ref[...] = jnp.concatenate(dzs, axis=1)
        if has_w:
            dw_ref = refs[6]
            first = (pl.program_id(0) == 0) & (pl.program_id(1) == 0)

            @pl.when(first)
            def _():
                dw_ref[...] = part

            @pl.when(jnp.logical_not(first))
            def _():
                dw_ref[...] += part

    blk = pl.BlockSpec((tr, wc), lambda i, j: (i, j))
    vec = pl.BlockSpec((1, HEAD_DIM), lambda i, j: (0, 0))
    return pl.pallas_call(
        _skip_ref(body, 3 + has_w, len(more)), name=name, grid=(t // tr, inner // wc),
        in_specs=[blk, blk, pl.BlockSpec((tr, wc), lambda i, j: (i, j + c0))] + ([vec] if has_w else []) + more_specs,
        out_specs=[blk, pl.BlockSpec((tr, wc), lambda i, j: (i, j + out0 // wc))] + ([vec] if has_w else []),
        out_shape=[jax.ShapeDtypeStruct((t, inner), do_dtype), jax.ShapeDtypeStruct((t, width), BF16)]
        + ([jax.ShapeDtypeStruct((1, HEAD_DIM), F32)] if has_w else []),
        input_output_aliases=aliases, compiler_params=_cparams("arbitrary", "arbitrary"),
    )(*([dg, o, proj] + ([norm_w] if has_w else []) + more))


N_REL = 2 * REL_CLIP + 1
REL_PAD = 640
WIN = 2 * Q_TILE


def _diag_onehot():
    i = lax.broadcasted_iota(jnp.int32, (REL_PAD, WIN), 0)
    j = lax.broadcasted_iota(jnp.int32, (REL_PAD, WIN), 1)
    rel = jnp.where(j < Q_TILE + CHUNK, Q_TILE - j, Q_TILE + WIN - j)
    used = (j < Q_TILE + CHUNK) | (j > WIN - CHUNK)
    idx = jnp.clip(rel, -REL_CLIP, REL_CLIP) + REL_CLIP
    return jnp.where(used & (i == idx), 1.0, 0.0).astype(F32)


def _band_mask():
    r = lax.broadcasted_iota(jnp.int32, (Q_TILE, WIN), 0) // CHUNK
    kc = lax.broadcasted_iota(jnp.int32, (Q_TILE, WIN), 1) // CHUNK - LEFT_CHUNKS
    return (kc <= r) & (kc >= r - LEFT_CHUNKS)


def _bias_tiles(rel_bias_pad, name):
    nh = rel_bias_pad.shape[0]

    def body(rb_ref, o_ref):
        dvec = _nn(rb_ref[...], _diag_onehot(), HIGHEST)[0:1, :]
        tile = pltpu.roll(jnp.broadcast_to(dvec, (Q_TILE, WIN)), 0, 1, stride=1, stride_axis=0)
        o_ref[...] = jnp.where(_band_mask(), tile, NEG_BIG)

    return pl.pallas_call(
        body, name=name, grid=(nh,),
        in_specs=[pl.BlockSpec((None, 8, REL_PAD), lambda h: (h, 0, 0))],
        out_specs=pl.BlockSpec((None, Q_TILE, WIN), lambda h: (h, 0, 0)),
        out_shape=jax.ShapeDtypeStruct((nh, Q_TILE, WIN), F32),
        compiler_params=_cparams("parallel"),
    )(rel_bias_pad)


def _bias_grad(dtile, name):
    nh = dtile.shape[0]

    def body(d_ref, o_ref):
        ri = lax.broadcasted_iota(jnp.int32, (Q_TILE, Q_TILE), 0)
        ci = lax.broadcasted_iota(jnp.int32, (Q_TILE, Q_TILE), 1)
        flip = jnp.where(ri + ci == Q_TILE - 1, 1.0, 0.0).astype(F32)
        rev = _dot_exact(flip, d_ref[...], NN, True)
        rolled = pltpu.roll(rev, WIN - (Q_TILE - 1), 1, stride=1, stride_axis=0)
        diag = jnp.broadcast_to(jnp.sum(rolled, axis=0, keepdims=True), (8, WIN))
        o_ref[...] = _nt(diag, _diag_onehot(), HIGHEST)

    return pl.pallas_call(
        body, name=name, grid=(nh,),
        in_specs=[pl.BlockSpec((None, Q_TILE, WIN), lambda h: (h, 0, 0))],
        out_specs=pl.BlockSpec((None, 8, REL_PAD), lambda h: (h, 0, 0)),
        out_shape=jax.ShapeDtypeStruct((nh, 8, REL_PAD), F32),
        compiler_params=_cparams("parallel"),
    )(dtile)


GROUP = 2 * CHUNK
BAND = Q_TILE + GROUP


N_GROUPS = Q_TILE // GROUP
ATTN_HB = 2
ATTN_HB_FWD = 4


def _head_cols(j):
    return slice(j * HEAD_DIM, (j + 1) * HEAD_DIM)


def _groups(ref, units):
    return jnp.stack([ref[GROUP * g:GROUP * (g + 1), _head_cols(j)] for j, g in units])


def _bands(r0_ref, r1_ref, units):
    return jnp.stack([jnp.concatenate([r0_ref[GROUP * g:, _head_cols(j)], r1_ref[:GROUP * (g + 1), _head_cols(j)]],
                                      axis=0) for j, g in units])


def _group_probs(q, kw, b_ref, units, first_tile):
    bias = jnp.stack([b_ref[j, GROUP * g:GROUP * (g + 1), GROUP * g:GROUP * g + BAND] for j, g in units])
    s = _dot(q, kw, BNT) * (HEAD_DIM ** -0.5) + bias
    col = jnp.stack([lax.broadcasted_iota(jnp.int32, (GROUP, BAND), 1) + GROUP * g for _, g in units])
    s = jnp.where(first_tile & (col < Q_TILE), NEG_BIG, s)
    p = jnp.exp(s - jnp.max(s, axis=-1, keepdims=True))
    return p * (1.0 / jnp.sum(p, axis=-1, keepdims=True))


def _attn_fwd(q, k, v, v_col0, bias, name):
    t, inner = q.shape
    nh, nt = inner // HEAD_DIM, t // Q_TILE
    hb = min(ATTN_HB_FWD, nh)
    wc = hb * HEAD_DIM
    vh = v_col0 // wc
    units = [(j, g) for j in range(hb) for g in range(N_GROUPS)]

    def body(q_ref, k0_ref, k1_ref, v0_ref, v1_ref, b_ref, o_ref):
        p = _group_probs(_groups(q_ref, units), _bands(k0_ref, k1_ref, units), b_ref, units, pl.program_id(1) == 0)
        o = _dot(_bf(p), _bf(_bands(v0_ref, v1_ref, units)), BNN)
        for n, (j, g) in enumerate(units):
            o_ref[GROUP * g:GROUP * (g + 1), _head_cols(j)] = o[n]

    cur = pl.BlockSpec((Q_TILE, wc), lambda h, i: (i, h))
    prev = pl.BlockSpec((Q_TILE, wc), lambda h, i: (jnp.maximum(i - 1, 0), h))
    v_cur = pl.BlockSpec((Q_TILE, wc), lambda h, i: (i, h + vh))
    v_prev = pl.BlockSpec((Q_TILE, wc), lambda h, i: (jnp.maximum(i - 1, 0), h + vh))
    return pl.pallas_call(
        body, name=name, grid=(nh // hb, nt),
        in_specs=[cur, prev, cur, v_prev, v_cur, pl.BlockSpec((hb, Q_TILE, WIN), lambda h, i: (h, 0, 0))],
        out_specs=cur, out_shape=jax.ShapeDtypeStruct((t, inner), F32),
        compiler_params=_cparams("parallel", "parallel"),
    )(q, k, k, v, v, bias)


def _attn_bwd(q, k, v, v_col0, do, bias, name, dest=None):
    t, inner = q.shape
    nh, nt = inner // HEAD_DIM, t // Q_TILE
    scale = HEAD_DIM ** -0.5
    hb = min(ATTN_HB, nh)
    wc = hb * HEAD_DIM
    units = [(j, g) for j in range(hb) for g in range(N_GROUPS)]

    def body(q_ref, k0_ref, k1_ref, v0_ref, v1_ref, do_ref, b_ref, dq_ref, dk_ref, dv_ref, db_ref,
             ck_ref, cv_ref, wk_ref, wv_ref):
        i = pl.program_id(1)

        @pl.when(i == 0)
        def _():
            ck_ref[...] = jnp.zeros(blk, F32)
            cv_ref[...] = jnp.zeros(blk, F32)
            db_ref[...] = jnp.zeros((hb, Q_TILE, WIN), F32)

        @pl.when(i < nt)
        def _():
            qv, dov = _groups(q_ref, units), _groups(do_ref, units)
            kw, vw = _bands(k0_ref, k1_ref, units), _bf(_bands(v0_ref, v1_ref, units))
            p = _group_probs(qv, kw, b_ref, units, i == 0)
            dp = _dot(dov, vw, BNT)
            ds = p * (dp - jnp.sum(p * dp, axis=-1, keepdims=True))
            pb, dsb = _bf(p), _bf(ds)
            dq = _dot(dsb, kw, BNN) * scale
            dkw = _dot(dsb, qv, BTN) * scale
            dvw = _dot(pb, dov, BTN)
            for n, (j, g) in enumerate(units):
                rows, cols = slice(GROUP * g, GROUP * (g + 1)), slice(GROUP * g, GROUP * g + BAND)
                db_ref[j, rows, cols] += ds[n]
                dq_ref[rows, _head_cols(j)] = dq[n]
                for b in range(BAND // GROUP):
                    blk_rows, part = slice(GROUP * (g + b), GROUP * (g + b + 1)), slice(GROUP * b, GROUP * (b + 1))
                    if g == 0 or b == BAND // GROUP - 1:
                        wk_ref[blk_rows, _head_cols(j)] = dkw[n][part]
                        wv_ref[blk_rows, _head_cols(j)] = dvw[n][part]
                    else:
                        wk_ref[blk_rows, _head_cols(j)] += dkw[n][part]
                        wv_ref[blk_rows, _head_cols(j)] += dvw[n][part]
            dk_ref[...] = ck_ref[...] + wk_ref[:Q_TILE, :]
            dv_ref[...] = (cv_ref[...] + wv_ref[:Q_TILE, :]).astype(BF16)
            ck_ref[...] = wk_ref[Q_TILE:, :]
            cv_ref[...] = wv_ref[Q_TILE:, :]

        @pl.when(i == nt)
        def _():
            dk_ref[...] = ck_ref[...]
            dv_ref[...] = cv_ref[...].astype(BF16)

    blk = (Q_TILE, wc)
    cur = pl.BlockSpec(blk, lambda h, i: (jnp.minimum(i, nt - 1), h))
    prev = pl.BlockSpec(blk, lambda h, i: (jnp.clip(i - 1, 0, nt - 1), h))
    lag = pl.BlockSpec(blk, lambda h, i: (jnp.maximum(i - 1, 0), h))
    vh = v_col0 // wc
    v_cur = pl.BlockSpec(blk, lambda h, i: (jnp.minimum(i, nt - 1), h + vh))
    v_prev = pl.BlockSpec(blk, lambda h, i: (jnp.clip(i - 1, 0, nt - 1), h + vh))
    tile = pl.BlockSpec((hb, Q_TILE, WIN), lambda h, i: (h, 0, 0))
    width, out0, more, more_specs, aliases = _window(dest, inner, 7, 2)
    return pl.pallas_call(
        _skip_ref(body, 7, len(more)), name=name, grid=(nh // hb, nt + 1),
        in_specs=[cur, prev, cur, v_prev, v_cur, cur, tile] + more_specs,
        out_specs=[cur, lag, pl.BlockSpec(blk, lambda h, i: (jnp.maximum(i - 1, 0), h + out0 // wc)), tile],
        out_shape=[jax.ShapeDtypeStruct((t, inner), F32)] * 2 + [jax.ShapeDtypeStruct((t, width), BF16),
                                                                 jax.ShapeDtypeStruct((nh, Q_TILE, WIN), F32)],
        scratch_shapes=[pltpu.VMEM(blk, F32), pltpu.VMEM(blk, F32),
                        pltpu.VMEM((WIN, wc), F32), pltpu.VMEM((WIN, wc), F32)],
        input_output_aliases=aliases, compiler_params=_cparams("arbitrary", "arbitrary"),
    )(q, k, k, v, v, do, bias, *more)


def _pad_rel_bias(rel_bias):
    nh = rel_bias.shape[0]
    return jnp.broadcast_to(jnp.pad(rel_bias, ((0, 0), (0, REL_PAD - N_REL)))[:, None, :], (nh, 8, REL_PAD))


def _layer_b_fwd(h1, nw, w_in_t, qw, kw, bias, w_out, target):
    t, d = h1.shape
    inner = w_out.shape[0]
    hn = _rms_fwd(h1, nw, "b_rms")
    proj = _mm(hn, w_in_t, "nt", t, 4 * inner, d, out_dtype=F32, name="b_proj")
    qn = _headnorm_fwd(proj, qw, 0, inner, "b_qnorm")
    kn = _headnorm_fwd(proj, kw, inner, inner, "b_knorm")
    o = _attn_fwd(qn, kn, proj, 2 * inner, bias, "b_attn")
    g = _gate_fwd(o, proj, 3 * inner, inner, "b_gate")
    loss_parts = _mm(g, w_out, "nn", t, d, inner, out_dtype=F32, name="b_out", res=h1, loss_target=target)
    return loss_parts, (hn, proj, qn, kn, o, g)


def _layer_b_bwd(dh2, dh2b, h1, nw, w_in_t, qw, kw, bias, w_out, saved):
    hn, proj, qn, kn, o, g = saved
    t, d = h1.shape
    inner = w_out.shape[0]
    dg = _mm(dh2b, w_out, "nt", t, inner, d, out_dtype=F32, name="b_dgate")
    dw_out = _mm(g, dh2b, "tn", inner, d, t, out_dtype=BF16, name="b_dwout")
    do, dproj = _gate_bwd(dg, o, proj, 3 * inner, inner, "b_gate_bwd", BF16, dest=(None, 3 * inner, 4 * inner))
    dq, dk, dproj, dtile = _attn_bwd(qn, kn, proj, 2 * inner, do, bias, "b_attn_bwd",
                                     dest=(dproj, 2 * inner, 4 * inner))
    dproj, dqw = _headnorm_bwd(dq, proj, qw, 0, inner, "b_qnorm_bwd", dest=(dproj, 0, 4 * inner))
    dproj, dkw = _headnorm_bwd(dk, proj, kw, inner, inner, "b_knorm_bwd", dest=(dproj, inner, 4 * inner))
    dhn = _mm(dproj, w_in_t, "nn", t, d, 4 * inner, out_dtype=F32, name="b_dhn", **LONG_K_TILES)
    dw_in_t = _mm(dproj, hn, "tn", 4 * inner, d, t, out_dtype=BF16, name="b_dwin", **LONG_K_TILES)
    dh1, dh1b, dnw = _rms_bwd(h1, nw, dhn, dh2, "b_rms_bwd")
    drb = _bias_grad(dtile, "b_bias_grad")[:, 0, :N_REL]
    return dh1, dh1b, dnw, dw_in_t, dqw, dkw, drb, dw_out


LANES = 128


def _softplus(x):
    return jnp.maximum(x, 0.0) + jnp.log1p(jnp.exp(-jnp.abs(x)))


def _gates_fwd(ab, alog_row, dt_row, nh, name, tr=1024):
    t = ab.shape[0]
    tr = min(tr, t)

    def body(x_ref, al_ref, dt_ref, o_ref):
        x = x_ref[...]
        lane = lax.broadcasted_iota(jnp.int32, x.shape, 1)
        g = -jnp.exp(al_ref[...]) * _softplus(x + dt_ref[...])
        o_ref[...] = jnp.where(lane < nh, g, jnp.where(lane < 2 * nh, _sigmoid(x), 0.0))

    row = pl.BlockSpec((tr, LANES), lambda i: (i, 0))
    vec = pl.BlockSpec((1, LANES), lambda i: (0, 0))
    return pl.pallas_call(
        body, name=name, grid=(t // tr,), in_specs=[row, vec, vec], out_specs=row,
        out_shape=jax.ShapeDtypeStruct((t, LANES), F32), compiler_params=_cparams("parallel"),
    )(ab, alog_row, dt_row)


def _gates_bwd(ab, alog_row, dt_row, dgates, nh, name, tr=1024):
    t = ab.shape[0]
    tr = min(tr, t)
    npart = dgates.shape[0]

    def body(x_ref, al_ref, dt_ref, dg_ref, dx_ref, s_ref):
        x = x_ref[...]
        lane = lax.broadcasted_iota(jnp.int32, x.shape, 1)
        dgt = dg_ref[0]
        for p in range(1, npart):
            dgt = dgt + dg_ref[p]
        ea = jnp.exp(al_ref[...])
        xa = x + dt_ref[...]
        da = jnp.where(lane < nh, dgt * (-ea) * _sigmoid(xa), 0.0)
        beta = _sigmoid(x)
        db = jnp.where((lane >= nh) & (lane < 2 * nh), dgt * beta * (1.0 - beta), 0.0)
        dx_ref[...] = (da + db).astype(BF16)
        dal = jnp.sum(jnp.where(lane < nh, dgt * (-ea) * _softplus(xa), 0.0), axis=0, keepdims=True)
        ddt = jnp.sum(da, axis=0, keepdims=True)
        r8 = lax.broadcasted_iota(jnp.int32, (8, LANES), 0)
        part = jnp.where(r8 == 0, dal, jnp.where(r8 == 1, ddt, 0.0))

        @pl.when(pl.program_id(0) == 0)
        def _():
            s_ref[...] = part

        @pl.when(pl.program_id(0) > 0)
        def _():
            s_ref[...] += part

    row = pl.BlockSpec((tr, LANES), lambda i: (i, 0))
    vec = pl.BlockSpec((1, LANES), lambda i: (0, 0))
    return pl.pallas_call(
        body, name=name, grid=(t // tr,),
        in_specs=[row, vec, vec, pl.BlockSpec((npart, tr, LANES), lambda i: (0, i, 0))],
        out_specs=[row, pl.BlockSpec((8, LANES), lambda i: (0, 0))],
        out_shape=[jax.ShapeDtypeStruct((t, LANES), BF16), jax.ShapeDtypeStruct((8, LANES), F32)],
        compiler_params=_cparams("arbitrary"),
    )(ab, alog_row, dt_row, dgates)


HALO = 8


def _delayed(ext, rows):
    return [ext[HALO:HALO + rows]] + [pltpu.roll(ext, s, 0)[HALO:HALO + rows] for s in range(1, CONV_K)]


def _conv_taps(delayed, w):
    acc = delayed[0] * w[CONV_K - 1:CONV_K]
    for s in range(1, CONV_K):
        acc = acc + delayed[s] * w[CONV_K - 1 - s:CONV_K - s]
    return acc


def _conv_fwd(proj, conv_w, col0, inner, mode, name, tt=CONV_ROWS, hb=CONV_HEADS):
    t = proj.shape[0]
    tt = min(tt, t)
    hb = min(hb, inner // HEAD_DIM)
    wc = hb * HEAD_DIM
    c0 = col0 // wc
    hpb = tt // HALO

    def body(x_ref, halo_ref, w_ref, o_ref):
        halo = jnp.where(pl.program_id(1) == 0, 0.0, halo_ref[...])
        s = _silu(_conv_taps(_delayed(jnp.concatenate([halo, x_ref[...]], axis=0), tt), w_ref[...]))
        if mode == "v":
            o_ref[...] = s
        else:
            mul = HEAD_DIM ** -0.5 if mode == "q" else 1.0
            o_ref[...] = jnp.concatenate(
                [sh * (lax.rsqrt(jnp.sum(sh * sh, axis=-1, keepdims=True) + EPS) * mul) for sh in _heads_of(s, hb)], axis=1)

    return pl.pallas_call(
        body, name=name, grid=(inner // wc, t // tt),
        in_specs=[pl.BlockSpec((tt, wc), lambda j, i: (i, j + c0)),
                  pl.BlockSpec((HALO, wc), lambda j, i: (jnp.maximum(i * hpb - 1, 0), j + c0)),
                  pl.BlockSpec((CONV_K, wc), lambda j, i: (0, j + c0))],
        out_specs=pl.BlockSpec((tt, wc), lambda j, i: (i, j)),
        out_shape=jax.ShapeDtypeStruct((t, inner), F32),
        compiler_params=_cparams("parallel", "parallel"),
    )(proj, proj, conv_w)


def _conv_bwd(dy, proj, conv_w, col0, inner, mode, name, tt=CONV_ROWS, hb=CONV_HEADS, dest=None):
    t = proj.shape[0]
    tt = min(tt, t)
    nt = t // tt
    hb = min(hb, inner // HEAD_DIM)
    wc = hb * HEAD_DIM
    c0 = col0 // wc
    hpb = tt // HALO
    rows = tt + HALO

    def body(dy_ref, dyn_ref, x_ref, xp_ref, xn_ref, w_ref, dx_ref, dw_ref):
        i = pl.program_id(1)
        w = w_ref[...]
        xprev = jnp.where(i == 0, 0.0, xp_ref[...])
        delayed = _delayed(jnp.concatenate([xprev, x_ref[...], xn_ref[...]], axis=0), rows)
        c = _conv_taps(delayed, w)
        dyv = jnp.concatenate([dy_ref[...], jnp.where(i == nt - 1, 0.0, dyn_ref[...])], axis=0)
        sg = _sigmoid(c)
        s = c * sg
        if mode == "v":
            ds = dyv
        else:
            mul = HEAD_DIM ** -0.5 if mode == "q" else 1.0
            parts = []
            for dyh, sh in zip(_heads_of(dyv, hb), _heads_of(s, hb)):
                r = lax.rsqrt(jnp.sum(sh * sh, axis=-1, keepdims=True) + EPS)
                parts.append(mul * (r * dyh - sh * (r * r * r) * jnp.sum(dyh * sh, axis=-1, keepdims=True)))
            ds = jnp.concatenate(parts, axis=1)
        dc = ds * (sg * (1.0 + c * (1.0 - sg)))
        dx = dc[:tt] * w[CONV_K - 1:CONV_K]
        for sft in range(1, CONV_K):
            dx = dx + pltpu.roll(dc, rows - sft, 0)[:tt] * w[CONV_K - 1 - sft:CONV_K - sft]
        dx_ref[...] = dx.astype(BF16)
        r8 = lax.broadcasted_iota(jnp.int32, (8, wc), 0)
        part = jnp.zeros((8, wc), F32)
        for sft in range(CONV_K):
            part = part + jnp.where(r8 == CONV_K - 1 - sft,
                                    jnp.sum(dc[:tt] * delayed[sft][:tt], axis=0, keepdims=True), 0.0)

        @pl.when(i == 0)
        def _():
            dw_ref[...] = part

        @pl.when(i > 0)
        def _():
            dw_ref[...] += part

    cur = lambda off: pl.BlockSpec((tt, wc), lambda j, i: (i, j + off))
    nxt = lambda off: pl.BlockSpec((HALO, wc), lambda j, i: (jnp.minimum((i + 1) * hpb, t // HALO - 1), j + off))
    width, out0, more, more_specs, aliases = _window(dest, inner, 6, 0)
    return pl.pallas_call(
        _skip_ref(body, 6, len(more)), name=name, grid=(inner // wc, nt),
        in_specs=[cur(0), nxt(0), cur(c0),
                  pl.BlockSpec((HALO, wc), lambda j, i: (jnp.maximum(i * hpb - 1, 0), j + c0)), nxt(c0),
                  pl.BlockSpec((CONV_K, wc), lambda j, i: (0, j + c0))] + more_specs,
        out_specs=[pl.BlockSpec((tt, wc), lambda j, i: (i, j + out0 // wc)),
                   pl.BlockSpec((8, wc), lambda j, i: (0, j))],
        out_shape=[jax.ShapeDtypeStruct((t, width), BF16), jax.ShapeDtypeStruct((8, inner), F32)],
        input_output_aliases=aliases, compiler_params=_cparams("parallel", "arbitrary"),
    )(dy, dy, proj, proj, proj, conv_w, *more)


GDN_HB = 4
GDN_NB = 8
SCAN_HB = 16
SCAN_NB = 4


def _iota2(n, m):
    return lax.broadcasted_iota(jnp.int32, (n, m), 0), lax.broadcasted_iota(jnp.int32, (n, m), 1)


def _head_select(first_head, hb, lane0):
    r, lane = _iota2(8, LANES)
    return jnp.where((r < hb) & (lane == lane0 + first_head + r), 1.0, 0.0).astype(F32)


def _chunk_gates(gt, selg, selb):
    i, j = _iota2(CHUNK, CHUNK)
    gc_all = _dot_exact(jnp.where(j <= i, 1.0, 0.0), gt, NN, True)
    return (_dot_exact(gc_all, selg, NT, False), _dot_exact(selg, gc_all, NT, True),
            _dot_exact(gt, selb, NT, False))


def _decay_terms(gcol, grow):
    i, j = _iota2(CHUNK, CHUNK)
    glast = gcol[:, CHUNK - 1:CHUNK, :]
    decay = jnp.exp(jnp.where(j <= i, gcol - grow, NEG_BIG))
    return jnp.exp(gcol), jnp.exp(glast - gcol), jnp.exp(glast), decay


def _unit_lower_inverse(a):
    i, j = _iota2(CHUNK, CHUNK)
    same16 = (i // 16) == (j // 16)
    same32 = (i // 32) == (j // 32)
    m = jnp.where(same16, -a, 0.0)
    x = jnp.where(i == j, 1.0, 0.0) + m
    for _ in range(3):
        m = _dot3(m, m, BNN)
        x = x + _dot3(x, m, BNN)
    for off in (jnp.where(same32 & jnp.logical_not(same16), a, 0.0), jnp.where(same32, 0.0, a)):
        x = x - _dot3(_dot3(x, off, BNN), x, BNN)
    return x


def _unit_inputs(refs, g_ref, selg, selb, hb, nb):
    units = [(c, h) for c in range(nb) for h in range(hb)]
    rs = lambda c: slice(c * CHUNK, (c + 1) * CHUNK)
    cs = lambda h: slice(h * HEAD_DIM, (h + 1) * HEAD_DIM)
    gates = [_chunk_gates(g_ref[rs(c), :], selg, selb) for c in range(nb)]
    stacked = [jnp.stack([r[rs(c), cs(h)] for c, h in units]) for r in refs]
    gcol = jnp.stack([gates[c][0][:, h:h + 1] for c, h in units])
    grow = jnp.stack([gates[c][1][h:h + 1, :] for c, h in units])
    bcol = jnp.stack([gates[c][2][:, h:h + 1] for c, h in units])
    return units, rs, cs, stacked, gcol, grow, bcol


def _gdn_specs(nh, inner, t, heads=GDN_HB, chunks=GDN_NB):
    hb, nb = min(heads, nh), chunks
    rows = nb * CHUNK
    wide = pl.BlockSpec((rows, hb * HEAD_DIM), lambda g, n: (n, g))
    sq = pl.BlockSpec((hb, rows, CHUNK), lambda g, n: (g, n, 0))
    gts = pl.BlockSpec((rows, LANES), lambda g, n: (n, 0))
    glb = pl.BlockSpec((nb * 8, hb * HEAD_DIM), lambda g, n: (n, g))
    return hb, nb, rows, wide, sq, gts, glb


def _gdn_intra_fwd(q, k, v, gates, nh, name, comm=None):
    t, inner = q.shape
    hb, nb, rows, wide, sq, gts, glb = _gdn_specs(nh, inner, t)

    def body(q_ref, k_ref, v_ref, g_ref, qe_ref, kel_ref, wb_ref, w_ref, u_ref, qk_ref, tm_ref, gl_ref):
        first = pl.program_id(0) * hb
        selg, selb = _head_select(first, hb, 0), _head_select(first, hb, nh)
        i, j = _iota2(CHUNK, CHUNK)
        units, rs, cs, (qv, kv, vv), gcol, grow, bcol = _unit_inputs(
            (q_ref, k_ref, v_ref), g_ref, selg, selb, hb, nb)
        e, el, gl, decay = _decay_terms(gcol, grow)
        kb = kv * bcol
        qbf, kbf = _bf(qv), _bf(kv)
        a = jnp.where(j < i, _dot(_bf(kb), kbf, BNT) * decay, 0.0)
        tm = _unit_lower_inverse(a)
        uw = _dot3(tm, jnp.concatenate([vv * bcol, kb * e], axis=2), BNN)
        qk = _bf(_dot(qbf, kbf, BNT) * decay)
        qe, kel = _bf(qv * e), _bf(kv * el)
        for n, (c, h) in enumerate(units):
            w = uw[n, :, HEAD_DIM:]
            qe_ref[rs(c), cs(h)] = qe[n]
            kel_ref[rs(c), cs(h)] = kel[n]
            wb_ref[rs(c), cs(h)] = _bf(w)
            w_ref[rs(c), cs(h)] = w
            u_ref[rs(c), cs(h)] = uw[n, :, :HEAD_DIM]
            qk_ref[h, rs(c), :] = qk[n]
            tm_ref[h, rs(c), :] = tm[n]
            gl_ref[c * 8:(c + 1) * 8, cs(h)] = jnp.broadcast_to(gl[n], (8, HEAD_DIM))

    big = lambda dt: jax.ShapeDtypeStruct((t, inner), dt)
    return _grid_call(
        body, name=name, grid=(nh // hb, t // rows),
        in_specs=[wide, wide, wide, gts],
        out_specs=[wide] * 5 + [sq, sq, glb],
        out_shape=[big(BF16), big(BF16), big(BF16), big(F32), big(F32),
                   jax.ShapeDtypeStruct((nh, t, CHUNK), BF16), jax.ShapeDtypeStruct((nh, t, CHUNK), F32),
                   jax.ShapeDtypeStruct((t // CHUNK * 8, inner), F32)],
        args=(q, k, v, gates), semantics=("parallel", "parallel"), comm=comm)


def _gdn_scan_fwd(qe, kel, wb, u, qk, glb, nh, name):
    t, inner = u.shape
    hb, nb, rows, wide, sq, _, glb_spec = _gdn_specs(nh, inner, t, SCAN_HB, SCAN_NB)

    def body(qe_ref, kel_ref, wb_ref, u_ref, qk_ref, gl_ref, o_ref, vn_ref, sall_ref, s_ref):
        @pl.when(pl.program_id(1) == 0)
        def _():
            s_ref[...] = jnp.zeros(s_ref.shape, F32)

        cs = lambda h: slice(h * HEAD_DIM, (h + 1) * HEAD_DIM)
        for c in range(nb):
            rs = slice(c * CHUNK, (c + 1) * CHUNK)
            heads = lambda ref: jnp.stack([ref[rs, cs(h)] for h in range(hb)])
            s = s_ref[...]
            if c == 0:
                sall_ref[...] = s
            sb = _bf(s)
            vn = heads(u_ref) - _dot(heads(wb_ref), sb, BNN)
            vnb = _bf(vn)
            o = _dot(heads(qe_ref), sb, BNN) + _dot(qk_ref[:, rs, :], vnb, BNN)
            gl = jnp.stack([gl_ref[c * 8:c * 8 + 1, cs(h)] for h in range(hb)])
            s_ref[...] = s * gl + _dot(heads(kel_ref), vnb, BTN)
            for h in range(hb):
                vn_ref[rs, cs(h)] = vnb[h]
                o_ref[rs, cs(h)] = o[h]

    return pl.pallas_call(
        body, name=name, grid=(nh // hb, t // rows),
        in_specs=[wide, wide, wide, wide, sq, glb_spec],
        out_specs=[wide, wide, pl.BlockSpec((None, hb, HEAD_DIM, HEAD_DIM), lambda g, n: (n, g, 0, 0))],
        out_shape=[jax.ShapeDtypeStruct((t, inner), F32), jax.ShapeDtypeStruct((t, inner), BF16),
                   jax.ShapeDtypeStruct((t // rows, nh, HEAD_DIM, HEAD_DIM), F32)],
        scratch_shapes=[pltpu.VMEM((hb, HEAD_DIM, HEAD_DIM), F32)],
        compiler_params=_cparams("parallel", "arbitrary"),
    )(qe, kel, wb, u, qk, glb)


def _gdn_scan_bwd(do, qe, kel, wb, vn, qk, glb, sall, nh, name, comm=None):
    t, inner = do.shape
    hb, nb, rows, _, _, _, _ = _gdn_specs(nh, inner, t, SCAN_HB, SCAN_NB)
    last = t // rows - 1
    wide = pl.BlockSpec((rows, hb * HEAD_DIM), lambda g, n: (last - n, g))
    sq = pl.BlockSpec((hb, rows, CHUNK), lambda g, n: (g, last - n, 0))
    glb_spec = pl.BlockSpec((nb * 8, hb * HEAD_DIM), lambda g, n: (last - n, g))

    def body(do_ref, qe_ref, kel_ref, wb_ref, vn_ref, qk_ref, gl_ref, sall_ref,
             dvn_ref, dw_ref, dqe_ref, dkel_ref, dqk_ref, dgl_ref, ds_ref):
        @pl.when(pl.program_id(1) == 0)
        def _():
            ds_ref[...] = jnp.zeros(ds_ref.shape, F32)

        cs = lambda h: slice(h * HEAD_DIM, (h + 1) * HEAD_DIM)
        chunk = lambda ref, c: jnp.stack([ref[c * CHUNK:(c + 1) * CHUNK, cs(h)] for h in range(hb)])
        decay = lambda c: jnp.stack([gl_ref[c * 8:c * 8 + 1, cs(h)] for h in range(hb)])
        states = [sall_ref[...]]
        for c in range(nb - 1):
            states.append(states[c] * decay(c) + _dot(chunk(kel_ref, c), _bf(chunk(vn_ref, c)), BTN))
        for c in reversed(range(nb)):
            rs = slice(c * CHUNK, (c + 1) * CHUNK)
            heads = lambda ref: jnp.stack([ref[rs, cs(h)] for h in range(hb)])
            ds, s = ds_ref[...], states[c]
            dsb, sb = _bf(ds), _bf(s)
            dob, vnb = _bf(heads(do_ref)), _bf(heads(vn_ref))
            dvn = _dot(qk_ref[:, rs, :], dob, BTN) + _dot(heads(kel_ref), dsb, BNN)
            dvnb = _bf(dvn)
            dw = -_dot(dvnb, sb, BNT)
            dqe = _dot(dob, sb, BNT)
            dkel = _dot(vnb, dsb, BNT)
            dqk_ref[:, rs, :] = _dot(dob, vnb, BNT)
            dgl = jnp.sum(jnp.sum(ds * s, axis=2, keepdims=True), axis=1, keepdims=True)
            ds_ref[...] = ds * decay(c) + _dot(heads(qe_ref), dob, BTN) - _dot(heads(wb_ref), dvnb, BTN)
            for h in range(hb):
                dvn_ref[rs, cs(h)] = dvn[h]
                dw_ref[rs, cs(h)] = dw[h]
                dqe_ref[rs, cs(h)] = dqe[h]
                dkel_ref[rs, cs(h)] = dkel[h]
                dgl_ref[c * 8:(c + 1) * 8, cs(h)] = jnp.broadcast_to(dgl[h], (8, HEAD_DIM))

    big = jax.ShapeDtypeStruct((t, inner), F32)
    return _grid_call(
        body, name=name, grid=(nh // hb, t // rows),
        in_specs=[wide, wide, wide, wide, wide, sq, glb_spec,
                  pl.BlockSpec((None, hb, HEAD_DIM, HEAD_DIM), lambda g, n: (last - n, g, 0, 0))],
        out_specs=[wide] * 4 + [sq, glb_spec],
        out_shape=[big] * 4 + [jax.ShapeDtypeStruct((nh, t, CHUNK), F32),
                               jax.ShapeDtypeStruct((t // CHUNK * 8, inner), F32)],
        scratch_shapes=[pltpu.VMEM((hb, HEAD_DIM, HEAD_DIM), F32)],
        args=(do, qe, kel, wb, vn, qk, glb, sall), semantics=("parallel", "arbitrary"), comm=comm)


def _gdn_intra_bwd(q, k, v, gates, tm, w, u, dvn, dw, dqe, dkel, dqk, dglb, nh, name, comm=None):
    t, inner = q.shape
    hb, nb, rows, wide, sq, gts, glb = _gdn_specs(nh, inner, t)

    def body(q_ref, k_ref, v_ref, g_ref, tm_ref, w_ref, u_ref, dvn_ref, dw_ref, dqe_ref, dkel_ref, dqk_ref,
             dgl_ref, dq_ref, dk_ref, dv_ref, dg_ref):
        first = pl.program_id(0) * hb
        selg, selb = _head_select(first, hb, 0), _head_select(first, hb, nh)
        i, j = _iota2(CHUNK, CHUNK)
        lane8 = lax.broadcasted_iota(jnp.int32, (CHUNK, 8), 1)
        row = lax.broadcasted_iota(jnp.int32, (CHUNK, 1), 0)
        lower = jnp.where(j <= i, 1.0, 0.0).astype(F32)
        rsum = lambda x: jnp.sum(x, axis=-1, keepdims=True)
        units, rs, cs, (qv, kv, vv, wv, uv, dvn, dw, dqe, dkel), gcol, grow, bcol = _unit_inputs(
            (q_ref, k_ref, v_ref, w_ref, u_ref, dvn_ref, dw_ref, dqe_ref, dkel_ref), g_ref, selg, selb, hb, nb)
        nu = len(units)
        tmv = jnp.stack([tm_ref[h, rs(c), :] for c, h in units])
        dqk = jnp.where(j <= i, jnp.stack([dqk_ref[h, rs(c), :] for c, h in units]), 0.0)
        dgl = jnp.stack([dgl_ref[c * 8:c * 8 + 1, h * HEAD_DIM:h * HEAD_DIM + 1] for c, h in units])
        e, el, gl, decay = _decay_terms(gcol, grow)
        kb = kv * bcol
        qb, kbf, kbb = _bf(qv), _bf(kv), _bf(kb)
        dqkr = _bf(dqk * decay)
        dq = dqe * e + _dot(dqkr, kbf, BNN)
        dk = dkel * el + _dot(dqkr, qb, BTN)
        de = rsum(dqe * qv)
        del_ = rsum(dkel * kv)
        mq = dqk * _dot(qb, kbf, BNT) * decay
        dsol = _dot3(tmv, jnp.concatenate([dvn, dw], axis=2), BTN)
        dvb, dkbe = dsol[:, :, :HEAD_DIM], dsol[:, :, HEAD_DIM:]
        da = -jnp.where(j < i, _dot3(dsol, jnp.concatenate([uv, wv], axis=2), BNT), 0.0)
        dkk = _bf(da * decay)
        ma = da * _dot(kbb, kbf, BNT) * decay
        dkb = dkbe * e + _dot(dkk, kbf, BNN)
        de = de + rsum(dkbe * kb)
        dk = dk + _dot(dkk, kbb, BTN) + dkb * bcol
        dv = dvb * bcol
        dbeta = rsum(dkb * kv) + rsum(dvb * vv)
        m = mq + ma
        ones = jnp.ones((nu, CHUNK, LANES), F32)
        dgc = rsum(m) - _dot_exact(m, ones, BTN, False)[:, :, 0:1] + de * e - del_ * el
        tail = jnp.sum(del_ * el, axis=1, keepdims=True) + dgl * gl
        dgc = dgc + jnp.where(row == CHUNK - 1, tail, 0.0)
        for n, (c, h) in enumerate(units):
            dq_ref[rs(c), cs(h)] = dq[n]
            dk_ref[rs(c), cs(h)] = dk[n]
            dv_ref[rs(c), cs(h)] = dv[n]
        for c in range(nb):
            dgc_cols = jnp.zeros((CHUNK, 8), F32)
            dbeta_cols = jnp.zeros((CHUNK, 8), F32)
            for h in range(hb):
                dgc_cols = jnp.where(lane8 == h, dgc[c * hb + h], dgc_cols)
                dbeta_cols = jnp.where(lane8 == h, dbeta[c * hb + h], dbeta_cols)
            dg_cols = _dot_exact(lower, dgc_cols, TN, True)
            dg_ref[rs(c), :] = _dot_exact(dg_cols, selg, NN, False) + _dot_exact(dbeta_cols, selb, NN, False)

    big = jax.ShapeDtypeStruct((t, inner), F32)
    return _grid_call(
        body, name=name, grid=(nh // hb, t // rows),
        in_specs=[wide, wide, wide, gts, sq, wide, wide, wide, wide, wide, wide, sq, glb],
        out_specs=[wide, wide, wide, pl.BlockSpec((None, rows, LANES), lambda g, n: (g, n, 0))],
        out_shape=[big, big, big, jax.ShapeDtypeStruct((nh // hb, t, LANES), F32)],
        args=(q, k, v, gates, tm, w, u, dvn, dw, dqe, dkel, dqk, dglb), semantics=("parallel", "parallel"),
        comm=comm)


def _layer_a_fwd(x, hn, w_in_t, w_ab_t, conv_w, alog_row, dt_row, onw, nh, comm, w_out_of):
    t, d = x.shape
    inner = nh * HEAD_DIM
    proj = _mm(hn, w_in_t, "nt", t, 4 * inner, d, out_dtype=F32, name="a_proj")
    ab = _mm(hn, w_ab_t, "nt", t, LANES, d, out_dtype=F32, name="a_proj_ab")
    gates = _gates_fwd(ab, alog_row, dt_row, nh, "a_gates")
    q = _conv_fwd(proj, conv_w, 0, inner, "q", "a_conv_q")
    k = _conv_fwd(proj, conv_w, inner, inner, "k", "a_conv_k")
    v = _conv_fwd(proj, conv_w, 2 * inner, inner, "v", "a_conv_v")
    qe, kel, wb, w, u, qk, tm, glb, *carried = _gdn_intra_fwd(q, k, v, gates, nh, "a_intra", comm)
    o, vn, sall = _gdn_scan_fwd(qe, kel, wb, u, qk, glb, nh, "a_scan")
    g = _gate_fwd(o, proj, 3 * inner, inner, "a_gate", norm_w=onw)
    w_out = w_out_of(carried)
    h1 = _mm(g, w_out, "nn", t, d, inner, out_dtype=F32, name="a_out", res=x)
    return h1, (hn, proj, ab, gates, q, k, v, qe, kel, wb, w, u, qk, tm, glb, o, vn, sall, g), w_out, carried


def _layer_a_bwd(dh1, dh1b, x, nw, w_in_t, w_ab_t, conv_w, alog_row, dt_row, onw, w_out, nh, saved, comms_of,
                 own_comm):
    hn, proj, ab, gates, q, k, v, qe, kel, wb, w, u, qk, tm, glb, o, vn, sall, g = saved
    t, d = x.shape
    inner = w_out.shape[0]
    dg = _mm(dh1b, w_out, "nt", t, inner, d, out_dtype=F32, name="a_dgate")
    dw_out = _mm(g, dh1b, "tn", inner, d, t, out_dtype=BF16, name="a_dwout")
    comm_scan, comm_intra = comms_of(dw_out)
    do, dproj, donw = _gate_bwd(dg, o, proj, 3 * inner, inner, "a_gate_bwd", BF16, norm_w=onw,
                                dest=(None, 3 * inner, 4 * inner))
    dvn, dw, dqe, dkel, dqk, dglb, *carried_scan = _gdn_scan_bwd(do, qe, kel, wb, vn, qk, glb, sall, nh,
                                                                 "a_scan_bwd", comm_scan)
    dq, dk, dv, dgates, *carried = _gdn_intra_bwd(q, k, v, gates, tm, w, u, dvn, dw, dqe, dkel, dqk, dglb, nh,
                                                  "a_intra_bwd", comm_intra)
    carried = carried_scan + carried
    dproj, dcq = _conv_bwd(dq, proj, conv_w, 0, inner, "q", "a_conv_q_bwd", dest=(dproj, 0, 4 * inner))
    dproj, dck = _conv_bwd(dk, proj, conv_w, inner, inner, "k", "a_conv_k_bwd", dest=(dproj, inner, 4 * inner))
    dproj, dcv = _conv_bwd(dv, proj, conv_w, 2 * inner, inner, "v", "a_conv_v_bwd",
                           dest=(dproj, 2 * inner, 4 * inner))
    dab, dsmall = _gates_bwd(ab, alog_row, dt_row, dgates, nh, "a_gates_bwd")
    dw_in_t = _mm(dproj, hn, "tn", 4 * inner, d, t, out_dtype=BF16, name="a_dwin", **LONG_K_TILES)
    dw_ab_t = _mm(dab, hn, "tn", LANES, d, t, out_dtype=BF16, name="a_dwin_ab")
    dconv = jnp.concatenate([dcq[:CONV_K], dck[:CONV_K], dcv[:CONV_K]], axis=1)
    dhn = _mm(dab, w_ab_t, "nn", t, d, LANES, out_dtype=F32, name="a_dhn_ab")
    own = own_comm(dw_in_t, dw_ab_t, dconv)
    dhn = _mm(dproj, w_in_t, "nn", t, d, 4 * inner, out_dtype=F32, name="a_dhn", res=dhn, comm=own,
              **LONG_K_TILES)
    dhn, carried_own = dhn if own is not None else (dhn, [])
    dx, _, dnw = _rms_bwd(x, nw, dhn, dh1, "a_rms_bwd")
    return dx, dnw, dsmall, donw, carried, carried_own


def _rows_of(a, rows):
    flat = a.reshape(-1)
    return jnp.pad(flat, (0, rows * LANES - flat.shape[0])).reshape(rows, LANES)


def _to_slabs(g, axis):
    shape = g.shape[:axis] + (N_DEV, g.shape[axis] // N_DEV) + g.shape[axis + 1:]
    return jnp.moveaxis(g.reshape(shape), axis, 0)


def _from_slabs(s, axis):
    m = jnp.moveaxis(s, 0, axis)
    return m.reshape(m.shape[:axis] + (m.shape[axis] * m.shape[axis + 1],) + m.shape[axis + 2:])


def kernel(x, norm_w, a_w_in, a_conv_w, a_a_log, a_dt_bias, a_out_norm_w, a_w_out, b_w_in, b_q_norm_w, b_k_norm_w, b_rel_bias, b_w_out, loss_target, m_norm_w, m_a_w_in, m_a_conv_w, m_a_a_log, m_a_dt_bias, m_a_out_norm_w, m_a_w_out, m_b_w_in, m_b_q_norm_w, m_b_k_norm_w, m_b_rel_bias, m_b_w_out, v_norm_w, v_a_w_in, v_a_conv_w, v_a_a_log, v_a_dt_bias, v_a_out_norm_w, v_a_w_out, v_b_w_in, v_b_q_norm_w, v_b_k_norm_w, v_b_rel_bias, v_b_w_out):
    xs, target = x[0], loss_target[0]
    nh = a_a_log.shape[-1]
    inner = N_DEV * a_w_out.shape[1]

    d = xs.shape[1]
    nw0, nw1 = norm_w[0:1], norm_w[1:2]
    hn0, (ga_in, g_conv) = _rms_fwd(
        xs, nw0, "a_rms", comm=_RoutedGather([a_w_in[0].T.astype(BF16), a_conv_w[0]]))
    wa_in_t = ga_in.reshape(-1, d)
    wa_ab_t = jnp.pad(wa_in_t[4 * inner:], ((0, LANES - 2 * nh), (0, 0)))
    conv_w = _from_slabs(g_conv, 1)
    alog_row = jnp.pad(a_a_log, ((0, 0), (0, LANES - nh)))
    dt_row = jnp.pad(a_dt_bias, ((0, 0), (0, LANES - nh)))

    h1, saved_a, wa_out, (gb_in, gb_out, _) = _layer_a_fwd(
        xs, hn0, wa_in_t, wa_ab_t, conv_w, alog_row, dt_row, a_out_norm_w, nh,
        _Comm("gather", [b_w_in[0].T.astype(BF16), b_w_out[0].astype(BF16), a_w_out[0].astype(BF16)]),
        lambda gathered: _from_slabs(gathered[2], 0))
    wb_in_t = gb_in.reshape(-1, d)
    wb_out = _from_slabs(gb_out, 0)
    bias = _bias_tiles(_pad_rel_bias(b_rel_bias[0]), "b_bias_tiles")
    (dh2, dh2b, loss_row), saved_b = _layer_b_fwd(h1, nw1, wb_in_t, b_q_norm_w, b_k_norm_w, bias, wb_out, target)

    dh1, dh1b, dnw1, dwb_in_t, dqw, dkw, drb, dwb_out = _layer_b_bwd(
        dh2, dh2b, h1, nw1, wb_in_t, b_q_norm_w, b_k_norm_w, bias, wb_out, saved_b)

    def exchange_early(dwa_out):
        return (_Comm("exchange", [_to_slabs(dwb_out, 0).astype(BF16), _to_slabs(dwa_out, 0).astype(BF16)]),
                _Comm("exchange", [dwb_in_t.reshape(N_DEV, -1, d).astype(BF16)]))

    def exchange_last(dwa_in_t, dwa_ab_t, dconv):
        full = jnp.concatenate([dwa_in_t, dwa_ab_t[:2 * nh]], axis=0)
        return _Comm("exchange", [full.reshape(N_DEV, -1, d).astype(BF16), _to_slabs(dconv, 1)])

    dx, dnw0, dsmall, donw, (pb_out, pa_out, pb_in), (pa_in, p_conv) = _layer_a_bwd(
        dh1, dh1b, xs, nw0, wa_in_t, wa_ab_t, conv_w, alog_row, dt_row, a_out_norm_w, wa_out, nh, saved_a,
        exchange_early, exchange_last)
    big = {}
    for name, p, w, m, v in (("a_w_in", pa_in, a_w_in, m_a_w_in, v_a_w_in),
                             ("a_w_out", pa_out, a_w_out, m_a_w_out, v_a_w_out),
                             ("b_w_in", pb_in, b_w_in, m_b_w_in, v_b_w_in),
                             ("b_w_out", pb_out, b_w_out, m_b_w_out, v_b_w_out),
                             ("a_conv_w", p_conv, a_conv_w, m_a_conv_w, v_a_conv_w)):
        big[name] = [o[None] for o in _adamw(p, w[0], m[0], v[0], "adamw_" + name,
                                             transposed=name in ("a_w_in", "b_w_in"))]

    small = (("norm_w", norm_w, m_norm_w, v_norm_w, jnp.concatenate([dnw0, dnw1], axis=0)),
             ("a_a_log", a_a_log, m_a_a_log, v_a_a_log, dsmall[0:1, :nh]),
             ("a_dt_bias", a_dt_bias, m_a_dt_bias, v_a_dt_bias, dsmall[1:2, :nh]),
             ("a_out_norm_w", a_out_norm_w, m_a_out_norm_w, v_a_out_norm_w, donw),
             ("b_q_norm_w", b_q_norm_w, m_b_q_norm_w, v_b_q_norm_w, dqw),
             ("b_k_norm_w", b_k_norm_w, m_b_k_norm_w, v_b_k_norm_w, dkw),
             ("b_rel_bias", b_rel_bias, m_b_rel_bias, v_b_rel_bias, drb))
    rows = [8 * (-(-w.size // (8 * LANES))) for _, w, _, _, _ in small]
    pack = lambda arrs: jnp.concatenate([_rows_of(a, r) for a, r in zip(arrs, rows)] + [jnp.zeros((8, LANES), F32)], axis=0)
    g_pack = jnp.concatenate([_rows_of(g, r) for (_, _, _, _, g), r in zip(small, rows)]
                             + [jnp.broadcast_to(loss_row, (8, LANES))], axis=0)
    (g_all,) = _comm_call(_Comm("gather", [g_pack]), "gather_small_grads")
    outs_small = _adamw(g_all, pack([s[1] for s in small]), pack([s[2] for s in small]),
                        pack([s[3] for s in small]), "adamw_small")
    start = 0
    for (name, w, _, _, _), r in zip(small, rows):
        big[name] = [o[start:start + r].reshape(-1)[:w.size].reshape(w.shape) for o in outs_small]
        start += r
    loss = outs_small[0][start, 0]

    order = ("norm_w", "a_w_in", "a_conv_w", "a_a_log", "a_dt_bias", "a_out_norm_w", "a_w_out", "b_w_in",
             "b_q_norm_w", "b_k_norm_w", "b_rel_bias", "b_w_out")
    return (loss, dx[None]) + tuple(big[n][i] for i in range(4) for n in order)
```
